```python
import jax, jax.numpy as jnp
from jax import lax
import numpy as np

D_MODEL = 2048
BATCH = 8
SEQ = 4096
DEPTH = 1

D_MIX = D_MODEL
D_ATTN = D_MIX // 2
D_POOL = D_MIX - D_ATTN
HEAD_DIM = 128
N_HEADS = D_ATTN // HEAD_DIM
ROPE_DIM = HEAD_DIM // 4
ROPE_THETA = 500000.0
DILATED_PATTERNS = ((128, 1), (512, 4), (2048, 16))
POOL_WINDOWS = (2, 4, 8, 16)
N_POOL_GROUPS = len(POOL_WINDOWS)
POOL_GROUP_DIM = D_POOL // N_POOL_GROUPS
D_IN = 3 * D_ATTN + D_POOL + D_MIX
LN_EPS = 1e-5
DEEPNORM_ALPHA = (2.0 * DEPTH) ** 0.25
DEEPNORM_BETA = (8.0 * DEPTH) ** -0.25

kernel_name = "hybrid_dilated_attn_pool_deepnorm"


def rotary_partial(t, positions):
    half = ROPE_DIM // 2
    inv_freq = ROPE_THETA ** (-(2.0 * jnp.arange(half, dtype=jnp.float32)) / ROPE_DIM)
    ang = positions.astype(jnp.float32)[:, None] * inv_freq[None, :]
    cos = jnp.cos(ang)[None, :, None, :]
    sin = jnp.sin(ang)[None, :, None, :]
    t32 = t.astype(jnp.float32)
    t1, t2, rest = t32[..., :half], t32[..., half:ROPE_DIM], t32[..., ROPE_DIM:]
    out = jnp.concatenate([t1 * cos - t2 * sin, t1 * sin + t2 * cos, rest], axis=-1)
    return out.astype(t.dtype)


def banded_causal_attention(q, k, v, n_keys):
    G, N, Dh = q.shape
    W = n_keys
    nb = -(-N // W)
    Np = nb * W
    pad = Np - N
    qp = jnp.pad(q, ((0, 0), (0, pad), (0, 0)))
    kp = jnp.pad(k, ((0, 0), (W, pad), (0, 0)))
    vp = jnp.pad(v, ((0, 0), (W, pad), (0, 0)))
    qb = qp.reshape(G, nb, W, Dh)
    kb = jnp.concatenate([kp[:, :Np].reshape(G, nb, W, Dh), kp[:, W:].reshape(G, nb, W, Dh)], axis=2)
    vb = jnp.concatenate([vp[:, :Np].reshape(G, nb, W, Dh), vp[:, W:].reshape(G, nb, W, Dh)], axis=2)
    s = jnp.einsum('gcqd,gckd->gcqk', qb, kb).astype(jnp.float32) * (Dh ** -0.5)
    qi = jnp.arange(W)[:, None]
    kj = jnp.arange(2 * W)[None, :]
    dist = W + qi - kj
    key_pos = (jnp.arange(nb)[:, None, None] - 1) * W + kj[None]
    mask = (dist >= 0)[None] & (dist <= W)[None] & (key_pos >= 0)
    s = jnp.where(mask[None], s, -jnp.inf)
    m = jnp.max(s, axis=-1, keepdims=True)
    p = jnp.exp(s - m)
    den = jnp.sum(p, axis=-1, keepdims=True)
    o = jnp.einsum('gcqk,gckd->gcqd', p, vb.astype(jnp.float32)) / den
    lse = (m + jnp.log(den))[..., 0]
    return o.reshape(G, Np, Dh)[:, :N], lse.reshape(G, Np)[:, :N]


def dilated_attention(q, k, v, window, dilation):
    B, H, S, Dh = q.shape
    n_sub = S // dilation

    def to_residues(t):
        return t.reshape(B, H, n_sub, dilation, Dh).transpose(0, 1, 3, 2, 4).reshape(B * H * dilation, n_sub, Dh)

    o, lse = banded_causal_attention(to_residues(q), to_residues(k), to_residues(v), window // dilation)
    o = o.reshape(B, H, dilation, n_sub, Dh).transpose(0, 1, 3, 2, 4).reshape(B, H, S, Dh)
    lse = lse.reshape(B, H, dilation, n_sub).transpose(0, 1, 3, 2).reshape(B, H, S)
    return o, lse


def dilated_attention_mixture(q, k, v):
    outs, lses = [], []
    for window, dilation in DILATED_PATTERNS:
        o, lse = dilated_attention(q, k, v, window, dilation)
        outs.append(o)
        lses.append(lse)
    w = jax.nn.softmax(jnp.stack(lses, axis=0), axis=0)
    return jnp.einsum('pbhs,pbhsd->bhsd', w, jnp.stack(outs, axis=0))


def causal_pool_minus_identity(u, window):
    S = u.shape[1]
    u32 = u.astype(jnp.float32)
    c = jnp.cumsum(u32, axis=1)
    c_shift = jnp.pad(c, ((0, 0), (window, 0), (0, 0)))[:, :S]
    count = jnp.minimum(jnp.arange(S) + 1, window).astype(jnp.float32)
    return (c - c_shift) / count[None, :, None] - u32


def layer_norm(h, gain, bias):
    h32 = h.astype(jnp.float32)
    mu = jnp.mean(h32, axis=-1, keepdims=True)
    var = jnp.mean(jnp.square(h32 - mu), axis=-1, keepdims=True)
    return (h32 - mu) * lax.rsqrt(var + LN_EPS) * gain.astype(jnp.float32) + bias.astype(jnp.float32)


def _fwd_setup_inputs(seed: int = 0) -> dict:
    key = jax.random.key(seed)
    ks = jax.random.split(key, 8)
    x = jax.random.normal(ks[0], (BATCH, SEQ, D_MODEL), jnp.float32)
    w_in = jax.random.normal(ks[1], (DEPTH, D_MODEL, D_IN), jnp.float32) * D_MODEL ** -0.5
    w_pool = jax.random.normal(ks[2], (DEPTH, N_POOL_GROUPS, POOL_GROUP_DIM, POOL_GROUP_DIM), jnp.float32) * POOL_GROUP_DIM ** -0.5
    pool_scale = 1.0 + 0.02 * jax.random.normal(ks[3], (DEPTH, D_POOL), jnp.float32)
    w_out = jax.random.normal(ks[4], (DEPTH, D_MIX, D_MODEL), jnp.float32) * (D_MIX ** -0.5) * DEEPNORM_BETA
    ln_gain = 1.0 + 0.02 * jax.random.normal(ks[5], (DEPTH, D_MODEL), jnp.float32)
    ln_bias = 0.02 * jax.random.normal(ks[6], (DEPTH, D_MODEL), jnp.float32)
    return {"x": x, "w_in": w_in, "w_pool": w_pool, "pool_scale": pool_scale,
            "w_out": w_out, "ln_gain": ln_gain, "ln_bias": ln_bias}


def _fwd_reference(x, w_in, w_pool, pool_scale, w_out, ln_gain, ln_bias):
    B, S, _ = x.shape
    positions = jnp.arange(S, dtype=jnp.int32)
    for layer in range(DEPTH):
        h = jnp.einsum('bsd,de->bse', x, w_in[layer])
        q, k, v, u_pool, gate = jnp.split(
            h, [D_ATTN, 2 * D_ATTN, 3 * D_ATTN, 3 * D_ATTN + D_POOL], axis=-1)

        q = rotary_partial(q.reshape(B, S, N_HEADS, HEAD_DIM), positions).transpose(0, 2, 1, 3)
        k = rotary_partial(k.reshape(B, S, N_HEADS, HEAD_DIM), positions).transpose(0, 2, 1, 3)
        v = v.reshape(B, S, N_HEADS, HEAD_DIM).transpose(0, 2, 1, 3)
        attn = dilated_attention_mixture(q, k, v)
        attn = attn.transpose(0, 2, 1, 3).reshape(B, S, D_ATTN)

        u_groups = u_pool.reshape(B, S, N_POOL_GROUPS, POOL_GROUP_DIM)
        pooled = jnp.stack([causal_pool_minus_identity(u_groups[:, :, g], POOL_WINDOWS[g])
                            for g in range(N_POOL_GROUPS)], axis=2)
        pool_out = jnp.einsum('bsgc,gcd->bsgd', pooled, w_pool[layer].astype(jnp.float32))
        pool_out = pool_out.reshape(B, S, D_POOL) * pool_scale[layer].astype(jnp.float32)

        y = jnp.concatenate([attn, pool_out], axis=-1) * jax.nn.silu(gate.astype(jnp.float32))
        out = jnp.einsum('bse,ed->bsd', y.astype(x.dtype), w_out[layer])

        x = layer_norm(DEEPNORM_ALPHA * x.astype(jnp.float32) + out.astype(jnp.float32),
                       ln_gain[layer], ln_bias[layer]).astype(x.dtype)
    return x


import jax as _jax
import jax.numpy as _jnp

TWIN_FORMAT = 'train_step'
FWD_PARAMS = ['x', 'w_in', 'w_pool', 'pool_scale', 'w_out', 'ln_gain', 'ln_bias']
TWIN_WEIGHTS = ['w_in', 'w_pool', 'pool_scale', 'w_out', 'ln_gain', 'ln_bias']
TWIN_DIFF_INPUT = 'x'
TWIN_INPUTS = ['x', 'w_in', 'w_pool', 'pool_scale', 'w_out', 'ln_gain', 'ln_bias', 'loss_target', 'm_w_in', 'm_w_pool', 'm_pool_scale', 'm_w_out', 'm_ln_gain', 'm_ln_bias', 'v_w_in', 'v_w_pool', 'v_pool_scale', 'v_w_out', 'v_ln_gain', 'v_ln_bias']
TWIN_OUTPUTS = ['loss', 'grad_x', 'grad_w_in', 'grad_w_pool', 'grad_pool_scale', 'grad_w_out', 'grad_ln_gain', 'grad_ln_bias', 'delta_w_in', 'delta_w_pool', 'delta_pool_scale', 'delta_w_out', 'delta_ln_gain', 'delta_ln_bias', 'new_m_w_in', 'new_m_w_pool', 'new_m_pool_scale', 'new_m_w_out', 'new_m_ln_gain', 'new_m_ln_bias', 'new_v_w_in', 'new_v_w_pool', 'new_v_pool_scale', 'new_v_w_out', 'new_v_ln_gain', 'new_v_ln_bias']
TWIN_LEAF_KINDS = {'loss': 'loss', 'grad_x': 'grad_x', 'grad_w_in': 'grad_w', 'grad_w_pool': 'grad_w', 'grad_pool_scale': 'grad_w', 'grad_w_out': 'grad_w', 'grad_ln_gain': 'grad_w', 'grad_ln_bias': 'grad_w', 'delta_w_in': 'delta_w', 'delta_w_pool': 'delta_w', 'delta_pool_scale': 'delta_w', 'delta_w_out': 'delta_w', 'delta_ln_gain': 'delta_w', 'delta_ln_bias': 'delta_w', 'new_m_w_in': 'new_m', 'new_m_w_pool': 'new_m', 'new_m_pool_scale': 'new_m', 'new_m_w_out': 'new_m', 'new_m_ln_gain': 'new_m', 'new_m_ln_bias': 'new_m', 'new_v_w_in': 'new_v', 'new_v_w_pool': 'new_v', 'new_v_pool_scale': 'new_v', 'new_v_w_out': 'new_v', 'new_v_ln_gain': 'new_v', 'new_v_ln_bias': 'new_v'}


def _forward(args):
    return _fwd_reference(*[args[k] for k in FWD_PARAMS])


def _output_shape():
    def fwd():
        inp = _fwd_setup_inputs(0)
        return _fwd_reference(*[inp[k] for k in FWD_PARAMS])
    out = _jax.eval_shape(fwd)
    return out.shape, out.dtype

N_MICROBATCH = 1
ADAM_LR = 0.001
ADAM_B1 = 0.9
ADAM_B2 = 0.999
ADAM_EPS = 1e-08
ADAM_WD = 0.01
ADAM_STEP = 10
PER_EXAMPLE_BATCH_AXIS = {'x': 0, 'loss_target': 0}
SHARED_INPUTS = []
_WEIGHT_DTYPES = {'w_in': _jnp.float32, 'w_pool': _jnp.float32, 'pool_scale': _jnp.float32, 'w_out': _jnp.float32, 'ln_gain': _jnp.float32, 'ln_bias': _jnp.float32}
MOMENT_SCALE = {'w_in': 1.401414e-02, 'w_pool': 2.273959e-02, 'pool_scale': 2.454956e-02, 'w_out': 2.778350e-02, 'ln_gain': 1.598752e+01, 'ln_bias': 2.850008e-01}


def _to_microbatches(a, axis):
    t = _jnp.moveaxis(a, axis, 0)
    t = t.reshape((N_MICROBATCH, t.shape[0] // N_MICROBATCH) + t.shape[1:])
    return _jnp.moveaxis(t, 1, axis + 1)


def setup_inputs(seed: int = 0) -> dict:
    inp = _fwd_setup_inputs(seed)
    key = _jax.random.fold_in(_jax.random.key(seed), 7919)
    shape, _ = _output_shape()
    out = dict(inp)
    out["loss_target"] = _jax.random.normal(_jax.random.fold_in(key, 0), shape, _jnp.float32)
    for i, name in enumerate(TWIN_WEIGHTS):
        w = inp[name].astype(_jnp.float32)
        if MOMENT_SCALE is None:
            s = _jnp.sqrt(_jnp.mean(_jnp.square(w)) + 1e-30)
        else:
            s = MOMENT_SCALE[name]
        km, kv = _jax.random.split(_jax.random.fold_in(key, i + 1))
        out[name] = w
        out["m_" + name] = s * _jax.random.normal(km, w.shape, _jnp.float32)
        out["v_" + name] = (s * s) * _jax.random.uniform(kv, w.shape, _jnp.float32, 0.5, 1.5)
    if N_MICROBATCH > 1:
        for name, axis in PER_EXAMPLE_BATCH_AXIS.items():
            out[name] = _to_microbatches(out[name], axis)
    return {'x': out['x'], 'w_in': out['w_in'], 'w_pool': out['w_pool'], 'pool_scale': out['pool_scale'], 'w_out': out['w_out'], 'ln_gain': out['ln_gain'], 'ln_bias': out['ln_bias'], 'loss_target': out['loss_target'], 'm_w_in': out['m_w_in'], 'm_w_pool': out['m_w_pool'], 'm_pool_scale': out['m_pool_scale'], 'm_w_out': out['m_w_out'], 'm_ln_gain': out['m_ln_gain'], 'm_ln_bias': out['m_ln_bias'], 'v_w_in': out['v_w_in'], 'v_w_pool': out['v_w_pool'], 'v_pool_scale': out['v_pool_scale'], 'v_w_out': out['v_w_out'], 'v_ln_gain': out['v_ln_gain'], 'v_ln_bias': out['v_ln_bias']}


def _loss(weights, diff, rest, loss_target):
    with _jax.named_scope("forward"):
        args = {**rest, TWIN_DIFF_INPUT: diff, **{k: w.astype(_WEIGHT_DTYPES[k]) for k, w in weights.items()}}
        y = _forward(args)
    with _jax.named_scope("loss_head"):
        err = _jnp.square(y.astype(_jnp.float32) - loss_target)
        return 0.5 * _jnp.sum(_jnp.mean(err, axis=-1)) if err.ndim else 0.5 * err


def _adamw(w, g, m, v):
    m = ADAM_B1 * m + (1.0 - ADAM_B1) * g
    v = ADAM_B2 * v + (1.0 - ADAM_B2) * _jnp.square(g)
    m_hat = m / (1.0 - ADAM_B1 ** ADAM_STEP)
    v_hat = v / (1.0 - ADAM_B2 ** ADAM_STEP)
    delta = -ADAM_LR * (m_hat / (_jnp.sqrt(v_hat) + ADAM_EPS) + ADAM_WD * w)
    return delta, m, v


def reference(x, w_in, w_pool, pool_scale, w_out, ln_gain, ln_bias, loss_target, m_w_in, m_w_pool, m_pool_scale, m_w_out, m_ln_gain, m_ln_bias, v_w_in, v_w_pool, v_pool_scale, v_w_out, v_ln_gain, v_ln_bias):
    given = dict(x=x, w_in=w_in, w_pool=w_pool, pool_scale=pool_scale, w_out=w_out, ln_gain=ln_gain, ln_bias=ln_bias, loss_target=loss_target, m_w_in=m_w_in, m_w_pool=m_w_pool, m_pool_scale=m_pool_scale, m_w_out=m_w_out, m_ln_gain=m_ln_gain, m_ln_bias=m_ln_bias, v_w_in=v_w_in, v_w_pool=v_w_pool, v_pool_scale=v_pool_scale, v_w_out=v_w_out, v_ln_gain=v_ln_gain, v_ln_bias=v_ln_bias)
    weights = {n: given[n] for n in TWIN_WEIGHTS}
    shared = {n: given[n] for n in SHARED_INPUTS}
    per_example = {n: given[n] for n in ['x']}
    grad_fn = _jax.value_and_grad(_loss, argnums=(0, 1))

    def one_microbatch(ex, loss_target):
        ex = dict(ex)
        diff = ex.pop(TWIN_DIFF_INPUT)
        return grad_fn(weights, diff, {**shared, **ex}, loss_target)

    if N_MICROBATCH == 1:
        loss, (grad_w, grad_x) = one_microbatch(per_example, given["loss_target"])
    else:
        def body(carry, xs):
            loss_sum, grad_sum = carry
            l_k, (gw_k, gx_k) = one_microbatch(xs[0], xs[1])
            with _jax.named_scope("update"):
                return (loss_sum + l_k, _jax.tree.map(_jnp.add, grad_sum, gw_k)), gx_k

        init = (_jnp.zeros((), _jnp.float32), _jax.tree.map(_jnp.zeros_like, weights))
        (loss, grad_w), grad_x = _jax.lax.scan(body, init, (per_example, given["loss_target"]))
    with _jax.named_scope("update"):
        delta_w, new_m, new_v = {}, {}, {}
        for n in TWIN_WEIGHTS:
            delta_w[n], new_m[n], new_v[n] = _adamw(weights[n], grad_w[n], given["m_" + n], given["v_" + n])
    return (loss, grad_x, *[grad_w[n] for n in TWIN_WEIGHTS], *[delta_w[n] for n in TWIN_WEIGHTS],
            *[new_m[n] for n in TWIN_WEIGHTS], *[new_v[n] for n in TWIN_WEIGHTS])
```

```python
import functools

import jax
import jax.numpy as jnp
from jax import lax
from jax.experimental import pallas as pl
from jax.experimental.pallas import tpu as pltpu

F32 = jnp.float32
BF16 = jnp.bfloat16
SDS = jax.ShapeDtypeStruct
MESH = pl.DeviceIdType.MESH

N_DEV = 8
S = 4096
D = 2048
N_HEADS = 8
DH = 128
DA = N_HEADS * DH
DP = 1024
N_GROUPS = 4
GC = DP // N_GROUPS
POOL_WINDOWS = (2, 4, 8, 16)
HALO = 16
D_IN = 3 * DA + DP + D
WB = D_IN // N_DEV
OB = D // N_DEV
PB = GC // N_DEV
ROPE_DIM = DH // 4
ROPE_HALF = ROPE_DIM // 2
ROPE_THETA = 500000.0
DILATIONS = (1, 4, 16)
KB = 128
LN_EPS = 1e-5
ALPHA = 2.0 ** 0.25
SCALE = DH ** -0.5
NEG = -1e30
ADAM_LR, ADAM_B1, ADAM_B2, ADAM_EPS, ADAM_WD, ADAM_STEP = 0.001, 0.9, 0.999, 1e-08, 0.01, 10

VMEM_LIMIT_V7X = 56 * 1024 * 1024

NT = (((1,), (1,)), ((), ()))
TN = (((0,), (0,)), ((), ()))


def _params(sem=None):
    return pltpu.CompilerParams(dimension_semantics=sem, vmem_limit_bytes=VMEM_LIMIT_V7X)


def _dot(a, b, dims=None):
    if dims is None:
        return jnp.dot(a, b, preferred_element_type=F32)
    return lax.dot_general(a, b, dims, preferred_element_type=F32)


def _rope_tables():
    inv_freq = ROPE_THETA ** (-(2.0 * jnp.arange(ROPE_HALF, dtype=F32)) / ROPE_DIM)
    ang = jnp.arange(S, dtype=jnp.int32).astype(F32)[:, None] * inv_freq[None, :]
    cos, sin = jnp.cos(ang), jnp.sin(ang)
    rest = DH - ROPE_DIM
    c = jnp.concatenate([cos, cos, jnp.ones((S, rest), F32)], axis=1)
    a = jnp.concatenate([-sin, jnp.zeros((S, DH - ROPE_HALF), F32)], axis=1)
    b = jnp.concatenate([jnp.zeros((S, ROPE_HALF), F32), sin, jnp.zeros((S, rest), F32)], axis=1)
    return c, a, b


def _rope(t, c, a, b):
    return t * c + pltpu.roll(t, DH - ROPE_HALF, 1) * a + pltpu.roll(t, ROPE_HALF, 1) * b


def _rope_t(g, c, a, b):
    return g * c - pltpu.roll(g, DH - ROPE_HALF, 1) * a - pltpu.roll(g, ROPE_HALF, 1) * b


def _prep_x(x):
    tm = 512

    def body(x_ref, xb_ref, xt_ref):
        xv = x_ref[...]
        xb_ref[...] = xv.astype(BF16)
        xt_ref[...] = xv.T.astype(BF16)

    return pl.pallas_call(
        body, name="prep_x", grid=(S // tm,),
        in_specs=[pl.BlockSpec((tm, D), lambda i: (i, 0))],
        out_specs=[pl.BlockSpec((tm, D), lambda i: (i, 0)), pl.BlockSpec((D, tm), lambda i: (0, i))],
        out_shape=[SDS((S, D), BF16), SDS((D, S), BF16)],
        compiler_params=_params(("arbitrary",)),
    )(x)


def _proj(xb, wg, col0, ncol, out_dtype, name, rope=None):
    tm, tn = 1024, 256
    cb0 = col0 // tn
    per_blk = WB // tn

    def body(x_ref, w_ref, *rest):
        o_ref = rest[-1]
        acc = _dot(x_ref[...], w_ref[...])
        if rope is None:
            o_ref[...] = acc.astype(out_dtype)
        else:
            c, a, b = rest[0][...], rest[1][...], rest[2][...]
            for hh in range(tn // DH):
                hs = slice(hh * DH, (hh + 1) * DH)
                o_ref[:, hs] = _rope(acc[:, hs], c, a, b).astype(out_dtype)

    in_specs = [pl.BlockSpec((tm, D), lambda i, j: (i, 0)),
                pl.BlockSpec((None, D, tn), lambda i, j: ((cb0 + j) // per_blk, 0, (cb0 + j) % per_blk))]
    args = [xb, wg]
    if rope is not None:
        in_specs += [pl.BlockSpec((tm, DH), lambda i, j: (i, 0))] * 3
        args += list(rope)
    return pl.pallas_call(
        body, name=name, grid=(S // tm, ncol // tn),
        in_specs=in_specs,
        out_specs=pl.BlockSpec((tm, tn), lambda i, j: (i, j)),
        out_shape=SDS((S, ncol), out_dtype),
        compiler_params=_params(("arbitrary", "arbitrary")),
    )(*args)


def _band_masks(shift_first=None, shift_last=None):
    ri = lax.broadcasted_iota(jnp.int32, (KB, KB), 0)
    cj = lax.broadcasted_iota(jnp.int32, (KB, KB), 1)
    diag = cj <= ri
    off = cj >= ri
    off_first = None if shift_first is None else cj >= ri + shift_first
    off_last = None if shift_last is None else cj >= ri + shift_last
    return diag, off, off_first, off_last


def _attn_fwd(qk, v, d, name):
    n_sub = S // d
    bq = 2 * KB
    n_i = n_sub // bq
    qk2 = qk.reshape(n_sub, d * 2 * DA)
    v2 = v.reshape(n_sub, d * DA)

    def prev(i):
        return jnp.maximum(2 * i - 1, 0)

    def body(q_ref, kc_ref, kp_ref, vc_ref, vp_ref, o_ref, l_ref):
        i = pl.program_id(1)
        diag, off, off_first, _ = _band_masks(shift_first=jnp.where(i == 0, 1 << 20, 0))
        for h in range(N_HEADS):
            hs = slice(h * DH, (h + 1) * DH)
            for a in range(2):
                rs = slice(a * KB, (a + 1) * KB)
                q = q_ref[rs, hs]
                kd, vd = kc_ref[rs, hs], vc_ref[rs, hs]
                if a == 0:
                    ko, vo, om = kp_ref[:, hs], vp_ref[:, hs], off_first
                else:
                    ko, vo, om = kc_ref[0:KB, hs], vc_ref[0:KB, hs], off
                sd = jnp.where(diag, _dot(q, kd, NT) * SCALE, NEG)
                so = jnp.where(om, _dot(q, ko, NT) * SCALE, NEG)
                m = jnp.maximum(jnp.max(sd, axis=1, keepdims=True), jnp.max(so, axis=1, keepdims=True))
                pd = jnp.exp(sd - m)
                po = jnp.exp(so - m)
                den = jnp.sum(pd, axis=1, keepdims=True) + jnp.sum(po, axis=1, keepdims=True)
                o = (_dot(pd.astype(BF16), vd) + _dot(po.astype(BF16), vo)) / den
                o_ref[rs, hs] = o
                l_ref[rs, hs] = jnp.broadcast_to(m + jnp.log(den), (KB, DH))

    o, lse = pl.pallas_call(
        body, name=name, grid=(d, n_i),
        in_specs=[pl.BlockSpec((bq, DA), lambda r, i: (i, 2 * r)),
                  pl.BlockSpec((bq, DA), lambda r, i: (i, 2 * r + 1)),
                  pl.BlockSpec((KB, DA), lambda r, i: (prev(i), 2 * r + 1)),
                  pl.BlockSpec((bq, DA), lambda r, i: (i, r)),
                  pl.BlockSpec((KB, DA), lambda r, i: (prev(i), r))],
        out_specs=[pl.BlockSpec((bq, DA), lambda r, i: (i, r)), pl.BlockSpec((bq, DA), lambda r, i: (i, r))],
        out_shape=[SDS((n_sub, d * DA), F32), SDS((n_sub, d * DA), F32)],
        compiler_params=_params(("arbitrary", "arbitrary")),
    )(qk2, qk2, qk2, v2, v2)
    return o.reshape(S, DA), lse.reshape(S, DA)


def _pooled(ext_ref, g, rows, tm):
    w = POOL_WINDOWS[g]
    cs = slice(g * GC, (g + 1) * GC)
    cur = ext_ref[HALO:HALO + tm, cs]
    win = cur
    for j in range(1, w):
        win = win + ext_ref[HALO - j:HALO - j + tm, cs]
    cnt = jnp.minimum(rows + 1, w).astype(F32)
    return win / cnt - cur, cnt


def _fill_ext(ext_ref, u_ref, uh_ref, blk, tm):
    @pl.when(blk == 0)
    def _():
        ext_ref[0:HALO, :] = jnp.zeros((HALO, DP), F32)

    @pl.when(blk > 0)
    def _():
        ext_ref[0:HALO, :] = uh_ref[...]

    ext_ref[HALO:HALO + tm, :] = u_ref[...]


def _mix(o1, l1, o2, l2, o3, l3, u, gate, wp, scale):
    tm = 256

    def body(o1r, l1r, o2r, l2r, o3r, l3r, u_ref, uh_ref, g_ref, wp_ref, sc_ref,
             y_ref, yt_ref, attn_ref, lse_ref, ext_ref, ys_ref):
        i = pl.program_id(0)
        la, lb, lc = l1r[...], l2r[...], l3r[...]
        mx = jnp.maximum(jnp.maximum(la, lb), lc)
        ea, eb, ec = jnp.exp(la - mx), jnp.exp(lb - mx), jnp.exp(lc - mx)
        z = ea + eb + ec
        attn = (ea * o1r[...] + eb * o2r[...] + ec * o3r[...]) / z
        attn_ref[...] = attn
        lse_ref[...] = mx + jnp.log(z)
        gt = g_ref[:, 0:DA]
        ys_ref[:, 0:DA] = attn * (gt * jax.nn.sigmoid(gt))

        _fill_ext(ext_ref, u_ref, uh_ref, i, tm)
        rows = i * tm + lax.broadcasted_iota(jnp.int32, (tm, 1), 0)
        for g in range(N_GROUPS):
            cs = slice(g * GC, (g + 1) * GC)
            gs = slice(DA + g * GC, DA + (g + 1) * GC)
            pooled, _ = _pooled(ext_ref, g, rows, tm)
            po = _dot(pooled.astype(BF16), wp_ref[g]) * sc_ref[:, cs]
            gt = g_ref[:, gs]
            ys_ref[:, gs] = po * (gt * jax.nn.sigmoid(gt))
        yv = ys_ref[...]
        y_ref[...] = yv.astype(BF16)
        yt_ref[...] = yv.T.astype(BF16)

    row = lambda i: (i, 0)
    blk = pl.BlockSpec((tm, DA), row)
    return pl.pallas_call(
        body, name="mix", grid=(S // tm,),
        in_specs=[blk] * 6 + [
            pl.BlockSpec((tm, DP), row),
            pl.BlockSpec((HALO, DP), lambda i: (jnp.maximum(i * (tm // HALO) - 1, 0), 0)),
            pl.BlockSpec((tm, D), row),
            pl.BlockSpec((N_GROUPS, GC, GC), lambda i: (0, 0, 0)),
            pl.BlockSpec((1, DP), lambda i: (0, 0))],
        out_specs=[pl.BlockSpec((tm, D), row), pl.BlockSpec((D, tm), lambda i: (0, i)), blk, blk],
        out_shape=[SDS((S, D), BF16), SDS((D, S), BF16), SDS((S, DA), F32), SDS((S, DA), F32)],
        scratch_shapes=[pltpu.VMEM((HALO + tm, DP), F32), pltpu.VMEM((tm, D), F32)],
        compiler_params=_params(("arbitrary",)),
    )(o1, l1, o2, l2, o3, l3, u, u, gate, wp, scale)


def _outproj_ln(y, wout, x, target, gain, bias):
    tm = 256

    def body(y_ref, w_ref, x_ref, t_ref, g_ref, b_ref, dz_ref, dzb_ref, st_ref):
        i = pl.program_id(0)

        @pl.when(i == 0)
        def _():
            st_ref[...] = jnp.zeros((8, D), F32)

        z = ALPHA * x_ref[...] + _dot(y_ref[...], w_ref[...])
        mu = jnp.mean(z, axis=1, keepdims=True)
        zc = z - mu
        var = jnp.mean(zc * zc, axis=1, keepdims=True)
        rstd = lax.rsqrt(var + LN_EPS)
        xhat = zc * rstd
        gn = g_ref[...]
        diff = xhat * gn + b_ref[...] - t_ref[...]
        dyln = diff / D
        st_ref[0:1, :] += jnp.sum(dyln * xhat, axis=0, keepdims=True)
        st_ref[1:2, :] += jnp.sum(dyln, axis=0, keepdims=True)
        row_loss = jnp.sum(diff * diff, axis=1, keepdims=True) / D
        st_ref[2:3, :] += jnp.broadcast_to(0.5 * jnp.sum(row_loss, axis=0, keepdims=True), (1, D))
        dxh = dyln * gn
        m1 = jnp.mean(dxh, axis=1, keepdims=True)
        m2 = jnp.mean(dxh * xhat, axis=1, keepdims=True)
        dz = rstd * (dxh - m1 - xhat * m2)
        dz_ref[...] = dz
        dzb_ref[...] = dz.astype(BF16)

    row = lambda i: (i, 0)
    const = lambda i: (0, 0)
    return pl.pallas_call(
        body, name="outproj_ln", grid=(S // tm,),
        in_specs=[pl.BlockSpec((tm, D), row),
                  pl.BlockSpec((D, D), const, pipeline_mode=pl.Buffered(1)),
                  pl.BlockSpec((tm, D), row), pl.BlockSpec((tm, D), row),
                  pl.BlockSpec((1, D), const), pl.BlockSpec((1, D), const)],
        out_specs=[pl.BlockSpec((tm, D), row), pl.BlockSpec((tm, D), row), pl.BlockSpec((8, D), const)],
        out_shape=[SDS((S, D), F32), SDS((S, D), BF16), SDS((8, D), F32)],
        compiler_params=_params(("arbitrary",)),
    )(y, wout, x, target, gain, bias)


def _bwd_mid(dzb, wout, gate, attn, u, wp, scale):
    tm = 256
    n = S // tm

    def body(dz_ref, w_ref, g_ref, at_ref, u_ref, uh_ref, wp_ref, sc_ref,
             dh_ref, do_ref, dd_ref, dwp_ref, gps_ref, ext_ref, eext_ref, acc_ref):
        i = pl.program_id(0)
        ib = n - 1 - i

        @pl.when(i == 0)
        def _():
            eext_ref[tm:tm + HALO, :] = jnp.zeros((HALO, DP), F32)
            acc_ref[...] = jnp.zeros((N_GROUPS, GC, GC), F32)
            gps_ref[...] = jnp.zeros((8, DP), F32)

        dy = _dot(dz_ref[...], w_ref[...], NT)
        gt = g_ref[...]
        sg = jax.nn.sigmoid(gt)
        silu = gt * sg
        dsilu = sg * (1.0 + gt * (1.0 - sg))
        dmix = dy * silu
        dyg = dy * dsilu

        at = at_ref[...]
        dat = dmix[:, 0:DA]
        do_ref[...] = dat.astype(BF16)
        dh_ref[:, DP:DP + DA] = (dyg[:, 0:DA] * at).astype(BF16)
        prod = dat * at
        for h in range(N_HEADS):
            hs = slice(h * DH, (h + 1) * DH)
            dd_ref[:, hs] = jnp.broadcast_to(jnp.sum(prod[:, hs], axis=1, keepdims=True), (tm, DH))

        _fill_ext(ext_ref, u_ref, uh_ref, ib, tm)
        rows = ib * tm + lax.broadcasted_iota(jnp.int32, (tm, 1), 0)
        for g in range(N_GROUPS):
            w = POOL_WINDOWS[g]
            cs = slice(g * GC, (g + 1) * GC)
            gs = slice(DA + g * GC, DA + (g + 1) * GC)
            pooled, cnt = _pooled(ext_ref, g, rows, tm)
            pre = _dot(pooled.astype(BF16), wp_ref[g])
            sc = sc_ref[:, cs]
            dpo = dmix[:, gs]
            gps_ref[0:1, cs] += jnp.sum(dpo * pre, axis=0, keepdims=True)
            dh_ref[:, DP + DA + g * GC:DP + DA + (g + 1) * GC] = (dyg[:, gs] * (pre * sc)).astype(BF16)
            dpre = (dpo * sc).astype(BF16)
            acc_ref[g] += _dot(pooled.T.astype(BF16), dpre)
            dpooled = _dot(dpre, wp_ref[g], NT)
            eext_ref[0:tm, cs] = dpooled / cnt
            du = eext_ref[0:tm, cs]
            for j in range(1, w):
                du = du + eext_ref[j:j + tm, cs]
            dh_ref[:, cs] = (du - dpooled).astype(BF16)
        eext_ref[tm:tm + HALO, :] = eext_ref[0:HALO, :]

        @pl.when(i == n - 1)
        def _():
            for j in range(N_DEV):
                for g in range(N_GROUPS):
                    dwp_ref[j, g * PB:(g + 1) * PB, :] = acc_ref[g, j * PB:(j + 1) * PB, :]

    rev = lambda i: (n - 1 - i, 0)
    const = lambda i: (0, 0)
    return pl.pallas_call(
        body, name="bwd_mid", grid=(n,),
        in_specs=[pl.BlockSpec((tm, D), rev),
                  pl.BlockSpec((D, D), const, pipeline_mode=pl.Buffered(1)),
                  pl.BlockSpec((tm, D), rev), pl.BlockSpec((tm, DA), rev), pl.BlockSpec((tm, DP), rev),
                  pl.BlockSpec((HALO, DP), lambda i: (jnp.maximum((n - 1 - i) * (tm // HALO) - 1, 0), 0)),
                  pl.BlockSpec((N_GROUPS, GC, GC), lambda i: (0, 0, 0)),
                  pl.BlockSpec((1, DP), const)],
        out_specs=[pl.BlockSpec((tm, DP + D), rev), pl.BlockSpec((tm, DA), rev), pl.BlockSpec((tm, DA), rev),
                   pl.BlockSpec((N_DEV, N_GROUPS * PB, GC), lambda i: (0, 0, 0)),
                   pl.BlockSpec((8, DP), const)],
        out_shape=[SDS((S, DP + D), BF16), SDS((S, DA), BF16), SDS((S, DA), F32),
                   SDS((N_DEV, N_GROUPS * PB, GC), F32), SDS((8, DP), F32)],
        scratch_shapes=[pltpu.VMEM((HALO + tm, DP), F32), pltpu.VMEM((tm + HALO, DP), F32),
                        pltpu.VMEM((N_GROUPS, GC, GC), F32)],
        compiler_params=_params(("arbitrary",)),
    )(dzb, wout, gate, attn, u, u, wp, scale)


def _attn_bwd(qk, v, do, lse, dd, rope, d, acc_in, out_dtype, name):
    n_sub = S // d
    bq = 2 * KB
    n_i = n_sub // bq
    n_kb = n_sub // KB
    qk2 = qk.reshape(n_sub, d * 2 * DA)
    v2 = v.reshape(n_sub, d * DA)
    do2, lse2, dd2 = (t.reshape(n_sub, d * DA) for t in (do, lse, dd))
    tabs = [t.reshape(n_sub, d * DH) for t in rope]
    has_acc = acc_in is not None

    def nxt(i):
        return jnp.minimum(2 * i + 2, n_kb - 1)

    def body(qc_ref, qn_ref, kc_ref, vc_ref, doc_ref, don_ref, lc_ref, ln_ref, dc_ref, dn_ref,
             c_ref, a_ref, b_ref, *rest):
        if has_acc:
            acc_ref, out_ref, carry_ref = rest
        else:
            out_ref, carry_ref = rest
        i = pl.program_id(1)

        @pl.when(i == 0)
        def _():
            carry_ref[...] = jnp.zeros((KB, DA), F32)

        diag, off, _, off_last = _band_masks(shift_last=jnp.where(i == n_i - 1, 1 << 20, 0))
        lo, hi = slice(0, KB), slice(KB, 2 * KB)

        def pair(q, k, vv, dob, lq, dq_, mask):
            s = _dot(q, k, NT) * SCALE
            p = jnp.exp(jnp.where(mask, s - lq, NEG))
            dp = _dot(dob, vv, NT)
            ds = (p * (dp - dq_) * SCALE).astype(BF16)
            return _dot(ds, k), _dot(ds, q, TN), _dot(p.astype(BF16), dob, TN)

        for h in range(N_HEADS):
            hs = slice(h * DH, (h + 1) * DH)
            q0, q1, q2 = qc_ref[lo, hs], qc_ref[hi, hs], qn_ref[:, hs]
            k0, k1 = kc_ref[lo, hs], kc_ref[hi, hs]
            v0, v1 = vc_ref[lo, hs], vc_ref[hi, hs]
            d0, d1, d2 = doc_ref[lo, hs], doc_ref[hi, hs], don_ref[:, hs]
            l0, l1, l2 = lc_ref[lo, hs], lc_ref[hi, hs], ln_ref[:, hs]
            e0, e1, e2 = dc_ref[lo, hs], dc_ref[hi, hs], dn_ref[:, hs]
            dq00, dk00, dv00 = pair(q0, k0, v0, d0, l0, e0, diag)
            dq10, dk10, dv10 = pair(q1, k0, v0, d1, l1, e1, off)
            dq11, dk11, dv11 = pair(q1, k1, v1, d1, l1, e1, diag)
            dq21, dk21, dv21 = pair(q2, k1, v1, d2, l2, e2, off_last)
            dq_lo = carry_ref[:, hs] + dq00
            dq_hi = dq10 + dq11
            carry_ref[:, hs] = dq21
            parts = ((0, dq_lo, dq_hi, True), (DA, dk00 + dk10, dk11 + dk21, True), (2 * DA, dv00 + dv10, dv11 + dv21, False))
            for base, g_lo, g_hi, rot in parts:
                cs = slice(base + h * DH, base + (h + 1) * DH)
                for rs, gv in ((lo, g_lo), (hi, g_hi)):
                    if rot:
                        gv = _rope_t(gv, c_ref[rs, :], a_ref[rs, :], b_ref[rs, :])
                    if has_acc:
                        gv = gv + acc_ref[rs, cs]
                    out_ref[rs, cs] = gv.astype(out_dtype)

    cur = lambda col: (lambda r, i: (i, col(r)))
    nx = lambda col: (lambda r, i: (nxt(i), col(r)))
    same = lambda r: r
    in_specs = [pl.BlockSpec((bq, DA), cur(lambda r: 2 * r)), pl.BlockSpec((KB, DA), nx(lambda r: 2 * r)),
                pl.BlockSpec((bq, DA), cur(lambda r: 2 * r + 1)), pl.BlockSpec((bq, DA), cur(same)),
                pl.BlockSpec((bq, DA), cur(same)), pl.BlockSpec((KB, DA), nx(same)),
                pl.BlockSpec((bq, DA), cur(same)), pl.BlockSpec((KB, DA), nx(same)),
                pl.BlockSpec((bq, DA), cur(same)), pl.BlockSpec((KB, DA), nx(same))]
    in_specs += [pl.BlockSpec((bq, DH), cur(same))] * 3
    args = [qk2, qk2, qk2, v2, do2, do2, lse2, lse2, dd2, dd2] + tabs
    if has_acc:
        in_specs.append(pl.BlockSpec((bq, 3 * DA), cur(same)))
        args.append(acc_in.reshape(n_sub, d * 3 * DA))
    out = pl.pallas_call(
        body, name=name, grid=(d, n_i),
        in_specs=in_specs,
        out_specs=pl.BlockSpec((bq, 3 * DA), cur(same)),
        out_shape=SDS((n_sub, d * 3 * DA), out_dtype),
        scratch_shapes=[pltpu.VMEM((KB, DA), F32)],
        compiler_params=_params(("arbitrary", "arbitrary")),
    )(*args)
    return out.reshape(S, 3 * DA)


def _grad_w_out(yt, dzb):
    tk = 512

    def body(yt_ref, dz_ref, o_ref, acc_ref):
        k = pl.program_id(0)

        @pl.when(k == 0)
        def _():
            acc_ref[...] = jnp.zeros((D, D), F32)

        acc_ref[...] += _dot(yt_ref[...], dz_ref[...])

        @pl.when(k == S // tk - 1)
        def _():
            pltpu.sync_copy(acc_ref, o_ref)

    return pl.pallas_call(
        body, name="grad_w_out", grid=(S // tk,),
        in_specs=[pl.BlockSpec((D, tk), lambda k: (0, k)), pl.BlockSpec((tk, D), lambda k: (k, 0))],
        out_specs=pl.BlockSpec(memory_space=pl.ANY),
        out_shape=SDS((D, D), F32),
        scratch_shapes=[pltpu.VMEM((D, D), F32)],
        compiler_params=_params(("arbitrary",)),
    )(yt, dzb).reshape(N_DEV, OB, D)


def _dh_specs(rows, which_block, row_of):
    half = N_DEV // 2

    def spec(lo):
        def index(*g):
            j = which_block(*g)
            mine = (j >= lo) & (j < lo + half)
            return (jnp.where(mine, row_of(*g), 0), jnp.where(mine, j - lo, 0))
        return pl.BlockSpec((rows, WB), index)

    return [spec(0), spec(half)]


def _grad_w_in(xt, dh_a, dh_b):
    tk = 512
    n_k = S // tk
    half = N_DEV // 2

    def body(xt_ref, da_ref, db_ref, o_ref, o16_ref, acc_ref):
        j, k = pl.program_id(0), pl.program_id(1)

        @pl.when(k == 0)
        def _():
            acc_ref[...] = jnp.zeros((D, WB), F32)

        @pl.when(j < half)
        def _():
            acc_ref[...] += _dot(xt_ref[...], da_ref[...])

        @pl.when(j >= half)
        def _():
            acc_ref[...] += _dot(xt_ref[...], db_ref[...])

        @pl.when(k == n_k - 1)
        def _():
            o_ref[...] = acc_ref[...]
            o16_ref[...] = acc_ref[...].astype(BF16)

    blk = lambda j, k: (j, 0, 0)
    return pl.pallas_call(
        body, name="grad_w_in", grid=(N_DEV, n_k),
        in_specs=[pl.BlockSpec((D, tk), lambda j, k: (0, k))] + _dh_specs(tk, lambda j, k: j, lambda j, k: k),
        out_specs=[pl.BlockSpec((None, D, WB), blk), pl.BlockSpec((None, D, WB), blk)],
        out_shape=[SDS((N_DEV, D, WB), F32), SDS((N_DEV, D, WB), BF16)],
        scratch_shapes=[pltpu.VMEM((D, WB), F32)],
        compiler_params=_params(("arbitrary", "arbitrary")),
    )(xt, dh_a, dh_b)


def _grad_x(dz, dh_a, dh_b, wg):
    tm = 1024
    half = N_DEV // 2

    def body(dz_ref, da_ref, db_ref, w_ref, o_ref):
        j = pl.program_id(1)

        @pl.when(j == 0)
        def _():
            o_ref[...] = ALPHA * dz_ref[...]

        @pl.when(j < half)
        def _():
            o_ref[...] += _dot(da_ref[...], w_ref[...], NT)

        @pl.when(j >= half)
        def _():
            o_ref[...] += _dot(db_ref[...], w_ref[...], NT)

    return pl.pallas_call(
        body, name="grad_x", grid=(S // tm, N_DEV),
        in_specs=[pl.BlockSpec((tm, D), lambda i, j: (i, 0))]
        + _dh_specs(tm, lambda i, j: j, lambda i, j: i)
        + [pl.BlockSpec((None, D, WB), lambda i, j: (j, 0, 0))],
        out_specs=pl.BlockSpec((tm, D), lambda i, j: (i, 0)),
        out_shape=SDS((S, D), F32),
        compiler_params=_params(("arbitrary", "arbitrary")),
    )(dz, dh_a, dh_b, wg)


def _place():
    x, y, c = lax.axis_index("x"), lax.axis_index("y"), lax.axis_index("c")
    chips = [(x, y), (1 - x, y), (x, 1 - y), (1 - x, 1 - y)]
    return x, y, c, chips


def _blk(x, y, c):
    return 4 * x + 2 * y + c


def _adamw(w, g, m, v):
    m = ADAM_B1 * m + (1.0 - ADAM_B1) * g
    v = ADAM_B2 * v + (1.0 - ADAM_B2) * (g * g)
    m_hat = m / (1.0 - ADAM_B1 ** ADAM_STEP)
    v_hat = v / (1.0 - ADAM_B2 ** ADAM_STEP)
    delta = -ADAM_LR * (m_hat / (jnp.sqrt(v_hat) + ADAM_EPS) + ADAM_WD * w)
    return delta, m, v


def _gather_weights(w_in_s, w_out_s, w_pool_s):
    def body(win_ref, wout_ref, wpool_ref, gin_ref, gout_ref, gpool_ref, bin_ref, bout_ref, bpool_ref,
             send_sems, recv_sems, local_sems):
        x, y, c, chips = _place()
        me, sib = (x, y, c), (x, y, 1 - c)
        bin_ref[...] = win_ref[...].astype(BF16)
        bout_ref[...] = wout_ref[...].astype(BF16)
        bpool_ref[...] = wpool_ref[...].astype(BF16)

        def pool_rows(b):
            return gpool_ref.at[:, pl.ds(pl.multiple_of(b * PB, PB), PB), :]

        tensors = [(bin_ref, lambda b: gin_ref.at[b]), (bout_ref, lambda b: gout_ref.at[b]), (bpool_ref, pool_rows)]

        def copy(t, k, block, to, src=None):
            dst = tensors[t][1](_blk(*block))
            return pltpu.make_async_remote_copy(
                src_ref=dst if src is None else src, dst_ref=dst,
                send_sem=send_sems.at[t, k], recv_sem=recv_sems.at[t, k], device_id=to, device_id_type=MESH)

        started = []
        own = []
        for t, (src, dst) in enumerate(tensors):
            cp = pltpu.make_async_copy(src, dst(_blk(*me)), local_sems.at[t])
            cp.start()
            own.append(cp)
            first = [copy(t, 0, me, sib, src=src)]
            first += [copy(t, j, me, (*chips[j], c), src=src) for j in (1, 2, 3)]
            for cp in first:
                cp.start()
            started += first
        for t in range(len(tensors)):
            for j in (1, 2, 3):
                copy(t, j, (*chips[j], c), me).wait_recv()
                fwd = copy(t, 3 + j, (*chips[j], c), sib)
                fwd.start()
                started.append(fwd)
        for t in range(len(tensors)):
            copy(t, 0, sib, me).wait_recv()
            for j in (1, 2, 3):
                copy(t, 3 + j, (*chips[j], 1 - c), me).wait_recv()
        for cp in started:
            cp.wait_send()
        for cp in own:
            cp.wait()

    vmem = pl.BlockSpec(memory_space=pltpu.VMEM)
    hbm = pl.BlockSpec(memory_space=pl.ANY)
    gin, gout, gpool = pl.pallas_call(
        body, name="gather_weights",
        in_specs=[vmem, vmem, vmem], out_specs=[hbm, hbm, hbm],
        out_shape=[SDS((N_DEV, D, WB), BF16), SDS((N_DEV, OB, D), BF16), SDS((N_GROUPS, GC, GC), BF16)],
        scratch_shapes=[pltpu.VMEM((D, WB), BF16), pltpu.VMEM((OB, D), BF16), pltpu.VMEM((N_GROUPS, PB, GC), BF16),
                        pltpu.SemaphoreType.DMA((3, 7)), pltpu.SemaphoreType.DMA((3, 7)), pltpu.SemaphoreType.DMA((3,))],
        compiler_params=_params(),
    )(w_in_s, w_out_s, w_pool_s)
    return gin, gout.reshape(D, D), gpool


def _reduce_scatter_adamw(name, p32, p16, w, m, v, rc):
    _, r_tot, cols = p32.shape
    n_ch = r_tot // rc

    def body(p32_ref, p16_ref, w_ref, m_ref, v_ref, g_ref, d_ref, nm_ref, nv_ref, r1_ref, r2_ref,
             s2_ref, a_ref, b_ref, wb_ref, mb_ref, vb_ref, og_ref, od_ref, om_ref, ov_ref,
             s1_send, s1_recv, s2_send, s2_recv):
        x, y, c, chips = _place()
        sib = (x, y, 1 - c)

        def stage1(k):
            return pltpu.make_async_remote_copy(
                src_ref=p16_ref.at[_blk(*chips[k], 1 - c)], dst_ref=r1_ref.at[k],
                send_sem=s1_send.at[k], recv_sem=s1_recv.at[k], device_id=sib, device_id_type=MESH)

        def stage2(k):
            return pltpu.make_async_remote_copy(
                src_ref=s2_ref.at[k - 1], dst_ref=r2_ref.at[k - 1],
                send_sem=s2_send.at[k - 1], recv_sem=s2_recv.at[k - 1], device_id=(*chips[k], c), device_id_type=MESH)

        for k in range(4):
            stage1(k).start()
        for k in (1, 2, 3):
            stage1(k).wait_recv()
            for ch in range(n_ch):
                rows = pl.ds(ch * rc, rc)
                pltpu.sync_copy(p32_ref.at[_blk(*chips[k], c), rows], a_ref)
                pltpu.sync_copy(r1_ref.at[k, rows], b_ref.at[0])
                s2_ref[k - 1, rows, :] = (a_ref[...] + b_ref[0].astype(F32)).astype(BF16)
            stage2(k).start()
        stage1(0).wait_recv()
        for k in (1, 2, 3):
            stage2(k).wait_recv()
        for ch in range(n_ch):
            rows = pl.ds(ch * rc, rc)
            pltpu.sync_copy(p32_ref.at[_blk(x, y, c), rows], a_ref)
            pltpu.sync_copy(r1_ref.at[0, rows], b_ref.at[0])
            for k in (1, 2, 3):
                pltpu.sync_copy(r2_ref.at[k - 1, rows], b_ref.at[k])
            pltpu.sync_copy(w_ref.at[rows], wb_ref)
            pltpu.sync_copy(m_ref.at[rows], mb_ref)
            pltpu.sync_copy(v_ref.at[rows], vb_ref)
            g = a_ref[...] + b_ref[0].astype(F32)
            for k in (1, 2, 3):
                g = g + b_ref[k].astype(F32)
            delta, nm, nv = _adamw(wb_ref[...], g, mb_ref[...], vb_ref[...])
            og_ref[...] = g
            od_ref[...] = delta
            om_ref[...] = nm
            ov_ref[...] = nv
            pltpu.sync_copy(og_ref, g_ref.at[rows])
            pltpu.sync_copy(od_ref, d_ref.at[rows])
            pltpu.sync_copy(om_ref, nm_ref.at[rows])
            pltpu.sync_copy(ov_ref, nv_ref.at[rows])
        for k in range(4):
            stage1(k).wait_send()
        for k in (1, 2, 3):
            stage2(k).wait_send()

    hbm = pl.BlockSpec(memory_space=pl.ANY)
    shard = SDS((r_tot, cols), F32)
    chunk = pltpu.VMEM((rc, cols), F32)
    outs = pl.pallas_call(
        body, name=name,
        in_specs=[hbm] * 5, out_specs=[hbm] * 6,
        out_shape=[shard, shard, shard, shard, SDS((4, r_tot, cols), BF16), SDS((3, r_tot, cols), BF16)],
        scratch_shapes=[pltpu.VMEM((3, r_tot, cols), BF16), chunk, pltpu.VMEM((4, rc, cols), BF16)] + [chunk] * 7
        + [pltpu.SemaphoreType.DMA((4,)), pltpu.SemaphoreType.DMA((4,)),
           pltpu.SemaphoreType.DMA((3,)), pltpu.SemaphoreType.DMA((3,))],
        compiler_params=_params(),
    )(p32, p16, w, m, v)
    return outs[:4]


def _replicated_adamw(stats, gps, gain, bias, scale, m_gain, m_bias, m_scale, v_gain, v_bias, v_scale):
    def body(st_ref, gps_ref, g_ref, b_ref, s_ref, mg_ref, mb_ref, ms_ref, vg_ref, vb_ref, vs_ref,
             all_ref, tot_ref, dl_ref, nm_ref, nv_ref, loc_ref, send_sems, recv_sems):
        x, y, c, _ = _place()
        mine = _blk(x, y, c)
        loc_ref[...] = st_ref[...]
        loc_ref[3:4, 0:DP] = gps_ref[0:1, :]
        copies = []
        for k in range(1, N_DEV):
            fx, fy, fc = (k >> 2) & 1, (k >> 1) & 1, k & 1
            peer = (x ^ fx, y ^ fy, c ^ fc)
            cp = pltpu.make_async_remote_copy(
                src_ref=loc_ref, dst_ref=all_ref.at[mine], send_sem=send_sems.at[k - 1], recv_sem=recv_sems.at[k - 1],
                device_id=peer, device_id_type=MESH)
            cp.start()
            copies.append(cp)
        all_ref[mine] = loc_ref[...]
        for cp in copies:
            cp.wait()
        tot = all_ref[0]
        for b in range(1, N_DEV):
            tot = tot + all_ref[b]
        tot_ref[...] = tot
        dl_ref[...] = jnp.zeros((8, D), F32)
        nm_ref[...] = jnp.zeros((8, D), F32)
        nv_ref[...] = jnp.zeros((8, D), F32)
        for row, width, w_r, m_r, v_r in ((0, D, g_ref, mg_ref, vg_ref), (1, D, b_ref, mb_ref, vb_ref),
                                          (3, DP, s_ref, ms_ref, vs_ref)):
            delta, nm, nv = _adamw(w_r[...], tot[row:row + 1, 0:width], m_r[...], v_r[...])
            dl_ref[row:row + 1, 0:width] = delta
            nm_ref[row:row + 1, 0:width] = nm
            nv_ref[row:row + 1, 0:width] = nv

    vmem = pl.BlockSpec(memory_space=pltpu.VMEM)
    rows = SDS((8, D), F32)
    outs = pl.pallas_call(
        body, name="replicated_adamw",
        in_specs=[vmem] * 11, out_specs=[vmem] * 5,
        out_shape=[SDS((N_DEV, 8, D), F32), rows, rows, rows, rows],
        scratch_shapes=[pltpu.VMEM((8, D), F32),
                        pltpu.SemaphoreType.DMA((N_DEV - 1,)), pltpu.SemaphoreType.DMA((N_DEV - 1,))],
        compiler_params=_params(),
    )(stats, gps, gain, bias, scale, m_gain, m_bias, m_scale, v_gain, v_bias, v_scale)
    return outs[1:]


def kernel(x, w_in, w_pool, pool_scale, w_out, ln_gain, ln_bias, loss_target, m_w_in, m_w_pool, m_pool_scale, m_w_out, m_ln_gain, m_ln_bias, v_w_in, v_w_pool, v_pool_scale, v_w_out, v_ln_gain, v_ln_bias):
    pool_rows = (N_GROUPS * PB, GC)
    x2, target = x[0], loss_target[0]
    rope = _rope_tables()

    wg_in, wg_out, wg_pool = _gather_weights(w_in[0], w_out[0], w_pool[0])

    xb, xt = _prep_x(x2)
    qk = _proj(xb, wg_in, 0, 2 * DA, BF16, "proj_qk", rope=rope)
    v = _proj(xb, wg_in, 2 * DA, DA, BF16, "proj_v")
    u = _proj(xb, wg_in, 3 * DA, DP, F32, "proj_u")
    gate = _proj(xb, wg_in, 3 * DA + DP, D, F32, "proj_gate")
    fwd = [_attn_fwd(qk, v, d, f"attn_fwd_d{d}") for d in DILATIONS]
    y, yt, attn, lse = _mix(*fwd[0], *fwd[1], *fwd[2], u, gate, wg_pool, pool_scale)
    dz, dzb, stats = _outproj_ln(y, wg_out, x2, target, ln_gain, ln_bias)

    dh_ug, do, dd, dwp, gps = _bwd_mid(dzb, wg_out, gate, attn, u, wg_pool, pool_scale)
    acc = None
    for n, d in enumerate(DILATIONS):
        last = n == len(DILATIONS) - 1
        acc = _attn_bwd(qk, v, do, lse, dd, rope, d, acc, BF16 if last else F32, f"attn_bwd_d{d}")
    dh_qkv = acc
    dwout = _grad_w_out(yt, dzb)
    dwin, dwin16 = _grad_w_in(xt, dh_qkv, dh_ug)
    grad_x = _grad_x(dz, dh_qkv, dh_ug, wg_in)

    g_in, d_in, nm_in, nv_in = _reduce_scatter_adamw(
        "rs_w_in", dwin, dwin16, w_in[0], m_w_in[0], v_w_in[0], 512)
    g_out, d_out, nm_out, nv_out = _reduce_scatter_adamw(
        "rs_w_out", dwout, dwout.astype(BF16), w_out[0], m_w_out[0], v_w_out[0], OB)
    g_pool, d_pool, nm_pool, nv_pool = _reduce_scatter_adamw(
        "rs_w_pool", dwp, dwp.astype(BF16), w_pool[0].reshape(pool_rows), m_w_pool[0].reshape(pool_rows),
        v_w_pool[0].reshape(pool_rows), N_GROUPS * PB)

    tot, dl, nm, nv = _replicated_adamw(stats, gps, ln_gain, ln_bias, pool_scale, m_ln_gain, m_ln_bias, m_pool_scale,
                                        v_ln_gain, v_ln_bias, v_pool_scale)

    shard4 = lambda t: t.reshape(1, N_GROUPS, PB, GC)
    lead = lambda t: t[None]
    small = lambda t: (t[3:4, 0:DP], t[0:1], t[1:2])
    g_ps, g_gain, g_bias = small(tot)
    d_ps, d_gain, d_bias = small(dl)
    nm_ps, nm_gain, nm_bias = small(nm)
    nv_ps, nv_gain, nv_bias = small(nv)
    return (tot[2, 0], lead(grad_x),
            lead(g_in), shard4(g_pool), g_ps, lead(g_out), g_gain, g_bias,
            lead(d_in), shard4(d_pool), d_ps, lead(d_out), d_gain, d_bias,
            lead(nm_in), shard4(nm_pool), nm_ps, lead(nm_out), nm_gain, nm_bias,
            lead(nv_in), shard4(nv_pool), nv_ps, lead(nv_out), nv_gain, nv_bias)
```

```python
import functools

import jax
import jax.numpy as jnp
from jax import lax
from jax.experimental import pallas as pl
from jax.experimental.pallas import tpu as pltpu

F32 = jnp.float32
BF16 = jnp.bfloat16
SDS = jax.ShapeDtypeStruct
MESH = pl.DeviceIdType.MESH

N_DEV = 8
S = 4096
D = 2048
N_HEADS = 8
DH = 128
DA = N_HEADS * DH
DP = 1024
N_GROUPS = 4
GC = DP // N_GROUPS
POOL_WINDOWS = (2, 4, 8, 16)
HALO = 16
D_IN = 3 * DA + DP + D
WB = D_IN // N_DEV
OB = D // N_DEV
PB = GC // N_DEV
ROPE_DIM = DH // 4
ROPE_HALF = ROPE_DIM // 2
ROPE_THETA = 500000.0
DILATIONS = (1, 4, 16)
KB = 128
LN_EPS = 1e-5
ALPHA = 2.0 ** 0.25
SCALE = DH ** -0.5
NEG = -1e30
ADAM_LR, ADAM_B1, ADAM_B2, ADAM_EPS, ADAM_WD, ADAM_STEP = 0.001, 0.9, 0.999, 1e-08, 0.01, 10

VMEM_LIMIT_V7X = 56 * 1024 * 1024

NT = (((1,), (1,)), ((), ()))
TN = (((0,), (0,)), ((), ()))


def _params(sem=None):
    return pltpu.CompilerParams(dimension_semantics=sem, vmem_limit_bytes=VMEM_LIMIT_V7X)


def _dot(a, b, dims=None):
    if dims is None:
        return jnp.dot(a, b, preferred_element_type=F32)
    return lax.dot_general(a, b, dims, preferred_element_type=F32)


def _rope_tables():
    inv_freq = ROPE_THETA ** (-(2.0 * jnp.arange(ROPE_HALF, dtype=F32)) / ROPE_DIM)
    ang = jnp.arange(S, dtype=jnp.int32).astype(F32)[:, None] * inv_freq[None, :]
    cos, sin = jnp.cos(ang), jnp.sin(ang)
    rest = DH - ROPE_DIM
    c = jnp.concatenate([cos, cos, jnp.ones((S, rest), F32)], axis=1)
    a = jnp.concatenate([-sin, jnp.zeros((S, DH - ROPE_HALF), F32)], axis=1)
    b = jnp.concatenate([jnp.zeros((S, ROPE_HALF), F32), sin, jnp.zeros((S, rest), F32)], axis=1)
    return c, a, b


def _rope(t, c, a, b):
    return t * c + pltpu.roll(t, DH - ROPE_HALF, 1) * a + pltpu.roll(t, ROPE_HALF, 1) * b


def _rope_t(g, c, a, b):
    return g * c - pltpu.roll(g, DH - ROPE_HALF, 1) * a - pltpu.roll(g, ROPE_HALF, 1) * b


def _prep_x(x):
    tm = 512

    def body(x_ref, xb_ref, xt_ref):
        xv = x_ref[...]
        xb_ref[...] = xv.astype(BF16)
        xt_ref[...] = xv.T.astype(BF16)

    return pl.pallas_call(
        body, name="prep_x", grid=(S // tm,),
        in_specs=[pl.BlockSpec((tm, D), lambda i: (i, 0))],
        out_specs=[pl.BlockSpec((tm, D), lambda i: (i, 0)), pl.BlockSpec((D, tm), lambda i: (0, i))],
        out_shape=[SDS((S, D), BF16), SDS((D, S), BF16)],
        compiler_params=_params(("arbitrary",)),
    )(x)


def _residues(slab_ref, idx, d, r, n):
    return slab_ref[(*idx, pl.ds(r, n, stride=d), slice(None))]


def _proj(xb, wg, col0, ncol, out_dtype, name, rope=None, by_residue=False):
    tm, tn = 1024, 256
    cb0 = col0 // tn
    per_blk = WB // tn
    n_heads = tn // DH
    dils = DILATIONS[1:] if by_residue else ()

    def body(x_ref, w_ref, *rest):
        tabs = [t[...] for t in rest[:3]] if rope is not None else None
        rest = rest[3:] if rope is not None else rest
        o_ref = rest[0]
        acc = _dot(x_ref[...], w_ref[...])
        if not by_residue and rope is None:
            o_ref[...] = acc.astype(out_dtype)
            return
        slab_ref = rest[-1] if by_residue else None
        for hh in range(n_heads):
            hs = slice(hh * DH, (hh + 1) * DH)
            t = acc[:, hs] if rope is None else _rope(acc[:, hs], *tabs)
            o_ref[:, hs] = t.astype(out_dtype)
            if by_residue:
                slab_ref[hh] = t
        for n, d in enumerate(dils):
            for hh in range(n_heads):
                hs = slice(hh * DH, (hh + 1) * DH)
                for r in range(d):
                    rest[1 + n][r, :, hs] = _residues(slab_ref, (hh,), d, r, tm // d).astype(out_dtype)

    in_specs = [pl.BlockSpec((tm, D), lambda i, j: (i, 0)),
                pl.BlockSpec((None, D, tn), lambda i, j: ((cb0 + j) // per_blk, 0, (cb0 + j) % per_blk))]
    args = [xb, wg]
    if rope is not None:
        in_specs += [pl.BlockSpec((tm, DH), lambda i, j: (i, 0))] * 3
        args += list(rope)
    out_specs = [pl.BlockSpec((tm, tn), lambda i, j: (i, j))]
    out_shape = [SDS((S, ncol), out_dtype)]
    for d in dils:
        out_specs.append(pl.BlockSpec((d, tm // d, tn), lambda i, j: (0, i, j)))
        out_shape.append(SDS((d, S // d, ncol), out_dtype))
    outs = pl.pallas_call(
        body, name=name, grid=(S // tm, ncol // tn),
        in_specs=in_specs, out_specs=out_specs, out_shape=out_shape,
        scratch_shapes=[pltpu.VMEM((n_heads, tm, DH), F32)] if by_residue else [],
        compiler_params=_params(("arbitrary", "arbitrary")),
    )(*args)
    return list(outs) if by_residue else outs[0]


BQ = 2 * KB
LANES = 128


def _to_lane(acc, col, h):
    lane = lax.broadcasted_iota(jnp.int32, acc.shape, 1)
    return jnp.where(lane == h, col, acc)


def _attn_fwd(qk, v, name):
    d, n_sub, _ = qk.shape
    n_i = n_sub // BQ
    kw = KB + BQ

    def prev(i):
        return jnp.maximum(2 * i - 1, 0)

    def body(q_ref, kc_ref, kp_ref, vc_ref, vp_ref, o_ref, l_ref, kw_ref, vw_ref):
        i = pl.program_id(1)
        kw_ref[0:KB, :] = kp_ref[...]
        kw_ref[KB:kw, :] = kc_ref[...]
        vw_ref[0:KB, :] = vp_ref[...]
        vw_ref[KB:kw, :] = vc_ref[...]
        a = lax.broadcasted_iota(jnp.int32, (BQ, kw), 0)
        b = lax.broadcasted_iota(jnp.int32, (BQ, kw), 1)
        first_key = jnp.where(i == 0, KB, 0)
        mask = (b >= jnp.maximum(a, first_key)) & (b <= a + KB)
        lse = jnp.zeros((BQ, LANES), F32)
        for h in range(N_HEADS):
            hs = slice(h * DH, (h + 1) * DH)
            s = jnp.where(mask, _dot(q_ref[:, hs], kw_ref[:, hs], NT) * SCALE, NEG)
            m = jnp.max(s, axis=1, keepdims=True)
            p = jnp.exp(s - m)
            den = jnp.sum(p, axis=1, keepdims=True)
            o_ref[:, hs] = _dot(p.astype(BF16), vw_ref[:, hs]) / den
            lse = _to_lane(lse, m + jnp.log(den), h)
        l_ref[...] = lse

    return pl.pallas_call(
        body, name=name, grid=(d, n_i),
        in_specs=[pl.BlockSpec((None, BQ, DA), lambda r, i: (r, i, 0)),
                  pl.BlockSpec((None, BQ, DA), lambda r, i: (r, i, 1)),
                  pl.BlockSpec((None, KB, DA), lambda r, i: (r, prev(i), 1)),
                  pl.BlockSpec((None, BQ, DA), lambda r, i: (r, i, 0)),
                  pl.BlockSpec((None, KB, DA), lambda r, i: (r, prev(i), 0))],
        out_specs=[pl.BlockSpec((None, BQ, DA), lambda r, i: (r, i, 0)),
                   pl.BlockSpec((None, BQ, LANES), lambda r, i: (r, i, 0))],
        out_shape=[SDS((d, n_sub, DA), F32), SDS((d, n_sub, LANES), F32)],
        scratch_shapes=[pltpu.VMEM((kw, DA), BF16), pltpu.VMEM((kw, DA), BF16)],
        compiler_params=_params(("arbitrary", "arbitrary")),
    )(qk, qk, qk, v, v)


def _pooled(ext_ref, g, rows, tm):
    w = POOL_WINDOWS[g]
    cs = slice(g * GC, (g + 1) * GC)
    cur = ext_ref[HALO:HALO + tm, cs]
    win = cur
    for j in range(1, w):
        win = win + ext_ref[HALO - j:HALO - j + tm, cs]
    cnt = jnp.minimum(rows + 1, w).astype(F32)
    return win / cnt - cur, cnt


def _fill_ext(ext_ref, u_ref, uh_ref, blk, tm):
    @pl.when(blk == 0)
    def _():
        ext_ref[0:HALO, :] = jnp.zeros((HALO, DP), F32)

    @pl.when(blk > 0)
    def _():
        ext_ref[0:HALO, :] = uh_ref[...]

    ext_ref[HALO:HALO + tm, :] = u_ref[...]


def _residue_specs(tm, width):
    return [pl.BlockSpec((d, tm // d, width), lambda i: (0, i, 0)) for d in DILATIONS[1:]]


def _mix(o1, l1, o4, l4, o16, l16, u, gate, wp, scale):
    tm = 256
    n_slab = N_HEADS + 1

    def body(o1r, l1r, o4r, l4r, o16r, l16r, u_ref, uh_ref, g_ref, wp_ref, sc_ref,
             y_ref, yt_ref, attn_ref, lse_ref, lse4_ref, lse16_ref, ext_ref, ys_ref, nat_ref, ls_ref):
        i = pl.program_id(0)
        for n, (d, o_r, l_r) in enumerate(((DILATIONS[1], o4r, l4r), (DILATIONS[2], o16r, l16r))):
            for r in range(d):
                rows = pl.ds(r, tm // d, stride=d)
                for h in range(N_HEADS):
                    nat_ref[n, h, rows, :] = o_r[r, :, h * DH:(h + 1) * DH]
                nat_ref[n, N_HEADS, rows, :] = l_r[r]
        la, lb, lc = l1r[...], nat_ref[0, N_HEADS], nat_ref[1, N_HEADS]
        mx = jnp.maximum(jnp.maximum(la, lb), lc)
        ea, eb, ec = jnp.exp(la - mx), jnp.exp(lb - mx), jnp.exp(lc - mx)
        z = ea + eb + ec
        wa, wb, wc = ea / z, eb / z, ec / z
        lse = mx + jnp.log(z)
        lse_ref[...] = lse
        ls_ref[...] = lse
        for d, out in ((DILATIONS[1], lse4_ref), (DILATIONS[2], lse16_ref)):
            for r in range(d):
                out[r] = ls_ref[pl.ds(r, tm // d, stride=d), :]
        for h in range(N_HEADS):
            hs = slice(h * DH, (h + 1) * DH)
            hc = slice(h, h + 1)
            attn = wa[:, hc] * o1r[:, hs] + wb[:, hc] * nat_ref[0, h] + wc[:, hc] * nat_ref[1, h]
            attn_ref[:, hs] = attn
            gt = g_ref[:, hs]
            ys_ref[:, hs] = attn * (gt * jax.nn.sigmoid(gt))

        _fill_ext(ext_ref, u_ref, uh_ref, i, tm)
        rows = i * tm + lax.broadcasted_iota(jnp.int32, (tm, 1), 0)
        for g in range(N_GROUPS):
            cs = slice(g * GC, (g + 1) * GC)
            gs = slice(DA + g * GC, DA + (g + 1) * GC)
            pooled, _ = _pooled(ext_ref, g, rows, tm)
            po = _dot(pooled.astype(BF16), wp_ref[g]) * sc_ref[:, cs]
            gt = g_ref[:, gs]
            ys_ref[:, gs] = po * (gt * jax.nn.sigmoid(gt))
        yv = ys_ref[...]
        y_ref[...] = yv.astype(BF16)
        yt_ref[...] = yv.T.astype(BF16)

    row = lambda i: (i, 0)
    blk = pl.BlockSpec((tm, DA), row)
    lanes = pl.BlockSpec((tm, LANES), row)
    o_res, l_res = _residue_specs(tm, DA), _residue_specs(tm, LANES)
    return pl.pallas_call(
        body, name="mix", grid=(S // tm,),
        in_specs=[blk, lanes, o_res[0], l_res[0], o_res[1], l_res[1],
                  pl.BlockSpec((tm, DP), row),
                  pl.BlockSpec((HALO, DP), lambda i: (jnp.maximum(i * (tm // HALO) - 1, 0), 0)),
                  pl.BlockSpec((tm, D), row),
                  pl.BlockSpec((N_GROUPS, GC, GC), lambda i: (0, 0, 0)),
                  pl.BlockSpec((1, DP), lambda i: (0, 0))],
        out_specs=[pl.BlockSpec((tm, D), row), pl.BlockSpec((D, tm), lambda i: (0, i)), blk, lanes] + l_res,
        out_shape=[SDS((S, D), BF16), SDS((D, S), BF16), SDS((S, DA), F32), SDS((S, LANES), F32)]
        + [SDS((d, S // d, LANES), F32) for d in DILATIONS[1:]],
        scratch_shapes=[pltpu.VMEM((HALO + tm, DP), F32), pltpu.VMEM((tm, D), F32),
                        pltpu.VMEM((2, n_slab, tm, DH), F32), pltpu.VMEM((tm, LANES), F32)],
        compiler_params=_params(("arbitrary",)),
    )(o1, l1, o4, l4, o16, l16, u, u, gate, wp, scale)


def _outproj_ln(y, wout, x, target, gain, bias):
    tm = 256

    def body(y_ref, w_ref, x_ref, t_ref, g_ref, b_ref, dz_ref, dzb_ref, st_ref):
        i = pl.program_id(0)

        @pl.when(i == 0)
        def _():
            st_ref[...] = jnp.zeros((8, D), F32)

        z = ALPHA * x_ref[...] + _dot(y_ref[...], w_ref[...])
        mu = jnp.mean(z, axis=1, keepdims=True)
        zc = z - mu
        var = jnp.mean(zc * zc, axis=1, keepdims=True)
        rstd = lax.rsqrt(var + LN_EPS)
        xhat = zc * rstd
        gn = g_ref[...]
        diff = xhat * gn + b_ref[...] - t_ref[...]
        dyln = diff / D
        st_ref[0:1, :] += jnp.sum(dyln * xhat, axis=0, keepdims=True)
        st_ref[1:2, :] += jnp.sum(dyln, axis=0, keepdims=True)
        row_loss = jnp.sum(diff * diff, axis=1, keepdims=True) / D
        st_ref[2:3, :] += jnp.broadcast_to(0.5 * jnp.sum(row_loss, axis=0, keepdims=True), (1, D))
        dxh = dyln * gn
        m1 = jnp.mean(dxh, axis=1, keepdims=True)
        m2 = jnp.mean(dxh * xhat, axis=1, keepdims=True)
        dz = rstd * (dxh - m1 - xhat * m2)
        dz_ref[...] = dz
        dzb_ref[...] = dz.astype(BF16)

    row = lambda i: (i, 0)
    const = lambda i: (0, 0)
    return pl.pallas_call(
        body, name="outproj_ln", grid=(S // tm,),
        in_specs=[pl.BlockSpec((tm, D), row),
                  pl.BlockSpec((D, D), const, pipeline_mode=pl.Buffered(1)),
                  pl.BlockSpec((tm, D), row), pl.BlockSpec((tm, D), row),
                  pl.BlockSpec((1, D), const), pl.BlockSpec((1, D), const)],
        out_specs=[pl.BlockSpec((tm, D), row), pl.BlockSpec((tm, D), row), pl.BlockSpec((8, D), const)],
        out_shape=[SDS((S, D), F32), SDS((S, D), BF16), SDS((8, D), F32)],
        compiler_params=_params(("arbitrary",)),
    )(y, wout, x, target, gain, bias)


def _bwd_mid(dzb, wout, gate, attn, u, wp, scale):
    tm = 256
    n = S // tm

    def body(dz_ref, w_ref, g_ref, at_ref, u_ref, uh_ref, wp_ref, sc_ref,
             dh_ref, do_ref, do4_ref, do16_ref, dd_ref, dd4_ref, dd16_ref, dwp_ref, dwp16_ref, gps_ref,
             ext_ref, eext_ref, acc_ref, nat_ref, ds_ref):
        i = pl.program_id(0)
        ib = n - 1 - i

        @pl.when(i == 0)
        def _():
            eext_ref[tm:tm + HALO, :] = jnp.zeros((HALO, DP), F32)
            acc_ref[...] = jnp.zeros((N_GROUPS, GC, GC), F32)
            gps_ref[...] = jnp.zeros((8, DP), F32)

        dy = _dot(dz_ref[...], w_ref[...], NT)
        gt = g_ref[...]
        sg = jax.nn.sigmoid(gt)
        silu = gt * sg
        dsilu = sg * (1.0 + gt * (1.0 - sg))
        dmix = dy * silu
        dyg = dy * dsilu

        at = at_ref[...]
        dat = dmix[:, 0:DA]
        do_ref[...] = dat.astype(BF16)
        dh_ref[:, DP:DP + DA] = (dyg[:, 0:DA] * at).astype(BF16)
        prod = dat * at
        dd = jnp.zeros((tm, LANES), F32)
        for h in range(N_HEADS):
            hs = slice(h * DH, (h + 1) * DH)
            dd = _to_lane(dd, jnp.sum(prod[:, hs], axis=1, keepdims=True), h)
            nat_ref[h] = dat[:, hs]
        dd_ref[...] = dd
        ds_ref[...] = dd
        for d, do_out, dd_out in ((DILATIONS[1], do4_ref, dd4_ref), (DILATIONS[2], do16_ref, dd16_ref)):
            for r in range(d):
                dd_out[r] = _residues(ds_ref, (), d, r, tm // d)
                for h in range(N_HEADS):
                    do_out[r, :, h * DH:(h + 1) * DH] = _residues(nat_ref, (h,), d, r, tm // d).astype(BF16)

        _fill_ext(ext_ref, u_ref, uh_ref, ib, tm)
        rows = ib * tm + lax.broadcasted_iota(jnp.int32, (tm, 1), 0)
        for g in range(N_GROUPS):
            w = POOL_WINDOWS[g]
            cs = slice(g * GC, (g + 1) * GC)
            gs = slice(DA + g * GC, DA + (g + 1) * GC)
            pooled, cnt = _pooled(ext_ref, g, rows, tm)
            pre = _dot(pooled.astype(BF16), wp_ref[g])
            sc = sc_ref[:, cs]
            dpo = dmix[:, gs]
            gps_ref[0:1, cs] += jnp.sum(dpo * pre, axis=0, keepdims=True)
            dh_ref[:, DP + DA + g * GC:DP + DA + (g + 1) * GC] = (dyg[:, gs] * (pre * sc)).astype(BF16)
            dpre = (dpo * sc).astype(BF16)
            acc_ref[g] += _dot(pooled.T.astype(BF16), dpre)
            dpooled = _dot(dpre, wp_ref[g], NT)
            eext_ref[0:tm, cs] = dpooled / cnt
            du = eext_ref[0:tm, cs]
            for j in range(1, w):
                du = du + eext_ref[j:j + tm, cs]
            dh_ref[:, cs] = (du - dpooled).astype(BF16)
        eext_ref[tm:tm + HALO, :] = eext_ref[0:HALO, :]

        @pl.when(i == n - 1)
        def _():
            for j in range(N_DEV):
                for g in range(N_GROUPS):
                    blk = acc_ref[g, j * PB:(j + 1) * PB, :]
                    dwp_ref[j, g * PB:(g + 1) * PB, :] = blk
                    dwp16_ref[j, g * PB:(g + 1) * PB, :] = blk.astype(BF16)

    rev = lambda i: (n - 1 - i, 0)
    const = lambda i: (0, 0)
    res = lambda width: [pl.BlockSpec((d, tm // d, width), lambda i: (0, n - 1 - i, 0)) for d in DILATIONS[1:]]
    pool_blocks = pl.BlockSpec((N_DEV, N_GROUPS * PB, GC), lambda i: (0, 0, 0))
    return pl.pallas_call(
        body, name="bwd_mid", grid=(n,),
        in_specs=[pl.BlockSpec((tm, D), rev),
                  pl.BlockSpec((D, D), const, pipeline_mode=pl.Buffered(1)),
                  pl.BlockSpec((tm, D), rev), pl.BlockSpec((tm, DA), rev), pl.BlockSpec((tm, DP), rev),
                  pl.BlockSpec((HALO, DP), lambda i: (jnp.maximum((n - 1 - i) * (tm // HALO) - 1, 0), 0)),
                  pl.BlockSpec((N_GROUPS, GC, GC), lambda i: (0, 0, 0)),
                  pl.BlockSpec((1, DP), const)],
        out_specs=[pl.BlockSpec((tm, DP + D), rev), pl.BlockSpec((tm, DA), rev)] + res(DA)
        + [pl.BlockSpec((tm, LANES), rev)] + res(LANES) + [pool_blocks, pool_blocks, pl.BlockSpec((8, DP), const)],
        out_shape=[SDS((S, DP + D), BF16), SDS((S, DA), BF16)] + [SDS((d, S // d, DA), BF16) for d in DILATIONS[1:]]
        + [SDS((S, LANES), F32)] + [SDS((d, S // d, LANES), F32) for d in DILATIONS[1:]]
        + [SDS((N_DEV, N_GROUPS * PB, GC), F32), SDS((N_DEV, N_GROUPS * PB, GC), BF16), SDS((8, DP), F32)],
        scratch_shapes=[pltpu.VMEM((HALO + tm, DP), F32), pltpu.VMEM((tm + HALO, DP), F32),
                        pltpu.VMEM((N_GROUPS, GC, GC), F32), pltpu.VMEM((N_HEADS, tm, DH), F32),
                        pltpu.VMEM((tm, LANES), F32)],
        compiler_params=_params(("arbitrary",)),
    )(dzb, wout, gate, attn, u, u, wp, scale)


def _attn_bwd(qk, v, do, lse, dd, rope, name, others=None):
    d, n_sub, _ = qk.shape
    n_i = n_sub // BQ
    n_kb = n_sub // KB
    qw = BQ + KB
    final = others is not None
    out_dtype = BF16 if final else F32
    n_cb = 3 * DA // DH

    def nxt(i):
        return jnp.minimum(2 * i + 2, n_kb - 1)

    def body(qc_ref, qn_ref, kc_ref, vc_ref, doc_ref, don_ref, lc_ref, ln_ref, dc_ref, dn_ref,
             c_ref, a_ref, b_ref, *rest):
        if final:
            acc4_ref, acc16_ref, out_ref, carry_ref, qw_ref, dow_ref, lw_ref, dw_ref, nat_ref = rest
        else:
            out_ref, carry_ref, qw_ref, dow_ref, lw_ref, dw_ref = rest
        i = pl.program_id(1)

        @pl.when(i == 0)
        def _():
            carry_ref[...] = jnp.zeros((KB, DA), F32)

        for win, own, after in ((qw_ref, qc_ref, qn_ref), (dow_ref, doc_ref, don_ref), (lw_ref, lc_ref, ln_ref),
                                (dw_ref, dc_ref, dn_ref)):
            win[0:BQ, :] = own[...]
            win[BQ:qw, :] = after[...]
        if final:
            for n, (dil, acc) in enumerate(((DILATIONS[1], acc4_ref), (DILATIONS[2], acc16_ref))):
                for r in range(dil):
                    for cb in range(n_cb):
                        nat_ref[n, cb, pl.ds(r, BQ // dil, stride=dil), :] = acc[r, :, cb * DH:(cb + 1) * DH]

        a = lax.broadcasted_iota(jnp.int32, (qw, BQ), 0)
        b = lax.broadcasted_iota(jnp.int32, (qw, BQ), 1)
        n_q = jnp.where(i == n_i - 1, BQ, qw)
        mask = (b <= a) & (a <= b + KB) & (a < n_q)
        tabs = (c_ref[...], a_ref[...], b_ref[...])
        for h in range(N_HEADS):
            hs = slice(h * DH, (h + 1) * DH)
            hc = slice(h, h + 1)
            q, k, vv, dob = qw_ref[:, hs], kc_ref[:, hs], vc_ref[:, hs], dow_ref[:, hs]
            s = _dot(q, k, NT) * SCALE
            p = jnp.exp(jnp.where(mask, s - lw_ref[:, hc], NEG))
            dp = _dot(dob, vv, NT)
            ds = (p * (dp - dw_ref[:, hc]) * SCALE).astype(BF16)
            dq = _dot(ds, k)
            dk = _dot(ds, q, TN)
            dv = _dot(p.astype(BF16), dob, TN)
            dq_lo = dq[0:KB] + carry_ref[:, hs]
            carry_ref[:, hs] = dq[BQ:qw]
            dq_own = _rope_t(jnp.concatenate([dq_lo, dq[KB:BQ]], axis=0), *tabs)
            for base, gv in ((0, dq_own), (DA, _rope_t(dk, *tabs)), (2 * DA, dv)):
                if final:
                    cb = base // DH + h
                    gv = gv + nat_ref[0, cb] + nat_ref[1, cb]
                out_ref[:, base + h * DH:base + (h + 1) * DH] = gv.astype(out_dtype)

    def cur(rows, width, col=0):
        return pl.BlockSpec((None, rows, width), lambda r, i: (r, i, col))

    def nx(width, col=0):
        return pl.BlockSpec((None, KB, width), lambda r, i: (r, nxt(i), col))

    in_specs = [cur(BQ, DA), nx(DA), cur(BQ, DA, 1), cur(BQ, DA), cur(BQ, DA), nx(DA),
                cur(BQ, LANES), nx(LANES), cur(BQ, LANES), nx(LANES)] + [cur(BQ, DH)] * 3
    args = [qk, qk, qk, v, do, do, lse, lse, dd, dd] + list(rope)
    scratch = [pltpu.VMEM((KB, DA), F32), pltpu.VMEM((qw, DA), BF16), pltpu.VMEM((qw, DA), BF16),
               pltpu.VMEM((qw, LANES), F32), pltpu.VMEM((qw, LANES), F32)]
    if final:
        assert d == 1
        in_specs += [pl.BlockSpec((dil, BQ // dil, 3 * DA), lambda r, i: (0, i, 0)) for dil in DILATIONS[1:]]
        args += list(others)
        scratch.append(pltpu.VMEM((2, n_cb, BQ, DH), F32))
    return pl.pallas_call(
        body, name=name, grid=(d, n_i),
        in_specs=in_specs,
        out_specs=cur(BQ, 3 * DA),
        out_shape=SDS((d, n_sub, 3 * DA), out_dtype),
        scratch_shapes=scratch,
        compiler_params=_params(("arbitrary", "arbitrary")),
    )(*args)


def _grad_w_out(yt, dzb):
    tk = 512

    def body(yt_ref, dz_ref, o_ref, o16_ref, acc_ref, b16_ref):
        k = pl.program_id(0)

        @pl.when(k == 0)
        def _():
            acc_ref[...] = jnp.zeros((D, D), F32)

        acc_ref[...] += _dot(yt_ref[...], dz_ref[...])

        @pl.when(k == S // tk - 1)
        def _():
            b16_ref[...] = acc_ref[...].astype(BF16)
            pltpu.sync_copy(acc_ref, o_ref)
            pltpu.sync_copy(b16_ref, o16_ref)

    hbm = pl.BlockSpec(memory_space=pl.ANY)
    o, o16 = pl.pallas_call(
        body, name="grad_w_out", grid=(S // tk,),
        in_specs=[pl.BlockSpec((D, tk), lambda k: (0, k)), pl.BlockSpec((tk, D), lambda k: (k, 0))],
        out_specs=[hbm, hbm],
        out_shape=[SDS((D, D), F32), SDS((D, D), BF16)],
        scratch_shapes=[pltpu.VMEM((D, D), F32), pltpu.VMEM((D, D), BF16)],
        compiler_params=_params(("arbitrary",)),
    )(yt, dzb)
    return o.reshape(N_DEV, OB, D), o16.reshape(N_DEV, OB, D)


def _dh_specs(rows, which_block, row_of):
    half = N_DEV // 2

    def spec(lo):
        def index(*g):
            j = which_block(*g)
            mine = (j >= lo) & (j < lo + half)
            return (jnp.where(mine, row_of(*g), 0), jnp.where(mine, j - lo, 0))
        return pl.BlockSpec((rows, WB), index)

    return [spec(0), spec(half)]


def _grad_w_in(xt, dh_a, dh_b):
    tk = 512
    n_k = S // tk
    half = N_DEV // 2

    def body(xt_ref, da_ref, db_ref, o_ref, o16_ref, acc_ref):
        j, k = pl.program_id(0), pl.program_id(1)

        @pl.when(k == 0)
        def _():
            acc_ref[...] = jnp.zeros((D, WB), F32)

        @pl.when(j < half)
        def _():
            acc_ref[...] += _dot(xt_ref[...], da_ref[...])

        @pl.when(j >= half)
        def _():
            acc_ref[...] += _dot(xt_ref[...], db_ref[...])

        @pl.when(k == n_k - 1)
        def _():
            o_ref[...] = acc_ref[...]
            o16_ref[...] = acc_ref[...].astype(BF16)

    blk = lambda j, k: (j, 0, 0)
    return pl.pallas_call(
        body, name="grad_w_in", grid=(N_DEV, n_k),
        in_specs=[pl.BlockSpec((D, tk), lambda j, k: (0, k))] + _dh_specs(tk, lambda j, k: j, lambda j, k: k),
        out_specs=[pl.BlockSpec((None, D, WB), blk), pl.BlockSpec((None, D, WB), blk)],
        out_shape=[SDS((N_DEV, D, WB), F32), SDS((N_DEV, D, WB), BF16)],
        scratch_shapes=[pltpu.VMEM((D, WB), F32)],
        compiler_params=_params(("arbitrary", "arbitrary")),
    )(xt, dh_a, dh_b)


def _grad_x(dz, dh_a, dh_b, wg):
    tm = 1024
    half = N_DEV // 2

    def body(dz_ref, da_ref, db_ref, w_ref, o_ref):
        j = pl.program_id(1)

        @pl.when(j == 0)
        def _():
            o_ref[...] = ALPHA * dz_ref[...]

        @pl.when(j < half)
        def _():
            o_ref[...] += _dot(da_ref[...], w_ref[...], NT)

        @pl.when(j >= half)
        def _():
            o_ref[...] += _dot(db_ref[...], w_ref[...], NT)

    return pl.pallas_call(
        body, name="grad_x", grid=(S // tm, N_DEV),
        in_specs=[pl.BlockSpec((tm, D), lambda i, j: (i, 0))]
        + _dh_specs(tm, lambda i, j: j, lambda i, j: i)
        + [pl.BlockSpec((None, D, WB), lambda i, j: (j, 0, 0))],
        out_specs=pl.BlockSpec((tm, D), lambda i, j: (i, 0)),
        out_shape=SDS((S, D), F32),
        compiler_params=_params(("arbitrary", "arbitrary")),
    )(dz, dh_a, dh_b, wg)


def _place():
    x, y, c = lax.axis_index("x"), lax.axis_index("y"), lax.axis_index("c")
    chips = [(x, y), (1 - x, y), (x, 1 - y), (1 - x, 1 - y)]
    return x, y, c, chips


def _blk(x, y, c):
    return 4 * x + 2 * y + c


def _adamw(w, g, m, v):
    m = ADAM_B1 * m + (1.0 - ADAM_B1) * g
    v = ADAM_B2 * v + (1.0 - ADAM_B2) * (g * g)
    m_hat = m / (1.0 - ADAM_B1 ** ADAM_STEP)
    v_hat = v / (1.0 - ADAM_B2 ** ADAM_STEP)
    delta = -ADAM_LR * (m_hat / (jnp.sqrt(v_hat) + ADAM_EPS) + ADAM_WD * w)
    return delta, m, v


def _gather_weights(w_in_s, w_out_s, w_pool_s):
    def body(win_ref, wout_ref, wpool_ref, gin_ref, gout_ref, gpool_ref, bin_ref, bout_ref, bpool_ref,
             send_sems, recv_sems, local_sems):
        x, y, c, chips = _place()
        me, sib = (x, y, c), (x, y, 1 - c)
        bin_ref[...] = win_ref[...].astype(BF16)
        bout_ref[...] = wout_ref[...].astype(BF16)
        bpool_ref[...] = wpool_ref[...].astype(BF16)

        def pool_rows(b):
            return gpool_ref.at[:, pl.ds(pl.multiple_of(b * PB, PB), PB), :]

        tensors = [(bin_ref, lambda b: gin_ref.at[b]), (bout_ref, lambda b: gout_ref.at[b]), (bpool_ref, pool_rows)]

        def copy(t, k, block, to, src=None):
            dst = tensors[t][1](_blk(*block))
            return pltpu.make_async_remote_copy(
                src_ref=dst if src is None else src, dst_ref=dst,
                send_sem=send_sems.at[t, k], recv_sem=recv_sems.at[t, k], device_id=to, device_id_type=MESH)

        started = []
        own = []
        for t, (src, dst) in enumerate(tensors):
            cp = pltpu.make_async_copy(src, dst(_blk(*me)), local_sems.at[t])
            cp.start()
            own.append(cp)
            first = [copy(t, 0, me, sib, src=src)]
            first += [copy(t, j, me, (*chips[j], c), src=src) for j in (1, 2, 3)]
            for cp in first:
                cp.start()
            started += first
        for t in range(len(tensors)):
            for j in (1, 2, 3):
                copy(t, j, (*chips[j], c), me).wait_recv()
                fwd = copy(t, 3 + j, (*chips[j], c), sib)
                fwd.start()
                started.append(fwd)
        for t in range(len(tensors)):
            copy(t, 0, sib, me).wait_recv()
            for j in (1, 2, 3):
                copy(t, 3 + j, (*chips[j], 1 - c), me).wait_recv()
        for cp in started:
            cp.wait_send()
        for cp in own:
            cp.wait()

    vmem = pl.BlockSpec(memory_space=pltpu.VMEM)
    hbm = pl.BlockSpec(memory_space=pl.ANY)
    gin, gout, gpool = pl.pallas_call(
        body, name="gather_weights",
        in_specs=[vmem, vmem, vmem], out_specs=[hbm, hbm, hbm],
        out_shape=[SDS((N_DEV, D, WB), BF16), SDS((N_DEV, OB, D), BF16), SDS((N_GROUPS, GC, GC), BF16)],
        scratch_shapes=[pltpu.VMEM((D, WB), BF16), pltpu.VMEM((OB, D), BF16), pltpu.VMEM((N_GROUPS, PB, GC), BF16),
                        pltpu.SemaphoreType.DMA((3, 7)), pltpu.SemaphoreType.DMA((3, 7)), pltpu.SemaphoreType.DMA((3,))],
        compiler_params=_params(),
    )(w_in_s, w_out_s, w_pool_s)
    return gin, gout.reshape(D, D), gpool


def _reduce_scatter_adamw(name, p32, p16, w, m, v, rc):
    _, r_tot, cols = p32.shape
    n_ch = r_tot // rc

    def body(p32_ref, p16_ref, w_ref, m_ref, v_ref, g_ref, d_ref, nm_ref, nv_ref, r1_ref, r2_ref,
             s2_ref, a_ref, b_ref, wb_ref, mb_ref, vb_ref, og_ref, od_ref, om_ref, ov_ref,
             s1_send, s1_recv, s2_send, s2_recv):
        x, y, c, chips = _place()
        sib = (x, y, 1 - c)

        def stage1(k):
            return pltpu.make_async_remote_copy(
                src_ref=p16_ref.at[_blk(*chips[k], 1 - c)], dst_ref=r1_ref.at[k],
                send_sem=s1_send.at[k], recv_sem=s1_recv.at[k], device_id=sib, device_id_type=MESH)

        def stage2(k):
            return pltpu.make_async_remote_copy(
                src_ref=s2_ref.at[k - 1], dst_ref=r2_ref.at[k - 1],
                send_sem=s2_send.at[k - 1], recv_sem=s2_recv.at[k - 1], device_id=(*chips[k], c), device_id_type=MESH)

        for k in range(4):
            stage1(k).start()
        for k in (1, 2, 3):
            stage1(k).wait_recv()
            for ch in range(n_ch):
                rows = pl.ds(ch * rc, rc)
                pltpu.sync_copy(p32_ref.at[_blk(*chips[k], c), rows], a_ref)
                pltpu.sync_copy(r1_ref.at[k, rows], b_ref.at[0])
                s2_ref[k - 1, rows, :] = (a_ref[...] + b_ref[0].astype(F32)).astype(BF16)
            stage2(k).start()
        stage1(0).wait_recv()
        for k in (1, 2, 3):
            stage2(k).wait_recv()
        for ch in range(n_ch):
            rows = pl.ds(ch * rc, rc)
            pltpu.sync_copy(p32_ref.at[_blk(x, y, c), rows], a_ref)
            pltpu.sync_copy(r1_ref.at[0, rows], b_ref.at[0])
            for k in (1, 2, 3):
                pltpu.sync_copy(r2_ref.at[k - 1, rows], b_ref.at[k])
            pltpu.sync_copy(w_ref.at[rows], wb_ref)
            pltpu.sync_copy(m_ref.at[rows], mb_ref)
            pltpu.sync_copy(v_ref.at[rows], vb_ref)
            g = a_ref[...] + b_ref[0].astype(F32)
            for k in (1, 2, 3):
                g = g + b_ref[k].astype(F32)
            delta, nm, nv = _adamw(wb_ref[...], g, mb_ref[...], vb_ref[...])
            og_ref[...] = g
            od_ref[...] = delta
            om_ref[...] = nm
            ov_ref[...] = nv
            pltpu.sync_copy(og_ref, g_ref.at[rows])
            pltpu.sync_copy(od_ref, d_ref.at[rows])
            pltpu.sync_copy(om_ref, nm_ref.at[rows])
            pltpu.sync_copy(ov_ref, nv_ref.at[rows])
        for k in range(4):
            stage1(k).wait_send()
        for k in (1, 2, 3):
            stage2(k).wait_send()

    hbm = pl.BlockSpec(memory_space=pl.ANY)
    shard = SDS((r_tot, cols), F32)
    chunk = pltpu.VMEM((rc, cols), F32)
    outs = pl.pallas_call(
        body, name=name,
        in_specs=[hbm] * 5, out_specs=[hbm] * 6,
        out_shape=[shard, shard, shard, shard, SDS((4, r_tot, cols), BF16), SDS((3, r_tot, cols), BF16)],
        scratch_shapes=[pltpu.VMEM((3, r_tot, cols), BF16), chunk, pltpu.VMEM((4, rc, cols), BF16)] + [chunk] * 7
        + [pltpu.SemaphoreType.DMA((4,)), pltpu.SemaphoreType.DMA((4,)),
           pltpu.SemaphoreType.DMA((3,)), pltpu.SemaphoreType.DMA((3,))],
        compiler_params=_params(),
    )(p32, p16, w, m, v)
    return outs[:4]


def _replicated_adamw(stats, gps, gain, bias, scale, m_gain, m_bias, m_scale, v_gain, v_bias, v_scale):
    def body(st_ref, gps_ref, g_ref, b_ref, s_ref, mg_ref, mb_ref, ms_ref, vg_ref, vb_ref, vs_ref,
             all_ref, tot_ref, dl_ref, nm_ref, nv_ref, loc_ref, send_sems, recv_sems):
        x, y, c, _ = _place()
        mine = _blk(x, y, c)
        loc_ref[...] = st_ref[...]
        loc_ref[3:4, 0:DP] = gps_ref[0:1, :]
        copies = []
        for k in range(1, N_DEV):
            fx, fy, fc = (k >> 2) & 1, (k >> 1) & 1, k & 1
            peer = (x ^ fx, y ^ fy, c ^ fc)
            cp = pltpu.make_async_remote_copy(
                src_ref=loc_ref, dst_ref=all_ref.at[mine], send_sem=send_sems.at[k - 1], recv_sem=recv_sems.at[k - 1],
                device_id=peer, device_id_type=MESH)
            cp.start()
            copies.append(cp)
        all_ref[mine] = loc_ref[...]
        for cp in copies:
            cp.wait()
        tot = all_ref[0]
        for b in range(1, N_DEV):
            tot = tot + all_ref[b]
        tot_ref[...] = tot
        dl_ref[...] = jnp.zeros((8, D), F32)
        nm_ref[...] = jnp.zeros((8, D), F32)
        nv_ref[...] = jnp.zeros((8, D), F32)
        for row, width, w_r, m_r, v_r in ((0, D, g_ref, mg_ref, vg_ref), (1, D, b_ref, mb_ref, vb_ref),
                                          (3, DP, s_ref, ms_ref, vs_ref)):
            delta, nm, nv = _adamw(w_r[...], tot[row:row + 1, 0:width], m_r[...], v_r[...])
            dl_ref[row:row + 1, 0:width] = delta
            nm_ref[row:row + 1, 0:width] = nm
            nv_ref[row:row + 1, 0:width] = nv

    vmem = pl.BlockSpec(memory_space=pltpu.VMEM)
    rows = SDS((8, D), F32)
    outs = pl.pallas_call(
        body, name="replicated_adamw",
        in_specs=[vmem] * 11, out_specs=[vmem] * 5,
        out_shape=[SDS((N_DEV, 8, D), F32), rows, rows, rows, rows],
        scratch_shapes=[pltpu.VMEM((8, D), F32),
                        pltpu.SemaphoreType.DMA((N_DEV - 1,)), pltpu.SemaphoreType.DMA((N_DEV - 1,))],
        compiler_params=_params(),
    )(stats, gps, gain, bias, scale, m_gain, m_bias, m_scale, v_gain, v_bias, v_scale)
    return outs[1:]


def kernel(x, w_in, w_pool, pool_scale, w_out, ln_gain, ln_bias, loss_target, m_w_in, m_w_pool, m_pool_scale, m_w_out, m_ln_gain, m_ln_bias, v_w_in, v_w_pool, v_pool_scale, v_w_out, v_ln_gain, v_ln_bias):
    pool_rows = (N_GROUPS * PB, GC)
    x2, target = x[0], loss_target[0]
    rope = _rope_tables()

    wg_in, wg_out, wg_pool = _gather_weights(w_in[0], w_out[0], w_pool[0])

    by_res = lambda t, d: t.reshape(S // d, d, DH).transpose(1, 0, 2)
    ropes = [[by_res(t, d) for t in rope] for d in DILATIONS]

    xb, xt = _prep_x(x2)
    qk = _proj(xb, wg_in, 0, 2 * DA, BF16, "proj_qk", rope=rope, by_residue=True)
    v = _proj(xb, wg_in, 2 * DA, DA, BF16, "proj_v", by_residue=True)
    qk[0], v[0] = qk[0][None], v[0][None]
    u = _proj(xb, wg_in, 3 * DA, DP, F32, "proj_u")
    gate = _proj(xb, wg_in, 3 * DA + DP, D, F32, "proj_gate")
    fwd = [_attn_fwd(qk[n], v[n], f"attn_fwd_d{d}") for n, d in enumerate(DILATIONS)]
    y, yt, attn, lse1, lse4, lse16 = _mix(fwd[0][0][0], fwd[0][1][0], *fwd[1], *fwd[2], u, gate, wg_pool, pool_scale)
    dz, dzb, stats = _outproj_ln(y, wg_out, x2, target, ln_gain, ln_bias)

    dh_ug, do1, do4, do16, dd1, dd4, dd16, dwp, dwp16, gps = _bwd_mid(dzb, wg_out, gate, attn, u, wg_pool, pool_scale)
    do, lse, dd = [do1[None], do4, do16], [lse1[None], lse4, lse16], [dd1[None], dd4, dd16]
    others = [_attn_bwd(qk[n], v[n], do[n], lse[n], dd[n], ropes[n], f"attn_bwd_d{DILATIONS[n]}") for n in (1, 2)]
    dh_qkv = _attn_bwd(qk[0], v[0], do[0], lse[0], dd[0], ropes[0], "attn_bwd_d1", others=others)[0]
    dwout, dwout16 = _grad_w_out(yt, dzb)
    dwin, dwin16 = _grad_w_in(xt, dh_qkv, dh_ug)
    grad_x = _grad_x(dz, dh_qkv, dh_ug, wg_in)

    g_in, d_in, nm_in, nv_in = _reduce_scatter_adamw(
        "rs_w_in", dwin, dwin16, w_in[0], m_w_in[0], v_w_in[0], 512)
    g_out, d_out, nm_out, nv_out = _reduce_scatter_adamw(
        "rs_w_out", dwout, dwout16, w_out[0], m_w_out[0], v_w_out[0], OB)
    g_pool, d_pool, nm_pool, nv_pool = _reduce_scatter_adamw(
        "rs_w_pool", dwp, dwp16, w_pool[0].reshape(pool_rows), m_w_pool[0].reshape(pool_rows),
        v_w_pool[0].reshape(pool_rows), N_GROUPS * PB)

    tot, dl, nm, nv = _replicated_adamw(stats, gps, ln_gain, ln_bias, pool_scale, m_ln_gain, m_ln_bias, m_pool_scale,
                                        v_ln_gain, v_ln_bias, v_pool_scale)

    shard4 = lambda t: t.reshape(1, N_GROUPS, PB, GC)
    lead = lambda t: t[None]
    small = lambda t: (t[3:4, 0:DP], t[0:1], t[1:2])
    g_ps, g_gain, g_bias = small(tot)
    d_ps, d_gain, d_bias = small(dl)
    nm_ps, nm_gain, nm_bias = small(nm)
    nv_ps, nv_gain, nv_bias = small(nv)
    return (tot[2, 0], lead(grad_x),
            lead(g_in), shard4(g_pool), g_ps, lead(g_out), g_gain, g_bias,
            lead(d_in), shard4(d_pool), d_ps, lead(d_out), d_gain, d_bias,
            lead(nm_in), shard4(nm_pool), nm_ps, lead(nm_out), nm_gain, nm_bias,
            lead(nv_in), shard4(nv_pool), nv_ps, lead(nv_out), nv_gain, nv_bias)
```

```python
import functools

import jax
import jax.numpy as jnp
from jax import lax
from jax.experimental import pallas as pl
from jax.experimental.pallas import tpu as pltpu

F32 = jnp.float32
BF16 = jnp.bfloat16
SDS = jax.ShapeDtypeStruct
MESH = pl.DeviceIdType.MESH

N_DEV = 8
S = 4096
D = 2048
N_HEADS = 8
DH = 128
DA = N_HEADS * DH
DP = 1024
N_GROUPS = 4
GC = DP // N_GROUPS
POOL_WINDOWS = (2, 4, 8, 16)
HALO = 16
D_IN = 3 * DA + DP + D
WB = D_IN // N_DEV
OB = D // N_DEV
PB = GC // N_DEV
ROPE_DIM = DH // 4
ROPE_HALF = ROPE_DIM // 2
ROPE_THETA = 500000.0
DILATIONS = (1, 4, 16)
KB = 128
LN_EPS = 1e-5
ALPHA = 2.0 ** 0.25
SCALE = DH ** -0.5
NEG = -1e30
ADAM_LR, ADAM_B1, ADAM_B2, ADAM_EPS, ADAM_WD, ADAM_STEP = 0.001, 0.9, 0.999, 1e-08, 0.01, 10

VMEM_LIMIT_V7X = 56 * 1024 * 1024

NT = (((1,), (1,)), ((), ()))
TN = (((0,), (0,)), ((), ()))


def _params(sem=None):
    return pltpu.CompilerParams(dimension_semantics=sem, vmem_limit_bytes=VMEM_LIMIT_V7X)


def _dot(a, b, dims=None):
    if dims is None:
        return jnp.dot(a, b, preferred_element_type=F32)
    return lax.dot_general(a, b, dims, preferred_element_type=F32)


def _rope_tables():
    inv_freq = ROPE_THETA ** (-(2.0 * jnp.arange(ROPE_HALF, dtype=F32)) / ROPE_DIM)
    ang = jnp.arange(S, dtype=jnp.int32).astype(F32)[:, None] * inv_freq[None, :]
    cos, sin = jnp.cos(ang), jnp.sin(ang)
    rest = DH - ROPE_DIM
    c = jnp.concatenate([cos, cos, jnp.ones((S, rest), F32)], axis=1)
    a = jnp.concatenate([-sin, jnp.zeros((S, DH - ROPE_HALF), F32)], axis=1)
    b = jnp.concatenate([jnp.zeros((S, ROPE_HALF), F32), sin, jnp.zeros((S, rest), F32)], axis=1)
    return c, a, b


def _rope(t, c, a, b):
    return t * c + pltpu.roll(t, DH - ROPE_HALF, 1) * a + pltpu.roll(t, ROPE_HALF, 1) * b


def _rope_t(g, c, a, b):
    return g * c - pltpu.roll(g, DH - ROPE_HALF, 1) * a - pltpu.roll(g, ROPE_HALF, 1) * b


def _prep_x(x):
    tm = 512

    def body(x_ref, xb_ref, xt_ref):
        xv = x_ref[...]
        xb_ref[...] = xv.astype(BF16)
        xt_ref[...] = xv.T.astype(BF16)

    return pl.pallas_call(
        body, name="prep_x", grid=(S // tm,),
        in_specs=[pl.BlockSpec((tm, D), lambda i: (i, 0))],
        out_specs=[pl.BlockSpec((tm, D), lambda i: (i, 0)), pl.BlockSpec((D, tm), lambda i: (0, i))],
        out_shape=[SDS((S, D), BF16), SDS((D, S), BF16)],
        compiler_params=_params(("arbitrary",)),
    )(x)


def _residues(slab_ref, idx, d, r, n):
    return slab_ref[(*idx, pl.ds(r, n, stride=d), slice(None))]


def _proj(xb, wg, col0, ncol, out_dtype, name, rope=None, by_residue=False, rider=None):
    tm, tn = 1024, 256
    cb0 = col0 // tn
    per_blk = WB // tn
    n_heads = tn // DH
    dils = DILATIONS[1:] if by_residue else ()

    def body(x_ref, w_ref, *rest):
        tabs = [t[...] for t in rest[:3]] if rope is not None else None
        rest = rest[3:] if rope is not None else rest
        o_ref = rest[0]
        acc = _dot(x_ref[...], w_ref[...])
        if not by_residue and rope is None:
            o_ref[...] = acc.astype(out_dtype)
            return
        slab_ref = rest[-1] if by_residue else None
        for hh in range(n_heads):
            hs = slice(hh * DH, (hh + 1) * DH)
            t = acc[:, hs] if rope is None else _rope(acc[:, hs], *tabs)
            o_ref[:, hs] = t.astype(out_dtype)
            if by_residue:
                slab_ref[hh] = t
        for n, d in enumerate(dils):
            for hh in range(n_heads):
                hs = slice(hh * DH, (hh + 1) * DH)
                for r in range(d):
                    rest[1 + n][r, :, hs] = _residues(slab_ref, (hh,), d, r, tm // d).astype(out_dtype)

    in_specs = [pl.BlockSpec((tm, D), lambda i, j: (i, 0)),
                pl.BlockSpec((None, D, tn), lambda i, j: ((cb0 + j) // per_blk, 0, (cb0 + j) % per_blk))]
    args = [xb, wg]
    if rope is not None:
        in_specs += [pl.BlockSpec((tm, DH), lambda i, j: (i, 0))] * 3
        args += list(rope)
    out_specs = [pl.BlockSpec((tm, tn), lambda i, j: (i, j))]
    out_shape = [SDS((S, ncol), out_dtype)]
    for d in dils:
        out_specs.append(pl.BlockSpec((d, tm // d, tn), lambda i, j: (0, i, j)))
        out_shape.append(SDS((d, S // d, ncol), out_dtype))
    outs, carried = _call_carrying(
        body, rider, (S // tm, ncol // tn), in_specs, out_specs, out_shape,
        [pltpu.VMEM((n_heads, tm, DH), F32)] if by_residue else [], args, name)
    outs = outs if by_residue else outs[0]
    return outs if rider is None else (outs, carried)


BQ = 2 * KB
LANES = 128


def _to_lane(acc, col, h):
    lane = lax.broadcasted_iota(jnp.int32, acc.shape, 1)
    return jnp.where(lane == h, col, acc)


def _attn_fwd(qk, v, name):
    d, n_sub, _ = qk.shape
    n_i = n_sub // BQ
    kw = KB + BQ

    def prev(i):
        return jnp.maximum(2 * i - 1, 0)

    def body(q_ref, kc_ref, kp_ref, vc_ref, vp_ref, o_ref, l_ref, kw_ref, vw_ref):
        i = pl.program_id(1)
        kw_ref[0:KB, :] = kp_ref[...]
        kw_ref[KB:kw, :] = kc_ref[...]
        vw_ref[0:KB, :] = vp_ref[...]
        vw_ref[KB:kw, :] = vc_ref[...]
        a = lax.broadcasted_iota(jnp.int32, (BQ, kw), 0)
        b = lax.broadcasted_iota(jnp.int32, (BQ, kw), 1)
        first_key = jnp.where(i == 0, KB, 0)
        mask = (b >= jnp.maximum(a, first_key)) & (b <= a + KB)
        lse = jnp.zeros((BQ, LANES), F32)
        for h in range(N_HEADS):
            hs = slice(h * DH, (h + 1) * DH)
            s = jnp.where(mask, _dot(q_ref[:, hs], kw_ref[:, hs], NT) * SCALE, NEG)
            m = jnp.max(s, axis=1, keepdims=True)
            p = jnp.exp(s - m)
            den = jnp.sum(p, axis=1, keepdims=True)
            o_ref[:, hs] = _dot(p.astype(BF16), vw_ref[:, hs]) / den
            lse = _to_lane(lse, m + jnp.log(den), h)
        l_ref[...] = lse

    return pl.pallas_call(
        body, name=name, grid=(d, n_i),
        in_specs=[pl.BlockSpec((None, BQ, DA), lambda r, i: (r, i, 0)),
                  pl.BlockSpec((None, BQ, DA), lambda r, i: (r, i, 1)),
                  pl.BlockSpec((None, KB, DA), lambda r, i: (r, prev(i), 1)),
                  pl.BlockSpec((None, BQ, DA), lambda r, i: (r, i, 0)),
                  pl.BlockSpec((None, KB, DA), lambda r, i: (r, prev(i), 0))],
        out_specs=[pl.BlockSpec((None, BQ, DA), lambda r, i: (r, i, 0)),
                   pl.BlockSpec((None, BQ, LANES), lambda r, i: (r, i, 0))],
        out_shape=[SDS((d, n_sub, DA), F32), SDS((d, n_sub, LANES), F32)],
        scratch_shapes=[pltpu.VMEM((kw, DA), BF16), pltpu.VMEM((kw, DA), BF16)],
        compiler_params=_params(("arbitrary", "arbitrary")),
    )(qk, qk, qk, v, v)


def _pooled(ext_ref, g, rows, tm):
    w = POOL_WINDOWS[g]
    cs = slice(g * GC, (g + 1) * GC)
    cur = ext_ref[HALO:HALO + tm, cs]
    win = cur
    for j in range(1, w):
        win = win + ext_ref[HALO - j:HALO - j + tm, cs]
    cnt = jnp.minimum(rows + 1, w).astype(F32)
    return win / cnt - cur, cnt


def _fill_ext(ext_ref, u_ref, uh_ref, blk, tm):
    @pl.when(blk == 0)
    def _():
        ext_ref[0:HALO, :] = jnp.zeros((HALO, DP), F32)

    @pl.when(blk > 0)
    def _():
        ext_ref[0:HALO, :] = uh_ref[...]

    ext_ref[HALO:HALO + tm, :] = u_ref[...]


def _residue_specs(tm, width):
    return [pl.BlockSpec((d, tm // d, width), lambda i: (0, i, 0)) for d in DILATIONS[1:]]


def _mix(o1, l1, o4, l4, o16, l16, u, gate, wp, scale):
    tm = 256
    n_slab = N_HEADS + 1

    def body(o1r, l1r, o4r, l4r, o16r, l16r, u_ref, uh_ref, g_ref, wp_ref, sc_ref,
             y_ref, yt_ref, attn_ref, lse_ref, lse4_ref, lse16_ref, ext_ref, ys_ref, nat_ref, ls_ref):
        i = pl.program_id(0)
        for n, (d, o_r, l_r) in enumerate(((DILATIONS[1], o4r, l4r), (DILATIONS[2], o16r, l16r))):
            for r in range(d):
                rows = pl.ds(r, tm // d, stride=d)
                for h in range(N_HEADS):
                    nat_ref[n, h, rows, :] = o_r[r, :, h * DH:(h + 1) * DH]
                nat_ref[n, N_HEADS, rows, :] = l_r[r]
        la, lb, lc = l1r[...], nat_ref[0, N_HEADS], nat_ref[1, N_HEADS]
        mx = jnp.maximum(jnp.maximum(la, lb), lc)
        ea, eb, ec = jnp.exp(la - mx), jnp.exp(lb - mx), jnp.exp(lc - mx)
        z = ea + eb + ec
        wa, wb, wc = ea / z, eb / z, ec / z
        lse = mx + jnp.log(z)
        lse_ref[...] = lse
        ls_ref[...] = lse
        for d, out in ((DILATIONS[1], lse4_ref), (DILATIONS[2], lse16_ref)):
            for r in range(d):
                out[r] = ls_ref[pl.ds(r, tm // d, stride=d), :]
        for h in range(N_HEADS):
            hs = slice(h * DH, (h + 1) * DH)
            hc = slice(h, h + 1)
            attn = wa[:, hc] * o1r[:, hs] + wb[:, hc] * nat_ref[0, h] + wc[:, hc] * nat_ref[1, h]
            attn_ref[:, hs] = attn
            gt = g_ref[:, hs]
            ys_ref[:, hs] = attn * (gt * jax.nn.sigmoid(gt))

        _fill_ext(ext_ref, u_ref, uh_ref, i, tm)
        rows = i * tm + lax.broadcasted_iota(jnp.int32, (tm, 1), 0)
        for g in range(N_GROUPS):
            cs = slice(g * GC, (g + 1) * GC)
            gs = slice(DA + g * GC, DA + (g + 1) * GC)
            pooled, _ = _pooled(ext_ref, g, rows, tm)
            po = _dot(pooled.astype(BF16), wp_ref[g]) * sc_ref[:, cs]
            gt = g_ref[:, gs]
            ys_ref[:, gs] = po * (gt * jax.nn.sigmoid(gt))
        yv = ys_ref[...]
        y_ref[...] = yv.astype(BF16)
        yt_ref[...] = yv.T.astype(BF16)

    row = lambda i: (i, 0)
    blk = pl.BlockSpec((tm, DA), row)
    lanes = pl.BlockSpec((tm, LANES), row)
    o_res, l_res = _residue_specs(tm, DA), _residue_specs(tm, LANES)
    return pl.pallas_call(
        body, name="mix", grid=(S // tm,),
        in_specs=[blk, lanes, o_res[0], l_res[0], o_res[1], l_res[1],
                  pl.BlockSpec((tm, DP), row),
                  pl.BlockSpec((HALO, DP), lambda i: (jnp.maximum(i * (tm // HALO) - 1, 0), 0)),
                  pl.BlockSpec((tm, D), row),
                  pl.BlockSpec((N_GROUPS, GC, GC), lambda i: (0, 0, 0)),
                  pl.BlockSpec((1, DP), lambda i: (0, 0))],
        out_specs=[pl.BlockSpec((tm, D), row), pl.BlockSpec((D, tm), lambda i: (0, i)), blk, lanes] + l_res,
        out_shape=[SDS((S, D), BF16), SDS((D, S), BF16), SDS((S, DA), F32), SDS((S, LANES), F32)]
        + [SDS((d, S // d, LANES), F32) for d in DILATIONS[1:]],
        scratch_shapes=[pltpu.VMEM((HALO + tm, DP), F32), pltpu.VMEM((tm, D), F32),
                        pltpu.VMEM((2, n_slab, tm, DH), F32), pltpu.VMEM((tm, LANES), F32)],
        compiler_params=_params(("arbitrary",)),
    )(o1, l1, o4, l4, o16, l16, u, u, gate, wp, scale)


def _outproj_ln(y, wout, x, target, gain, bias):
    tm = 256

    def body(y_ref, w_ref, x_ref, t_ref, g_ref, b_ref, dz_ref, dzb_ref, st_ref):
        i = pl.program_id(0)

        @pl.when(i == 0)
        def _():
            st_ref[...] = jnp.zeros((8, D), F32)

        z = ALPHA * x_ref[...] + _dot(y_ref[...], w_ref[...])
        mu = jnp.mean(z, axis=1, keepdims=True)
        zc = z - mu
        var = jnp.mean(zc * zc, axis=1, keepdims=True)
        rstd = lax.rsqrt(var + LN_EPS)
        xhat = zc * rstd
        gn = g_ref[...]
        diff = xhat * gn + b_ref[...] - t_ref[...]
        dyln = diff / D
        st_ref[0:1, :] += jnp.sum(dyln * xhat, axis=0, keepdims=True)
        st_ref[1:2, :] += jnp.sum(dyln, axis=0, keepdims=True)
        row_loss = jnp.sum(diff * diff, axis=1, keepdims=True) / D
        st_ref[2:3, :] += jnp.broadcast_to(0.5 * jnp.sum(row_loss, axis=0, keepdims=True), (1, D))
        dxh = dyln * gn
        m1 = jnp.mean(dxh, axis=1, keepdims=True)
        m2 = jnp.mean(dxh * xhat, axis=1, keepdims=True)
        dz = rstd * (dxh - m1 - xhat * m2)
        dz_ref[...] = dz
        dzb_ref[...] = dz.astype(BF16)

    row = lambda i: (i, 0)
    const = lambda i: (0, 0)
    return pl.pallas_call(
        body, name="outproj_ln", grid=(S // tm,),
        in_specs=[pl.BlockSpec((tm, D), row),
                  pl.BlockSpec((D, D), const, pipeline_mode=pl.Buffered(1)),
                  pl.BlockSpec((tm, D), row), pl.BlockSpec((tm, D), row),
                  pl.BlockSpec((1, D), const), pl.BlockSpec((1, D), const)],
        out_specs=[pl.BlockSpec((tm, D), row), pl.BlockSpec((tm, D), row), pl.BlockSpec((8, D), const)],
        out_shape=[SDS((S, D), F32), SDS((S, D), BF16), SDS((8, D), F32)],
        compiler_params=_params(("arbitrary",)),
    )(y, wout, x, target, gain, bias)


def _bwd_mid(dzb, wout, gate, attn, u, wp, scale, rider=None):
    tm = 256
    n = S // tm

    def body(dz_ref, w_ref, g_ref, at_ref, u_ref, uh_ref, wp_ref, sc_ref,
             dh_ref, do_ref, do4_ref, do16_ref, dd_ref, dd4_ref, dd16_ref, dwp_ref, dwp16_ref, gps_ref,
             ext_ref, eext_ref, acc_ref, nat_ref, ds_ref):
        i = pl.program_id(0)
        ib = n - 1 - i

        @pl.when(i == 0)
        def _():
            eext_ref[tm:tm + HALO, :] = jnp.zeros((HALO, DP), F32)
            acc_ref[...] = jnp.zeros((N_GROUPS, GC, GC), F32)
            gps_ref[...] = jnp.zeros((8, DP), F32)

        dy = _dot(dz_ref[...], w_ref[...], NT)
        gt = g_ref[...]
        sg = jax.nn.sigmoid(gt)
        silu = gt * sg
        dsilu = sg * (1.0 + gt * (1.0 - sg))
        dmix = dy * silu
        dyg = dy * dsilu

        at = at_ref[...]
        dat = dmix[:, 0:DA]
        do_ref[...] = dat.astype(BF16)
        dh_ref[:, DP:DP + DA] = (dyg[:, 0:DA] * at).astype(BF16)
        prod = dat * at
        dd = jnp.zeros((tm, LANES), F32)
        for h in range(N_HEADS):
            hs = slice(h * DH, (h + 1) * DH)
            dd = _to_lane(dd, jnp.sum(prod[:, hs], axis=1, keepdims=True), h)
            nat_ref[h] = dat[:, hs]
        dd_ref[...] = dd
        ds_ref[...] = dd
        for d, do_out, dd_out in ((DILATIONS[1], do4_ref, dd4_ref), (DILATIONS[2], do16_ref, dd16_ref)):
            for r in range(d):
                dd_out[r] = _residues(ds_ref, (), d, r, tm // d)
                for h in range(N_HEADS):
                    do_out[r, :, h * DH:(h + 1) * DH] = _residues(nat_ref, (h,), d, r, tm // d).astype(BF16)

        _fill_ext(ext_ref, u_ref, uh_ref, ib, tm)
        rows = ib * tm + lax.broadcasted_iota(jnp.int32, (tm, 1), 0)
        for g in range(N_GROUPS):
            w = POOL_WINDOWS[g]
            cs = slice(g * GC, (g + 1) * GC)
            gs = slice(DA + g * GC, DA + (g + 1) * GC)
            pooled, cnt = _pooled(ext_ref, g, rows, tm)
            pre = _dot(pooled.astype(BF16), wp_ref[g])
            sc = sc_ref[:, cs]
            dpo = dmix[:, gs]
            gps_ref[0:1, cs] += jnp.sum(dpo * pre, axis=0, keepdims=True)
            dh_ref[:, DP + DA + g * GC:DP + DA + (g + 1) * GC] = (dyg[:, gs] * (pre * sc)).astype(BF16)
            dpre = (dpo * sc).astype(BF16)
            acc_ref[g] += _dot(pooled.T.astype(BF16), dpre)
            dpooled = _dot(dpre, wp_ref[g], NT)
            eext_ref[0:tm, cs] = dpooled / cnt
            du = eext_ref[0:tm, cs]
            for j in range(1, w):
                du = du + eext_ref[j:j + tm, cs]
            dh_ref[:, cs] = (du - dpooled).astype(BF16)
        eext_ref[tm:tm + HALO, :] = eext_ref[0:HALO, :]

        @pl.when(i == n - 1)
        def _():
            for j in range(N_DEV):
                for g in range(N_GROUPS):
                    blk = acc_ref[g, j * PB:(j + 1) * PB, :]
                    dwp_ref[j, g * PB:(g + 1) * PB, :] = blk
                    dwp16_ref[j, g * PB:(g + 1) * PB, :] = blk.astype(BF16)

    rev = lambda i: (n - 1 - i, 0)
    const = lambda i: (0, 0)
    res = lambda width: [pl.BlockSpec((d, tm // d, width), lambda i: (0, n - 1 - i, 0)) for d in DILATIONS[1:]]
    pool_blocks = pl.BlockSpec((N_DEV, N_GROUPS * PB, GC), lambda i: (0, 0, 0))
    outs, carried = _call_carrying(
        body, rider, (n,),
        [pl.BlockSpec((tm, D), rev),
         pl.BlockSpec((D, D), const, pipeline_mode=pl.Buffered(1)),
         pl.BlockSpec((tm, D), rev), pl.BlockSpec((tm, DA), rev), pl.BlockSpec((tm, DP), rev),
         pl.BlockSpec((HALO, DP), lambda i: (jnp.maximum((n - 1 - i) * (tm // HALO) - 1, 0), 0)),
         pl.BlockSpec((N_GROUPS, GC, GC), lambda i: (0, 0, 0)),
         pl.BlockSpec((1, DP), const)],
        [pl.BlockSpec((tm, DP + D), rev), pl.BlockSpec((tm, DA), rev)] + res(DA)
        + [pl.BlockSpec((tm, LANES), rev)] + res(LANES) + [pool_blocks, pool_blocks, pl.BlockSpec((8, DP), const)],
        [SDS((S, DP + D), BF16), SDS((S, DA), BF16)] + [SDS((d, S // d, DA), BF16) for d in DILATIONS[1:]]
        + [SDS((S, LANES), F32)] + [SDS((d, S // d, LANES), F32) for d in DILATIONS[1:]]
        + [SDS((N_DEV, N_GROUPS * PB, GC), F32), SDS((N_DEV, N_GROUPS * PB, GC), BF16), SDS((8, DP), F32)],
        [pltpu.VMEM((HALO + tm, DP), F32), pltpu.VMEM((tm + HALO, DP), F32),
         pltpu.VMEM((N_GROUPS, GC, GC), F32), pltpu.VMEM((N_HEADS, tm, DH), F32), pltpu.VMEM((tm, LANES), F32)],
        [dzb, wout, gate, attn, u, u, wp, scale], "bwd_mid")
    return outs, carried


def _attn_bwd(qk, v, do, lse, dd, rope, name, others=None):
    d, n_sub, _ = qk.shape
    n_i = n_sub // BQ
    n_kb = n_sub // KB
    qw = BQ + KB
    final = others is not None
    out_dtype = BF16 if final else F32
    n_cb = 3 * DA // DH

    def nxt(i):
        return jnp.minimum(2 * i + 2, n_kb - 1)

    def body(qc_ref, qn_ref, kc_ref, vc_ref, doc_ref, don_ref, lc_ref, ln_ref, dc_ref, dn_ref,
             c_ref, a_ref, b_ref, *rest):
        if final:
            acc4_ref, acc16_ref, out_ref, carry_ref, qw_ref, dow_ref, lw_ref, dw_ref, nat_ref = rest
        else:
            out_ref, carry_ref, qw_ref, dow_ref, lw_ref, dw_ref = rest
        i = pl.program_id(1)

        @pl.when(i == 0)
        def _():
            carry_ref[...] = jnp.zeros((KB, DA), F32)

        for win, own, after in ((qw_ref, qc_ref, qn_ref), (dow_ref, doc_ref, don_ref), (lw_ref, lc_ref, ln_ref),
                                (dw_ref, dc_ref, dn_ref)):
            win[0:BQ, :] = own[...]
            win[BQ:qw, :] = after[...]
        if final:
            for n, (dil, acc) in enumerate(((DILATIONS[1], acc4_ref), (DILATIONS[2], acc16_ref))):
                for r in range(dil):
                    for cb in range(n_cb):
                        nat_ref[n, cb, pl.ds(r, BQ // dil, stride=dil), :] = acc[r, :, cb * DH:(cb + 1) * DH]

        a = lax.broadcasted_iota(jnp.int32, (qw, BQ), 0)
        b = lax.broadcasted_iota(jnp.int32, (qw, BQ), 1)
        n_q = jnp.where(i == n_i - 1, BQ, qw)
        mask = (b <= a) & (a <= b + KB) & (a < n_q)
        tabs = (c_ref[...], a_ref[...], b_ref[...])
        for h in range(N_HEADS):
            hs = slice(h * DH, (h + 1) * DH)
            hc = slice(h, h + 1)
            q, k, vv, dob = qw_ref[:, hs], kc_ref[:, hs], vc_ref[:, hs], dow_ref[:, hs]
            s = _dot(q, k, NT) * SCALE
            p = jnp.exp(jnp.where(mask, s - lw_ref[:, hc], NEG))
            dp = _dot(dob, vv, NT)
            ds = (p * (dp - dw_ref[:, hc]) * SCALE).astype(BF16)
            dq = _dot(ds, k)
            dk = _dot(ds, q, TN)
            dv = _dot(p.astype(BF16), dob, TN)
            dq_lo = dq[0:KB] + carry_ref[:, hs]
            carry_ref[:, hs] = dq[BQ:qw]
            dq_own = _rope_t(jnp.concatenate([dq_lo, dq[KB:BQ]], axis=0), *tabs)
            for base, gv in ((0, dq_own), (DA, _rope_t(dk, *tabs)), (2 * DA, dv)):
                if final:
                    cb = base // DH + h
                    gv = gv + nat_ref[0, cb] + nat_ref[1, cb]
                out_ref[:, base + h * DH:base + (h + 1) * DH] = gv.astype(out_dtype)

    def cur(rows, width, col=0):
        return pl.BlockSpec((None, rows, width), lambda r, i: (r, i, col))

    def nx(width, col=0):
        return pl.BlockSpec((None, KB, width), lambda r, i: (r, nxt(i), col))

    in_specs = [cur(BQ, DA), nx(DA), cur(BQ, DA, 1), cur(BQ, DA), cur(BQ, DA), nx(DA),
                cur(BQ, LANES), nx(LANES), cur(BQ, LANES), nx(LANES)] + [cur(BQ, DH)] * 3
    args = [qk, qk, qk, v, do, do, lse, lse, dd, dd] + list(rope)
    scratch = [pltpu.VMEM((KB, DA), F32), pltpu.VMEM((qw, DA), BF16), pltpu.VMEM((qw, DA), BF16),
               pltpu.VMEM((qw, LANES), F32), pltpu.VMEM((qw, LANES), F32)]
    if final:
        assert d == 1
        in_specs += [pl.BlockSpec((dil, BQ // dil, 3 * DA), lambda r, i: (0, i, 0)) for dil in DILATIONS[1:]]
        args += list(others)
        scratch.append(pltpu.VMEM((2, n_cb, BQ, DH), F32))
    return pl.pallas_call(
        body, name=name, grid=(d, n_i),
        in_specs=in_specs,
        out_specs=cur(BQ, 3 * DA),
        out_shape=SDS((d, n_sub, 3 * DA), out_dtype),
        scratch_shapes=scratch,
        compiler_params=_params(("arbitrary", "arbitrary")),
    )(*args)


def _grad_w_out(yt, dzb):
    tk = 512

    def body(yt_ref, dz_ref, o_ref, o16_ref, acc_ref, b16_ref):
        k = pl.program_id(0)

        @pl.when(k == 0)
        def _():
            acc_ref[...] = jnp.zeros((D, D), F32)

        acc_ref[...] += _dot(yt_ref[...], dz_ref[...])

        @pl.when(k == S // tk - 1)
        def _():
            b16_ref[...] = acc_ref[...].astype(BF16)
            pltpu.sync_copy(acc_ref, o_ref)
            pltpu.sync_copy(b16_ref, o16_ref)

    hbm = pl.BlockSpec(memory_space=pl.ANY)
    o, o16 = pl.pallas_call(
        body, name="grad_w_out", grid=(S // tk,),
        in_specs=[pl.BlockSpec((D, tk), lambda k: (0, k)), pl.BlockSpec((tk, D), lambda k: (k, 0))],
        out_specs=[hbm, hbm],
        out_shape=[SDS((D, D), F32), SDS((D, D), BF16)],
        scratch_shapes=[pltpu.VMEM((D, D), F32), pltpu.VMEM((D, D), BF16)],
        compiler_params=_params(("arbitrary",)),
    )(yt, dzb)
    return o.reshape(N_DEV, OB, D), o16.reshape(N_DEV, OB, D)


def _dh_specs(rows, which_block, row_of):
    half = N_DEV // 2

    def spec(lo):
        def index(*g):
            j = which_block(*g)
            mine = (j >= lo) & (j < lo + half)
            return (jnp.where(mine, row_of(*g), 0), jnp.where(mine, j - lo, 0))
        return pl.BlockSpec((rows, WB), index)

    return [spec(0), spec(half)]


def _grad_w_in(xt, dh_a, dh_b):
    tk = 512
    n_k = S // tk
    half = N_DEV // 2

    def body(xt_ref, da_ref, db_ref, o_ref, o16_ref, acc_ref):
        j, k = pl.program_id(0), pl.program_id(1)

        @pl.when(k == 0)
        def _():
            acc_ref[...] = jnp.zeros((D, WB), F32)

        @pl.when(j < half)
        def _():
            acc_ref[...] += _dot(xt_ref[...], da_ref[...])

        @pl.when(j >= half)
        def _():
            acc_ref[...] += _dot(xt_ref[...], db_ref[...])

        @pl.when(k == n_k - 1)
        def _():
            o_ref[...] = acc_ref[...]
            o16_ref[...] = acc_ref[...].astype(BF16)

    blk = lambda j, k: (j, 0, 0)
    return pl.pallas_call(
        body, name="grad_w_in", grid=(N_DEV, n_k),
        in_specs=[pl.BlockSpec((D, tk), lambda j, k: (0, k))] + _dh_specs(tk, lambda j, k: j, lambda j, k: k),
        out_specs=[pl.BlockSpec((None, D, WB), blk), pl.BlockSpec((None, D, WB), blk)],
        out_shape=[SDS((N_DEV, D, WB), F32), SDS((N_DEV, D, WB), BF16)],
        scratch_shapes=[pltpu.VMEM((D, WB), F32)],
        compiler_params=_params(("arbitrary", "arbitrary")),
    )(xt, dh_a, dh_b)


def _grad_x(dz, dh_a, dh_b, wg, rider=None):
    tm = 1024
    half = N_DEV // 2

    def body(dz_ref, da_ref, db_ref, w_ref, o_ref):
        j = pl.program_id(1)

        @pl.when(j == 0)
        def _():
            o_ref[...] = ALPHA * dz_ref[...]

        @pl.when(j < half)
        def _():
            o_ref[...] += _dot(da_ref[...], w_ref[...], NT)

        @pl.when(j >= half)
        def _():
            o_ref[...] += _dot(db_ref[...], w_ref[...], NT)

    outs, carried = _call_carrying(
        body, rider, (S // tm, N_DEV),
        [pl.BlockSpec((tm, D), lambda i, j: (i, 0))] + _dh_specs(tm, lambda i, j: j, lambda i, j: i)
        + [pl.BlockSpec((None, D, WB), lambda i, j: (j, 0, 0))],
        [pl.BlockSpec((tm, D), lambda i, j: (i, 0))], [SDS((S, D), F32)], [], [dz, dh_a, dh_b, wg], "grad_x")
    return outs[0], carried


def _place():
    x, y, c = lax.axis_index("x"), lax.axis_index("y"), lax.axis_index("c")
    chips = [(x, y), (1 - x, y), (x, 1 - y), (1 - x, 1 - y)]
    return x, y, c, chips


def _blk(x, y, c):
    return 4 * x + 2 * y + c


def _adamw(w, g, m, v):
    m = ADAM_B1 * m + (1.0 - ADAM_B1) * g
    v = ADAM_B2 * v + (1.0 - ADAM_B2) * (g * g)
    m_hat = m / (1.0 - ADAM_B1 ** ADAM_STEP)
    v_hat = v / (1.0 - ADAM_B2 ** ADAM_STEP)
    delta = -ADAM_LR * (m_hat / (jnp.sqrt(v_hat) + ADAM_EPS) + ADAM_WD * w)
    return delta, m, v


class _Rider:
    def __init__(self, args, in_specs, out_shape, out_specs, scratch, start, finish):
        self.args, self.in_specs, self.out_shape, self.out_specs = args, in_specs, out_shape, out_specs
        self.scratch, self.start, self.finish = scratch, start, finish


def _carry(body, rider, n_in, n_out, first, last):
    if rider is None:
        return body
    r_in, r_out, r_scr = len(rider.args), len(rider.out_shape), len(rider.scratch)

    def carrying(*refs):
        o0 = n_in + r_in
        s0 = o0 + n_out + r_out
        s1 = len(refs) - r_scr
        theirs = (refs[n_in:o0], refs[o0 + n_out:s0], refs[s1:])
        pl.when(first())(lambda: rider.start(*theirs))
        body(*refs[:n_in], *refs[o0:o0 + n_out], *refs[s0:s1])
        pl.when(last())(lambda: rider.finish(*theirs))

    return carrying


def _call_carrying(body, rider, grid, in_specs, out_specs, out_shape, scratch, args, name):
    n_in, n_out = len(in_specs), len(out_specs)
    ids = lambda: [pl.program_id(a) for a in range(len(grid))]
    first = lambda: functools.reduce(jnp.logical_and, [i == 0 for i in ids()])
    last = lambda: functools.reduce(jnp.logical_and, [i == n - 1 for i, n in zip(ids(), grid)])
    if rider is not None:
        in_specs, args = in_specs + rider.in_specs, list(args) + rider.args
        out_specs, out_shape = out_specs + rider.out_specs, out_shape + rider.out_shape
        scratch = scratch + rider.scratch
    outs = pl.pallas_call(
        _carry(body, rider, n_in, n_out, first, last), name=name, grid=grid,
        in_specs=in_specs, out_specs=out_specs, out_shape=out_shape, scratch_shapes=scratch,
        compiler_params=_params(("arbitrary",) * len(grid)),
    )(*args)
    return list(outs[:n_out]), list(outs[n_out:])


def _gather_copy(tensors, send_sems, recv_sems, t, k, block, to, src=None):
    dst = tensors[t][1](_blk(*block))
    return pltpu.make_async_remote_copy(
        src_ref=dst if src is None else src, dst_ref=dst,
        send_sem=send_sems.at[t, k], recv_sem=recv_sems.at[t, k], device_id=to, device_id_type=MESH)


def _gather_start(tensors, send_sems, recv_sems, local_sems):
    x, y, c, chips = _place()
    me, sib = (x, y, c), (x, y, 1 - c)
    for t, (src, dst) in enumerate(tensors):
        pltpu.make_async_copy(src, dst(_blk(*me)), local_sems.at[t]).start()
        _gather_copy(tensors, send_sems, recv_sems, t, 0, me, sib, src).start()
        for j in (1, 2, 3):
            _gather_copy(tensors, send_sems, recv_sems, t, j, me, (*chips[j], c), src).start()


def _gather_finish(tensors, send_sems, recv_sems, local_sems):
    x, y, c, chips = _place()
    me, sib = (x, y, c), (x, y, 1 - c)
    copy = functools.partial(_gather_copy, tensors, send_sems, recv_sems)
    for t in range(len(tensors)):
        for j in (1, 2, 3):
            copy(t, j, (*chips[j], c), me).wait_recv()
            copy(t, 3 + j, (*chips[j], c), sib).start()
    for t, (src, dst) in enumerate(tensors):
        copy(t, 0, sib, me).wait_recv()
        for j in (1, 2, 3):
            copy(t, 3 + j, (*chips[j], 1 - c), me).wait_recv()
        copy(t, 0, me, sib, src).wait_send()
        for j in (1, 2, 3):
            copy(t, j, me, (*chips[j], c), src).wait_send()
            copy(t, 3 + j, (*chips[j], c), sib).wait_send()
        pltpu.make_async_copy(src, dst(_blk(*me)), local_sems.at[t]).wait()


def _gather_w_in(w_in_s):
    def body(win_ref, gin_ref, bin_ref, send_sems, recv_sems, local_sems):
        bin_ref[...] = win_ref[...].astype(BF16)
        tensors = [(bin_ref, lambda b: gin_ref.at[b])]
        _gather_start(tensors, send_sems, recv_sems, local_sems)
        _gather_finish(tensors, send_sems, recv_sems, local_sems)

    return pl.pallas_call(
        body, name="gather_w_in",
        in_specs=[pl.BlockSpec(memory_space=pltpu.VMEM)], out_specs=pl.BlockSpec(memory_space=pl.ANY),
        out_shape=SDS((N_DEV, D, WB), BF16),
        scratch_shapes=[pltpu.VMEM((D, WB), BF16),
                        pltpu.SemaphoreType.DMA((1, 7)), pltpu.SemaphoreType.DMA((1, 7)), pltpu.SemaphoreType.DMA((1,))],
        compiler_params=_params(),
    )(w_in_s)


def _gather_small_rider(w_out_s, w_pool_s):
    def tensors(ins, outs, scr):
        gout_ref, gpool_ref = outs

        def pool_rows(b):
            return gpool_ref.at[:, pl.ds(pl.multiple_of(b * PB, PB), PB), :]

        return [(scr[0], lambda b: gout_ref.at[b]), (scr[1], pool_rows)]

    def start(ins, outs, scr):
        scr[0][...] = ins[0][...].astype(BF16)
        scr[1][...] = ins[1][...].astype(BF16)
        _gather_start(tensors(ins, outs, scr), *scr[2:])

    def finish(ins, outs, scr):
        _gather_finish(tensors(ins, outs, scr), *scr[2:])

    vmem = pl.BlockSpec(memory_space=pltpu.VMEM)
    hbm = pl.BlockSpec(memory_space=pl.ANY)
    return _Rider(
        args=[w_out_s, w_pool_s], in_specs=[vmem, vmem],
        out_shape=[SDS((N_DEV, OB, D), BF16), SDS((N_GROUPS, GC, GC), BF16)], out_specs=[hbm, hbm],
        scratch=[pltpu.VMEM((OB, D), BF16), pltpu.VMEM((N_GROUPS, PB, GC), BF16),
                 pltpu.SemaphoreType.DMA((2, 7)), pltpu.SemaphoreType.DMA((2, 7)), pltpu.SemaphoreType.DMA((2,))],
        start=start, finish=finish)


def _block_table():
    x, y, c, chips = _place()
    return jnp.stack([_blk(*chip, c) for chip in chips]).astype(jnp.int32)


def _rs_exchange(name, p16):
    _, r_tot, cols = p16.shape

    def body(p16_ref, r1_ref, send_sems, recv_sems):
        x, y, c, chips = _place()
        copies = [pltpu.make_async_remote_copy(
            src_ref=p16_ref.at[_blk(*chips[k], 1 - c)], dst_ref=r1_ref.at[k],
            send_sem=send_sems.at[k], recv_sem=recv_sems.at[k], device_id=(x, y, 1 - c), device_id_type=MESH)
            for k in range(4)]
        for cp in copies:
            cp.start()
        for cp in copies:
            cp.wait()

    hbm = pl.BlockSpec(memory_space=pl.ANY)
    return pl.pallas_call(
        body, name=name, in_specs=[hbm], out_specs=hbm, out_shape=SDS((4, r_tot, cols), BF16),
        scratch_shapes=[pltpu.SemaphoreType.DMA((4,)), pltpu.SemaphoreType.DMA((4,))],
        compiler_params=_params(),
    )(p16)


def _chip_sums(name, table, p32, r1, rc):
    _, r_tot, cols = p32.shape

    def body(tbl_ref, p_ref, r_ref, o_ref):
        o_ref[...] = (p_ref[...] + r_ref[...].astype(F32)).astype(BF16)

    return pl.pallas_call(
        body, name=name,
        grid_spec=pltpu.PrefetchScalarGridSpec(
            num_scalar_prefetch=1, grid=(3, r_tot // rc),
            in_specs=[pl.BlockSpec((None, rc, cols), lambda k, ch, tbl: (tbl[k + 1], ch, 0)),
                      pl.BlockSpec((None, rc, cols), lambda k, ch, tbl: (k + 1, ch, 0))],
            out_specs=pl.BlockSpec((None, rc, cols), lambda k, ch, tbl: (k, ch, 0))),
        out_shape=SDS((3, r_tot, cols), BF16),
        compiler_params=_params(("arbitrary", "arbitrary")),
    )(table, p32, r1)


def _stage2_rider(sums, stats=None):
    n_t = len(sums)

    def copies(ins, outs, scr):
        x, y, c, chips = _place()
        out = []
        for t in range(n_t):
            for k in (1, 2, 3):
                out.append(pltpu.make_async_remote_copy(
                    src_ref=ins[t].at[k - 1], dst_ref=outs[t].at[k - 1],
                    send_sem=scr[0].at[t, k - 1], recv_sem=scr[1].at[t, k - 1],
                    device_id=(*chips[k], c), device_id_type=MESH))
        if stats is not None:
            for k in range(1, N_DEV):
                peer = (x ^ ((k >> 2) & 1), y ^ ((k >> 1) & 1), c ^ (k & 1))
                out.append(pltpu.make_async_remote_copy(
                    src_ref=scr[4], dst_ref=outs[n_t].at[_blk(x, y, c)],
                    send_sem=scr[2].at[k - 1], recv_sem=scr[3].at[k - 1], device_id=peer, device_id_type=MESH))
        return out

    def own_rows(outs, scr):
        x, y, c, _ = _place()
        return pltpu.make_async_copy(scr[4], outs[n_t].at[_blk(x, y, c)], scr[5])

    def start(ins, outs, scr):
        if stats is not None:
            scr[4][...] = ins[n_t][...]
            scr[4][3:4, 0:DP] = ins[n_t + 1][0:1, :]
            own_rows(outs, scr).start()
        for cp in copies(ins, outs, scr):
            cp.start()

    def finish(ins, outs, scr):
        for cp in copies(ins, outs, scr):
            cp.wait()
        if stats is not None:
            own_rows(outs, scr).wait()

    vmem = pl.BlockSpec(memory_space=pltpu.VMEM)
    hbm = pl.BlockSpec(memory_space=pl.ANY)
    scratch = [pltpu.SemaphoreType.DMA((n_t, 3)), pltpu.SemaphoreType.DMA((n_t, 3))]
    args, in_specs = list(sums), [hbm] * n_t
    out_shape, out_specs = [SDS(s.shape, BF16) for s in sums], [hbm] * n_t
    if stats is not None:
        scratch += [pltpu.SemaphoreType.DMA((N_DEV - 1,)), pltpu.SemaphoreType.DMA((N_DEV - 1,)),
                    pltpu.VMEM((8, D), F32), pltpu.SemaphoreType.DMA(())]
        args, in_specs = args + list(stats), in_specs + [vmem, vmem]
        out_shape, out_specs = out_shape + [SDS((N_DEV, 8, D), F32)], out_specs + [hbm]
    return _Rider(args, in_specs, out_shape, out_specs, scratch, start, finish)


def _adamw_shard(name, table, p32, r1, r2, w, m, v, rc):
    _, r_tot, cols = p32.shape

    def body(tbl_ref, p_ref, r1_ref, r2_ref, w_ref, m_ref, v_ref, g_ref, d_ref, nm_ref, nv_ref):
        g = p_ref[...] + r1_ref[...].astype(F32)
        for k in range(3):
            g = g + r2_ref[k].astype(F32)
        delta, nm, nv = _adamw(w_ref[...], g, m_ref[...], v_ref[...])
        g_ref[...] = g
        d_ref[...] = delta
        nm_ref[...] = nm
        nv_ref[...] = nv

    rows = pl.BlockSpec((rc, cols), lambda ch, tbl: (ch, 0))
    shard = SDS((r_tot, cols), F32)
    return pl.pallas_call(
        body, name=name,
        grid_spec=pltpu.PrefetchScalarGridSpec(
            num_scalar_prefetch=1, grid=(r_tot // rc,),
            in_specs=[pl.BlockSpec((None, rc, cols), lambda ch, tbl: (tbl[0], ch, 0)),
                      pl.BlockSpec((None, rc, cols), lambda ch, tbl: (0, ch, 0)),
                      pl.BlockSpec((3, rc, cols), lambda ch, tbl: (0, ch, 0)), rows, rows, rows],
            out_specs=[rows, rows, rows, rows]),
        out_shape=[shard, shard, shard, shard],
        compiler_params=_params(("arbitrary",)),
    )(table, p32, r1, r2, w, m, v)


def _replicated_adamw(gathered, gain, bias, scale, m_gain, m_bias, m_scale, v_gain, v_bias, v_scale):
    def body(all_ref, g_ref, b_ref, s_ref, mg_ref, mb_ref, ms_ref, vg_ref, vb_ref, vs_ref,
             tot_ref, dl_ref, nm_ref, nv_ref):
        tot = all_ref[0]
        for b in range(1, N_DEV):
            tot = tot + all_ref[b]
        tot_ref[...] = tot
        dl_ref[...] = jnp.zeros((8, D), F32)
        nm_ref[...] = jnp.zeros((8, D), F32)
        nv_ref[...] = jnp.zeros((8, D), F32)
        for row, width, w_r, m_r, v_r in ((0, D, g_ref, mg_ref, vg_ref), (1, D, b_ref, mb_ref, vb_ref),
                                          (3, DP, s_ref, ms_ref, vs_ref)):
            delta, nm, nv = _adamw(w_r[...], tot[row:row + 1, 0:width], m_r[...], v_r[...])
            dl_ref[row:row + 1, 0:width] = delta
            nm_ref[row:row + 1, 0:width] = nm
            nv_ref[row:row + 1, 0:width] = nv

    vmem = pl.BlockSpec(memory_space=pltpu.VMEM)
    rows = SDS((8, D), F32)
    return pl.pallas_call(
        body, name="replicated_adamw",
        in_specs=[vmem] * 10, out_specs=[vmem] * 4, out_shape=[rows, rows, rows, rows],
        compiler_params=_params(),
    )(gathered, gain, bias, scale, m_gain, m_bias, m_scale, v_gain, v_bias, v_scale)


def kernel(x, w_in, w_pool, pool_scale, w_out, ln_gain, ln_bias, loss_target, m_w_in, m_w_pool, m_pool_scale, m_w_out, m_ln_gain, m_ln_bias, v_w_in, v_w_pool, v_pool_scale, v_w_out, v_ln_gain, v_ln_bias):
    pool_rows = (N_GROUPS * PB, GC)
    x2, target = x[0], loss_target[0]
    rope = _rope_tables()

    table = _block_table()
    wg_in = _gather_w_in(w_in[0])

    by_res = lambda t, d: t.reshape(S // d, d, DH).transpose(1, 0, 2)
    ropes = [[by_res(t, d) for t in rope] for d in DILATIONS]

    xb, xt = _prep_x(x2)
    qk, (wg_out, wg_pool) = _proj(xb, wg_in, 0, 2 * DA, BF16, "proj_qk", rope=rope, by_residue=True,
                                  rider=_gather_small_rider(w_out[0], w_pool[0]))
    wg_out = wg_out.reshape(D, D)
    v = _proj(xb, wg_in, 2 * DA, DA, BF16, "proj_v", by_residue=True)
    qk[0], v[0] = qk[0][None], v[0][None]
    u = _proj(xb, wg_in, 3 * DA, DP, F32, "proj_u")
    gate = _proj(xb, wg_in, 3 * DA + DP, D, F32, "proj_gate")
    fwd = [_attn_fwd(qk[n], v[n], f"attn_fwd_d{d}") for n, d in enumerate(DILATIONS)]
    y, yt, attn, lse1, lse4, lse16 = _mix(fwd[0][0][0], fwd[0][1][0], *fwd[1], *fwd[2], u, gate, wg_pool, pool_scale)
    dz, dzb, stats = _outproj_ln(y, wg_out, x2, target, ln_gain, ln_bias)

    dwout, dwout16 = _grad_w_out(yt, dzb)
    r1_out = _rs_exchange("rs_out_exchange", dwout16)
    s2_out = _chip_sums("rs_out_sums", table, dwout, r1_out, OB)
    mid, (r2_out,) = _bwd_mid(dzb, wg_out, gate, attn, u, wg_pool, pool_scale, rider=_stage2_rider([s2_out]))
    dh_ug, do1, do4, do16, dd1, dd4, dd16, dwp, dwp16, gps = mid
    g_out, d_out, nm_out, nv_out = _adamw_shard(
        "adamw_w_out", table, dwout, r1_out, r2_out, w_out[0], m_w_out[0], v_w_out[0], OB // 2)

    do, lse, dd = [do1[None], do4, do16], [lse1[None], lse4, lse16], [dd1[None], dd4, dd16]
    others = [_attn_bwd(qk[n], v[n], do[n], lse[n], dd[n], ropes[n], f"attn_bwd_d{DILATIONS[n]}") for n in (1, 2)]
    dh_qkv = _attn_bwd(qk[0], v[0], do[0], lse[0], dd[0], ropes[0], "attn_bwd_d1", others=others)[0]

    dwin, dwin16 = _grad_w_in(xt, dh_qkv, dh_ug)
    r1_in = _rs_exchange("rs_in_exchange", dwin16)
    s2_in = _chip_sums("rs_in_sums", table, dwin, r1_in, 512)
    r1_pool = _rs_exchange("rs_pool_exchange", dwp16)
    s2_pool = _chip_sums("rs_pool_sums", table, dwp, r1_pool, N_GROUPS * PB)
    grad_x, (r2_in, r2_pool, gathered) = _grad_x(
        dz, dh_qkv, dh_ug, wg_in, rider=_stage2_rider([s2_in, s2_pool], stats=(stats, gps)))
    g_in, d_in, nm_in, nv_in = _adamw_shard(
        "adamw_w_in", table, dwin, r1_in, r2_in, w_in[0], m_w_in[0], v_w_in[0], 256)
    g_pool, d_pool, nm_pool, nv_pool = _adamw_shard(
        "adamw_w_pool", table, dwp, r1_pool, r2_pool, w_pool[0].reshape(pool_rows), m_w_pool[0].reshape(pool_rows),
        v_w_pool[0].reshape(pool_rows), N_GROUPS * PB)
    tot, dl, nm, nv = _replicated_adamw(gathered, ln_gain, ln_bias, pool_scale, m_ln_gain, m_ln_bias, m_pool_scale,
                                        v_ln_gain, v_ln_bias, v_pool_scale)

    shard4 = lambda t: t.reshape(1, N_GROUPS, PB, GC)
    lead = lambda t: t[None]
    small = lambda t: (t[3:4, 0:DP], t[0:1], t[1:2])
    g_ps, g_gain, g_bias = small(tot)
    d_ps, d_gain, d_bias = small(dl)
    nm_ps, nm_gain, nm_bias = small(nm)
    nv_ps, nv_gain, nv_bias = small(nv)
    return (tot[2, 0], lead(grad_x),
            lead(g_in), shard4(g_pool), g_ps, lead(g_out), g_gain, g_bias,
            lead(d_in), shard4(d_pool), d_ps, lead(d_out), d_gain, d_bias,
            lead(nm_in), shard4(nm_pool), nm_ps, lead(nm_out), nm_gain, nm_bias,
            lead(nv_in), shard4(nv_pool), nv_ps, lead(nv_out), nv_gain, nv_bias)
```

```python
import functools

import jax
import jax.numpy as jnp
from jax import lax
from jax.experimental import pallas as pl
from jax.experimental.pallas import tpu as pltpu

F32 = jnp.float32
BF16 = jnp.bfloat16
SDS = jax.ShapeDtypeStruct
MESH = pl.DeviceIdType.MESH

N_DEV = 8
S = 4096
D = 2048
N_HEADS = 8
DH = 128
DA = N_HEADS * DH
DP = 1024
N_GROUPS = 4
GC = DP // N_GROUPS
POOL_WINDOWS = (2, 4, 8, 16)
HALO = 16
D_IN = 3 * DA + DP + D
WB = D_IN // N_DEV
OB = D // N_DEV
PB = GC // N_DEV
ROPE_DIM = DH // 4
ROPE_HALF = ROPE_DIM // 2
ROPE_THETA = 500000.0
DILATIONS = (1, 4, 16)
KB = 128
LN_EPS = 1e-5
ALPHA = 2.0 ** 0.25
SCALE = DH ** -0.5
NEG = -1e30
ADAM_LR, ADAM_B1, ADAM_B2, ADAM_EPS, ADAM_WD, ADAM_STEP = 0.001, 0.9, 0.999, 1e-08, 0.01, 10

VMEM_LIMIT_V7X = 60 * 1024 * 1024

NT = (((1,), (1,)), ((), ()))
TN = (((0,), (0,)), ((), ()))


def _params(sem=None):
    return pltpu.CompilerParams(dimension_semantics=sem, vmem_limit_bytes=VMEM_LIMIT_V7X)


def _dot(a, b, dims=None):
    if dims is None:
        return jnp.dot(a, b, preferred_element_type=F32)
    return lax.dot_general(a, b, dims, preferred_element_type=F32)


def _rope_tables():
    inv_freq = ROPE_THETA ** (-(2.0 * jnp.arange(ROPE_HALF, dtype=F32)) / ROPE_DIM)
    ang = jnp.arange(S, dtype=jnp.int32).astype(F32)[:, None] * inv_freq[None, :]
    cos, sin = jnp.cos(ang), jnp.sin(ang)
    rest = DH - ROPE_DIM
    c = jnp.concatenate([cos, cos, jnp.ones((S, rest), F32)], axis=1)
    a = jnp.concatenate([-sin, jnp.zeros((S, DH - ROPE_HALF), F32)], axis=1)
    b = jnp.concatenate([jnp.zeros((S, ROPE_HALF), F32), sin, jnp.zeros((S, rest), F32)], axis=1)
    return c, a, b


def _rope(t, c, a, b):
    return t * c + pltpu.roll(t, DH - ROPE_HALF, 1) * a + pltpu.roll(t, ROPE_HALF, 1) * b


def _rope_t(g, c, a, b):
    return g * c - pltpu.roll(g, DH - ROPE_HALF, 1) * a - pltpu.roll(g, ROPE_HALF, 1) * b


def _prep_x(x):
    tm = 512

    def body(x_ref, xb_ref, xt_ref):
        xv = x_ref[...]
        xb_ref[...] = xv.astype(BF16)
        xt_ref[...] = xv.T.astype(BF16)

    return pl.pallas_call(
        body, name="prep_x", grid=(S // tm,),
        in_specs=[pl.BlockSpec((tm, D), lambda i: (i, 0))],
        out_specs=[pl.BlockSpec((tm, D), lambda i: (i, 0)), pl.BlockSpec((D, tm), lambda i: (0, i))],
        out_shape=[SDS((S, D), BF16), SDS((D, S), BF16)],
        compiler_params=_params(("arbitrary",)),
    )(x)


def _residues(slab_ref, idx, d, r, n):
    return slab_ref[(*idx, pl.ds(r, n, stride=d), slice(None))]


def _proj(xb, wg, col0, ncol, out_dtype, name, rope=None, by_residue=False, rider=None):
    tm, tn = S, 256
    cb0 = col0 // tn
    per_blk = WB // tn
    n_heads = tn // DH
    dils = DILATIONS[1:] if by_residue else ()
    once = pl.Buffered(1)

    def body(x_ref, w_ref, *rest):
        tabs = [t[...] for t in rest[:3]] if rope is not None else None
        rest = rest[3:] if rope is not None else rest
        o_ref = rest[0]
        acc = _dot(x_ref[...], w_ref[...])
        if not by_residue and rope is None:
            o_ref[...] = acc.astype(out_dtype)
            return
        slab_ref = rest[-1] if by_residue else None
        for hh in range(n_heads):
            hs = slice(hh * DH, (hh + 1) * DH)
            t = acc[:, hs] if rope is None else _rope(acc[:, hs], *tabs)
            o_ref[:, hs] = t.astype(out_dtype)
            if by_residue:
                slab_ref[hh] = t
        for n, d in enumerate(dils):
            for hh in range(n_heads):
                hs = slice(hh * DH, (hh + 1) * DH)
                for r in range(d):
                    rest[1 + n][r, :, hs] = _residues(slab_ref, (hh,), d, r, tm // d).astype(out_dtype)

    in_specs = [pl.BlockSpec((tm, D), lambda j: (0, 0), pipeline_mode=once),
                pl.BlockSpec((None, D, tn), lambda j: ((cb0 + j) // per_blk, 0, (cb0 + j) % per_blk))]
    args = [xb, wg]
    if rope is not None:
        in_specs += [pl.BlockSpec((tm, DH), lambda j: (0, 0), pipeline_mode=once)] * 3
        args += list(rope)
    out_specs = [pl.BlockSpec((tm, tn), lambda j: (0, j))]
    out_shape = [SDS((S, ncol), out_dtype)]
    for d in dils:
        out_specs.append(pl.BlockSpec((d, tm // d, tn), lambda j: (0, 0, j)))
        out_shape.append(SDS((d, S // d, ncol), out_dtype))
    outs, carried = _call_carrying(
        body, rider, (ncol // tn,), in_specs, out_specs, out_shape,
        [pltpu.VMEM((n_heads, tm, DH), F32)] if by_residue else [], args, name)
    outs = outs if by_residue else outs[0]
    return outs if rider is None else (outs, carried)


BQ = 2 * KB
LANES = 128


def _to_lane(acc, col, h):
    lane = lax.broadcasted_iota(jnp.int32, acc.shape, 1)
    return jnp.where(lane == h, col, acc)


def _attn_fwd(qk, v, name):
    d, n_sub, _ = qk.shape
    n_i = n_sub // BQ
    kw = KB + BQ

    def prev(i):
        return jnp.maximum(2 * i - 1, 0)

    def body(q_ref, kc_ref, kp_ref, vc_ref, vp_ref, o_ref, l_ref, kw_ref, vw_ref):
        i = pl.program_id(1)
        kw_ref[0:KB, :] = kp_ref[...]
        kw_ref[KB:kw, :] = kc_ref[...]
        vw_ref[0:KB, :] = vp_ref[...]
        vw_ref[KB:kw, :] = vc_ref[...]
        a = lax.broadcasted_iota(jnp.int32, (KB, 2 * KB), 0)
        b = lax.broadcasted_iota(jnp.int32, (KB, 2 * KB), 1)
        band = (b >= a) & (b <= a + KB)
        first_key = jnp.where(i == 0, KB, 0)
        masks = (band & (b >= first_key), band)
        for half in range(2):
            rs = slice(half * KB, (half + 1) * KB)
            ks = slice(half * KB, (half + 2) * KB)
            lse = jnp.zeros((KB, LANES), F32)
            for h in range(N_HEADS):
                hs = slice(h * DH, (h + 1) * DH)
                s = jnp.where(masks[half], _dot(q_ref[rs, hs], kw_ref[ks, hs], NT) * SCALE, NEG)
                m = jnp.max(s, axis=1, keepdims=True)
                p = jnp.exp(s - m)
                den = jnp.sum(p, axis=1, keepdims=True)
                o_ref[rs, hs] = _dot(p.astype(BF16), vw_ref[ks, hs]) / den
                lse = _to_lane(lse, m + jnp.log(den), h)
            l_ref[rs, :] = lse

    return pl.pallas_call(
        body, name=name, grid=(d, n_i),
        in_specs=[pl.BlockSpec((None, BQ, DA), lambda r, i: (r, i, 0)),
                  pl.BlockSpec((None, BQ, DA), lambda r, i: (r, i, 1)),
                  pl.BlockSpec((None, KB, DA), lambda r, i: (r, prev(i), 1)),
                  pl.BlockSpec((None, BQ, DA), lambda r, i: (r, i, 0)),
                  pl.BlockSpec((None, KB, DA), lambda r, i: (r, prev(i), 0))],
        out_specs=[pl.BlockSpec((None, BQ, DA), lambda r, i: (r, i, 0)),
                   pl.BlockSpec((None, BQ, LANES), lambda r, i: (r, i, 0))],
        out_shape=[SDS((d, n_sub, DA), F32), SDS((d, n_sub, LANES), F32)],
        scratch_shapes=[pltpu.VMEM((kw, DA), BF16), pltpu.VMEM((kw, DA), BF16)],
        compiler_params=_params(("arbitrary", "arbitrary")),
    )(qk, qk, qk, v, v)


def _pooled(ext_ref, g, rows, tm):
    w = POOL_WINDOWS[g]
    cs = slice(g * GC, (g + 1) * GC)
    cur = ext_ref[HALO:HALO + tm, cs]
    win = cur
    for j in range(1, w):
        win = win + ext_ref[HALO - j:HALO - j + tm, cs]
    cnt = jnp.minimum(rows + 1, w).astype(F32)
    return win / cnt - cur, cnt


def _fill_ext(ext_ref, u_ref, uh_ref, blk, tm):
    @pl.when(blk == 0)
    def _():
        ext_ref[0:HALO, :] = jnp.zeros((HALO, DP), F32)

    @pl.when(blk > 0)
    def _():
        ext_ref[0:HALO, :] = uh_ref[...]

    ext_ref[HALO:HALO + tm, :] = u_ref[...]


def _residue_specs(tm, width):
    return [pl.BlockSpec((d, tm // d, width), lambda i: (0, i, 0)) for d in DILATIONS[1:]]


def _mix(o1, l1, o4, l4, o16, l16, u, gate, wp, scale):
    tm = 256
    n_slab = N_HEADS + 1

    def body(o1r, l1r, o4r, l4r, o16r, l16r, u_ref, uh_ref, g_ref, wp_ref, sc_ref,
             y_ref, yt_ref, attn_ref, lse_ref, lse4_ref, lse16_ref, ext_ref, ys_ref, nat_ref, ls_ref):
        i = pl.program_id(0)
        for n, (d, o_r, l_r) in enumerate(((DILATIONS[1], o4r, l4r), (DILATIONS[2], o16r, l16r))):
            for r in range(d):
                rows = pl.ds(r, tm // d, stride=d)
                for h in range(N_HEADS):
                    nat_ref[n, h, rows, :] = o_r[r, :, h * DH:(h + 1) * DH]
                nat_ref[n, N_HEADS, rows, :] = l_r[r]
        la, lb, lc = l1r[...], nat_ref[0, N_HEADS], nat_ref[1, N_HEADS]
        mx = jnp.maximum(jnp.maximum(la, lb), lc)
        ea, eb, ec = jnp.exp(la - mx), jnp.exp(lb - mx), jnp.exp(lc - mx)
        z = ea + eb + ec
        wa, wb, wc = ea / z, eb / z, ec / z
        lse = mx + jnp.log(z)
        lse_ref[...] = lse
        ls_ref[...] = lse
        for d, out in ((DILATIONS[1], lse4_ref), (DILATIONS[2], lse16_ref)):
            for r in range(d):
                out[r] = ls_ref[pl.ds(r, tm // d, stride=d), :]
        for h in range(N_HEADS):
            hs = slice(h * DH, (h + 1) * DH)
            hc = slice(h, h + 1)
            attn = wa[:, hc] * o1r[:, hs] + wb[:, hc] * nat_ref[0, h] + wc[:, hc] * nat_ref[1, h]
            attn_ref[:, hs] = attn
            gt = g_ref[:, hs]
            ys_ref[:, hs] = attn * (gt * jax.nn.sigmoid(gt))

        _fill_ext(ext_ref, u_ref, uh_ref, i, tm)
        rows = i * tm + lax.broadcasted_iota(jnp.int32, (tm, 1), 0)
        for g in range(N_GROUPS):
            cs = slice(g * GC, (g + 1) * GC)
            gs = slice(DA + g * GC, DA + (g + 1) * GC)
            pooled, _ = _pooled(ext_ref, g, rows, tm)
            po = _dot(pooled.astype(BF16), wp_ref[g]) * sc_ref[:, cs]
            gt = g_ref[:, gs]
            ys_ref[:, gs] = po * (gt * jax.nn.sigmoid(gt))
        yv = ys_ref[...]
        y_ref[...] = yv.astype(BF16)
        yt_ref[...] = yv.T.astype(BF16)

    row = lambda i: (i, 0)
    blk = pl.BlockSpec((tm, DA), row)
    lanes = pl.BlockSpec((tm, LANES), row)
    o_res, l_res = _residue_specs(tm, DA), _residue_specs(tm, LANES)
    return pl.pallas_call(
        body, name="mix", grid=(S // tm,),
        in_specs=[blk, lanes, o_res[0], l_res[0], o_res[1], l_res[1],
                  pl.BlockSpec((tm, DP), row),
                  pl.BlockSpec((HALO, DP), lambda i: (jnp.maximum(i * (tm // HALO) - 1, 0), 0)),
                  pl.BlockSpec((tm, D), row),
                  pl.BlockSpec((N_GROUPS, GC, GC), lambda i: (0, 0, 0)),
                  pl.BlockSpec((1, DP), lambda i: (0, 0))],
        out_specs=[pl.BlockSpec((tm, D), row), pl.BlockSpec((D, tm), lambda i: (0, i)), blk, lanes] + l_res,
        out_shape=[SDS((S, D), BF16), SDS((D, S), BF16), SDS((S, DA), F32), SDS((S, LANES), F32)]
        + [SDS((d, S // d, LANES), F32) for d in DILATIONS[1:]],
        scratch_shapes=[pltpu.VMEM((HALO + tm, DP), F32), pltpu.VMEM((tm, D), F32),
                        pltpu.VMEM((2, n_slab, tm, DH), F32), pltpu.VMEM((tm, LANES), F32)],
        compiler_params=_params(("arbitrary",)),
    )(o1, l1, o4, l4, o16, l16, u, u, gate, wp, scale)


def _outproj_ln(y, wout, x, target, gain, bias):
    tm = 512
    te = 128

    def body(y_ref, w_ref, x_ref, t_ref, g_ref, b_ref, dz_ref, dzb_ref, st_ref, out_ref):
        i = pl.program_id(0)

        @pl.when(i == 0)
        def _():
            st_ref[...] = jnp.zeros((8, D), F32)

        out_ref[...] = _dot(y_ref[...], w_ref[...])
        gn = g_ref[...]
        for e in range(tm // te):
            rs = slice(e * te, (e + 1) * te)
            z = ALPHA * x_ref[rs, :] + out_ref[rs, :]
            mu = jnp.mean(z, axis=1, keepdims=True)
            zc = z - mu
            var = jnp.mean(zc * zc, axis=1, keepdims=True)
            rstd = lax.rsqrt(var + LN_EPS)
            xhat = zc * rstd
            diff = xhat * gn + b_ref[...] - t_ref[rs, :]
            dyln = diff / D
            st_ref[0:1, :] += jnp.sum(dyln * xhat, axis=0, keepdims=True)
            st_ref[1:2, :] += jnp.sum(dyln, axis=0, keepdims=True)
            row_loss = jnp.sum(diff * diff, axis=1, keepdims=True) / D
            st_ref[2:3, :] += jnp.broadcast_to(0.5 * jnp.sum(row_loss, axis=0, keepdims=True), (1, D))
            dxh = dyln * gn
            m1 = jnp.mean(dxh, axis=1, keepdims=True)
            m2 = jnp.mean(dxh * xhat, axis=1, keepdims=True)
            dz = rstd * (dxh - m1 - xhat * m2)
            dz_ref[rs, :] = dz
            dzb_ref[rs, :] = dz.astype(BF16)

    row = lambda i: (i, 0)
    const = lambda i: (0, 0)
    return pl.pallas_call(
        body, name="outproj_ln", grid=(S // tm,),
        in_specs=[pl.BlockSpec((tm, D), row),
                  pl.BlockSpec((D, D), const, pipeline_mode=pl.Buffered(1)),
                  pl.BlockSpec((tm, D), row), pl.BlockSpec((tm, D), row),
                  pl.BlockSpec((1, D), const), pl.BlockSpec((1, D), const)],
        out_specs=[pl.BlockSpec((tm, D), row), pl.BlockSpec((tm, D), row), pl.BlockSpec((8, D), const)],
        out_shape=[SDS((S, D), F32), SDS((S, D), BF16), SDS((8, D), F32)],
        scratch_shapes=[pltpu.VMEM((tm, D), F32)],
        compiler_params=_params(("arbitrary",)),
    )(y, wout, x, target, gain, bias)


def _bwd_mid(dzb, wout, gate, attn, u, wp, scale, rider=None):
    tm = 256
    n = S // tm

    def body(dz_ref, w_ref, g_ref, at_ref, u_ref, uh_ref, wp_ref, sc_ref,
             dh_ref, do_ref, do4_ref, do16_ref, dd_ref, dd4_ref, dd16_ref, dwp_ref, dwp16_ref, gps_ref,
             ext_ref, eext_ref, acc_ref, nat_ref, ds_ref):
        i = pl.program_id(0)
        ib = n - 1 - i

        @pl.when(i == 0)
        def _():
            eext_ref[tm:tm + HALO, :] = jnp.zeros((HALO, DP), F32)
            acc_ref[...] = jnp.zeros((N_GROUPS, GC, GC), F32)
            gps_ref[...] = jnp.zeros((8, DP), F32)

        dy = _dot(dz_ref[...], w_ref[...], NT)
        gt = g_ref[...]
        sg = jax.nn.sigmoid(gt)
        silu = gt * sg
        dsilu = sg * (1.0 + gt * (1.0 - sg))
        dmix = dy * silu
        dyg = dy * dsilu

        at = at_ref[...]
        dat = dmix[:, 0:DA]
        do_ref[...] = dat.astype(BF16)
        dh_ref[:, DP:DP + DA] = (dyg[:, 0:DA] * at).astype(BF16)
        prod = dat * at
        dd = jnp.zeros((tm, LANES), F32)
        for h in range(N_HEADS):
            hs = slice(h * DH, (h + 1) * DH)
            dd = _to_lane(dd, jnp.sum(prod[:, hs], axis=1, keepdims=True), h)
            nat_ref[h] = dat[:, hs]
        dd_ref[...] = dd
        ds_ref[...] = dd
        for d, do_out, dd_out in ((DILATIONS[1], do4_ref, dd4_ref), (DILATIONS[2], do16_ref, dd16_ref)):
            for r in range(d):
                dd_out[r] = _residues(ds_ref, (), d, r, tm // d)
                for h in range(N_HEADS):
                    do_out[r, :, h * DH:(h + 1) * DH] = _residues(nat_ref, (h,), d, r, tm // d).astype(BF16)

        _fill_ext(ext_ref, u_ref, uh_ref, ib, tm)
        rows = ib * tm + lax.broadcasted_iota(jnp.int32, (tm, 1), 0)
        for g in range(N_GROUPS):
            w = POOL_WINDOWS[g]
            cs = slice(g * GC, (g + 1) * GC)
            gs = slice(DA + g * GC, DA + (g + 1) * GC)
            pooled, cnt = _pooled(ext_ref, g, rows, tm)
            pre = _dot(pooled.astype(BF16), wp_ref[g])
            sc = sc_ref[:, cs]
            dpo = dmix[:, gs]
            gps_ref[0:1, cs] += jnp.sum(dpo * pre, axis=0, keepdims=True)
            dh_ref[:, DP + DA + g * GC:DP + DA + (g + 1) * GC] = (dyg[:, gs] * (pre * sc)).astype(BF16)
            dpre = (dpo * sc).astype(BF16)
            acc_ref[g] += _dot(pooled.T.astype(BF16), dpre)
            dpooled = _dot(dpre, wp_ref[g], NT)
            eext_ref[0:tm, cs] = dpooled / cnt
            du = eext_ref[0:tm, cs]
            for j in range(1, w):
                du = du + eext_ref[j:j + tm, cs]
            dh_ref[:, cs] = (du - dpooled).astype(BF16)
        eext_ref[tm:tm + HALO, :] = eext_ref[0:HALO, :]

        @pl.when(i == n - 1)
        def _():
            for j in range(N_DEV):
                for g in range(N_GROUPS):
                    blk = acc_ref[g, j * PB:(j + 1) * PB, :]
                    dwp_ref[j, g * PB:(g + 1) * PB, :] = blk
                    dwp16_ref[j, g * PB:(g + 1) * PB, :] = blk.astype(BF16)

    rev = lambda i: (n - 1 - i, 0)
    const = lambda i: (0, 0)
    res = lambda width: [pl.BlockSpec((d, tm // d, width), lambda i: (0, n - 1 - i, 0)) for d in DILATIONS[1:]]
    pool_blocks = pl.BlockSpec((N_DEV, N_GROUPS * PB, GC), lambda i: (0, 0, 0))
    outs, carried = _call_carrying(
        body, rider, (n,),
        [pl.BlockSpec((tm, D), rev),
         pl.BlockSpec((D, D), const, pipeline_mode=pl.Buffered(1)),
         pl.BlockSpec((tm, D), rev), pl.BlockSpec((tm, DA), rev), pl.BlockSpec((tm, DP), rev),
         pl.BlockSpec((HALO, DP), lambda i: (jnp.maximum((n - 1 - i) * (tm // HALO) - 1, 0), 0)),
         pl.BlockSpec((N_GROUPS, GC, GC), lambda i: (0, 0, 0)),
         pl.BlockSpec((1, DP), const)],
        [pl.BlockSpec((tm, DP + D), lambda i: (n - 1 - i, 1)), pl.BlockSpec((tm, DA), rev)] + res(DA)
        + [pl.BlockSpec((tm, LANES), rev)] + res(LANES) + [pool_blocks, pool_blocks, pl.BlockSpec((8, DP), const)],
        [SDS((S, D_IN), BF16), SDS((S, DA), BF16)] + [SDS((d, S // d, DA), BF16) for d in DILATIONS[1:]]
        + [SDS((S, LANES), F32)] + [SDS((d, S // d, LANES), F32) for d in DILATIONS[1:]]
        + [SDS((N_DEV, N_GROUPS * PB, GC), F32), SDS((N_DEV, N_GROUPS * PB, GC), BF16), SDS((8, DP), F32)],
        [pltpu.VMEM((HALO + tm, DP), F32), pltpu.VMEM((tm + HALO, DP), F32),
         pltpu.VMEM((N_GROUPS, GC, GC), F32), pltpu.VMEM((N_HEADS, tm, DH), F32), pltpu.VMEM((tm, LANES), F32)],
        [dzb, wout, gate, attn, u, u, wp, scale], "bwd_mid")
    return outs, carried


def _attn_bwd(qk, v, do, lse, dd, rope, name, others=None, dh=None):
    d, n_sub, _ = qk.shape
    n_i = n_sub // BQ
    n_kb = n_sub // KB
    qw = BQ + KB
    final = others is not None
    out_dtype = BF16 if final else F32
    n_cb = 3 * DA // DH

    def nxt(i):
        return jnp.minimum(2 * i + 2, n_kb - 1)

    def body(qc_ref, qn_ref, kc_ref, vc_ref, doc_ref, don_ref, lc_ref, ln_ref, dc_ref, dn_ref,
             c_ref, a_ref, b_ref, *rest):
        if final:
            acc4_ref, acc16_ref, _, out_ref, carry_ref, qw_ref, dow_ref, lw_ref, dw_ref, nat_ref = rest
        else:
            out_ref, carry_ref, qw_ref, dow_ref, lw_ref, dw_ref = rest
        i = pl.program_id(1)

        @pl.when(i == 0)
        def _():
            carry_ref[...] = jnp.zeros((KB, DA), F32)

        for win, own, after in ((qw_ref, qc_ref, qn_ref), (dow_ref, doc_ref, don_ref), (lw_ref, lc_ref, ln_ref),
                                (dw_ref, dc_ref, dn_ref)):
            win[0:BQ, :] = own[...]
            win[BQ:qw, :] = after[...]
        if final:
            for n, (dil, acc) in enumerate(((DILATIONS[1], acc4_ref), (DILATIONS[2], acc16_ref))):
                for r in range(dil):
                    for cb in range(n_cb):
                        nat_ref[n, cb, pl.ds(r, BQ // dil, stride=dil), :] = acc[r, :, cb * DH:(cb + 1) * DH]

        a = lax.broadcasted_iota(jnp.int32, (2 * KB, KB), 0)
        b = lax.broadcasted_iota(jnp.int32, (2 * KB, KB), 1)
        band = (b <= a) & (a <= b + KB)
        n_q = jnp.where(i == n_i - 1, KB, 2 * KB)
        masks = (band, band & (a < n_q))
        for h in range(N_HEADS):
            hs = slice(h * DH, (h + 1) * DH)
            hc = slice(h, h + 1)
            dq_parts = []
            for half in range(2):
                ks = slice(half * KB, (half + 1) * KB)
                qs = slice(half * KB, (half + 2) * KB)
                q, k, vv, dob = qw_ref[qs, hs], kc_ref[ks, hs], vc_ref[ks, hs], dow_ref[qs, hs]
                s = _dot(q, k, NT) * SCALE
                p = jnp.exp(jnp.where(masks[half], s - lw_ref[qs, hc], NEG))
                dp = _dot(dob, vv, NT)
                ds = (p * (dp - dw_ref[qs, hc]) * SCALE).astype(BF16)
                dq_parts.append(_dot(ds, k))
                dk = _rope_t(_dot(ds, q, TN), c_ref[ks, :], a_ref[ks, :], b_ref[ks, :])
                dv = _dot(p.astype(BF16), dob, TN)
                for base, gv in ((DA, dk), (2 * DA, dv)):
                    if final:
                        cb = base // DH + h
                        gv = gv + nat_ref[0, cb, ks, :] + nat_ref[1, cb, ks, :]
                    out_ref[ks, base + h * DH:base + (h + 1) * DH] = gv.astype(out_dtype)
            dq_own = (dq_parts[0][0:KB] + carry_ref[:, hs], dq_parts[0][KB:2 * KB] + dq_parts[1][0:KB])
            carry_ref[:, hs] = dq_parts[1][KB:2 * KB]
            for half in range(2):
                ks = slice(half * KB, (half + 1) * KB)
                gv = _rope_t(dq_own[half], c_ref[ks, :], a_ref[ks, :], b_ref[ks, :])
                if final:
                    gv = gv + nat_ref[0, h, ks, :] + nat_ref[1, h, ks, :]
                out_ref[ks, h * DH:(h + 1) * DH] = gv.astype(out_dtype)

    def cur(rows, width, col=0):
        return pl.BlockSpec((None, rows, width), lambda r, i: (r, i, col))

    def nx(width, col=0):
        return pl.BlockSpec((None, KB, width), lambda r, i: (r, nxt(i), col))

    in_specs = [cur(BQ, DA), nx(DA), cur(BQ, DA, 1), cur(BQ, DA), cur(BQ, DA), nx(DA),
                cur(BQ, LANES), nx(LANES), cur(BQ, LANES), nx(LANES)] + [cur(BQ, DH)] * 3
    args = [qk, qk, qk, v, do, do, lse, lse, dd, dd] + list(rope)
    scratch = [pltpu.VMEM((KB, DA), F32), pltpu.VMEM((qw, DA), BF16), pltpu.VMEM((qw, DA), BF16),
               pltpu.VMEM((qw, LANES), F32), pltpu.VMEM((qw, LANES), F32)]
    if final:
        assert d == 1
        in_specs += [pl.BlockSpec((dil, BQ // dil, 3 * DA), lambda r, i: (0, i, 0)) for dil in DILATIONS[1:]]
        in_specs.append(pl.BlockSpec(memory_space=pl.ANY))
        args += list(others) + [dh[None]]
        scratch.append(pltpu.VMEM((2, n_cb, BQ, DH), F32))
    return pl.pallas_call(
        body, name=name, grid=(d, n_i),
        in_specs=in_specs,
        out_specs=cur(BQ, 3 * DA),
        out_shape=SDS((d, n_sub, D_IN if final else 3 * DA), out_dtype),
        input_output_aliases={len(args) - 1: 0} if final else {},
        scratch_shapes=scratch,
        compiler_params=_params(("arbitrary", "arbitrary")),
    )(*args)


def _grad_w_out(yt, dzb):
    tk = 512

    def body(yt_ref, dz_ref, o_ref, o16_ref, acc_ref, b16_ref):
        k = pl.program_id(0)

        @pl.when(k == 0)
        def _():
            acc_ref[...] = jnp.zeros((D, D), F32)

        acc_ref[...] += _dot(yt_ref[...], dz_ref[...])

        @pl.when(k == S // tk - 1)
        def _():
            b16_ref[...] = acc_ref[...].astype(BF16)
            pltpu.sync_copy(acc_ref, o_ref)
            pltpu.sync_copy(b16_ref, o16_ref)

    hbm = pl.BlockSpec(memory_space=pl.ANY)
    o, o16 = pl.pallas_call(
        body, name="grad_w_out", grid=(S // tk,),
        in_specs=[pl.BlockSpec((D, tk), lambda k: (0, k)), pl.BlockSpec((tk, D), lambda k: (k, 0))],
        out_specs=[hbm, hbm],
        out_shape=[SDS((D, D), F32), SDS((D, D), BF16)],
        scratch_shapes=[pltpu.VMEM((D, D), F32), pltpu.VMEM((D, D), BF16)],
        compiler_params=_params(("arbitrary",)),
    )(yt, dzb)
    return o.reshape(N_DEV, OB, D), o16.reshape(N_DEV, OB, D)


def _grad_w_in(xt, dh):
    tn = 256
    per_blk = WB // tn

    def body(xt_ref, dh_ref, o_ref, o16_ref):
        acc = _dot(xt_ref[...], dh_ref[...])
        o_ref[...] = acc
        o16_ref[...] = acc.astype(BF16)

    blk = pl.BlockSpec((None, D, tn), lambda c: (c // per_blk, 0, c % per_blk))
    return pl.pallas_call(
        body, name="grad_w_in", grid=(D_IN // tn,),
        in_specs=[pl.BlockSpec((D, S), lambda c: (0, 0), pipeline_mode=pl.Buffered(1)),
                  pl.BlockSpec((S, tn), lambda c: (0, c))],
        out_specs=[blk, blk],
        out_shape=[SDS((N_DEV, D, WB), F32), SDS((N_DEV, D, WB), BF16)],
        compiler_params=_params(("arbitrary",)),
    )(xt, dh)


def _grad_x(dz, dh, wg, rider=None):
    tm = 1024

    def body(dz_ref, dh_ref, w_ref, o_ref):
        @pl.when(pl.program_id(1) == 0)
        def _():
            o_ref[...] = ALPHA * dz_ref[...]

        o_ref[...] += _dot(dh_ref[...], w_ref[...], NT)

    outs, carried = _call_carrying(
        body, rider, (S // tm, N_DEV),
        [pl.BlockSpec((tm, D), lambda i, j: (i, 0)), pl.BlockSpec((tm, WB), lambda i, j: (i, j)),
         pl.BlockSpec((None, D, WB), lambda i, j: (j, 0, 0))],
        [pl.BlockSpec((tm, D), lambda i, j: (i, 0))], [SDS((S, D), F32)], [], [dz, dh, wg], "grad_x")
    return outs[0], carried


def _place():
    x, y, c = lax.axis_index("x"), lax.axis_index("y"), lax.axis_index("c")
    chips = [(x, y), (1 - x, y), (x, 1 - y), (1 - x, 1 - y)]
    return x, y, c, chips


def _blk(x, y, c):
    return 4 * x + 2 * y + c


def _adamw(w, g, m, v):
    m = ADAM_B1 * m + (1.0 - ADAM_B1) * g
    v = ADAM_B2 * v + (1.0 - ADAM_B2) * (g * g)
    m_hat = m / (1.0 - ADAM_B1 ** ADAM_STEP)
    v_hat = v / (1.0 - ADAM_B2 ** ADAM_STEP)
    delta = -ADAM_LR * (m_hat / (jnp.sqrt(v_hat) + ADAM_EPS) + ADAM_WD * w)
    return delta, m, v


class _Rider:
    def __init__(self, args, in_specs, out_shape, out_specs, scratch, start, finish):
        self.args, self.in_specs, self.out_shape, self.out_specs = args, in_specs, out_shape, out_specs
        self.scratch, self.start, self.finish = scratch, start, finish


def _carry(body, rider, n_in, n_out, first, last):
    if rider is None:
        return body
    r_in, r_out, r_scr = len(rider.args), len(rider.out_shape), len(rider.scratch)

    def carrying(*refs):
        o0 = n_in + r_in
        s0 = o0 + n_out + r_out
        s1 = len(refs) - r_scr
        theirs = (refs[n_in:o0], refs[o0 + n_out:s0], refs[s1:])
        pl.when(first())(lambda: rider.start(*theirs))
        body(*refs[:n_in], *refs[o0:o0 + n_out], *refs[s0:s1])
        pl.when(last())(lambda: rider.finish(*theirs))

    return carrying


def _call_carrying(body, rider, grid, in_specs, out_specs, out_shape, scratch, args, name):
    n_in, n_out = len(in_specs), len(out_specs)
    ids = lambda: [pl.program_id(a) for a in range(len(grid))]
    first = lambda: functools.reduce(jnp.logical_and, [i == 0 for i in ids()])
    last = lambda: functools.reduce(jnp.logical_and, [i == n - 1 for i, n in zip(ids(), grid)])
    if rider is not None:
        in_specs, args = in_specs + rider.in_specs, list(args) + rider.args
        out_specs, out_shape = out_specs + rider.out_specs, out_shape + rider.out_shape
        scratch = scratch + rider.scratch
    outs = pl.pallas_call(
        _carry(body, rider, n_in, n_out, first, last), name=name, grid=grid,
        in_specs=in_specs, out_specs=out_specs, out_shape=out_shape, scratch_shapes=scratch,
        compiler_params=_params(("arbitrary",) * len(grid)),
    )(*args)
    return list(outs[:n_out]), list(outs[n_out:])


def _gather_copy(tensors, send_sems, recv_sems, t, k, block, to, src=None):
    dst = tensors[t][1](_blk(*block))
    return pltpu.make_async_remote_copy(
        src_ref=dst if src is None else src, dst_ref=dst,
        send_sem=send_sems.at[t, k], recv_sem=recv_sems.at[t, k], device_id=to, device_id_type=MESH)


def _gather_start(tensors, send_sems, recv_sems, local_sems):
    x, y, c, chips = _place()
    me, sib = (x, y, c), (x, y, 1 - c)
    for t, (src, dst) in enumerate(tensors):
        pltpu.make_async_copy(src, dst(_blk(*me)), local_sems.at[t]).start()
        _gather_copy(tensors, send_sems, recv_sems, t, 0, me, sib, src).start()
        for j in (1, 2, 3):
            _gather_copy(tensors, send_sems, recv_sems, t, j, me, (*chips[j], c), src).start()


def _gather_finish(tensors, send_sems, recv_sems, local_sems):
    x, y, c, chips = _place()
    me, sib = (x, y, c), (x, y, 1 - c)
    copy = functools.partial(_gather_copy, tensors, send_sems, recv_sems)
    for t in range(len(tensors)):
        for j in (1, 2, 3):
            copy(t, j, (*chips[j], c), me).wait_recv()
            copy(t, 3 + j, (*chips[j], c), sib).start()
    for t, (src, dst) in enumerate(tensors):
        copy(t, 0, sib, me).wait_recv()
        for j in (1, 2, 3):
            copy(t, 3 + j, (*chips[j], 1 - c), me).wait_recv()
        copy(t, 0, me, sib, src).wait_send()
        for j in (1, 2, 3):
            copy(t, j, me, (*chips[j], c), src).wait_send()
            copy(t, 3 + j, (*chips[j], c), sib).wait_send()
        pltpu.make_async_copy(src, dst(_blk(*me)), local_sems.at[t]).wait()


def _gather_w_in(w_in_s):
    def body(win_ref, gin_ref, bin_ref, send_sems, recv_sems, local_sems):
        bin_ref[...] = win_ref[...].astype(BF16)
        tensors = [(bin_ref, lambda b: gin_ref.at[b])]
        _gather_start(tensors, send_sems, recv_sems, local_sems)
        _gather_finish(tensors, send_sems, recv_sems, local_sems)

    return pl.pallas_call(
        body, name="gather_w_in",
        in_specs=[pl.BlockSpec(memory_space=pltpu.VMEM)], out_specs=pl.BlockSpec(memory_space=pl.ANY),
        out_shape=SDS((N_DEV, D, WB), BF16),
        scratch_shapes=[pltpu.VMEM((D, WB), BF16),
                        pltpu.SemaphoreType.DMA((1, 7)), pltpu.SemaphoreType.DMA((1, 7)), pltpu.SemaphoreType.DMA((1,))],
        compiler_params=_params(),
    )(w_in_s)


def _gather_small_rider(w_out_s, w_pool_s):
    def tensors(ins, outs, scr):
        gout_ref, gpool_ref = outs

        def pool_rows(b):
            return gpool_ref.at[:, pl.ds(pl.multiple_of(b * PB, PB), PB), :]

        return [(scr[0], lambda b: gout_ref.at[b]), (scr[1], pool_rows)]

    def start(ins, outs, scr):
        scr[0][...] = ins[0][...].astype(BF16)
        scr[1][...] = ins[1][...].astype(BF16)
        _gather_start(tensors(ins, outs, scr), *scr[2:])

    def finish(ins, outs, scr):
        _gather_finish(tensors(ins, outs, scr), *scr[2:])

    vmem = pl.BlockSpec(memory_space=pltpu.VMEM)
    hbm = pl.BlockSpec(memory_space=pl.ANY)
    return _Rider(
        args=[w_out_s, w_pool_s], in_specs=[vmem, vmem],
        out_shape=[SDS((N_DEV, OB, D), BF16), SDS((N_GROUPS, GC, GC), BF16)], out_specs=[hbm, hbm],
        scratch=[pltpu.VMEM((OB, D), BF16), pltpu.VMEM((N_GROUPS, PB, GC), BF16),
                 pltpu.SemaphoreType.DMA((2, 7)), pltpu.SemaphoreType.DMA((2, 7)), pltpu.SemaphoreType.DMA((2,))],
        start=start, finish=finish)


def _block_table():
    x, y, c, chips = _place()
    return jnp.stack([_blk(*chip, c) for chip in chips]).astype(jnp.int32)


def _rs_exchange(name, p16):
    _, r_tot, cols = p16.shape

    def body(p16_ref, r1_ref, send_sems, recv_sems):
        x, y, c, chips = _place()
        copies = [pltpu.make_async_remote_copy(
            src_ref=p16_ref.at[_blk(*chips[k], 1 - c)], dst_ref=r1_ref.at[k],
            send_sem=send_sems.at[k], recv_sem=recv_sems.at[k], device_id=(x, y, 1 - c), device_id_type=MESH)
            for k in range(4)]
        for cp in copies:
            cp.start()
        for cp in copies:
            cp.wait()

    hbm = pl.BlockSpec(memory_space=pl.ANY)
    return pl.pallas_call(
        body, name=name, in_specs=[hbm], out_specs=hbm, out_shape=SDS((4, r_tot, cols), BF16),
        scratch_shapes=[pltpu.SemaphoreType.DMA((4,)), pltpu.SemaphoreType.DMA((4,))],
        compiler_params=_params(),
    )(p16)


def _chip_sums(name, table, p32, r1, rc):
    _, r_tot, cols = p32.shape

    def body(tbl_ref, p_ref, r_ref, o_ref):
        o_ref[...] = (p_ref[...] + r_ref[...].astype(F32)).astype(BF16)

    return pl.pallas_call(
        body, name=name,
        grid_spec=pltpu.PrefetchScalarGridSpec(
            num_scalar_prefetch=1, grid=(3, r_tot // rc),
            in_specs=[pl.BlockSpec((None, rc, cols), lambda k, ch, tbl: (tbl[k + 1], ch, 0)),
                      pl.BlockSpec((None, rc, cols), lambda k, ch, tbl: (k + 1, ch, 0))],
            out_specs=pl.BlockSpec((None, rc, cols), lambda k, ch, tbl: (k, ch, 0))),
        out_shape=SDS((3, r_tot, cols), BF16),
        compiler_params=_params(("arbitrary", "arbitrary")),
    )(table, p32, r1)


def _stage2_rider(sums, stats=None):
    n_t = len(sums)

    def copies(ins, outs, scr):
        x, y, c, chips = _place()
        out = []
        for t in range(n_t):
            for k in (1, 2, 3):
                out.append(pltpu.make_async_remote_copy(
                    src_ref=ins[t].at[k - 1], dst_ref=outs[t].at[k - 1],
                    send_sem=scr[0].at[t, k - 1], recv_sem=scr[1].at[t, k - 1],
                    device_id=(*chips[k], c), device_id_type=MESH))
        if stats is not None:
            for k in range(1, N_DEV):
                peer = (x ^ ((k >> 2) & 1), y ^ ((k >> 1) & 1), c ^ (k & 1))
                out.append(pltpu.make_async_remote_copy(
                    src_ref=scr[4], dst_ref=outs[n_t].at[_blk(x, y, c)],
                    send_sem=scr[2].at[k - 1], recv_sem=scr[3].at[k - 1], device_id=peer, device_id_type=MESH))
        return out

    def own_rows(outs, scr):
        x, y, c, _ = _place()
        return pltpu.make_async_copy(scr[4], outs[n_t].at[_blk(x, y, c)], scr[5])

    def start(ins, outs, scr):
        if stats is not None:
            scr[4][...] = ins[n_t][...]
            scr[4][3:4, 0:DP] = ins[n_t + 1][0:1, :]
            own_rows(outs, scr).start()
        for cp in copies(ins, outs, scr):
            cp.start()

    def finish(ins, outs, scr):
        for cp in copies(ins, outs, scr):
            cp.wait()
        if stats is not None:
            own_rows(outs, scr).wait()

    vmem = pl.BlockSpec(memory_space=pltpu.VMEM)
    hbm = pl.BlockSpec(memory_space=pl.ANY)
    scratch = [pltpu.SemaphoreType.DMA((n_t, 3)), pltpu.SemaphoreType.DMA((n_t, 3))]
    args, in_specs = list(sums), [hbm] * n_t
    out_shape, out_specs = [SDS(s.shape, BF16) for s in sums], [hbm] * n_t
    if stats is not None:
        scratch += [pltpu.SemaphoreType.DMA((N_DEV - 1,)), pltpu.SemaphoreType.DMA((N_DEV - 1,)),
                    pltpu.VMEM((8, D), F32), pltpu.SemaphoreType.DMA(())]
        args, in_specs = args + list(stats), in_specs + [vmem, vmem]
        out_shape, out_specs = out_shape + [SDS((N_DEV, 8, D), F32)], out_specs + [hbm]
    return _Rider(args, in_specs, out_shape, out_specs, scratch, start, finish)


def _adamw_shard(name, table, p32, r1, r2, w, m, v, rc):
    _, r_tot, cols = p32.shape

    def body(tbl_ref, p_ref, r1_ref, r2_ref, w_ref, m_ref, v_ref, g_ref, d_ref, nm_ref, nv_ref):
        g = p_ref[...] + r1_ref[...].astype(F32)
        for k in range(3):
            g = g + r2_ref[k].astype(F32)
        delta, nm, nv = _adamw(w_ref[...], g, m_ref[...], v_ref[...])
        g_ref[...] = g
        d_ref[...] = delta
        nm_ref[...] = nm
        nv_ref[...] = nv

    rows = pl.BlockSpec((rc, cols), lambda ch, tbl: (ch, 0))
    shard = SDS((r_tot, cols), F32)
    return pl.pallas_call(
        body, name=name,
        grid_spec=pltpu.PrefetchScalarGridSpec(
            num_scalar_prefetch=1, grid=(r_tot // rc,),
            in_specs=[pl.BlockSpec((None, rc, cols), lambda ch, tbl: (tbl[0], ch, 0)),
                      pl.BlockSpec((None, rc, cols), lambda ch, tbl: (0, ch, 0)),
                      pl.BlockSpec((3, rc, cols), lambda ch, tbl: (0, ch, 0)), rows, rows, rows],
            out_specs=[rows, rows, rows, rows]),
        out_shape=[shard, shard, shard, shard],
        compiler_params=_params(("arbitrary",)),
    )(table, p32, r1, r2, w, m, v)


def _replicated_adamw(gathered, gain, bias, scale, m_gain, m_bias, m_scale, v_gain, v_bias, v_scale):
    def body(all_ref, g_ref, b_ref, s_ref, mg_ref, mb_ref, ms_ref, vg_ref, vb_ref, vs_ref,
             tot_ref, dl_ref, nm_ref, nv_ref):
        tot = all_ref[0]
        for b in range(1, N_DEV):
            tot = tot + all_ref[b]
        tot_ref[...] = tot
        dl_ref[...] = jnp.zeros((8, D), F32)
        nm_ref[...] = jnp.zeros((8, D), F32)
        nv_ref[...] = jnp.zeros((8, D), F32)
        for row, width, w_r, m_r, v_r in ((0, D, g_ref, mg_ref, vg_ref), (1, D, b_ref, mb_ref, vb_ref),
                                          (3, DP, s_ref, ms_ref, vs_ref)):
            delta, nm, nv = _adamw(w_r[...], tot[row:row + 1, 0:width], m_r[...], v_r[...])
            dl_ref[row:row + 1, 0:width] = delta
            nm_ref[row:row + 1, 0:width] = nm
            nv_ref[row:row + 1, 0:width] = nv

    vmem = pl.BlockSpec(memory_space=pltpu.VMEM)
    rows = SDS((8, D), F32)
    return pl.pallas_call(
        body, name="replicated_adamw",
        in_specs=[vmem] * 10, out_specs=[vmem] * 4, out_shape=[rows, rows, rows, rows],
        compiler_params=_params(),
    )(gathered, gain, bias, scale, m_gain, m_bias, m_scale, v_gain, v_bias, v_scale)


def kernel(x, w_in, w_pool, pool_scale, w_out, ln_gain, ln_bias, loss_target, m_w_in, m_w_pool, m_pool_scale, m_w_out, m_ln_gain, m_ln_bias, v_w_in, v_w_pool, v_pool_scale, v_w_out, v_ln_gain, v_ln_bias):
    pool_rows = (N_GROUPS * PB, GC)
    x2, target = x[0], loss_target[0]
    rope = lax.optimization_barrier(_rope_tables())

    table = _block_table()
    wg_in = _gather_w_in(w_in[0])

    by_res = lambda t, d: t.reshape(S // d, d, DH).transpose(1, 0, 2)
    ropes = [[by_res(t, d) for t in rope] for d in DILATIONS]

    xb, xt = _prep_x(x2)
    qk, (wg_out, wg_pool) = _proj(xb, wg_in, 0, 2 * DA, BF16, "proj_qk", rope=rope, by_residue=True,
                                  rider=_gather_small_rider(w_out[0], w_pool[0]))
    wg_out = wg_out.reshape(D, D)
    v = _proj(xb, wg_in, 2 * DA, DA, BF16, "proj_v", by_residue=True)
    qk[0], v[0] = qk[0][None], v[0][None]
    u = _proj(xb, wg_in, 3 * DA, DP, F32, "proj_u")
    gate = _proj(xb, wg_in, 3 * DA + DP, D, F32, "proj_gate")
    fwd = [_attn_fwd(qk[n], v[n], f"attn_fwd_d{d}") for n, d in enumerate(DILATIONS)]
    y, yt, attn, lse1, lse4, lse16 = _mix(fwd[0][0][0], fwd[0][1][0], *fwd[1], *fwd[2], u, gate, wg_pool, pool_scale)
    dz, dzb, stats = _outproj_ln(y, wg_out, x2, target, ln_gain, ln_bias)

    dwout, dwout16 = _grad_w_out(yt, dzb)
    r1_out = _rs_exchange("rs_out_exchange", dwout16)
    s2_out = _chip_sums("rs_out_sums", table, dwout, r1_out, OB)
    mid, (r2_out,) = _bwd_mid(dzb, wg_out, gate, attn, u, wg_pool, pool_scale, rider=_stage2_rider([s2_out]))
    dh, do1, do4, do16, dd1, dd4, dd16, dwp, dwp16, gps = mid
    g_out, d_out, nm_out, nv_out = _adamw_shard(
        "adamw_w_out", table, dwout, r1_out, r2_out, w_out[0], m_w_out[0], v_w_out[0], OB // 2)

    do, lse, dd = [do1[None], do4, do16], [lse1[None], lse4, lse16], [dd1[None], dd4, dd16]
    others = [_attn_bwd(qk[n], v[n], do[n], lse[n], dd[n], ropes[n], f"attn_bwd_d{DILATIONS[n]}") for n in (1, 2)]
    dh = _attn_bwd(qk[0], v[0], do[0], lse[0], dd[0], ropes[0], "attn_bwd_d1", others=others, dh=dh)[0]

    dwin, dwin16 = _grad_w_in(xt, dh)
    r1_in = _rs_exchange("rs_in_exchange", dwin16)
    s2_in = _chip_sums("rs_in_sums", table, dwin, r1_in, 512)
    r1_pool = _rs_exchange("rs_pool_exchange", dwp16)
    s2_pool = _chip_sums("rs_pool_sums", table, dwp, r1_pool, N_GROUPS * PB)
    grad_x, (r2_in, r2_pool, gathered) = _grad_x(
        dz, dh, wg_in, rider=_stage2_rider([s2_in, s2_pool], stats=(stats, gps)))
    g_in, d_in, nm_in, nv_in = _adamw_shard(
        "adamw_w_in", table, dwin, r1_in, r2_in, w_in[0], m_w_in[0], v_w_in[0], 256)
    g_pool, d_pool, nm_pool, nv_pool = _adamw_shard(
        "adamw_w_pool", table, dwp, r1_pool, r2_pool, w_pool[0].reshape(pool_rows), m_w_pool[0].reshape(pool_rows),
        v_w_pool[0].reshape(pool_rows), N_GROUPS * PB)
    tot, dl, nm, nv = _replicated_adamw(gathered, ln_gain, ln_bias, pool_scale, m_ln_gain, m_ln_bias, m_pool_scale,
                                        v_ln_gain, v_ln_bias, v_pool_scale)

    shard4 = lambda t: t.reshape(1, N_GROUPS, PB, GC)
    lead = lambda t: t[None]
    small = lambda t: (t[3:4, 0:DP], t[0:1], t[1:2])
    g_ps, g_gain, g_bias = small(tot)
    d_ps, d_gain, d_bias = small(dl)
    nm_ps, nm_gain, nm_bias = small(nm)
    nv_ps, nv_gain, nv_bias = small(nv)
    return (tot[2, 0], lead(grad_x),
            lead(g_in), shard4(g_pool), g_ps, lead(g_out), g_gain, g_bias,
            lead(d_in), shard4(d_pool), d_ps, lead(d_out), d_gain, d_bias,
            lead(nm_in), shard4(nm_pool), nm_ps, lead(nm_out), nm_gain, nm_bias,
            lead(nv_in), shard4(nv_pool), nv_ps, lead(nv_out), nv_gain, nv_bias)
```

```python
import functools

import jax
import jax.numpy as jnp
from jax import lax
from jax.experimental import pallas as pl
from jax.experimental.pallas import tpu as pltpu

F32 = jnp.float32
BF16 = jnp.bfloat16
SDS = jax.ShapeDtypeStruct
MESH = pl.DeviceIdType.MESH

N_DEV = 8
S = 4096
D = 2048
N_HEADS = 8
DH = 128
DA = N_HEADS * DH
DP = 1024
N_GROUPS = 4
GC = DP // N_GROUPS
POOL_WINDOWS = (2, 4, 8, 16)
HALO = 16
D_IN = 3 * DA + DP + D
WB = D_IN // N_DEV
TN = 256
HW = 3 * DA
OB = D // N_DEV
PB = GC // N_DEV
ROPE_DIM = DH // 4
ROPE_HALF = ROPE_DIM // 2
ROPE_THETA = 500000.0
DILATIONS = (1, 4, 16)
KB = 128
LN_EPS = 1e-5
ALPHA = 2.0 ** 0.25
SCALE = DH ** -0.5
NEG = -1e30
ADAM_LR, ADAM_B1, ADAM_B2, ADAM_EPS, ADAM_WD, ADAM_STEP = 0.001, 0.9, 0.999, 1e-08, 0.01, 10

VMEM_LIMIT_V7X = 60 * 1024 * 1024

NT = (((1,), (1,)), ((), ()))
T_N = (((0,), (0,)), ((), ()))


def _params(sem=None):
    return pltpu.CompilerParams(dimension_semantics=sem, vmem_limit_bytes=VMEM_LIMIT_V7X)


def _dot(a, b, dims=None):
    if dims is None:
        return jnp.dot(a, b, preferred_element_type=F32)
    return lax.dot_general(a, b, dims, preferred_element_type=F32)


def _rope_tables():
    inv_freq = ROPE_THETA ** (-(2.0 * jnp.arange(ROPE_HALF, dtype=F32)) / ROPE_DIM)
    ang = jnp.arange(S, dtype=jnp.int32).astype(F32)[:, None] * inv_freq[None, :]
    cos, sin = jnp.cos(ang), jnp.sin(ang)
    rest = DH - ROPE_DIM
    c = jnp.concatenate([cos, cos, jnp.ones((S, rest), F32)], axis=1)
    sn = jnp.concatenate([-sin, sin, jnp.zeros((S, rest), F32)], axis=1)
    return c, sn


def _rope_partner(t):
    lane = lax.broadcasted_iota(jnp.int32, t.shape, 1)
    return jnp.where(lane < ROPE_HALF, pltpu.roll(t, DH - ROPE_HALF, 1), pltpu.roll(t, ROPE_HALF, 1))


def _rope(t, c, sn):
    return t * c + _rope_partner(t) * sn


def _rope_t(g, c, sn):
    return g * c - _rope_partner(g) * sn


def _prep_x(x):
    tm = 512

    def body(x_ref, xb_ref, xt_ref):
        xv = x_ref[...]
        xb_ref[...] = xv.astype(BF16)
        xt_ref[...] = xv.T.astype(BF16)

    return pl.pallas_call(
        body, name="prep_x", grid=(S // tm,),
        in_specs=[pl.BlockSpec((tm, D), lambda i: (i, 0))],
        out_specs=[pl.BlockSpec((tm, D), lambda i: (i, 0)), pl.BlockSpec((D, tm), lambda i: (0, i))],
        out_shape=[SDS((S, D), BF16), SDS((D, S), BF16)],
        compiler_params=_params(("arbitrary",)),
    )(x)


def _residues(slab_ref, idx, d, r, n):
    return slab_ref[(*idx, pl.ds(r, n, stride=d), slice(None))]


def _proj_phase(xb, wg, s, rope, prev, rider):
    n_heads = TN // DH
    n16 = HW // TN
    j16 = n16 // W_PHASES
    n_rope = 2 * DA // TN
    once = pl.Buffered(1)

    def body(x_ref, w_ref, c_ref, sn_ref, *rest):
        h16_ref, h4_ref, h16r_ref, h32_ref, slab_ref = rest[-5:]
        j = pl.program_id(0)
        acc = _dot(x_ref[...], w_ref[...])

        def heads(with_rope):
            for hh in range(n_heads):
                hs = slice(hh * DH, (hh + 1) * DH)
                t = _rope(acc[:, hs], c_ref[...], sn_ref[...]) if with_rope else acc[:, hs]
                h16_ref[:, hs] = t.astype(BF16)
                slab_ref[...] = t
                for d, out in ((DILATIONS[1], h4_ref), (DILATIONS[2], h16r_ref)):
                    for r in range(d):
                        out[r, :, hs] = _residues(slab_ref, (), d, r, S // d).astype(BF16)

        pl.when(W_PHASES * j + s < n_rope)(lambda: heads(True))
        pl.when((W_PHASES * j + s >= n_rope) & (j < j16))(lambda: heads(False))

        @pl.when(j >= j16)
        def _():
            h32_ref[...] = acc

    col16 = lambda j: W_PHASES * jnp.minimum(j, j16 - 1) + s
    col32 = lambda j: W_PHASES * jnp.maximum(j, j16) + s - n16
    hbm = pl.BlockSpec(memory_space=pl.ANY)
    in_specs = [pl.BlockSpec((S, D), lambda j: (0, 0), pipeline_mode=once),
                pl.BlockSpec((None, D, TN), lambda j: (j, 0, 0))]
    in_specs += [pl.BlockSpec((S, DH), lambda j: (0, 0), pipeline_mode=once)] * 2
    args = [xb, wg] + list(rope)
    aliases = {}
    if prev is not None:
        aliases = {len(args) + n: n for n in range(4)}
        in_specs, args = in_specs + [hbm] * 4, args + list(prev)
    out_specs = [pl.BlockSpec((S, TN), lambda j: (0, col16(j)))]
    out_specs += [pl.BlockSpec((d, S // d, TN), lambda j: (0, 0, col16(j))) for d in DILATIONS[1:]]
    out_specs += [pl.BlockSpec((S, TN), lambda j: (0, col32(j)))]
    out_shape = [SDS((S, HW), BF16)] + [SDS((d, S // d, HW), BF16) for d in DILATIONS[1:]] + [SDS((S, HW), F32)]
    return _call_carrying(
        body, rider, (N_DEV,), in_specs, out_specs, out_shape, [pltpu.VMEM((S, DH), F32)], args,
        f"proj_phase{s}", aliases)


BQ = 2 * KB
LANES = 128


def _to_lane(acc, col, h):
    lane = lax.broadcasted_iota(jnp.int32, acc.shape, 1)
    return jnp.where(lane == h, col, acc)


def _attn_fwd(h16, name):
    d, n_sub, _ = h16.shape
    n_i = n_sub // BQ
    kw = KB + BQ

    def prev(i):
        return jnp.maximum(2 * i - 1, 0)

    def body(q_ref, kc_ref, kp_ref, vc_ref, vp_ref, o_ref, l_ref, kw_ref, vw_ref):
        i = pl.program_id(1)
        kw_ref[0:KB, :] = kp_ref[...]
        kw_ref[KB:kw, :] = kc_ref[...]
        vw_ref[0:KB, :] = vp_ref[...]
        vw_ref[KB:kw, :] = vc_ref[...]
        a = lax.broadcasted_iota(jnp.int32, (KB, 2 * KB), 0)
        b = lax.broadcasted_iota(jnp.int32, (KB, 2 * KB), 1)
        band = (b >= a) & (b <= a + KB)
        first_key = jnp.where(i == 0, KB, 0)
        masks = (band & (b >= first_key), band)
        for half in range(2):
            rs = slice(half * KB, (half + 1) * KB)
            ks = slice(half * KB, (half + 2) * KB)
            lse = jnp.zeros((KB, LANES), F32)
            for h in range(N_HEADS):
                hs = slice(h * DH, (h + 1) * DH)
                s = jnp.where(masks[half], _dot(q_ref[rs, hs], kw_ref[ks, hs], NT) * SCALE, NEG)
                m = jnp.max(s, axis=1, keepdims=True)
                p = jnp.exp(s - m)
                den = jnp.sum(p, axis=1, keepdims=True)
                o_ref[rs, hs] = _dot(p.astype(BF16), vw_ref[ks, hs]) / den
                lse = _to_lane(lse, m + jnp.log(den), h)
            l_ref[rs, :] = lse

    return pl.pallas_call(
        body, name=name, grid=(d, n_i),
        in_specs=[pl.BlockSpec((None, BQ, DA), lambda r, i: (r, i, 0)),
                  pl.BlockSpec((None, BQ, DA), lambda r, i: (r, i, 1)),
                  pl.BlockSpec((None, KB, DA), lambda r, i: (r, prev(i), 1)),
                  pl.BlockSpec((None, BQ, DA), lambda r, i: (r, i, 2)),
                  pl.BlockSpec((None, KB, DA), lambda r, i: (r, prev(i), 2))],
        out_specs=[pl.BlockSpec((None, BQ, DA), lambda r, i: (r, i, 0)),
                   pl.BlockSpec((None, BQ, LANES), lambda r, i: (r, i, 0))],
        out_shape=[SDS((d, n_sub, DA), F32), SDS((d, n_sub, LANES), F32)],
        scratch_shapes=[pltpu.VMEM((kw, DA), BF16), pltpu.VMEM((kw, DA), BF16)],
        compiler_params=_params(("arbitrary", "arbitrary")),
    )(h16, h16, h16, h16, h16)


def _pooled(ext_ref, g, rows, tm):
    w = POOL_WINDOWS[g]
    cs = slice(g * GC, (g + 1) * GC)
    cur = ext_ref[HALO:HALO + tm, cs]
    win = cur
    for j in range(1, w):
        win = win + ext_ref[HALO - j:HALO - j + tm, cs]
    cnt = jnp.minimum(rows + 1, w).astype(F32)
    return win / cnt - cur, cnt


def _fill_ext(ext_ref, u_ref, uh_ref, blk, tm):
    @pl.when(blk == 0)
    def _():
        ext_ref[0:HALO, :] = jnp.zeros((HALO, DP), F32)

    @pl.when(blk > 0)
    def _():
        ext_ref[0:HALO, :] = uh_ref[...]

    ext_ref[HALO:HALO + tm, :] = u_ref[...]


def _residue_specs(tm, width):
    return [pl.BlockSpec((d, tm // d, width), lambda i: (0, i, 0)) for d in DILATIONS[1:]]


def _mix(o1, l1, o4, l4, o16, l16, h32, wp, scale):
    tm = 256
    n_slab = N_HEADS + 1

    def body(o1r, l1r, o4r, l4r, o16r, l16r, u_ref, uh_ref, ga_ref, gp_ref, wp_ref, sc_ref,
             y_ref, yt_ref, attn_ref, lse_ref, lse4_ref, lse16_ref, ext_ref, ys_ref, nat_ref, ls_ref):
        i = pl.program_id(0)
        for n, (d, o_r, l_r) in enumerate(((DILATIONS[1], o4r, l4r), (DILATIONS[2], o16r, l16r))):
            for r in range(d):
                rows = pl.ds(r, tm // d, stride=d)
                for h in range(N_HEADS):
                    nat_ref[n, h, rows, :] = o_r[r, :, h * DH:(h + 1) * DH]
                nat_ref[n, N_HEADS, rows, :] = l_r[r]
        la, lb, lc = l1r[...], nat_ref[0, N_HEADS], nat_ref[1, N_HEADS]
        mx = jnp.maximum(jnp.maximum(la, lb), lc)
        ea, eb, ec = jnp.exp(la - mx), jnp.exp(lb - mx), jnp.exp(lc - mx)
        z = ea + eb + ec
        wa, wb, wc = ea / z, eb / z, ec / z
        lse = mx + jnp.log(z)
        lse_ref[...] = lse
        ls_ref[...] = lse
        for d, out in ((DILATIONS[1], lse4_ref), (DILATIONS[2], lse16_ref)):
            for r in range(d):
                out[r] = ls_ref[pl.ds(r, tm // d, stride=d), :]
        for h in range(N_HEADS):
            hs = slice(h * DH, (h + 1) * DH)
            hc = slice(h, h + 1)
            attn = wa[:, hc] * o1r[:, hs] + wb[:, hc] * nat_ref[0, h] + wc[:, hc] * nat_ref[1, h]
            attn_ref[:, hs] = attn
            gt = ga_ref[:, hs]
            ys_ref[:, hs] = attn * (gt * jax.nn.sigmoid(gt))

        _fill_ext(ext_ref, u_ref, uh_ref, i, tm)
        rows = i * tm + lax.broadcasted_iota(jnp.int32, (tm, 1), 0)
        for g in range(N_GROUPS):
            cs = slice(g * GC, (g + 1) * GC)
            gs = slice(DA + g * GC, DA + (g + 1) * GC)
            pooled, _ = _pooled(ext_ref, g, rows, tm)
            po = _dot(pooled.astype(BF16), wp_ref[g]) * sc_ref[:, cs]
            gt = gp_ref[:, cs]
            ys_ref[:, gs] = po * (gt * jax.nn.sigmoid(gt))
        yv = ys_ref[...]
        y_ref[...] = yv.astype(BF16)
        yt_ref[...] = yv.T.astype(BF16)

    row = lambda i: (i, 0)
    blk = pl.BlockSpec((tm, DA), row)
    lanes = pl.BlockSpec((tm, LANES), row)
    o_res, l_res = _residue_specs(tm, DA), _residue_specs(tm, LANES)
    return pl.pallas_call(
        body, name="mix", grid=(S // tm,),
        in_specs=[blk, lanes, o_res[0], l_res[0], o_res[1], l_res[1],
                  pl.BlockSpec((tm, DP), row),
                  pl.BlockSpec((HALO, DP), lambda i: (jnp.maximum(i * (tm // HALO) - 1, 0), 0)),
                  pl.BlockSpec((tm, DA), lambda i: (i, 1)), pl.BlockSpec((tm, DP), lambda i: (i, 2)),
                  pl.BlockSpec((N_GROUPS, GC, GC), lambda i: (0, 0, 0)),
                  pl.BlockSpec((1, DP), lambda i: (0, 0))],
        out_specs=[pl.BlockSpec((tm, D), row), pl.BlockSpec((D, tm), lambda i: (0, i)), blk, lanes] + l_res,
        out_shape=[SDS((S, D), BF16), SDS((D, S), BF16), SDS((S, DA), F32), SDS((S, LANES), F32)]
        + [SDS((d, S // d, LANES), F32) for d in DILATIONS[1:]],
        scratch_shapes=[pltpu.VMEM((HALO + tm, DP), F32), pltpu.VMEM((tm, D), F32),
                        pltpu.VMEM((2, n_slab, tm, DH), F32), pltpu.VMEM((tm, LANES), F32)],
        compiler_params=_params(("arbitrary",)),
    )(o1, l1, o4, l4, o16, l16, h32, h32, h32, h32, wp, scale)


def _outproj_ln(y, wout, x, target, gain, bias):
    tm = 512
    te = 128

    def body(y_ref, w_ref, x_ref, t_ref, g_ref, b_ref, dz_ref, dzb_ref, st_ref, out_ref):
        i = pl.program_id(0)

        @pl.when(i == 0)
        def _():
            st_ref[...] = jnp.zeros((8, D), F32)

        out_ref[...] = _dot(y_ref[...], w_ref[...])
        gn = g_ref[...]
        for e in range(tm // te):
            rs = slice(e * te, (e + 1) * te)
            z = ALPHA * x_ref[rs, :] + out_ref[rs, :]
            mu = jnp.mean(z, axis=1, keepdims=True)
            zc = z - mu
            var = jnp.mean(zc * zc, axis=1, keepdims=True)
            rstd = lax.rsqrt(var + LN_EPS)
            xhat = zc * rstd
            diff = xhat * gn + b_ref[...] - t_ref[rs, :]
            dyln = diff / D
            st_ref[0:1, :] += jnp.sum(dyln * xhat, axis=0, keepdims=True)
            st_ref[1:2, :] += jnp.sum(dyln, axis=0, keepdims=True)
            row_loss = jnp.sum(diff * diff, axis=1, keepdims=True) / D
            st_ref[2:3, :] += jnp.broadcast_to(0.5 * jnp.sum(row_loss, axis=0, keepdims=True), (1, D))
            dxh = dyln * gn
            m1 = jnp.mean(dxh, axis=1, keepdims=True)
            m2 = jnp.mean(dxh * xhat, axis=1, keepdims=True)
            dz = rstd * (dxh - m1 - xhat * m2)
            dz_ref[rs, :] = dz
            dzb_ref[rs, :] = dz.astype(BF16)

    row = lambda i: (i, 0)
    const = lambda i: (0, 0)
    return pl.pallas_call(
        body, name="outproj_ln", grid=(S // tm,),
        in_specs=[pl.BlockSpec((tm, D), row),
                  pl.BlockSpec((D, D), const, pipeline_mode=pl.Buffered(1)),
                  pl.BlockSpec((tm, D), row), pl.BlockSpec((tm, D), row),
                  pl.BlockSpec((1, D), const), pl.BlockSpec((1, D), const)],
        out_specs=[pl.BlockSpec((tm, D), row), pl.BlockSpec((tm, D), row), pl.BlockSpec((8, D), const)],
        out_shape=[SDS((S, D), F32), SDS((S, D), BF16), SDS((8, D), F32)],
        scratch_shapes=[pltpu.VMEM((tm, D), F32)],
        compiler_params=_params(("arbitrary",)),
    )(y, wout, x, target, gain, bias)


def _bwd_mid(dzb, wout, h32, attn, wp, scale, rider=None):
    tm = 256
    n = S // tm

    def body(dz_ref, w_ref, ga_ref, gp_ref, at_ref, u_ref, uh_ref, wp_ref, sc_ref,
             dh_ref, do_ref, do4_ref, do16_ref, dd_ref, dd4_ref, dd16_ref, dwp_ref, dwp16_ref, gps_ref,
             ext_ref, eext_ref, acc_ref, nat_ref, ds_ref):
        i = pl.program_id(0)
        ib = n - 1 - i

        @pl.when(i == 0)
        def _():
            eext_ref[tm:tm + HALO, :] = jnp.zeros((HALO, DP), F32)
            acc_ref[...] = jnp.zeros((N_GROUPS, GC, GC), F32)
            gps_ref[...] = jnp.zeros((8, DP), F32)

        dy = _dot(dz_ref[...], w_ref[...], NT)

        def through_gate(dy_part, gt):
            sg = jax.nn.sigmoid(gt)
            return dy_part * (gt * sg), dy_part * (sg * (1.0 + gt * (1.0 - sg)))

        dat, dyg_a = through_gate(dy[:, 0:DA], ga_ref[...])
        dmix_p, dyg_p = through_gate(dy[:, DA:D], gp_ref[...])

        at = at_ref[...]
        do_ref[...] = dat.astype(BF16)
        dh_ref[:, DP:DP + DA] = (dyg_a * at).astype(BF16)
        prod = dat * at
        dd = jnp.zeros((tm, LANES), F32)
        for h in range(N_HEADS):
            hs = slice(h * DH, (h + 1) * DH)
            dd = _to_lane(dd, jnp.sum(prod[:, hs], axis=1, keepdims=True), h)
            nat_ref[h] = dat[:, hs]
        dd_ref[...] = dd
        ds_ref[...] = dd
        for d, do_out, dd_out in ((DILATIONS[1], do4_ref, dd4_ref), (DILATIONS[2], do16_ref, dd16_ref)):
            for r in range(d):
                dd_out[r] = _residues(ds_ref, (), d, r, tm // d)
                for h in range(N_HEADS):
                    do_out[r, :, h * DH:(h + 1) * DH] = _residues(nat_ref, (h,), d, r, tm // d).astype(BF16)

        _fill_ext(ext_ref, u_ref, uh_ref, ib, tm)
        rows = ib * tm + lax.broadcasted_iota(jnp.int32, (tm, 1), 0)
        for g in range(N_GROUPS):
            w = POOL_WINDOWS[g]
            cs = slice(g * GC, (g + 1) * GC)
            pooled, cnt = _pooled(ext_ref, g, rows, tm)
            pre = _dot(pooled.astype(BF16), wp_ref[g])
            sc = sc_ref[:, cs]
            dpo = dmix_p[:, cs]
            gps_ref[0:1, cs] += jnp.sum(dpo * pre, axis=0, keepdims=True)
            dh_ref[:, DP + DA + g * GC:DP + DA + (g + 1) * GC] = (dyg_p[:, cs] * (pre * sc)).astype(BF16)
            dpre = (dpo * sc).astype(BF16)
            acc_ref[g] += _dot(pooled.T.astype(BF16), dpre)
            dpooled = _dot(dpre, wp_ref[g], NT)
            eext_ref[0:tm, cs] = dpooled / cnt
            du = eext_ref[0:tm, cs]
            for j in range(1, w):
                du = du + eext_ref[j:j + tm, cs]
            dh_ref[:, cs] = (du - dpooled).astype(BF16)
        eext_ref[tm:tm + HALO, :] = eext_ref[0:HALO, :]

        @pl.when(i == n - 1)
        def _():
            for j in range(N_DEV):
                for g in range(N_GROUPS):
                    blk = acc_ref[g, j * PB:(j + 1) * PB, :]
                    dwp_ref[j, g * PB:(g + 1) * PB, :] = blk
                    dwp16_ref[j, g * PB:(g + 1) * PB, :] = blk.astype(BF16)

    rev = lambda i: (n - 1 - i, 0)
    const = lambda i: (0, 0)
    res = lambda width: [pl.BlockSpec((d, tm // d, width), lambda i: (0, n - 1 - i, 0)) for d in DILATIONS[1:]]
    pool_blocks = pl.BlockSpec((N_DEV, N_GROUPS * PB, GC), lambda i: (0, 0, 0))
    outs, carried = _call_carrying(
        body, rider, (n,),
        [pl.BlockSpec((tm, D), rev),
         pl.BlockSpec((D, D), const, pipeline_mode=pl.Buffered(1)),
         pl.BlockSpec((tm, DA), lambda i: (n - 1 - i, 1)), pl.BlockSpec((tm, DP), lambda i: (n - 1 - i, 2)),
         pl.BlockSpec((tm, DA), rev), pl.BlockSpec((tm, DP), rev),
         pl.BlockSpec((HALO, DP), lambda i: (jnp.maximum((n - 1 - i) * (tm // HALO) - 1, 0), 0)),
         pl.BlockSpec((N_GROUPS, GC, GC), lambda i: (0, 0, 0)),
         pl.BlockSpec((1, DP), const)],
        [pl.BlockSpec((tm, DP + D), lambda i: (n - 1 - i, 1)), pl.BlockSpec((tm, DA), rev)] + res(DA)
        + [pl.BlockSpec((tm, LANES), rev)] + res(LANES) + [pool_blocks, pool_blocks, pl.BlockSpec((8, DP), const)],
        [SDS((S, D_IN), BF16), SDS((S, DA), BF16)] + [SDS((d, S // d, DA), BF16) for d in DILATIONS[1:]]
        + [SDS((S, LANES), F32)] + [SDS((d, S // d, LANES), F32) for d in DILATIONS[1:]]
        + [SDS((N_DEV, N_GROUPS * PB, GC), F32), SDS((N_DEV, N_GROUPS * PB, GC), BF16), SDS((8, DP), F32)],
        [pltpu.VMEM((HALO + tm, DP), F32), pltpu.VMEM((tm + HALO, DP), F32),
         pltpu.VMEM((N_GROUPS, GC, GC), F32), pltpu.VMEM((N_HEADS, tm, DH), F32), pltpu.VMEM((tm, LANES), F32)],
        [dzb, wout, h32, h32, attn, h32, h32, wp, scale], "bwd_mid")
    return outs, carried


def _attn_bwd(h16, do, lse, dd, rope, name, others=None, dh=None):
    d, n_sub, _ = h16.shape
    n_i = n_sub // BQ
    n_kb = n_sub // KB
    qw = BQ + KB
    final = others is not None
    out_dtype = BF16 if final else F32
    n_cb = 3 * DA // DH

    def nxt(i):
        return jnp.minimum(2 * i + 2, n_kb - 1)

    def body(qc_ref, qn_ref, kc_ref, vc_ref, doc_ref, don_ref, lc_ref, ln_ref, dc_ref, dn_ref,
             c_ref, sn_ref, *rest):
        if final:
            acc4_ref, acc16_ref, _, out_ref, carry_ref, qw_ref, dow_ref, lw_ref, dw_ref, nat_ref = rest
        else:
            out_ref, carry_ref, qw_ref, dow_ref, lw_ref, dw_ref = rest
        i = pl.program_id(1)

        @pl.when(i == 0)
        def _():
            carry_ref[...] = jnp.zeros((KB, DA), F32)

        for win, own, after in ((qw_ref, qc_ref, qn_ref), (dow_ref, doc_ref, don_ref), (lw_ref, lc_ref, ln_ref),
                                (dw_ref, dc_ref, dn_ref)):
            win[0:BQ, :] = own[...]
            win[BQ:qw, :] = after[...]
        if final:
            for n, (dil, acc) in enumerate(((DILATIONS[1], acc4_ref), (DILATIONS[2], acc16_ref))):
                for r in range(dil):
                    for cb in range(n_cb):
                        nat_ref[n, cb, pl.ds(r, BQ // dil, stride=dil), :] = acc[r, :, cb * DH:(cb + 1) * DH]

        a = lax.broadcasted_iota(jnp.int32, (qw, BQ), 0)
        b = lax.broadcasted_iota(jnp.int32, (qw, BQ), 1)
        n_q = jnp.where(i == n_i - 1, BQ, qw)
        mask = (b <= a) & (a <= b + KB) & (a < n_q)
        tabs = (c_ref[...], sn_ref[...])
        for h in range(N_HEADS):
            hs = slice(h * DH, (h + 1) * DH)
            hc = slice(h, h + 1)
            q, k, vv, dob = qw_ref[:, hs], kc_ref[:, hs], vc_ref[:, hs], dow_ref[:, hs]
            s = _dot(q, k, NT) * SCALE
            p = jnp.exp(jnp.where(mask, s - lw_ref[:, hc], NEG))
            dp = _dot(dob, vv, NT)
            ds = (p * (dp - dw_ref[:, hc]) * SCALE).astype(BF16)
            dq = _dot(ds, k)
            dk = _dot(ds, q, T_N)
            dv = _dot(p.astype(BF16), dob, T_N)
            dq_lo = dq[0:KB] + carry_ref[:, hs]
            carry_ref[:, hs] = dq[BQ:qw]
            dq_own = _rope_t(jnp.concatenate([dq_lo, dq[KB:BQ]], axis=0), *tabs)
            for base, gv in ((0, dq_own), (DA, _rope_t(dk, *tabs)), (2 * DA, dv)):
                if final:
                    cb = base // DH + h
                    gv = gv + nat_ref[0, cb] + nat_ref[1, cb]
                out_ref[:, base + h * DH:base + (h + 1) * DH] = gv.astype(out_dtype)

    def cur(rows, width, col=0):
        return pl.BlockSpec((None, rows, width), lambda r, i: (r, i, col))

    def nx(width, col=0):
        return pl.BlockSpec((None, KB, width), lambda r, i: (r, nxt(i), col))

    in_specs = [cur(BQ, DA), nx(DA), cur(BQ, DA, 1), cur(BQ, DA, 2), cur(BQ, DA), nx(DA),
                cur(BQ, LANES), nx(LANES), cur(BQ, LANES), nx(LANES)] + [cur(BQ, DH)] * 2
    args = [h16, h16, h16, h16, do, do, lse, lse, dd, dd] + list(rope)
    scratch = [pltpu.VMEM((KB, DA), F32), pltpu.VMEM((qw, DA), BF16), pltpu.VMEM((qw, DA), BF16),
               pltpu.VMEM((qw, LANES), F32), pltpu.VMEM((qw, LANES), F32)]
    if final:
        assert d == 1
        in_specs += [pl.BlockSpec((dil, BQ // dil, 3 * DA), lambda r, i: (0, i, 0)) for dil in DILATIONS[1:]]
        in_specs.append(pl.BlockSpec(memory_space=pl.ANY))
        args += list(others) + [dh[None]]
        scratch.append(pltpu.VMEM((2, n_cb, BQ, DH), F32))
    return pl.pallas_call(
        body, name=name, grid=(d, n_i),
        in_specs=in_specs,
        out_specs=cur(BQ, 3 * DA),
        out_shape=SDS((d, n_sub, D_IN if final else 3 * DA), out_dtype),
        input_output_aliases={len(args) - 1: 0} if final else {},
        scratch_shapes=scratch,
        compiler_params=_params(("arbitrary", "arbitrary")),
    )(*args)


def _grad_w_out(yt, dzb):
    tn = 256

    def body(yt_ref, dz_ref, o_ref, o16_ref):
        acc = _dot(yt_ref[...], dz_ref[...])
        o_ref[...] = acc
        o16_ref[...] = acc.astype(BF16)

    cols = pl.BlockSpec((D, tn), lambda c: (0, c))
    o, o16 = pl.pallas_call(
        body, name="grad_w_out", grid=(D // tn,),
        in_specs=[pl.BlockSpec((D, S), lambda c: (0, 0), pipeline_mode=pl.Buffered(1)),
                  pl.BlockSpec((S, tn), lambda c: (0, c))],
        out_specs=[cols, cols],
        out_shape=[SDS((D, D), F32), SDS((D, D), BF16)],
        compiler_params=_params(("arbitrary",)),
    )(yt, dzb)
    return o.reshape(N_DEV, OB, D), o16.reshape(N_DEV, OB, D)


def _grad_w_in(xt, dh):
    tn = 256
    per_blk = WB // tn

    def body(xt_ref, dh_ref, o_ref, o16_ref):
        acc = _dot(xt_ref[...], dh_ref[...])
        o_ref[...] = acc
        o16_ref[...] = acc.astype(BF16)

    blk = pl.BlockSpec((None, D, tn), lambda c: (c // per_blk, 0, c % per_blk))
    return pl.pallas_call(
        body, name="grad_w_in", grid=(D_IN // tn,),
        in_specs=[pl.BlockSpec((D, S), lambda c: (0, 0), pipeline_mode=pl.Buffered(1)),
                  pl.BlockSpec((S, tn), lambda c: (0, c))],
        out_specs=[blk, blk],
        out_shape=[SDS((N_DEV, D, WB), F32), SDS((N_DEV, D, WB), BF16)],
        compiler_params=_params(("arbitrary",)),
    )(xt, dh)


def _grad_x(dz, dh, wgs, rider=None):
    tm = 1024

    def body(dz_ref, dh_ref, *rest):
        o_ref = rest[-1]

        @pl.when(pl.program_id(1) == 0)
        def _():
            o_ref[...] = ALPHA * dz_ref[...]

        acc = _dot(dh_ref[:, 0:TN], rest[0][...], NT)
        for s in range(1, W_PHASES):
            acc = acc + _dot(dh_ref[:, s * TN:(s + 1) * TN], rest[s][...], NT)
        o_ref[...] += acc

    outs, carried = _call_carrying(
        body, rider, (S // tm, N_DEV),
        [pl.BlockSpec((tm, D), lambda i, j: (i, 0)), pl.BlockSpec((tm, WB), lambda i, j: (i, j))]
        + [pl.BlockSpec((None, D, TN), lambda i, j: (j, 0, 0))] * W_PHASES,
        [pl.BlockSpec((tm, D), lambda i, j: (i, 0))], [SDS((S, D), F32)], [], [dz, dh] + list(wgs), "grad_x")
    return outs[0], carried


def _place():
    x, y, c = lax.axis_index("x"), lax.axis_index("y"), lax.axis_index("c")
    chips = [(x, y), (1 - x, y), (x, 1 - y), (1 - x, 1 - y)]
    return x, y, c, chips


def _blk(x, y, c):
    return 4 * x + 2 * y + c


def _adamw(w, g, m, v):
    m = ADAM_B1 * m + (1.0 - ADAM_B1) * g
    v = ADAM_B2 * v + (1.0 - ADAM_B2) * (g * g)
    m_hat = m / (1.0 - ADAM_B1 ** ADAM_STEP)
    v_hat = v / (1.0 - ADAM_B2 ** ADAM_STEP)
    delta = -ADAM_LR * (m_hat / (jnp.sqrt(v_hat) + ADAM_EPS) + ADAM_WD * w)
    return delta, m, v


class _Rider:
    def __init__(self, args, in_specs, out_shape, out_specs, scratch, start, finish):
        self.args, self.in_specs, self.out_shape, self.out_specs = args, in_specs, out_shape, out_specs
        self.scratch, self.start, self.finish = scratch, start, finish


def _carry(body, rider, n_in, n_out, first, last):
    if rider is None:
        return body
    r_in, r_out, r_scr = len(rider.args), len(rider.out_shape), len(rider.scratch)

    def carrying(*refs):
        o0 = n_in + r_in
        s0 = o0 + n_out + r_out
        s1 = len(refs) - r_scr
        theirs = (refs[n_in:o0], refs[o0 + n_out:s0], refs[s1:])
        pl.when(first())(lambda: rider.start(*theirs))
        body(*refs[:n_in], *refs[o0:o0 + n_out], *refs[s0:s1])
        pl.when(last())(lambda: rider.finish(*theirs))

    return carrying


def _call_carrying(body, rider, grid, in_specs, out_specs, out_shape, scratch, args, name, aliases=None):
    n_in, n_out = len(in_specs), len(out_specs)
    ids = lambda: [pl.program_id(a) for a in range(len(grid))]
    first = lambda: functools.reduce(jnp.logical_and, [i == 0 for i in ids()])
    last = lambda: functools.reduce(jnp.logical_and, [i == n - 1 for i, n in zip(ids(), grid)])
    if rider is not None:
        in_specs, args = in_specs + rider.in_specs, list(args) + rider.args
        out_specs, out_shape = out_specs + rider.out_specs, out_shape + rider.out_shape
        scratch = scratch + rider.scratch
    outs = pl.pallas_call(
        _carry(body, rider, n_in, n_out, first, last), name=name, grid=grid,
        in_specs=in_specs, out_specs=out_specs, out_shape=out_shape, scratch_shapes=scratch,
        input_output_aliases=aliases or {}, compiler_params=_params(("arbitrary",) * len(grid)),
    )(*args)
    return list(outs[:n_out]), list(outs[n_out:])


def _gather_copy(tensors, send_sems, recv_sems, t, k, block, to, src=None):
    dst = tensors[t][1](_blk(*block))
    return pltpu.make_async_remote_copy(
        src_ref=dst if src is None else src, dst_ref=dst,
        send_sem=send_sems.at[t, k], recv_sem=recv_sems.at[t, k], device_id=to, device_id_type=MESH)


def _gather_start(tensors, send_sems, recv_sems, local_sems):
    x, y, c, chips = _place()
    me, sib = (x, y, c), (x, y, 1 - c)
    for t, (src, dst) in enumerate(tensors):
        pltpu.make_async_copy(src, dst(_blk(*me)), local_sems.at[t]).start()
        _gather_copy(tensors, send_sems, recv_sems, t, 0, me, sib, src).start()
        for j in (1, 2, 3):
            _gather_copy(tensors, send_sems, recv_sems, t, j, me, (*chips[j], c), src).start()


def _gather_finish(tensors, send_sems, recv_sems, local_sems):
    x, y, c, chips = _place()
    me, sib = (x, y, c), (x, y, 1 - c)
    copy = functools.partial(_gather_copy, tensors, send_sems, recv_sems)
    for t in range(len(tensors)):
        for j in (1, 2, 3):
            copy(t, j, (*chips[j], c), me).wait_recv()
            copy(t, 3 + j, (*chips[j], c), sib).start()
    for t, (src, dst) in enumerate(tensors):
        copy(t, 0, sib, me).wait_recv()
        for j in (1, 2, 3):
            copy(t, 3 + j, (*chips[j], 1 - c), me).wait_recv()
        copy(t, 0, me, sib, src).wait_send()
        for j in (1, 2, 3):
            copy(t, j, me, (*chips[j], c), src).wait_send()
            copy(t, 3 + j, (*chips[j], c), sib).wait_send()
        pltpu.make_async_copy(src, dst(_blk(*me)), local_sems.at[t]).wait()


W_PHASES = WB // TN


def _gather_w_in_first(w_in_s, w_out_s, w_pool_s):
    def body(win_ref, wout_ref, wpool_ref, in16_ref, out16_ref, pool16_ref, g_ref, bin_ref, bout_ref, bpool_ref,
             send_sems, recv_sems, local_sems, keep_sems):
        keeps = []
        for n, (src, buf, dst) in enumerate(((win_ref, bin_ref, in16_ref), (wout_ref, bout_ref, out16_ref),
                                             (wpool_ref, bpool_ref, pool16_ref))):
            buf[...] = src[...].astype(BF16)
            keeps.append(pltpu.make_async_copy(buf, dst, keep_sems.at[n]))
            keeps[-1].start()
        tensors = [(bin_ref.at[:, pl.ds(0, TN)], lambda b: g_ref.at[b])]
        _gather_start(tensors, send_sems, recv_sems, local_sems)
        _gather_finish(tensors, send_sems, recv_sems, local_sems)
        for keep in keeps:
            keep.wait()

    vmem = pl.BlockSpec(memory_space=pltpu.VMEM)
    hbm = pl.BlockSpec(memory_space=pl.ANY)
    shapes = [(D, WB), (OB, D), (N_GROUPS, PB, GC)]
    return pl.pallas_call(
        body, name="gather_w_in_0",
        in_specs=[vmem] * 3, out_specs=[hbm] * 4,
        out_shape=[SDS(sh, BF16) for sh in shapes] + [SDS((N_DEV, D, TN), BF16)],
        scratch_shapes=[pltpu.VMEM(sh, BF16) for sh in shapes]
        + [pltpu.SemaphoreType.DMA((1, 7)), pltpu.SemaphoreType.DMA((1, 7)), pltpu.SemaphoreType.DMA((1,)),
           pltpu.SemaphoreType.DMA((3,))],
        compiler_params=_params(),
    )(w_in_s, w_out_s, w_pool_s)


def _gather_w_in_rider(shard16, s):
    def tensors(ins, outs):
        return [(ins[0].at[:, pl.ds(s * TN, TN)], lambda b: outs[0].at[b])]

    hbm = pl.BlockSpec(memory_space=pl.ANY)
    return _Rider(
        args=[shard16], in_specs=[hbm], out_shape=[SDS((N_DEV, D, TN), BF16)], out_specs=[hbm],
        scratch=[pltpu.SemaphoreType.DMA((1, 7)), pltpu.SemaphoreType.DMA((1, 7)), pltpu.SemaphoreType.DMA((1,))],
        start=lambda ins, outs, scr: _gather_start(tensors(ins, outs), *scr),
        finish=lambda ins, outs, scr: _gather_finish(tensors(ins, outs), *scr))


def _gather_small_rider(w_out16, w_pool16):
    def tensors(ins, outs):
        gout_ref, gpool_ref = outs

        def pool_rows(b):
            return gpool_ref.at[:, pl.ds(pl.multiple_of(b * PB, PB), PB), :]

        return [(ins[0], lambda b: gout_ref.at[b]), (ins[1], pool_rows)]

    hbm = pl.BlockSpec(memory_space=pl.ANY)
    return _Rider(
        args=[w_out16, w_pool16], in_specs=[hbm, hbm],
        out_shape=[SDS((N_DEV, OB, D), BF16), SDS((N_GROUPS, GC, GC), BF16)], out_specs=[hbm, hbm],
        scratch=[pltpu.SemaphoreType.DMA((2, 7)), pltpu.SemaphoreType.DMA((2, 7)), pltpu.SemaphoreType.DMA((2,))],
        start=lambda ins, outs, scr: _gather_start(tensors(ins, outs), *scr),
        finish=lambda ins, outs, scr: _gather_finish(tensors(ins, outs), *scr))


def _block_table():
    x, y, c, chips = _place()
    return jnp.stack([_blk(*chip, c) for chip in chips]).astype(jnp.int32)


def _rs_exchange(name, p16):
    _, r_tot, cols = p16.shape

    def body(p16_ref, r1_ref, send_sems, recv_sems):
        x, y, c, chips = _place()
        copies = [pltpu.make_async_remote_copy(
            src_ref=p16_ref.at[_blk(*chips[k], 1 - c)], dst_ref=r1_ref.at[k],
            send_sem=send_sems.at[k], recv_sem=recv_sems.at[k], device_id=(x, y, 1 - c), device_id_type=MESH)
            for k in range(4)]
        for cp in copies:
            cp.start()
        for cp in copies:
            cp.wait()

    hbm = pl.BlockSpec(memory_space=pl.ANY)
    return pl.pallas_call(
        body, name=name, in_specs=[hbm], out_specs=hbm, out_shape=SDS((4, r_tot, cols), BF16),
        scratch_shapes=[pltpu.SemaphoreType.DMA((4,)), pltpu.SemaphoreType.DMA((4,))],
        compiler_params=_params(),
    )(p16)


def _chip_sums(name, table, p32, r1, rc):
    _, r_tot, cols = p32.shape

    def body(tbl_ref, p_ref, r_ref, o_ref):
        o_ref[...] = (p_ref[...] + r_ref[...].astype(F32)).astype(BF16)

    return pl.pallas_call(
        body, name=name,
        grid_spec=pltpu.PrefetchScalarGridSpec(
            num_scalar_prefetch=1, grid=(3, r_tot // rc),
            in_specs=[pl.BlockSpec((None, rc, cols), lambda k, ch, tbl: (tbl[k + 1], ch, 0)),
                      pl.BlockSpec((None, rc, cols), lambda k, ch, tbl: (k + 1, ch, 0))],
            out_specs=pl.BlockSpec((None, rc, cols), lambda k, ch, tbl: (k, ch, 0))),
        out_shape=SDS((3, r_tot, cols), BF16),
        compiler_params=_params(("arbitrary", "arbitrary")),
    )(table, p32, r1)


def _stage2_rider(sums, stats=None):
    n_t = len(sums)

    def copies(ins, outs, scr):
        x, y, c, chips = _place()
        out = []
        for t in range(n_t):
            for k in (1, 2, 3):
                out.append(pltpu.make_async_remote_copy(
                    src_ref=ins[t].at[k - 1], dst_ref=outs[t].at[k - 1],
                    send_sem=scr[0].at[t, k - 1], recv_sem=scr[1].at[t, k - 1],
                    device_id=(*chips[k], c), device_id_type=MESH))
        if stats is not None:
            for k in range(1, N_DEV):
                peer = (x ^ ((k >> 2) & 1), y ^ ((k >> 1) & 1), c ^ (k & 1))
                out.append(pltpu.make_async_remote_copy(
                    src_ref=scr[4], dst_ref=outs[n_t].at[_blk(x, y, c)],
                    send_sem=scr[2].at[k - 1], recv_sem=scr[3].at[k - 1], device_id=peer, device_id_type=MESH))
        return out

    def own_rows(outs, scr):
        x, y, c, _ = _place()
        return pltpu.make_async_copy(scr[4], outs[n_t].at[_blk(x, y, c)], scr[5])

    def start(ins, outs, scr):
        if stats is not None:
            scr[4][...] = ins[n_t][...]
            scr[4][3:4, 0:DP] = ins[n_t + 1][0:1, :]
            own_rows(outs, scr).start()
        for cp in copies(ins, outs, scr):
            cp.start()

    def finish(ins, outs, scr):
        for cp in copies(ins, outs, scr):
            cp.wait()
        if stats is not None:
            own_rows(outs, scr).wait()

    vmem = pl.BlockSpec(memory_space=pltpu.VMEM)
    hbm = pl.BlockSpec(memory_space=pl.ANY)
    scratch = [pltpu.SemaphoreType.DMA((n_t, 3)), pltpu.SemaphoreType.DMA((n_t, 3))]
    args, in_specs = list(sums), [hbm] * n_t
    out_shape, out_specs = [SDS(s.shape, BF16) for s in sums], [hbm] * n_t
    if stats is not None:
        scratch += [pltpu.SemaphoreType.DMA((N_DEV - 1,)), pltpu.SemaphoreType.DMA((N_DEV - 1,)),
                    pltpu.VMEM((8, D), F32), pltpu.SemaphoreType.DMA(())]
        args, in_specs = args + list(stats), in_specs + [vmem, vmem]
        out_shape, out_specs = out_shape + [SDS((N_DEV, 8, D), F32)], out_specs + [hbm]
    return _Rider(args, in_specs, out_shape, out_specs, scratch, start, finish)


def _adamw_shard(name, table, p32, r1, r2, w, m, v, rc):
    _, r_tot, cols = p32.shape

    def body(tbl_ref, p_ref, r1_ref, r2_ref, w_ref, m_ref, v_ref, g_ref, d_ref, nm_ref, nv_ref):
        g = p_ref[...] + r1_ref[...].astype(F32)
        for k in range(3):
            g = g + r2_ref[k].astype(F32)
        delta, nm, nv = _adamw(w_ref[...], g, m_ref[...], v_ref[...])
        g_ref[...] = g
        d_ref[...] = delta
        nm_ref[...] = nm
        nv_ref[...] = nv

    rows = pl.BlockSpec((rc, cols), lambda ch, tbl: (ch, 0))
    shard = SDS((r_tot, cols), F32)
    return pl.pallas_call(
        body, name=name,
        grid_spec=pltpu.PrefetchScalarGridSpec(
            num_scalar_prefetch=1, grid=(r_tot // rc,),
            in_specs=[pl.BlockSpec((None, rc, cols), lambda ch, tbl: (tbl[0], ch, 0)),
                      pl.BlockSpec((None, rc, cols), lambda ch, tbl: (0, ch, 0)),
                      pl.BlockSpec((3, rc, cols), lambda ch, tbl: (0, ch, 0)), rows, rows, rows],
            out_specs=[rows, rows, rows, rows]),
        out_shape=[shard, shard, shard, shard],
        compiler_params=_params(("arbitrary",)),
    )(table, p32, r1, r2, w, m, v)


def _replicated_adamw(gathered, gain, bias, scale, m_gain, m_bias, m_scale, v_gain, v_bias, v_scale):
    def body(all_ref, g_ref, b_ref, s_ref, mg_ref, mb_ref, ms_ref, vg_ref, vb_ref, vs_ref,
             tot_ref, dl_ref, nm_ref, nv_ref):
        tot = all_ref[0]
        for b in range(1, N_DEV):
            tot = tot + all_ref[b]
        tot_ref[...] = tot
        dl_ref[...] = jnp.zeros((8, D), F32)
        nm_ref[...] = jnp.zeros((8, D), F32)
        nv_ref[...] = jnp.zeros((8, D), F32)
        for row, width, w_r, m_r, v_r in ((0, D, g_ref, mg_ref, vg_ref), (1, D, b_ref, mb_ref, vb_ref),
                                          (3, DP, s_ref, ms_ref, vs_ref)):
            delta, nm, nv = _adamw(w_r[...], tot[row:row + 1, 0:width], m_r[...], v_r[...])
            dl_ref[row:row + 1, 0:width] = delta
            nm_ref[row:row + 1, 0:width] = nm
            nv_ref[row:row + 1, 0:width] = nv

    vmem = pl.BlockSpec(memory_space=pltpu.VMEM)
    rows = SDS((8, D), F32)
    return pl.pallas_call(
        body, name="replicated_adamw",
        in_specs=[vmem] * 10, out_specs=[vmem] * 4, out_shape=[rows, rows, rows, rows],
        compiler_params=_params(),
    )(gathered, gain, bias, scale, m_gain, m_bias, m_scale, v_gain, v_bias, v_scale)


def kernel(x, w_in, w_pool, pool_scale, w_out, ln_gain, ln_bias, loss_target, m_w_in, m_w_pool, m_pool_scale, m_w_out, m_ln_gain, m_ln_bias, v_w_in, v_w_pool, v_pool_scale, v_w_out, v_ln_gain, v_ln_bias):
    pool_rows = (N_GROUPS * PB, GC)
    x2, target = x[0], loss_target[0]
    rope = lax.optimization_barrier(_rope_tables())

    table = _block_table()
    by_res = lambda t, d: t.reshape(S // d, d, DH).transpose(1, 0, 2)
    ropes = [[by_res(t, d) for t in rope] for d in DILATIONS]

    shard16, w_out16, w_pool16, wg0 = _gather_w_in_first(w_in[0], w_out[0], w_pool[0])
    xb, xt = _prep_x(x2)
    h, (wg1,) = _proj_phase(xb, wg0, 0, rope, None, _gather_w_in_rider(shard16, 1))
    h, (wg2,) = _proj_phase(xb, wg1, 1, rope, h, _gather_w_in_rider(shard16, 2))
    h, (wg_out, wg_pool) = _proj_phase(xb, wg2, 2, rope, h, _gather_small_rider(w_out16, w_pool16))
    wg_out = wg_out.reshape(D, D)
    h16, h32 = [h[0][None], h[1], h[2]], h[3]
    fwd = [_attn_fwd(h16[n], f"attn_fwd_d{d}") for n, d in enumerate(DILATIONS)]
    y, yt, attn, lse1, lse4, lse16 = _mix(fwd[0][0][0], fwd[0][1][0], *fwd[1], *fwd[2], h32, wg_pool, pool_scale)
    dz, dzb, stats = _outproj_ln(y, wg_out, x2, target, ln_gain, ln_bias)

    dwout, dwout16 = _grad_w_out(yt, dzb)
    r1_out = _rs_exchange("rs_out_exchange", dwout16)
    s2_out = _chip_sums("rs_out_sums", table, dwout, r1_out, OB)
    mid, (r2_out,) = _bwd_mid(dzb, wg_out, h32, attn, wg_pool, pool_scale, rider=_stage2_rider([s2_out]))
    dh, do1, do4, do16, dd1, dd4, dd16, dwp, dwp16, gps = mid
    g_out, d_out, nm_out, nv_out = _adamw_shard(
        "adamw_w_out", table, dwout, r1_out, r2_out, w_out[0], m_w_out[0], v_w_out[0], OB // 2)

    do, lse, dd = [do1[None], do4, do16], [lse1[None], lse4, lse16], [dd1[None], dd4, dd16]
    others = [_attn_bwd(h16[n], do[n], lse[n], dd[n], ropes[n], f"attn_bwd_d{DILATIONS[n]}") for n in (1, 2)]
    dh = _attn_bwd(h16[0], do[0], lse[0], dd[0], ropes[0], "attn_bwd_d1", others=others, dh=dh)[0]

    dwin, dwin16 = _grad_w_in(xt, dh)
    r1_in = _rs_exchange("rs_in_exchange", dwin16)
    s2_in = _chip_sums("rs_in_sums", table, dwin, r1_in, 512)
    r1_pool = _rs_exchange("rs_pool_exchange", dwp16)
    s2_pool = _chip_sums("rs_pool_sums", table, dwp, r1_pool, N_GROUPS * PB)
    grad_x, (r2_in, r2_pool, gathered) = _grad_x(
        dz, dh, (wg0, wg1, wg2), rider=_stage2_rider([s2_in, s2_pool], stats=(stats, gps)))
    g_in, d_in, nm_in, nv_in = _adamw_shard(
        "adamw_w_in", table, dwin, r1_in, r2_in, w_in[0], m_w_in[0], v_w_in[0], 256)
    g_pool, d_pool, nm_pool, nv_pool = _adamw_shard(
        "adamw_w_pool", table, dwp, r1_pool, r2_pool, w_pool[0].reshape(pool_rows), m_w_pool[0].reshape(pool_rows),
        v_w_pool[0].reshape(pool_rows), N_GROUPS * PB)
    tot, dl, nm, nv = _replicated_adamw(gathered, ln_gain, ln_bias, pool_scale, m_ln_gain, m_ln_bias, m_pool_scale,
                                        v_ln_gain, v_ln_bias, v_pool_scale)

    shard4 = lambda t: t.reshape(1, N_GROUPS, PB, GC)
    lead = lambda t: t[None]
    small = lambda t: (t[3:4, 0:DP], t[0:1], t[1:2])
    g_ps, g_gain, g_bias = small(tot)
    d_ps, d_gain, d_bias = small(dl)
    nm_ps, nm_gain, nm_bias = small(nm)
    nv_ps, nv_gain, nv_bias = small(nv)
    return (tot[2, 0], lead(grad_x),
            lead(g_in), shard4(g_pool), g_ps, lead(g_out), g_gain, g_bias,
            lead(d_in), shard4(d_pool), d_ps, lead(d_out), d_gain, d_bias,
            lead(nm_in), shard4(nm_pool), nm_ps, lead(nm_out), nm_gain, nm_bias,
            lead(nv_in), shard4(nv_pool), nv_ps, lead(nv_out), nv_gain, nv_bias)
```

```python
import functools

import jax
import jax.numpy as jnp
from jax import lax
from jax.experimental import pallas as pl
from jax.experimental.pallas import tpu as pltpu

F32 = jnp.float32
BF16 = jnp.bfloat16
SDS = jax.ShapeDtypeStruct
MESH = pl.DeviceIdType.MESH

N_DEV = 8
S = 4096
D = 2048
N_HEADS = 8
DH = 128
DA = N_HEADS * DH
DP = 1024
N_GROUPS = 4
GC = DP // N_GROUPS
POOL_WINDOWS = (2, 4, 8, 16)
HALO = 16
D_IN = 3 * DA + DP + D
WB = D_IN // N_DEV
TN = 256
HW = 3 * DA
OB = D // N_DEV
PB = GC // N_DEV
ROPE_DIM = DH // 4
ROPE_HALF = ROPE_DIM // 2
ROPE_THETA = 500000.0
DILATIONS = (1, 4, 16)
KB = 128
LN_EPS = 1e-5
ALPHA = 2.0 ** 0.25
SCALE = DH ** -0.5
NEG = -1e30
ADAM_LR, ADAM_B1, ADAM_B2, ADAM_EPS, ADAM_WD, ADAM_STEP = 0.001, 0.9, 0.999, 1e-08, 0.01, 10

VMEM_LIMIT_V7X = 60 * 1024 * 1024

NT = (((1,), (1,)), ((), ()))
T_N = (((0,), (0,)), ((), ()))


def _params(sem=None):
    return pltpu.CompilerParams(dimension_semantics=sem, vmem_limit_bytes=VMEM_LIMIT_V7X)


def _dot(a, b, dims=None):
    if dims is None:
        return jnp.dot(a, b, preferred_element_type=F32)
    return lax.dot_general(a, b, dims, preferred_element_type=F32)


def _rope_tables():
    inv_freq = ROPE_THETA ** (-(2.0 * jnp.arange(ROPE_HALF, dtype=F32)) / ROPE_DIM)
    ang = jnp.arange(S, dtype=jnp.int32).astype(F32)[:, None] * inv_freq[None, :]
    cos, sin = jnp.cos(ang), jnp.sin(ang)
    rest = DH - ROPE_DIM
    c = jnp.concatenate([cos, cos, jnp.ones((S, rest), F32)], axis=1)
    sn = jnp.concatenate([-sin, sin, jnp.zeros((S, rest), F32)], axis=1)
    return c, sn


def _rope_partner(t):
    lane = lax.broadcasted_iota(jnp.int32, t.shape, 1)
    return jnp.where(lane < ROPE_HALF, pltpu.roll(t, DH - ROPE_HALF, 1), pltpu.roll(t, ROPE_HALF, 1))


def _rope(t, c, sn):
    return t * c + _rope_partner(t) * sn


def _rope_t(g, c, sn):
    return g * c - _rope_partner(g) * sn


def _prep_x(x):
    tm = 512

    def body(x_ref, xb_ref, xt_ref):
        xv = x_ref[...]
        xb_ref[...] = xv.astype(BF16)
        xt_ref[...] = xv.T.astype(BF16)

    return pl.pallas_call(
        body, name="prep_x", grid=(S // tm,),
        in_specs=[pl.BlockSpec((tm, D), lambda i: (i, 0))],
        out_specs=[pl.BlockSpec((tm, D), lambda i: (i, 0)), pl.BlockSpec((D, tm), lambda i: (0, i))],
        out_shape=[SDS((S, D), BF16), SDS((D, S), BF16)],
        compiler_params=_params(("arbitrary",)),
    )(x)


def _residues(slab_ref, idx, d, r, n):
    return slab_ref[(*idx, pl.ds(r, n, stride=d), slice(None))]


def _proj_phase(xb, wg, s, rope, prev, rider):
    n_heads = TN // DH
    n16 = HW // TN
    j16 = n16 // W_PHASES
    n_rope = 2 * DA // TN
    n_part = 2
    once = pl.Buffered(1)

    def body(x_ref, w_ref, c_ref, sn_ref, *rest):
        h16_ref, h4_ref, h16r_ref, h32_ref, acc_ref, slab_ref, res_ref = rest[-7:]
        j = pl.program_id(0)
        rc = 64
        d4, d16 = DILATIONS[1], DILATIONS[2]
        sh = S // n_part
        n4, n16 = sh // d4, sh // d16

        def heads(with_rope):
            for part in range(n_part):
                acc_ref[...] = _dot(x_ref[part * sh:(part + 1) * sh, :], w_ref[...])
                for hh in range(n_heads):
                    hs = slice(hh * DH, (hh + 1) * DH)

                    def rope_rows(ci, carry):
                        rows = pl.ds(pl.multiple_of(ci * rc, rc), rc)
                        tok = pl.ds(pl.multiple_of(part * sh + ci * rc, rc), rc)
                        t = acc_ref[rows, hs]
                        if with_rope:
                            t = _rope(t, c_ref[tok, :], sn_ref[tok, :])
                        h16_ref[tok, hs] = t.astype(BF16)
                        slab_ref[rows, :] = t
                        return carry

                    lax.fori_loop(0, sh // rc, rope_rows, 0)
                    for q in range(d4):
                        for c0 in range(0, n4, KB):
                            t = slab_ref[pl.ds(q + d4 * c0, KB, stride=d4), :]
                            h4_ref[q, part * n4 + c0:part * n4 + c0 + KB, hs] = t.astype(BF16)
                            res_ref[c0:c0 + KB, :] = t
                        for m in range(d16 // d4):
                            t = res_ref[pl.ds(m, n16, stride=d4), :]
                            h16r_ref[d4 * m + q, part * n16:(part + 1) * n16, hs] = t.astype(BF16)

        pl.when(W_PHASES * j + s < n_rope)(lambda: heads(True))
        pl.when((W_PHASES * j + s >= n_rope) & (j < j16))(lambda: heads(False))

        @pl.when(j >= j16)
        def _():
            h32_ref[...] = _dot(x_ref[...], w_ref[...])

    col16 = lambda j: W_PHASES * jnp.minimum(j, j16 - 1) + s
    col32 = lambda j: W_PHASES * jnp.maximum(j, j16) + s - n16
    hbm = pl.BlockSpec(memory_space=pl.ANY)
    in_specs = [pl.BlockSpec((S, D), lambda j: (0, 0), pipeline_mode=once),
                pl.BlockSpec((None, D, TN), lambda j: (j, 0, 0))]
    in_specs += [pl.BlockSpec((S, DH), lambda j: (0, 0), pipeline_mode=once)] * 2
    args = [xb, wg] + list(rope)
    aliases = {}
    if prev is not None:
        aliases = {len(args) + n: n for n in range(4)}
        in_specs, args = in_specs + [hbm] * 4, args + list(prev)
    out_specs = [pl.BlockSpec((S, TN), lambda j: (0, col16(j)))]
    out_specs += [pl.BlockSpec((d, S // d, TN), lambda j: (0, 0, col16(j))) for d in DILATIONS[1:]]
    out_specs += [pl.BlockSpec((S, TN), lambda j: (0, col32(j)))]
    out_shape = [SDS((S, HW), BF16)] + [SDS((d, S // d, HW), BF16) for d in DILATIONS[1:]] + [SDS((S, HW), F32)]
    return _call_carrying(
        body, rider, (N_DEV,), in_specs, out_specs, out_shape,
        [pltpu.VMEM((S // n_part, TN), F32), pltpu.VMEM((S // n_part, DH), F32),
         pltpu.VMEM((S // n_part // DILATIONS[1], DH), F32)], args,
        f"proj_phase{s}", aliases)


BQ = 2 * KB
LANES = 128


def _to_lane(acc, col, h):
    lane = lax.broadcasted_iota(jnp.int32, acc.shape, 1)
    return jnp.where(lane == h, col, acc)


def _attn_fwd(h16, name):
    d, n_sub, _ = h16.shape
    n_i = n_sub // BQ
    kw = KB + BQ

    def prev(i):
        return jnp.maximum(2 * i - 1, 0)

    def body(q_ref, kc_ref, kp_ref, vc_ref, vp_ref, o_ref, l_ref, kw_ref, vw_ref):
        i = pl.program_id(1)
        kw_ref[0:KB, :] = kp_ref[...]
        kw_ref[KB:kw, :] = kc_ref[...]
        vw_ref[0:KB, :] = vp_ref[...]
        vw_ref[KB:kw, :] = vc_ref[...]
        a = lax.broadcasted_iota(jnp.int32, (KB, 2 * KB), 0)
        b = lax.broadcasted_iota(jnp.int32, (KB, 2 * KB), 1)
        band = (b >= a) & (b <= a + KB)
        first_key = jnp.where(i == 0, KB, 0)
        masks = (band & (b >= first_key), band)
        for half in range(2):
            rs = slice(half * KB, (half + 1) * KB)
            ks = slice(half * KB, (half + 2) * KB)
            lse = jnp.zeros((KB, LANES), F32)
            for h in range(N_HEADS):
                hs = slice(h * DH, (h + 1) * DH)
                s = jnp.where(masks[half], _dot(q_ref[rs, hs], kw_ref[ks, hs], NT) * SCALE, NEG)
                m = jnp.max(s, axis=1, keepdims=True)
                p = jnp.exp(s - m)
                den = jnp.sum(p, axis=1, keepdims=True)
                o_ref[rs, hs] = _dot(p.astype(BF16), vw_ref[ks, hs]) / den
                lse = _to_lane(lse, m + jnp.log(den), h)
            l_ref[rs, :] = lse

    return pl.pallas_call(
        body, name=name, grid=(d, n_i),
        in_specs=[pl.BlockSpec((None, BQ, DA), lambda r, i: (r, i, 0)),
                  pl.BlockSpec((None, BQ, DA), lambda r, i: (r, i, 1)),
                  pl.BlockSpec((None, KB, DA), lambda r, i: (r, prev(i), 1)),
                  pl.BlockSpec((None, BQ, DA), lambda r, i: (r, i, 2)),
                  pl.BlockSpec((None, KB, DA), lambda r, i: (r, prev(i), 2))],
        out_specs=[pl.BlockSpec((None, BQ, DA), lambda r, i: (r, i, 0)),
                   pl.BlockSpec((None, BQ, LANES), lambda r, i: (r, i, 0))],
        out_shape=[SDS((d, n_sub, DA), F32), SDS((d, n_sub, LANES), F32)],
        scratch_shapes=[pltpu.VMEM((kw, DA), BF16), pltpu.VMEM((kw, DA), BF16)],
        compiler_params=_params(("arbitrary", "arbitrary")),
    )(h16, h16, h16, h16, h16)


def _pooled(ext_ref, g, rows, tm):
    w = POOL_WINDOWS[g]
    cs = slice(g * GC, (g + 1) * GC)
    cur = ext_ref[HALO:HALO + tm, cs]
    win = cur
    for j in range(1, w):
        win = win + ext_ref[HALO - j:HALO - j + tm, cs]
    cnt = jnp.minimum(rows + 1, w).astype(F32)
    return win / cnt - cur, cnt


def _fill_ext(ext_ref, u_ref, uh_ref, blk, tm):
    @pl.when(blk == 0)
    def _():
        ext_ref[0:HALO, :] = jnp.zeros((HALO, DP), F32)

    @pl.when(blk > 0)
    def _():
        ext_ref[0:HALO, :] = uh_ref[...]

    ext_ref[HALO:HALO + tm, :] = u_ref[...]


def _residue_specs(tm, width):
    return [pl.BlockSpec((d, tm // d, width), lambda i: (0, i, 0)) for d in DILATIONS[1:]]


def _mix(o1, l1, o4, l4, o16, l16, h32, wp, scale):
    tm = 256
    n_slab = N_HEADS + 1

    def body(o1r, l1r, o4r, l4r, o16r, l16r, u_ref, uh_ref, ga_ref, gp_ref, wp_ref, sc_ref,
             y_ref, yt_ref, attn_ref, lse_ref, lse4_ref, lse16_ref, ext_ref, ys_ref, nat_ref, ls_ref):
        i = pl.program_id(0)
        for n, (d, o_r, l_r) in enumerate(((DILATIONS[1], o4r, l4r), (DILATIONS[2], o16r, l16r))):
            for r in range(d):
                rows = pl.ds(r, tm // d, stride=d)
                for h in range(N_HEADS):
                    nat_ref[n, h, rows, :] = o_r[r, :, h * DH:(h + 1) * DH]
                nat_ref[n, N_HEADS, rows, :] = l_r[r]
        la, lb, lc = l1r[...], nat_ref[0, N_HEADS], nat_ref[1, N_HEADS]
        mx = jnp.maximum(jnp.maximum(la, lb), lc)
        ea, eb, ec = jnp.exp(la - mx), jnp.exp(lb - mx), jnp.exp(lc - mx)
        z = ea + eb + ec
        wa, wb, wc = ea / z, eb / z, ec / z
        lse = mx + jnp.log(z)
        lse_ref[...] = lse
        ls_ref[...] = lse
        for d, out in ((DILATIONS[1], lse4_ref), (DILATIONS[2], lse16_ref)):
            for r in range(d):
                out[r] = ls_ref[pl.ds(r, tm // d, stride=d), :]
        for h in range(N_HEADS):
            hs = slice(h * DH, (h + 1) * DH)
            hc = slice(h, h + 1)
            attn = wa[:, hc] * o1r[:, hs] + wb[:, hc] * nat_ref[0, h] + wc[:, hc] * nat_ref[1, h]
            attn_ref[:, hs] = attn
            gt = ga_ref[:, hs]
            ys_ref[:, hs] = attn * (gt * jax.nn.sigmoid(gt))

        _fill_ext(ext_ref, u_ref, uh_ref, i, tm)
        rows = i * tm + lax.broadcasted_iota(jnp.int32, (tm, 1), 0)
        for g in range(N_GROUPS):
            cs = slice(g * GC, (g + 1) * GC)
            gs = slice(DA + g * GC, DA + (g + 1) * GC)
            pooled, _ = _pooled(ext_ref, g, rows, tm)
            po = _dot(pooled.astype(BF16), wp_ref[g]) * sc_ref[:, cs]
            gt = gp_ref[:, cs]
            ys_ref[:, gs] = po * (gt * jax.nn.sigmoid(gt))
        yv = ys_ref[...]
        y_ref[...] = yv.astype(BF16)
        yt_ref[...] = yv.T.astype(BF16)

    row = lambda i: (i, 0)
    blk = pl.BlockSpec((tm, DA), row)
    lanes = pl.BlockSpec((tm, LANES), row)
    o_res, l_res = _residue_specs(tm, DA), _residue_specs(tm, LANES)
    return pl.pallas_call(
        body, name="mix", grid=(S // tm,),
        in_specs=[blk, lanes, o_res[0], l_res[0], o_res[1], l_res[1],
                  pl.BlockSpec((tm, DP), row),
                  pl.BlockSpec((HALO, DP), lambda i: (jnp.maximum(i * (tm // HALO) - 1, 0), 0)),
                  pl.BlockSpec((tm, DA), lambda i: (i, 1)), pl.BlockSpec((tm, DP), lambda i: (i, 2)),
                  pl.BlockSpec((N_GROUPS, GC, GC), lambda i: (0, 0, 0)),
                  pl.BlockSpec((1, DP), lambda i: (0, 0))],
        out_specs=[pl.BlockSpec((tm, D), row), pl.BlockSpec((D, tm), lambda i: (0, i)), blk, lanes] + l_res,
        out_shape=[SDS((S, D), BF16), SDS((D, S), BF16), SDS((S, DA), F32), SDS((S, LANES), F32)]
        + [SDS((d, S // d, LANES), F32) for d in DILATIONS[1:]],
        scratch_shapes=[pltpu.VMEM((HALO + tm, DP), F32), pltpu.VMEM((tm, D), F32),
                        pltpu.VMEM((2, n_slab, tm, DH), F32), pltpu.VMEM((tm, LANES), F32)],
        compiler_params=_params(("arbitrary",)),
    )(o1, l1, o4, l4, o16, l16, h32, h32, h32, h32, wp, scale)


def _outproj_ln(y, wout, x, target, gain, bias):
    tm = 512
    te = 16

    def body(y_ref, w_ref, x_ref, t_ref, g_ref, b_ref, dz_ref, dzb_ref, st_ref, out_ref, sums_ref):
        i = pl.program_id(0)

        @pl.when(i == 0)
        def _():
            st_ref[...] = jnp.zeros((8, D), F32)

        out_ref[...] = _dot(y_ref[...], w_ref[...])
        sums_ref[...] = jnp.zeros((3, te, D), F32)

        def rows_group(e, carry):
            rs = pl.ds(pl.multiple_of(e * te, te), te)
            gn = g_ref[...]
            z = ALPHA * x_ref[rs, :] + out_ref[rs, :]
            mu = jnp.mean(z, axis=1, keepdims=True)
            zc = z - mu
            var = jnp.mean(zc * zc, axis=1, keepdims=True)
            rstd = lax.rsqrt(var + LN_EPS)
            xhat = zc * rstd
            diff = xhat * gn + b_ref[...] - t_ref[rs, :]
            dyln = diff / D
            sums_ref[0] += dyln * xhat
            sums_ref[1] += dyln
            sums_ref[2] += diff * diff
            dxh = dyln * gn
            m1 = jnp.mean(dxh, axis=1, keepdims=True)
            m2 = jnp.mean(dxh * xhat, axis=1, keepdims=True)
            dz = rstd * (dxh - m1 - xhat * m2)
            dz_ref[rs, :] = dz
            dzb_ref[rs, :] = dz.astype(BF16)
            return carry

        lax.fori_loop(0, tm // te, rows_group, 0)
        st_ref[0:1, :] += jnp.sum(sums_ref[0], axis=0, keepdims=True)
        st_ref[1:2, :] += jnp.sum(sums_ref[1], axis=0, keepdims=True)
        sq = jnp.sum(jnp.sum(sums_ref[2], axis=0, keepdims=True), axis=1, keepdims=True)
        st_ref[2:3, :] += jnp.broadcast_to(0.5 * sq / D, (1, D))

    row = lambda i: (i, 0)
    const = lambda i: (0, 0)
    return pl.pallas_call(
        body, name="outproj_ln", grid=(S // tm,),
        in_specs=[pl.BlockSpec((tm, D), row),
                  pl.BlockSpec((D, D), const, pipeline_mode=pl.Buffered(1)),
                  pl.BlockSpec((tm, D), row), pl.BlockSpec((tm, D), row),
                  pl.BlockSpec((1, D), const), pl.BlockSpec((1, D), const)],
        out_specs=[pl.BlockSpec((tm, D), row), pl.BlockSpec((tm, D), row), pl.BlockSpec((8, D), const)],
        out_shape=[SDS((S, D), F32), SDS((S, D), BF16), SDS((8, D), F32)],
        scratch_shapes=[pltpu.VMEM((tm, D), F32), pltpu.VMEM((3, te, D), F32)],
        compiler_params=_params(("arbitrary",)),
    )(y, wout, x, target, gain, bias)


def _bwd_mid(dzb, wout, h32, attn, wp, scale, rider=None):
    tm = 256
    n = S // tm

    def body(dz_ref, w_ref, ga_ref, gp_ref, at_ref, u_ref, uh_ref, wp_ref, sc_ref,
             dh_ref, do_ref, do4_ref, do16_ref, dd_ref, dd4_ref, dd16_ref, dwp_ref, dwp16_ref, gps_ref,
             ext_ref, eext_ref, acc_ref, nat_ref, ds_ref):
        i = pl.program_id(0)
        ib = n - 1 - i

        @pl.when(i == 0)
        def _():
            eext_ref[tm:tm + HALO, :] = jnp.zeros((HALO, DP), F32)
            acc_ref[...] = jnp.zeros((N_GROUPS, GC, GC), F32)
            gps_ref[...] = jnp.zeros((8, DP), F32)

        dy = _dot(dz_ref[...], w_ref[...], NT)

        def through_gate(dy_part, gt):
            sg = jax.nn.sigmoid(gt)
            return dy_part * (gt * sg), dy_part * (sg * (1.0 + gt * (1.0 - sg)))

        dat, dyg_a = through_gate(dy[:, 0:DA], ga_ref[...])
        dmix_p, dyg_p = through_gate(dy[:, DA:D], gp_ref[...])

        at = at_ref[...]
        do_ref[...] = dat.astype(BF16)
        dh_ref[:, DP:DP + DA] = (dyg_a * at).astype(BF16)
        prod = dat * at
        dd = jnp.zeros((tm, LANES), F32)
        for h in range(N_HEADS):
            hs = slice(h * DH, (h + 1) * DH)
            dd = _to_lane(dd, jnp.sum(prod[:, hs], axis=1, keepdims=True), h)
            nat_ref[h] = dat[:, hs]
        dd_ref[...] = dd
        ds_ref[...] = dd
        for d, do_out, dd_out in ((DILATIONS[1], do4_ref, dd4_ref), (DILATIONS[2], do16_ref, dd16_ref)):
            for r in range(d):
                dd_out[r] = _residues(ds_ref, (), d, r, tm // d)
                for h in range(N_HEADS):
                    do_out[r, :, h * DH:(h + 1) * DH] = _residues(nat_ref, (h,), d, r, tm // d).astype(BF16)

        _fill_ext(ext_ref, u_ref, uh_ref, ib, tm)
        rows = ib * tm + lax.broadcasted_iota(jnp.int32, (tm, 1), 0)
        for g in range(N_GROUPS):
            w = POOL_WINDOWS[g]
            cs = slice(g * GC, (g + 1) * GC)
            pooled, cnt = _pooled(ext_ref, g, rows, tm)
            pre = _dot(pooled.astype(BF16), wp_ref[g])
            sc = sc_ref[:, cs]
            dpo = dmix_p[:, cs]
            gps_ref[0:1, cs] += jnp.sum(dpo * pre, axis=0, keepdims=True)
            dh_ref[:, DP + DA + g * GC:DP + DA + (g + 1) * GC] = (dyg_p[:, cs] * (pre * sc)).astype(BF16)
            dpre = (dpo * sc).astype(BF16)
            acc_ref[g] += _dot(pooled.T.astype(BF16), dpre)
            dpooled = _dot(dpre, wp_ref[g], NT)
            eext_ref[0:tm, cs] = dpooled / cnt
            du = eext_ref[0:tm, cs]
            for j in range(1, w):
                du = du + eext_ref[j:j + tm, cs]
            dh_ref[:, cs] = (du - dpooled).astype(BF16)
        eext_ref[tm:tm + HALO, :] = eext_ref[0:HALO, :]

        @pl.when(i == n - 1)
        def _():
            for j in range(N_DEV):
                for g in range(N_GROUPS):
                    blk = acc_ref[g, j * PB:(j + 1) * PB, :]
                    dwp_ref[j, g * PB:(g + 1) * PB, :] = blk
                    dwp16_ref[j, g * PB:(g + 1) * PB, :] = blk.astype(BF16)

    rev = lambda i: (n - 1 - i, 0)
    const = lambda i: (0, 0)
    res = lambda width: [pl.BlockSpec((d, tm // d, width), lambda i: (0, n - 1 - i, 0)) for d in DILATIONS[1:]]
    pool_blocks = pl.BlockSpec((N_DEV, N_GROUPS * PB, GC), lambda i: (0, 0, 0))
    outs, carried = _call_carrying(
        body, rider, (n,),
        [pl.BlockSpec((tm, D), rev),
         pl.BlockSpec((D, D), const, pipeline_mode=pl.Buffered(1)),
         pl.BlockSpec((tm, DA), lambda i: (n - 1 - i, 1)), pl.BlockSpec((tm, DP), lambda i: (n - 1 - i, 2)),
         pl.BlockSpec((tm, DA), rev), pl.BlockSpec((tm, DP), rev),
         pl.BlockSpec((HALO, DP), lambda i: (jnp.maximum((n - 1 - i) * (tm // HALO) - 1, 0), 0)),
         pl.BlockSpec((N_GROUPS, GC, GC), lambda i: (0, 0, 0)),
         pl.BlockSpec((1, DP), const)],
        [pl.BlockSpec((tm, DP + D), lambda i: (n - 1 - i, 1)), pl.BlockSpec((tm, DA), rev)] + res(DA)
        + [pl.BlockSpec((tm, LANES), rev)] + res(LANES) + [pool_blocks, pool_blocks, pl.BlockSpec((8, DP), const)],
        [SDS((S, D_IN), BF16), SDS((S, DA), BF16)] + [SDS((d, S // d, DA), BF16) for d in DILATIONS[1:]]
        + [SDS((S, LANES), F32)] + [SDS((d, S // d, LANES), F32) for d in DILATIONS[1:]]
        + [SDS((N_DEV, N_GROUPS * PB, GC), F32), SDS((N_DEV, N_GROUPS * PB, GC), BF16), SDS((8, DP), F32)],
        [pltpu.VMEM((HALO + tm, DP), F32), pltpu.VMEM((tm + HALO, DP), F32),
         pltpu.VMEM((N_GROUPS, GC, GC), F32), pltpu.VMEM((N_HEADS, tm, DH), F32), pltpu.VMEM((tm, LANES), F32)],
        [dzb, wout, h32, h32, attn, h32, h32, wp, scale], "bwd_mid")
    return outs, carried


def _attn_bwd(h16, do, lse, dd, rope, name, others=None, dh=None):
    d, n_sub, _ = h16.shape
    n_i = n_sub // BQ
    n_kb = n_sub // KB
    qw = BQ + KB
    final = others is not None
    out_dtype = BF16 if final else F32
    n_cb = 3 * DA // DH

    def nxt(i):
        return jnp.minimum(2 * i + 2, n_kb - 1)

    def body(qc_ref, qn_ref, kc_ref, vc_ref, doc_ref, don_ref, lc_ref, ln_ref, dc_ref, dn_ref,
             c_ref, sn_ref, *rest):
        if final:
            acc4_ref, acc16_ref, _, out_ref, carry_ref, qw_ref, dow_ref, lw_ref, dw_ref, nat_ref = rest
        else:
            out_ref, carry_ref, qw_ref, dow_ref, lw_ref, dw_ref = rest
        i = pl.program_id(1)

        @pl.when(i == 0)
        def _():
            carry_ref[...] = jnp.zeros((KB, DA), F32)

        for win, own, after in ((qw_ref, qc_ref, qn_ref), (dow_ref, doc_ref, don_ref), (lw_ref, lc_ref, ln_ref),
                                (dw_ref, dc_ref, dn_ref)):
            win[0:BQ, :] = own[...]
            win[BQ:qw, :] = after[...]
        if final:
            for n, (dil, acc) in enumerate(((DILATIONS[1], acc4_ref), (DILATIONS[2], acc16_ref))):
                for r in range(dil):
                    for cb in range(n_cb):
                        nat_ref[n, cb, pl.ds(r, BQ // dil, stride=dil), :] = acc[r, :, cb * DH:(cb + 1) * DH]

        a = lax.broadcasted_iota(jnp.int32, (qw, BQ), 0)
        b = lax.broadcasted_iota(jnp.int32, (qw, BQ), 1)
        n_q = jnp.where(i == n_i - 1, BQ, qw)
        mask = (b <= a) & (a <= b + KB) & (a < n_q)
        tabs = (c_ref[...], sn_ref[...])
        for h in range(N_HEADS):
            hs = slice(h * DH, (h + 1) * DH)
            hc = slice(h, h + 1)
            q, k, vv, dob = qw_ref[:, hs], kc_ref[:, hs], vc_ref[:, hs], dow_ref[:, hs]
            s = _dot(q, k, NT) * SCALE
            p = jnp.exp(jnp.where(mask, s - lw_ref[:, hc], NEG))
            dp = _dot(dob, vv, NT)
            ds = (p * (dp - dw_ref[:, hc]) * SCALE).astype(BF16)
            dq = _dot(ds, k)
            dk = _dot(ds, q, T_N)
            dv = _dot(p.astype(BF16), dob, T_N)
            dq_lo = dq[0:KB] + carry_ref[:, hs]
            carry_ref[:, hs] = dq[BQ:qw]
            dq_own = _rope_t(jnp.concatenate([dq_lo, dq[KB:BQ]], axis=0), *tabs)
            for base, gv in ((0, dq_own), (DA, _rope_t(dk, *tabs)), (2 * DA, dv)):
                if final:
                    cb = base // DH + h
                    gv = gv + nat_ref[0, cb] + nat_ref[1, cb]
                out_ref[:, base + h * DH:base + (h + 1) * DH] = gv.astype(out_dtype)

    def cur(rows, width, col=0):
        return pl.BlockSpec((None, rows, width), lambda r, i: (r, i, col))

    def nx(width, col=0):
        return pl.BlockSpec((None, KB, width), lambda r, i: (r, nxt(i), col))

    in_specs = [cur(BQ, DA), nx(DA), cur(BQ, DA, 1), cur(BQ, DA, 2), cur(BQ, DA), nx(DA),
                cur(BQ, LANES), nx(LANES), cur(BQ, LANES), nx(LANES)] + [cur(BQ, DH)] * 2
    args = [h16, h16, h16, h16, do, do, lse, lse, dd, dd] + list(rope)
    scratch = [pltpu.VMEM((KB, DA), F32), pltpu.VMEM((qw, DA), BF16), pltpu.VMEM((qw, DA), BF16),
               pltpu.VMEM((qw, LANES), F32), pltpu.VMEM((qw, LANES), F32)]
    if final:
        assert d == 1
        in_specs += [pl.BlockSpec((dil, BQ // dil, 3 * DA), lambda r, i: (0, i, 0)) for dil in DILATIONS[1:]]
        in_specs.append(pl.BlockSpec(memory_space=pl.ANY))
        args += list(others) + [dh[None]]
        scratch.append(pltpu.VMEM((2, n_cb, BQ, DH), F32))
    return pl.pallas_call(
        body, name=name, grid=(d, n_i),
        in_specs=in_specs,
        out_specs=cur(BQ, 3 * DA),
        out_shape=SDS((d, n_sub, D_IN if final else 3 * DA), out_dtype),
        input_output_aliases={len(args) - 1: 0} if final else {},
        scratch_shapes=scratch,
        compiler_params=_params(("arbitrary", "arbitrary")),
    )(*args)


def _grad_w_out(yt, dzb):
    tn = 256

    def body(yt_ref, dz_ref, o_ref, o16_ref):
        acc = _dot(yt_ref[...], dz_ref[...])
        o_ref[...] = acc
        o16_ref[...] = acc.astype(BF16)

    cols = pl.BlockSpec((D, tn), lambda c: (0, c))
    o, o16 = pl.pallas_call(
        body, name="grad_w_out", grid=(D // tn,),
        in_specs=[pl.BlockSpec((D, S), lambda c: (0, 0), pipeline_mode=pl.Buffered(1)),
                  pl.BlockSpec((S, tn), lambda c: (0, c))],
        out_specs=[cols, cols],
        out_shape=[SDS((D, D), F32), SDS((D, D), BF16)],
        compiler_params=_params(("arbitrary",)),
    )(yt, dzb)
    return o.reshape(N_DEV, OB, D), o16.reshape(N_DEV, OB, D)


def _grad_w_in(xt, dh):
    tn = 256
    per_blk = WB // tn

    def body(xt_ref, dh_ref, o_ref, o16_ref):
        acc = _dot(xt_ref[...], dh_ref[...])
        o_ref[...] = acc
        o16_ref[...] = acc.astype(BF16)

    blk = pl.BlockSpec((None, D, tn), lambda c: (c // per_blk, 0, c % per_blk))
    return pl.pallas_call(
        body, name="grad_w_in", grid=(D_IN // tn,),
        in_specs=[pl.BlockSpec((D, S), lambda c: (0, 0), pipeline_mode=pl.Buffered(1)),
                  pl.BlockSpec((S, tn), lambda c: (0, c))],
        out_specs=[blk, blk],
        out_shape=[SDS((N_DEV, D, WB), F32), SDS((N_DEV, D, WB), BF16)],
        compiler_params=_params(("arbitrary",)),
    )(xt, dh)


def _grad_x(dz, dh, wgs, rider=None):
    tm = 1024

    def body(dz_ref, dh_ref, *rest):
        o_ref = rest[-1]

        @pl.when(pl.program_id(1) == 0)
        def _():
            o_ref[...] = ALPHA * dz_ref[...]

        acc = _dot(dh_ref[:, 0:TN], rest[0][...], NT)
        for s in range(1, W_PHASES):
            acc = acc + _dot(dh_ref[:, s * TN:(s + 1) * TN], rest[s][...], NT)
        o_ref[...] += acc

    outs, carried = _call_carrying(
        body, rider, (S // tm, N_DEV),
        [pl.BlockSpec((tm, D), lambda i, j: (i, 0)), pl.BlockSpec((tm, WB), lambda i, j: (i, j))]
        + [pl.BlockSpec((None, D, TN), lambda i, j: (j, 0, 0))] * W_PHASES,
        [pl.BlockSpec((tm, D), lambda i, j: (i, 0))], [SDS((S, D), F32)], [], [dz, dh] + list(wgs), "grad_x")
    return outs[0], carried


def _place():
    x, y, c = lax.axis_index("x"), lax.axis_index("y"), lax.axis_index("c")
    chips = [(x, y), (1 - x, y), (x, 1 - y), (1 - x, 1 - y)]
    return x, y, c, chips


def _blk(x, y, c):
    return 4 * x + 2 * y + c


def _adamw(w, g, m, v):
    m = ADAM_B1 * m + (1.0 - ADAM_B1) * g
    v = ADAM_B2 * v + (1.0 - ADAM_B2) * (g * g)
    m_hat = m / (1.0 - ADAM_B1 ** ADAM_STEP)
    v_hat = v / (1.0 - ADAM_B2 ** ADAM_STEP)
    delta = -ADAM_LR * (m_hat / (jnp.sqrt(v_hat) + ADAM_EPS) + ADAM_WD * w)
    return delta, m, v


class _Rider:
    def __init__(self, args, in_specs, out_shape, out_specs, scratch, start, finish):
        self.args, self.in_specs, self.out_shape, self.out_specs = args, in_specs, out_shape, out_specs
        self.scratch, self.start, self.finish = scratch, start, finish


def _carry(body, rider, n_in, n_out, first, last):
    if rider is None:
        return body
    r_in, r_out, r_scr = len(rider.args), len(rider.out_shape), len(rider.scratch)

    def carrying(*refs):
        o0 = n_in + r_in
        s0 = o0 + n_out + r_out
        s1 = len(refs) - r_scr
        theirs = (refs[n_in:o0], refs[o0 + n_out:s0], refs[s1:])
        pl.when(first())(lambda: rider.start(*theirs))
        body(*refs[:n_in], *refs[o0:o0 + n_out], *refs[s0:s1])
        pl.when(last())(lambda: rider.finish(*theirs))

    return carrying


def _call_carrying(body, rider, grid, in_specs, out_specs, out_shape, scratch, args, name, aliases=None):
    n_in, n_out = len(in_specs), len(out_specs)
    ids = lambda: [pl.program_id(a) for a in range(len(grid))]
    first = lambda: functools.reduce(jnp.logical_and, [i == 0 for i in ids()])
    last = lambda: functools.reduce(jnp.logical_and, [i == n - 1 for i, n in zip(ids(), grid)])
    if rider is not None:
        in_specs, args = in_specs + rider.in_specs, list(args) + rider.args
        out_specs, out_shape = out_specs + rider.out_specs, out_shape + rider.out_shape
        scratch = scratch + rider.scratch
    outs = pl.pallas_call(
        _carry(body, rider, n_in, n_out, first, last), name=name, grid=grid,
        in_specs=in_specs, out_specs=out_specs, out_shape=out_shape, scratch_shapes=scratch,
        input_output_aliases=aliases or {}, compiler_params=_params(("arbitrary",) * len(grid)),
    )(*args)
    return list(outs[:n_out]), list(outs[n_out:])


def _gather_copy(tensors, send_sems, recv_sems, t, k, block, to, src=None):
    dst = tensors[t][1](_blk(*block))
    return pltpu.make_async_remote_copy(
        src_ref=dst if src is None else src, dst_ref=dst,
        send_sem=send_sems.at[t, k], recv_sem=recv_sems.at[t, k], device_id=to, device_id_type=MESH)


def _gather_start(tensors, send_sems, recv_sems, local_sems):
    x, y, c, chips = _place()
    me, sib = (x, y, c), (x, y, 1 - c)
    for t, (src, dst) in enumerate(tensors):
        pltpu.make_async_copy(src, dst(_blk(*me)), local_sems.at[t]).start()
        _gather_copy(tensors, send_sems, recv_sems, t, 0, me, sib, src).start()
        for j in (1, 2, 3):
            _gather_copy(tensors, send_sems, recv_sems, t, j, me, (*chips[j], c), src).start()


def _gather_finish(tensors, send_sems, recv_sems, local_sems):
    x, y, c, chips = _place()
    me, sib = (x, y, c), (x, y, 1 - c)
    copy = functools.partial(_gather_copy, tensors, send_sems, recv_sems)
    for t in range(len(tensors)):
        for j in (1, 2, 3):
            copy(t, j, (*chips[j], c), me).wait_recv()
            copy(t, 3 + j, (*chips[j], c), sib).start()
    for t, (src, dst) in enumerate(tensors):
        copy(t, 0, sib, me).wait_recv()
        for j in (1, 2, 3):
            copy(t, 3 + j, (*chips[j], 1 - c), me).wait_recv()
        copy(t, 0, me, sib, src).wait_send()
        for j in (1, 2, 3):
            copy(t, j, me, (*chips[j], c), src).wait_send()
            copy(t, 3 + j, (*chips[j], c), sib).wait_send()
        pltpu.make_async_copy(src, dst(_blk(*me)), local_sems.at[t]).wait()


W_PHASES = WB // TN


def _gather_w_in_first(w_in_s, w_out_s, w_pool_s):
    def body(win_ref, wout_ref, wpool_ref, in16_ref, out16_ref, pool16_ref, g_ref, bin_ref, bout_ref, bpool_ref,
             send_sems, recv_sems, local_sems, keep_sems):
        keeps = []
        for n, (src, buf, dst) in enumerate(((win_ref, bin_ref, in16_ref), (wout_ref, bout_ref, out16_ref),
                                             (wpool_ref, bpool_ref, pool16_ref))):
            buf[...] = src[...].astype(BF16)
            keeps.append(pltpu.make_async_copy(buf, dst, keep_sems.at[n]))
            keeps[-1].start()
        tensors = [(bin_ref.at[:, pl.ds(0, TN)], lambda b: g_ref.at[b])]
        _gather_start(tensors, send_sems, recv_sems, local_sems)
        _gather_finish(tensors, send_sems, recv_sems, local_sems)
        for keep in keeps:
            keep.wait()

    vmem = pl.BlockSpec(memory_space=pltpu.VMEM)
    hbm = pl.BlockSpec(memory_space=pl.ANY)
    shapes = [(D, WB), (OB, D), (N_GROUPS, PB, GC)]
    return pl.pallas_call(
        body, name="gather_w_in_0",
        in_specs=[vmem] * 3, out_specs=[hbm] * 4,
        out_shape=[SDS(sh, BF16) for sh in shapes] + [SDS((N_DEV, D, TN), BF16)],
        scratch_shapes=[pltpu.VMEM(sh, BF16) for sh in shapes]
        + [pltpu.SemaphoreType.DMA((1, 7)), pltpu.SemaphoreType.DMA((1, 7)), pltpu.SemaphoreType.DMA((1,)),
           pltpu.SemaphoreType.DMA((3,))],
        compiler_params=_params(),
    )(w_in_s, w_out_s, w_pool_s)


def _gather_w_in_rider(shard16, s):
    def tensors(ins, outs):
        return [(ins[0].at[:, pl.ds(s * TN, TN)], lambda b: outs[0].at[b])]

    hbm = pl.BlockSpec(memory_space=pl.ANY)
    return _Rider(
        args=[shard16], in_specs=[hbm], out_shape=[SDS((N_DEV, D, TN), BF16)], out_specs=[hbm],
        scratch=[pltpu.SemaphoreType.DMA((1, 7)), pltpu.SemaphoreType.DMA((1, 7)), pltpu.SemaphoreType.DMA((1,))],
        start=lambda ins, outs, scr: _gather_start(tensors(ins, outs), *scr),
        finish=lambda ins, outs, scr: _gather_finish(tensors(ins, outs), *scr))


def _gather_small_rider(w_out16, w_pool16):
    def tensors(ins, outs):
        gout_ref, gpool_ref = outs

        def pool_rows(b):
            return gpool_ref.at[:, pl.ds(pl.multiple_of(b * PB, PB), PB), :]

        return [(ins[0], lambda b: gout_ref.at[b]), (ins[1], pool_rows)]

    hbm = pl.BlockSpec(memory_space=pl.ANY)
    return _Rider(
        args=[w_out16, w_pool16], in_specs=[hbm, hbm],
        out_shape=[SDS((N_DEV, OB, D), BF16), SDS((N_GROUPS, GC, GC), BF16)], out_specs=[hbm, hbm],
        scratch=[pltpu.SemaphoreType.DMA((2, 7)), pltpu.SemaphoreType.DMA((2, 7)), pltpu.SemaphoreType.DMA((2,))],
        start=lambda ins, outs, scr: _gather_start(tensors(ins, outs), *scr),
        finish=lambda ins, outs, scr: _gather_finish(tensors(ins, outs), *scr))


def _block_table():
    x, y, c, chips = _place()
    return jnp.stack([_blk(*chip, c) for chip in chips]).astype(jnp.int32)


def _rs_exchange(name, p16):
    _, r_tot, cols = p16.shape

    def body(p16_ref, r1_ref, send_sems, recv_sems):
        x, y, c, chips = _place()
        copies = [pltpu.make_async_remote_copy(
            src_ref=p16_ref.at[_blk(*chips[k], 1 - c)], dst_ref=r1_ref.at[k],
            send_sem=send_sems.at[k], recv_sem=recv_sems.at[k], device_id=(x, y, 1 - c), device_id_type=MESH)
            for k in range(4)]
        for cp in copies:
            cp.start()
        for cp in copies:
            cp.wait()

    hbm = pl.BlockSpec(memory_space=pl.ANY)
    return pl.pallas_call(
        body, name=name, in_specs=[hbm], out_specs=hbm, out_shape=SDS((4, r_tot, cols), BF16),
        scratch_shapes=[pltpu.SemaphoreType.DMA((4,)), pltpu.SemaphoreType.DMA((4,))],
        compiler_params=_params(),
    )(p16)


def _chip_sums(name, table, p32, r1, rc):
    _, r_tot, cols = p32.shape

    def body(tbl_ref, p_ref, r_ref, o_ref):
        o_ref[...] = (p_ref[...] + r_ref[...].astype(F32)).astype(BF16)

    return pl.pallas_call(
        body, name=name,
        grid_spec=pltpu.PrefetchScalarGridSpec(
            num_scalar_prefetch=1, grid=(3, r_tot // rc),
            in_specs=[pl.BlockSpec((None, rc, cols), lambda k, ch, tbl: (tbl[k + 1], ch, 0)),
                      pl.BlockSpec((None, rc, cols), lambda k, ch, tbl: (k + 1, ch, 0))],
            out_specs=pl.BlockSpec((None, rc, cols), lambda k, ch, tbl: (k, ch, 0))),
        out_shape=SDS((3, r_tot, cols), BF16),
        compiler_params=_params(("arbitrary", "arbitrary")),
    )(table, p32, r1)


def _stage2_rider(sums, stats=None):
    n_t = len(sums)

    def copies(ins, outs, scr):
        x, y, c, chips = _place()
        out = []
        for t in range(n_t):
            for k in (1, 2, 3):
                out.append(pltpu.make_async_remote_copy(
                    src_ref=ins[t].at[k - 1], dst_ref=outs[t].at[k - 1],
                    send_sem=scr[0].at[t, k - 1], recv_sem=scr[1].at[t, k - 1],
                    device_id=(*chips[k], c), device_id_type=MESH))
        if stats is not None:
            for k in range(1, N_DEV):
                peer = (x ^ ((k >> 2) & 1), y ^ ((k >> 1) & 1), c ^ (k & 1))
                out.append(pltpu.make_async_remote_copy(
                    src_ref=scr[4], dst_ref=outs[n_t].at[_blk(x, y, c)],
                    send_sem=scr[2].at[k - 1], recv_sem=scr[3].at[k - 1], device_id=peer, device_id_type=MESH))
        return out

    def own_rows(outs, scr):
        x, y, c, _ = _place()
        return pltpu.make_async_copy(scr[4], outs[n_t].at[_blk(x, y, c)], scr[5])

    def start(ins, outs, scr):
        if stats is not None:
            scr[4][...] = ins[n_t][...]
            scr[4][3:4, 0:DP] = ins[n_t + 1][0:1, :]
            own_rows(outs, scr).start()
        for cp in copies(ins, outs, scr):
            cp.start()

    def finish(ins, outs, scr):
        for cp in copies(ins, outs, scr):
            cp.wait()
        if stats is not None:
            own_rows(outs, scr).wait()

    vmem = pl.BlockSpec(memory_space=pltpu.VMEM)
    hbm = pl.BlockSpec(memory_space=pl.ANY)
    scratch = [pltpu.SemaphoreType.DMA((n_t, 3)), pltpu.SemaphoreType.DMA((n_t, 3))]
    args, in_specs = list(sums), [hbm] * n_t
    out_shape, out_specs = [SDS(s.shape, BF16) for s in sums], [hbm] * n_t
    if stats is not None:
        scratch += [pltpu.SemaphoreType.DMA((N_DEV - 1,)), pltpu.SemaphoreType.DMA((N_DEV - 1,)),
                    pltpu.VMEM((8, D), F32), pltpu.SemaphoreType.DMA(())]
        args, in_specs = args + list(stats), in_specs + [vmem, vmem]
        out_shape, out_specs = out_shape + [SDS((N_DEV, 8, D), F32)], out_specs + [hbm]
    return _Rider(args, in_specs, out_shape, out_specs, scratch, start, finish)


def _adamw_shard(name, table, p32, r1, r2, w, m, v, rc):
    _, r_tot, cols = p32.shape

    def body(tbl_ref, p_ref, r1_ref, r2_ref, w_ref, m_ref, v_ref, g_ref, d_ref, nm_ref, nv_ref):
        g = p_ref[...] + r1_ref[...].astype(F32)
        for k in range(3):
            g = g + r2_ref[k].astype(F32)
        delta, nm, nv = _adamw(w_ref[...], g, m_ref[...], v_ref[...])
        g_ref[...] = g
        d_ref[...] = delta
        nm_ref[...] = nm
        nv_ref[...] = nv

    rows = pl.BlockSpec((rc, cols), lambda ch, tbl: (ch, 0))
    shard = SDS((r_tot, cols), F32)
    return pl.pallas_call(
        body, name=name,
        grid_spec=pltpu.PrefetchScalarGridSpec(
            num_scalar_prefetch=1, grid=(r_tot // rc,),
            in_specs=[pl.BlockSpec((None, rc, cols), lambda ch, tbl: (tbl[0], ch, 0)),
                      pl.BlockSpec((None, rc, cols), lambda ch, tbl: (0, ch, 0)),
                      pl.BlockSpec((3, rc, cols), lambda ch, tbl: (0, ch, 0)), rows, rows, rows],
            out_specs=[rows, rows, rows, rows]),
        out_shape=[shard, shard, shard, shard],
        compiler_params=_params(("arbitrary",)),
    )(table, p32, r1, r2, w, m, v)


def _replicated_adamw(gathered, gain, bias, scale, m_gain, m_bias, m_scale, v_gain, v_bias, v_scale):
    def body(all_ref, g_ref, b_ref, s_ref, mg_ref, mb_ref, ms_ref, vg_ref, vb_ref, vs_ref,
             tot_ref, dl_ref, nm_ref, nv_ref):
        tot = all_ref[0]
        for b in range(1, N_DEV):
            tot = tot + all_ref[b]
        tot_ref[...] = tot
        dl_ref[...] = jnp.zeros((8, D), F32)
        nm_ref[...] = jnp.zeros((8, D), F32)
        nv_ref[...] = jnp.zeros((8, D), F32)
        for row, width, w_r, m_r, v_r in ((0, D, g_ref, mg_ref, vg_ref), (1, D, b_ref, mb_ref, vb_ref),
                                          (3, DP, s_ref, ms_ref, vs_ref)):
            delta, nm, nv = _adamw(w_r[...], tot[row:row + 1, 0:width], m_r[...], v_r[...])
            dl_ref[row:row + 1, 0:width] = delta
            nm_ref[row:row + 1, 0:width] = nm
            nv_ref[row:row + 1, 0:width] = nv

    vmem = pl.BlockSpec(memory_space=pltpu.VMEM)
    rows = SDS((8, D), F32)
    return pl.pallas_call(
        body, name="replicated_adamw",
        in_specs=[vmem] * 10, out_specs=[vmem] * 4, out_shape=[rows, rows, rows, rows],
        compiler_params=_params(),
    )(gathered, gain, bias, scale, m_gain, m_bias, m_scale, v_gain, v_bias, v_scale)


def kernel(x, w_in, w_pool, pool_scale, w_out, ln_gain, ln_bias, loss_target, m_w_in, m_w_pool, m_pool_scale, m_w_out, m_ln_gain, m_ln_bias, v_w_in, v_w_pool, v_pool_scale, v_w_out, v_ln_gain, v_ln_bias):
    pool_rows = (N_GROUPS * PB, GC)
    x2, target = x[0], loss_target[0]
    rope = lax.optimization_barrier(_rope_tables())

    table = _block_table()
    by_res = lambda t, d: t.reshape(S // d, d, DH).transpose(1, 0, 2)
    ropes = [[by_res(t, d) for t in rope] for d in DILATIONS]

    shard16, w_out16, w_pool16, wg0 = _gather_w_in_first(w_in[0], w_out[0], w_pool[0])
    xb, xt = _prep_x(x2)
    h, (wg1,) = _proj_phase(xb, wg0, 0, rope, None, _gather_w_in_rider(shard16, 1))
    h, (wg2,) = _proj_phase(xb, wg1, 1, rope, h, _gather_w_in_rider(shard16, 2))
    h, (wg_out, wg_pool) = _proj_phase(xb, wg2, 2, rope, h, _gather_small_rider(w_out16, w_pool16))
    wg_out = wg_out.reshape(D, D)
    h16, h32 = [h[0][None], h[1], h[2]], h[3]
    fwd = [_attn_fwd(h16[n], f"attn_fwd_d{d}") for n, d in enumerate(DILATIONS)]
    y, yt, attn, lse1, lse4, lse16 = _mix(fwd[0][0][0], fwd[0][1][0], *fwd[1], *fwd[2], h32, wg_pool, pool_scale)
    dz, dzb, stats = _outproj_ln(y, wg_out, x2, target, ln_gain, ln_bias)

    dwout, dwout16 = _grad_w_out(yt, dzb)
    r1_out = _rs_exchange("rs_out_exchange", dwout16)
    s2_out = _chip_sums("rs_out_sums", table, dwout, r1_out, OB)
    mid, (r2_out,) = _bwd_mid(dzb, wg_out, h32, attn, wg_pool, pool_scale, rider=_stage2_rider([s2_out]))
    dh, do1, do4, do16, dd1, dd4, dd16, dwp, dwp16, gps = mid
    g_out, d_out, nm_out, nv_out = _adamw_shard(
        "adamw_w_out", table, dwout, r1_out, r2_out, w_out[0], m_w_out[0], v_w_out[0], OB // 2)

    do, lse, dd = [do1[None], do4, do16], [lse1[None], lse4, lse16], [dd1[None], dd4, dd16]
    others = [_attn_bwd(h16[n], do[n], lse[n], dd[n], ropes[n], f"attn_bwd_d{DILATIONS[n]}") for n in (1, 2)]
    dh = _attn_bwd(h16[0], do[0], lse[0], dd[0], ropes[0], "attn_bwd_d1", others=others, dh=dh)[0]

    dwin, dwin16 = _grad_w_in(xt, dh)
    r1_in = _rs_exchange("rs_in_exchange", dwin16)
    s2_in = _chip_sums("rs_in_sums", table, dwin, r1_in, 512)
    r1_pool = _rs_exchange("rs_pool_exchange", dwp16)
    s2_pool = _chip_sums("rs_pool_sums", table, dwp, r1_pool, N_GROUPS * PB)
    grad_x, (r2_in, r2_pool, gathered) = _grad_x(
        dz, dh, (wg0, wg1, wg2), rider=_stage2_rider([s2_in, s2_pool], stats=(stats, gps)))
    g_in, d_in, nm_in, nv_in = _adamw_shard(
        "adamw_w_in", table, dwin, r1_in, r2_in, w_in[0], m_w_in[0], v_w_in[0], 256)
    g_pool, d_pool, nm_pool, nv_pool = _adamw_shard(
        "adamw_w_pool", table, dwp, r1_pool, r2_pool, w_pool[0].reshape(pool_rows), m_w_pool[0].reshape(pool_rows),
        v_w_pool[0].reshape(pool_rows), N_GROUPS * PB)
    tot, dl, nm, nv = _replicated_adamw(gathered, ln_gain, ln_bias, pool_scale, m_ln_gain, m_ln_bias, m_pool_scale,
                                        v_ln_gain, v_ln_bias, v_pool_scale)

    shard4 = lambda t: t.reshape(1, N_GROUPS, PB, GC)
    lead = lambda t: t[None]
    small = lambda t: (t[3:4, 0:DP], t[0:1], t[1:2])
    g_ps, g_gain, g_bias = small(tot)
    d_ps, d_gain, d_bias = small(dl)
    nm_ps, nm_gain, nm_bias = small(nm)
    nv_ps, nv_gain, nv_bias = small(nv)
    return (tot[2, 0], lead(grad_x),
            lead(g_in), shard4(g_pool), g_ps, lead(g_out), g_gain, g_bias,
            lead(d_in), shard4(d_pool), d_ps, lead(d_out), d_gain, d_bias,
            lead(nm_in), shard4(nm_pool), nm_ps, lead(nm_out), nm_gain, nm_bias,
            lead(nv_in), shard4(nv_pool), nv_ps, lead(nv_out), nv_gain, nv_bias)
```

```python
import functools

import jax
import jax.numpy as jnp
from jax import lax
from jax.experimental import pallas as pl
from jax.experimental.pallas import tpu as pltpu

F32 = jnp.float32
BF16 = jnp.bfloat16
SDS = jax.ShapeDtypeStruct
MESH = pl.DeviceIdType.MESH

N_DEV = 8
S = 4096
D = 2048
N_HEADS = 8
DH = 128
DA = N_HEADS * DH
DP = 1024
N_GROUPS = 4
GC = DP // N_GROUPS
POOL_WINDOWS = (2, 4, 8, 16)
HALO = 16
D_IN = 3 * DA + DP + D
WB = D_IN // N_DEV
TN = 256
HW = 3 * DA
OB = D // N_DEV
PB = GC // N_DEV
ROPE_DIM = DH // 4
ROPE_HALF = ROPE_DIM // 2
ROPE_THETA = 500000.0
DILATIONS = (1, 4, 16)
KB = 128
LN_EPS = 1e-5
ALPHA = 2.0 ** 0.25
SCALE = DH ** -0.5
NEG = -1e30
ADAM_LR, ADAM_B1, ADAM_B2, ADAM_EPS, ADAM_WD, ADAM_STEP = 0.001, 0.9, 0.999, 1e-08, 0.01, 10

VMEM_LIMIT_V7X = 60 * 1024 * 1024

NT = (((1,), (1,)), ((), ()))
T_N = (((0,), (0,)), ((), ()))


def _params(sem=None):
    return pltpu.CompilerParams(dimension_semantics=sem, vmem_limit_bytes=VMEM_LIMIT_V7X)


def _dot(a, b, dims=None):
    if dims is None:
        return jnp.dot(a, b, preferred_element_type=F32)
    return lax.dot_general(a, b, dims, preferred_element_type=F32)


def _rope_tables():
    inv_freq = ROPE_THETA ** (-(2.0 * jnp.arange(ROPE_HALF, dtype=F32)) / ROPE_DIM)
    ang = jnp.arange(S, dtype=jnp.int32).astype(F32)[:, None] * inv_freq[None, :]
    cos, sin = jnp.cos(ang), jnp.sin(ang)
    rest = DH - ROPE_DIM
    c = jnp.concatenate([cos, cos, jnp.ones((S, rest), F32)], axis=1)
    sn = jnp.concatenate([-sin, sin, jnp.zeros((S, rest), F32)], axis=1)
    return c, sn


def _rope_partner(t):
    lane = lax.broadcasted_iota(jnp.int32, t.shape, 1)
    return jnp.where(lane < ROPE_HALF, pltpu.roll(t, DH - ROPE_HALF, 1), pltpu.roll(t, ROPE_HALF, 1))


def _rope(t, c, sn):
    return t * c + _rope_partner(t) * sn


def _rope_t(g, c, sn):
    return g * c - _rope_partner(g) * sn


def _prep_x(x, rider):
    tm = 512

    def body(x_ref, xb_ref, xt_ref):
        xv = x_ref[...]
        xb_ref[...] = xv.astype(BF16)
        xt_ref[...] = xv.T.astype(BF16)

    return _call_carrying(
        body, rider, (S // tm,), [pl.BlockSpec((tm, D), lambda i: (i, 0))],
        [pl.BlockSpec((tm, D), lambda i: (i, 0)), pl.BlockSpec((D, tm), lambda i: (0, i))],
        [SDS((S, D), BF16), SDS((D, S), BF16)], [], [x], "prep_x")


def _residues(slab_ref, idx, d, r, n):
    return slab_ref[(*idx, pl.ds(r, n, stride=d), slice(None))]


def _proj_phase(xb, wg, s, rope, prev, rider):
    n_heads = TN // DH
    n16 = HW // TN
    j16 = n16 // W_PHASES
    n_rope = 2 * DA // TN
    n_part = 2
    once = pl.Buffered(1)

    def body(x_ref, w_ref, c_ref, sn_ref, *rest):
        h16_ref, h4_ref, h16r_ref, h32_ref, acc_ref, slab_ref, res_ref = rest[-7:]
        j = pl.program_id(0)
        rc = 64
        d4, d16 = DILATIONS[1], DILATIONS[2]
        sh = S // n_part
        n4, n16 = sh // d4, sh // d16

        def heads(with_rope):
            for part in range(n_part):
                acc_ref[...] = _dot(x_ref[part * sh:(part + 1) * sh, :], w_ref[...])
                for hh in range(n_heads):
                    hs = slice(hh * DH, (hh + 1) * DH)

                    def rope_rows(ci, carry):
                        rows = pl.ds(pl.multiple_of(ci * rc, rc), rc)
                        tok = pl.ds(pl.multiple_of(part * sh + ci * rc, rc), rc)
                        t = acc_ref[rows, hs]
                        if with_rope:
                            t = _rope(t, c_ref[tok, :], sn_ref[tok, :])
                        h16_ref[tok, hs] = t.astype(BF16)
                        slab_ref[rows, :] = t
                        return carry

                    lax.fori_loop(0, sh // rc, rope_rows, 0, unroll=8)
                    for q in range(d4):
                        for c0 in range(0, n4, KB):
                            t = slab_ref[pl.ds(q + d4 * c0, KB, stride=d4), :]
                            h4_ref[q, part * n4 + c0:part * n4 + c0 + KB, hs] = t.astype(BF16)
                            res_ref[c0:c0 + KB, :] = t
                        for m in range(d16 // d4):
                            t = res_ref[pl.ds(m, n16, stride=d4), :]
                            h16r_ref[d4 * m + q, part * n16:(part + 1) * n16, hs] = t.astype(BF16)

        pl.when(W_PHASES * j + s < n_rope)(lambda: heads(True))
        pl.when((W_PHASES * j + s >= n_rope) & (j < j16))(lambda: heads(False))

        @pl.when(j >= j16)
        def _():
            h32_ref[...] = _dot(x_ref[...], w_ref[...])

    col16 = lambda j: W_PHASES * jnp.minimum(j, j16 - 1) + s
    col32 = lambda j: W_PHASES * jnp.maximum(j, j16) + s - n16
    hbm = pl.BlockSpec(memory_space=pl.ANY)
    in_specs = [pl.BlockSpec((S, D), lambda j: (0, 0), pipeline_mode=once),
                pl.BlockSpec((None, D, TN), lambda j: (j, 0, 0))]
    in_specs += [pl.BlockSpec((S, DH), lambda j: (0, 0), pipeline_mode=once)] * 2
    args = [xb, wg] + list(rope)
    aliases = {}
    if prev is not None:
        aliases = {len(args) + n: n for n in range(4)}
        in_specs, args = in_specs + [hbm] * 4, args + list(prev)
    out_specs = [pl.BlockSpec((S, TN), lambda j: (0, col16(j)))]
    out_specs += [pl.BlockSpec((d, S // d, TN), lambda j: (0, 0, col16(j))) for d in DILATIONS[1:]]
    out_specs += [pl.BlockSpec((S, TN), lambda j: (0, col32(j)))]
    out_shape = [SDS((S, HW), BF16)] + [SDS((d, S // d, HW), BF16) for d in DILATIONS[1:]] + [SDS((S, HW), F32)]
    return _call_carrying(
        body, rider, (N_DEV,), in_specs, out_specs, out_shape,
        [pltpu.VMEM((S // n_part, TN), F32), pltpu.VMEM((S // n_part, DH), F32),
         pltpu.VMEM((S // n_part // DILATIONS[1], DH), F32)], args,
        f"proj_phase{s}", aliases)


BQ = 2 * KB
LANES = 128


def _to_lane(acc, col, h):
    lane = lax.broadcasted_iota(jnp.int32, acc.shape, 1)
    return jnp.where(lane == h, col, acc)


def _attn_fwd(h16, name):
    d, n_sub, _ = h16.shape
    n_i = n_sub // BQ
    kw = KB + BQ

    def prev(i):
        return jnp.maximum(2 * i - 1, 0)

    def body(q_ref, kc_ref, kp_ref, vc_ref, vp_ref, o_ref, l_ref, kw_ref, vw_ref):
        i = pl.program_id(1)
        kw_ref[0:KB, :] = kp_ref[...]
        kw_ref[KB:kw, :] = kc_ref[...]
        vw_ref[0:KB, :] = vp_ref[...]
        vw_ref[KB:kw, :] = vc_ref[...]
        a = lax.broadcasted_iota(jnp.int32, (KB, 2 * KB), 0)
        b = lax.broadcasted_iota(jnp.int32, (KB, 2 * KB), 1)
        band = (b >= a) & (b <= a + KB)
        first_key = jnp.where(i == 0, KB, 0)
        masks = (band & (b >= first_key), band)
        for half in range(2):
            rs = slice(half * KB, (half + 1) * KB)
            ks = slice(half * KB, (half + 2) * KB)
            lse = jnp.zeros((KB, LANES), F32)
            for h in range(N_HEADS):
                hs = slice(h * DH, (h + 1) * DH)
                s = jnp.where(masks[half], _dot(q_ref[rs, hs], kw_ref[ks, hs], NT) * SCALE, NEG)
                m = jnp.max(s, axis=1, keepdims=True)
                p = jnp.exp(s - m)
                den = jnp.sum(p, axis=1, keepdims=True)
                o_ref[rs, hs] = _dot(p.astype(BF16), vw_ref[ks, hs]) / den
                lse = _to_lane(lse, m + jnp.log(den), h)
            l_ref[rs, :] = lse

    return pl.pallas_call(
        body, name=name, grid=(d, n_i),
        in_specs=[pl.BlockSpec((None, BQ, DA), lambda r, i: (r, i, 0)),
                  pl.BlockSpec((None, BQ, DA), lambda r, i: (r, i, 1)),
                  pl.BlockSpec((None, KB, DA), lambda r, i: (r, prev(i), 1)),
                  pl.BlockSpec((None, BQ, DA), lambda r, i: (r, i, 2)),
                  pl.BlockSpec((None, KB, DA), lambda r, i: (r, prev(i), 2))],
        out_specs=[pl.BlockSpec((None, BQ, DA), lambda r, i: (r, i, 0)),
                   pl.BlockSpec((None, BQ, LANES), lambda r, i: (r, i, 0))],
        out_shape=[SDS((d, n_sub, DA), F32), SDS((d, n_sub, LANES), F32)],
        scratch_shapes=[pltpu.VMEM((kw, DA), BF16), pltpu.VMEM((kw, DA), BF16)],
        compiler_params=_params(("arbitrary", "arbitrary")),
    )(h16, h16, h16, h16, h16)


def _pooled(ext_ref, g, rows, tm):
    w = POOL_WINDOWS[g]
    cs = slice(g * GC, (g + 1) * GC)
    cur = ext_ref[HALO:HALO + tm, cs]
    win = cur
    for j in range(1, w):
        win = win + ext_ref[HALO - j:HALO - j + tm, cs]
    cnt = jnp.minimum(rows + 1, w).astype(F32)
    return win / cnt - cur, cnt


def _fill_ext(ext_ref, u_ref, uh_ref, blk, tm):
    @pl.when(blk == 0)
    def _():
        ext_ref[0:HALO, :] = jnp.zeros((HALO, DP), F32)

    @pl.when(blk > 0)
    def _():
        ext_ref[0:HALO, :] = uh_ref[...]

    ext_ref[HALO:HALO + tm, :] = u_ref[...]


def _residue_specs(tm, width):
    return [pl.BlockSpec((d, tm // d, width), lambda i: (0, i, 0)) for d in DILATIONS[1:]]


def _mix(o1, l1, o4, l4, o16, l16, h32, wp, scale):
    tm = 256
    n_slab = N_HEADS + 1

    def body(o1r, l1r, o4r, l4r, o16r, l16r, u_ref, uh_ref, ga_ref, gp_ref, wp_ref, sc_ref,
             y_ref, yt_ref, attn_ref, lse_ref, lse4_ref, lse16_ref, ext_ref, ys_ref, nat_ref, ls_ref):
        i = pl.program_id(0)
        for n, (d, o_r, l_r) in enumerate(((DILATIONS[1], o4r, l4r), (DILATIONS[2], o16r, l16r))):
            for r in range(d):
                rows = pl.ds(r, tm // d, stride=d)
                for h in range(N_HEADS):
                    nat_ref[n, h, rows, :] = o_r[r, :, h * DH:(h + 1) * DH]
                nat_ref[n, N_HEADS, rows, :] = l_r[r]
        la, lb, lc = l1r[...], nat_ref[0, N_HEADS], nat_ref[1, N_HEADS]
        mx = jnp.maximum(jnp.maximum(la, lb), lc)
        ea, eb, ec = jnp.exp(la - mx), jnp.exp(lb - mx), jnp.exp(lc - mx)
        z = ea + eb + ec
        wa, wb, wc = ea / z, eb / z, ec / z
        lse = mx + jnp.log(z)
        lse_ref[...] = lse
        ls_ref[...] = lse
        for d, out in ((DILATIONS[1], lse4_ref), (DILATIONS[2], lse16_ref)):
            for r in range(d):
                out[r] = ls_ref[pl.ds(r, tm // d, stride=d), :]
        for h in range(N_HEADS):
            hs = slice(h * DH, (h + 1) * DH)
            hc = slice(h, h + 1)
            attn = wa[:, hc] * o1r[:, hs] + wb[:, hc] * nat_ref[0, h] + wc[:, hc] * nat_ref[1, h]
            attn_ref[:, hs] = attn
            gt = ga_ref[:, hs]
            ys_ref[:, hs] = attn * (gt * jax.nn.sigmoid(gt))

        _fill_ext(ext_ref, u_ref, uh_ref, i, tm)
        rows = i * tm + lax.broadcasted_iota(jnp.int32, (tm, 1), 0)
        for g in range(N_GROUPS):
            cs = slice(g * GC, (g + 1) * GC)
            gs = slice(DA + g * GC, DA + (g + 1) * GC)
            pooled, _ = _pooled(ext_ref, g, rows, tm)
            po = _dot(pooled.astype(BF16), wp_ref[g]) * sc_ref[:, cs]
            gt = gp_ref[:, cs]
            ys_ref[:, gs] = po * (gt * jax.nn.sigmoid(gt))
        yv = ys_ref[...]
        y_ref[...] = yv.astype(BF16)
        yt_ref[...] = yv.T.astype(BF16)

    row = lambda i: (i, 0)
    blk = pl.BlockSpec((tm, DA), row)
    lanes = pl.BlockSpec((tm, LANES), row)
    o_res, l_res = _residue_specs(tm, DA), _residue_specs(tm, LANES)
    return pl.pallas_call(
        body, name="mix", grid=(S // tm,),
        in_specs=[blk, lanes, o_res[0], l_res[0], o_res[1], l_res[1],
                  pl.BlockSpec((tm, DP), row),
                  pl.BlockSpec((HALO, DP), lambda i: (jnp.maximum(i * (tm // HALO) - 1, 0), 0)),
                  pl.BlockSpec((tm, DA), lambda i: (i, 1)), pl.BlockSpec((tm, DP), lambda i: (i, 2)),
                  pl.BlockSpec((N_GROUPS, GC, GC), lambda i: (0, 0, 0)),
                  pl.BlockSpec((1, DP), lambda i: (0, 0))],
        out_specs=[pl.BlockSpec((tm, D), row), pl.BlockSpec((D, tm), lambda i: (0, i)), blk, lanes] + l_res,
        out_shape=[SDS((S, D), BF16), SDS((D, S), BF16), SDS((S, DA), F32), SDS((S, LANES), F32)]
        + [SDS((d, S // d, LANES), F32) for d in DILATIONS[1:]],
        scratch_shapes=[pltpu.VMEM((HALO + tm, DP), F32), pltpu.VMEM((tm, D), F32),
                        pltpu.VMEM((2, n_slab, tm, DH), F32), pltpu.VMEM((tm, LANES), F32)],
        compiler_params=_params(("arbitrary",)),
    )(o1, l1, o4, l4, o16, l16, h32, h32, h32, h32, wp, scale)


def _outproj_ln(y, wout, x, target, gain, bias):
    tm = 512
    te = 16

    def body(y_ref, w_ref, x_ref, t_ref, g_ref, b_ref, dz_ref, dzb_ref, st_ref, outa_ref, outb_ref, part_ref):
        i = pl.program_id(0)
        th = tm // 2
        halves = (outa_ref, outb_ref)

        @pl.when(i == 0)
        def _():
            st_ref[...] = jnp.zeros((8, D), F32)

        def fold(v):
            return v[0:8] + v[8:16]

        def rows_group(e):
            rs = slice(e * te, (e + 1) * te)
            gn = g_ref[...]
            z = ALPHA * x_ref[rs, :] + halves[e * te // th][e * te % th:e * te % th + te, :]
            mu = jnp.mean(z, axis=1, keepdims=True)
            zc = z - mu
            var = jnp.mean(zc * zc, axis=1, keepdims=True)
            rstd = lax.rsqrt(var + LN_EPS)
            xhat = zc * rstd
            diff = xhat * gn + b_ref[...] - t_ref[rs, :]
            dyln = diff / D
            part_ref[e, 0] = fold(dyln * xhat)
            part_ref[e, 1] = fold(dyln)
            part_ref[e, 2] = fold(diff * diff)
            dxh = dyln * gn
            m1 = jnp.mean(dxh, axis=1, keepdims=True)
            m2 = jnp.mean(dxh * xhat, axis=1, keepdims=True)
            dz = rstd * (dxh - m1 - xhat * m2)
            dz_ref[rs, :] = dz
            dzb_ref[rs, :] = dz.astype(BF16)

        for half in range(2):
            halves[half][...] = _dot(y_ref[half * th:(half + 1) * th, :], w_ref[...])
            for e in range(half * th // te, (half + 1) * th // te):
                rows_group(e)
        sums = [part_ref[0, k] for k in range(3)]
        for e in range(1, tm // te):
            sums = [acc + part_ref[e, k] for k, acc in enumerate(sums)]
        st_ref[0:1, :] += jnp.sum(sums[0], axis=0, keepdims=True)
        st_ref[1:2, :] += jnp.sum(sums[1], axis=0, keepdims=True)
        sq = jnp.sum(jnp.sum(sums[2], axis=0, keepdims=True), axis=1, keepdims=True)
        st_ref[2:3, :] += jnp.broadcast_to(0.5 * sq / D, (1, D))

    row = lambda i: (i, 0)
    const = lambda i: (0, 0)
    return pl.pallas_call(
        body, name="outproj_ln", grid=(S // tm,),
        in_specs=[pl.BlockSpec((tm, D), row),
                  pl.BlockSpec((D, D), const, pipeline_mode=pl.Buffered(1)),
                  pl.BlockSpec((tm, D), row), pl.BlockSpec((tm, D), row),
                  pl.BlockSpec((1, D), const), pl.BlockSpec((1, D), const)],
        out_specs=[pl.BlockSpec((tm, D), row), pl.BlockSpec((tm, D), row), pl.BlockSpec((8, D), const)],
        out_shape=[SDS((S, D), F32), SDS((S, D), BF16), SDS((8, D), F32)],
        scratch_shapes=[pltpu.VMEM((tm // 2, D), F32), pltpu.VMEM((tm // 2, D), F32),
                        pltpu.VMEM((tm // te, 3, 8, D), F32)],
        compiler_params=_params(("arbitrary",)),
    )(y, wout, x, target, gain, bias)


def _bwd_mid(dzb, wout, h32, attn, wp, scale, rider=None):
    tm = 256
    n = S // tm

    def body(dz_ref, w_ref, ga_ref, gp_ref, at_ref, u_ref, uh_ref, wp_ref, sc_ref,
             dh_ref, do_ref, do4_ref, do16_ref, dd_ref, dd4_ref, dd16_ref, dwp_ref, dwp16_ref, gps_ref,
             ext_ref, eext_ref, acc_ref, nat_ref, ds_ref):
        i = pl.program_id(0)
        ib = n - 1 - i

        @pl.when(i == 0)
        def _():
            eext_ref[tm:tm + HALO, :] = jnp.zeros((HALO, DP), F32)
            acc_ref[...] = jnp.zeros((N_GROUPS, GC, GC), F32)
            gps_ref[...] = jnp.zeros((8, DP), F32)

        dy = _dot(dz_ref[...], w_ref[...], NT)

        def through_gate(dy_part, gt):
            sg = jax.nn.sigmoid(gt)
            return dy_part * (gt * sg), dy_part * (sg * (1.0 + gt * (1.0 - sg)))

        dat, dyg_a = through_gate(dy[:, 0:DA], ga_ref[...])
        dmix_p, dyg_p = through_gate(dy[:, DA:D], gp_ref[...])

        at = at_ref[...]
        do_ref[...] = dat.astype(BF16)
        dh_ref[:, DP:DP + DA] = (dyg_a * at).astype(BF16)
        prod = dat * at
        dd = jnp.zeros((tm, LANES), F32)
        for h in range(N_HEADS):
            hs = slice(h * DH, (h + 1) * DH)
            dd = _to_lane(dd, jnp.sum(prod[:, hs], axis=1, keepdims=True), h)
            nat_ref[h] = dat[:, hs]
        dd_ref[...] = dd
        ds_ref[...] = dd
        for d, do_out, dd_out in ((DILATIONS[1], do4_ref, dd4_ref), (DILATIONS[2], do16_ref, dd16_ref)):
            for r in range(d):
                dd_out[r] = _residues(ds_ref, (), d, r, tm // d)
                for h in range(N_HEADS):
                    do_out[r, :, h * DH:(h + 1) * DH] = _residues(nat_ref, (h,), d, r, tm // d).astype(BF16)

        _fill_ext(ext_ref, u_ref, uh_ref, ib, tm)
        rows = ib * tm + lax.broadcasted_iota(jnp.int32, (tm, 1), 0)
        for g in range(N_GROUPS):
            w = POOL_WINDOWS[g]
            cs = slice(g * GC, (g + 1) * GC)
            pooled, cnt = _pooled(ext_ref, g, rows, tm)
            pre = _dot(pooled.astype(BF16), wp_ref[g])
            sc = sc_ref[:, cs]
            dpo = dmix_p[:, cs]
            gps_ref[0:1, cs] += jnp.sum(dpo * pre, axis=0, keepdims=True)
            dh_ref[:, DP + DA + g * GC:DP + DA + (g + 1) * GC] = (dyg_p[:, cs] * (pre * sc)).astype(BF16)
            dpre = (dpo * sc).astype(BF16)
            acc_ref[g] += _dot(pooled.T.astype(BF16), dpre)
            dpooled = _dot(dpre, wp_ref[g], NT)
            eext_ref[0:tm, cs] = dpooled / cnt
            du = eext_ref[0:tm, cs]
            for j in range(1, w):
                du = du + eext_ref[j:j + tm, cs]
            dh_ref[:, cs] = (du - dpooled).astype(BF16)
        eext_ref[tm:tm + HALO, :] = eext_ref[0:HALO, :]

        @pl.when(i == n - 1)
        def _():
            for j in range(N_DEV):
                for g in range(N_GROUPS):
                    blk = acc_ref[g, j * PB:(j + 1) * PB, :]
                    dwp_ref[j, g * PB:(g + 1) * PB, :] = blk
                    dwp16_ref[j, g * PB:(g + 1) * PB, :] = blk.astype(BF16)

    rev = lambda i: (n - 1 - i, 0)
    const = lambda i: (0, 0)
    res = lambda width: [pl.BlockSpec((d, tm // d, width), lambda i: (0, n - 1 - i, 0)) for d in DILATIONS[1:]]
    pool_blocks = pl.BlockSpec((N_DEV, N_GROUPS * PB, GC), lambda i: (0, 0, 0))
    outs, carried = _call_carrying(
        body, rider, (n,),
        [pl.BlockSpec((tm, D), rev),
         pl.BlockSpec((D, D), const, pipeline_mode=pl.Buffered(1)),
         pl.BlockSpec((tm, DA), lambda i: (n - 1 - i, 1)), pl.BlockSpec((tm, DP), lambda i: (n - 1 - i, 2)),
         pl.BlockSpec((tm, DA), rev), pl.BlockSpec((tm, DP), rev),
         pl.BlockSpec((HALO, DP), lambda i: (jnp.maximum((n - 1 - i) * (tm // HALO) - 1, 0), 0)),
         pl.BlockSpec((N_GROUPS, GC, GC), lambda i: (0, 0, 0)),
         pl.BlockSpec((1, DP), const)],
        [pl.BlockSpec((tm, DP + D), lambda i: (n - 1 - i, 1)), pl.BlockSpec((tm, DA), rev)] + res(DA)
        + [pl.BlockSpec((tm, LANES), rev)] + res(LANES) + [pool_blocks, pool_blocks, pl.BlockSpec((8, DP), const)],
        [SDS((S, D_IN), BF16), SDS((S, DA), BF16)] + [SDS((d, S // d, DA), BF16) for d in DILATIONS[1:]]
        + [SDS((S, LANES), F32)] + [SDS((d, S // d, LANES), F32) for d in DILATIONS[1:]]
        + [SDS((N_DEV, N_GROUPS * PB, GC), F32), SDS((N_DEV, N_GROUPS * PB, GC), BF16), SDS((8, DP), F32)],
        [pltpu.VMEM((HALO + tm, DP), F32), pltpu.VMEM((tm + HALO, DP), F32),
         pltpu.VMEM((N_GROUPS, GC, GC), F32), pltpu.VMEM((N_HEADS, tm, DH), F32), pltpu.VMEM((tm, LANES), F32)],
        [dzb, wout, h32, h32, attn, h32, h32, wp, scale], "bwd_mid")
    return outs, carried


def _attn_bwd(h16, do, lse, dd, rope, name, others=None, dh=None):
    d, n_sub, _ = h16.shape
    n_i = n_sub // BQ
    n_kb = n_sub // KB
    qw = BQ + KB
    final = others is not None
    out_dtype = BF16 if final else F32
    n_cb = 3 * DA // DH

    def nxt(i):
        return jnp.minimum(2 * i + 2, n_kb - 1)

    def body(qc_ref, qn_ref, kc_ref, vc_ref, doc_ref, don_ref, lc_ref, ln_ref, dc_ref, dn_ref,
             c_ref, sn_ref, *rest):
        if final:
            acc4_ref, acc16_ref, _, out_ref, carry_ref, qw_ref, dow_ref, lw_ref, dw_ref, nat_ref = rest
        else:
            out_ref, carry_ref, qw_ref, dow_ref, lw_ref, dw_ref = rest
        i = pl.program_id(1)

        @pl.when(i == 0)
        def _():
            carry_ref[...] = jnp.zeros((KB, DA), F32)

        for win, own, after in ((qw_ref, qc_ref, qn_ref), (dow_ref, doc_ref, don_ref), (lw_ref, lc_ref, ln_ref),
                                (dw_ref, dc_ref, dn_ref)):
            win[0:BQ, :] = own[...]
            win[BQ:qw, :] = after[...]
        if final:
            for n, (dil, acc) in enumerate(((DILATIONS[1], acc4_ref), (DILATIONS[2], acc16_ref))):
                for r in range(dil):
                    for cb in range(n_cb):
                        nat_ref[n, cb, pl.ds(r, BQ // dil, stride=dil), :] = acc[r, :, cb * DH:(cb + 1) * DH]

        a = lax.broadcasted_iota(jnp.int32, (qw, BQ), 0)
        b = lax.broadcasted_iota(jnp.int32, (qw, BQ), 1)
        n_q = jnp.where(i == n_i - 1, BQ, qw)
        mask = (b <= a) & (a <= b + KB) & (a < n_q)
        tabs = (c_ref[...], sn_ref[...])
        for h in range(N_HEADS):
            hs = slice(h * DH, (h + 1) * DH)
            hc = slice(h, h + 1)
            q, k, vv, dob = qw_ref[:, hs], kc_ref[:, hs], vc_ref[:, hs], dow_ref[:, hs]
            s = _dot(q, k, NT) * SCALE
            p = jnp.exp(jnp.where(mask, s - lw_ref[:, hc], NEG))
            dp = _dot(dob, vv, NT)
            ds = (p * (dp - dw_ref[:, hc]) * SCALE).astype(BF16)
            dq = _dot(ds, k)
            dk = _dot(ds, q, T_N)
            dv = _dot(p.astype(BF16), dob, T_N)
            dq_lo = dq[0:KB] + carry_ref[:, hs]
            carry_ref[:, hs] = dq[BQ:qw]
            dq_own = _rope_t(jnp.concatenate([dq_lo, dq[KB:BQ]], axis=0), *tabs)
            for base, gv in ((0, dq_own), (DA, _rope_t(dk, *tabs)), (2 * DA, dv)):
                if final:
                    cb = base // DH + h
                    gv = gv + nat_ref[0, cb] + nat_ref[1, cb]
                out_ref[:, base + h * DH:base + (h + 1) * DH] = gv.astype(out_dtype)

    def cur(rows, width, col=0):
        return pl.BlockSpec((None, rows, width), lambda r, i: (r, i, col))

    def nx(width, col=0):
        return pl.BlockSpec((None, KB, width), lambda r, i: (r, nxt(i), col))

    in_specs = [cur(BQ, DA), nx(DA), cur(BQ, DA, 1), cur(BQ, DA, 2), cur(BQ, DA), nx(DA),
                cur(BQ, LANES), nx(LANES), cur(BQ, LANES), nx(LANES)] + [cur(BQ, DH)] * 2
    args = [h16, h16, h16, h16, do, do, lse, lse, dd, dd] + list(rope)
    scratch = [pltpu.VMEM((KB, DA), F32), pltpu.VMEM((qw, DA), BF16), pltpu.VMEM((qw, DA), BF16),
               pltpu.VMEM((qw, LANES), F32), pltpu.VMEM((qw, LANES), F32)]
    if final:
        assert d == 1
        in_specs += [pl.BlockSpec((dil, BQ // dil, 3 * DA), lambda r, i: (0, i, 0)) for dil in DILATIONS[1:]]
        in_specs.append(pl.BlockSpec(memory_space=pl.ANY))
        args += list(others) + [dh[None]]
        scratch.append(pltpu.VMEM((2, n_cb, BQ, DH), F32))
    return pl.pallas_call(
        body, name=name, grid=(d, n_i),
        in_specs=in_specs,
        out_specs=cur(BQ, 3 * DA),
        out_shape=SDS((d, n_sub, D_IN if final else 3 * DA), out_dtype),
        input_output_aliases={len(args) - 1: 0} if final else {},
        scratch_shapes=scratch,
        compiler_params=_params(("arbitrary", "arbitrary")),
    )(*args)


def _grad_w_out(yt, dzb):
    tn = 256

    def body(yt_ref, dz_ref, o_ref, o16_ref):
        acc = _dot(yt_ref[...], dz_ref[...])
        o_ref[...] = acc
        o16_ref[...] = acc.astype(BF16)

    cols = pl.BlockSpec((D, tn), lambda c: (0, c))
    o, o16 = pl.pallas_call(
        body, name="grad_w_out", grid=(D // tn,),
        in_specs=[pl.BlockSpec((D, S), lambda c: (0, 0), pipeline_mode=pl.Buffered(1)),
                  pl.BlockSpec((S, tn), lambda c: (0, c))],
        out_specs=[cols, cols],
        out_shape=[SDS((D, D), F32), SDS((D, D), BF16)],
        compiler_params=_params(("arbitrary",)),
    )(yt, dzb)
    return o.reshape(N_DEV, OB, D), o16.reshape(N_DEV, OB, D)


def _grad_w_in(xt, dh):
    tn = 256
    per_blk = WB // tn

    def body(xt_ref, dh_ref, o_ref, o16_ref):
        acc = _dot(xt_ref[...], dh_ref[...])
        o_ref[...] = acc
        o16_ref[...] = acc.astype(BF16)

    blk = pl.BlockSpec((None, D, tn), lambda c: (c // per_blk, 0, c % per_blk))
    return pl.pallas_call(
        body, name="grad_w_in", grid=(D_IN // tn,),
        in_specs=[pl.BlockSpec((D, S), lambda c: (0, 0), pipeline_mode=pl.Buffered(1)),
                  pl.BlockSpec((S, tn), lambda c: (0, c))],
        out_specs=[blk, blk],
        out_shape=[SDS((N_DEV, D, WB), F32), SDS((N_DEV, D, WB), BF16)],
        compiler_params=_params(("arbitrary",)),
    )(xt, dh)


def _grad_x(dz, dh, wgs, rider=None):
    tm = 1024

    def body(dz_ref, dh_ref, *rest):
        o_ref = rest[-1]

        @pl.when(pl.program_id(1) == 0)
        def _():
            o_ref[...] = ALPHA * dz_ref[...]

        acc = _dot(dh_ref[:, 0:TN], rest[0][...], NT)
        for s in range(1, W_PHASES):
            acc = acc + _dot(dh_ref[:, s * TN:(s + 1) * TN], rest[s][...], NT)
        o_ref[...] += acc

    outs, carried = _call_carrying(
        body, rider, (S // tm, N_DEV),
        [pl.BlockSpec((tm, D), lambda i, j: (i, 0)), pl.BlockSpec((tm, WB), lambda i, j: (i, j))]
        + [pl.BlockSpec((None, D, TN), lambda i, j: (j, 0, 0))] * W_PHASES,
        [pl.BlockSpec((tm, D), lambda i, j: (i, 0))], [SDS((S, D), F32)], [], [dz, dh] + list(wgs), "grad_x")
    return outs[0], carried


def _place():
    x, y, c = lax.axis_index("x"), lax.axis_index("y"), lax.axis_index("c")
    chips = [(x, y), (1 - x, y), (x, 1 - y), (1 - x, 1 - y)]
    return x, y, c, chips


def _blk(x, y, c):
    return 4 * x + 2 * y + c


def _adamw(w, g, m, v):
    m = ADAM_B1 * m + (1.0 - ADAM_B1) * g
    v = ADAM_B2 * v + (1.0 - ADAM_B2) * (g * g)
    m_hat = m / (1.0 - ADAM_B1 ** ADAM_STEP)
    v_hat = v / (1.0 - ADAM_B2 ** ADAM_STEP)
    delta = -ADAM_LR * (m_hat / (jnp.sqrt(v_hat) + ADAM_EPS) + ADAM_WD * w)
    return delta, m, v


class _Rider:
    def __init__(self, args, in_specs, out_shape, out_specs, scratch, start, finish):
        self.args, self.in_specs, self.out_shape, self.out_specs = args, in_specs, out_shape, out_specs
        self.scratch, self.start, self.finish = scratch, start, finish


def _carry(body, rider, n_in, n_out, first, last):
    if rider is None:
        return body
    r_in, r_out, r_scr = len(rider.args), len(rider.out_shape), len(rider.scratch)

    def carrying(*refs):
        o0 = n_in + r_in
        s0 = o0 + n_out + r_out
        s1 = len(refs) - r_scr
        theirs = (refs[n_in:o0], refs[o0 + n_out:s0], refs[s1:])
        pl.when(first())(lambda: rider.start(*theirs))
        body(*refs[:n_in], *refs[o0:o0 + n_out], *refs[s0:s1])
        pl.when(last())(lambda: rider.finish(*theirs))

    return carrying


def _call_carrying(body, rider, grid, in_specs, out_specs, out_shape, scratch, args, name, aliases=None):
    n_in, n_out = len(in_specs), len(out_specs)
    ids = lambda: [pl.program_id(a) for a in range(len(grid))]
    first = lambda: functools.reduce(jnp.logical_and, [i == 0 for i in ids()])
    last = lambda: functools.reduce(jnp.logical_and, [i == n - 1 for i, n in zip(ids(), grid)])
    if rider is not None:
        in_specs, args = in_specs + rider.in_specs, list(args) + rider.args
        out_specs, out_shape = out_specs + rider.out_specs, out_shape + rider.out_shape
        scratch = scratch + rider.scratch
    outs = pl.pallas_call(
        _carry(body, rider, n_in, n_out, first, last), name=name, grid=grid,
        in_specs=in_specs, out_specs=out_specs, out_shape=out_shape, scratch_shapes=scratch,
        input_output_aliases=aliases or {}, compiler_params=_params(("arbitrary",) * len(grid)),
    )(*args)
    return list(outs[:n_out]), list(outs[n_out:])


def _gather_copy(tensors, send_sems, recv_sems, t, k, block, to, src=None):
    dst = tensors[t][1](_blk(*block))
    return pltpu.make_async_remote_copy(
        src_ref=dst if src is None else src, dst_ref=dst,
        send_sem=send_sems.at[t, k], recv_sem=recv_sems.at[t, k], device_id=to, device_id_type=MESH)


def _gather_start(tensors, send_sems, recv_sems, local_sems):
    x, y, c, chips = _place()
    me, sib = (x, y, c), (x, y, 1 - c)
    for t, (src, dst) in enumerate(tensors):
        pltpu.make_async_copy(src, dst(_blk(*me)), local_sems.at[t]).start()
        _gather_copy(tensors, send_sems, recv_sems, t, 0, me, sib, src).start()
        for j in (1, 2, 3):
            _gather_copy(tensors, send_sems, recv_sems, t, j, me, (*chips[j], c), src).start()


def _gather_finish(tensors, send_sems, recv_sems, local_sems):
    x, y, c, chips = _place()
    me, sib = (x, y, c), (x, y, 1 - c)
    copy = functools.partial(_gather_copy, tensors, send_sems, recv_sems)
    for t in range(len(tensors)):
        for j in (1, 2, 3):
            copy(t, j, (*chips[j], c), me).wait_recv()
            copy(t, 3 + j, (*chips[j], c), sib).start()
    for t, (src, dst) in enumerate(tensors):
        copy(t, 0, sib, me).wait_recv()
        for j in (1, 2, 3):
            copy(t, 3 + j, (*chips[j], 1 - c), me).wait_recv()
        copy(t, 0, me, sib, src).wait_send()
        for j in (1, 2, 3):
            copy(t, j, me, (*chips[j], c), src).wait_send()
            copy(t, 3 + j, (*chips[j], c), sib).wait_send()
        pltpu.make_async_copy(src, dst(_blk(*me)), local_sems.at[t]).wait()


W_PHASES = WB // TN


def _cast_shards(w_in_s, w_out_s, w_pool_s):
    def body(*refs):
        for src, dst in zip(refs[:3], refs[3:]):
            dst[...] = src[...].astype(BF16)

    vmem = pl.BlockSpec(memory_space=pltpu.VMEM)
    return pl.pallas_call(
        body, name="cast_shards", in_specs=[vmem] * 3, out_specs=[vmem] * 3,
        out_shape=[SDS(w.shape, BF16) for w in (w_in_s, w_out_s, w_pool_s)],
        compiler_params=_params(),
    )(w_in_s, w_out_s, w_pool_s)


def _gather_w_in_rider(shard16, s):
    def tensors(ins, outs):
        return [(ins[0].at[:, pl.ds(s * TN, TN)], lambda b: outs[0].at[b])]

    hbm = pl.BlockSpec(memory_space=pl.ANY)
    return _Rider(
        args=[shard16], in_specs=[hbm], out_shape=[SDS((N_DEV, D, TN), BF16)], out_specs=[hbm],
        scratch=[pltpu.SemaphoreType.DMA((1, 7)), pltpu.SemaphoreType.DMA((1, 7)), pltpu.SemaphoreType.DMA((1,))],
        start=lambda ins, outs, scr: _gather_start(tensors(ins, outs), *scr),
        finish=lambda ins, outs, scr: _gather_finish(tensors(ins, outs), *scr))


def _gather_small_rider(w_out16, w_pool16):
    def tensors(ins, outs):
        gout_ref, gpool_ref = outs

        def pool_rows(b):
            return gpool_ref.at[:, pl.ds(pl.multiple_of(b * PB, PB), PB), :]

        return [(ins[0], lambda b: gout_ref.at[b]), (ins[1], pool_rows)]

    hbm = pl.BlockSpec(memory_space=pl.ANY)
    return _Rider(
        args=[w_out16, w_pool16], in_specs=[hbm, hbm],
        out_shape=[SDS((N_DEV, OB, D), BF16), SDS((N_GROUPS, GC, GC), BF16)], out_specs=[hbm, hbm],
        scratch=[pltpu.SemaphoreType.DMA((2, 7)), pltpu.SemaphoreType.DMA((2, 7)), pltpu.SemaphoreType.DMA((2,))],
        start=lambda ins, outs, scr: _gather_start(tensors(ins, outs), *scr),
        finish=lambda ins, outs, scr: _gather_finish(tensors(ins, outs), *scr))


def _block_table():
    x, y, c, chips = _place()
    return jnp.stack([_blk(*chip, c) for chip in chips]).astype(jnp.int32)


def _rs_exchange(name, p16):
    _, r_tot, cols = p16.shape

    def body(p16_ref, r1_ref, send_sems, recv_sems):
        x, y, c, chips = _place()
        copies = [pltpu.make_async_remote_copy(
            src_ref=p16_ref.at[_blk(*chips[k], 1 - c)], dst_ref=r1_ref.at[k],
            send_sem=send_sems.at[k], recv_sem=recv_sems.at[k], device_id=(x, y, 1 - c), device_id_type=MESH)
            for k in range(4)]
        for cp in copies:
            cp.start()
        for cp in copies:
            cp.wait()

    hbm = pl.BlockSpec(memory_space=pl.ANY)
    return pl.pallas_call(
        body, name=name, in_specs=[hbm], out_specs=hbm, out_shape=SDS((4, r_tot, cols), BF16),
        scratch_shapes=[pltpu.SemaphoreType.DMA((4,)), pltpu.SemaphoreType.DMA((4,))],
        compiler_params=_params(),
    )(p16)


def _chip_sums(name, table, p32, r1, rc):
    _, r_tot, cols = p32.shape

    def body(tbl_ref, p_ref, r_ref, o_ref):
        o_ref[...] = (p_ref[...] + r_ref[...].astype(F32)).astype(BF16)

    return pl.pallas_call(
        body, name=name,
        grid_spec=pltpu.PrefetchScalarGridSpec(
            num_scalar_prefetch=1, grid=(3, r_tot // rc),
            in_specs=[pl.BlockSpec((None, rc, cols), lambda k, ch, tbl: (tbl[k + 1], ch, 0)),
                      pl.BlockSpec((None, rc, cols), lambda k, ch, tbl: (k + 1, ch, 0))],
            out_specs=pl.BlockSpec((None, rc, cols), lambda k, ch, tbl: (k, ch, 0))),
        out_shape=SDS((3, r_tot, cols), BF16),
        compiler_params=_params(("arbitrary", "arbitrary")),
    )(table, p32, r1)


def _stage2_rider(sums, stats=None):
    n_t = len(sums)

    def copies(ins, outs, scr):
        x, y, c, chips = _place()
        out = []
        for t in range(n_t):
            for k in (1, 2, 3):
                out.append(pltpu.make_async_remote_copy(
                    src_ref=ins[t].at[k - 1], dst_ref=outs[t].at[k - 1],
                    send_sem=scr[0].at[t, k - 1], recv_sem=scr[1].at[t, k - 1],
                    device_id=(*chips[k], c), device_id_type=MESH))
        if stats is not None:
            for k in range(1, N_DEV):
                peer = (x ^ ((k >> 2) & 1), y ^ ((k >> 1) & 1), c ^ (k & 1))
                out.append(pltpu.make_async_remote_copy(
                    src_ref=scr[4], dst_ref=outs[n_t].at[_blk(x, y, c)],
                    send_sem=scr[2].at[k - 1], recv_sem=scr[3].at[k - 1], device_id=peer, device_id_type=MESH))
        return out

    def own_rows(outs, scr):
        x, y, c, _ = _place()
        return pltpu.make_async_copy(scr[4], outs[n_t].at[_blk(x, y, c)], scr[5])

    def start(ins, outs, scr):
        if stats is not None:
            scr[4][...] = ins[n_t][...]
            scr[4][3:4, 0:DP] = ins[n_t + 1][0:1, :]
            own_rows(outs, scr).start()
        for cp in copies(ins, outs, scr):
            cp.start()

    def finish(ins, outs, scr):
        for cp in copies(ins, outs, scr):
            cp.wait()
        if stats is not None:
            own_rows(outs, scr).wait()

    vmem = pl.BlockSpec(memory_space=pltpu.VMEM)
    hbm = pl.BlockSpec(memory_space=pl.ANY)
    scratch = [pltpu.SemaphoreType.DMA((n_t, 3)), pltpu.SemaphoreType.DMA((n_t, 3))]
    args, in_specs = list(sums), [hbm] * n_t
    out_shape, out_specs = [SDS(s.shape, BF16) for s in sums], [hbm] * n_t
    if stats is not None:
        scratch += [pltpu.SemaphoreType.DMA((N_DEV - 1,)), pltpu.SemaphoreType.DMA((N_DEV - 1,)),
                    pltpu.VMEM((8, D), F32), pltpu.SemaphoreType.DMA(())]
        args, in_specs = args + list(stats), in_specs + [vmem, vmem]
        out_shape, out_specs = out_shape + [SDS((N_DEV, 8, D), F32)], out_specs + [hbm]
    return _Rider(args, in_specs, out_shape, out_specs, scratch, start, finish)


def _adamw_shard(name, table, p32, r1, r2, w, m, v, rc):
    _, r_tot, cols = p32.shape

    def body(tbl_ref, p_ref, r1_ref, r2_ref, w_ref, m_ref, v_ref, g_ref, d_ref, nm_ref, nv_ref):
        g = p_ref[...] + r1_ref[...].astype(F32)
        for k in range(3):
            g = g + r2_ref[k].astype(F32)
        delta, nm, nv = _adamw(w_ref[...], g, m_ref[...], v_ref[...])
        g_ref[...] = g
        d_ref[...] = delta
        nm_ref[...] = nm
        nv_ref[...] = nv

    rows = pl.BlockSpec((rc, cols), lambda ch, tbl: (ch, 0))
    shard = SDS((r_tot, cols), F32)
    return pl.pallas_call(
        body, name=name,
        grid_spec=pltpu.PrefetchScalarGridSpec(
            num_scalar_prefetch=1, grid=(r_tot // rc,),
            in_specs=[pl.BlockSpec((None, rc, cols), lambda ch, tbl: (tbl[0], ch, 0)),
                      pl.BlockSpec((None, rc, cols), lambda ch, tbl: (0, ch, 0)),
                      pl.BlockSpec((3, rc, cols), lambda ch, tbl: (0, ch, 0)), rows, rows, rows],
            out_specs=[rows, rows, rows, rows]),
        out_shape=[shard, shard, shard, shard],
        compiler_params=_params(("arbitrary",)),
    )(table, p32, r1, r2, w, m, v)


def _replicated_adamw(gathered, gain, bias, scale, m_gain, m_bias, m_scale, v_gain, v_bias, v_scale):
    def body(all_ref, g_ref, b_ref, s_ref, mg_ref, mb_ref, ms_ref, vg_ref, vb_ref, vs_ref,
             tot_ref, dl_ref, nm_ref, nv_ref):
        tot = all_ref[0]
        for b in range(1, N_DEV):
            tot = tot + all_ref[b]
        tot_ref[...] = tot
        dl_ref[...] = jnp.zeros((8, D), F32)
        nm_ref[...] = jnp.zeros((8, D), F32)
        nv_ref[...] = jnp.zeros((8, D), F32)
        for row, width, w_r, m_r, v_r in ((0, D, g_ref, mg_ref, vg_ref), (1, D, b_ref, mb_ref, vb_ref),
                                          (3, DP, s_ref, ms_ref, vs_ref)):
            delta, nm, nv = _adamw(w_r[...], tot[row:row + 1, 0:width], m_r[...], v_r[...])
            dl_ref[row:row + 1, 0:width] = delta
            nm_ref[row:row + 1, 0:width] = nm
            nv_ref[row:row + 1, 0:width] = nv

    vmem = pl.BlockSpec(memory_space=pltpu.VMEM)
    rows = SDS((8, D), F32)
    return pl.pallas_call(
        body, name="replicated_adamw",
        in_specs=[vmem] * 10, out_specs=[vmem] * 4, out_shape=[rows, rows, rows, rows],
        compiler_params=_params(),
    )(gathered, gain, bias, scale, m_gain, m_bias, m_scale, v_gain, v_bias, v_scale)


def kernel(x, w_in, w_pool, pool_scale, w_out, ln_gain, ln_bias, loss_target, m_w_in, m_w_pool, m_pool_scale, m_w_out, m_ln_gain, m_ln_bias, v_w_in, v_w_pool, v_pool_scale, v_w_out, v_ln_gain, v_ln_bias):
    pool_rows = (N_GROUPS * PB, GC)
    x2, target = x[0], loss_target[0]
    rope = lax.optimization_barrier(_rope_tables())

    table = _block_table()
    by_res = lambda t, d: t.reshape(S // d, d, DH).transpose(1, 0, 2)
    ropes = [[by_res(t, d) for t in rope] for d in DILATIONS]

    shard16, w_out16, w_pool16 = _cast_shards(w_in[0], w_out[0], w_pool[0])
    (xb, xt), (wg0,) = _prep_x(x2, _gather_w_in_rider(shard16, 0))
    h, (wg1,) = _proj_phase(xb, wg0, 0, rope, None, _gather_w_in_rider(shard16, 1))
    h, (wg2,) = _proj_phase(xb, wg1, 1, rope, h, _gather_w_in_rider(shard16, 2))
    h, (wg_out, wg_pool) = _proj_phase(xb, wg2, 2, rope, h, _gather_small_rider(w_out16, w_pool16))
    wg_out = wg_out.reshape(D, D)
    h16, h32 = [h[0][None], h[1], h[2]], h[3]
    fwd = [_attn_fwd(h16[n], f"attn_fwd_d{d}") for n, d in enumerate(DILATIONS)]
    y, yt, attn, lse1, lse4, lse16 = _mix(fwd[0][0][0], fwd[0][1][0], *fwd[1], *fwd[2], h32, wg_pool, pool_scale)
    dz, dzb, stats = _outproj_ln(y, wg_out, x2, target, ln_gain, ln_bias)

    dwout, dwout16 = _grad_w_out(yt, dzb)
    r1_out = _rs_exchange("rs_out_exchange", dwout16)
    s2_out = _chip_sums("rs_out_sums", table, dwout, r1_out, OB)
    mid, (r2_out,) = _bwd_mid(dzb, wg_out, h32, attn, wg_pool, pool_scale, rider=_stage2_rider([s2_out]))
    dh, do1, do4, do16, dd1, dd4, dd16, dwp, dwp16, gps = mid
    g_out, d_out, nm_out, nv_out = _adamw_shard(
        "adamw_w_out", table, dwout, r1_out, r2_out, w_out[0], m_w_out[0], v_w_out[0], OB // 2)

    do, lse, dd = [do1[None], do4, do16], [lse1[None], lse4, lse16], [dd1[None], dd4, dd16]
    others = [_attn_bwd(h16[n], do[n], lse[n], dd[n], ropes[n], f"attn_bwd_d{DILATIONS[n]}") for n in (1, 2)]
    dh = _attn_bwd(h16[0], do[0], lse[0], dd[0], ropes[0], "attn_bwd_d1", others=others, dh=dh)[0]

    dwin, dwin16 = _grad_w_in(xt, dh)
    r1_in = _rs_exchange("rs_in_exchange", dwin16)
    s2_in = _chip_sums("rs_in_sums", table, dwin, r1_in, 512)
    r1_pool = _rs_exchange("rs_pool_exchange", dwp16)
    s2_pool = _chip_sums("rs_pool_sums", table, dwp, r1_pool, N_GROUPS * PB)
    grad_x, (r2_in, r2_pool, gathered) = _grad_x(
        dz, dh, (wg0, wg1, wg2), rider=_stage2_rider([s2_in, s2_pool], stats=(stats, gps)))
    g_in, d_in, nm_in, nv_in = _adamw_shard(
        "adamw_w_in", table, dwin, r1_in, r2_in, w_in[0], m_w_in[0], v_w_in[0], 256)
    g_pool, d_pool, nm_pool, nv_pool = _adamw_shard(
        "adamw_w_pool", table, dwp, r1_pool, r2_pool, w_pool[0].reshape(pool_rows), m_w_pool[0].reshape(pool_rows),
        v_w_pool[0].reshape(pool_rows), N_GROUPS * PB)
    tot, dl, nm, nv = _replicated_adamw(gathered, ln_gain, ln_bias, pool_scale, m_ln_gain, m_ln_bias, m_pool_scale,
                                        v_ln_gain, v_ln_bias, v_pool_scale)

    shard4 = lambda t: t.reshape(1, N_GROUPS, PB, GC)
    lead = lambda t: t[None]
    small = lambda t: (t[3:4, 0:DP], t[0:1], t[1:2])
    g_ps, g_gain, g_bias = small(tot)
    d_ps, d_gain, d_bias = small(dl)
    nm_ps, nm_gain, nm_bias = small(nm)
    nv_ps, nv_gain, nv_bias = small(nv)
    return (tot[2, 0], lead(grad_x),
            lead(g_in), shard4(g_pool), g_ps, lead(g_out), g_gain, g_bias,
            lead(d_in), shard4(d_pool), d_ps, lead(d_out), d_gain, d_bias,
            lead(nm_in), shard4(nm_pool), nm_ps, lead(nm_out), nm_gain, nm_bias,
            lead(nv_in), shard4(nv_pool), nv_ps, lead(nv_out), nv_gain, nv_bias)
```

```python
import functools

import jax
import jax.numpy as jnp
from jax import lax
from jax.experimental import pallas as pl
from jax.experimental.pallas import tpu as pltpu

F32 = jnp.float32
BF16 = jnp.bfloat16
SDS = jax.ShapeDtypeStruct
MESH = pl.DeviceIdType.MESH

N_DEV = 8
S = 4096
D = 2048
N_HEADS = 8
DH = 128
DA = N_HEADS * DH
DP = 1024
N_GROUPS = 4
GC = DP // N_GROUPS
POOL_WINDOWS = (2, 4, 8, 16)
HALO = 16
D_IN = 3 * DA + DP + D
WB = D_IN // N_DEV
TN = 256
HW = 3 * DA
OB = D // N_DEV
PB = GC // N_DEV
ROPE_DIM = DH // 4
ROPE_HALF = ROPE_DIM // 2
ROPE_THETA = 500000.0
DILATIONS = (1, 4, 16)
KB = 128
LN_EPS = 1e-5
ALPHA = 2.0 ** 0.25
SCALE = DH ** -0.5
NEG = -1e30
ADAM_LR, ADAM_B1, ADAM_B2, ADAM_EPS, ADAM_WD, ADAM_STEP = 0.001, 0.9, 0.999, 1e-08, 0.01, 10

VMEM_LIMIT_V7X = 61 * 1024 * 1024

NT = (((1,), (1,)), ((), ()))
T_N = (((0,), (0,)), ((), ()))


def _params(sem=None):
    return pltpu.CompilerParams(dimension_semantics=sem, vmem_limit_bytes=VMEM_LIMIT_V7X)


def _dot(a, b, dims=None):
    if dims is None:
        return jnp.dot(a, b, preferred_element_type=F32)
    return lax.dot_general(a, b, dims, preferred_element_type=F32)


def _rope_tables():
    inv_freq = ROPE_THETA ** (-(2.0 * jnp.arange(ROPE_HALF, dtype=F32)) / ROPE_DIM)
    ang = jnp.arange(S, dtype=jnp.int32).astype(F32)[:, None] * inv_freq[None, :]
    cos, sin = jnp.cos(ang), jnp.sin(ang)
    rest = DH - ROPE_DIM
    c = jnp.concatenate([cos, cos, jnp.ones((S, rest), F32)], axis=1)
    sn = jnp.concatenate([-sin, sin, jnp.zeros((S, rest), F32)], axis=1)
    return c, sn


def _rope_partner(t):
    lane = lax.broadcasted_iota(jnp.int32, t.shape, 1)
    return jnp.where(lane < ROPE_HALF, pltpu.roll(t, DH - ROPE_HALF, 1), pltpu.roll(t, ROPE_HALF, 1))


def _rope(t, c, sn):
    return t * c + _rope_partner(t) * sn


def _rope_t(g, c, sn):
    return g * c - _rope_partner(g) * sn


def _prep_x(x, rider):
    tm = 512

    def body(x_ref, xb_ref, xt_ref):
        xv = x_ref[...]
        xb_ref[...] = xv.astype(BF16)
        xt_ref[...] = xv.T.astype(BF16)

    return _call_carrying(
        body, rider, (S // tm,), [pl.BlockSpec((tm, D), lambda i: (i, 0))],
        [pl.BlockSpec((tm, D), lambda i: (i, 0)), pl.BlockSpec((D, tm), lambda i: (0, i))],
        [SDS((S, D), BF16), SDS((D, S), BF16)], [], [x], "prep_x")


def _residues(slab_ref, idx, d, r, n):
    return slab_ref[(*idx, pl.ds(r, n, stride=d), slice(None))]


def _proj_phase(xb, wg, s, rope, prev, rider):
    n_heads = TN // DH
    n16 = HW // TN
    j16 = n16 // W_PHASES
    n_rope = 2 * DA // TN
    once = pl.Buffered(1)

    def body(x_ref, w_ref, c_ref, sn_ref, *rest):
        h16_ref, h4_ref, h16r_ref, h32_ref, slab_ref, res_ref = rest[-6:]
        j = pl.program_id(0)
        d4, d16 = DILATIONS[1], DILATIONS[2]
        acc = _dot(x_ref[...], w_ref[...])

        def heads(with_rope):
            for hh in range(n_heads):
                hs = slice(hh * DH, (hh + 1) * DH)
                t = _rope(acc[:, hs], c_ref[...], sn_ref[...]) if with_rope else acc[:, hs]
                h16_ref[:, hs] = t.astype(BF16)
                slab_ref[...] = t
                for q in range(d4):
                    t4 = _residues(slab_ref, (), d4, q, S // d4)
                    h4_ref[q, :, hs] = t4.astype(BF16)
                    res_ref[...] = t4
                    for m in range(d16 // d4):
                        h16r_ref[d4 * m + q, :, hs] = _residues(res_ref, (), d4, m, S // d16).astype(BF16)

        pl.when(W_PHASES * j + s < n_rope)(lambda: heads(True))
        pl.when((W_PHASES * j + s >= n_rope) & (j < j16))(lambda: heads(False))

        @pl.when(j >= j16)
        def _():
            h32_ref[...] = acc

    col16 = lambda j: W_PHASES * jnp.minimum(j, j16 - 1) + s
    col32 = lambda j: W_PHASES * jnp.maximum(j, j16) + s - n16
    hbm = pl.BlockSpec(memory_space=pl.ANY)
    in_specs = [pl.BlockSpec((S, D), lambda j: (0, 0), pipeline_mode=once),
                pl.BlockSpec((None, D, TN), lambda j: (j, 0, 0))]
    in_specs += [pl.BlockSpec((S, DH), lambda j: (0, 0), pipeline_mode=once)] * 2
    args = [xb, wg] + list(rope)
    aliases = {}
    if prev is not None:
        aliases = {len(args) + n: n for n in range(4)}
        in_specs, args = in_specs + [hbm] * 4, args + list(prev)
    out_specs = [pl.BlockSpec((S, TN), lambda j: (0, col16(j)))]
    out_specs += [pl.BlockSpec((d, S // d, TN), lambda j: (0, 0, col16(j))) for d in DILATIONS[1:]]
    out_specs += [pl.BlockSpec((S, TN), lambda j: (0, col32(j)))]
    out_shape = [SDS((S, HW), BF16)] + [SDS((d, S // d, HW), BF16) for d in DILATIONS[1:]] + [SDS((S, HW), F32)]
    return _call_carrying(
        body, rider, (N_DEV,), in_specs, out_specs, out_shape,
        [pltpu.VMEM((S, DH), F32), pltpu.VMEM((S // DILATIONS[1], DH), F32)], args,
        f"proj_phase{s}", aliases)


BQ = 2 * KB
LANES = 128


def _to_lane(acc, col, h):
    lane = lax.broadcasted_iota(jnp.int32, acc.shape, 1)
    return jnp.where(lane == h, col, acc)


def _attn_fwd(h16, name):
    d, n_sub, _ = h16.shape
    n_i = n_sub // BQ
    kw = KB + BQ

    def prev(i):
        return jnp.maximum(2 * i - 1, 0)

    def body(q_ref, kc_ref, kp_ref, vc_ref, vp_ref, o_ref, l_ref, kw_ref, vw_ref):
        i = pl.program_id(1)
        kw_ref[0:KB, :] = kp_ref[...]
        kw_ref[KB:kw, :] = kc_ref[...]
        vw_ref[0:KB, :] = vp_ref[...]
        vw_ref[KB:kw, :] = vc_ref[...]
        a = lax.broadcasted_iota(jnp.int32, (KB, 2 * KB), 0)
        b = lax.broadcasted_iota(jnp.int32, (KB, 2 * KB), 1)
        band = (b >= a) & (b <= a + KB)
        first_key = jnp.where(i == 0, KB, 0)
        masks = (band & (b >= first_key), band)
        for half in range(2):
            rs = slice(half * KB, (half + 1) * KB)
            ks = slice(half * KB, (half + 2) * KB)
            lse = jnp.zeros((KB, LANES), F32)
            for h in range(N_HEADS):
                hs = slice(h * DH, (h + 1) * DH)
                s = jnp.where(masks[half], _dot(q_ref[rs, hs], kw_ref[ks, hs], NT) * SCALE, NEG)
                m = jnp.max(s, axis=1, keepdims=True)
                p = jnp.exp(s - m)
                den = jnp.sum(p, axis=1, keepdims=True)
                o_ref[rs, hs] = _dot(p.astype(BF16), vw_ref[ks, hs]) / den
                lse = _to_lane(lse, m + jnp.log(den), h)
            l_ref[rs, :] = lse

    return pl.pallas_call(
        body, name=name, grid=(d, n_i),
        in_specs=[pl.BlockSpec((None, BQ, DA), lambda r, i: (r, i, 0)),
                  pl.BlockSpec((None, BQ, DA), lambda r, i: (r, i, 1)),
                  pl.BlockSpec((None, KB, DA), lambda r, i: (r, prev(i), 1)),
                  pl.BlockSpec((None, BQ, DA), lambda r, i: (r, i, 2)),
                  pl.BlockSpec((None, KB, DA), lambda r, i: (r, prev(i), 2))],
        out_specs=[pl.BlockSpec((None, BQ, DA), lambda r, i: (r, i, 0)),
                   pl.BlockSpec((None, BQ, LANES), lambda r, i: (r, i, 0))],
        out_shape=[SDS((d, n_sub, DA), F32), SDS((d, n_sub, LANES), F32)],
        scratch_shapes=[pltpu.VMEM((kw, DA), BF16), pltpu.VMEM((kw, DA), BF16)],
        compiler_params=_params(("arbitrary", "arbitrary")),
    )(h16, h16, h16, h16, h16)


def _pooled(ext_ref, g, rows, tm):
    w = POOL_WINDOWS[g]
    cs = slice(g * GC, (g + 1) * GC)
    cur = ext_ref[HALO:HALO + tm, cs]
    win = cur
    for j in range(1, w):
        win = win + ext_ref[HALO - j:HALO - j + tm, cs]
    cnt = jnp.minimum(rows + 1, w).astype(F32)
    return win / cnt - cur, cnt


def _fill_ext(ext_ref, u_ref, uh_ref, blk, tm):
    @pl.when(blk == 0)
    def _():
        ext_ref[0:HALO, :] = jnp.zeros((HALO, DP), F32)

    @pl.when(blk > 0)
    def _():
        ext_ref[0:HALO, :] = uh_ref[...]

    ext_ref[HALO:HALO + tm, :] = u_ref[...]


def _residue_specs(tm, width):
    return [pl.BlockSpec((d, tm // d, width), lambda i: (0, i, 0)) for d in DILATIONS[1:]]


def _mix(o1, l1, o4, l4, o16, l16, h32, wp, scale):
    tm = 256
    n_slab = N_HEADS + 1

    def body(o1r, l1r, o4r, l4r, o16r, l16r, u_ref, uh_ref, ga_ref, gp_ref, wp_ref, sc_ref,
             y_ref, yt_ref, attn_ref, lse_ref, lse4_ref, lse16_ref, ext_ref, ys_ref, nat_ref, ls_ref):
        i = pl.program_id(0)
        for n, (d, o_r, l_r) in enumerate(((DILATIONS[1], o4r, l4r), (DILATIONS[2], o16r, l16r))):
            for r in range(d):
                rows = pl.ds(r, tm // d, stride=d)
                for h in range(N_HEADS):
                    nat_ref[n, h, rows, :] = o_r[r, :, h * DH:(h + 1) * DH]
                nat_ref[n, N_HEADS, rows, :] = l_r[r]
        la, lb, lc = l1r[...], nat_ref[0, N_HEADS], nat_ref[1, N_HEADS]
        mx = jnp.maximum(jnp.maximum(la, lb), lc)
        ea, eb, ec = jnp.exp(la - mx), jnp.exp(lb - mx), jnp.exp(lc - mx)
        z = ea + eb + ec
        wa, wb, wc = ea / z, eb / z, ec / z
        lse = mx + jnp.log(z)
        lse_ref[...] = lse
        ls_ref[...] = lse
        for d, out in ((DILATIONS[1], lse4_ref), (DILATIONS[2], lse16_ref)):
            for r in range(d):
                out[r] = ls_ref[pl.ds(r, tm // d, stride=d), :]
        for h in range(N_HEADS):
            hs = slice(h * DH, (h + 1) * DH)
            hc = slice(h, h + 1)
            attn = wa[:, hc] * o1r[:, hs] + wb[:, hc] * nat_ref[0, h] + wc[:, hc] * nat_ref[1, h]
            attn_ref[:, hs] = attn
            gt = ga_ref[:, hs]
            ys_ref[:, hs] = attn * (gt * jax.nn.sigmoid(gt))

        _fill_ext(ext_ref, u_ref, uh_ref, i, tm)
        rows = i * tm + lax.broadcasted_iota(jnp.int32, (tm, 1), 0)
        for g in range(N_GROUPS):
            cs = slice(g * GC, (g + 1) * GC)
            gs = slice(DA + g * GC, DA + (g + 1) * GC)
            pooled, _ = _pooled(ext_ref, g, rows, tm)
            po = _dot(pooled.astype(BF16), wp_ref[g]) * sc_ref[:, cs]
            gt = gp_ref[:, cs]
            ys_ref[:, gs] = po * (gt * jax.nn.sigmoid(gt))
        yv = ys_ref[...]
        y_ref[...] = yv.astype(BF16)
        yt_ref[...] = yv.T.astype(BF16)

    row = lambda i: (i, 0)
    blk = pl.BlockSpec((tm, DA), row)
    lanes = pl.BlockSpec((tm, LANES), row)
    o_res, l_res = _residue_specs(tm, DA), _residue_specs(tm, LANES)
    return pl.pallas_call(
        body, name="mix", grid=(S // tm,),
        in_specs=[blk, lanes, o_res[0], l_res[0], o_res[1], l_res[1],
                  pl.BlockSpec((tm, DP), row),
                  pl.BlockSpec((HALO, DP), lambda i: (jnp.maximum(i * (tm // HALO) - 1, 0), 0)),
                  pl.BlockSpec((tm, DA), lambda i: (i, 1)), pl.BlockSpec((tm, DP), lambda i: (i, 2)),
                  pl.BlockSpec((N_GROUPS, GC, GC), lambda i: (0, 0, 0)),
                  pl.BlockSpec((1, DP), lambda i: (0, 0))],
        out_specs=[pl.BlockSpec((tm, D), row), pl.BlockSpec((D, tm), lambda i: (0, i)), blk, lanes] + l_res,
        out_shape=[SDS((S, D), BF16), SDS((D, S), BF16), SDS((S, DA), F32), SDS((S, LANES), F32)]
        + [SDS((d, S // d, LANES), F32) for d in DILATIONS[1:]],
        scratch_shapes=[pltpu.VMEM((HALO + tm, DP), F32), pltpu.VMEM((tm, D), F32),
                        pltpu.VMEM((2, n_slab, tm, DH), F32), pltpu.VMEM((tm, LANES), F32)],
        compiler_params=_params(("arbitrary",)),
    )(o1, l1, o4, l4, o16, l16, h32, h32, h32, h32, wp, scale)


def _outproj_ln(y, wout, x, target, gain, bias):
    tm = 512
    te = 128

    def body(y_ref, w_ref, x_ref, t_ref, g_ref, b_ref, dz_ref, dzb_ref, st_ref, out_ref):
        i = pl.program_id(0)

        @pl.when(i == 0)
        def _():
            st_ref[...] = jnp.zeros((8, D), F32)

        out_ref[...] = _dot(y_ref[...], w_ref[...])
        gn = g_ref[...]
        for e in range(tm // te):
            rs = slice(e * te, (e + 1) * te)
            z = ALPHA * x_ref[rs, :] + out_ref[rs, :]
            mu = jnp.mean(z, axis=1, keepdims=True)
            zc = z - mu
            var = jnp.mean(zc * zc, axis=1, keepdims=True)
            rstd = lax.rsqrt(var + LN_EPS)
            xhat = zc * rstd
            diff = xhat * gn + b_ref[...] - t_ref[rs, :]
            dyln = diff / D
            st_ref[0:1, :] += jnp.sum(dyln * xhat, axis=0, keepdims=True)
            st_ref[1:2, :] += jnp.sum(dyln, axis=0, keepdims=True)
            row_loss = jnp.sum(diff * diff, axis=1, keepdims=True) / D
            st_ref[2:3, :] += jnp.broadcast_to(0.5 * jnp.sum(row_loss, axis=0, keepdims=True), (1, D))
            dxh = dyln * gn
            m1 = jnp.mean(dxh, axis=1, keepdims=True)
            m2 = jnp.mean(dxh * xhat, axis=1, keepdims=True)
            dz = rstd * (dxh - m1 - xhat * m2)
            dz_ref[rs, :] = dz
            dzb_ref[rs, :] = dz.astype(BF16)

    row = lambda i: (i, 0)
    const = lambda i: (0, 0)
    return pl.pallas_call(
        body, name="outproj_ln", grid=(S // tm,),
        in_specs=[pl.BlockSpec((tm, D), row),
                  pl.BlockSpec((D, D), const, pipeline_mode=pl.Buffered(1)),
                  pl.BlockSpec((tm, D), row), pl.BlockSpec((tm, D), row),
                  pl.BlockSpec((1, D), const), pl.BlockSpec((1, D), const)],
        out_specs=[pl.BlockSpec((tm, D), row), pl.BlockSpec((tm, D), row), pl.BlockSpec((8, D), const)],
        out_shape=[SDS((S, D), F32), SDS((S, D), BF16), SDS((8, D), F32)],
        scratch_shapes=[pltpu.VMEM((tm, D), F32)],
        compiler_params=_params(("arbitrary",)),
    )(y, wout, x, target, gain, bias)


def _bwd_mid(dzb, wout, h32, attn, wp, scale, rider=None):
    tm = 256
    n = S // tm

    def body(dz_ref, w_ref, ga_ref, gp_ref, at_ref, u_ref, uh_ref, wp_ref, sc_ref,
             dh_ref, do_ref, do4_ref, do16_ref, dd_ref, dd4_ref, dd16_ref, dwp_ref, dwp16_ref, gps_ref,
             ext_ref, eext_ref, acc_ref, nat_ref, ds_ref):
        i = pl.program_id(0)
        ib = n - 1 - i

        @pl.when(i == 0)
        def _():
            eext_ref[tm:tm + HALO, :] = jnp.zeros((HALO, DP), F32)
            acc_ref[...] = jnp.zeros((N_GROUPS, GC, GC), F32)
            gps_ref[...] = jnp.zeros((8, DP), F32)

        dy = _dot(dz_ref[...], w_ref[...], NT)

        def through_gate(dy_part, gt):
            sg = jax.nn.sigmoid(gt)
            return dy_part * (gt * sg), dy_part * (sg * (1.0 + gt * (1.0 - sg)))

        dat, dyg_a = through_gate(dy[:, 0:DA], ga_ref[...])
        dmix_p, dyg_p = through_gate(dy[:, DA:D], gp_ref[...])

        at = at_ref[...]
        do_ref[...] = dat.astype(BF16)
        dh_ref[:, DP:DP + DA] = (dyg_a * at).astype(BF16)
        prod = dat * at
        dd = jnp.zeros((tm, LANES), F32)
        for h in range(N_HEADS):
            hs = slice(h * DH, (h + 1) * DH)
            dd = _to_lane(dd, jnp.sum(prod[:, hs], axis=1, keepdims=True), h)
            nat_ref[h] = dat[:, hs]
        dd_ref[...] = dd
        ds_ref[...] = dd
        for d, do_out, dd_out in ((DILATIONS[1], do4_ref, dd4_ref), (DILATIONS[2], do16_ref, dd16_ref)):
            for r in range(d):
                dd_out[r] = _residues(ds_ref, (), d, r, tm // d)
                for h in range(N_HEADS):
                    do_out[r, :, h * DH:(h + 1) * DH] = _residues(nat_ref, (h,), d, r, tm // d).astype(BF16)

        _fill_ext(ext_ref, u_ref, uh_ref, ib, tm)
        rows = ib * tm + lax.broadcasted_iota(jnp.int32, (tm, 1), 0)
        for g in range(N_GROUPS):
            w = POOL_WINDOWS[g]
            cs = slice(g * GC, (g + 1) * GC)
            pooled, cnt = _pooled(ext_ref, g, rows, tm)
            pre = _dot(pooled.astype(BF16), wp_ref[g])
            sc = sc_ref[:, cs]
            dpo = dmix_p[:, cs]
            gps_ref[0:1, cs] += jnp.sum(dpo * pre, axis=0, keepdims=True)
            dh_ref[:, DP + DA + g * GC:DP + DA + (g + 1) * GC] = (dyg_p[:, cs] * (pre * sc)).astype(BF16)
            dpre = (dpo * sc).astype(BF16)
            acc_ref[g] += _dot(pooled.T.astype(BF16), dpre)
            dpooled = _dot(dpre, wp_ref[g], NT)
            eext_ref[0:tm, cs] = dpooled / cnt
            du = eext_ref[0:tm, cs]
            for j in range(1, w):
                du = du + eext_ref[j:j + tm, cs]
            dh_ref[:, cs] = (du - dpooled).astype(BF16)
        eext_ref[tm:tm + HALO, :] = eext_ref[0:HALO, :]

        @pl.when(i == n - 1)
        def _():
            for j in range(N_DEV):
                for g in range(N_GROUPS):
                    blk = acc_ref[g, j * PB:(j + 1) * PB, :]
                    dwp_ref[j, g * PB:(g + 1) * PB, :] = blk
                    dwp16_ref[j, g * PB:(g + 1) * PB, :] = blk.astype(BF16)

    rev = lambda i: (n - 1 - i, 0)
    const = lambda i: (0, 0)
    res = lambda width: [pl.BlockSpec((d, tm // d, width), lambda i: (0, n - 1 - i, 0)) for d in DILATIONS[1:]]
    pool_blocks = pl.BlockSpec((N_DEV, N_GROUPS * PB, GC), lambda i: (0, 0, 0))
    outs, carried = _call_carrying(
        body, rider, (n,),
        [pl.BlockSpec((tm, D), rev),
         pl.BlockSpec((D, D), const, pipeline_mode=pl.Buffered(1)),
         pl.BlockSpec((tm, DA), lambda i: (n - 1 - i, 1)), pl.BlockSpec((tm, DP), lambda i: (n - 1 - i, 2)),
         pl.BlockSpec((tm, DA), rev), pl.BlockSpec((tm, DP), rev),
         pl.BlockSpec((HALO, DP), lambda i: (jnp.maximum((n - 1 - i) * (tm // HALO) - 1, 0), 0)),
         pl.BlockSpec((N_GROUPS, GC, GC), lambda i: (0, 0, 0)),
         pl.BlockSpec((1, DP), const)],
        [pl.BlockSpec((tm, DP + D), lambda i: (n - 1 - i, 1)), pl.BlockSpec((tm, DA), rev)] + res(DA)
        + [pl.BlockSpec((tm, LANES), rev)] + res(LANES) + [pool_blocks, pool_blocks, pl.BlockSpec((8, DP), const)],
        [SDS((S, D_IN), BF16), SDS((S, DA), BF16)] + [SDS((d, S // d, DA), BF16) for d in DILATIONS[1:]]
        + [SDS((S, LANES), F32)] + [SDS((d, S // d, LANES), F32) for d in DILATIONS[1:]]
        + [SDS((N_DEV, N_GROUPS * PB, GC), F32), SDS((N_DEV, N_GROUPS * PB, GC), BF16), SDS((8, DP), F32)],
        [pltpu.VMEM((HALO + tm, DP), F32), pltpu.VMEM((tm + HALO, DP), F32),
         pltpu.VMEM((N_GROUPS, GC, GC), F32), pltpu.VMEM((N_HEADS, tm, DH), F32), pltpu.VMEM((tm, LANES), F32)],
        [dzb, wout, h32, h32, attn, h32, h32, wp, scale], "bwd_mid")
    return outs, carried


def _attn_bwd(h16, do, lse, dd, rope, name, others=None, dh=None):
    d, n_sub, _ = h16.shape
    n_i = n_sub // BQ
    n_kb = n_sub // KB
    qw = BQ + KB
    final = others is not None
    out_dtype = BF16 if final else F32
    n_cb = 3 * DA // DH

    def nxt(i):
        return jnp.minimum(2 * i + 2, n_kb - 1)

    def body(qc_ref, qn_ref, kc_ref, vc_ref, doc_ref, don_ref, lc_ref, ln_ref, dc_ref, dn_ref,
             c_ref, sn_ref, *rest):
        if final:
            acc4_ref, acc16_ref, _, out_ref, carry_ref, qw_ref, dow_ref, lw_ref, dw_ref, nat_ref = rest
        else:
            out_ref, carry_ref, qw_ref, dow_ref, lw_ref, dw_ref = rest
        i = pl.program_id(1)

        @pl.when(i == 0)
        def _():
            carry_ref[...] = jnp.zeros((KB, DA), F32)

        for win, own, after in ((qw_ref, qc_ref, qn_ref), (dow_ref, doc_ref, don_ref), (lw_ref, lc_ref, ln_ref),
                                (dw_ref, dc_ref, dn_ref)):
            win[0:BQ, :] = own[...]
            win[BQ:qw, :] = after[...]
        if final:
            for n, (dil, acc) in enumerate(((DILATIONS[1], acc4_ref), (DILATIONS[2], acc16_ref))):
                for r in range(dil):
                    for cb in range(n_cb):
                        nat_ref[n, cb, pl.ds(r, BQ // dil, stride=dil), :] = acc[r, :, cb * DH:(cb + 1) * DH]

        a = lax.broadcasted_iota(jnp.int32, (qw, BQ), 0)
        b = lax.broadcasted_iota(jnp.int32, (qw, BQ), 1)
        n_q = jnp.where(i == n_i - 1, BQ, qw)
        mask = (b <= a) & (a <= b + KB) & (a < n_q)
        tabs = (c_ref[...], sn_ref[...])
        for h in range(N_HEADS):
            hs = slice(h * DH, (h + 1) * DH)
            hc = slice(h, h + 1)
            q, k, vv, dob = qw_ref[:, hs], kc_ref[:, hs], vc_ref[:, hs], dow_ref[:, hs]
            s = _dot(q, k, NT) * SCALE
            p = jnp.exp(jnp.where(mask, s - lw_ref[:, hc], NEG))
            dp = _dot(dob, vv, NT)
            ds = (p * (dp - dw_ref[:, hc]) * SCALE).astype(BF16)
            dq = _dot(ds, k)
            dk = _dot(ds, q, T_N)
            dv = _dot(p.astype(BF16), dob, T_N)
            dq_lo = dq[0:KB] + carry_ref[:, hs]
            carry_ref[:, hs] = dq[BQ:qw]
            dq_own = _rope_t(jnp.concatenate([dq_lo, dq[KB:BQ]], axis=0), *tabs)
            for base, gv in ((0, dq_own), (DA, _rope_t(dk, *tabs)), (2 * DA, dv)):
                if final:
                    cb = base // DH + h
                    gv = gv + nat_ref[0, cb] + nat_ref[1, cb]
                out_ref[:, base + h * DH:base + (h + 1) * DH] = gv.astype(out_dtype)

    def cur(rows, width, col=0):
        return pl.BlockSpec((None, rows, width), lambda r, i: (r, i, col))

    def nx(width, col=0):
        return pl.BlockSpec((None, KB, width), lambda r, i: (r, nxt(i), col))

    in_specs = [cur(BQ, DA), nx(DA), cur(BQ, DA, 1), cur(BQ, DA, 2), cur(BQ, DA), nx(DA),
                cur(BQ, LANES), nx(LANES), cur(BQ, LANES), nx(LANES)] + [cur(BQ, DH)] * 2
    args = [h16, h16, h16, h16, do, do, lse, lse, dd, dd] + list(rope)
    scratch = [pltpu.VMEM((KB, DA), F32), pltpu.VMEM((qw, DA), BF16), pltpu.VMEM((qw, DA), BF16),
               pltpu.VMEM((qw, LANES), F32), pltpu.VMEM((qw, LANES), F32)]
    if final:
        assert d == 1
        in_specs += [pl.BlockSpec((dil, BQ // dil, 3 * DA), lambda r, i: (0, i, 0)) for dil in DILATIONS[1:]]
        in_specs.append(pl.BlockSpec(memory_space=pl.ANY))
        args += list(others) + [dh[None]]
        scratch.append(pltpu.VMEM((2, n_cb, BQ, DH), F32))
    return pl.pallas_call(
        body, name=name, grid=(d, n_i),
        in_specs=in_specs,
        out_specs=cur(BQ, 3 * DA),
        out_shape=SDS((d, n_sub, D_IN if final else 3 * DA), out_dtype),
        input_output_aliases={len(args) - 1: 0} if final else {},
        scratch_shapes=scratch,
        compiler_params=_params(("arbitrary", "arbitrary")),
    )(*args)


def _grad_w_out(yt, dzb):
    tn = 256

    def body(yt_ref, dz_ref, o_ref, o16_ref):
        acc = _dot(yt_ref[...], dz_ref[...])
        o_ref[...] = acc
        o16_ref[...] = acc.astype(BF16)

    cols = pl.BlockSpec((D, tn), lambda c: (0, c))
    o, o16 = pl.pallas_call(
        body, name="grad_w_out", grid=(D // tn,),
        in_specs=[pl.BlockSpec((D, S), lambda c: (0, 0), pipeline_mode=pl.Buffered(1)),
                  pl.BlockSpec((S, tn), lambda c: (0, c))],
        out_specs=[cols, cols],
        out_shape=[SDS((D, D), F32), SDS((D, D), BF16)],
        compiler_params=_params(("arbitrary",)),
    )(yt, dzb)
    return o.reshape(N_DEV, OB, D), o16.reshape(N_DEV, OB, D)


def _grad_w_in(xt, dh):
    tn = 256
    per_blk = WB // tn

    def body(xt_ref, dh_ref, o_ref, o16_ref):
        acc = _dot(xt_ref[...], dh_ref[...])
        o_ref[...] = acc
        o16_ref[...] = acc.astype(BF16)

    blk = pl.BlockSpec((None, D, tn), lambda c: (c // per_blk, 0, c % per_blk))
    return pl.pallas_call(
        body, name="grad_w_in", grid=(D_IN // tn,),
        in_specs=[pl.BlockSpec((D, S), lambda c: (0, 0), pipeline_mode=pl.Buffered(1)),
                  pl.BlockSpec((S, tn), lambda c: (0, c))],
        out_specs=[blk, blk],
        out_shape=[SDS((N_DEV, D, WB), F32), SDS((N_DEV, D, WB), BF16)],
        compiler_params=_params(("arbitrary",)),
    )(xt, dh)


def _grad_x(dz, dh, wgs, rider=None):
    tm = 1024

    def body(dz_ref, dh_ref, *rest):
        o_ref = rest[-1]

        @pl.when(pl.program_id(1) == 0)
        def _():
            o_ref[...] = ALPHA * dz_ref[...]

        acc = _dot(dh_ref[:, 0:TN], rest[0][...], NT)
        for s in range(1, W_PHASES):
            acc = acc + _dot(dh_ref[:, s * TN:(s + 1) * TN], rest[s][...], NT)
        o_ref[...] += acc

    outs, carried = _call_carrying(
        body, rider, (S // tm, N_DEV),
        [pl.BlockSpec((tm, D), lambda i, j: (i, 0)), pl.BlockSpec((tm, WB), lambda i, j: (i, j))]
        + [pl.BlockSpec((None, D, TN), lambda i, j: (j, 0, 0))] * W_PHASES,
        [pl.BlockSpec((tm, D), lambda i, j: (i, 0))], [SDS((S, D), F32)], [], [dz, dh] + list(wgs), "grad_x")
    return outs[0], carried


def _place():
    x, y, c = lax.axis_index("x"), lax.axis_index("y"), lax.axis_index("c")
    chips = [(x, y), (1 - x, y), (x, 1 - y), (1 - x, 1 - y)]
    return x, y, c, chips


def _blk(x, y, c):
    return 4 * x + 2 * y + c


def _adamw(w, g, m, v):
    m = ADAM_B1 * m + (1.0 - ADAM_B1) * g
    v = ADAM_B2 * v + (1.0 - ADAM_B2) * (g * g)
    m_hat = m / (1.0 - ADAM_B1 ** ADAM_STEP)
    v_hat = v / (1.0 - ADAM_B2 ** ADAM_STEP)
    delta = -ADAM_LR * (m_hat / (jnp.sqrt(v_hat) + ADAM_EPS) + ADAM_WD * w)
    return delta, m, v


class _Rider:
    def __init__(self, args, in_specs, out_shape, out_specs, scratch, start, finish):
        self.args, self.in_specs, self.out_shape, self.out_specs = args, in_specs, out_shape, out_specs
        self.scratch, self.start, self.finish = scratch, start, finish


def _carry(body, rider, n_in, n_out, first, last):
    if rider is None:
        return body
    r_in, r_out, r_scr = len(rider.args), len(rider.out_shape), len(rider.scratch)

    def carrying(*refs):
        o0 = n_in + r_in
        s0 = o0 + n_out + r_out
        s1 = len(refs) - r_scr
        theirs = (refs[n_in:o0], refs[o0 + n_out:s0], refs[s1:])
        pl.when(first())(lambda: rider.start(*theirs))
        body(*refs[:n_in], *refs[o0:o0 + n_out], *refs[s0:s1])
        pl.when(last())(lambda: rider.finish(*theirs))

    return carrying


def _call_carrying(body, rider, grid, in_specs, out_specs, out_shape, scratch, args, name, aliases=None):
    n_in, n_out = len(in_specs), len(out_specs)
    ids = lambda: [pl.program_id(a) for a in range(len(grid))]
    first = lambda: functools.reduce(jnp.logical_and, [i == 0 for i in ids()])
    last = lambda: functools.reduce(jnp.logical_and, [i == n - 1 for i, n in zip(ids(), grid)])
    if rider is not None:
        in_specs, args = in_specs + rider.in_specs, list(args) + rider.args
        out_specs, out_shape = out_specs + rider.out_specs, out_shape + rider.out_shape
        scratch = scratch + rider.scratch
    outs = pl.pallas_call(
        _carry(body, rider, n_in, n_out, first, last), name=name, grid=grid,
        in_specs=in_specs, out_specs=out_specs, out_shape=out_shape, scratch_shapes=scratch,
        input_output_aliases=aliases or {}, compiler_params=_params(("arbitrary",) * len(grid)),
    )(*args)
    return list(outs[:n_out]), list(outs[n_out:])


def _gather_copy(tensors, send_sems, recv_sems, t, k, block, to, src=None):
    dst = tensors[t][1](_blk(*block))
    return pltpu.make_async_remote_copy(
        src_ref=dst if src is None else src, dst_ref=dst,
        send_sem=send_sems.at[t, k], recv_sem=recv_sems.at[t, k], device_id=to, device_id_type=MESH)


def _gather_start(tensors, send_sems, recv_sems, local_sems):
    x, y, c, chips = _place()
    me, sib = (x, y, c), (x, y, 1 - c)
    for t, (src, dst) in enumerate(tensors):
        pltpu.make_async_copy(src, dst(_blk(*me)), local_sems.at[t]).start()
        _gather_copy(tensors, send_sems, recv_sems, t, 0, me, sib, src).start()
        for j in (1, 2, 3):
            _gather_copy(tensors, send_sems, recv_sems, t, j, me, (*chips[j], c), src).start()


def _gather_finish(tensors, send_sems, recv_sems, local_sems):
    x, y, c, chips = _place()
    me, sib = (x, y, c), (x, y, 1 - c)
    copy = functools.partial(_gather_copy, tensors, send_sems, recv_sems)
    for t in range(len(tensors)):
        for j in (1, 2, 3):
            copy(t, j, (*chips[j], c), me).wait_recv()
            copy(t, 3 + j, (*chips[j], c), sib).start()
    for t, (src, dst) in enumerate(tensors):
        copy(t, 0, sib, me).wait_recv()
        for j in (1, 2, 3):
            copy(t, 3 + j, (*chips[j], 1 - c), me).wait_recv()
        copy(t, 0, me, sib, src).wait_send()
        for j in (1, 2, 3):
            copy(t, j, me, (*chips[j], c), src).wait_send()
            copy(t, 3 + j, (*chips[j], c), sib).wait_send()
        pltpu.make_async_copy(src, dst(_blk(*me)), local_sems.at[t]).wait()


W_PHASES = WB // TN


def _cast_shards(w_in_s, w_out_s, w_pool_s):
    def body(*refs):
        for src, dst in zip(refs[:3], refs[3:]):
            dst[...] = src[...].astype(BF16)

    vmem = pl.BlockSpec(memory_space=pltpu.VMEM)
    return pl.pallas_call(
        body, name="cast_shards", in_specs=[vmem] * 3, out_specs=[vmem] * 3,
        out_shape=[SDS(w.shape, BF16) for w in (w_in_s, w_out_s, w_pool_s)],
        compiler_params=_params(),
    )(w_in_s, w_out_s, w_pool_s)


def _gather_w_in_rider(shard16, s):
    def tensors(ins, outs):
        return [(ins[0].at[:, pl.ds(s * TN, TN)], lambda b: outs[0].at[b])]

    hbm = pl.BlockSpec(memory_space=pl.ANY)
    return _Rider(
        args=[shard16], in_specs=[hbm], out_shape=[SDS((N_DEV, D, TN), BF16)], out_specs=[hbm],
        scratch=[pltpu.SemaphoreType.DMA((1, 7)), pltpu.SemaphoreType.DMA((1, 7)), pltpu.SemaphoreType.DMA((1,))],
        start=lambda ins, outs, scr: _gather_start(tensors(ins, outs), *scr),
        finish=lambda ins, outs, scr: _gather_finish(tensors(ins, outs), *scr))


def _gather_small_rider(w_out16, w_pool16):
    def tensors(ins, outs):
        gout_ref, gpool_ref = outs

        def pool_rows(b):
            return gpool_ref.at[:, pl.ds(pl.multiple_of(b * PB, PB), PB), :]

        return [(ins[0], lambda b: gout_ref.at[b]), (ins[1], pool_rows)]

    hbm = pl.BlockSpec(memory_space=pl.ANY)
    return _Rider(
        args=[w_out16, w_pool16], in_specs=[hbm, hbm],
        out_shape=[SDS((N_DEV, OB, D), BF16), SDS((N_GROUPS, GC, GC), BF16)], out_specs=[hbm, hbm],
        scratch=[pltpu.SemaphoreType.DMA((2, 7)), pltpu.SemaphoreType.DMA((2, 7)), pltpu.SemaphoreType.DMA((2,))],
        start=lambda ins, outs, scr: _gather_start(tensors(ins, outs), *scr),
        finish=lambda ins, outs, scr: _gather_finish(tensors(ins, outs), *scr))


def _block_table():
    x, y, c, chips = _place()
    return jnp.stack([_blk(*chip, c) for chip in chips]).astype(jnp.int32)


def _rs_exchange(name, p16):
    _, r_tot, cols = p16.shape

    def body(p16_ref, r1_ref, send_sems, recv_sems):
        x, y, c, chips = _place()
        copies = [pltpu.make_async_remote_copy(
            src_ref=p16_ref.at[_blk(*chips[k], 1 - c)], dst_ref=r1_ref.at[k],
            send_sem=send_sems.at[k], recv_sem=recv_sems.at[k], device_id=(x, y, 1 - c), device_id_type=MESH)
            for k in range(4)]
        for cp in copies:
            cp.start()
        for cp in copies:
            cp.wait()

    hbm = pl.BlockSpec(memory_space=pl.ANY)
    return pl.pallas_call(
        body, name=name, in_specs=[hbm], out_specs=hbm, out_shape=SDS((4, r_tot, cols), BF16),
        scratch_shapes=[pltpu.SemaphoreType.DMA((4,)), pltpu.SemaphoreType.DMA((4,))],
        compiler_params=_params(),
    )(p16)


def _chip_sums(name, table, p32, r1, rc):
    _, r_tot, cols = p32.shape

    def body(tbl_ref, p_ref, r_ref, o_ref):
        o_ref[...] = (p_ref[...] + r_ref[...].astype(F32)).astype(BF16)

    return pl.pallas_call(
        body, name=name,
        grid_spec=pltpu.PrefetchScalarGridSpec(
            num_scalar_prefetch=1, grid=(3, r_tot // rc),
            in_specs=[pl.BlockSpec((None, rc, cols), lambda k, ch, tbl: (tbl[k + 1], ch, 0)),
                      pl.BlockSpec((None, rc, cols), lambda k, ch, tbl: (k + 1, ch, 0))],
            out_specs=pl.BlockSpec((None, rc, cols), lambda k, ch, tbl: (k, ch, 0))),
        out_shape=SDS((3, r_tot, cols), BF16),
        compiler_params=_params(("arbitrary", "arbitrary")),
    )(table, p32, r1)


def _stage2_rider(sums, stats=None):
    n_t = len(sums)

    def copies(ins, outs, scr):
        x, y, c, chips = _place()
        out = []
        for t in range(n_t):
            for k in (1, 2, 3):
                out.append(pltpu.make_async_remote_copy(
                    src_ref=ins[t].at[k - 1], dst_ref=outs[t].at[k - 1],
                    send_sem=scr[0].at[t, k - 1], recv_sem=scr[1].at[t, k - 1],
                    device_id=(*chips[k], c), device_id_type=MESH))
        if stats is not None:
            for k in range(1, N_DEV):
                peer = (x ^ ((k >> 2) & 1), y ^ ((k >> 1) & 1), c ^ (k & 1))
                out.append(pltpu.make_async_remote_copy(
                    src_ref=scr[4], dst_ref=outs[n_t].at[_blk(x, y, c)],
                    send_sem=scr[2].at[k - 1], recv_sem=scr[3].at[k - 1], device_id=peer, device_id_type=MESH))
        return out

    def own_rows(outs, scr):
        x, y, c, _ = _place()
        return pltpu.make_async_copy(scr[4], outs[n_t].at[_blk(x, y, c)], scr[5])

    def start(ins, outs, scr):
        if stats is not None:
            scr[4][...] = ins[n_t][...]
            scr[4][3:4, 0:DP] = ins[n_t + 1][0:1, :]
            own_rows(outs, scr).start()
        for cp in copies(ins, outs, scr):
            cp.start()

    def finish(ins, outs, scr):
        for cp in copies(ins, outs, scr):
            cp.wait()
        if stats is not None:
            own_rows(outs, scr).wait()

    vmem = pl.BlockSpec(memory_space=pltpu.VMEM)
    hbm = pl.BlockSpec(memory_space=pl.ANY)
    scratch = [pltpu.SemaphoreType.DMA((n_t, 3)), pltpu.SemaphoreType.DMA((n_t, 3))]
    args, in_specs = list(sums), [hbm] * n_t
    out_shape, out_specs = [SDS(s.shape, BF16) for s in sums], [hbm] * n_t
    if stats is not None:
        scratch += [pltpu.SemaphoreType.DMA((N_DEV - 1,)), pltpu.SemaphoreType.DMA((N_DEV - 1,)),
                    pltpu.VMEM((8, D), F32), pltpu.SemaphoreType.DMA(())]
        args, in_specs = args + list(stats), in_specs + [vmem, vmem]
        out_shape, out_specs = out_shape + [SDS((N_DEV, 8, D), F32)], out_specs + [hbm]
    return _Rider(args, in_specs, out_shape, out_specs, scratch, start, finish)


def _adamw_shard(name, table, p32, r1, r2, w, m, v, rc):
    _, r_tot, cols = p32.shape

    def body(tbl_ref, p_ref, r1_ref, r2_ref, w_ref, m_ref, v_ref, g_ref, d_ref, nm_ref, nv_ref):
        g = p_ref[...] + r1_ref[...].astype(F32)
        for k in range(3):
            g = g + r2_ref[k].astype(F32)
        delta, nm, nv = _adamw(w_ref[...], g, m_ref[...], v_ref[...])
        g_ref[...] = g
        d_ref[...] = delta
        nm_ref[...] = nm
        nv_ref[...] = nv

    rows = pl.BlockSpec((rc, cols), lambda ch, tbl: (ch, 0))
    shard = SDS((r_tot, cols), F32)
    return pl.pallas_call(
        body, name=name,
        grid_spec=pltpu.PrefetchScalarGridSpec(
            num_scalar_prefetch=1, grid=(r_tot // rc,),
            in_specs=[pl.BlockSpec((None, rc, cols), lambda ch, tbl: (tbl[0], ch, 0)),
                      pl.BlockSpec((None, rc, cols), lambda ch, tbl: (0, ch, 0)),
                      pl.BlockSpec((3, rc, cols), lambda ch, tbl: (0, ch, 0)), rows, rows, rows],
            out_specs=[rows, rows, rows, rows]),
        out_shape=[shard, shard, shard, shard],
        compiler_params=_params(("arbitrary",)),
    )(table, p32, r1, r2, w, m, v)


def _replicated_adamw(gathered, gain, bias, scale, m_gain, m_bias, m_scale, v_gain, v_bias, v_scale):
    def body(all_ref, g_ref, b_ref, s_ref, mg_ref, mb_ref, ms_ref, vg_ref, vb_ref, vs_ref,
             tot_ref, dl_ref, nm_ref, nv_ref):
        tot = all_ref[0]
        for b in range(1, N_DEV):
            tot = tot + all_ref[b]
        tot_ref[...] = tot
        dl_ref[...] = jnp.zeros((8, D), F32)
        nm_ref[...] = jnp.zeros((8, D), F32)
        nv_ref[...] = jnp.zeros((8, D), F32)
        for row, width, w_r, m_r, v_r in ((0, D, g_ref, mg_ref, vg_ref), (1, D, b_ref, mb_ref, vb_ref),
                                          (3, DP, s_ref, ms_ref, vs_ref)):
            delta, nm, nv = _adamw(w_r[...], tot[row:row + 1, 0:width], m_r[...], v_r[...])
            dl_ref[row:row + 1, 0:width] = delta
            nm_ref[row:row + 1, 0:width] = nm
            nv_ref[row:row + 1, 0:width] = nv

    vmem = pl.BlockSpec(memory_space=pltpu.VMEM)
    rows = SDS((8, D), F32)
    return pl.pallas_call(
        body, name="replicated_adamw",
        in_specs=[vmem] * 10, out_specs=[vmem] * 4, out_shape=[rows, rows, rows, rows],
        compiler_params=_params(),
    )(gathered, gain, bias, scale, m_gain, m_bias, m_scale, v_gain, v_bias, v_scale)


def kernel(x, w_in, w_pool, pool_scale, w_out, ln_gain, ln_bias, loss_target, m_w_in, m_w_pool, m_pool_scale, m_w_out, m_ln_gain, m_ln_bias, v_w_in, v_w_pool, v_pool_scale, v_w_out, v_ln_gain, v_ln_bias):
    pool_rows = (N_GROUPS * PB, GC)
    x2, target = x[0], loss_target[0]
    rope = lax.optimization_barrier(_rope_tables())

    table = _block_table()
    by_res = lambda t, d: t.reshape(S // d, d, DH).transpose(1, 0, 2)
    ropes = [[by_res(t, d) for t in rope] for d in DILATIONS]

    shard16, w_out16, w_pool16 = _cast_shards(w_in[0], w_out[0], w_pool[0])
    (xb, xt), (wg0,) = _prep_x(x2, _gather_w_in_rider(shard16, 0))
    h, (wg1,) = _proj_phase(xb, wg0, 0, rope, None, _gather_w_in_rider(shard16, 1))
    h, (wg2,) = _proj_phase(xb, wg1, 1, rope, h, _gather_w_in_rider(shard16, 2))
    h, (wg_out, wg_pool) = _proj_phase(xb, wg2, 2, rope, h, _gather_small_rider(w_out16, w_pool16))
    wg_out = wg_out.reshape(D, D)
    h16, h32 = [h[0][None], h[1], h[2]], h[3]
    fwd = [_attn_fwd(h16[n], f"attn_fwd_d{d}") for n, d in enumerate(DILATIONS)]
    y, yt, attn, lse1, lse4, lse16 = _mix(fwd[0][0][0], fwd[0][1][0], *fwd[1], *fwd[2], h32, wg_pool, pool_scale)
    dz, dzb, stats = _outproj_ln(y, wg_out, x2, target, ln_gain, ln_bias)

    dwout, dwout16 = _grad_w_out(yt, dzb)
    r1_out = _rs_exchange("rs_out_exchange", dwout16)
    s2_out = _chip_sums("rs_out_sums", table, dwout, r1_out, OB)
    mid, (r2_out,) = _bwd_mid(dzb, wg_out, h32, attn, wg_pool, pool_scale, rider=_stage2_rider([s2_out]))
    dh, do1, do4, do16, dd1, dd4, dd16, dwp, dwp16, gps = mid
    g_out, d_out, nm_out, nv_out = _adamw_shard(
        "adamw_w_out", table, dwout, r1_out, r2_out, w_out[0], m_w_out[0], v_w_out[0], OB // 2)

    do, lse, dd = [do1[None], do4, do16], [lse1[None], lse4, lse16], [dd1[None], dd4, dd16]
    others = [_attn_bwd(h16[n], do[n], lse[n], dd[n], ropes[n], f"attn_bwd_d{DILATIONS[n]}") for n in (1, 2)]
    dh = _attn_bwd(h16[0], do[0], lse[0], dd[0], ropes[0], "attn_bwd_d1", others=others, dh=dh)[0]

    dwin, dwin16 = _grad_w_in(xt, dh)
    r1_in = _rs_exchange("rs_in_exchange", dwin16)
    s2_in = _chip_sums("rs_in_sums", table, dwin, r1_in, 512)
    r1_pool = _rs_exchange("rs_pool_exchange", dwp16)
    s2_pool = _chip_sums("rs_pool_sums", table, dwp, r1_pool, N_GROUPS * PB)
    grad_x, (r2_in, r2_pool, gathered) = _grad_x(
        dz, dh, (wg0, wg1, wg2), rider=_stage2_rider([s2_in, s2_pool], stats=(stats, gps)))
    g_in, d_in, nm_in, nv_in = _adamw_shard(
        "adamw_w_in", table, dwin, r1_in, r2_in, w_in[0], m_w_in[0], v_w_in[0], 256)
    g_pool, d_pool, nm_pool, nv_pool = _adamw_shard(
        "adamw_w_pool", table, dwp, r1_pool, r2_pool, w_pool[0].reshape(pool_rows), m_w_pool[0].reshape(pool_rows),
        v_w_pool[0].reshape(pool_rows), N_GROUPS * PB)
    tot, dl, nm, nv = _replicated_adamw(gathered, ln_gain, ln_bias, pool_scale, m_ln_gain, m_ln_bias, m_pool_scale,
                                        v_ln_gain, v_ln_bias, v_pool_scale)

    shard4 = lambda t: t.reshape(1, N_GROUPS, PB, GC)
    lead = lambda t: t[None]
    small = lambda t: (t[3:4, 0:DP], t[0:1], t[1:2])
    g_ps, g_gain, g_bias = small(tot)
    d_ps, d_gain, d_bias = small(dl)
    nm_ps, nm_gain, nm_bias = small(nm)
    nv_ps, nv_gain, nv_bias = small(nv)
    return (tot[2, 0], lead(grad_x),
            lead(g_in), shard4(g_pool), g_ps, lead(g_out), g_gain, g_bias,
            lead(d_in), shard4(d_pool), d_ps, lead(d_out), d_gain, d_bias,
            lead(nm_in), shard4(nm_pool), nm_ps, lead(nm_out), nm_gain, nm_bias,
            lead(nv_in), shard4(nv_pool), nv_ps, lead(nv_out), nv_gain, nv_bias)
```

```python
import functools

import jax
import jax.numpy as jnp
from jax import lax
from jax.experimental import pallas as pl
from jax.experimental.pallas import tpu as pltpu

F32 = jnp.float32
BF16 = jnp.bfloat16
SDS = jax.ShapeDtypeStruct
MESH = pl.DeviceIdType.MESH

N_DEV = 8
S = 4096
D = 2048
N_HEADS = 8
DH = 128
DA = N_HEADS * DH
DP = 1024
N_GROUPS = 4
GC = DP // N_GROUPS
POOL_WINDOWS = (2, 4, 8, 16)
HALO = 16
D_IN = 3 * DA + DP + D
WB = D_IN // N_DEV
TN = 256
HW = 3 * DA
OB = D // N_DEV
PB = GC // N_DEV
ROPE_DIM = DH // 4
ROPE_HALF = ROPE_DIM // 2
ROPE_THETA = 500000.0
DILATIONS = (1, 4, 16)
KB = 128
LN_EPS = 1e-5
ALPHA = 2.0 ** 0.25
SCALE = DH ** -0.5
NEG = -1e30
ADAM_LR, ADAM_B1, ADAM_B2, ADAM_EPS, ADAM_WD, ADAM_STEP = 0.001, 0.9, 0.999, 1e-08, 0.01, 10

VMEM_LIMIT_V7X = 61 * 1024 * 1024

NT = (((1,), (1,)), ((), ()))
T_N = (((0,), (0,)), ((), ()))


def _params(sem=None):
    return pltpu.CompilerParams(dimension_semantics=sem, vmem_limit_bytes=VMEM_LIMIT_V7X)


def _dot(a, b, dims=None):
    if dims is None:
        return jnp.dot(a, b, preferred_element_type=F32)
    return lax.dot_general(a, b, dims, preferred_element_type=F32)


def _rope_tables():
    inv_freq = ROPE_THETA ** (-(2.0 * jnp.arange(ROPE_HALF, dtype=F32)) / ROPE_DIM)
    ang = jnp.arange(S, dtype=jnp.int32).astype(F32)[:, None] * inv_freq[None, :]
    cos, sin = jnp.cos(ang), jnp.sin(ang)
    rest = DH - ROPE_DIM
    c = jnp.concatenate([cos, cos, jnp.ones((S, rest), F32)], axis=1)
    sn = jnp.concatenate([-sin, sin, jnp.zeros((S, rest), F32)], axis=1)
    return c, sn


def _rope_partner(t):
    lane = lax.broadcasted_iota(jnp.int32, t.shape, 1)
    return jnp.where(lane < ROPE_HALF, pltpu.roll(t, DH - ROPE_HALF, 1), pltpu.roll(t, ROPE_HALF, 1))


def _rope(t, c, sn):
    return t * c + _rope_partner(t) * sn


def _rope_t(g, c, sn):
    return g * c - _rope_partner(g) * sn


def _prep_x(x, rider):
    tm = 512

    def body(x_ref, xb_ref, xt_ref):
        xv = x_ref[...]
        xb_ref[...] = xv.astype(BF16)
        xt_ref[...] = xv.T.astype(BF16)

    return _call_carrying(
        body, rider, (S // tm,), [pl.BlockSpec((tm, D), lambda i: (i, 0))],
        [pl.BlockSpec((tm, D), lambda i: (i, 0)), pl.BlockSpec((D, tm), lambda i: (0, i))],
        [SDS((S, D), BF16), SDS((D, S), BF16)], [], [x], "prep_x")


def _residues(slab_ref, idx, d, r, n):
    return slab_ref[(*idx, pl.ds(r, n, stride=d), slice(None))]


def _proj_phase(xb, wg, s, rope, prev, rider):
    n_heads = TN // DH
    n16 = HW // TN
    j16 = n16 // W_PHASES
    n_rope = 2 * DA // TN
    once = pl.Buffered(1)

    def body(x_ref, w_ref, c_ref, sn_ref, *rest):
        h16_ref, h4_ref, h16r_ref, h32_ref, slab_ref, res_ref = rest[-6:]
        j = pl.program_id(0)
        d4, d16 = DILATIONS[1], DILATIONS[2]
        acc = _dot(x_ref[...], w_ref[...])

        def heads(with_rope):
            for hh in range(n_heads):
                hs = slice(hh * DH, (hh + 1) * DH)
                t = _rope(acc[:, hs], c_ref[...], sn_ref[...]) if with_rope else acc[:, hs]
                h16_ref[:, hs] = t.astype(BF16)
                slab_ref[...] = t
                for q in range(d4):
                    t4 = _residues(slab_ref, (), d4, q, S // d4)
                    h4_ref[q, :, hs] = t4.astype(BF16)
                    res_ref[...] = t4
                    for m in range(d16 // d4):
                        h16r_ref[d4 * m + q, :, hs] = _residues(res_ref, (), d4, m, S // d16).astype(BF16)

        pl.when(W_PHASES * j + s < n_rope)(lambda: heads(True))
        pl.when((W_PHASES * j + s >= n_rope) & (j < j16))(lambda: heads(False))

        @pl.when(j >= j16)
        def _():
            h32_ref[...] = acc

    col16 = lambda j: W_PHASES * jnp.minimum(j, j16 - 1) + s
    col32 = lambda j: W_PHASES * jnp.maximum(j, j16) + s - n16
    hbm = pl.BlockSpec(memory_space=pl.ANY)
    in_specs = [pl.BlockSpec((S, D), lambda j: (0, 0), pipeline_mode=once),
                pl.BlockSpec((None, D, TN), lambda j: (j, 0, 0))]
    in_specs += [pl.BlockSpec((S, DH), lambda j: (0, 0), pipeline_mode=once)] * 2
    args = [xb, wg] + list(rope)
    aliases = {}
    if prev is not None:
        aliases = {len(args) + n: n for n in range(4)}
        in_specs, args = in_specs + [hbm] * 4, args + list(prev)
    out_specs = [pl.BlockSpec((S, TN), lambda j: (0, col16(j)))]
    out_specs += [pl.BlockSpec((d, S // d, TN), lambda j: (0, 0, col16(j))) for d in DILATIONS[1:]]
    out_specs += [pl.BlockSpec((S, TN), lambda j: (0, col32(j)))]
    out_shape = [SDS((S, HW), BF16)] + [SDS((d, S // d, HW), BF16) for d in DILATIONS[1:]] + [SDS((S, HW), F32)]
    return _call_carrying(
        body, rider, (N_DEV,), in_specs, out_specs, out_shape,
        [pltpu.VMEM((S, DH), F32), pltpu.VMEM((S // DILATIONS[1], DH), F32)], args,
        f"proj_phase{s}", aliases)


BQ = 2 * KB
LANES = 128


def _to_lane(acc, col, h):
    lane = lax.broadcasted_iota(jnp.int32, acc.shape, 1)
    return jnp.where(lane == h, col, acc)


def _attn_fwd(h16, name):
    d, n_sub, _ = h16.shape
    n_i = n_sub // BQ
    kw = KB + BQ

    def prev(i):
        return jnp.maximum(2 * i - 1, 0)

    def body(q_ref, kc_ref, kp_ref, vc_ref, vp_ref, o_ref, l_ref, kw_ref, vw_ref):
        i = pl.program_id(1)
        kw_ref[0:KB, :] = kp_ref[...]
        kw_ref[KB:kw, :] = kc_ref[...]
        vw_ref[0:KB, :] = vp_ref[...]
        vw_ref[KB:kw, :] = vc_ref[...]
        a = lax.broadcasted_iota(jnp.int32, (KB, 2 * KB), 0)
        b = lax.broadcasted_iota(jnp.int32, (KB, 2 * KB), 1)
        band = (b >= a) & (b <= a + KB)
        first_key = jnp.where(i == 0, KB, 0)
        masks = (band & (b >= first_key), band)
        for half in range(2):
            rs = slice(half * KB, (half + 1) * KB)
            ks = slice(half * KB, (half + 2) * KB)
            lse = jnp.zeros((KB, LANES), F32)
            for h in range(N_HEADS):
                hs = slice(h * DH, (h + 1) * DH)
                s = jnp.where(masks[half], _dot(q_ref[rs, hs], kw_ref[ks, hs], NT) * SCALE, NEG)
                m = jnp.max(s, axis=1, keepdims=True)
                p = jnp.exp(s - m)
                den = jnp.sum(p, axis=1, keepdims=True)
                o_ref[rs, hs] = _dot(p.astype(BF16), vw_ref[ks, hs]) / den
                lse = _to_lane(lse, m + jnp.log(den), h)
            l_ref[rs, :] = lse

    return pl.pallas_call(
        body, name=name, grid=(d, n_i),
        in_specs=[pl.BlockSpec((None, BQ, DA), lambda r, i: (r, i, 0)),
                  pl.BlockSpec((None, BQ, DA), lambda r, i: (r, i, 1)),
                  pl.BlockSpec((None, KB, DA), lambda r, i: (r, prev(i), 1)),
                  pl.BlockSpec((None, BQ, DA), lambda r, i: (r, i, 2)),
                  pl.BlockSpec((None, KB, DA), lambda r, i: (r, prev(i), 2))],
        out_specs=[pl.BlockSpec((None, BQ, DA), lambda r, i: (r, i, 0)),
                   pl.BlockSpec((None, BQ, LANES), lambda r, i: (r, i, 0))],
        out_shape=[SDS((d, n_sub, DA), F32), SDS((d, n_sub, LANES), F32)],
        scratch_shapes=[pltpu.VMEM((kw, DA), BF16), pltpu.VMEM((kw, DA), BF16)],
        compiler_params=_params(("arbitrary", "arbitrary")),
    )(h16, h16, h16, h16, h16)


def _pooled(ext_ref, g, rows, tm):
    w = POOL_WINDOWS[g]
    cs = slice(g * GC, (g + 1) * GC)
    cur = ext_ref[HALO:HALO + tm, cs]
    win = cur
    for j in range(1, w):
        win = win + ext_ref[HALO - j:HALO - j + tm, cs]
    cnt = jnp.minimum(rows + 1, w).astype(F32)
    return win / cnt - cur, cnt


def _fill_ext(ext_ref, u_ref, uh_ref, blk, tm):
    @pl.when(blk == 0)
    def _():
        ext_ref[0:HALO, :] = jnp.zeros((HALO, DP), F32)

    @pl.when(blk > 0)
    def _():
        ext_ref[0:HALO, :] = uh_ref[...]

    ext_ref[HALO:HALO + tm, :] = u_ref[...]


def _residue_specs(tm, width):
    return [pl.BlockSpec((d, tm // d, width), lambda i: (0, i, 0)) for d in DILATIONS[1:]]


def _mix(o1, l1, o4, l4, o16, l16, h32, wp, scale):
    tm = 256
    n_slab = N_HEADS + 1

    def body(o1r, l1r, o4r, l4r, o16r, l16r, u_ref, uh_ref, ga_ref, gp_ref, wp_ref, sc_ref,
             y_ref, yt_ref, attn_ref, lse_ref, lse4_ref, lse16_ref, ext_ref, ys_ref, nat_ref, ls_ref):
        i = pl.program_id(0)
        for n, (d, o_r, l_r) in enumerate(((DILATIONS[1], o4r, l4r), (DILATIONS[2], o16r, l16r))):
            for r in range(d):
                rows = pl.ds(r, tm // d, stride=d)
                for h in range(N_HEADS):
                    nat_ref[n, h, rows, :] = o_r[r, :, h * DH:(h + 1) * DH]
                nat_ref[n, N_HEADS, rows, :] = l_r[r]
        la, lb, lc = l1r[...], nat_ref[0, N_HEADS], nat_ref[1, N_HEADS]
        mx = jnp.maximum(jnp.maximum(la, lb), lc)
        ea, eb, ec = jnp.exp(la - mx), jnp.exp(lb - mx), jnp.exp(lc - mx)
        z = ea + eb + ec
        wa, wb, wc = ea / z, eb / z, ec / z
        lse = mx + jnp.log(z)
        lse_ref[...] = lse
        ls_ref[...] = lse
        for d, out in ((DILATIONS[1], lse4_ref), (DILATIONS[2], lse16_ref)):
            for r in range(d):
                out[r] = ls_ref[pl.ds(r, tm // d, stride=d), :]
        for h in range(N_HEADS):
            hs = slice(h * DH, (h + 1) * DH)
            hc = slice(h, h + 1)
            attn = wa[:, hc] * o1r[:, hs] + wb[:, hc] * nat_ref[0, h] + wc[:, hc] * nat_ref[1, h]
            attn_ref[:, hs] = attn
            gt = ga_ref[:, hs]
            ys_ref[:, hs] = attn * (gt * jax.nn.sigmoid(gt))

        _fill_ext(ext_ref, u_ref, uh_ref, i, tm)
        rows = i * tm + lax.broadcasted_iota(jnp.int32, (tm, 1), 0)
        for g in range(N_GROUPS):
            cs = slice(g * GC, (g + 1) * GC)
            gs = slice(DA + g * GC, DA + (g + 1) * GC)
            pooled, _ = _pooled(ext_ref, g, rows, tm)
            po = _dot(pooled.astype(BF16), wp_ref[g]) * sc_ref[:, cs]
            gt = gp_ref[:, cs]
            ys_ref[:, gs] = po * (gt * jax.nn.sigmoid(gt))
        yv = ys_ref[...]
        y_ref[...] = yv.astype(BF16)
        yt_ref[...] = yv.T.astype(BF16)

    row = lambda i: (i, 0)
    blk = pl.BlockSpec((tm, DA), row)
    lanes = pl.BlockSpec((tm, LANES), row)
    o_res, l_res = _residue_specs(tm, DA), _residue_specs(tm, LANES)
    return pl.pallas_call(
        body, name="mix", grid=(S // tm,),
        in_specs=[blk, lanes, o_res[0], l_res[0], o_res[1], l_res[1],
                  pl.BlockSpec((tm, DP), row),
                  pl.BlockSpec((HALO, DP), lambda i: (jnp.maximum(i * (tm // HALO) - 1, 0), 0)),
                  pl.BlockSpec((tm, DA), lambda i: (i, 1)), pl.BlockSpec((tm, DP), lambda i: (i, 2)),
                  pl.BlockSpec((N_GROUPS, GC, GC), lambda i: (0, 0, 0)),
                  pl.BlockSpec((1, DP), lambda i: (0, 0))],
        out_specs=[pl.BlockSpec((tm, D), row), pl.BlockSpec((D, tm), lambda i: (0, i)), blk, lanes] + l_res,
        out_shape=[SDS((S, D), BF16), SDS((D, S), BF16), SDS((S, DA), F32), SDS((S, LANES), F32)]
        + [SDS((d, S // d, LANES), F32) for d in DILATIONS[1:]],
        scratch_shapes=[pltpu.VMEM((HALO + tm, DP), F32), pltpu.VMEM((tm, D), F32),
                        pltpu.VMEM((2, n_slab, tm, DH), F32), pltpu.VMEM((tm, LANES), F32)],
        compiler_params=_params(("arbitrary",)),
    )(o1, l1, o4, l4, o16, l16, h32, h32, h32, h32, wp, scale)


def _outproj_ln(y, wout, x, target, gain, bias):
    tm = 512
    te = 128

    def body(y_ref, w_ref, x_ref, t_ref, g_ref, b_ref, dz_ref, dzb_ref, st_ref, out_ref):
        i = pl.program_id(0)

        @pl.when(i == 0)
        def _():
            st_ref[...] = jnp.zeros((8, D), F32)

        out_ref[...] = _dot(y_ref[...], w_ref[...])
        gn = g_ref[...]
        for e in range(tm // te):
            rs = slice(e * te, (e + 1) * te)
            z = ALPHA * x_ref[rs, :] + out_ref[rs, :]
            mu = jnp.mean(z, axis=1, keepdims=True)
            zc = z - mu
            var = jnp.mean(zc * zc, axis=1, keepdims=True)
            rstd = lax.rsqrt(var + LN_EPS)
            xhat = zc * rstd
            diff = xhat * gn + b_ref[...] - t_ref[rs, :]
            dyln = diff / D
            st_ref[0:1, :] += jnp.sum(dyln * xhat, axis=0, keepdims=True)
            st_ref[1:2, :] += jnp.sum(dyln, axis=0, keepdims=True)
            row_loss = jnp.sum(diff * diff, axis=1, keepdims=True) / D
            st_ref[2:3, :] += jnp.broadcast_to(0.5 * jnp.sum(row_loss, axis=0, keepdims=True), (1, D))
            dxh = dyln * gn
            m1 = jnp.mean(dxh, axis=1, keepdims=True)
            m2 = jnp.mean(dxh * xhat, axis=1, keepdims=True)
            dz = rstd * (dxh - m1 - xhat * m2)
            dz_ref[rs, :] = dz
            dzb_ref[rs, :] = dz.astype(BF16)

    row = lambda i: (i, 0)
    const = lambda i: (0, 0)
    return pl.pallas_call(
        body, name="outproj_ln", grid=(S // tm,),
        in_specs=[pl.BlockSpec((tm, D), row),
                  pl.BlockSpec((D, D), const, pipeline_mode=pl.Buffered(1)),
                  pl.BlockSpec((tm, D), row), pl.BlockSpec((tm, D), row),
                  pl.BlockSpec((1, D), const), pl.BlockSpec((1, D), const)],
        out_specs=[pl.BlockSpec((tm, D), row), pl.BlockSpec((tm, D), row), pl.BlockSpec((8, D), const)],
        out_shape=[SDS((S, D), F32), SDS((S, D), BF16), SDS((8, D), F32)],
        scratch_shapes=[pltpu.VMEM((tm, D), F32)],
        compiler_params=_params(("arbitrary",)),
    )(y, wout, x, target, gain, bias)


def _bwd_mid(dzb, wout, h32, attn, wp, scale, rider=None):
    tm = 256
    n = S // tm

    def body(dz_ref, w_ref, ga_ref, gp_ref, at_ref, u_ref, uh_ref, wp_ref, sc_ref,
             dh_ref, do_ref, do4_ref, do16_ref, dd_ref, dd4_ref, dd16_ref, dwp_ref, dwp16_ref, gps_ref,
             ext_ref, eext_ref, acc_ref, nat_ref, ds_ref):
        i = pl.program_id(0)
        ib = n - 1 - i

        @pl.when(i == 0)
        def _():
            eext_ref[tm:tm + HALO, :] = jnp.zeros((HALO, DP), F32)
            acc_ref[...] = jnp.zeros((N_GROUPS, GC, GC), F32)
            gps_ref[...] = jnp.zeros((8, DP), F32)

        dy = _dot(dz_ref[...], w_ref[...], NT)

        def through_gate(dy_part, gt):
            sg = jax.nn.sigmoid(gt)
            return dy_part * (gt * sg), dy_part * (sg * (1.0 + gt * (1.0 - sg)))

        dat, dyg_a = through_gate(dy[:, 0:DA], ga_ref[...])
        dmix_p, dyg_p = through_gate(dy[:, DA:D], gp_ref[...])

        at = at_ref[...]
        do_ref[...] = dat.astype(BF16)
        dh_ref[:, DP:DP + DA] = (dyg_a * at).astype(BF16)
        prod = dat * at
        dd = jnp.zeros((tm, LANES), F32)
        for h in range(N_HEADS):
            hs = slice(h * DH, (h + 1) * DH)
            dd = _to_lane(dd, jnp.sum(prod[:, hs], axis=1, keepdims=True), h)
            nat_ref[h] = dat[:, hs]
        dd_ref[...] = dd
        ds_ref[...] = dd
        for d, do_out, dd_out in ((DILATIONS[1], do4_ref, dd4_ref), (DILATIONS[2], do16_ref, dd16_ref)):
            for r in range(d):
                dd_out[r] = _residues(ds_ref, (), d, r, tm // d)
                for h in range(N_HEADS):
                    do_out[r, :, h * DH:(h + 1) * DH] = _residues(nat_ref, (h,), d, r, tm // d).astype(BF16)

        _fill_ext(ext_ref, u_ref, uh_ref, ib, tm)
        rows = ib * tm + lax.broadcasted_iota(jnp.int32, (tm, 1), 0)
        for g in range(N_GROUPS):
            w = POOL_WINDOWS[g]
            cs = slice(g * GC, (g + 1) * GC)
            pooled, cnt = _pooled(ext_ref, g, rows, tm)
            pre = _dot(pooled.astype(BF16), wp_ref[g])
            sc = sc_ref[:, cs]
            dpo = dmix_p[:, cs]
            gps_ref[0:1, cs] += jnp.sum(dpo * pre, axis=0, keepdims=True)
            dh_ref[:, DP + DA + g * GC:DP + DA + (g + 1) * GC] = (dyg_p[:, cs] * (pre * sc)).astype(BF16)
            dpre = (dpo * sc).astype(BF16)
            acc_ref[g] += _dot(pooled.T.astype(BF16), dpre)
            dpooled = _dot(dpre, wp_ref[g], NT)
            eext_ref[0:tm, cs] = dpooled / cnt
            du = eext_ref[0:tm, cs]
            for j in range(1, w):
                du = du + eext_ref[j:j + tm, cs]
            dh_ref[:, cs] = (du - dpooled).astype(BF16)
        eext_ref[tm:tm + HALO, :] = eext_ref[0:HALO, :]

        @pl.when(i == n - 1)
        def _():
            for j in range(N_DEV):
                for g in range(N_GROUPS):
                    blk = acc_ref[g, j * PB:(j + 1) * PB, :]
                    dwp_ref[j, g * PB:(g + 1) * PB, :] = blk
                    dwp16_ref[j, g * PB:(g + 1) * PB, :] = blk.astype(BF16)

    rev = lambda i: (n - 1 - i, 0)
    const = lambda i: (0, 0)
    res = lambda width: [pl.BlockSpec((d, tm // d, width), lambda i: (0, n - 1 - i, 0)) for d in DILATIONS[1:]]
    pool_blocks = pl.BlockSpec((N_DEV, N_GROUPS * PB, GC), lambda i: (0, 0, 0))
    outs, carried = _call_carrying(
        body, rider, (n,),
        [pl.BlockSpec((tm, D), rev),
         pl.BlockSpec((D, D), const, pipeline_mode=pl.Buffered(1)),
         pl.BlockSpec((tm, DA), lambda i: (n - 1 - i, 1)), pl.BlockSpec((tm, DP), lambda i: (n - 1 - i, 2)),
         pl.BlockSpec((tm, DA), rev), pl.BlockSpec((tm, DP), rev),
         pl.BlockSpec((HALO, DP), lambda i: (jnp.maximum((n - 1 - i) * (tm // HALO) - 1, 0), 0)),
         pl.BlockSpec((N_GROUPS, GC, GC), lambda i: (0, 0, 0)),
         pl.BlockSpec((1, DP), const)],
        [pl.BlockSpec((tm, DP + D), lambda i: (n - 1 - i, 1)), pl.BlockSpec((tm, DA), rev)] + res(DA)
        + [pl.BlockSpec((tm, LANES), rev)] + res(LANES) + [pool_blocks, pool_blocks, pl.BlockSpec((8, DP), const)],
        [SDS((S, D_IN), BF16), SDS((S, DA), BF16)] + [SDS((d, S // d, DA), BF16) for d in DILATIONS[1:]]
        + [SDS((S, LANES), F32)] + [SDS((d, S // d, LANES), F32) for d in DILATIONS[1:]]
        + [SDS((N_DEV, N_GROUPS * PB, GC), F32), SDS((N_DEV, N_GROUPS * PB, GC), BF16), SDS((8, DP), F32)],
        [pltpu.VMEM((HALO + tm, DP), F32), pltpu.VMEM((tm + HALO, DP), F32),
         pltpu.VMEM((N_GROUPS, GC, GC), F32), pltpu.VMEM((N_HEADS, tm, DH), F32), pltpu.VMEM((tm, LANES), F32)],
        [dzb, wout, h32, h32, attn, h32, h32, wp, scale], "bwd_mid")
    return outs, carried


def _attn_bwd(h16, do, lse, dd, rope, name, others=None, dh=None):
    d, n_sub, _ = h16.shape
    n_i = n_sub // BQ
    n_kb = n_sub // KB
    qw = BQ + KB
    final = others is not None
    out_dtype = BF16 if final else F32
    n_cb = 3 * DA // DH

    def nxt(i):
        return jnp.minimum(2 * i + 2, n_kb - 1)

    def body(qc_ref, qn_ref, kc_ref, vc_ref, doc_ref, don_ref, lc_ref, ln_ref, dc_ref, dn_ref,
             c_ref, sn_ref, *rest):
        if final:
            acc4_ref, acc16_ref, _, out_ref, carry_ref, qw_ref, dow_ref, lw_ref, dw_ref, nat_ref = rest
        else:
            out_ref, carry_ref, qw_ref, dow_ref, lw_ref, dw_ref = rest
        i = pl.program_id(1)

        @pl.when(i == 0)
        def _():
            carry_ref[...] = jnp.zeros((KB, DA), F32)

        for win, own, after in ((qw_ref, qc_ref, qn_ref), (dow_ref, doc_ref, don_ref), (lw_ref, lc_ref, ln_ref),
                                (dw_ref, dc_ref, dn_ref)):
            win[0:BQ, :] = own[...]
            win[BQ:qw, :] = after[...]
        if final:
            for n, (dil, acc) in enumerate(((DILATIONS[1], acc4_ref), (DILATIONS[2], acc16_ref))):
                for r in range(dil):
                    for cb in range(n_cb):
                        nat_ref[n, cb, pl.ds(r, BQ // dil, stride=dil), :] = acc[r, :, cb * DH:(cb + 1) * DH]

        a = lax.broadcasted_iota(jnp.int32, (qw, BQ), 0)
        b = lax.broadcasted_iota(jnp.int32, (qw, BQ), 1)
        n_q = jnp.where(i == n_i - 1, BQ, qw)
        mask = (b <= a) & (a <= b + KB) & (a < n_q)
        tok = pl.ds(pl.program_id(0) + d * BQ * i, BQ, stride=d)
        tabs = (c_ref[tok, :], sn_ref[tok, :])
        for h in range(N_HEADS):
            hs = slice(h * DH, (h + 1) * DH)
            hc = slice(h, h + 1)
            q, k, vv, dob = qw_ref[:, hs], kc_ref[:, hs], vc_ref[:, hs], dow_ref[:, hs]
            s = _dot(q, k, NT) * SCALE
            p = jnp.exp(jnp.where(mask, s - lw_ref[:, hc], NEG))
            dp = _dot(dob, vv, NT)
            ds = (p * (dp - dw_ref[:, hc]) * SCALE).astype(BF16)
            dq = _dot(ds, k)
            dk = _dot(ds, q, T_N)
            dv = _dot(p.astype(BF16), dob, T_N)
            dq_lo = dq[0:KB] + carry_ref[:, hs]
            carry_ref[:, hs] = dq[BQ:qw]
            dq_own = _rope_t(jnp.concatenate([dq_lo, dq[KB:BQ]], axis=0), *tabs)
            for base, gv in ((0, dq_own), (DA, _rope_t(dk, *tabs)), (2 * DA, dv)):
                if final:
                    cb = base // DH + h
                    gv = gv + nat_ref[0, cb] + nat_ref[1, cb]
                out_ref[:, base + h * DH:base + (h + 1) * DH] = gv.astype(out_dtype)

    def cur(rows, width, col=0):
        return pl.BlockSpec((None, rows, width), lambda r, i: (r, i, col))

    def nx(width, col=0):
        return pl.BlockSpec((None, KB, width), lambda r, i: (r, nxt(i), col))

    in_specs = [cur(BQ, DA), nx(DA), cur(BQ, DA, 1), cur(BQ, DA, 2), cur(BQ, DA), nx(DA),
                cur(BQ, LANES), nx(LANES), cur(BQ, LANES), nx(LANES)]
    in_specs += [pl.BlockSpec((S, DH), lambda r, i: (0, 0), pipeline_mode=pl.Buffered(1))] * 2
    args = [h16, h16, h16, h16, do, do, lse, lse, dd, dd] + list(rope)
    scratch = [pltpu.VMEM((KB, DA), F32), pltpu.VMEM((qw, DA), BF16), pltpu.VMEM((qw, DA), BF16),
               pltpu.VMEM((qw, LANES), F32), pltpu.VMEM((qw, LANES), F32)]
    if final:
        assert d == 1
        in_specs += [pl.BlockSpec((dil, BQ // dil, 3 * DA), lambda r, i: (0, i, 0)) for dil in DILATIONS[1:]]
        in_specs.append(pl.BlockSpec(memory_space=pl.ANY))
        args += list(others) + [dh[None]]
        scratch.append(pltpu.VMEM((2, n_cb, BQ, DH), F32))
    return pl.pallas_call(
        body, name=name, grid=(d, n_i),
        in_specs=in_specs,
        out_specs=cur(BQ, 3 * DA),
        out_shape=SDS((d, n_sub, D_IN if final else 3 * DA), out_dtype),
        input_output_aliases={len(args) - 1: 0} if final else {},
        scratch_shapes=scratch,
        compiler_params=_params(("arbitrary", "arbitrary")),
    )(*args)


def _grad_w_out(yt, dzb):
    tn = 256

    def body(yt_ref, dz_ref, o_ref, o16_ref):
        acc = _dot(yt_ref[...], dz_ref[...])
        o_ref[...] = acc
        o16_ref[...] = acc.astype(BF16)

    cols = pl.BlockSpec((D, tn), lambda c: (0, c))
    o, o16 = pl.pallas_call(
        body, name="grad_w_out", grid=(D // tn,),
        in_specs=[pl.BlockSpec((D, S), lambda c: (0, 0), pipeline_mode=pl.Buffered(1)),
                  pl.BlockSpec((S, tn), lambda c: (0, c))],
        out_specs=[cols, cols],
        out_shape=[SDS((D, D), F32), SDS((D, D), BF16)],
        compiler_params=_params(("arbitrary",)),
    )(yt, dzb)
    return o.reshape(N_DEV, OB, D), o16.reshape(N_DEV, OB, D)


def _grad_w_in(xt, dh):
    tn = 256
    per_blk = WB // tn

    def body(xt_ref, dh_ref, o_ref, o16_ref):
        acc = _dot(xt_ref[...], dh_ref[...])
        o_ref[...] = acc
        o16_ref[...] = acc.astype(BF16)

    blk = pl.BlockSpec((None, D, tn), lambda c: (c // per_blk, 0, c % per_blk))
    return pl.pallas_call(
        body, name="grad_w_in", grid=(D_IN // tn,),
        in_specs=[pl.BlockSpec((D, S), lambda c: (0, 0), pipeline_mode=pl.Buffered(1)),
                  pl.BlockSpec((S, tn), lambda c: (0, c))],
        out_specs=[blk, blk],
        out_shape=[SDS((N_DEV, D, WB), F32), SDS((N_DEV, D, WB), BF16)],
        compiler_params=_params(("arbitrary",)),
    )(xt, dh)


def _grad_x(dz, dh, wgs, rider=None):
    tm = 1024

    def body(dz_ref, dh_ref, *rest):
        o_ref = rest[-1]

        @pl.when(pl.program_id(1) == 0)
        def _():
            o_ref[...] = ALPHA * dz_ref[...]

        acc = _dot(dh_ref[:, 0:TN], rest[0][...], NT)
        for s in range(1, W_PHASES):
            acc = acc + _dot(dh_ref[:, s * TN:(s + 1) * TN], rest[s][...], NT)
        o_ref[...] += acc

    outs, carried = _call_carrying(
        body, rider, (S // tm, N_DEV),
        [pl.BlockSpec((tm, D), lambda i, j: (i, 0)), pl.BlockSpec((tm, WB), lambda i, j: (i, j))]
        + [pl.BlockSpec((None, D, TN), lambda i, j: (j, 0, 0))] * W_PHASES,
        [pl.BlockSpec((tm, D), lambda i, j: (i, 0))], [SDS((S, D), F32)], [], [dz, dh] + list(wgs), "grad_x")
    return outs[0], carried


def _place():
    x, y, c = lax.axis_index("x"), lax.axis_index("y"), lax.axis_index("c")
    chips = [(x, y), (1 - x, y), (x, 1 - y), (1 - x, 1 - y)]
    return x, y, c, chips


def _blk(x, y, c):
    return 4 * x + 2 * y + c


def _adamw(w, g, m, v):
    m = ADAM_B1 * m + (1.0 - ADAM_B1) * g
    v = ADAM_B2 * v + (1.0 - ADAM_B2) * (g * g)
    m_hat = m / (1.0 - ADAM_B1 ** ADAM_STEP)
    v_hat = v / (1.0 - ADAM_B2 ** ADAM_STEP)
    delta = -ADAM_LR * (m_hat / (jnp.sqrt(v_hat) + ADAM_EPS) + ADAM_WD * w)
    return delta, m, v


class _Rider:
    def __init__(self, args, in_specs, out_shape, out_specs, scratch, start, finish):
        self.args, self.in_specs, self.out_shape, self.out_specs = args, in_specs, out_shape, out_specs
        self.scratch, self.start, self.finish = scratch, start, finish


def _carry(body, rider, n_in, n_out, first, last):
    if rider is None:
        return body
    r_in, r_out, r_scr = len(rider.args), len(rider.out_shape), len(rider.scratch)

    def carrying(*refs):
        o0 = n_in + r_in
        s0 = o0 + n_out + r_out
        s1 = len(refs) - r_scr
        theirs = (refs[n_in:o0], refs[o0 + n_out:s0], refs[s1:])
        pl.when(first())(lambda: rider.start(*theirs))
        body(*refs[:n_in], *refs[o0:o0 + n_out], *refs[s0:s1])
        pl.when(last())(lambda: rider.finish(*theirs))

    return carrying


def _call_carrying(body, rider, grid, in_specs, out_specs, out_shape, scratch, args, name, aliases=None):
    n_in, n_out = len(in_specs), len(out_specs)
    ids = lambda: [pl.program_id(a) for a in range(len(grid))]
    first = lambda: functools.reduce(jnp.logical_and, [i == 0 for i in ids()])
    last = lambda: functools.reduce(jnp.logical_and, [i == n - 1 for i, n in zip(ids(), grid)])
    if rider is not None:
        in_specs, args = in_specs + rider.in_specs, list(args) + rider.args
        out_specs, out_shape = out_specs + rider.out_specs, out_shape + rider.out_shape
        scratch = scratch + rider.scratch
    outs = pl.pallas_call(
        _carry(body, rider, n_in, n_out, first, last), name=name, grid=grid,
        in_specs=in_specs, out_specs=out_specs, out_shape=out_shape, scratch_shapes=scratch,
        input_output_aliases=aliases or {}, compiler_params=_params(("arbitrary",) * len(grid)),
    )(*args)
    return list(outs[:n_out]), list(outs[n_out:])


def _gather_copy(tensors, send_sems, recv_sems, t, k, block, to, src=None):
    dst = tensors[t][1](_blk(*block))
    return pltpu.make_async_remote_copy(
        src_ref=dst if src is None else src, dst_ref=dst,
        send_sem=send_sems.at[t, k], recv_sem=recv_sems.at[t, k], device_id=to, device_id_type=MESH)


def _gather_start(tensors, send_sems, recv_sems, local_sems):
    x, y, c, chips = _place()
    me, sib = (x, y, c), (x, y, 1 - c)
    for t, (src, dst) in enumerate(tensors):
        pltpu.make_async_copy(src, dst(_blk(*me)), local_sems.at[t]).start()
        _gather_copy(tensors, send_sems, recv_sems, t, 0, me, sib, src).start()
        for j in (1, 2, 3):
            _gather_copy(tensors, send_sems, recv_sems, t, j, me, (*chips[j], c), src).start()


def _gather_finish(tensors, send_sems, recv_sems, local_sems):
    x, y, c, chips = _place()
    me, sib = (x, y, c), (x, y, 1 - c)
    copy = functools.partial(_gather_copy, tensors, send_sems, recv_sems)
    for t in range(len(tensors)):
        for j in (1, 2, 3):
            copy(t, j, (*chips[j], c), me).wait_recv()
            copy(t, 3 + j, (*chips[j], c), sib).start()
    for t, (src, dst) in enumerate(tensors):
        copy(t, 0, sib, me).wait_recv()
        for j in (1, 2, 3):
            copy(t, 3 + j, (*chips[j], 1 - c), me).wait_recv()
        copy(t, 0, me, sib, src).wait_send()
        for j in (1, 2, 3):
            copy(t, j, me, (*chips[j], c), src).wait_send()
            copy(t, 3 + j, (*chips[j], c), sib).wait_send()
        pltpu.make_async_copy(src, dst(_blk(*me)), local_sems.at[t]).wait()


W_PHASES = WB // TN


def _cast_shards(w_in_s, w_out_s, w_pool_s):
    def body(*refs):
        for src, dst in zip(refs[:3], refs[3:]):
            dst[...] = src[...].astype(BF16)

    vmem = pl.BlockSpec(memory_space=pltpu.VMEM)
    return pl.pallas_call(
        body, name="cast_shards", in_specs=[vmem] * 3, out_specs=[vmem] * 3,
        out_shape=[SDS(w.shape, BF16) for w in (w_in_s, w_out_s, w_pool_s)],
        compiler_params=_params(),
    )(w_in_s, w_out_s, w_pool_s)


def _gather_w_in_rider(shard16, s):
    def tensors(ins, outs):
        return [(ins[0].at[:, pl.ds(s * TN, TN)], lambda b: outs[0].at[b])]

    hbm = pl.BlockSpec(memory_space=pl.ANY)
    return _Rider(
        args=[shard16], in_specs=[hbm], out_shape=[SDS((N_DEV, D, TN), BF16)], out_specs=[hbm],
        scratch=[pltpu.SemaphoreType.DMA((1, 7)), pltpu.SemaphoreType.DMA((1, 7)), pltpu.SemaphoreType.DMA((1,))],
        start=lambda ins, outs, scr: _gather_start(tensors(ins, outs), *scr),
        finish=lambda ins, outs, scr: _gather_finish(tensors(ins, outs), *scr))


def _gather_small_rider(w_out16, w_pool16):
    def tensors(ins, outs):
        gout_ref, gpool_ref = outs

        def pool_rows(b):
            return gpool_ref.at[:, pl.ds(pl.multiple_of(b * PB, PB), PB), :]

        return [(ins[0], lambda b: gout_ref.at[b]), (ins[1], pool_rows)]

    hbm = pl.BlockSpec(memory_space=pl.ANY)
    return _Rider(
        args=[w_out16, w_pool16], in_specs=[hbm, hbm],
        out_shape=[SDS((N_DEV, OB, D), BF16), SDS((N_GROUPS, GC, GC), BF16)], out_specs=[hbm, hbm],
        scratch=[pltpu.SemaphoreType.DMA((2, 7)), pltpu.SemaphoreType.DMA((2, 7)), pltpu.SemaphoreType.DMA((2,))],
        start=lambda ins, outs, scr: _gather_start(tensors(ins, outs), *scr),
        finish=lambda ins, outs, scr: _gather_finish(tensors(ins, outs), *scr))


def _block_table():
    x, y, c, chips = _place()
    return jnp.stack([_blk(*chip, c) for chip in chips]).astype(jnp.int32)


def _rs_exchange(name, p16):
    _, r_tot, cols = p16.shape

    def body(p16_ref, r1_ref, send_sems, recv_sems):
        x, y, c, chips = _place()
        copies = [pltpu.make_async_remote_copy(
            src_ref=p16_ref.at[_blk(*chips[k], 1 - c)], dst_ref=r1_ref.at[k],
            send_sem=send_sems.at[k], recv_sem=recv_sems.at[k], device_id=(x, y, 1 - c), device_id_type=MESH)
            for k in range(4)]
        for cp in copies:
            cp.start()
        for cp in copies:
            cp.wait()

    hbm = pl.BlockSpec(memory_space=pl.ANY)
    return pl.pallas_call(
        body, name=name, in_specs=[hbm], out_specs=hbm, out_shape=SDS((4, r_tot, cols), BF16),
        scratch_shapes=[pltpu.SemaphoreType.DMA((4,)), pltpu.SemaphoreType.DMA((4,))],
        compiler_params=_params(),
    )(p16)


def _chip_sums(name, table, p32, r1, rc):
    _, r_tot, cols = p32.shape

    def body(tbl_ref, p_ref, r_ref, o_ref):
        o_ref[...] = (p_ref[...] + r_ref[...].astype(F32)).astype(BF16)

    return pl.pallas_call(
        body, name=name,
        grid_spec=pltpu.PrefetchScalarGridSpec(
            num_scalar_prefetch=1, grid=(3, r_tot // rc),
            in_specs=[pl.BlockSpec((None, rc, cols), lambda k, ch, tbl: (tbl[k + 1], ch, 0)),
                      pl.BlockSpec((None, rc, cols), lambda k, ch, tbl: (k + 1, ch, 0))],
            out_specs=pl.BlockSpec((None, rc, cols), lambda k, ch, tbl: (k, ch, 0))),
        out_shape=SDS((3, r_tot, cols), BF16),
        compiler_params=_params(("arbitrary", "arbitrary")),
    )(table, p32, r1)


def _stage2_rider(sums, stats=None):
    n_t = len(sums)

    def copies(ins, outs, scr):
        x, y, c, chips = _place()
        out = []
        for t in range(n_t):
            for k in (1, 2, 3):
                out.append(pltpu.make_async_remote_copy(
                    src_ref=ins[t].at[k - 1], dst_ref=outs[t].at[k - 1],
                    send_sem=scr[0].at[t, k - 1], recv_sem=scr[1].at[t, k - 1],
                    device_id=(*chips[k], c), device_id_type=MESH))
        if stats is not None:
            for k in range(1, N_DEV):
                peer = (x ^ ((k >> 2) & 1), y ^ ((k >> 1) & 1), c ^ (k & 1))
                out.append(pltpu.make_async_remote_copy(
                    src_ref=scr[4], dst_ref=outs[n_t].at[_blk(x, y, c)],
                    send_sem=scr[2].at[k - 1], recv_sem=scr[3].at[k - 1], device_id=peer, device_id_type=MESH))
        return out

    def own_rows(outs, scr):
        x, y, c, _ = _place()
        return pltpu.make_async_copy(scr[4], outs[n_t].at[_blk(x, y, c)], scr[5])

    def start(ins, outs, scr):
        if stats is not None:
            scr[4][...] = ins[n_t][...]
            scr[4][3:4, 0:DP] = ins[n_t + 1][0:1, :]
            own_rows(outs, scr).start()
        for cp in copies(ins, outs, scr):
            cp.start()

    def finish(ins, outs, scr):
        for cp in copies(ins, outs, scr):
            cp.wait()
        if stats is not None:
            own_rows(outs, scr).wait()

    vmem = pl.BlockSpec(memory_space=pltpu.VMEM)
    hbm = pl.BlockSpec(memory_space=pl.ANY)
    scratch = [pltpu.SemaphoreType.DMA((n_t, 3)), pltpu.SemaphoreType.DMA((n_t, 3))]
    args, in_specs = list(sums), [hbm] * n_t
    out_shape, out_specs = [SDS(s.shape, BF16) for s in sums], [hbm] * n_t
    if stats is not None:
        scratch += [pltpu.SemaphoreType.DMA((N_DEV - 1,)), pltpu.SemaphoreType.DMA((N_DEV - 1,)),
                    pltpu.VMEM((8, D), F32), pltpu.SemaphoreType.DMA(())]
        args, in_specs = args + list(stats), in_specs + [vmem, vmem]
        out_shape, out_specs = out_shape + [SDS((N_DEV, 8, D), F32)], out_specs + [hbm]
    return _Rider(args, in_specs, out_shape, out_specs, scratch, start, finish)


def _adamw_shard(name, table, p32, r1, r2, w, m, v, rc):
    _, r_tot, cols = p32.shape

    def body(tbl_ref, p_ref, r1_ref, r2_ref, w_ref, m_ref, v_ref, g_ref, d_ref, nm_ref, nv_ref):
        g = p_ref[...] + r1_ref[...].astype(F32)
        for k in range(3):
            g = g + r2_ref[k].astype(F32)
        delta, nm, nv = _adamw(w_ref[...], g, m_ref[...], v_ref[...])
        g_ref[...] = g
        d_ref[...] = delta
        nm_ref[...] = nm
        nv_ref[...] = nv

    rows = pl.BlockSpec((rc, cols), lambda ch, tbl: (ch, 0))
    shard = SDS((r_tot, cols), F32)
    return pl.pallas_call(
        body, name=name,
        grid_spec=pltpu.PrefetchScalarGridSpec(
            num_scalar_prefetch=1, grid=(r_tot // rc,),
            in_specs=[pl.BlockSpec((None, rc, cols), lambda ch, tbl: (tbl[0], ch, 0)),
                      pl.BlockSpec((None, rc, cols), lambda ch, tbl: (0, ch, 0)),
                      pl.BlockSpec((3, rc, cols), lambda ch, tbl: (0, ch, 0)), rows, rows, rows],
            out_specs=[rows, rows, rows, rows]),
        out_shape=[shard, shard, shard, shard],
        compiler_params=_params(("arbitrary",)),
    )(table, p32, r1, r2, w, m, v)


def _replicated_adamw(gathered, gain, bias, scale, m_gain, m_bias, m_scale, v_gain, v_bias, v_scale):
    def body(all_ref, s_ref, g_ref, b_ref, ms_ref, mg_ref, mb_ref, vs_ref, vg_ref, vb_ref, loss_ref, *outs):
        tot = all_ref[0]
        for b in range(1, N_DEV):
            tot = tot + all_ref[b]
        loss_ref[...] = tot[2:3, 0:LANES]
        for n, (row, width, w_r, m_r, v_r) in enumerate(((3, DP, s_ref, ms_ref, vs_ref), (0, D, g_ref, mg_ref, vg_ref),
                                                         (1, D, b_ref, mb_ref, vb_ref))):
            g = tot[row:row + 1, 0:width]
            delta, nm, nv = _adamw(w_r[...], g, m_r[...], v_r[...])
            for out, val in zip(outs[4 * n:4 * n + 4], (g, delta, nm, nv)):
                out[...] = val

    vmem = pl.BlockSpec(memory_space=pltpu.VMEM)
    shapes = [SDS((1, LANES), F32)] + [SDS((1, width), F32) for width in (DP, D, D) for _ in range(4)]
    return pl.pallas_call(
        body, name="replicated_adamw",
        in_specs=[vmem] * 10, out_specs=[vmem] * len(shapes), out_shape=shapes,
        compiler_params=_params(),
    )(gathered, scale, gain, bias, m_scale, m_gain, m_bias, v_scale, v_gain, v_bias)


def kernel(x, w_in, w_pool, pool_scale, w_out, ln_gain, ln_bias, loss_target, m_w_in, m_w_pool, m_pool_scale, m_w_out, m_ln_gain, m_ln_bias, v_w_in, v_w_pool, v_pool_scale, v_w_out, v_ln_gain, v_ln_bias):
    pool_rows = (N_GROUPS * PB, GC)
    x2, target = x[0], loss_target[0]
    rope = _rope_tables()
    table = _block_table()

    shard16, w_out16, w_pool16 = _cast_shards(w_in[0], w_out[0], w_pool[0])
    (xb, xt), (wg0,) = _prep_x(x2, _gather_w_in_rider(shard16, 0))
    h, (wg1,) = _proj_phase(xb, wg0, 0, rope, None, _gather_w_in_rider(shard16, 1))
    h, (wg2,) = _proj_phase(xb, wg1, 1, rope, h, _gather_w_in_rider(shard16, 2))
    h, (wg_out, wg_pool) = _proj_phase(xb, wg2, 2, rope, h, _gather_small_rider(w_out16, w_pool16))
    wg_out = wg_out.reshape(D, D)
    h16, h32 = [h[0][None], h[1], h[2]], h[3]
    fwd = [_attn_fwd(h16[n], f"attn_fwd_d{d}") for n, d in enumerate(DILATIONS)]
    y, yt, attn, lse1, lse4, lse16 = _mix(fwd[0][0][0], fwd[0][1][0], *fwd[1], *fwd[2], h32, wg_pool, pool_scale)
    dz, dzb, stats = _outproj_ln(y, wg_out, x2, target, ln_gain, ln_bias)

    dwout, dwout16 = _grad_w_out(yt, dzb)
    r1_out = _rs_exchange("rs_out_exchange", dwout16)
    s2_out = _chip_sums("rs_out_sums", table, dwout, r1_out, OB)
    mid, (r2_out,) = _bwd_mid(dzb, wg_out, h32, attn, wg_pool, pool_scale, rider=_stage2_rider([s2_out]))
    dh, do1, do4, do16, dd1, dd4, dd16, dwp, dwp16, gps = mid
    g_out, d_out, nm_out, nv_out = _adamw_shard(
        "adamw_w_out", table, dwout, r1_out, r2_out, w_out[0], m_w_out[0], v_w_out[0], OB // 2)

    do, lse, dd = [do1[None], do4, do16], [lse1[None], lse4, lse16], [dd1[None], dd4, dd16]
    others = [_attn_bwd(h16[n], do[n], lse[n], dd[n], rope, f"attn_bwd_d{DILATIONS[n]}") for n in (1, 2)]
    dh = _attn_bwd(h16[0], do[0], lse[0], dd[0], rope, "attn_bwd_d1", others=others, dh=dh)[0]

    dwin, dwin16 = _grad_w_in(xt, dh)
    r1_in = _rs_exchange("rs_in_exchange", dwin16)
    s2_in = _chip_sums("rs_in_sums", table, dwin, r1_in, 512)
    r1_pool = _rs_exchange("rs_pool_exchange", dwp16)
    s2_pool = _chip_sums("rs_pool_sums", table, dwp, r1_pool, N_GROUPS * PB)
    grad_x, (r2_in, r2_pool, gathered) = _grad_x(
        dz, dh, (wg0, wg1, wg2), rider=_stage2_rider([s2_in, s2_pool], stats=(stats, gps)))
    g_in, d_in, nm_in, nv_in = _adamw_shard(
        "adamw_w_in", table, dwin, r1_in, r2_in, w_in[0], m_w_in[0], v_w_in[0], 256)
    g_pool, d_pool, nm_pool, nv_pool = _adamw_shard(
        "adamw_w_pool", table, dwp, r1_pool, r2_pool, w_pool[0].reshape(pool_rows), m_w_pool[0].reshape(pool_rows),
        v_w_pool[0].reshape(pool_rows), N_GROUPS * PB)
    loss, *small = _replicated_adamw(gathered, ln_gain, ln_bias, pool_scale, m_ln_gain, m_ln_bias, m_pool_scale,
                                     v_ln_gain, v_ln_bias, v_pool_scale)
    (g_ps, d_ps, nm_ps, nv_ps, g_gain, d_gain, nm_gain, nv_gain, g_bias, d_bias, nm_bias, nv_bias) = small

    shard4 = lambda t: t.reshape(1, N_GROUPS, PB, GC)
    lead = lambda t: t[None]
    return (loss[0, 0], lead(grad_x),
            lead(g_in), shard4(g_pool), g_ps, lead(g_out), g_gain, g_bias,
            lead(d_in), shard4(d_pool), d_ps, lead(d_out), d_gain, d_bias,
            lead(nm_in), shard4(nm_pool), nm_ps, lead(nm_out), nm_gain, nm_bias,
            lead(nv_in), shard4(nv_pool), nv_ps, lead(nv_out), nv_gain, nv_bias)
```

```python
import functools

import jax
import jax.numpy as jnp
from jax import lax
from jax.experimental import pallas as pl
from jax.experimental.pallas import tpu as pltpu

F32 = jnp.float32
BF16 = jnp.bfloat16
SDS = jax.ShapeDtypeStruct
MESH = pl.DeviceIdType.MESH

N_DEV = 8
S = 4096
D = 2048
N_HEADS = 8
DH = 128
DA = N_HEADS * DH
DP = 1024
N_GROUPS = 4
GC = DP // N_GROUPS
POOL_WINDOWS = (2, 4, 8, 16)
HALO = 16
D_IN = 3 * DA + DP + D
WB = D_IN // N_DEV
TN = 256
HW = 3 * DA
OB = D // N_DEV
PB = GC // N_DEV
ROPE_DIM = DH // 4
ROPE_HALF = ROPE_DIM // 2
ROPE_THETA = 500000.0
DILATIONS = (1, 4, 16)
KB = 128
LN_EPS = 1e-5
ALPHA = 2.0 ** 0.25
SCALE = DH ** -0.5
NEG = -1e30
ADAM_LR, ADAM_B1, ADAM_B2, ADAM_EPS, ADAM_WD, ADAM_STEP = 0.001, 0.9, 0.999, 1e-08, 0.01, 10

VMEM_LIMIT_V7X = 61 * 1024 * 1024

NT = (((1,), (1,)), ((), ()))
T_N = (((0,), (0,)), ((), ()))


def _params(sem=None):
    return pltpu.CompilerParams(dimension_semantics=sem, vmem_limit_bytes=VMEM_LIMIT_V7X)


def _dot(a, b, dims=None):
    if dims is None:
        return jnp.dot(a, b, preferred_element_type=F32)
    return lax.dot_general(a, b, dims, preferred_element_type=F32)


def _rope_tables():
    inv_freq = ROPE_THETA ** (-(2.0 * jnp.arange(ROPE_HALF, dtype=F32)) / ROPE_DIM)
    ang = jnp.arange(S, dtype=jnp.int32).astype(F32)[:, None] * inv_freq[None, :]
    cos, sin = jnp.cos(ang), jnp.sin(ang)
    rest = DH - ROPE_DIM
    c = jnp.concatenate([cos, cos, jnp.ones((S, rest), F32)], axis=1)
    sn = jnp.concatenate([-sin, sin, jnp.zeros((S, rest), F32)], axis=1)
    return c, sn


def _rope_partner(t):
    lane = lax.broadcasted_iota(jnp.int32, t.shape, 1)
    return jnp.where(lane < ROPE_HALF, pltpu.roll(t, DH - ROPE_HALF, 1), pltpu.roll(t, ROPE_HALF, 1))


def _rope(t, c, sn):
    return t * c + _rope_partner(t) * sn


def _rope_t(g, c, sn):
    return g * c - _rope_partner(g) * sn


def _prep_x(x, rider):
    tm = 512

    def body(x_ref, xb_ref, xt_ref):
        xv = x_ref[...]
        xb_ref[...] = xv.astype(BF16)
        xt_ref[...] = xv.T.astype(BF16)

    return _call_carrying(
        body, rider, (S // tm,), [pl.BlockSpec((tm, D), lambda i: (i, 0))],
        [pl.BlockSpec((tm, D), lambda i: (i, 0)), pl.BlockSpec((D, tm), lambda i: (0, i))],
        [SDS((S, D), BF16), SDS((D, S), BF16)], [], [x], "prep_x")


def _residues(slab_ref, idx, d, r, n):
    return slab_ref[(*idx, pl.ds(r, n, stride=d), slice(None))]


def _proj_phase(xb, wg, s, rope, prev, rider):
    n_heads = TN // DH
    n16 = HW // TN
    j16 = n16 // W_PHASES
    n_rope = 2 * DA // TN
    once = pl.Buffered(1)

    def body(x_ref, w_ref, c_ref, sn_ref, *rest):
        h16_ref, h4_ref, h16r_ref, h32_ref, slab_ref, res_ref = rest[-6:]
        j = pl.program_id(0)
        d4, d16 = DILATIONS[1], DILATIONS[2]
        acc = _dot(x_ref[...], w_ref[...])

        def heads(with_rope):
            for hh in range(n_heads):
                hs = slice(hh * DH, (hh + 1) * DH)
                t = _rope(acc[:, hs], c_ref[...], sn_ref[...]) if with_rope else acc[:, hs]
                h16_ref[:, hs] = t.astype(BF16)
                slab_ref[...] = t
                for q in range(d4):
                    t4 = _residues(slab_ref, (), d4, q, S // d4)
                    h4_ref[q, :, hs] = t4.astype(BF16)
                    res_ref[...] = t4
                    for m in range(d16 // d4):
                        h16r_ref[d4 * m + q, :, hs] = _residues(res_ref, (), d4, m, S // d16).astype(BF16)

        pl.when(W_PHASES * j + s < n_rope)(lambda: heads(True))
        pl.when((W_PHASES * j + s >= n_rope) & (j < j16))(lambda: heads(False))

        @pl.when(j >= j16)
        def _():
            h32_ref[...] = acc

    col16 = lambda j: W_PHASES * jnp.minimum(j, j16 - 1) + s
    col32 = lambda j: W_PHASES * jnp.maximum(j, j16) + s - n16
    hbm = pl.BlockSpec(memory_space=pl.ANY)
    in_specs = [pl.BlockSpec((S, D), lambda j: (0, 0), pipeline_mode=once),
                pl.BlockSpec((None, D, TN), lambda j: (j, 0, 0))]
    in_specs += [pl.BlockSpec((S, DH), lambda j: (0, 0), pipeline_mode=once)] * 2
    args = [xb, wg] + list(rope)
    aliases = {}
    if prev is not None:
        aliases = {len(args) + n: n for n in range(4)}
        in_specs, args = in_specs + [hbm] * 4, args + list(prev)
    out_specs = [pl.BlockSpec((S, TN), lambda j: (0, col16(j)))]
    out_specs += [pl.BlockSpec((d, S // d, TN), lambda j: (0, 0, col16(j))) for d in DILATIONS[1:]]
    out_specs += [pl.BlockSpec((S, TN), lambda j: (0, col32(j)))]
    out_shape = [SDS((S, HW), BF16)] + [SDS((d, S // d, HW), BF16) for d in DILATIONS[1:]] + [SDS((S, HW), F32)]
    return _call_carrying(
        body, rider, (N_DEV,), in_specs, out_specs, out_shape,
        [pltpu.VMEM((S, DH), F32), pltpu.VMEM((S // DILATIONS[1], DH), F32)], args,
        f"proj_phase{s}", aliases)


BQ = 2 * KB
LANES = 128


def _to_lane(acc, col, h):
    lane = lax.broadcasted_iota(jnp.int32, acc.shape, 1)
    return jnp.where(lane == h, col, acc)


def _attn_fwd(h16, name):
    d, n_sub, _ = h16.shape
    n_i = n_sub // BQ
    kw = KB + BQ

    def prev(i):
        return jnp.maximum(2 * i - 1, 0)

    def body(q_ref, kc_ref, kp_ref, vc_ref, vp_ref, o_ref, l_ref, kw_ref, vw_ref):
        i = pl.program_id(1)
        kw_ref[0:KB, :] = kp_ref[...]
        kw_ref[KB:kw, :] = kc_ref[...]
        vw_ref[0:KB, :] = vp_ref[...]
        vw_ref[KB:kw, :] = vc_ref[...]
        a = lax.broadcasted_iota(jnp.int32, (KB, 2 * KB), 0)
        b = lax.broadcasted_iota(jnp.int32, (KB, 2 * KB), 1)
        band = (b >= a) & (b <= a + KB)
        first_key = jnp.where(i == 0, KB, 0)
        masks = (band & (b >= first_key), band)
        for half in range(2):
            rs = slice(half * KB, (half + 1) * KB)
            ks = slice(half * KB, (half + 2) * KB)
            lse = jnp.zeros((KB, LANES), F32)
            for h in range(N_HEADS):
                hs = slice(h * DH, (h + 1) * DH)
                s = jnp.where(masks[half], _dot(q_ref[rs, hs], kw_ref[ks, hs], NT) * SCALE, NEG)
                m = jnp.max(s, axis=1, keepdims=True)
                p = jnp.exp(s - m)
                den = jnp.sum(p, axis=1, keepdims=True)
                o_ref[rs, hs] = _dot(p.astype(BF16), vw_ref[ks, hs]) / den
                lse = _to_lane(lse, m + jnp.log(den), h)
            l_ref[rs, :] = lse

    return pl.pallas_call(
        body, name=name, grid=(d, n_i),
        in_specs=[pl.BlockSpec((None, BQ, DA), lambda r, i: (r, i, 0)),
                  pl.BlockSpec((None, BQ, DA), lambda r, i: (r, i, 1)),
                  pl.BlockSpec((None, KB, DA), lambda r, i: (r, prev(i), 1)),
                  pl.BlockSpec((None, BQ, DA), lambda r, i: (r, i, 2)),
                  pl.BlockSpec((None, KB, DA), lambda r, i: (r, prev(i), 2))],
        out_specs=[pl.BlockSpec((None, BQ, DA), lambda r, i: (r, i, 0)),
                   pl.BlockSpec((None, BQ, LANES), lambda r, i: (r, i, 0))],
        out_shape=[SDS((d, n_sub, DA), F32), SDS((d, n_sub, LANES), F32)],
        scratch_shapes=[pltpu.VMEM((kw, DA), BF16), pltpu.VMEM((kw, DA), BF16)],
        compiler_params=_params(("arbitrary", "arbitrary")),
    )(h16, h16, h16, h16, h16)


def _pooled(ext_ref, g, rows, tm):
    w = POOL_WINDOWS[g]
    cs = slice(g * GC, (g + 1) * GC)
    cur = ext_ref[HALO:HALO + tm, cs]
    win = cur
    for j in range(1, w):
        win = win + ext_ref[HALO - j:HALO - j + tm, cs]
    cnt = jnp.minimum(rows + 1, w).astype(F32)
    return win / cnt - cur, cnt


def _fill_ext(ext_ref, u_ref, uh_ref, blk, tm):
    @pl.when(blk == 0)
    def _():
        ext_ref[0:HALO, :] = jnp.zeros((HALO, DP), F32)

    @pl.when(blk > 0)
    def _():
        ext_ref[0:HALO, :] = uh_ref[...]

    ext_ref[HALO:HALO + tm, :] = u_ref[...]


def _residue_specs(tm, width):
    return [pl.BlockSpec((d, tm // d, width), lambda i: (0, i, 0)) for d in DILATIONS[1:]]


def _mix(o1, l1, o4, l4, o16, l16, h32, wp, scale):
    tm = 256
    n_slab = N_HEADS + 1

    def body(o1r, l1r, o4r, l4r, o16r, l16r, u_ref, uh_ref, ga_ref, gp_ref, wp_ref, sc_ref,
             y_ref, yt_ref, attn_ref, lse_ref, lse4_ref, lse16_ref, ext_ref, ys_ref, nat_ref, ls_ref):
        i = pl.program_id(0)
        for n, (d, o_r, l_r) in enumerate(((DILATIONS[1], o4r, l4r), (DILATIONS[2], o16r, l16r))):
            for r in range(d):
                rows = pl.ds(r, tm // d, stride=d)
                for h in range(N_HEADS):
                    nat_ref[n, h, rows, :] = o_r[r, :, h * DH:(h + 1) * DH]
                nat_ref[n, N_HEADS, rows, :] = l_r[r]
        la, lb, lc = l1r[...], nat_ref[0, N_HEADS], nat_ref[1, N_HEADS]
        mx = jnp.maximum(jnp.maximum(la, lb), lc)
        ea, eb, ec = jnp.exp(la - mx), jnp.exp(lb - mx), jnp.exp(lc - mx)
        z = ea + eb + ec
        wa, wb, wc = ea / z, eb / z, ec / z
        lse = mx + jnp.log(z)
        lse_ref[...] = lse
        ls_ref[...] = lse
        for d, out in ((DILATIONS[1], lse4_ref), (DILATIONS[2], lse16_ref)):
            for r in range(d):
                out[r] = ls_ref[pl.ds(r, tm // d, stride=d), :]
        for h in range(N_HEADS):
            hs = slice(h * DH, (h + 1) * DH)
            hc = slice(h, h + 1)
            attn = wa[:, hc] * o1r[:, hs] + wb[:, hc] * nat_ref[0, h] + wc[:, hc] * nat_ref[1, h]
            attn_ref[:, hs] = attn
            gt = ga_ref[:, hs]
            ys_ref[:, hs] = attn * (gt * jax.nn.sigmoid(gt))

        _fill_ext(ext_ref, u_ref, uh_ref, i, tm)
        rows = i * tm + lax.broadcasted_iota(jnp.int32, (tm, 1), 0)
        for g in range(N_GROUPS):
            cs = slice(g * GC, (g + 1) * GC)
            gs = slice(DA + g * GC, DA + (g + 1) * GC)
            pooled, _ = _pooled(ext_ref, g, rows, tm)
            po = _dot(pooled.astype(BF16), wp_ref[g]) * sc_ref[:, cs]
            gt = gp_ref[:, cs]
            ys_ref[:, gs] = po * (gt * jax.nn.sigmoid(gt))
        yv = ys_ref[...]
        y_ref[...] = yv.astype(BF16)
        yt_ref[...] = yv.T.astype(BF16)

    row = lambda i: (i, 0)
    blk = pl.BlockSpec((tm, DA), row)
    lanes = pl.BlockSpec((tm, LANES), row)
    o_res, l_res = _residue_specs(tm, DA), _residue_specs(tm, LANES)
    return pl.pallas_call(
        body, name="mix", grid=(S // tm,),
        in_specs=[blk, lanes, o_res[0], l_res[0], o_res[1], l_res[1],
                  pl.BlockSpec((tm, DP), row),
                  pl.BlockSpec((HALO, DP), lambda i: (jnp.maximum(i * (tm // HALO) - 1, 0), 0)),
                  pl.BlockSpec((tm, DA), lambda i: (i, 1)), pl.BlockSpec((tm, DP), lambda i: (i, 2)),
                  pl.BlockSpec((N_GROUPS, GC, GC), lambda i: (0, 0, 0)),
                  pl.BlockSpec((1, DP), lambda i: (0, 0))],
        out_specs=[pl.BlockSpec((tm, D), row), pl.BlockSpec((D, tm), lambda i: (0, i)), blk, lanes] + l_res,
        out_shape=[SDS((S, D), BF16), SDS((D, S), BF16), SDS((S, DA), F32), SDS((S, LANES), F32)]
        + [SDS((d, S // d, LANES), F32) for d in DILATIONS[1:]],
        scratch_shapes=[pltpu.VMEM((HALO + tm, DP), F32), pltpu.VMEM((tm, D), F32),
                        pltpu.VMEM((2, n_slab, tm, DH), F32), pltpu.VMEM((tm, LANES), F32)],
        compiler_params=_params(("arbitrary",)),
    )(o1, l1, o4, l4, o16, l16, h32, h32, h32, h32, wp, scale)


def _outproj_ln(y, wout, x, target, gain, bias):
    tm = 512
    te = 128

    def body(y_ref, w_ref, x_ref, t_ref, g_ref, b_ref, dz_ref, dzb_ref, st_ref, out_ref):
        i = pl.program_id(0)

        @pl.when(i == 0)
        def _():
            st_ref[...] = jnp.zeros((8, D), F32)

        out_ref[...] = _dot(y_ref[...], w_ref[...])
        gn = g_ref[...]
        for e in range(tm // te):
            rs = slice(e * te, (e + 1) * te)
            z = ALPHA * x_ref[rs, :] + out_ref[rs, :]
            mu = jnp.mean(z, axis=1, keepdims=True)
            zc = z - mu
            var = jnp.mean(zc * zc, axis=1, keepdims=True)
            rstd = lax.rsqrt(var + LN_EPS)
            xhat = zc * rstd
            diff = xhat * gn + b_ref[...] - t_ref[rs, :]
            dyln = diff / D
            st_ref[0:1, :] += jnp.sum(dyln * xhat, axis=0, keepdims=True)
            st_ref[1:2, :] += jnp.sum(dyln, axis=0, keepdims=True)
            row_loss = jnp.sum(diff * diff, axis=1, keepdims=True) / D
            st_ref[2:3, :] += jnp.broadcast_to(0.5 * jnp.sum(row_loss, axis=0, keepdims=True), (1, D))
            dxh = dyln * gn
            m1 = jnp.mean(dxh, axis=1, keepdims=True)
            m2 = jnp.mean(dxh * xhat, axis=1, keepdims=True)
            dz = rstd * (dxh - m1 - xhat * m2)
            dz_ref[rs, :] = dz
            dzb_ref[rs, :] = dz.astype(BF16)

    row = lambda i: (i, 0)
    const = lambda i: (0, 0)
    return pl.pallas_call(
        body, name="outproj_ln", grid=(S // tm,),
        in_specs=[pl.BlockSpec((tm, D), row),
                  pl.BlockSpec((D, D), const, pipeline_mode=pl.Buffered(1)),
                  pl.BlockSpec((tm, D), row), pl.BlockSpec((tm, D), row),
                  pl.BlockSpec((1, D), const), pl.BlockSpec((1, D), const)],
        out_specs=[pl.BlockSpec((tm, D), row), pl.BlockSpec((tm, D), row), pl.BlockSpec((8, D), const)],
        out_shape=[SDS((S, D), F32), SDS((S, D), BF16), SDS((8, D), F32)],
        scratch_shapes=[pltpu.VMEM((tm, D), F32)],
        compiler_params=_params(("arbitrary",)),
    )(y, wout, x, target, gain, bias)


def _bwd_mid(dzb, wout, h32, attn, wp, scale, rider=None):
    tm = 256
    n = S // tm

    def body(dz_ref, w_ref, ga_ref, gp_ref, at_ref, u_ref, uh_ref, wp_ref, sc_ref,
             dh_ref, do_ref, do4_ref, do16_ref, dd_ref, dd4_ref, dd16_ref, dwp_ref, dwp16_ref, gps_ref,
             ext_ref, eext_ref, acc_ref, nat_ref, ds_ref):
        i = pl.program_id(0)
        ib = n - 1 - i

        @pl.when(i == 0)
        def _():
            eext_ref[tm:tm + HALO, :] = jnp.zeros((HALO, DP), F32)
            acc_ref[...] = jnp.zeros((N_GROUPS, GC, GC), F32)
            gps_ref[...] = jnp.zeros((8, DP), F32)

        dy = _dot(dz_ref[...], w_ref[...], NT)

        def through_gate(dy_part, gt):
            sg = jax.nn.sigmoid(gt)
            return dy_part * (gt * sg), dy_part * (sg * (1.0 + gt * (1.0 - sg)))

        dat, dyg_a = through_gate(dy[:, 0:DA], ga_ref[...])
        dmix_p, dyg_p = through_gate(dy[:, DA:D], gp_ref[...])

        at = at_ref[...]
        do_ref[...] = dat.astype(BF16)
        dh_ref[:, DP:DP + DA] = (dyg_a * at).astype(BF16)
        prod = dat * at
        dd = jnp.zeros((tm, LANES), F32)
        for h in range(N_HEADS):
            hs = slice(h * DH, (h + 1) * DH)
            dd = _to_lane(dd, jnp.sum(prod[:, hs], axis=1, keepdims=True), h)
            nat_ref[h] = dat[:, hs]
        dd_ref[...] = dd
        ds_ref[...] = dd
        for d, do_out, dd_out in ((DILATIONS[1], do4_ref, dd4_ref), (DILATIONS[2], do16_ref, dd16_ref)):
            for r in range(d):
                dd_out[r] = _residues(ds_ref, (), d, r, tm // d)
                for h in range(N_HEADS):
                    do_out[r, :, h * DH:(h + 1) * DH] = _residues(nat_ref, (h,), d, r, tm // d).astype(BF16)

        _fill_ext(ext_ref, u_ref, uh_ref, ib, tm)
        rows = ib * tm + lax.broadcasted_iota(jnp.int32, (tm, 1), 0)
        for g in range(N_GROUPS):
            w = POOL_WINDOWS[g]
            cs = slice(g * GC, (g + 1) * GC)
            pooled, cnt = _pooled(ext_ref, g, rows, tm)
            pre = _dot(pooled.astype(BF16), wp_ref[g])
            sc = sc_ref[:, cs]
            dpo = dmix_p[:, cs]
            gps_ref[0:1, cs] += jnp.sum(dpo * pre, axis=0, keepdims=True)
            dh_ref[:, DP + DA + g * GC:DP + DA + (g + 1) * GC] = (dyg_p[:, cs] * (pre * sc)).astype(BF16)
            dpre = (dpo * sc).astype(BF16)
            acc_ref[g] += _dot(pooled.T.astype(BF16), dpre)
            dpooled = _dot(dpre, wp_ref[g], NT)
            eext_ref[0:tm, cs] = dpooled / cnt
            du = eext_ref[0:tm, cs]
            for j in range(1, w):
                du = du + eext_ref[j:j + tm, cs]
            dh_ref[:, cs] = (du - dpooled).astype(BF16)
        eext_ref[tm:tm + HALO, :] = eext_ref[0:HALO, :]

        @pl.when(i == n - 1)
        def _():
            for j in range(N_DEV):
                for g in range(N_GROUPS):
                    blk = acc_ref[g, j * PB:(j + 1) * PB, :]
                    dwp_ref[j, g * PB:(g + 1) * PB, :] = blk
                    dwp16_ref[j, g * PB:(g + 1) * PB, :] = blk.astype(BF16)

    rev = lambda i: (n - 1 - i, 0)
    const = lambda i: (0, 0)
    res = lambda width: [pl.BlockSpec((d, tm // d, width), lambda i: (0, n - 1 - i, 0)) for d in DILATIONS[1:]]
    pool_blocks = pl.BlockSpec((N_DEV, N_GROUPS * PB, GC), lambda i: (0, 0, 0))
    outs, carried = _call_carrying(
        body, rider, (n,),
        [pl.BlockSpec((tm, D), rev),
         pl.BlockSpec((D, D), const, pipeline_mode=pl.Buffered(1)),
         pl.BlockSpec((tm, DA), lambda i: (n - 1 - i, 1)), pl.BlockSpec((tm, DP), lambda i: (n - 1 - i, 2)),
         pl.BlockSpec((tm, DA), rev), pl.BlockSpec((tm, DP), rev),
         pl.BlockSpec((HALO, DP), lambda i: (jnp.maximum((n - 1 - i) * (tm // HALO) - 1, 0), 0)),
         pl.BlockSpec((N_GROUPS, GC, GC), lambda i: (0, 0, 0)),
         pl.BlockSpec((1, DP), const)],
        [pl.BlockSpec((tm, DP + D), lambda i: (n - 1 - i, 1)), pl.BlockSpec((tm, DA), rev)] + res(DA)
        + [pl.BlockSpec((tm, LANES), rev)] + res(LANES) + [pool_blocks, pool_blocks, pl.BlockSpec((8, DP), const)],
        [SDS((S, D_IN), BF16), SDS((S, DA), BF16)] + [SDS((d, S // d, DA), BF16) for d in DILATIONS[1:]]
        + [SDS((S, LANES), F32)] + [SDS((d, S // d, LANES), F32) for d in DILATIONS[1:]]
        + [SDS((N_DEV, N_GROUPS * PB, GC), F32), SDS((N_DEV, N_GROUPS * PB, GC), BF16), SDS((8, DP), F32)],
        [pltpu.VMEM((HALO + tm, DP), F32), pltpu.VMEM((tm + HALO, DP), F32),
         pltpu.VMEM((N_GROUPS, GC, GC), F32), pltpu.VMEM((N_HEADS, tm, DH), F32), pltpu.VMEM((tm, LANES), F32)],
        [dzb, wout, h32, h32, attn, h32, h32, wp, scale], "bwd_mid")
    return outs, carried


def _attn_bwd(h16, do, lse, dd, rope, name, others=None, dh=None):
    d, n_sub, _ = h16.shape
    n_i = n_sub // BQ
    n_kb = n_sub // KB
    qw = BQ + KB
    final = others is not None
    out_dtype = BF16 if final else F32
    n_cb = 3 * DA // DH

    def nxt(i):
        return jnp.minimum(2 * i + 2, n_kb - 1)

    def body(qc_ref, qn_ref, kc_ref, vc_ref, doc_ref, don_ref, lc_ref, ln_ref, dc_ref, dn_ref,
             c_ref, sn_ref, *rest):
        if final:
            acc4_ref, acc16_ref, _, out_ref, carry_ref, qw_ref, dow_ref, lw_ref, dw_ref, nat_ref = rest
        else:
            out_ref, carry_ref, qw_ref, dow_ref, lw_ref, dw_ref = rest
        i = pl.program_id(1)

        @pl.when(i == 0)
        def _():
            carry_ref[...] = jnp.zeros((KB, DA), F32)

        for win, own, after in ((qw_ref, qc_ref, qn_ref), (dow_ref, doc_ref, don_ref), (lw_ref, lc_ref, ln_ref),
                                (dw_ref, dc_ref, dn_ref)):
            win[0:BQ, :] = own[...]
            win[BQ:qw, :] = after[...]
        if final:
            for n, (dil, acc) in enumerate(((DILATIONS[1], acc4_ref), (DILATIONS[2], acc16_ref))):
                for r in range(dil):
                    for cb in range(n_cb):
                        nat_ref[n, cb, pl.ds(r, BQ // dil, stride=dil), :] = acc[r, :, cb * DH:(cb + 1) * DH]

        a = lax.broadcasted_iota(jnp.int32, (qw, BQ), 0)
        b = lax.broadcasted_iota(jnp.int32, (qw, BQ), 1)
        n_q = jnp.where(i == n_i - 1, BQ, qw)
        mask = (b <= a) & (a <= b + KB) & (a < n_q)
        tok = pl.ds(pl.program_id(0) + d * BQ * i, BQ, stride=d)
        tabs = (c_ref[tok, :], sn_ref[tok, :])
        for h in range(N_HEADS):
            hs = slice(h * DH, (h + 1) * DH)
            hc = slice(h, h + 1)
            q, k, vv, dob = qw_ref[:, hs], kc_ref[:, hs], vc_ref[:, hs], dow_ref[:, hs]
            s = _dot(q, k, NT) * SCALE
            p = jnp.exp(jnp.where(mask, s - lw_ref[:, hc], NEG))
            dp = _dot(dob, vv, NT)
            ds = (p * (dp - dw_ref[:, hc]) * SCALE).astype(BF16)
            dq = _dot(ds, k)
            dk = _dot(ds, q, T_N)
            dv = _dot(p.astype(BF16), dob, T_N)
            dq_lo = dq[0:KB] + carry_ref[:, hs]
            carry_ref[:, hs] = dq[BQ:qw]
            dq_own = _rope_t(jnp.concatenate([dq_lo, dq[KB:BQ]], axis=0), *tabs)
            for base, gv in ((0, dq_own), (DA, _rope_t(dk, *tabs)), (2 * DA, dv)):
                if final:
                    cb = base // DH + h
                    gv = gv + nat_ref[0, cb] + nat_ref[1, cb]
                out_ref[:, base + h * DH:base + (h + 1) * DH] = gv.astype(out_dtype)

    def cur(rows, width, col=0):
        return pl.BlockSpec((None, rows, width), lambda r, i: (r, i, col))

    def nx(width, col=0):
        return pl.BlockSpec((None, KB, width), lambda r, i: (r, nxt(i), col))

    in_specs = [cur(BQ, DA), nx(DA), cur(BQ, DA, 1), cur(BQ, DA, 2), cur(BQ, DA), nx(DA),
                cur(BQ, LANES), nx(LANES), cur(BQ, LANES), nx(LANES)]
    in_specs += [pl.BlockSpec((S, DH), lambda r, i: (0, 0), pipeline_mode=pl.Buffered(1))] * 2
    args = [h16, h16, h16, h16, do, do, lse, lse, dd, dd] + list(rope)
    scratch = [pltpu.VMEM((KB, DA), F32), pltpu.VMEM((qw, DA), BF16), pltpu.VMEM((qw, DA), BF16),
               pltpu.VMEM((qw, LANES), F32), pltpu.VMEM((qw, LANES), F32)]
    if final:
        assert d == 1
        in_specs += [pl.BlockSpec((dil, BQ // dil, 3 * DA), lambda r, i: (0, i, 0)) for dil in DILATIONS[1:]]
        in_specs.append(pl.BlockSpec(memory_space=pl.ANY))
        args += list(others) + [dh[None]]
        scratch.append(pltpu.VMEM((2, n_cb, BQ, DH), F32))
    return pl.pallas_call(
        body, name=name, grid=(d, n_i),
        in_specs=in_specs,
        out_specs=cur(BQ, 3 * DA),
        out_shape=SDS((d, n_sub, D_IN if final else 3 * DA), out_dtype),
        input_output_aliases={len(args) - 1: 0} if final else {},
        scratch_shapes=scratch,
        compiler_params=_params(("arbitrary", "arbitrary")),
    )(*args)


def _grad_w_out(yt, dzb):
    tn = 256

    def body(yt_ref, dz_ref, o_ref, o16_ref):
        acc = _dot(yt_ref[...], dz_ref[...])
        o_ref[...] = acc
        o16_ref[...] = acc.astype(BF16)

    cols = pl.BlockSpec((D, tn), lambda c: (0, c))
    o, o16 = pl.pallas_call(
        body, name="grad_w_out", grid=(D // tn,),
        in_specs=[pl.BlockSpec((D, S), lambda c: (0, 0), pipeline_mode=pl.Buffered(1)),
                  pl.BlockSpec((S, tn), lambda c: (0, c))],
        out_specs=[cols, cols],
        out_shape=[SDS((D, D), F32), SDS((D, D), BF16)],
        compiler_params=_params(("arbitrary",)),
    )(yt, dzb)
    return o.reshape(N_DEV, OB, D), o16.reshape(N_DEV, OB, D)


def _grad_w_in(xt, dh, order):
    n_send = N_DEV // 2
    n_steps = D_IN // TN
    step_of_send = [W_PHASES * k + W_PHASES + 1 for k in range(n_send)]

    def body(ord_ref, xt_ref, dh_ref, o_ref, o16_ref, r1_ref, st_ref, piece_sems, send_sems, recv_sems):
        c = pl.program_id(0)
        x, y, core, _ = _place()
        acc = _dot(xt_ref[...], dh_ref[...])
        o_ref[...] = acc

        def piece(cc):
            blk, sub = ord_ref[cc // W_PHASES], cc % W_PHASES
            return pltpu.make_async_copy(st_ref.at[cc % 2], o16_ref.at[blk, :, pl.ds(sub * TN, TN)], piece_sems.at[cc % 2])

        def send(k):
            return pltpu.make_async_remote_copy(
                src_ref=o16_ref.at[ord_ref[k]], dst_ref=r1_ref.at[k], send_sem=send_sems.at[k], recv_sem=recv_sems.at[k],
                device_id=(x, y, 1 - core), device_id_type=MESH)

        @pl.when((c >= 2) & (c < n_send * W_PHASES + 2))
        def _():
            piece(c - 2).wait()

        @pl.when(c < n_send * W_PHASES)
        def _():
            st_ref[c % 2] = acc.astype(BF16)
            piece(c).start()

        for k in range(n_send):
            pl.when(c == step_of_send[k])(lambda k=k: send(k).start())

        @pl.when(c == n_steps - 1)
        def _():
            for k in range(n_send):
                send(k).wait()

    col = lambda c, ordr: W_PHASES * ordr[c // W_PHASES] + c % W_PHASES
    hbm = pl.BlockSpec(memory_space=pl.ANY)
    partial, _, got = pl.pallas_call(
        body, name="grad_w_in",
        grid_spec=pltpu.PrefetchScalarGridSpec(
            num_scalar_prefetch=1, grid=(n_steps,),
            in_specs=[pl.BlockSpec((D, S), lambda c, ordr: (0, 0), pipeline_mode=pl.Buffered(1)),
                      pl.BlockSpec((S, TN), lambda c, ordr: (0, col(c, ordr)))],
            out_specs=[pl.BlockSpec((None, D, TN), lambda c, ordr: (ordr[c // W_PHASES], 0, c % W_PHASES)), hbm, hbm],
            scratch_shapes=[pltpu.VMEM((2, D, TN), BF16), pltpu.SemaphoreType.DMA((2,)),
                            pltpu.SemaphoreType.DMA((n_send,)), pltpu.SemaphoreType.DMA((n_send,))]),
        out_shape=[SDS((N_DEV, D, WB), F32), SDS((N_DEV, D, WB), BF16), SDS((n_send, D, WB), BF16)],
        compiler_params=_params(("arbitrary",)),
    )(order, xt, dh)
    return partial, got


def _grad_x(dz, dh, wgs, rider=None):
    tm = 1024

    def body(dz_ref, dh_ref, *rest):
        o_ref = rest[-1]

        @pl.when(pl.program_id(1) == 0)
        def _():
            o_ref[...] = ALPHA * dz_ref[...]

        acc = _dot(dh_ref[:, 0:TN], rest[0][...], NT)
        for s in range(1, W_PHASES):
            acc = acc + _dot(dh_ref[:, s * TN:(s + 1) * TN], rest[s][...], NT)
        o_ref[...] += acc

    outs, carried = _call_carrying(
        body, rider, (S // tm, N_DEV),
        [pl.BlockSpec((tm, D), lambda i, j: (i, 0)), pl.BlockSpec((tm, WB), lambda i, j: (i, j))]
        + [pl.BlockSpec((None, D, TN), lambda i, j: (j, 0, 0))] * W_PHASES,
        [pl.BlockSpec((tm, D), lambda i, j: (i, 0))], [SDS((S, D), F32)], [], [dz, dh] + list(wgs), "grad_x")
    return outs[0], carried


def _place():
    x, y, c = lax.axis_index("x"), lax.axis_index("y"), lax.axis_index("c")
    chips = [(x, y), (1 - x, y), (x, 1 - y), (1 - x, 1 - y)]
    return x, y, c, chips


def _blk(x, y, c):
    return 4 * x + 2 * y + c


def _adamw(w, g, m, v):
    m = ADAM_B1 * m + (1.0 - ADAM_B1) * g
    v = ADAM_B2 * v + (1.0 - ADAM_B2) * (g * g)
    m_hat = m / (1.0 - ADAM_B1 ** ADAM_STEP)
    v_hat = v / (1.0 - ADAM_B2 ** ADAM_STEP)
    delta = -ADAM_LR * (m_hat / (jnp.sqrt(v_hat) + ADAM_EPS) + ADAM_WD * w)
    return delta, m, v


class _Rider:
    def __init__(self, args, in_specs, out_shape, out_specs, scratch, start, finish):
        self.args, self.in_specs, self.out_shape, self.out_specs = args, in_specs, out_shape, out_specs
        self.scratch, self.start, self.finish = scratch, start, finish


def _carry(body, rider, n_in, n_out, first, last):
    if rider is None:
        return body
    r_in, r_out, r_scr = len(rider.args), len(rider.out_shape), len(rider.scratch)

    def carrying(*refs):
        o0 = n_in + r_in
        s0 = o0 + n_out + r_out
        s1 = len(refs) - r_scr
        theirs = (refs[n_in:o0], refs[o0 + n_out:s0], refs[s1:])
        pl.when(first())(lambda: rider.start(*theirs))
        body(*refs[:n_in], *refs[o0:o0 + n_out], *refs[s0:s1])
        pl.when(last())(lambda: rider.finish(*theirs))

    return carrying


def _call_carrying(body, rider, grid, in_specs, out_specs, out_shape, scratch, args, name, aliases=None):
    n_in, n_out = len(in_specs), len(out_specs)
    ids = lambda: [pl.program_id(a) for a in range(len(grid))]
    first = lambda: functools.reduce(jnp.logical_and, [i == 0 for i in ids()])
    last = lambda: functools.reduce(jnp.logical_and, [i == n - 1 for i, n in zip(ids(), grid)])
    if rider is not None:
        in_specs, args = in_specs + rider.in_specs, list(args) + rider.args
        out_specs, out_shape = out_specs + rider.out_specs, out_shape + rider.out_shape
        scratch = scratch + rider.scratch
    outs = pl.pallas_call(
        _carry(body, rider, n_in, n_out, first, last), name=name, grid=grid,
        in_specs=in_specs, out_specs=out_specs, out_shape=out_shape, scratch_shapes=scratch,
        input_output_aliases=aliases or {}, compiler_params=_params(("arbitrary",) * len(grid)),
    )(*args)
    return list(outs[:n_out]), list(outs[n_out:])


def _gather_copy(tensors, send_sems, recv_sems, t, k, block, to, src=None):
    dst = tensors[t][1](_blk(*block))
    return pltpu.make_async_remote_copy(
        src_ref=dst if src is None else src, dst_ref=dst,
        send_sem=send_sems.at[t, k], recv_sem=recv_sems.at[t, k], device_id=to, device_id_type=MESH)


def _gather_start(tensors, send_sems, recv_sems, local_sems):
    x, y, c, chips = _place()
    me, sib = (x, y, c), (x, y, 1 - c)
    for t, (src, dst) in enumerate(tensors):
        pltpu.make_async_copy(src, dst(_blk(*me)), local_sems.at[t]).start()
        _gather_copy(tensors, send_sems, recv_sems, t, 0, me, sib, src).start()
        for j in (1, 2, 3):
            _gather_copy(tensors, send_sems, recv_sems, t, j, me, (*chips[j], c), src).start()


def _gather_finish(tensors, send_sems, recv_sems, local_sems):
    x, y, c, chips = _place()
    me, sib = (x, y, c), (x, y, 1 - c)
    copy = functools.partial(_gather_copy, tensors, send_sems, recv_sems)
    for t in range(len(tensors)):
        for j in (1, 2, 3):
            copy(t, j, (*chips[j], c), me).wait_recv()
            copy(t, 3 + j, (*chips[j], c), sib).start()
    for t, (src, dst) in enumerate(tensors):
        copy(t, 0, sib, me).wait_recv()
        for j in (1, 2, 3):
            copy(t, 3 + j, (*chips[j], 1 - c), me).wait_recv()
        copy(t, 0, me, sib, src).wait_send()
        for j in (1, 2, 3):
            copy(t, j, me, (*chips[j], c), src).wait_send()
            copy(t, 3 + j, (*chips[j], c), sib).wait_send()
        pltpu.make_async_copy(src, dst(_blk(*me)), local_sems.at[t]).wait()


W_PHASES = WB // TN


def _cast_shards(w_in_s, w_out_s, w_pool_s):
    def body(*refs):
        for src, dst in zip(refs[:3], refs[3:]):
            dst[...] = src[...].astype(BF16)

    vmem = pl.BlockSpec(memory_space=pltpu.VMEM)
    return pl.pallas_call(
        body, name="cast_shards", in_specs=[vmem] * 3, out_specs=[vmem] * 3,
        out_shape=[SDS(w.shape, BF16) for w in (w_in_s, w_out_s, w_pool_s)],
        compiler_params=_params(),
    )(w_in_s, w_out_s, w_pool_s)


def _gather_w_in_rider(shard16, s):
    def tensors(ins, outs):
        return [(ins[0].at[:, pl.ds(s * TN, TN)], lambda b: outs[0].at[b])]

    hbm = pl.BlockSpec(memory_space=pl.ANY)
    return _Rider(
        args=[shard16], in_specs=[hbm], out_shape=[SDS((N_DEV, D, TN), BF16)], out_specs=[hbm],
        scratch=[pltpu.SemaphoreType.DMA((1, 7)), pltpu.SemaphoreType.DMA((1, 7)), pltpu.SemaphoreType.DMA((1,))],
        start=lambda ins, outs, scr: _gather_start(tensors(ins, outs), *scr),
        finish=lambda ins, outs, scr: _gather_finish(tensors(ins, outs), *scr))


def _gather_small_rider(w_out16, w_pool16):
    def tensors(ins, outs):
        gout_ref, gpool_ref = outs

        def pool_rows(b):
            return gpool_ref.at[:, pl.ds(pl.multiple_of(b * PB, PB), PB), :]

        return [(ins[0], lambda b: gout_ref.at[b]), (ins[1], pool_rows)]

    hbm = pl.BlockSpec(memory_space=pl.ANY)
    return _Rider(
        args=[w_out16, w_pool16], in_specs=[hbm, hbm],
        out_shape=[SDS((N_DEV, OB, D), BF16), SDS((N_GROUPS, GC, GC), BF16)], out_specs=[hbm, hbm],
        scratch=[pltpu.SemaphoreType.DMA((2, 7)), pltpu.SemaphoreType.DMA((2, 7)), pltpu.SemaphoreType.DMA((2,))],
        start=lambda ins, outs, scr: _gather_start(tensors(ins, outs), *scr),
        finish=lambda ins, outs, scr: _gather_finish(tensors(ins, outs), *scr))


def _block_table():
    x, y, c, chips = _place()
    return jnp.stack([_blk(*chip, c) for chip in chips]).astype(jnp.int32)


def _sibling_first_order():
    x, y, c, chips = _place()
    return jnp.stack([_blk(*chip, side) for side in (1 - c, c) for chip in chips]).astype(jnp.int32)


def _rs_exchange(name, p16):
    _, r_tot, cols = p16.shape

    def body(p16_ref, r1_ref, send_sems, recv_sems):
        x, y, c, chips = _place()
        copies = [pltpu.make_async_remote_copy(
            src_ref=p16_ref.at[_blk(*chips[k], 1 - c)], dst_ref=r1_ref.at[k],
            send_sem=send_sems.at[k], recv_sem=recv_sems.at[k], device_id=(x, y, 1 - c), device_id_type=MESH)
            for k in range(4)]
        for cp in copies:
            cp.start()
        for cp in copies:
            cp.wait()

    hbm = pl.BlockSpec(memory_space=pl.ANY)
    return pl.pallas_call(
        body, name=name, in_specs=[hbm], out_specs=hbm, out_shape=SDS((4, r_tot, cols), BF16),
        scratch_shapes=[pltpu.SemaphoreType.DMA((4,)), pltpu.SemaphoreType.DMA((4,))],
        compiler_params=_params(),
    )(p16)


def _chip_sums(name, table, p32, r1, rc):
    _, r_tot, cols = p32.shape

    def body(tbl_ref, p_ref, r_ref, o_ref):
        o_ref[...] = (p_ref[...] + r_ref[...].astype(F32)).astype(BF16)

    return pl.pallas_call(
        body, name=name,
        grid_spec=pltpu.PrefetchScalarGridSpec(
            num_scalar_prefetch=1, grid=(3, r_tot // rc),
            in_specs=[pl.BlockSpec((None, rc, cols), lambda k, ch, tbl: (tbl[k + 1], ch, 0)),
                      pl.BlockSpec((None, rc, cols), lambda k, ch, tbl: (k + 1, ch, 0))],
            out_specs=pl.BlockSpec((None, rc, cols), lambda k, ch, tbl: (k, ch, 0))),
        out_shape=SDS((3, r_tot, cols), BF16),
        compiler_params=_params(("arbitrary", "arbitrary")),
    )(table, p32, r1)


def _stage2_rider(sums, stats=None):
    n_t = len(sums)

    def copies(ins, outs, scr):
        x, y, c, chips = _place()
        out = []
        for t in range(n_t):
            for k in (1, 2, 3):
                out.append(pltpu.make_async_remote_copy(
                    src_ref=ins[t].at[k - 1], dst_ref=outs[t].at[k - 1],
                    send_sem=scr[0].at[t, k - 1], recv_sem=scr[1].at[t, k - 1],
                    device_id=(*chips[k], c), device_id_type=MESH))
        if stats is not None:
            for k in range(1, N_DEV):
                peer = (x ^ ((k >> 2) & 1), y ^ ((k >> 1) & 1), c ^ (k & 1))
                out.append(pltpu.make_async_remote_copy(
                    src_ref=scr[4], dst_ref=outs[n_t].at[_blk(x, y, c)],
                    send_sem=scr[2].at[k - 1], recv_sem=scr[3].at[k - 1], device_id=peer, device_id_type=MESH))
        return out

    def own_rows(outs, scr):
        x, y, c, _ = _place()
        return pltpu.make_async_copy(scr[4], outs[n_t].at[_blk(x, y, c)], scr[5])

    def start(ins, outs, scr):
        if stats is not None:
            scr[4][...] = ins[n_t][...]
            scr[4][3:4, 0:DP] = ins[n_t + 1][0:1, :]
            own_rows(outs, scr).start()
        for cp in copies(ins, outs, scr):
            cp.start()

    def finish(ins, outs, scr):
        for cp in copies(ins, outs, scr):
            cp.wait()
        if stats is not None:
            own_rows(outs, scr).wait()

    vmem = pl.BlockSpec(memory_space=pltpu.VMEM)
    hbm = pl.BlockSpec(memory_space=pl.ANY)
    scratch = [pltpu.SemaphoreType.DMA((n_t, 3)), pltpu.SemaphoreType.DMA((n_t, 3))]
    args, in_specs = list(sums), [hbm] * n_t
    out_shape, out_specs = [SDS(s.shape, BF16) for s in sums], [hbm] * n_t
    if stats is not None:
        scratch += [pltpu.SemaphoreType.DMA((N_DEV - 1,)), pltpu.SemaphoreType.DMA((N_DEV - 1,)),
                    pltpu.VMEM((8, D), F32), pltpu.SemaphoreType.DMA(())]
        args, in_specs = args + list(stats), in_specs + [vmem, vmem]
        out_shape, out_specs = out_shape + [SDS((N_DEV, 8, D), F32)], out_specs + [hbm]
    return _Rider(args, in_specs, out_shape, out_specs, scratch, start, finish)


def _adamw_shard(name, table, p32, r1, r2, w, m, v, rc):
    _, r_tot, cols = p32.shape

    def body(tbl_ref, p_ref, r1_ref, r2_ref, w_ref, m_ref, v_ref, g_ref, d_ref, nm_ref, nv_ref):
        g = p_ref[...] + r1_ref[...].astype(F32)
        for k in range(3):
            g = g + r2_ref[k].astype(F32)
        delta, nm, nv = _adamw(w_ref[...], g, m_ref[...], v_ref[...])
        g_ref[...] = g
        d_ref[...] = delta
        nm_ref[...] = nm
        nv_ref[...] = nv

    rows = pl.BlockSpec((rc, cols), lambda ch, tbl: (ch, 0))
    shard = SDS((r_tot, cols), F32)
    return pl.pallas_call(
        body, name=name,
        grid_spec=pltpu.PrefetchScalarGridSpec(
            num_scalar_prefetch=1, grid=(r_tot // rc,),
            in_specs=[pl.BlockSpec((None, rc, cols), lambda ch, tbl: (tbl[0], ch, 0)),
                      pl.BlockSpec((None, rc, cols), lambda ch, tbl: (0, ch, 0)),
                      pl.BlockSpec((3, rc, cols), lambda ch, tbl: (0, ch, 0)), rows, rows, rows],
            out_specs=[rows, rows, rows, rows]),
        out_shape=[shard, shard, shard, shard],
        compiler_params=_params(("arbitrary",)),
    )(table, p32, r1, r2, w, m, v)


def _replicated_adamw(gathered, gain, bias, scale, m_gain, m_bias, m_scale, v_gain, v_bias, v_scale):
    def body(all_ref, s_ref, g_ref, b_ref, ms_ref, mg_ref, mb_ref, vs_ref, vg_ref, vb_ref, loss_ref, *outs):
        tot = all_ref[0]
        for b in range(1, N_DEV):
            tot = tot + all_ref[b]
        loss_ref[...] = tot[2:3, 0:LANES]
        for n, (row, width, w_r, m_r, v_r) in enumerate(((3, DP, s_ref, ms_ref, vs_ref), (0, D, g_ref, mg_ref, vg_ref),
                                                         (1, D, b_ref, mb_ref, vb_ref))):
            g = tot[row:row + 1, 0:width]
            delta, nm, nv = _adamw(w_r[...], g, m_r[...], v_r[...])
            for out, val in zip(outs[4 * n:4 * n + 4], (g, delta, nm, nv)):
                out[...] = val

    vmem = pl.BlockSpec(memory_space=pltpu.VMEM)
    shapes = [SDS((1, LANES), F32)] + [SDS((1, width), F32) for width in (DP, D, D) for _ in range(4)]
    return pl.pallas_call(
        body, name="replicated_adamw",
        in_specs=[vmem] * 10, out_specs=[vmem] * len(shapes), out_shape=shapes,
        compiler_params=_params(),
    )(gathered, scale, gain, bias, m_scale, m_gain, m_bias, v_scale, v_gain, v_bias)


def kernel(x, w_in, w_pool, pool_scale, w_out, ln_gain, ln_bias, loss_target, m_w_in, m_w_pool, m_pool_scale, m_w_out, m_ln_gain, m_ln_bias, v_w_in, v_w_pool, v_pool_scale, v_w_out, v_ln_gain, v_ln_bias):
    pool_rows = (N_GROUPS * PB, GC)
    x2, target = x[0], loss_target[0]
    rope = _rope_tables()
    table = _block_table()

    shard16, w_out16, w_pool16 = _cast_shards(w_in[0], w_out[0], w_pool[0])
    (xb, xt), (wg0,) = _prep_x(x2, _gather_w_in_rider(shard16, 0))
    h, (wg1,) = _proj_phase(xb, wg0, 0, rope, None, _gather_w_in_rider(shard16, 1))
    h, (wg2,) = _proj_phase(xb, wg1, 1, rope, h, _gather_w_in_rider(shard16, 2))
    h, (wg_out, wg_pool) = _proj_phase(xb, wg2, 2, rope, h, _gather_small_rider(w_out16, w_pool16))
    wg_out = wg_out.reshape(D, D)
    h16, h32 = [h[0][None], h[1], h[2]], h[3]
    fwd = [_attn_fwd(h16[n], f"attn_fwd_d{d}") for n, d in enumerate(DILATIONS)]
    y, yt, attn, lse1, lse4, lse16 = _mix(fwd[0][0][0], fwd[0][1][0], *fwd[1], *fwd[2], h32, wg_pool, pool_scale)
    dz, dzb, stats = _outproj_ln(y, wg_out, x2, target, ln_gain, ln_bias)

    dwout, dwout16 = _grad_w_out(yt, dzb)
    r1_out = _rs_exchange("rs_out_exchange", dwout16)
    s2_out = _chip_sums("rs_out_sums", table, dwout, r1_out, OB)
    mid, (r2_out,) = _bwd_mid(dzb, wg_out, h32, attn, wg_pool, pool_scale, rider=_stage2_rider([s2_out]))
    dh, do1, do4, do16, dd1, dd4, dd16, dwp, dwp16, gps = mid
    g_out, d_out, nm_out, nv_out = _adamw_shard(
        "adamw_w_out", table, dwout, r1_out, r2_out, w_out[0], m_w_out[0], v_w_out[0], OB // 2)

    do, lse, dd = [do1[None], do4, do16], [lse1[None], lse4, lse16], [dd1[None], dd4, dd16]
    others = [_attn_bwd(h16[n], do[n], lse[n], dd[n], rope, f"attn_bwd_d{DILATIONS[n]}") for n in (1, 2)]
    dh = _attn_bwd(h16[0], do[0], lse[0], dd[0], rope, "attn_bwd_d1", others=others, dh=dh)[0]

    dwin, r1_in = _grad_w_in(xt, dh, _sibling_first_order())
    s2_in = _chip_sums("rs_in_sums", table, dwin, r1_in, 512)
    r1_pool = _rs_exchange("rs_pool_exchange", dwp16)
    s2_pool = _chip_sums("rs_pool_sums", table, dwp, r1_pool, N_GROUPS * PB)
    grad_x, (r2_in, r2_pool, gathered) = _grad_x(
        dz, dh, (wg0, wg1, wg2), rider=_stage2_rider([s2_in, s2_pool], stats=(stats, gps)))
    g_in, d_in, nm_in, nv_in = _adamw_shard(
        "adamw_w_in", table, dwin, r1_in, r2_in, w_in[0], m_w_in[0], v_w_in[0], 256)
    g_pool, d_pool, nm_pool, nv_pool = _adamw_shard(
        "adamw_w_pool", table, dwp, r1_pool, r2_pool, w_pool[0].reshape(pool_rows), m_w_pool[0].reshape(pool_rows),
        v_w_pool[0].reshape(pool_rows), N_GROUPS * PB)
    loss, *small = _replicated_adamw(gathered, ln_gain, ln_bias, pool_scale, m_ln_gain, m_ln_bias, m_pool_scale,
                                     v_ln_gain, v_ln_bias, v_pool_scale)
    (g_ps, d_ps, nm_ps, nv_ps, g_gain, d_gain, nm_gain, nv_gain, g_bias, d_bias, nm_bias, nv_bias) = small

    shard4 = lambda t: t.reshape(1, N_GROUPS, PB, GC)
    lead = lambda t: t[None]
    return (loss[0, 0], lead(grad_x),
            lead(g_in), shard4(g_pool), g_ps, lead(g_out), g_gain, g_bias,
            lead(d_in), shard4(d_pool), d_ps, lead(d_out), d_gain, d_bias,
            lead(nm_in), shard4(nm_pool), nm_ps, lead(nm_out), nm_gain, nm_bias,
            lead(nv_in), shard4(nv_pool), nv_ps, lead(nv_out), nv_gain, nv_bias)
```

```python
import functools

import jax
import jax.numpy as jnp
from jax import lax
from jax.experimental import pallas as pl
from jax.experimental.pallas import tpu as pltpu

F32 = jnp.float32
BF16 = jnp.bfloat16
SDS = jax.ShapeDtypeStruct
MESH = pl.DeviceIdType.MESH

N_DEV = 8
S = 4096
D = 2048
N_HEADS = 8
DH = 128
DA = N_HEADS * DH
DP = 1024
N_GROUPS = 4
GC = DP // N_GROUPS
POOL_WINDOWS = (2, 4, 8, 16)
HALO = 16
D_IN = 3 * DA + DP + D
WB = D_IN // N_DEV
TN = 256
HW = 3 * DA
OB = D // N_DEV
PB = GC // N_DEV
ROPE_DIM = DH // 4
ROPE_HALF = ROPE_DIM // 2
ROPE_THETA = 500000.0
DILATIONS = (1, 4, 16)
KB = 128
LN_EPS = 1e-5
ALPHA = 2.0 ** 0.25
SCALE = DH ** -0.5
NEG = -1e30
ADAM_LR, ADAM_B1, ADAM_B2, ADAM_EPS, ADAM_WD, ADAM_STEP = 0.001, 0.9, 0.999, 1e-08, 0.01, 10

VMEM_LIMIT_V7X = 61 * 1024 * 1024

NT = (((1,), (1,)), ((), ()))
T_N = (((0,), (0,)), ((), ()))


def _params(sem=None):
    return pltpu.CompilerParams(dimension_semantics=sem, vmem_limit_bytes=VMEM_LIMIT_V7X)


def _dot(a, b, dims=None):
    if dims is None:
        return jnp.dot(a, b, preferred_element_type=F32)
    return lax.dot_general(a, b, dims, preferred_element_type=F32)


def _rope_tables():
    inv_freq = ROPE_THETA ** (-(2.0 * jnp.arange(ROPE_HALF, dtype=F32)) / ROPE_DIM)
    ang = jnp.arange(S, dtype=jnp.int32).astype(F32)[:, None] * inv_freq[None, :]
    cos, sin = jnp.cos(ang), jnp.sin(ang)
    rest = DH - ROPE_DIM
    c = jnp.concatenate([cos, cos, jnp.ones((S, rest), F32)], axis=1)
    sn = jnp.concatenate([-sin, sin, jnp.zeros((S, rest), F32)], axis=1)
    return c, sn


def _rope_partner(t):
    lane = lax.broadcasted_iota(jnp.int32, t.shape, 1)
    return jnp.where(lane < ROPE_HALF, pltpu.roll(t, DH - ROPE_HALF, 1), pltpu.roll(t, ROPE_HALF, 1))


def _rope(t, c, sn):
    return t * c + _rope_partner(t) * sn


def _rope_t(g, c, sn):
    return g * c - _rope_partner(g) * sn


def _prep_x(x, rider):
    tm = 512

    def body(x_ref, xb_ref, xt_ref):
        xv = x_ref[...]
        xb_ref[...] = xv.astype(BF16)
        xt_ref[...] = xv.T.astype(BF16)

    return _call_carrying(
        body, rider, (S // tm,), [pl.BlockSpec((tm, D), lambda i: (i, 0))],
        [pl.BlockSpec((tm, D), lambda i: (i, 0)), pl.BlockSpec((D, tm), lambda i: (0, i))],
        [SDS((S, D), BF16), SDS((D, S), BF16)], [], [x], "prep_x")


def _residues(slab_ref, idx, d, r, n):
    return slab_ref[(*idx, pl.ds(r, n, stride=d), slice(None))]


def _proj_phase(xb, wg, s, rope, prev, rider):
    n_heads = TN // DH
    n16 = HW // TN
    j16 = n16 // W_PHASES
    n_rope = 2 * DA // TN
    once = pl.Buffered(1)

    def body(x_ref, w_ref, c_ref, sn_ref, *rest):
        h16_ref, h4_ref, h16r_ref, h32_ref, slab_ref, res_ref = rest[-6:]
        j = pl.program_id(0)
        d4, d16 = DILATIONS[1], DILATIONS[2]
        acc = _dot(x_ref[...], w_ref[...])

        def heads(with_rope):
            for hh in range(n_heads):
                hs = slice(hh * DH, (hh + 1) * DH)
                t = _rope(acc[:, hs], c_ref[...], sn_ref[...]) if with_rope else acc[:, hs]
                h16_ref[:, hs] = t.astype(BF16)
                slab_ref[...] = t
                for q in range(d4):
                    t4 = _residues(slab_ref, (), d4, q, S // d4)
                    h4_ref[q, :, hs] = t4.astype(BF16)
                    res_ref[...] = t4
                    for m in range(d16 // d4):
                        h16r_ref[d4 * m + q, :, hs] = _residues(res_ref, (), d4, m, S // d16).astype(BF16)

        pl.when(W_PHASES * j + s < n_rope)(lambda: heads(True))
        pl.when((W_PHASES * j + s >= n_rope) & (j < j16))(lambda: heads(False))

        @pl.when(j >= j16)
        def _():
            h32_ref[...] = acc

    col16 = lambda j: W_PHASES * jnp.minimum(j, j16 - 1) + s
    col32 = lambda j: W_PHASES * jnp.maximum(j, j16) + s - n16
    hbm = pl.BlockSpec(memory_space=pl.ANY)
    in_specs = [pl.BlockSpec((S, D), lambda j: (0, 0), pipeline_mode=once),
                pl.BlockSpec((None, D, TN), lambda j: (j, 0, 0))]
    in_specs += [pl.BlockSpec((S, DH), lambda j: (0, 0), pipeline_mode=once)] * 2
    args = [xb, wg] + list(rope)
    aliases = {}
    if prev is not None:
        aliases = {len(args) + n: n for n in range(4)}
        in_specs, args = in_specs + [hbm] * 4, args + list(prev)
    out_specs = [pl.BlockSpec((S, TN), lambda j: (0, col16(j)))]
    out_specs += [pl.BlockSpec((d, S // d, TN), lambda j: (0, 0, col16(j))) for d in DILATIONS[1:]]
    out_specs += [pl.BlockSpec((S, TN), lambda j: (0, col32(j)))]
    out_shape = [SDS((S, HW), BF16)] + [SDS((d, S // d, HW), BF16) for d in DILATIONS[1:]] + [SDS((S, HW), F32)]
    return _call_carrying(
        body, rider, (N_DEV,), in_specs, out_specs, out_shape,
        [pltpu.VMEM((S, DH), F32), pltpu.VMEM((S // DILATIONS[1], DH), F32)], args,
        f"proj_phase{s}", aliases)


BQ = 2 * KB
LANES = 128


def _to_lane(acc, col, h):
    lane = lax.broadcasted_iota(jnp.int32, acc.shape, 1)
    return jnp.where(lane == h, col, acc)


def _attn_fwd(h16, name):
    d, n_sub, _ = h16.shape
    n_i = n_sub // BQ
    kw = KB + BQ

    def prev(i):
        return jnp.maximum(2 * i - 1, 0)

    def body(q_ref, kc_ref, kp_ref, vc_ref, vp_ref, o_ref, l_ref, kw_ref, vw_ref):
        i = pl.program_id(1)
        kw_ref[0:KB, :] = kp_ref[...]
        kw_ref[KB:kw, :] = kc_ref[...]
        vw_ref[0:KB, :] = vp_ref[...]
        vw_ref[KB:kw, :] = vc_ref[...]
        a = lax.broadcasted_iota(jnp.int32, (KB, 2 * KB), 0)
        b = lax.broadcasted_iota(jnp.int32, (KB, 2 * KB), 1)
        band = (b >= a) & (b <= a + KB)
        first_key = jnp.where(i == 0, KB, 0)
        masks = (band & (b >= first_key), band)
        for half in range(2):
            rs = slice(half * KB, (half + 1) * KB)
            ks = slice(half * KB, (half + 2) * KB)
            lse = jnp.zeros((KB, LANES), F32)
            for h in range(N_HEADS):
                hs = slice(h * DH, (h + 1) * DH)
                s = jnp.where(masks[half], _dot(q_ref[rs, hs], kw_ref[ks, hs], NT) * SCALE, NEG)
                m = jnp.max(s, axis=1, keepdims=True)
                p = jnp.exp(s - m)
                den = jnp.sum(p, axis=1, keepdims=True)
                o_ref[rs, hs] = _dot(p.astype(BF16), vw_ref[ks, hs]) / den
                lse = _to_lane(lse, m + jnp.log(den), h)
            l_ref[rs, :] = lse

    return pl.pallas_call(
        body, name=name, grid=(d, n_i),
        in_specs=[pl.BlockSpec((None, BQ, DA), lambda r, i: (r, i, 0)),
                  pl.BlockSpec((None, BQ, DA), lambda r, i: (r, i, 1)),
                  pl.BlockSpec((None, KB, DA), lambda r, i: (r, prev(i), 1)),
                  pl.BlockSpec((None, BQ, DA), lambda r, i: (r, i, 2)),
                  pl.BlockSpec((None, KB, DA), lambda r, i: (r, prev(i), 2))],
        out_specs=[pl.BlockSpec((None, BQ, DA), lambda r, i: (r, i, 0)),
                   pl.BlockSpec((None, BQ, LANES), lambda r, i: (r, i, 0))],
        out_shape=[SDS((d, n_sub, DA), F32), SDS((d, n_sub, LANES), F32)],
        scratch_shapes=[pltpu.VMEM((kw, DA), BF16), pltpu.VMEM((kw, DA), BF16)],
        compiler_params=_params(("arbitrary", "arbitrary")),
    )(h16, h16, h16, h16, h16)


def _pooled(ext_ref, g, rows, tm):
    w = POOL_WINDOWS[g]
    cs = slice(g * GC, (g + 1) * GC)
    cur = ext_ref[HALO:HALO + tm, cs]
    win = cur
    for j in range(1, w):
        win = win + ext_ref[HALO - j:HALO - j + tm, cs]
    cnt = jnp.minimum(rows + 1, w).astype(F32)
    return win / cnt - cur, cnt


def _fill_ext(ext_ref, u_ref, uh_ref, blk, tm):
    @pl.when(blk == 0)
    def _():
        ext_ref[0:HALO, :] = jnp.zeros((HALO, DP), F32)

    @pl.when(blk > 0)
    def _():
        ext_ref[0:HALO, :] = uh_ref[...]

    ext_ref[HALO:HALO + tm, :] = u_ref[...]


def _residue_specs(tm, width):
    return [pl.BlockSpec((d, tm // d, width), lambda i: (0, i, 0)) for d in DILATIONS[1:]]


def _mix(o1, l1, o4, l4, o16, l16, h32, wp, scale):
    tm = 256
    n_slab = N_HEADS + 1

    def body(o1r, l1r, o4r, l4r, o16r, l16r, u_ref, uh_ref, ga_ref, gp_ref, wp_ref, sc_ref,
             y_ref, yt_ref, attn_ref, lse_ref, lse4_ref, lse16_ref, ext_ref, ys_ref, nat_ref, ls_ref):
        i = pl.program_id(0)
        for n, (d, o_r, l_r) in enumerate(((DILATIONS[1], o4r, l4r), (DILATIONS[2], o16r, l16r))):
            for r in range(d):
                rows = pl.ds(r, tm // d, stride=d)
                for h in range(N_HEADS):
                    nat_ref[n, h, rows, :] = o_r[r, :, h * DH:(h + 1) * DH]
                nat_ref[n, N_HEADS, rows, :] = l_r[r]
        la, lb, lc = l1r[...], nat_ref[0, N_HEADS], nat_ref[1, N_HEADS]
        mx = jnp.maximum(jnp.maximum(la, lb), lc)
        ea, eb, ec = jnp.exp(la - mx), jnp.exp(lb - mx), jnp.exp(lc - mx)
        z = ea + eb + ec
        wa, wb, wc = ea / z, eb / z, ec / z
        lse = mx + jnp.log(z)
        lse_ref[...] = lse
        ls_ref[...] = lse
        for d, out in ((DILATIONS[1], lse4_ref), (DILATIONS[2], lse16_ref)):
            for r in range(d):
                out[r] = ls_ref[pl.ds(r, tm // d, stride=d), :]
        for h in range(N_HEADS):
            hs = slice(h * DH, (h + 1) * DH)
            hc = slice(h, h + 1)
            attn = wa[:, hc] * o1r[:, hs] + wb[:, hc] * nat_ref[0, h] + wc[:, hc] * nat_ref[1, h]
            attn_ref[:, hs] = attn
            gt = ga_ref[:, hs]
            ys_ref[:, hs] = attn * (gt * jax.nn.sigmoid(gt))

        _fill_ext(ext_ref, u_ref, uh_ref, i, tm)
        rows = i * tm + lax.broadcasted_iota(jnp.int32, (tm, 1), 0)
        for g in range(N_GROUPS):
            cs = slice(g * GC, (g + 1) * GC)
            gs = slice(DA + g * GC, DA + (g + 1) * GC)
            pooled, _ = _pooled(ext_ref, g, rows, tm)
            po = _dot(pooled.astype(BF16), wp_ref[g]) * sc_ref[:, cs]
            gt = gp_ref[:, cs]
            ys_ref[:, gs] = po * (gt * jax.nn.sigmoid(gt))
        yv = ys_ref[...]
        y_ref[...] = yv.astype(BF16)
        yt_ref[...] = yv.T.astype(BF16)

    row = lambda i: (i, 0)
    blk = pl.BlockSpec((tm, DA), row)
    lanes = pl.BlockSpec((tm, LANES), row)
    o_res, l_res = _residue_specs(tm, DA), _residue_specs(tm, LANES)
    return pl.pallas_call(
        body, name="mix", grid=(S // tm,),
        in_specs=[blk, lanes, o_res[0], l_res[0], o_res[1], l_res[1],
                  pl.BlockSpec((tm, DP), row),
                  pl.BlockSpec((HALO, DP), lambda i: (jnp.maximum(i * (tm // HALO) - 1, 0), 0)),
                  pl.BlockSpec((tm, DA), lambda i: (i, 1)), pl.BlockSpec((tm, DP), lambda i: (i, 2)),
                  pl.BlockSpec((N_GROUPS, GC, GC), lambda i: (0, 0, 0)),
                  pl.BlockSpec((1, DP), lambda i: (0, 0))],
        out_specs=[pl.BlockSpec((tm, D), row), pl.BlockSpec((D, tm), lambda i: (0, i)), blk, lanes] + l_res,
        out_shape=[SDS((S, D), BF16), SDS((D, S), BF16), SDS((S, DA), F32), SDS((S, LANES), F32)]
        + [SDS((d, S // d, LANES), F32) for d in DILATIONS[1:]],
        scratch_shapes=[pltpu.VMEM((HALO + tm, DP), F32), pltpu.VMEM((tm, D), F32),
                        pltpu.VMEM((2, n_slab, tm, DH), F32), pltpu.VMEM((tm, LANES), F32)],
        compiler_params=_params(("arbitrary",)),
    )(o1, l1, o4, l4, o16, l16, h32, h32, h32, h32, wp, scale)


def _outproj_ln(y, wout, x, target, gain, bias):
    tm = 512
    te = 128

    def body(y_ref, w_ref, x_ref, t_ref, g_ref, b_ref, dz_ref, dzb_ref, st_ref, out_ref):
        i = pl.program_id(0)

        @pl.when(i == 0)
        def _():
            st_ref[...] = jnp.zeros((8, D), F32)

        out_ref[...] = _dot(y_ref[...], w_ref[...])
        gn = g_ref[...]
        for e in range(tm // te):
            rs = slice(e * te, (e + 1) * te)
            z = ALPHA * x_ref[rs, :] + out_ref[rs, :]
            mu = jnp.mean(z, axis=1, keepdims=True)
            zc = z - mu
            var = jnp.mean(zc * zc, axis=1, keepdims=True)
            rstd = lax.rsqrt(var + LN_EPS)
            xhat = zc * rstd
            diff = xhat * gn + b_ref[...] - t_ref[rs, :]
            dyln = diff / D
            st_ref[0:1, :] += jnp.sum(dyln * xhat, axis=0, keepdims=True)
            st_ref[1:2, :] += jnp.sum(dyln, axis=0, keepdims=True)
            row_loss = jnp.sum(diff * diff, axis=1, keepdims=True) / D
            st_ref[2:3, :] += jnp.broadcast_to(0.5 * jnp.sum(row_loss, axis=0, keepdims=True), (1, D))
            dxh = dyln * gn
            m1 = jnp.mean(dxh, axis=1, keepdims=True)
            m2 = jnp.mean(dxh * xhat, axis=1, keepdims=True)
            dz = rstd * (dxh - m1 - xhat * m2)
            dz_ref[rs, :] = dz
            dzb_ref[rs, :] = dz.astype(BF16)

    row = lambda i: (i, 0)
    const = lambda i: (0, 0)
    return pl.pallas_call(
        body, name="outproj_ln", grid=(S // tm,),
        in_specs=[pl.BlockSpec((tm, D), row),
                  pl.BlockSpec((D, D), const, pipeline_mode=pl.Buffered(1)),
                  pl.BlockSpec((tm, D), row), pl.BlockSpec((tm, D), row),
                  pl.BlockSpec((1, D), const), pl.BlockSpec((1, D), const)],
        out_specs=[pl.BlockSpec((tm, D), row), pl.BlockSpec((tm, D), row), pl.BlockSpec((8, D), const)],
        out_shape=[SDS((S, D), F32), SDS((S, D), BF16), SDS((8, D), F32)],
        scratch_shapes=[pltpu.VMEM((tm, D), F32)],
        compiler_params=_params(("arbitrary",)),
    )(y, wout, x, target, gain, bias)


def _bwd_mid(dzb, wout, h32, attn, wp, scale, rider=None):
    tm = 256
    n = S // tm

    def body(dz_ref, w_ref, ga_ref, gp_ref, at_ref, u_ref, uh_ref, wp_ref, sc_ref,
             dh_ref, do_ref, do4_ref, do16_ref, dd_ref, dd4_ref, dd16_ref, dwp_ref, dwp16_ref, gps_ref,
             ext_ref, eext_ref, acc_ref, nat_ref, ds_ref):
        i = pl.program_id(0)
        ib = n - 1 - i

        @pl.when(i == 0)
        def _():
            eext_ref[tm:tm + HALO, :] = jnp.zeros((HALO, DP), F32)
            acc_ref[...] = jnp.zeros((N_GROUPS, GC, GC), F32)
            gps_ref[...] = jnp.zeros((8, DP), F32)

        dy = _dot(dz_ref[...], w_ref[...], NT)

        def through_gate(dy_part, gt):
            sg = jax.nn.sigmoid(gt)
            return dy_part * (gt * sg), dy_part * (sg * (1.0 + gt * (1.0 - sg)))

        dat, dyg_a = through_gate(dy[:, 0:DA], ga_ref[...])
        dmix_p, dyg_p = through_gate(dy[:, DA:D], gp_ref[...])

        at = at_ref[...]
        do_ref[...] = dat.astype(BF16)
        dh_ref[:, DP:DP + DA] = (dyg_a * at).astype(BF16)
        prod = dat * at
        dd = jnp.zeros((tm, LANES), F32)
        for h in range(N_HEADS):
            hs = slice(h * DH, (h + 1) * DH)
            dd = _to_lane(dd, jnp.sum(prod[:, hs], axis=1, keepdims=True), h)
            nat_ref[h] = dat[:, hs]
        dd_ref[...] = dd
        ds_ref[...] = dd
        for d, do_out, dd_out in ((DILATIONS[1], do4_ref, dd4_ref), (DILATIONS[2], do16_ref, dd16_ref)):
            for r in range(d):
                dd_out[r] = _residues(ds_ref, (), d, r, tm // d)
                for h in range(N_HEADS):
                    do_out[r, :, h * DH:(h + 1) * DH] = _residues(nat_ref, (h,), d, r, tm // d).astype(BF16)

        _fill_ext(ext_ref, u_ref, uh_ref, ib, tm)
        rows = ib * tm + lax.broadcasted_iota(jnp.int32, (tm, 1), 0)
        for g in range(N_GROUPS):
            w = POOL_WINDOWS[g]
            cs = slice(g * GC, (g + 1) * GC)
            pooled, cnt = _pooled(ext_ref, g, rows, tm)
            pre = _dot(pooled.astype(BF16), wp_ref[g])
            sc = sc_ref[:, cs]
            dpo = dmix_p[:, cs]
            gps_ref[0:1, cs] += jnp.sum(dpo * pre, axis=0, keepdims=True)
            dh_ref[:, DP + DA + g * GC:DP + DA + (g + 1) * GC] = (dyg_p[:, cs] * (pre * sc)).astype(BF16)
            dpre = (dpo * sc).astype(BF16)
            acc_ref[g] += _dot(pooled.T.astype(BF16), dpre)
            dpooled = _dot(dpre, wp_ref[g], NT)
            eext_ref[0:tm, cs] = dpooled / cnt
            du = eext_ref[0:tm, cs]
            for j in range(1, w):
                du = du + eext_ref[j:j + tm, cs]
            dh_ref[:, cs] = (du - dpooled).astype(BF16)
        eext_ref[tm:tm + HALO, :] = eext_ref[0:HALO, :]

        @pl.when(i == n - 1)
        def _():
            for j in range(N_DEV):
                for g in range(N_GROUPS):
                    blk = acc_ref[g, j * PB:(j + 1) * PB, :]
                    dwp_ref[j, g * PB:(g + 1) * PB, :] = blk
                    dwp16_ref[j, g * PB:(g + 1) * PB, :] = blk.astype(BF16)

    rev = lambda i: (n - 1 - i, 0)
    const = lambda i: (0, 0)
    res = lambda width: [pl.BlockSpec((d, tm // d, width), lambda i: (0, n - 1 - i, 0)) for d in DILATIONS[1:]]
    pool_blocks = pl.BlockSpec((N_DEV, N_GROUPS * PB, GC), lambda i: (0, 0, 0))
    outs, carried = _call_carrying(
        body, rider, (n,),
        [pl.BlockSpec((tm, D), rev),
         pl.BlockSpec((D, D), const, pipeline_mode=pl.Buffered(1)),
         pl.BlockSpec((tm, DA), lambda i: (n - 1 - i, 1)), pl.BlockSpec((tm, DP), lambda i: (n - 1 - i, 2)),
         pl.BlockSpec((tm, DA), rev), pl.BlockSpec((tm, DP), rev),
         pl.BlockSpec((HALO, DP), lambda i: (jnp.maximum((n - 1 - i) * (tm // HALO) - 1, 0), 0)),
         pl.BlockSpec((N_GROUPS, GC, GC), lambda i: (0, 0, 0)),
         pl.BlockSpec((1, DP), const)],
        [pl.BlockSpec((tm, DP + D), lambda i: (n - 1 - i, 1)), pl.BlockSpec((tm, DA), rev)] + res(DA)
        + [pl.BlockSpec((tm, LANES), rev)] + res(LANES) + [pool_blocks, pool_blocks, pl.BlockSpec((8, DP), const)],
        [SDS((S, D_IN), BF16), SDS((S, DA), BF16)] + [SDS((d, S // d, DA), BF16) for d in DILATIONS[1:]]
        + [SDS((S, LANES), F32)] + [SDS((d, S // d, LANES), F32) for d in DILATIONS[1:]]
        + [SDS((N_DEV, N_GROUPS * PB, GC), F32), SDS((N_DEV, N_GROUPS * PB, GC), BF16), SDS((8, DP), F32)],
        [pltpu.VMEM((HALO + tm, DP), F32), pltpu.VMEM((tm + HALO, DP), F32),
         pltpu.VMEM((N_GROUPS, GC, GC), F32), pltpu.VMEM((N_HEADS, tm, DH), F32), pltpu.VMEM((tm, LANES), F32)],
        [dzb, wout, h32, h32, attn, h32, h32, wp, scale], "bwd_mid")
    return outs, carried


def _attn_bwd(h16, do, lse, dd, rope, name, others=None, dh=None, rider=None):
    d, n_sub, _ = h16.shape
    n_i = n_sub // BQ
    n_kb = n_sub // KB
    qw = BQ + KB
    final = others is not None
    out_dtype = BF16 if final else F32
    n_cb = 3 * DA // DH

    def nxt(i):
        return jnp.minimum(2 * i + 2, n_kb - 1)

    def body(qc_ref, qn_ref, kc_ref, vc_ref, doc_ref, don_ref, lc_ref, ln_ref, dc_ref, dn_ref,
             c_ref, sn_ref, *rest):
        if final:
            acc4_ref, acc16_ref, _, out_ref, carry_ref, qw_ref, dow_ref, lw_ref, dw_ref, nat_ref = rest
        else:
            out_ref, carry_ref, qw_ref, dow_ref, lw_ref, dw_ref = rest
        i = pl.program_id(1)

        @pl.when(i == 0)
        def _():
            carry_ref[...] = jnp.zeros((KB, DA), F32)

        for win, own, after in ((qw_ref, qc_ref, qn_ref), (dow_ref, doc_ref, don_ref), (lw_ref, lc_ref, ln_ref),
                                (dw_ref, dc_ref, dn_ref)):
            win[0:BQ, :] = own[...]
            win[BQ:qw, :] = after[...]
        if final:
            for n, (dil, acc) in enumerate(((DILATIONS[1], acc4_ref), (DILATIONS[2], acc16_ref))):
                for r in range(dil):
                    for cb in range(n_cb):
                        nat_ref[n, cb, pl.ds(r, BQ // dil, stride=dil), :] = acc[r, :, cb * DH:(cb + 1) * DH]

        a = lax.broadcasted_iota(jnp.int32, (qw, BQ), 0)
        b = lax.broadcasted_iota(jnp.int32, (qw, BQ), 1)
        n_q = jnp.where(i == n_i - 1, BQ, qw)
        mask = (b <= a) & (a <= b + KB) & (a < n_q)
        tok = pl.ds(pl.program_id(0) + d * BQ * i, BQ, stride=d)
        tabs = (c_ref[tok, :], sn_ref[tok, :])
        for h in range(N_HEADS):
            hs = slice(h * DH, (h + 1) * DH)
            hc = slice(h, h + 1)
            q, k, vv, dob = qw_ref[:, hs], kc_ref[:, hs], vc_ref[:, hs], dow_ref[:, hs]
            s = _dot(q, k, NT) * SCALE
            p = jnp.exp(jnp.where(mask, s - lw_ref[:, hc], NEG))
            dp = _dot(dob, vv, NT)
            ds = (p * (dp - dw_ref[:, hc]) * SCALE).astype(BF16)
            dq = _dot(ds, k)
            dk = _dot(ds, q, T_N)
            dv = _dot(p.astype(BF16), dob, T_N)
            dq_lo = dq[0:KB] + carry_ref[:, hs]
            carry_ref[:, hs] = dq[BQ:qw]
            dq_own = _rope_t(jnp.concatenate([dq_lo, dq[KB:BQ]], axis=0), *tabs)
            for base, gv in ((0, dq_own), (DA, _rope_t(dk, *tabs)), (2 * DA, dv)):
                if final:
                    cb = base // DH + h
                    gv = gv + nat_ref[0, cb] + nat_ref[1, cb]
                out_ref[:, base + h * DH:base + (h + 1) * DH] = gv.astype(out_dtype)

    def cur(rows, width, col=0):
        return pl.BlockSpec((None, rows, width), lambda r, i: (r, i, col))

    def nx(width, col=0):
        return pl.BlockSpec((None, KB, width), lambda r, i: (r, nxt(i), col))

    in_specs = [cur(BQ, DA), nx(DA), cur(BQ, DA, 1), cur(BQ, DA, 2), cur(BQ, DA), nx(DA),
                cur(BQ, LANES), nx(LANES), cur(BQ, LANES), nx(LANES)]
    in_specs += [pl.BlockSpec((S, DH), lambda r, i: (0, 0), pipeline_mode=pl.Buffered(1))] * 2
    args = [h16, h16, h16, h16, do, do, lse, lse, dd, dd] + list(rope)
    scratch = [pltpu.VMEM((KB, DA), F32), pltpu.VMEM((qw, DA), BF16), pltpu.VMEM((qw, DA), BF16),
               pltpu.VMEM((qw, LANES), F32), pltpu.VMEM((qw, LANES), F32)]
    if final:
        assert d == 1
        in_specs += [pl.BlockSpec((dil, BQ // dil, 3 * DA), lambda r, i: (0, i, 0)) for dil in DILATIONS[1:]]
        in_specs.append(pl.BlockSpec(memory_space=pl.ANY))
        args += list(others) + [dh[None]]
        scratch.append(pltpu.VMEM((2, n_cb, BQ, DH), F32))
    outs, carried = _call_carrying(
        body, rider, (d, n_i), in_specs, [cur(BQ, 3 * DA)], [SDS((d, n_sub, D_IN if final else 3 * DA), out_dtype)],
        scratch, args, name, {len(args) - 1: 0} if final else {})
    return outs[0] if rider is None else (outs[0], carried)


def _grad_w_out(yt, dzb):
    tn = 256

    def body(yt_ref, dz_ref, o_ref, o16_ref):
        acc = _dot(yt_ref[...], dz_ref[...])
        o_ref[...] = acc
        o16_ref[...] = acc.astype(BF16)

    cols = pl.BlockSpec((D, tn), lambda c: (0, c))
    o, o16 = pl.pallas_call(
        body, name="grad_w_out", grid=(D // tn,),
        in_specs=[pl.BlockSpec((D, S), lambda c: (0, 0), pipeline_mode=pl.Buffered(1)),
                  pl.BlockSpec((S, tn), lambda c: (0, c))],
        out_specs=[cols, cols],
        out_shape=[SDS((D, D), F32), SDS((D, D), BF16)],
        compiler_params=_params(("arbitrary",)),
    )(yt, dzb)
    return o.reshape(N_DEV, OB, D), o16.reshape(N_DEV, OB, D)


def _grad_w_in(xt, dh, order):
    n_send = N_DEV // 2
    n_steps = D_IN // TN
    own0 = n_send * W_PHASES
    sums0 = own0 + W_PHASES
    step_of_send = [W_PHASES * k + W_PHASES + 1 for k in range(n_send)]

    def body(ord_ref, xt_ref, dh_ref, o_ref, o16_ref, r1_ref, s2_ref, st_ref, got_ref,
             piece_sems, send_sems, recv_sems, got_sem):
        c = pl.program_id(0)
        x, y, core, _ = _place()

        def piece(cc):
            blk, sub = ord_ref[cc // W_PHASES], cc % W_PHASES
            return pltpu.make_async_copy(st_ref.at[cc % 2], o16_ref.at[blk, :, pl.ds(sub * TN, TN)], piece_sems.at[cc % 2])

        def send(k):
            return pltpu.make_async_remote_copy(
                src_ref=o16_ref.at[ord_ref[k]], dst_ref=r1_ref.at[k], send_sem=send_sems.at[k], recv_sem=recv_sems.at[k],
                device_id=(x, y, 1 - core), device_id_type=MESH)

        def sibling_part(cc):
            k, sub = cc // W_PHASES - n_send, cc % W_PHASES
            return pltpu.make_async_copy(r1_ref.at[k, :, pl.ds(sub * TN, TN)], got_ref, got_sem)

        @pl.when((c >= 2) & (c < own0 + 2))
        def _():
            piece(c - 2).wait()

        for k in range(n_send):
            pl.when(c == step_of_send[k])(lambda k=k: send(k).start())
        for k in range(1, n_send):
            pl.when(c == own0 + W_PHASES * k)(lambda k=k: send(k).wait_recv())
        pl.when(c >= sums0)(lambda: sibling_part(c).start())

        acc = _dot(xt_ref[...], dh_ref[...])
        o_ref[...] = acc

        @pl.when(c < own0)
        def _():
            st_ref[c % 2] = acc.astype(BF16)
            piece(c).start()

        @pl.when(c >= sums0)
        def _():
            sibling_part(c).wait()
            s2_ref[...] = (acc + got_ref[...].astype(F32)).astype(BF16)

        @pl.when(c == n_steps - 1)
        def _():
            send(0).wait_recv()
            for k in range(n_send):
                send(k).wait_send()

    col = lambda c, ordr: W_PHASES * ordr[c // W_PHASES] + c % W_PHASES
    hbm = pl.BlockSpec(memory_space=pl.ANY)
    sums_blk = lambda c, ordr: (jnp.maximum(c // W_PHASES - (n_send + 1), 0), 0, jnp.where(c >= sums0, c % W_PHASES, 0))
    partial, _, got, sums = pl.pallas_call(
        body, name="grad_w_in",
        grid_spec=pltpu.PrefetchScalarGridSpec(
            num_scalar_prefetch=1, grid=(n_steps,),
            in_specs=[pl.BlockSpec((D, S), lambda c, ordr: (0, 0), pipeline_mode=pl.Buffered(1)),
                      pl.BlockSpec((S, TN), lambda c, ordr: (0, col(c, ordr)))],
            out_specs=[pl.BlockSpec((None, D, TN), lambda c, ordr: (ordr[c // W_PHASES], 0, c % W_PHASES)), hbm, hbm,
                       pl.BlockSpec((None, D, TN), sums_blk)],
            scratch_shapes=[pltpu.VMEM((2, D, TN), BF16), pltpu.VMEM((D, TN), BF16), pltpu.SemaphoreType.DMA((2,)),
                            pltpu.SemaphoreType.DMA((n_send,)), pltpu.SemaphoreType.DMA((n_send,)),
                            pltpu.SemaphoreType.DMA(())]),
        out_shape=[SDS((N_DEV, D, WB), F32), SDS((N_DEV, D, WB), BF16), SDS((n_send, D, WB), BF16),
                   SDS((n_send - 1, D, WB), BF16)],
        compiler_params=_params(("arbitrary",)),
    )(order, xt, dh)
    return partial, got, sums


def _grad_x(dz, dh, wgs, rider=None):
    tm = 1024

    def body(dz_ref, dh_ref, *rest):
        o_ref = rest[-1]

        @pl.when(pl.program_id(1) == 0)
        def _():
            o_ref[...] = ALPHA * dz_ref[...]

        acc = _dot(dh_ref[:, 0:TN], rest[0][...], NT)
        for s in range(1, W_PHASES):
            acc = acc + _dot(dh_ref[:, s * TN:(s + 1) * TN], rest[s][...], NT)
        o_ref[...] += acc

    outs, carried = _call_carrying(
        body, rider, (S // tm, N_DEV),
        [pl.BlockSpec((tm, D), lambda i, j: (i, 0)), pl.BlockSpec((tm, WB), lambda i, j: (i, j))]
        + [pl.BlockSpec((None, D, TN), lambda i, j: (j, 0, 0))] * W_PHASES,
        [pl.BlockSpec((tm, D), lambda i, j: (i, 0))], [SDS((S, D), F32)], [], [dz, dh] + list(wgs), "grad_x")
    return outs[0], carried


def _place():
    x, y, c = lax.axis_index("x"), lax.axis_index("y"), lax.axis_index("c")
    chips = [(x, y), (1 - x, y), (x, 1 - y), (1 - x, 1 - y)]
    return x, y, c, chips


def _blk(x, y, c):
    return 4 * x + 2 * y + c


def _adamw(w, g, m, v):
    m = ADAM_B1 * m + (1.0 - ADAM_B1) * g
    v = ADAM_B2 * v + (1.0 - ADAM_B2) * (g * g)
    m_hat = m / (1.0 - ADAM_B1 ** ADAM_STEP)
    v_hat = v / (1.0 - ADAM_B2 ** ADAM_STEP)
    delta = -ADAM_LR * (m_hat / (jnp.sqrt(v_hat) + ADAM_EPS) + ADAM_WD * w)
    return delta, m, v


class _Rider:
    def __init__(self, args, in_specs, out_shape, out_specs, scratch, start, finish):
        self.args, self.in_specs, self.out_shape, self.out_specs = args, in_specs, out_shape, out_specs
        self.scratch, self.start, self.finish = scratch, start, finish


def _carry(body, rider, n_in, n_out, first, last):
    if rider is None:
        return body
    r_in, r_out, r_scr = len(rider.args), len(rider.out_shape), len(rider.scratch)

    def carrying(*refs):
        o0 = n_in + r_in
        s0 = o0 + n_out + r_out
        s1 = len(refs) - r_scr
        theirs = (refs[n_in:o0], refs[o0 + n_out:s0], refs[s1:])
        pl.when(first())(lambda: rider.start(*theirs))
        body(*refs[:n_in], *refs[o0:o0 + n_out], *refs[s0:s1])
        pl.when(last())(lambda: rider.finish(*theirs))

    return carrying


def _call_carrying(body, rider, grid, in_specs, out_specs, out_shape, scratch, args, name, aliases=None):
    n_in, n_out = len(in_specs), len(out_specs)
    ids = lambda: [pl.program_id(a) for a in range(len(grid))]
    first = lambda: functools.reduce(jnp.logical_and, [i == 0 for i in ids()])
    last = lambda: functools.reduce(jnp.logical_and, [i == n - 1 for i, n in zip(ids(), grid)])
    if rider is not None:
        in_specs, args = in_specs + rider.in_specs, list(args) + rider.args
        out_specs, out_shape = out_specs + rider.out_specs, out_shape + rider.out_shape
        scratch = scratch + rider.scratch
    outs = pl.pallas_call(
        _carry(body, rider, n_in, n_out, first, last), name=name, grid=grid,
        in_specs=in_specs, out_specs=out_specs, out_shape=out_shape, scratch_shapes=scratch,
        input_output_aliases=aliases or {}, compiler_params=_params(("arbitrary",) * len(grid)),
    )(*args)
    return list(outs[:n_out]), list(outs[n_out:])


def _gather_copy(tensors, send_sems, recv_sems, t, k, block, to, src=None):
    dst = tensors[t][1](_blk(*block))
    return pltpu.make_async_remote_copy(
        src_ref=dst if src is None else src, dst_ref=dst,
        send_sem=send_sems.at[t, k], recv_sem=recv_sems.at[t, k], device_id=to, device_id_type=MESH)


def _gather_start(tensors, send_sems, recv_sems, local_sems):
    x, y, c, chips = _place()
    me, sib = (x, y, c), (x, y, 1 - c)
    for t, (src, dst) in enumerate(tensors):
        pltpu.make_async_copy(src, dst(_blk(*me)), local_sems.at[t]).start()
        _gather_copy(tensors, send_sems, recv_sems, t, 0, me, sib, src).start()
        for j in (1, 2, 3):
            _gather_copy(tensors, send_sems, recv_sems, t, j, me, (*chips[j], c), src).start()


def _gather_finish(tensors, send_sems, recv_sems, local_sems):
    x, y, c, chips = _place()
    me, sib = (x, y, c), (x, y, 1 - c)
    copy = functools.partial(_gather_copy, tensors, send_sems, recv_sems)
    for t in range(len(tensors)):
        for j in (1, 2, 3):
            copy(t, j, (*chips[j], c), me).wait_recv()
            copy(t, 3 + j, (*chips[j], c), sib).start()
    for t, (src, dst) in enumerate(tensors):
        copy(t, 0, sib, me).wait_recv()
        for j in (1, 2, 3):
            copy(t, 3 + j, (*chips[j], 1 - c), me).wait_recv()
        copy(t, 0, me, sib, src).wait_send()
        for j in (1, 2, 3):
            copy(t, j, me, (*chips[j], c), src).wait_send()
            copy(t, 3 + j, (*chips[j], c), sib).wait_send()
        pltpu.make_async_copy(src, dst(_blk(*me)), local_sems.at[t]).wait()


W_PHASES = WB // TN


def _cast_shards(w_in_s, w_out_s, w_pool_s):
    def body(*refs):
        for src, dst in zip(refs[:3], refs[3:]):
            dst[...] = src[...].astype(BF16)

    vmem = pl.BlockSpec(memory_space=pltpu.VMEM)
    return pl.pallas_call(
        body, name="cast_shards", in_specs=[vmem] * 3, out_specs=[vmem] * 3,
        out_shape=[SDS(w.shape, BF16) for w in (w_in_s, w_out_s, w_pool_s)],
        compiler_params=_params(),
    )(w_in_s, w_out_s, w_pool_s)


def _gather_w_in_rider(shard16, s):
    def tensors(ins, outs):
        return [(ins[0].at[:, pl.ds(s * TN, TN)], lambda b: outs[0].at[b])]

    hbm = pl.BlockSpec(memory_space=pl.ANY)
    return _Rider(
        args=[shard16], in_specs=[hbm], out_shape=[SDS((N_DEV, D, TN), BF16)], out_specs=[hbm],
        scratch=[pltpu.SemaphoreType.DMA((1, 7)), pltpu.SemaphoreType.DMA((1, 7)), pltpu.SemaphoreType.DMA((1,))],
        start=lambda ins, outs, scr: _gather_start(tensors(ins, outs), *scr),
        finish=lambda ins, outs, scr: _gather_finish(tensors(ins, outs), *scr))


def _gather_small_rider(w_out16, w_pool16):
    def tensors(ins, outs):
        gout_ref, gpool_ref = outs

        def pool_rows(b):
            return gpool_ref.at[:, pl.ds(pl.multiple_of(b * PB, PB), PB), :]

        return [(ins[0], lambda b: gout_ref.at[b]), (ins[1], pool_rows)]

    hbm = pl.BlockSpec(memory_space=pl.ANY)
    return _Rider(
        args=[w_out16, w_pool16], in_specs=[hbm, hbm],
        out_shape=[SDS((N_DEV, OB, D), BF16), SDS((N_GROUPS, GC, GC), BF16)], out_specs=[hbm, hbm],
        scratch=[pltpu.SemaphoreType.DMA((2, 7)), pltpu.SemaphoreType.DMA((2, 7)), pltpu.SemaphoreType.DMA((2,))],
        start=lambda ins, outs, scr: _gather_start(tensors(ins, outs), *scr),
        finish=lambda ins, outs, scr: _gather_finish(tensors(ins, outs), *scr))


def _block_table():
    x, y, c, chips = _place()
    return jnp.stack([_blk(*chip, c) for chip in chips]).astype(jnp.int32)


def _sibling_first_order():
    x, y, c, chips = _place()
    return jnp.stack([_blk(*chip, side) for side in (1 - c, c) for chip in chips]).astype(jnp.int32)


def _exchange_rider(p16):
    _, r_tot, cols = p16.shape

    def copies(ins, outs, scr):
        x, y, c, chips = _place()
        return [pltpu.make_async_remote_copy(
            src_ref=ins[0].at[_blk(*chips[k], 1 - c)], dst_ref=outs[0].at[k],
            send_sem=scr[0].at[k], recv_sem=scr[1].at[k], device_id=(x, y, 1 - c), device_id_type=MESH)
            for k in range(4)]

    def start(ins, outs, scr):
        for cp in copies(ins, outs, scr):
            cp.start()

    def finish(ins, outs, scr):
        for cp in copies(ins, outs, scr):
            cp.wait()

    hbm = pl.BlockSpec(memory_space=pl.ANY)
    return _Rider([p16], [hbm], [SDS((4, r_tot, cols), BF16)], [hbm],
                  [pltpu.SemaphoreType.DMA((4,)), pltpu.SemaphoreType.DMA((4,))], start, finish)


def _chip_sums(name, table, p32, r1, rc):
    _, r_tot, cols = p32.shape

    def body(tbl_ref, p_ref, r_ref, o_ref):
        o_ref[...] = (p_ref[...] + r_ref[...].astype(F32)).astype(BF16)

    return pl.pallas_call(
        body, name=name,
        grid_spec=pltpu.PrefetchScalarGridSpec(
            num_scalar_prefetch=1, grid=(3, r_tot // rc),
            in_specs=[pl.BlockSpec((None, rc, cols), lambda k, ch, tbl: (tbl[k + 1], ch, 0)),
                      pl.BlockSpec((None, rc, cols), lambda k, ch, tbl: (k + 1, ch, 0))],
            out_specs=pl.BlockSpec((None, rc, cols), lambda k, ch, tbl: (k, ch, 0))),
        out_shape=SDS((3, r_tot, cols), BF16),
        compiler_params=_params(("arbitrary", "arbitrary")),
    )(table, p32, r1)


def _stage2_rider(sums, stats=None):
    n_t = len(sums)

    def copies(ins, outs, scr):
        x, y, c, chips = _place()
        out = []
        for t in range(n_t):
            for k in (1, 2, 3):
                out.append(pltpu.make_async_remote_copy(
                    src_ref=ins[t].at[k - 1], dst_ref=outs[t].at[k - 1],
                    send_sem=scr[0].at[t, k - 1], recv_sem=scr[1].at[t, k - 1],
                    device_id=(*chips[k], c), device_id_type=MESH))
        if stats is not None:
            for k in range(1, N_DEV):
                peer = (x ^ ((k >> 2) & 1), y ^ ((k >> 1) & 1), c ^ (k & 1))
                out.append(pltpu.make_async_remote_copy(
                    src_ref=scr[4], dst_ref=outs[n_t].at[_blk(x, y, c)],
                    send_sem=scr[2].at[k - 1], recv_sem=scr[3].at[k - 1], device_id=peer, device_id_type=MESH))
        return out

    def own_rows(outs, scr):
        x, y, c, _ = _place()
        return pltpu.make_async_copy(scr[4], outs[n_t].at[_blk(x, y, c)], scr[5])

    def start(ins, outs, scr):
        if stats is not None:
            scr[4][...] = ins[n_t][...]
            scr[4][3:4, 0:DP] = ins[n_t + 1][0:1, :]
            own_rows(outs, scr).start()
        for cp in copies(ins, outs, scr):
            cp.start()

    def finish(ins, outs, scr):
        for cp in copies(ins, outs, scr):
            cp.wait()
        if stats is not None:
            own_rows(outs, scr).wait()

    vmem = pl.BlockSpec(memory_space=pltpu.VMEM)
    hbm = pl.BlockSpec(memory_space=pl.ANY)
    scratch = [pltpu.SemaphoreType.DMA((n_t, 3)), pltpu.SemaphoreType.DMA((n_t, 3))]
    args, in_specs = list(sums), [hbm] * n_t
    out_shape, out_specs = [SDS(s.shape, BF16) for s in sums], [hbm] * n_t
    if stats is not None:
        scratch += [pltpu.SemaphoreType.DMA((N_DEV - 1,)), pltpu.SemaphoreType.DMA((N_DEV - 1,)),
                    pltpu.VMEM((8, D), F32), pltpu.SemaphoreType.DMA(())]
        args, in_specs = args + list(stats), in_specs + [vmem, vmem]
        out_shape, out_specs = out_shape + [SDS((N_DEV, 8, D), F32)], out_specs + [hbm]
    return _Rider(args, in_specs, out_shape, out_specs, scratch, start, finish)


def _adamw_shard(name, table, p32, r1, r2, w, m, v, rc):
    _, r_tot, cols = p32.shape

    def body(tbl_ref, p_ref, r1_ref, r2_ref, w_ref, m_ref, v_ref, g_ref, d_ref, nm_ref, nv_ref):
        g = p_ref[...] + r1_ref[...].astype(F32)
        for k in range(3):
            g = g + r2_ref[k].astype(F32)
        delta, nm, nv = _adamw(w_ref[...], g, m_ref[...], v_ref[...])
        g_ref[...] = g
        d_ref[...] = delta
        nm_ref[...] = nm
        nv_ref[...] = nv

    rows = pl.BlockSpec((rc, cols), lambda ch, tbl: (ch, 0))
    shard = SDS((r_tot, cols), F32)
    return pl.pallas_call(
        body, name=name,
        grid_spec=pltpu.PrefetchScalarGridSpec(
            num_scalar_prefetch=1, grid=(r_tot // rc,),
            in_specs=[pl.BlockSpec((None, rc, cols), lambda ch, tbl: (tbl[0], ch, 0)),
                      pl.BlockSpec((None, rc, cols), lambda ch, tbl: (0, ch, 0)),
                      pl.BlockSpec((3, rc, cols), lambda ch, tbl: (0, ch, 0)), rows, rows, rows],
            out_specs=[rows, rows, rows, rows]),
        out_shape=[shard, shard, shard, shard],
        compiler_params=_params(("arbitrary",)),
    )(table, p32, r1, r2, w, m, v)


def _replicated_adamw(gathered, gain, bias, scale, m_gain, m_bias, m_scale, v_gain, v_bias, v_scale):
    def body(all_ref, s_ref, g_ref, b_ref, ms_ref, mg_ref, mb_ref, vs_ref, vg_ref, vb_ref, loss_ref, *outs):
        tot = all_ref[0]
        for b in range(1, N_DEV):
            tot = tot + all_ref[b]
        loss_ref[...] = tot[2:3, 0:LANES]
        for n, (row, width, w_r, m_r, v_r) in enumerate(((3, DP, s_ref, ms_ref, vs_ref), (0, D, g_ref, mg_ref, vg_ref),
                                                         (1, D, b_ref, mb_ref, vb_ref))):
            g = tot[row:row + 1, 0:width]
            delta, nm, nv = _adamw(w_r[...], g, m_r[...], v_r[...])
            for out, val in zip(outs[4 * n:4 * n + 4], (g, delta, nm, nv)):
                out[...] = val

    vmem = pl.BlockSpec(memory_space=pltpu.VMEM)
    shapes = [SDS((1, LANES), F32)] + [SDS((1, width), F32) for width in (DP, D, D) for _ in range(4)]
    return pl.pallas_call(
        body, name="replicated_adamw",
        in_specs=[vmem] * 10, out_specs=[vmem] * len(shapes), out_shape=shapes,
        compiler_params=_params(),
    )(gathered, scale, gain, bias, m_scale, m_gain, m_bias, v_scale, v_gain, v_bias)


def kernel(x, w_in, w_pool, pool_scale, w_out, ln_gain, ln_bias, loss_target, m_w_in, m_w_pool, m_pool_scale, m_w_out, m_ln_gain, m_ln_bias, v_w_in, v_w_pool, v_pool_scale, v_w_out, v_ln_gain, v_ln_bias):
    pool_rows = (N_GROUPS * PB, GC)
    x2, target = x[0], loss_target[0]
    rope = _rope_tables()
    table = _block_table()

    shard16, w_out16, w_pool16 = _cast_shards(w_in[0], w_out[0], w_pool[0])
    (xb, xt), (wg0,) = _prep_x(x2, _gather_w_in_rider(shard16, 0))
    h, (wg1,) = _proj_phase(xb, wg0, 0, rope, None, _gather_w_in_rider(shard16, 1))
    h, (wg2,) = _proj_phase(xb, wg1, 1, rope, h, _gather_w_in_rider(shard16, 2))
    h, (wg_out, wg_pool) = _proj_phase(xb, wg2, 2, rope, h, _gather_small_rider(w_out16, w_pool16))
    wg_out = wg_out.reshape(D, D)
    h16, h32 = [h[0][None], h[1], h[2]], h[3]
    fwd = [_attn_fwd(h16[n], f"attn_fwd_d{d}") for n, d in enumerate(DILATIONS)]
    y, yt, attn, lse1, lse4, lse16 = _mix(fwd[0][0][0], fwd[0][1][0], *fwd[1], *fwd[2], h32, wg_pool, pool_scale)
    dz, dzb, stats = _outproj_ln(y, wg_out, x2, target, ln_gain, ln_bias)

    dwout, dwout16 = _grad_w_out(yt, dzb)
    mid, (r1_out,) = _bwd_mid(dzb, wg_out, h32, attn, wg_pool, pool_scale, rider=_exchange_rider(dwout16))
    dh, do1, do4, do16, dd1, dd4, dd16, dwp, dwp16, gps = mid
    s2_out = _chip_sums("rs_out_sums", table, dwout, r1_out, OB)

    do, lse, dd = [do1[None], do4, do16], [lse1[None], lse4, lse16], [dd1[None], dd4, dd16]
    acc4, (r2_out,) = _attn_bwd(h16[1], do[1], lse[1], dd[1], rope, "attn_bwd_d4", rider=_stage2_rider([s2_out]))
    g_out, d_out, nm_out, nv_out = _adamw_shard(
        "adamw_w_out", table, dwout, r1_out, r2_out, w_out[0], m_w_out[0], v_w_out[0], OB // 2)
    acc16, (r1_pool,) = _attn_bwd(h16[2], do[2], lse[2], dd[2], rope, "attn_bwd_d16", rider=_exchange_rider(dwp16))
    s2_pool = _chip_sums("rs_pool_sums", table, dwp, r1_pool, N_GROUPS * PB)
    dh = _attn_bwd(h16[0], do[0], lse[0], dd[0], rope, "attn_bwd_d1", others=(acc4, acc16), dh=dh)[0]

    dwin, r1_in, s2_in = _grad_w_in(xt, dh, _sibling_first_order())
    grad_x, (r2_in, r2_pool, gathered) = _grad_x(
        dz, dh, (wg0, wg1, wg2), rider=_stage2_rider([s2_in, s2_pool], stats=(stats, gps)))
    g_in, d_in, nm_in, nv_in = _adamw_shard(
        "adamw_w_in", table, dwin, r1_in, r2_in, w_in[0], m_w_in[0], v_w_in[0], 256)
    g_pool, d_pool, nm_pool, nv_pool = _adamw_shard(
        "adamw_w_pool", table, dwp, r1_pool, r2_pool, w_pool[0].reshape(pool_rows), m_w_pool[0].reshape(pool_rows),
        v_w_pool[0].reshape(pool_rows), N_GROUPS * PB)
    loss, *small = _replicated_adamw(gathered, ln_gain, ln_bias, pool_scale, m_ln_gain, m_ln_bias, m_pool_scale,
                                     v_ln_gain, v_ln_bias, v_pool_scale)
    (g_ps, d_ps, nm_ps, nv_ps, g_gain, d_gain, nm_gain, nv_gain, g_bias, d_bias, nm_bias, nv_bias) = small

    shard4 = lambda t: t.reshape(1, N_GROUPS, PB, GC)
    lead = lambda t: t[None]
    return (loss[0, 0], lead(grad_x),
            lead(g_in), shard4(g_pool), g_ps, lead(g_out), g_gain, g_bias,
            lead(d_in), shard4(d_pool), d_ps, lead(d_out), d_gain, d_bias,
            lead(nm_in), shard4(nm_pool), nm_ps, lead(nm_out), nm_gain, nm_bias,
            lead(nv_in), shard4(nv_pool), nv_ps, lead(nv_out), nv_gain, nv_bias)
```

```python
import functools

import jax
import jax.numpy as jnp
from jax import lax
from jax.experimental import pallas as pl
from jax.experimental.pallas import tpu as pltpu

F32 = jnp.float32
BF16 = jnp.bfloat16
SDS = jax.ShapeDtypeStruct
MESH = pl.DeviceIdType.MESH

N_DEV = 8
S = 4096
D = 2048
N_HEADS = 8
DH = 128
DA = N_HEADS * DH
DP = 1024
N_GROUPS = 4
GC = DP // N_GROUPS
POOL_WINDOWS = (2, 4, 8, 16)
HALO = 16
D_IN = 3 * DA + DP + D
WB = D_IN // N_DEV
TN = 256
HW = 3 * DA
OB = D // N_DEV
PB = GC // N_DEV
ROPE_DIM = DH // 4
ROPE_HALF = ROPE_DIM // 2
ROPE_THETA = 500000.0
DILATIONS = (1, 4, 16)
KB = 128
LN_EPS = 1e-5
ALPHA = 2.0 ** 0.25
SCALE = DH ** -0.5
NEG = -1e30
ADAM_LR, ADAM_B1, ADAM_B2, ADAM_EPS, ADAM_WD, ADAM_STEP = 0.001, 0.9, 0.999, 1e-08, 0.01, 10

VMEM_LIMIT_V7X = 61 * 1024 * 1024

NT = (((1,), (1,)), ((), ()))
T_N = (((0,), (0,)), ((), ()))


def _params(sem=None):
    return pltpu.CompilerParams(dimension_semantics=sem, vmem_limit_bytes=VMEM_LIMIT_V7X)


def _dot(a, b, dims=None):
    if dims is None:
        return jnp.dot(a, b, preferred_element_type=F32)
    return lax.dot_general(a, b, dims, preferred_element_type=F32)


def _rope_tables():
    inv_freq = ROPE_THETA ** (-(2.0 * jnp.arange(ROPE_HALF, dtype=F32)) / ROPE_DIM)
    ang = jnp.arange(S, dtype=jnp.int32).astype(F32)[:, None] * inv_freq[None, :]
    cos, sin = jnp.cos(ang), jnp.sin(ang)
    rest = DH - ROPE_DIM
    c = jnp.concatenate([cos, cos, jnp.ones((S, rest), F32)], axis=1)
    sn = jnp.concatenate([-sin, sin, jnp.zeros((S, rest), F32)], axis=1)
    return c, sn


def _rope_partner(t):
    lane = lax.broadcasted_iota(jnp.int32, t.shape, 1)
    return jnp.where(lane < ROPE_HALF, pltpu.roll(t, DH - ROPE_HALF, 1), pltpu.roll(t, ROPE_HALF, 1))


def _rope(t, c, sn):
    return t * c + _rope_partner(t) * sn


def _rope_t(g, c, sn):
    return g * c - _rope_partner(g) * sn


def _prep_x(x, rider):
    tm = 512

    def body(x_ref, xb_ref, xt_ref):
        xv = x_ref[...]
        xb_ref[...] = xv.astype(BF16)
        xt_ref[...] = xv.T.astype(BF16)

    return _call_carrying(
        body, rider, (S // tm,), [pl.BlockSpec((tm, D), lambda i: (i, 0))],
        [pl.BlockSpec((tm, D), lambda i: (i, 0)), pl.BlockSpec((D, tm), lambda i: (0, i))],
        [SDS((S, D), BF16), SDS((D, S), BF16)], [], [x], "prep_x")


def _residues(slab_ref, idx, d, r, n):
    return slab_ref[(*idx, pl.ds(r, n, stride=d), slice(None))]


def _proj_phase(xb, wg, s, rope, prev, rider):
    n_heads = TN // DH
    n16 = HW // TN
    j16 = n16 // W_PHASES
    n_rope = 2 * DA // TN
    once = pl.Buffered(1)

    def body(x_ref, w_ref, c_ref, sn_ref, *rest):
        h16_ref, h4_ref, h16r_ref, h32_ref, slab_ref, res_ref = rest[-6:]
        j = pl.program_id(0)
        d4, d16 = DILATIONS[1], DILATIONS[2]
        acc = _dot(x_ref[...], w_ref[...])

        def heads(with_rope):
            for hh in range(n_heads):
                hs = slice(hh * DH, (hh + 1) * DH)
                t = _rope(acc[:, hs], c_ref[...], sn_ref[...]) if with_rope else acc[:, hs]
                h16_ref[:, hs] = t.astype(BF16)
                slab_ref[...] = t
                for q in range(d4):
                    t4 = _residues(slab_ref, (), d4, q, S // d4)
                    h4_ref[q, :, hs] = t4.astype(BF16)
                    res_ref[...] = t4
                    for m in range(d16 // d4):
                        h16r_ref[d4 * m + q, :, hs] = _residues(res_ref, (), d4, m, S // d16).astype(BF16)

        pl.when(W_PHASES * j + s < n_rope)(lambda: heads(True))
        pl.when((W_PHASES * j + s >= n_rope) & (j < j16))(lambda: heads(False))

        @pl.when(j >= j16)
        def _():
            h32_ref[...] = acc

    col16 = lambda j: W_PHASES * jnp.minimum(j, j16 - 1) + s
    col32 = lambda j: W_PHASES * jnp.maximum(j, j16) + s - n16
    hbm = pl.BlockSpec(memory_space=pl.ANY)
    in_specs = [pl.BlockSpec((S, D), lambda j: (0, 0), pipeline_mode=once),
                pl.BlockSpec((None, D, TN), lambda j: (j, 0, 0))]
    in_specs += [pl.BlockSpec((S, DH), lambda j: (0, 0), pipeline_mode=once)] * 2
    args = [xb, wg] + list(rope)
    aliases = {}
    if prev is not None:
        aliases = {len(args) + n: n for n in range(4)}
        in_specs, args = in_specs + [hbm] * 4, args + list(prev)
    out_specs = [pl.BlockSpec((S, TN), lambda j: (0, col16(j)))]
    out_specs += [pl.BlockSpec((d, S // d, TN), lambda j: (0, 0, col16(j))) for d in DILATIONS[1:]]
    out_specs += [pl.BlockSpec((S, TN), lambda j: (0, col32(j)))]
    out_shape = [SDS((S, HW), BF16)] + [SDS((d, S // d, HW), BF16) for d in DILATIONS[1:]] + [SDS((S, HW), F32)]
    return _call_carrying(
        body, rider, (N_DEV,), in_specs, out_specs, out_shape,
        [pltpu.VMEM((S, DH), F32), pltpu.VMEM((S // DILATIONS[1], DH), F32)], args,
        f"proj_phase{s}", aliases)


BQ = 2 * KB
LANES = 128


def _to_lane(acc, col, h):
    lane = lax.broadcasted_iota(jnp.int32, acc.shape, 1)
    return jnp.where(lane == h, col, acc)


def _attn_fwd(h16, name):
    d, n_sub, _ = h16.shape
    n_i = n_sub // BQ
    kw = KB + BQ

    def prev(i):
        return jnp.maximum(2 * i - 1, 0)

    def body(q_ref, kc_ref, kp_ref, vc_ref, vp_ref, o_ref, l_ref, kw_ref, vw_ref):
        i = pl.program_id(1)
        kw_ref[0:KB, :] = kp_ref[...]
        kw_ref[KB:kw, :] = kc_ref[...]
        vw_ref[0:KB, :] = vp_ref[...]
        vw_ref[KB:kw, :] = vc_ref[...]
        a = lax.broadcasted_iota(jnp.int32, (KB, 2 * KB), 0)
        b = lax.broadcasted_iota(jnp.int32, (KB, 2 * KB), 1)
        band = (b >= a) & (b <= a + KB)
        first_key = jnp.where(i == 0, KB, 0)
        masks = (band & (b >= first_key), band)
        for half in range(2):
            rs = slice(half * KB, (half + 1) * KB)
            ks = slice(half * KB, (half + 2) * KB)
            lse = jnp.zeros((KB, LANES), F32)
            for h in range(N_HEADS):
                hs = slice(h * DH, (h + 1) * DH)
                s = jnp.where(masks[half], _dot(q_ref[rs, hs], kw_ref[ks, hs], NT) * SCALE, NEG)
                m = jnp.max(s, axis=1, keepdims=True)
                p = jnp.exp(s - m)
                den = jnp.sum(p, axis=1, keepdims=True)
                o_ref[rs, hs] = (_dot(p.astype(BF16), vw_ref[ks, hs]) / den).astype(BF16)
                lse = _to_lane(lse, m + jnp.log(den), h)
            l_ref[rs, :] = lse

    return pl.pallas_call(
        body, name=name, grid=(d, n_i),
        in_specs=[pl.BlockSpec((None, BQ, DA), lambda r, i: (r, i, 0)),
                  pl.BlockSpec((None, BQ, DA), lambda r, i: (r, i, 1)),
                  pl.BlockSpec((None, KB, DA), lambda r, i: (r, prev(i), 1)),
                  pl.BlockSpec((None, BQ, DA), lambda r, i: (r, i, 2)),
                  pl.BlockSpec((None, KB, DA), lambda r, i: (r, prev(i), 2))],
        out_specs=[pl.BlockSpec((None, BQ, DA), lambda r, i: (r, i, 0)),
                   pl.BlockSpec((None, BQ, LANES), lambda r, i: (r, i, 0))],
        out_shape=[SDS((d, n_sub, DA), BF16), SDS((d, n_sub, LANES), F32)],
        scratch_shapes=[pltpu.VMEM((kw, DA), BF16), pltpu.VMEM((kw, DA), BF16)],
        compiler_params=_params(("arbitrary", "arbitrary")),
    )(h16, h16, h16, h16, h16)


def _pooled(ext_ref, g, rows, tm):
    w = POOL_WINDOWS[g]
    cs = slice(g * GC, (g + 1) * GC)
    cur = ext_ref[HALO:HALO + tm, cs]
    win = cur
    for j in range(1, w):
        win = win + ext_ref[HALO - j:HALO - j + tm, cs]
    cnt = jnp.minimum(rows + 1, w).astype(F32)
    return win / cnt - cur, cnt


def _fill_ext(ext_ref, u_ref, uh_ref, blk, tm):
    @pl.when(blk == 0)
    def _():
        ext_ref[0:HALO, :] = jnp.zeros((HALO, DP), F32)

    @pl.when(blk > 0)
    def _():
        ext_ref[0:HALO, :] = uh_ref[...]

    ext_ref[HALO:HALO + tm, :] = u_ref[...]


def _residue_specs(tm, width):
    return [pl.BlockSpec((d, tm // d, width), lambda i: (0, i, 0)) for d in DILATIONS[1:]]


def _mix(o1, l1, o4, l4, o16, l16, h32, wp, scale):
    tm = 256
    n_slab = N_HEADS + 1

    def body(o1r, l1r, o4r, l4r, o16r, l16r, u_ref, uh_ref, ga_ref, gp_ref, wp_ref, sc_ref,
             y_ref, yt_ref, attn_ref, lse_ref, lse4_ref, lse16_ref, ext_ref, ys_ref, nat_ref, ls_ref):
        i = pl.program_id(0)
        for n, (d, o_r, l_r) in enumerate(((DILATIONS[1], o4r, l4r), (DILATIONS[2], o16r, l16r))):
            for r in range(d):
                rows = pl.ds(r, tm // d, stride=d)
                for h in range(N_HEADS):
                    nat_ref[n, h, rows, :] = o_r[r, :, h * DH:(h + 1) * DH].astype(F32)
                nat_ref[n, N_HEADS, rows, :] = l_r[r]
        la, lb, lc = l1r[...], nat_ref[0, N_HEADS], nat_ref[1, N_HEADS]
        mx = jnp.maximum(jnp.maximum(la, lb), lc)
        ea, eb, ec = jnp.exp(la - mx), jnp.exp(lb - mx), jnp.exp(lc - mx)
        z = ea + eb + ec
        wa, wb, wc = ea / z, eb / z, ec / z
        lse = mx + jnp.log(z)
        lse_ref[...] = lse
        ls_ref[...] = lse
        for d, out in ((DILATIONS[1], lse4_ref), (DILATIONS[2], lse16_ref)):
            for r in range(d):
                out[r] = ls_ref[pl.ds(r, tm // d, stride=d), :]
        for h in range(N_HEADS):
            hs = slice(h * DH, (h + 1) * DH)
            hc = slice(h, h + 1)
            attn = wa[:, hc] * o1r[:, hs].astype(F32) + wb[:, hc] * nat_ref[0, h] + wc[:, hc] * nat_ref[1, h]
            attn_ref[:, hs] = attn
            gt = ga_ref[:, hs]
            ys_ref[:, hs] = attn * (gt * jax.nn.sigmoid(gt))

        _fill_ext(ext_ref, u_ref, uh_ref, i, tm)
        rows = i * tm + lax.broadcasted_iota(jnp.int32, (tm, 1), 0)
        for g in range(N_GROUPS):
            cs = slice(g * GC, (g + 1) * GC)
            gs = slice(DA + g * GC, DA + (g + 1) * GC)
            pooled, _ = _pooled(ext_ref, g, rows, tm)
            po = _dot(pooled.astype(BF16), wp_ref[g]) * sc_ref[:, cs]
            gt = gp_ref[:, cs]
            ys_ref[:, gs] = po * (gt * jax.nn.sigmoid(gt))
        yv = ys_ref[...]
        y_ref[...] = yv.astype(BF16)
        yt_ref[...] = yv.T.astype(BF16)

    row = lambda i: (i, 0)
    blk = pl.BlockSpec((tm, DA), row)
    lanes = pl.BlockSpec((tm, LANES), row)
    o_res, l_res = _residue_specs(tm, DA), _residue_specs(tm, LANES)
    return pl.pallas_call(
        body, name="mix", grid=(S // tm,),
        in_specs=[blk, lanes, o_res[0], l_res[0], o_res[1], l_res[1],
                  pl.BlockSpec((tm, DP), row),
                  pl.BlockSpec((HALO, DP), lambda i: (jnp.maximum(i * (tm // HALO) - 1, 0), 0)),
                  pl.BlockSpec((tm, DA), lambda i: (i, 1)), pl.BlockSpec((tm, DP), lambda i: (i, 2)),
                  pl.BlockSpec((N_GROUPS, GC, GC), lambda i: (0, 0, 0)),
                  pl.BlockSpec((1, DP), lambda i: (0, 0))],
        out_specs=[pl.BlockSpec((tm, D), row), pl.BlockSpec((D, tm), lambda i: (0, i)), blk, lanes] + l_res,
        out_shape=[SDS((S, D), BF16), SDS((D, S), BF16), SDS((S, DA), F32), SDS((S, LANES), F32)]
        + [SDS((d, S // d, LANES), F32) for d in DILATIONS[1:]],
        scratch_shapes=[pltpu.VMEM((HALO + tm, DP), F32), pltpu.VMEM((tm, D), F32),
                        pltpu.VMEM((2, n_slab, tm, DH), F32), pltpu.VMEM((tm, LANES), F32)],
        compiler_params=_params(("arbitrary",)),
    )(o1, l1, o4, l4, o16, l16, h32, h32, h32, h32, wp, scale)


def _outproj_ln(y, wout, x, target, gain, bias):
    tm = 512
    te = 128

    def body(y_ref, w_ref, x_ref, t_ref, g_ref, b_ref, dz_ref, dzb_ref, st_ref, out_ref):
        i = pl.program_id(0)

        @pl.when(i == 0)
        def _():
            st_ref[...] = jnp.zeros((8, D), F32)

        out_ref[...] = _dot(y_ref[...], w_ref[...])
        gn = g_ref[...]
        for e in range(tm // te):
            rs = slice(e * te, (e + 1) * te)
            z = ALPHA * x_ref[rs, :] + out_ref[rs, :]
            mu = jnp.mean(z, axis=1, keepdims=True)
            zc = z - mu
            var = jnp.mean(zc * zc, axis=1, keepdims=True)
            rstd = lax.rsqrt(var + LN_EPS)
            xhat = zc * rstd
            diff = xhat * gn + b_ref[...] - t_ref[rs, :]
            dyln = diff / D
            st_ref[0:1, :] += jnp.sum(dyln * xhat, axis=0, keepdims=True)
            st_ref[1:2, :] += jnp.sum(dyln, axis=0, keepdims=True)
            row_loss = jnp.sum(diff * diff, axis=1, keepdims=True) / D
            st_ref[2:3, :] += jnp.broadcast_to(0.5 * jnp.sum(row_loss, axis=0, keepdims=True), (1, D))
            dxh = dyln * gn
            m1 = jnp.mean(dxh, axis=1, keepdims=True)
            m2 = jnp.mean(dxh * xhat, axis=1, keepdims=True)
            dz = rstd * (dxh - m1 - xhat * m2)
            dz_ref[rs, :] = dz
            dzb_ref[rs, :] = dz.astype(BF16)

    row = lambda i: (i, 0)
    const = lambda i: (0, 0)
    return pl.pallas_call(
        body, name="outproj_ln", grid=(S // tm,),
        in_specs=[pl.BlockSpec((tm, D), row),
                  pl.BlockSpec((D, D), const, pipeline_mode=pl.Buffered(1)),
                  pl.BlockSpec((tm, D), row), pl.BlockSpec((tm, D), row),
                  pl.BlockSpec((1, D), const), pl.BlockSpec((1, D), const)],
        out_specs=[pl.BlockSpec((tm, D), row), pl.BlockSpec((tm, D), row), pl.BlockSpec((8, D), const)],
        out_shape=[SDS((S, D), F32), SDS((S, D), BF16), SDS((8, D), F32)],
        scratch_shapes=[pltpu.VMEM((tm, D), F32)],
        compiler_params=_params(("arbitrary",)),
    )(y, wout, x, target, gain, bias)


def _bwd_mid(dzb, wout, h32, attn, wp, scale, rider=None):
    tm = 256
    n = S // tm

    def body(dz_ref, w_ref, ga_ref, gp_ref, at_ref, u_ref, uh_ref, wp_ref, sc_ref,
             dh_ref, do_ref, do4_ref, do16_ref, dd_ref, dd4_ref, dd16_ref, dwp_ref, dwp16_ref, gps_ref,
             ext_ref, eext_ref, acc_ref, nat_ref, ds_ref):
        i = pl.program_id(0)
        ib = n - 1 - i

        @pl.when(i == 0)
        def _():
            eext_ref[tm:tm + HALO, :] = jnp.zeros((HALO, DP), F32)
            acc_ref[...] = jnp.zeros((N_GROUPS, GC, GC), F32)
            gps_ref[...] = jnp.zeros((8, DP), F32)

        dy = _dot(dz_ref[...], w_ref[...], NT)

        def through_gate(dy_part, gt):
            sg = jax.nn.sigmoid(gt)
            return dy_part * (gt * sg), dy_part * (sg * (1.0 + gt * (1.0 - sg)))

        dat, dyg_a = through_gate(dy[:, 0:DA], ga_ref[...])
        dmix_p, dyg_p = through_gate(dy[:, DA:D], gp_ref[...])

        at = at_ref[...]
        do_ref[...] = dat.astype(BF16)
        dh_ref[:, DP:DP + DA] = (dyg_a * at).astype(BF16)
        prod = dat * at
        dd = jnp.zeros((tm, LANES), F32)
        for h in range(N_HEADS):
            hs = slice(h * DH, (h + 1) * DH)
            dd = _to_lane(dd, jnp.sum(prod[:, hs], axis=1, keepdims=True), h)
            nat_ref[h] = dat[:, hs]
        dd_ref[...] = dd
        ds_ref[...] = dd
        for d, do_out, dd_out in ((DILATIONS[1], do4_ref, dd4_ref), (DILATIONS[2], do16_ref, dd16_ref)):
            for r in range(d):
                dd_out[r] = _residues(ds_ref, (), d, r, tm // d)
                for h in range(N_HEADS):
                    do_out[r, :, h * DH:(h + 1) * DH] = _residues(nat_ref, (h,), d, r, tm // d).astype(BF16)

        _fill_ext(ext_ref, u_ref, uh_ref, ib, tm)
        rows = ib * tm + lax.broadcasted_iota(jnp.int32, (tm, 1), 0)
        for g in range(N_GROUPS):
            w = POOL_WINDOWS[g]
            cs = slice(g * GC, (g + 1) * GC)
            pooled, cnt = _pooled(ext_ref, g, rows, tm)
            pre = _dot(pooled.astype(BF16), wp_ref[g])
            sc = sc_ref[:, cs]
            dpo = dmix_p[:, cs]
            gps_ref[0:1, cs] += jnp.sum(dpo * pre, axis=0, keepdims=True)
            dh_ref[:, DP + DA + g * GC:DP + DA + (g + 1) * GC] = (dyg_p[:, cs] * (pre * sc)).astype(BF16)
            dpre = (dpo * sc).astype(BF16)
            acc_ref[g] += _dot(pooled.T.astype(BF16), dpre)
            dpooled = _dot(dpre, wp_ref[g], NT)
            eext_ref[0:tm, cs] = dpooled / cnt
            du = eext_ref[0:tm, cs]
            for j in range(1, w):
                du = du + eext_ref[j:j + tm, cs]
            dh_ref[:, cs] = (du - dpooled).astype(BF16)
        eext_ref[tm:tm + HALO, :] = eext_ref[0:HALO, :]

        @pl.when(i == n - 1)
        def _():
            for j in range(N_DEV):
                for g in range(N_GROUPS):
                    blk = acc_ref[g, j * PB:(j + 1) * PB, :]
                    dwp_ref[j, g * PB:(g + 1) * PB, :] = blk
                    dwp16_ref[j, g * PB:(g + 1) * PB, :] = blk.astype(BF16)

    rev = lambda i: (n - 1 - i, 0)
    const = lambda i: (0, 0)
    res = lambda width: [pl.BlockSpec((d, tm // d, width), lambda i: (0, n - 1 - i, 0)) for d in DILATIONS[1:]]
    pool_blocks = pl.BlockSpec((N_DEV, N_GROUPS * PB, GC), lambda i: (0, 0, 0))
    outs, carried = _call_carrying(
        body, rider, (n,),
        [pl.BlockSpec((tm, D), rev),
         pl.BlockSpec((D, D), const, pipeline_mode=pl.Buffered(1)),
         pl.BlockSpec((tm, DA), lambda i: (n - 1 - i, 1)), pl.BlockSpec((tm, DP), lambda i: (n - 1 - i, 2)),
         pl.BlockSpec((tm, DA), rev), pl.BlockSpec((tm, DP), rev),
         pl.BlockSpec((HALO, DP), lambda i: (jnp.maximum((n - 1 - i) * (tm // HALO) - 1, 0), 0)),
         pl.BlockSpec((N_GROUPS, GC, GC), lambda i: (0, 0, 0)),
         pl.BlockSpec((1, DP), const)],
        [pl.BlockSpec((tm, DP + D), lambda i: (n - 1 - i, 1)), pl.BlockSpec((tm, DA), rev)] + res(DA)
        + [pl.BlockSpec((tm, LANES), rev)] + res(LANES) + [pool_blocks, pool_blocks, pl.BlockSpec((8, DP), const)],
        [SDS((S, D_IN), BF16), SDS((S, DA), BF16)] + [SDS((d, S // d, DA), BF16) for d in DILATIONS[1:]]
        + [SDS((S, LANES), F32)] + [SDS((d, S // d, LANES), F32) for d in DILATIONS[1:]]
        + [SDS((N_DEV, N_GROUPS * PB, GC), F32), SDS((N_DEV, N_GROUPS * PB, GC), BF16), SDS((8, DP), F32)],
        [pltpu.VMEM((HALO + tm, DP), F32), pltpu.VMEM((tm + HALO, DP), F32),
         pltpu.VMEM((N_GROUPS, GC, GC), F32), pltpu.VMEM((N_HEADS, tm, DH), F32), pltpu.VMEM((tm, LANES), F32)],
        [dzb, wout, h32, h32, attn, h32, h32, wp, scale], "bwd_mid")
    return outs, carried


def _attn_bwd(h16, do, lse, dd, rope, name, others=None, dh=None, rider=None):
    d, n_sub, _ = h16.shape
    n_i = n_sub // BQ
    n_kb = n_sub // KB
    qw = BQ + KB
    final = others is not None
    out_dtype = BF16
    n_cb = 3 * DA // DH

    def nxt(i):
        return jnp.minimum(2 * i + 2, n_kb - 1)

    def body(qc_ref, qn_ref, kc_ref, vc_ref, doc_ref, don_ref, lc_ref, ln_ref, dc_ref, dn_ref,
             c_ref, sn_ref, *rest):
        if final:
            acc4_ref, acc16_ref, _, out_ref, carry_ref, qw_ref, dow_ref, lw_ref, dw_ref, nat_ref = rest
        else:
            out_ref, carry_ref, qw_ref, dow_ref, lw_ref, dw_ref = rest
        i = pl.program_id(1)

        @pl.when(i == 0)
        def _():
            carry_ref[...] = jnp.zeros((KB, DA), F32)

        for win, own, after in ((qw_ref, qc_ref, qn_ref), (dow_ref, doc_ref, don_ref), (lw_ref, lc_ref, ln_ref),
                                (dw_ref, dc_ref, dn_ref)):
            win[0:BQ, :] = own[...]
            win[BQ:qw, :] = after[...]
        if final:
            for n, (dil, acc) in enumerate(((DILATIONS[1], acc4_ref), (DILATIONS[2], acc16_ref))):
                for r in range(dil):
                    for cb in range(n_cb):
                        nat_ref[n, cb, pl.ds(r, BQ // dil, stride=dil), :] = acc[r, :, cb * DH:(cb + 1) * DH].astype(F32)

        a = lax.broadcasted_iota(jnp.int32, (qw, BQ), 0)
        b = lax.broadcasted_iota(jnp.int32, (qw, BQ), 1)
        n_q = jnp.where(i == n_i - 1, BQ, qw)
        mask = (b <= a) & (a <= b + KB) & (a < n_q)
        tok = pl.ds(pl.program_id(0) + d * BQ * i, BQ, stride=d)
        tabs = (c_ref[tok, :], sn_ref[tok, :])
        for h in range(N_HEADS):
            hs = slice(h * DH, (h + 1) * DH)
            hc = slice(h, h + 1)
            q, k, vv, dob = qw_ref[:, hs], kc_ref[:, hs], vc_ref[:, hs], dow_ref[:, hs]
            s = _dot(q, k, NT) * SCALE
            p = jnp.exp(jnp.where(mask, s - lw_ref[:, hc], NEG))
            dp = _dot(dob, vv, NT)
            ds = (p * (dp - dw_ref[:, hc]) * SCALE).astype(BF16)
            dq = _dot(ds, k)
            dk = _dot(ds, q, T_N)
            dv = _dot(p.astype(BF16), dob, T_N)
            dq_lo = dq[0:KB] + carry_ref[:, hs]
            carry_ref[:, hs] = dq[BQ:qw]
            dq_own = _rope_t(jnp.concatenate([dq_lo, dq[KB:BQ]], axis=0), *tabs)
            for base, gv in ((0, dq_own), (DA, _rope_t(dk, *tabs)), (2 * DA, dv)):
                if final:
                    cb = base // DH + h
                    gv = gv + nat_ref[0, cb] + nat_ref[1, cb]
                out_ref[:, base + h * DH:base + (h + 1) * DH] = gv.astype(out_dtype)

    def cur(rows, width, col=0):
        return pl.BlockSpec((None, rows, width), lambda r, i: (r, i, col))

    def nx(width, col=0):
        return pl.BlockSpec((None, KB, width), lambda r, i: (r, nxt(i), col))

    in_specs = [cur(BQ, DA), nx(DA), cur(BQ, DA, 1), cur(BQ, DA, 2), cur(BQ, DA), nx(DA),
                cur(BQ, LANES), nx(LANES), cur(BQ, LANES), nx(LANES)]
    in_specs += [pl.BlockSpec((S, DH), lambda r, i: (0, 0), pipeline_mode=pl.Buffered(1))] * 2
    args = [h16, h16, h16, h16, do, do, lse, lse, dd, dd] + list(rope)
    scratch = [pltpu.VMEM((KB, DA), F32), pltpu.VMEM((qw, DA), BF16), pltpu.VMEM((qw, DA), BF16),
               pltpu.VMEM((qw, LANES), F32), pltpu.VMEM((qw, LANES), F32)]
    if final:
        assert d == 1
        in_specs += [pl.BlockSpec((dil, BQ // dil, 3 * DA), lambda r, i: (0, i, 0)) for dil in DILATIONS[1:]]
        in_specs.append(pl.BlockSpec(memory_space=pl.ANY))
        args += list(others) + [dh[None]]
        scratch.append(pltpu.VMEM((2, n_cb, BQ, DH), F32))
    outs, carried = _call_carrying(
        body, rider, (d, n_i), in_specs, [cur(BQ, 3 * DA)], [SDS((d, n_sub, D_IN if final else 3 * DA), out_dtype)],
        scratch, args, name, {len(args) - 1: 0} if final else {})
    return outs[0] if rider is None else (outs[0], carried)


def _grad_w_out(yt, dzb):
    tn = 256

    def body(yt_ref, dz_ref, o_ref, o16_ref):
        acc = _dot(yt_ref[...], dz_ref[...])
        o_ref[...] = acc
        o16_ref[...] = acc.astype(BF16)

    cols = pl.BlockSpec((D, tn), lambda c: (0, c))
    o, o16 = pl.pallas_call(
        body, name="grad_w_out", grid=(D // tn,),
        in_specs=[pl.BlockSpec((D, S), lambda c: (0, 0), pipeline_mode=pl.Buffered(1)),
                  pl.BlockSpec((S, tn), lambda c: (0, c))],
        out_specs=[cols, cols],
        out_shape=[SDS((D, D), F32), SDS((D, D), BF16)],
        compiler_params=_params(("arbitrary",)),
    )(yt, dzb)
    return o.reshape(N_DEV, OB, D), o16.reshape(N_DEV, OB, D)


def _grad_w_in(xt, dh, order):
    n_send = N_DEV // 2
    n_steps = D_IN // TN
    own0 = n_send * W_PHASES
    sums0 = own0 + W_PHASES
    step_of_send = [W_PHASES * k + W_PHASES + 1 for k in range(n_send)]

    def body(ord_ref, xt_ref, dh_ref, o_ref, o16_ref, r1_ref, s2_ref, st_ref, got_ref,
             piece_sems, send_sems, recv_sems, got_sem):
        c = pl.program_id(0)
        x, y, core, _ = _place()

        def piece(cc):
            blk, sub = ord_ref[cc // W_PHASES], cc % W_PHASES
            return pltpu.make_async_copy(st_ref.at[cc % 2], o16_ref.at[blk, :, pl.ds(sub * TN, TN)], piece_sems.at[cc % 2])

        def send(k):
            return pltpu.make_async_remote_copy(
                src_ref=o16_ref.at[ord_ref[k]], dst_ref=r1_ref.at[k], send_sem=send_sems.at[k], recv_sem=recv_sems.at[k],
                device_id=(x, y, 1 - core), device_id_type=MESH)

        def sibling_part(cc):
            k, sub = cc // W_PHASES - n_send, cc % W_PHASES
            return pltpu.make_async_copy(r1_ref.at[k, :, pl.ds(sub * TN, TN)], got_ref, got_sem)

        @pl.when((c >= 2) & (c < own0 + 2))
        def _():
            piece(c - 2).wait()

        for k in range(n_send):
            pl.when(c == step_of_send[k])(lambda k=k: send(k).start())
        for k in range(1, n_send):
            pl.when(c == own0 + W_PHASES * k)(lambda k=k: send(k).wait_recv())
        pl.when(c >= sums0)(lambda: sibling_part(c).start())

        acc = _dot(xt_ref[...], dh_ref[...])
        o_ref[...] = acc

        @pl.when(c < own0)
        def _():
            st_ref[c % 2] = acc.astype(BF16)
            piece(c).start()

        @pl.when(c >= sums0)
        def _():
            sibling_part(c).wait()
            s2_ref[...] = (acc + got_ref[...].astype(F32)).astype(BF16)

        @pl.when(c == n_steps - 1)
        def _():
            send(0).wait_recv()
            for k in range(n_send):
                send(k).wait_send()

    col = lambda c, ordr: W_PHASES * ordr[c // W_PHASES] + c % W_PHASES
    hbm = pl.BlockSpec(memory_space=pl.ANY)
    sums_blk = lambda c, ordr: (jnp.maximum(c // W_PHASES - (n_send + 1), 0), 0, jnp.where(c >= sums0, c % W_PHASES, 0))
    partial, _, got, sums = pl.pallas_call(
        body, name="grad_w_in",
        grid_spec=pltpu.PrefetchScalarGridSpec(
            num_scalar_prefetch=1, grid=(n_steps,),
            in_specs=[pl.BlockSpec((D, S), lambda c, ordr: (0, 0), pipeline_mode=pl.Buffered(1)),
                      pl.BlockSpec((S, TN), lambda c, ordr: (0, col(c, ordr)))],
            out_specs=[pl.BlockSpec((None, D, TN), lambda c, ordr: (ordr[c // W_PHASES], 0, c % W_PHASES)), hbm, hbm,
                       pl.BlockSpec((None, D, TN), sums_blk)],
            scratch_shapes=[pltpu.VMEM((2, D, TN), BF16), pltpu.VMEM((D, TN), BF16), pltpu.SemaphoreType.DMA((2,)),
                            pltpu.SemaphoreType.DMA((n_send,)), pltpu.SemaphoreType.DMA((n_send,)),
                            pltpu.SemaphoreType.DMA(())]),
        out_shape=[SDS((N_DEV, D, WB), F32), SDS((N_DEV, D, WB), BF16), SDS((n_send, D, WB), BF16),
                   SDS((n_send - 1, D, WB), BF16)],
        compiler_params=_params(("arbitrary",)),
    )(order, xt, dh)
    return partial, got, sums


def _grad_x(dz, dh, wgs, rider=None):
    tm = 1024

    def body(dz_ref, dh_ref, *rest):
        o_ref = rest[-1]

        @pl.when(pl.program_id(1) == 0)
        def _():
            o_ref[...] = ALPHA * dz_ref[...]

        acc = _dot(dh_ref[:, 0:TN], rest[0][...], NT)
        for s in range(1, W_PHASES):
            acc = acc + _dot(dh_ref[:, s * TN:(s + 1) * TN], rest[s][...], NT)
        o_ref[...] += acc

    outs, carried = _call_carrying(
        body, rider, (S // tm, N_DEV),
        [pl.BlockSpec((tm, D), lambda i, j: (i, 0)), pl.BlockSpec((tm, WB), lambda i, j: (i, j))]
        + [pl.BlockSpec((None, D, TN), lambda i, j: (j, 0, 0))] * W_PHASES,
        [pl.BlockSpec((tm, D), lambda i, j: (i, 0))], [SDS((S, D), F32)], [], [dz, dh] + list(wgs), "grad_x")
    return outs[0], carried


def _place():
    x, y, c = lax.axis_index("x"), lax.axis_index("y"), lax.axis_index("c")
    chips = [(x, y), (1 - x, y), (x, 1 - y), (1 - x, 1 - y)]
    return x, y, c, chips


def _blk(x, y, c):
    return 4 * x + 2 * y + c


def _adamw(w, g, m, v):
    m = ADAM_B1 * m + (1.0 - ADAM_B1) * g
    v = ADAM_B2 * v + (1.0 - ADAM_B2) * (g * g)
    m_hat = m / (1.0 - ADAM_B1 ** ADAM_STEP)
    v_hat = v / (1.0 - ADAM_B2 ** ADAM_STEP)
    delta = -ADAM_LR * (m_hat / (jnp.sqrt(v_hat) + ADAM_EPS) + ADAM_WD * w)
    return delta, m, v


class _Rider:
    def __init__(self, args, in_specs, out_shape, out_specs, scratch, start, finish):
        self.args, self.in_specs, self.out_shape, self.out_specs = args, in_specs, out_shape, out_specs
        self.scratch, self.start, self.finish = scratch, start, finish


def _carry(body, rider, n_in, n_out, first, last):
    if rider is None:
        return body
    r_in, r_out, r_scr = len(rider.args), len(rider.out_shape), len(rider.scratch)

    def carrying(*refs):
        o0 = n_in + r_in
        s0 = o0 + n_out + r_out
        s1 = len(refs) - r_scr
        theirs = (refs[n_in:o0], refs[o0 + n_out:s0], refs[s1:])
        pl.when(first())(lambda: rider.start(*theirs))
        body(*refs[:n_in], *refs[o0:o0 + n_out], *refs[s0:s1])
        pl.when(last())(lambda: rider.finish(*theirs))

    return carrying


def _call_carrying(body, rider, grid, in_specs, out_specs, out_shape, scratch, args, name, aliases=None):
    n_in, n_out = len(in_specs), len(out_specs)
    ids = lambda: [pl.program_id(a) for a in range(len(grid))]
    first = lambda: functools.reduce(jnp.logical_and, [i == 0 for i in ids()])
    last = lambda: functools.reduce(jnp.logical_and, [i == n - 1 for i, n in zip(ids(), grid)])
    if rider is not None:
        in_specs, args = in_specs + rider.in_specs, list(args) + rider.args
        out_specs, out_shape = out_specs + rider.out_specs, out_shape + rider.out_shape
        scratch = scratch + rider.scratch
    outs = pl.pallas_call(
        _carry(body, rider, n_in, n_out, first, last), name=name, grid=grid,
        in_specs=in_specs, out_specs=out_specs, out_shape=out_shape, scratch_shapes=scratch,
        input_output_aliases=aliases or {}, compiler_params=_params(("arbitrary",) * len(grid)),
    )(*args)
    return list(outs[:n_out]), list(outs[n_out:])


def _gather_copy(tensors, send_sems, recv_sems, t, k, block, to, src=None):
    dst = tensors[t][1](_blk(*block))
    return pltpu.make_async_remote_copy(
        src_ref=dst if src is None else src, dst_ref=dst,
        send_sem=send_sems.at[t, k], recv_sem=recv_sems.at[t, k], device_id=to, device_id_type=MESH)


def _gather_start(tensors, send_sems, recv_sems, local_sems):
    x, y, c, chips = _place()
    me, sib = (x, y, c), (x, y, 1 - c)
    for t, (src, dst) in enumerate(tensors):
        pltpu.make_async_copy(src, dst(_blk(*me)), local_sems.at[t]).start()
        _gather_copy(tensors, send_sems, recv_sems, t, 0, me, sib, src).start()
        for j in (1, 2, 3):
            _gather_copy(tensors, send_sems, recv_sems, t, j, me, (*chips[j], c), src).start()


def _gather_finish(tensors, send_sems, recv_sems, local_sems):
    x, y, c, chips = _place()
    me, sib = (x, y, c), (x, y, 1 - c)
    copy = functools.partial(_gather_copy, tensors, send_sems, recv_sems)
    for t in range(len(tensors)):
        for j in (1, 2, 3):
            copy(t, j, (*chips[j], c), me).wait_recv()
            copy(t, 3 + j, (*chips[j], c), sib).start()
    for t, (src, dst) in enumerate(tensors):
        copy(t, 0, sib, me).wait_recv()
        for j in (1, 2, 3):
            copy(t, 3 + j, (*chips[j], 1 - c), me).wait_recv()
        copy(t, 0, me, sib, src).wait_send()
        for j in (1, 2, 3):
            copy(t, j, me, (*chips[j], c), src).wait_send()
            copy(t, 3 + j, (*chips[j], c), sib).wait_send()
        pltpu.make_async_copy(src, dst(_blk(*me)), local_sems.at[t]).wait()


W_PHASES = WB // TN


def _cast_shards(w_in_s, w_out_s, w_pool_s):
    def body(*refs):
        for src, dst in zip(refs[:3], refs[3:]):
            dst[...] = src[...].astype(BF16)

    vmem = pl.BlockSpec(memory_space=pltpu.VMEM)
    return pl.pallas_call(
        body, name="cast_shards", in_specs=[vmem] * 3, out_specs=[vmem] * 3,
        out_shape=[SDS(w.shape, BF16) for w in (w_in_s, w_out_s, w_pool_s)],
        compiler_params=_params(),
    )(w_in_s, w_out_s, w_pool_s)


def _gather_w_in_rider(shard16, s):
    def tensors(ins, outs):
        return [(ins[0].at[:, pl.ds(s * TN, TN)], lambda b: outs[0].at[b])]

    hbm = pl.BlockSpec(memory_space=pl.ANY)
    return _Rider(
        args=[shard16], in_specs=[hbm], out_shape=[SDS((N_DEV, D, TN), BF16)], out_specs=[hbm],
        scratch=[pltpu.SemaphoreType.DMA((1, 7)), pltpu.SemaphoreType.DMA((1, 7)), pltpu.SemaphoreType.DMA((1,))],
        start=lambda ins, outs, scr: _gather_start(tensors(ins, outs), *scr),
        finish=lambda ins, outs, scr: _gather_finish(tensors(ins, outs), *scr))


def _gather_small_rider(w_out16, w_pool16):
    def tensors(ins, outs):
        gout_ref, gpool_ref = outs

        def pool_rows(b):
            return gpool_ref.at[:, pl.ds(pl.multiple_of(b * PB, PB), PB), :]

        return [(ins[0], lambda b: gout_ref.at[b]), (ins[1], pool_rows)]

    hbm = pl.BlockSpec(memory_space=pl.ANY)
    return _Rider(
        args=[w_out16, w_pool16], in_specs=[hbm, hbm],
        out_shape=[SDS((N_DEV, OB, D), BF16), SDS((N_GROUPS, GC, GC), BF16)], out_specs=[hbm, hbm],
        scratch=[pltpu.SemaphoreType.DMA((2, 7)), pltpu.SemaphoreType.DMA((2, 7)), pltpu.SemaphoreType.DMA((2,))],
        start=lambda ins, outs, scr: _gather_start(tensors(ins, outs), *scr),
        finish=lambda ins, outs, scr: _gather_finish(tensors(ins, outs), *scr))


def _block_table():
    x, y, c, chips = _place()
    return jnp.stack([_blk(*chip, c) for chip in chips]).astype(jnp.int32)


def _sibling_first_order():
    x, y, c, chips = _place()
    return jnp.stack([_blk(*chip, side) for side in (1 - c, c) for chip in chips]).astype(jnp.int32)


def _exchange_rider(p16):
    _, r_tot, cols = p16.shape

    def copies(ins, outs, scr):
        x, y, c, chips = _place()
        return [pltpu.make_async_remote_copy(
            src_ref=ins[0].at[_blk(*chips[k], 1 - c)], dst_ref=outs[0].at[k],
            send_sem=scr[0].at[k], recv_sem=scr[1].at[k], device_id=(x, y, 1 - c), device_id_type=MESH)
            for k in range(4)]

    def start(ins, outs, scr):
        for cp in copies(ins, outs, scr):
            cp.start()

    def finish(ins, outs, scr):
        for cp in copies(ins, outs, scr):
            cp.wait()

    hbm = pl.BlockSpec(memory_space=pl.ANY)
    return _Rider([p16], [hbm], [SDS((4, r_tot, cols), BF16)], [hbm],
                  [pltpu.SemaphoreType.DMA((4,)), pltpu.SemaphoreType.DMA((4,))], start, finish)


def _chip_sums(name, table, p32, r1, rc):
    _, r_tot, cols = p32.shape

    def body(tbl_ref, p_ref, r_ref, o_ref):
        o_ref[...] = (p_ref[...] + r_ref[...].astype(F32)).astype(BF16)

    return pl.pallas_call(
        body, name=name,
        grid_spec=pltpu.PrefetchScalarGridSpec(
            num_scalar_prefetch=1, grid=(3, r_tot // rc),
            in_specs=[pl.BlockSpec((None, rc, cols), lambda k, ch, tbl: (tbl[k + 1], ch, 0)),
                      pl.BlockSpec((None, rc, cols), lambda k, ch, tbl: (k + 1, ch, 0))],
            out_specs=pl.BlockSpec((None, rc, cols), lambda k, ch, tbl: (k, ch, 0))),
        out_shape=SDS((3, r_tot, cols), BF16),
        compiler_params=_params(("arbitrary", "arbitrary")),
    )(table, p32, r1)


def _stage2_rider(sums, stats=None):
    n_t = len(sums)

    def copies(ins, outs, scr):
        x, y, c, chips = _place()
        out = []
        for t in range(n_t):
            for k in (1, 2, 3):
                out.append(pltpu.make_async_remote_copy(
                    src_ref=ins[t].at[k - 1], dst_ref=outs[t].at[k - 1],
                    send_sem=scr[0].at[t, k - 1], recv_sem=scr[1].at[t, k - 1],
                    device_id=(*chips[k], c), device_id_type=MESH))
        if stats is not None:
            for k in range(1, N_DEV):
                peer = (x ^ ((k >> 2) & 1), y ^ ((k >> 1) & 1), c ^ (k & 1))
                out.append(pltpu.make_async_remote_copy(
                    src_ref=scr[4], dst_ref=outs[n_t].at[_blk(x, y, c)],
                    send_sem=scr[2].at[k - 1], recv_sem=scr[3].at[k - 1], device_id=peer, device_id_type=MESH))
        return out

    def own_rows(outs, scr):
        x, y, c, _ = _place()
        return pltpu.make_async_copy(scr[4], outs[n_t].at[_blk(x, y, c)], scr[5])

    def start(ins, outs, scr):
        if stats is not None:
            scr[4][...] = ins[n_t][...]
            scr[4][3:4, 0:DP] = ins[n_t + 1][0:1, :]
            own_rows(outs, scr).start()
        for cp in copies(ins, outs, scr):
            cp.start()

    def finish(ins, outs, scr):
        for cp in copies(ins, outs, scr):
            cp.wait()
        if stats is not None:
            own_rows(outs, scr).wait()

    vmem = pl.BlockSpec(memory_space=pltpu.VMEM)
    hbm = pl.BlockSpec(memory_space=pl.ANY)
    scratch = [pltpu.SemaphoreType.DMA((n_t, 3)), pltpu.SemaphoreType.DMA((n_t, 3))]
    args, in_specs = list(sums), [hbm] * n_t
    out_shape, out_specs = [SDS(s.shape, BF16) for s in sums], [hbm] * n_t
    if stats is not None:
        scratch += [pltpu.SemaphoreType.DMA((N_DEV - 1,)), pltpu.SemaphoreType.DMA((N_DEV - 1,)),
                    pltpu.VMEM((8, D), F32), pltpu.SemaphoreType.DMA(())]
        args, in_specs = args + list(stats), in_specs + [vmem, vmem]
        out_shape, out_specs = out_shape + [SDS((N_DEV, 8, D), F32)], out_specs + [hbm]
    return _Rider(args, in_specs, out_shape, out_specs, scratch, start, finish)


def _adamw_shard(name, table, p32, r1, r2, w, m, v, rc):
    _, r_tot, cols = p32.shape

    def body(tbl_ref, p_ref, r1_ref, r2_ref, w_ref, m_ref, v_ref, g_ref, d_ref, nm_ref, nv_ref):
        g = p_ref[...] + r1_ref[...].astype(F32)
        for k in range(3):
            g = g + r2_ref[k].astype(F32)
        delta, nm, nv = _adamw(w_ref[...], g, m_ref[...], v_ref[...])
        g_ref[...] = g
        d_ref[...] = delta
        nm_ref[...] = nm
        nv_ref[...] = nv

    rows = pl.BlockSpec((rc, cols), lambda ch, tbl: (ch, 0))
    shard = SDS((r_tot, cols), F32)
    return pl.pallas_call(
        body, name=name,
        grid_spec=pltpu.PrefetchScalarGridSpec(
            num_scalar_prefetch=1, grid=(r_tot // rc,),
            in_specs=[pl.BlockSpec((None, rc, cols), lambda ch, tbl: (tbl[0], ch, 0)),
                      pl.BlockSpec((None, rc, cols), lambda ch, tbl: (0, ch, 0)),
                      pl.BlockSpec((3, rc, cols), lambda ch, tbl: (0, ch, 0)), rows, rows, rows],
            out_specs=[rows, rows, rows, rows]),
        out_shape=[shard, shard, shard, shard],
        compiler_params=_params(("arbitrary",)),
    )(table, p32, r1, r2, w, m, v)


def _replicated_adamw(gathered, gain, bias, scale, m_gain, m_bias, m_scale, v_gain, v_bias, v_scale):
    def body(all_ref, s_ref, g_ref, b_ref, ms_ref, mg_ref, mb_ref, vs_ref, vg_ref, vb_ref, loss_ref, *outs):
        tot = all_ref[0]
        for b in range(1, N_DEV):
            tot = tot + all_ref[b]
        loss_ref[...] = tot[2:3, 0:LANES]
        for n, (row, width, w_r, m_r, v_r) in enumerate(((3, DP, s_ref, ms_ref, vs_ref), (0, D, g_ref, mg_ref, vg_ref),
                                                         (1, D, b_ref, mb_ref, vb_ref))):
            g = tot[row:row + 1, 0:width]
            delta, nm, nv = _adamw(w_r[...], g, m_r[...], v_r[...])
            for out, val in zip(outs[4 * n:4 * n + 4], (g, delta, nm, nv)):
                out[...] = val

    vmem = pl.BlockSpec(memory_space=pltpu.VMEM)
    shapes = [SDS((1, LANES), F32)] + [SDS((1, width), F32) for width in (DP, D, D) for _ in range(4)]
    return pl.pallas_call(
        body, name="replicated_adamw",
        in_specs=[vmem] * 10, out_specs=[vmem] * len(shapes), out_shape=shapes,
        compiler_params=_params(),
    )(gathered, scale, gain, bias, m_scale, m_gain, m_bias, v_scale, v_gain, v_bias)


def kernel(x, w_in, w_pool, pool_scale, w_out, ln_gain, ln_bias, loss_target, m_w_in, m_w_pool, m_pool_scale, m_w_out, m_ln_gain, m_ln_bias, v_w_in, v_w_pool, v_pool_scale, v_w_out, v_ln_gain, v_ln_bias):
    pool_rows = (N_GROUPS * PB, GC)
    x2, target = x[0], loss_target[0]
    rope = _rope_tables()
    table = _block_table()

    shard16, w_out16, w_pool16 = _cast_shards(w_in[0], w_out[0], w_pool[0])
    (xb, xt), (wg0,) = _prep_x(x2, _gather_w_in_rider(shard16, 0))
    h, (wg1,) = _proj_phase(xb, wg0, 0, rope, None, _gather_w_in_rider(shard16, 1))
    h, (wg2,) = _proj_phase(xb, wg1, 1, rope, h, _gather_w_in_rider(shard16, 2))
    h, (wg_out, wg_pool) = _proj_phase(xb, wg2, 2, rope, h, _gather_small_rider(w_out16, w_pool16))
    wg_out = wg_out.reshape(D, D)
    h16, h32 = [h[0][None], h[1], h[2]], h[3]
    fwd = [_attn_fwd(h16[n], f"attn_fwd_d{d}") for n, d in enumerate(DILATIONS)]
    y, yt, attn, lse1, lse4, lse16 = _mix(fwd[0][0][0], fwd[0][1][0], *fwd[1], *fwd[2], h32, wg_pool, pool_scale)
    dz, dzb, stats = _outproj_ln(y, wg_out, x2, target, ln_gain, ln_bias)

    dwout, dwout16 = _grad_w_out(yt, dzb)
    mid, (r1_out,) = _bwd_mid(dzb, wg_out, h32, attn, wg_pool, pool_scale, rider=_exchange_rider(dwout16))
    dh, do1, do4, do16, dd1, dd4, dd16, dwp, dwp16, gps = mid
    s2_out = _chip_sums("rs_out_sums", table, dwout, r1_out, OB)

    do, lse, dd = [do1[None], do4, do16], [lse1[None], lse4, lse16], [dd1[None], dd4, dd16]
    acc4, (r2_out,) = _attn_bwd(h16[1], do[1], lse[1], dd[1], rope, "attn_bwd_d4", rider=_stage2_rider([s2_out]))
    g_out, d_out, nm_out, nv_out = _adamw_shard(
        "adamw_w_out", table, dwout, r1_out, r2_out, w_out[0], m_w_out[0], v_w_out[0], OB // 2)
    acc16, (r1_pool,) = _attn_bwd(h16[2], do[2], lse[2], dd[2], rope, "attn_bwd_d16", rider=_exchange_rider(dwp16))
    s2_pool = _chip_sums("rs_pool_sums", table, dwp, r1_pool, N_GROUPS * PB)
    dh = _attn_bwd(h16[0], do[0], lse[0], dd[0], rope, "attn_bwd_d1", others=(acc4, acc16), dh=dh)[0]

    dwin, r1_in, s2_in = _grad_w_in(xt, dh, _sibling_first_order())
    grad_x, (r2_in, r2_pool, gathered) = _grad_x(
        dz, dh, (wg0, wg1, wg2), rider=_stage2_rider([s2_in, s2_pool], stats=(stats, gps)))
    g_in, d_in, nm_in, nv_in = _adamw_shard(
        "adamw_w_in", table, dwin, r1_in, r2_in, w_in[0], m_w_in[0], v_w_in[0], 256)
    g_pool, d_pool, nm_pool, nv_pool = _adamw_shard(
        "adamw_w_pool", table, dwp, r1_pool, r2_pool, w_pool[0].reshape(pool_rows), m_w_pool[0].reshape(pool_rows),
        v_w_pool[0].reshape(pool_rows), N_GROUPS * PB)
    loss, *small = _replicated_adamw(gathered, ln_gain, ln_bias, pool_scale, m_ln_gain, m_ln_bias, m_pool_scale,
                                     v_ln_gain, v_ln_bias, v_pool_scale)
    (g_ps, d_ps, nm_ps, nv_ps, g_gain, d_gain, nm_gain, nv_gain, g_bias, d_bias, nm_bias, nv_bias) = small

    shard4 = lambda t: t.reshape(1, N_GROUPS, PB, GC)
    lead = lambda t: t[None]
    return (loss[0, 0], lead(grad_x),
            lead(g_in), shard4(g_pool), g_ps, lead(g_out), g_gain, g_bias,
            lead(d_in), shard4(d_pool), d_ps, lead(d_out), d_gain, d_bias,
            lead(nm_in), shard4(nm_pool), nm_ps, lead(nm_out), nm_gain, nm_bias,
            lead(nv_in), shard4(nv_pool), nv_ps, lead(nv_out), nv_gain, nv_bias)
```

```python
import functools

import jax
import jax.numpy as jnp
from jax import lax
from jax.experimental import pallas as pl
from jax.experimental.pallas import tpu as pltpu

F32 = jnp.float32
BF16 = jnp.bfloat16
SDS = jax.ShapeDtypeStruct
MESH = pl.DeviceIdType.MESH

N_DEV = 8
S = 4096
D = 2048
N_HEADS = 8
DH = 128
DA = N_HEADS * DH
DP = 1024
N_GROUPS = 4
GC = DP // N_GROUPS
POOL_WINDOWS = (2, 4, 8, 16)
HALO = 16
D_IN = 3 * DA + DP + D
WB = D_IN // N_DEV
TN = 256
HW = 3 * DA
OB = D // N_DEV
PB = GC // N_DEV
ROPE_DIM = DH // 4
ROPE_HALF = ROPE_DIM // 2
ROPE_THETA = 500000.0
DILATIONS = (1, 4, 16)
KB = 128
LN_EPS = 1e-5
ALPHA = 2.0 ** 0.25
SCALE = DH ** -0.5
NEG = -1e30
ADAM_LR, ADAM_B1, ADAM_B2, ADAM_EPS, ADAM_WD, ADAM_STEP = 0.001, 0.9, 0.999, 1e-08, 0.01, 10

VMEM_LIMIT_V7X = 61 * 1024 * 1024

NT = (((1,), (1,)), ((), ()))
T_N = (((0,), (0,)), ((), ()))


def _params(sem=None):
    return pltpu.CompilerParams(dimension_semantics=sem, vmem_limit_bytes=VMEM_LIMIT_V7X)


def _dot(a, b, dims=None):
    if dims is None:
        return jnp.dot(a, b, preferred_element_type=F32)
    return lax.dot_general(a, b, dims, preferred_element_type=F32)


def _rope_tables():
    inv_freq = ROPE_THETA ** (-(2.0 * jnp.arange(ROPE_HALF, dtype=F32)) / ROPE_DIM)
    ang = jnp.arange(S, dtype=jnp.int32).astype(F32)[:, None] * inv_freq[None, :]
    cos, sin = jnp.cos(ang), jnp.sin(ang)
    rest = DH - ROPE_DIM
    c = jnp.concatenate([cos, cos, jnp.ones((S, rest), F32)], axis=1)
    sn = jnp.concatenate([-sin, sin, jnp.zeros((S, rest), F32)], axis=1)
    return c, sn


def _rope_partner(t):
    lane = lax.broadcasted_iota(jnp.int32, t.shape, 1)
    return jnp.where(lane < ROPE_HALF, pltpu.roll(t, DH - ROPE_HALF, 1), pltpu.roll(t, ROPE_HALF, 1))


def _rope(t, c, sn):
    return t * c + _rope_partner(t) * sn


def _rope_t(g, c, sn):
    return g * c - _rope_partner(g) * sn


def _prep_x(x, rider):
    tm = 512

    def body(x_ref, xb_ref, xt_ref):
        xv = x_ref[...]
        xb_ref[...] = xv.astype(BF16)
        xt_ref[...] = xv.T.astype(BF16)

    return _call_carrying(
        body, rider, (S // tm,), [pl.BlockSpec((tm, D), lambda i: (i, 0))],
        [pl.BlockSpec((tm, D), lambda i: (i, 0)), pl.BlockSpec((D, tm), lambda i: (0, i))],
        [SDS((S, D), BF16), SDS((D, S), BF16)], [], [x], "prep_x")


def _residues(slab_ref, idx, d, r, n):
    return slab_ref[(*idx, pl.ds(r, n, stride=d), slice(None))]


def _proj_phase(xb, wg, s, rope, prev, rider):
    n_heads = TN // DH
    n16 = HW // TN
    j16 = n16 // W_PHASES
    n_rope = 2 * DA // TN
    once = pl.Buffered(1)

    def body(x_ref, w_ref, c_ref, sn_ref, *rest):
        h16_ref, h4_ref, h16r_ref, hug_ref, slab_ref, res_ref = rest[-6:]
        j = pl.program_id(0)
        d4, d16 = DILATIONS[1], DILATIONS[2]
        acc = _dot(x_ref[...], w_ref[...])

        def heads(with_rope):
            for hh in range(n_heads):
                hs = slice(hh * DH, (hh + 1) * DH)
                t = _rope(acc[:, hs], c_ref[...], sn_ref[...]) if with_rope else acc[:, hs]
                h16_ref[:, hs] = t.astype(BF16)
                slab_ref[...] = t
                for q in range(d4):
                    t4 = _residues(slab_ref, (), d4, q, S // d4)
                    h4_ref[q, :, hs] = t4.astype(BF16)
                    res_ref[...] = t4
                    for m in range(d16 // d4):
                        h16r_ref[d4 * m + q, :, hs] = _residues(res_ref, (), d4, m, S // d16).astype(BF16)

        pl.when(W_PHASES * j + s < n_rope)(lambda: heads(True))
        pl.when((W_PHASES * j + s >= n_rope) & (j < j16))(lambda: heads(False))

        @pl.when(j >= j16)
        def _():
            hug_ref[...] = acc.astype(BF16)

    col16 = lambda j: W_PHASES * jnp.minimum(j, j16 - 1) + s
    col32 = lambda j: W_PHASES * jnp.maximum(j, j16) + s - n16
    hbm = pl.BlockSpec(memory_space=pl.ANY)
    in_specs = [pl.BlockSpec((S, D), lambda j: (0, 0), pipeline_mode=once),
                pl.BlockSpec((None, D, TN), lambda j: (j, 0, 0))]
    in_specs += [pl.BlockSpec((S, DH), lambda j: (0, 0), pipeline_mode=once)] * 2
    args = [xb, wg] + list(rope)
    aliases = {}
    if prev is not None:
        aliases = {len(args) + n: n for n in range(4)}
        in_specs, args = in_specs + [hbm] * 4, args + list(prev)
    out_specs = [pl.BlockSpec((S, TN), lambda j: (0, col16(j)))]
    out_specs += [pl.BlockSpec((d, S // d, TN), lambda j: (0, 0, col16(j))) for d in DILATIONS[1:]]
    out_specs += [pl.BlockSpec((S, TN), lambda j: (0, col32(j)))]
    out_shape = [SDS((S, HW), BF16)] + [SDS((d, S // d, HW), BF16) for d in DILATIONS[1:]] + [SDS((S, HW), BF16)]
    return _call_carrying(
        body, rider, (N_DEV,), in_specs, out_specs, out_shape,
        [pltpu.VMEM((S, DH), F32), pltpu.VMEM((S // DILATIONS[1], DH), F32)], args,
        f"proj_phase{s}", aliases)


BQ = 2 * KB
LANES = 128


def _to_lane(acc, col, h):
    lane = lax.broadcasted_iota(jnp.int32, acc.shape, 1)
    return jnp.where(lane == h, col, acc)


def _attn_fwd(h16, name):
    d, n_sub, _ = h16.shape
    n_i = n_sub // BQ
    kw = KB + BQ

    def prev(i):
        return jnp.maximum(2 * i - 1, 0)

    def body(q_ref, kc_ref, kp_ref, vc_ref, vp_ref, o_ref, l_ref, kw_ref, vw_ref):
        i = pl.program_id(1)
        kw_ref[0:KB, :] = kp_ref[...]
        kw_ref[KB:kw, :] = kc_ref[...]
        vw_ref[0:KB, :] = vp_ref[...]
        vw_ref[KB:kw, :] = vc_ref[...]
        a = lax.broadcasted_iota(jnp.int32, (KB, 2 * KB), 0)
        b = lax.broadcasted_iota(jnp.int32, (KB, 2 * KB), 1)
        band = (b >= a) & (b <= a + KB)
        first_key = jnp.where(i == 0, KB, 0)
        masks = (band & (b >= first_key), band)
        for half in range(2):
            rs = slice(half * KB, (half + 1) * KB)
            ks = slice(half * KB, (half + 2) * KB)
            lse = jnp.zeros((KB, LANES), F32)
            for h in range(N_HEADS):
                hs = slice(h * DH, (h + 1) * DH)
                s = jnp.where(masks[half], _dot(q_ref[rs, hs], kw_ref[ks, hs], NT) * SCALE, NEG)
                m = jnp.max(s, axis=1, keepdims=True)
                p = jnp.exp(s - m)
                den = jnp.sum(p, axis=1, keepdims=True)
                o_ref[rs, hs] = (_dot(p.astype(BF16), vw_ref[ks, hs]) / den).astype(BF16)
                lse = _to_lane(lse, m + jnp.log(den), h)
            l_ref[rs, :] = lse

    return pl.pallas_call(
        body, name=name, grid=(d, n_i),
        in_specs=[pl.BlockSpec((None, BQ, DA), lambda r, i: (r, i, 0)),
                  pl.BlockSpec((None, BQ, DA), lambda r, i: (r, i, 1)),
                  pl.BlockSpec((None, KB, DA), lambda r, i: (r, prev(i), 1)),
                  pl.BlockSpec((None, BQ, DA), lambda r, i: (r, i, 2)),
                  pl.BlockSpec((None, KB, DA), lambda r, i: (r, prev(i), 2))],
        out_specs=[pl.BlockSpec((None, BQ, DA), lambda r, i: (r, i, 0)),
                   pl.BlockSpec((None, BQ, LANES), lambda r, i: (r, i, 0))],
        out_shape=[SDS((d, n_sub, DA), BF16), SDS((d, n_sub, LANES), F32)],
        scratch_shapes=[pltpu.VMEM((kw, DA), BF16), pltpu.VMEM((kw, DA), BF16)],
        compiler_params=_params(("arbitrary", "arbitrary")),
    )(h16, h16, h16, h16, h16)


def _pooled(ext_ref, g, rows, tm):
    w = POOL_WINDOWS[g]
    cs = slice(g * GC, (g + 1) * GC)
    cur = ext_ref[HALO:HALO + tm, cs]
    win = cur
    for j in range(1, w):
        win = win + ext_ref[HALO - j:HALO - j + tm, cs]
    cnt = jnp.minimum(rows + 1, w).astype(F32)
    return win / cnt - cur, cnt


def _fill_ext(ext_ref, u_ref, uh_ref, blk, tm):
    @pl.when(blk == 0)
    def _():
        ext_ref[0:HALO, :] = jnp.zeros((HALO, DP), F32)

    @pl.when(blk > 0)
    def _():
        ext_ref[0:HALO, :] = uh_ref[...].astype(F32)

    ext_ref[HALO:HALO + tm, :] = u_ref[...].astype(F32)


def _residue_specs(tm, width):
    return [pl.BlockSpec((d, tm // d, width), lambda i: (0, i, 0)) for d in DILATIONS[1:]]


def _mix(o1, l1, o4, l4, o16, l16, hug, wp, scale):
    tm = 256
    n_slab = N_HEADS + 1

    def body(o1r, l1r, o4r, l4r, o16r, l16r, u_ref, uh_ref, ga_ref, gp_ref, wp_ref, sc_ref,
             y_ref, yt_ref, attn_ref, lse_ref, lse4_ref, lse16_ref, ext_ref, ys_ref, nat_ref, ls_ref):
        i = pl.program_id(0)
        for n, (d, o_r, l_r) in enumerate(((DILATIONS[1], o4r, l4r), (DILATIONS[2], o16r, l16r))):
            for r in range(d):
                rows = pl.ds(r, tm // d, stride=d)
                for h in range(N_HEADS):
                    nat_ref[n, h, rows, :] = o_r[r, :, h * DH:(h + 1) * DH].astype(F32)
                nat_ref[n, N_HEADS, rows, :] = l_r[r]
        la, lb, lc = l1r[...], nat_ref[0, N_HEADS], nat_ref[1, N_HEADS]
        mx = jnp.maximum(jnp.maximum(la, lb), lc)
        ea, eb, ec = jnp.exp(la - mx), jnp.exp(lb - mx), jnp.exp(lc - mx)
        z = ea + eb + ec
        wa, wb, wc = ea / z, eb / z, ec / z
        lse = mx + jnp.log(z)
        lse_ref[...] = lse
        ls_ref[...] = lse
        for d, out in ((DILATIONS[1], lse4_ref), (DILATIONS[2], lse16_ref)):
            for r in range(d):
                out[r] = ls_ref[pl.ds(r, tm // d, stride=d), :]
        for h in range(N_HEADS):
            hs = slice(h * DH, (h + 1) * DH)
            hc = slice(h, h + 1)
            attn = wa[:, hc] * o1r[:, hs].astype(F32) + wb[:, hc] * nat_ref[0, h] + wc[:, hc] * nat_ref[1, h]
            attn_ref[:, hs] = attn.astype(BF16)
            gt = ga_ref[:, hs].astype(F32)
            ys_ref[:, hs] = attn * (gt * jax.nn.sigmoid(gt))

        _fill_ext(ext_ref, u_ref, uh_ref, i, tm)
        rows = i * tm + lax.broadcasted_iota(jnp.int32, (tm, 1), 0)
        for g in range(N_GROUPS):
            cs = slice(g * GC, (g + 1) * GC)
            gs = slice(DA + g * GC, DA + (g + 1) * GC)
            pooled, _ = _pooled(ext_ref, g, rows, tm)
            po = _dot(pooled.astype(BF16), wp_ref[g]) * sc_ref[:, cs]
            gt = gp_ref[:, cs].astype(F32)
            ys_ref[:, gs] = po * (gt * jax.nn.sigmoid(gt))
        yv = ys_ref[...]
        y_ref[...] = yv.astype(BF16)
        yt_ref[...] = yv.T.astype(BF16)

    row = lambda i: (i, 0)
    blk = pl.BlockSpec((tm, DA), row)
    lanes = pl.BlockSpec((tm, LANES), row)
    o_res, l_res = _residue_specs(tm, DA), _residue_specs(tm, LANES)
    return pl.pallas_call(
        body, name="mix", grid=(S // tm,),
        in_specs=[blk, lanes, o_res[0], l_res[0], o_res[1], l_res[1],
                  pl.BlockSpec((tm, DP), row),
                  pl.BlockSpec((HALO, DP), lambda i: (jnp.maximum(i * (tm // HALO) - 1, 0), 0)),
                  pl.BlockSpec((tm, DA), lambda i: (i, 1)), pl.BlockSpec((tm, DP), lambda i: (i, 2)),
                  pl.BlockSpec((N_GROUPS, GC, GC), lambda i: (0, 0, 0)),
                  pl.BlockSpec((1, DP), lambda i: (0, 0))],
        out_specs=[pl.BlockSpec((tm, D), row), pl.BlockSpec((D, tm), lambda i: (0, i)), blk, lanes] + l_res,
        out_shape=[SDS((S, D), BF16), SDS((D, S), BF16), SDS((S, DA), BF16), SDS((S, LANES), F32)]
        + [SDS((d, S // d, LANES), F32) for d in DILATIONS[1:]],
        scratch_shapes=[pltpu.VMEM((HALO + tm, DP), F32), pltpu.VMEM((tm, D), F32),
                        pltpu.VMEM((2, n_slab, tm, DH), F32), pltpu.VMEM((tm, LANES), F32)],
        compiler_params=_params(("arbitrary",)),
    )(o1, l1, o4, l4, o16, l16, hug, hug, hug, hug, wp, scale)


def _outproj_ln(y, wout, x, target, gain, bias):
    tm = 512
    te = 128

    def body(y_ref, w_ref, x_ref, t_ref, g_ref, b_ref, dz_ref, dzb_ref, st_ref, out_ref):
        i = pl.program_id(0)

        @pl.when(i == 0)
        def _():
            st_ref[...] = jnp.zeros((8, D), F32)

        out_ref[...] = _dot(y_ref[...], w_ref[...])
        gn = g_ref[...]
        for e in range(tm // te):
            rs = slice(e * te, (e + 1) * te)
            z = ALPHA * x_ref[rs, :] + out_ref[rs, :]
            mu = jnp.mean(z, axis=1, keepdims=True)
            zc = z - mu
            var = jnp.mean(zc * zc, axis=1, keepdims=True)
            rstd = lax.rsqrt(var + LN_EPS)
            xhat = zc * rstd
            diff = xhat * gn + b_ref[...] - t_ref[rs, :]
            dyln = diff / D
            st_ref[0:1, :] += jnp.sum(dyln * xhat, axis=0, keepdims=True)
            st_ref[1:2, :] += jnp.sum(dyln, axis=0, keepdims=True)
            row_loss = jnp.sum(diff * diff, axis=1, keepdims=True) / D
            st_ref[2:3, :] += jnp.broadcast_to(0.5 * jnp.sum(row_loss, axis=0, keepdims=True), (1, D))
            dxh = dyln * gn
            m1 = jnp.mean(dxh, axis=1, keepdims=True)
            m2 = jnp.mean(dxh * xhat, axis=1, keepdims=True)
            dz = rstd * (dxh - m1 - xhat * m2)
            dz_ref[rs, :] = dz
            dzb_ref[rs, :] = dz.astype(BF16)

    row = lambda i: (i, 0)
    const = lambda i: (0, 0)
    return pl.pallas_call(
        body, name="outproj_ln", grid=(S // tm,),
        in_specs=[pl.BlockSpec((tm, D), row),
                  pl.BlockSpec((D, D), const, pipeline_mode=pl.Buffered(1)),
                  pl.BlockSpec((tm, D), row), pl.BlockSpec((tm, D), row),
                  pl.BlockSpec((1, D), const), pl.BlockSpec((1, D), const)],
        out_specs=[pl.BlockSpec((tm, D), row), pl.BlockSpec((tm, D), row), pl.BlockSpec((8, D), const)],
        out_shape=[SDS((S, D), F32), SDS((S, D), BF16), SDS((8, D), F32)],
        scratch_shapes=[pltpu.VMEM((tm, D), F32)],
        compiler_params=_params(("arbitrary",)),
    )(y, wout, x, target, gain, bias)


def _bwd_mid(dzb, wout, hug, attn, wp, scale, rider=None):
    tm = 256
    n = S // tm

    def body(dz_ref, w_ref, ga_ref, gp_ref, at_ref, u_ref, uh_ref, wp_ref, sc_ref,
             dh_ref, do_ref, do4_ref, do16_ref, dd_ref, dd4_ref, dd16_ref, dwp_ref, dwp16_ref, gps_ref,
             ext_ref, eext_ref, acc_ref, nat_ref, ds_ref):
        i = pl.program_id(0)
        ib = n - 1 - i

        @pl.when(i == 0)
        def _():
            eext_ref[tm:tm + HALO, :] = jnp.zeros((HALO, DP), F32)
            acc_ref[...] = jnp.zeros((N_GROUPS, GC, GC), F32)
            gps_ref[...] = jnp.zeros((8, DP), F32)

        dy = _dot(dz_ref[...], w_ref[...], NT)

        def through_gate(dy_part, gt):
            sg = jax.nn.sigmoid(gt)
            return dy_part * (gt * sg), dy_part * (sg * (1.0 + gt * (1.0 - sg)))

        dat, dyg_a = through_gate(dy[:, 0:DA], ga_ref[...].astype(F32))
        dmix_p, dyg_p = through_gate(dy[:, DA:D], gp_ref[...].astype(F32))

        at = at_ref[...].astype(F32)
        do_ref[...] = dat.astype(BF16)
        dh_ref[:, DP:DP + DA] = (dyg_a * at).astype(BF16)
        prod = dat * at
        dd = jnp.zeros((tm, LANES), F32)
        for h in range(N_HEADS):
            hs = slice(h * DH, (h + 1) * DH)
            dd = _to_lane(dd, jnp.sum(prod[:, hs], axis=1, keepdims=True), h)
            nat_ref[h] = dat[:, hs]
        dd_ref[...] = dd
        ds_ref[...] = dd
        for d, do_out, dd_out in ((DILATIONS[1], do4_ref, dd4_ref), (DILATIONS[2], do16_ref, dd16_ref)):
            for r in range(d):
                dd_out[r] = _residues(ds_ref, (), d, r, tm // d)
                for h in range(N_HEADS):
                    do_out[r, :, h * DH:(h + 1) * DH] = _residues(nat_ref, (h,), d, r, tm // d).astype(BF16)

        _fill_ext(ext_ref, u_ref, uh_ref, ib, tm)
        rows = ib * tm + lax.broadcasted_iota(jnp.int32, (tm, 1), 0)
        for g in range(N_GROUPS):
            w = POOL_WINDOWS[g]
            cs = slice(g * GC, (g + 1) * GC)
            pooled, cnt = _pooled(ext_ref, g, rows, tm)
            pre = _dot(pooled.astype(BF16), wp_ref[g])
            sc = sc_ref[:, cs]
            dpo = dmix_p[:, cs]
            gps_ref[0:1, cs] += jnp.sum(dpo * pre, axis=0, keepdims=True)
            dh_ref[:, DP + DA + g * GC:DP + DA + (g + 1) * GC] = (dyg_p[:, cs] * (pre * sc)).astype(BF16)
            dpre = (dpo * sc).astype(BF16)
            acc_ref[g] += _dot(pooled.T.astype(BF16), dpre)
            dpooled = _dot(dpre, wp_ref[g], NT)
            eext_ref[0:tm, cs] = dpooled / cnt
            du = eext_ref[0:tm, cs]
            for j in range(1, w):
                du = du + eext_ref[j:j + tm, cs]
            dh_ref[:, cs] = (du - dpooled).astype(BF16)
        eext_ref[tm:tm + HALO, :] = eext_ref[0:HALO, :]

        @pl.when(i == n - 1)
        def _():
            for j in range(N_DEV):
                for g in range(N_GROUPS):
                    blk = acc_ref[g, j * PB:(j + 1) * PB, :]
                    dwp_ref[j, g * PB:(g + 1) * PB, :] = blk
                    dwp16_ref[j, g * PB:(g + 1) * PB, :] = blk.astype(BF16)

    rev = lambda i: (n - 1 - i, 0)
    const = lambda i: (0, 0)
    res = lambda width: [pl.BlockSpec((d, tm // d, width), lambda i: (0, n - 1 - i, 0)) for d in DILATIONS[1:]]
    pool_blocks = pl.BlockSpec((N_DEV, N_GROUPS * PB, GC), lambda i: (0, 0, 0))
    outs, carried = _call_carrying(
        body, rider, (n,),
        [pl.BlockSpec((tm, D), rev),
         pl.BlockSpec((D, D), const, pipeline_mode=pl.Buffered(1)),
         pl.BlockSpec((tm, DA), lambda i: (n - 1 - i, 1)), pl.BlockSpec((tm, DP), lambda i: (n - 1 - i, 2)),
         pl.BlockSpec((tm, DA), rev), pl.BlockSpec((tm, DP), rev),
         pl.BlockSpec((HALO, DP), lambda i: (jnp.maximum((n - 1 - i) * (tm // HALO) - 1, 0), 0)),
         pl.BlockSpec((N_GROUPS, GC, GC), lambda i: (0, 0, 0)),
         pl.BlockSpec((1, DP), const)],
        [pl.BlockSpec((tm, DP + D), lambda i: (n - 1 - i, 1)), pl.BlockSpec((tm, DA), rev)] + res(DA)
        + [pl.BlockSpec((tm, LANES), rev)] + res(LANES) + [pool_blocks, pool_blocks, pl.BlockSpec((8, DP), const)],
        [SDS((S, D_IN), BF16), SDS((S, DA), BF16)] + [SDS((d, S // d, DA), BF16) for d in DILATIONS[1:]]
        + [SDS((S, LANES), F32)] + [SDS((d, S // d, LANES), F32) for d in DILATIONS[1:]]
        + [SDS((N_DEV, N_GROUPS * PB, GC), F32), SDS((N_DEV, N_GROUPS * PB, GC), BF16), SDS((8, DP), F32)],
        [pltpu.VMEM((HALO + tm, DP), F32), pltpu.VMEM((tm + HALO, DP), F32),
         pltpu.VMEM((N_GROUPS, GC, GC), F32), pltpu.VMEM((N_HEADS, tm, DH), F32), pltpu.VMEM((tm, LANES), F32)],
        [dzb, wout, hug, hug, attn, hug, hug, wp, scale], "bwd_mid")
    return outs, carried


def _attn_bwd(h16, do, lse, dd, rope, name, others=None, dh=None, rider=None):
    d, n_sub, _ = h16.shape
    n_i = n_sub // BQ
    n_kb = n_sub // KB
    qw = BQ + KB
    final = others is not None
    out_dtype = BF16
    n_cb = 3 * DA // DH

    def nxt(i):
        return jnp.minimum(2 * i + 2, n_kb - 1)

    def body(qc_ref, qn_ref, kc_ref, vc_ref, doc_ref, don_ref, lc_ref, ln_ref, dc_ref, dn_ref,
             c_ref, sn_ref, *rest):
        if final:
            acc4_ref, acc16_ref, _, out_ref, carry_ref, qw_ref, dow_ref, lw_ref, dw_ref, nat_ref = rest
        else:
            out_ref, carry_ref, qw_ref, dow_ref, lw_ref, dw_ref = rest
        i = pl.program_id(1)

        @pl.when(i == 0)
        def _():
            carry_ref[...] = jnp.zeros((KB, DA), F32)

        for win, own, after in ((qw_ref, qc_ref, qn_ref), (dow_ref, doc_ref, don_ref), (lw_ref, lc_ref, ln_ref),
                                (dw_ref, dc_ref, dn_ref)):
            win[0:BQ, :] = own[...]
            win[BQ:qw, :] = after[...]
        if final:
            for n, (dil, acc) in enumerate(((DILATIONS[1], acc4_ref), (DILATIONS[2], acc16_ref))):
                for r in range(dil):
                    for cb in range(n_cb):
                        nat_ref[n, cb, pl.ds(r, BQ // dil, stride=dil), :] = acc[r, :, cb * DH:(cb + 1) * DH].astype(F32)

        a = lax.broadcasted_iota(jnp.int32, (qw, BQ), 0)
        b = lax.broadcasted_iota(jnp.int32, (qw, BQ), 1)
        n_q = jnp.where(i == n_i - 1, BQ, qw)
        mask = (b <= a) & (a <= b + KB) & (a < n_q)
        tok = pl.ds(pl.program_id(0) + d * BQ * i, BQ, stride=d)
        tabs = (c_ref[tok, :], sn_ref[tok, :])
        for h in range(N_HEADS):
            hs = slice(h * DH, (h + 1) * DH)
            hc = slice(h, h + 1)
            q, k, vv, dob = qw_ref[:, hs], kc_ref[:, hs], vc_ref[:, hs], dow_ref[:, hs]
            s = _dot(q, k, NT) * SCALE
            p = jnp.exp(jnp.where(mask, s - lw_ref[:, hc], NEG))
            dp = _dot(dob, vv, NT)
            ds = (p * (dp - dw_ref[:, hc]) * SCALE).astype(BF16)
            dq = _dot(ds, k)
            dk = _dot(ds, q, T_N)
            dv = _dot(p.astype(BF16), dob, T_N)
            dq_lo = dq[0:KB] + carry_ref[:, hs]
            carry_ref[:, hs] = dq[BQ:qw]
            dq_own = _rope_t(jnp.concatenate([dq_lo, dq[KB:BQ]], axis=0), *tabs)
            for base, gv in ((0, dq_own), (DA, _rope_t(dk, *tabs)), (2 * DA, dv)):
                if final:
                    cb = base // DH + h
                    gv = gv + nat_ref[0, cb] + nat_ref[1, cb]
                out_ref[:, base + h * DH:base + (h + 1) * DH] = gv.astype(out_dtype)

    def cur(rows, width, col=0):
        return pl.BlockSpec((None, rows, width), lambda r, i: (r, i, col))

    def nx(width, col=0):
        return pl.BlockSpec((None, KB, width), lambda r, i: (r, nxt(i), col))

    in_specs = [cur(BQ, DA), nx(DA), cur(BQ, DA, 1), cur(BQ, DA, 2), cur(BQ, DA), nx(DA),
                cur(BQ, LANES), nx(LANES), cur(BQ, LANES), nx(LANES)]
    in_specs += [pl.BlockSpec((S, DH), lambda r, i: (0, 0), pipeline_mode=pl.Buffered(1))] * 2
    args = [h16, h16, h16, h16, do, do, lse, lse, dd, dd] + list(rope)
    scratch = [pltpu.VMEM((KB, DA), F32), pltpu.VMEM((qw, DA), BF16), pltpu.VMEM((qw, DA), BF16),
               pltpu.VMEM((qw, LANES), F32), pltpu.VMEM((qw, LANES), F32)]
    if final:
        assert d == 1
        in_specs += [pl.BlockSpec((dil, BQ // dil, 3 * DA), lambda r, i: (0, i, 0)) for dil in DILATIONS[1:]]
        in_specs.append(pl.BlockSpec(memory_space=pl.ANY))
        args += list(others) + [dh[None]]
        scratch.append(pltpu.VMEM((2, n_cb, BQ, DH), F32))
    outs, carried = _call_carrying(
        body, rider, (d, n_i), in_specs, [cur(BQ, 3 * DA)], [SDS((d, n_sub, D_IN if final else 3 * DA), out_dtype)],
        scratch, args, name, {len(args) - 1: 0} if final else {})
    return outs[0] if rider is None else (outs[0], carried)


def _grad_w_out(yt, dzb):
    tn = 256

    def body(yt_ref, dz_ref, o_ref, o16_ref):
        acc = _dot(yt_ref[...], dz_ref[...])
        o_ref[...] = acc
        o16_ref[...] = acc.astype(BF16)

    cols = pl.BlockSpec((D, tn), lambda c: (0, c))
    o, o16 = pl.pallas_call(
        body, name="grad_w_out", grid=(D // tn,),
        in_specs=[pl.BlockSpec((D, S), lambda c: (0, 0), pipeline_mode=pl.Buffered(1)),
                  pl.BlockSpec((S, tn), lambda c: (0, c))],
        out_specs=[cols, cols],
        out_shape=[SDS((D, D), F32), SDS((D, D), BF16)],
        compiler_params=_params(("arbitrary",)),
    )(yt, dzb)
    return o.reshape(N_DEV, OB, D), o16.reshape(N_DEV, OB, D)


def _grad_w_in(xt, dh, order):
    n_send = N_DEV // 2
    n_steps = D_IN // TN
    own0 = n_send * W_PHASES
    sums0 = own0 + W_PHASES
    step_of_send = [W_PHASES * k + W_PHASES + 1 for k in range(n_send)]

    def body(ord_ref, xt_ref, dh_ref, o_ref, o16_ref, r1_ref, s2_ref, st_ref, got_ref,
             piece_sems, send_sems, recv_sems, got_sem):
        c = pl.program_id(0)
        x, y, core, _ = _place()

        def piece(cc):
            blk, sub = ord_ref[cc // W_PHASES], cc % W_PHASES
            return pltpu.make_async_copy(st_ref.at[cc % 2], o16_ref.at[blk, :, pl.ds(sub * TN, TN)], piece_sems.at[cc % 2])

        def send(k):
            return pltpu.make_async_remote_copy(
                src_ref=o16_ref.at[ord_ref[k]], dst_ref=r1_ref.at[k], send_sem=send_sems.at[k], recv_sem=recv_sems.at[k],
                device_id=(x, y, 1 - core), device_id_type=MESH)

        def sibling_part(cc):
            k, sub = cc // W_PHASES - n_send, cc % W_PHASES
            return pltpu.make_async_copy(r1_ref.at[k, :, pl.ds(sub * TN, TN)], got_ref, got_sem)

        @pl.when((c >= 2) & (c < own0 + 2))
        def _():
            piece(c - 2).wait()

        for k in range(n_send):
            pl.when(c == step_of_send[k])(lambda k=k: send(k).start())
        for k in range(1, n_send):
            pl.when(c == own0 + W_PHASES * k)(lambda k=k: send(k).wait_recv())
        pl.when(c >= sums0)(lambda: sibling_part(c).start())

        acc = _dot(xt_ref[...], dh_ref[...])
        o_ref[...] = acc

        @pl.when(c < own0)
        def _():
            st_ref[c % 2] = acc.astype(BF16)
            piece(c).start()

        @pl.when(c >= sums0)
        def _():
            sibling_part(c).wait()
            s2_ref[...] = (acc + got_ref[...].astype(F32)).astype(BF16)

        @pl.when(c == n_steps - 1)
        def _():
            send(0).wait_recv()
            for k in range(n_send):
                send(k).wait_send()

    col = lambda c, ordr: W_PHASES * ordr[c // W_PHASES] + c % W_PHASES
    hbm = pl.BlockSpec(memory_space=pl.ANY)
    sums_blk = lambda c, ordr: (jnp.maximum(c // W_PHASES - (n_send + 1), 0), 0, jnp.where(c >= sums0, c % W_PHASES, 0))
    partial, _, got, sums = pl.pallas_call(
        body, name="grad_w_in",
        grid_spec=pltpu.PrefetchScalarGridSpec(
            num_scalar_prefetch=1, grid=(n_steps,),
            in_specs=[pl.BlockSpec((D, S), lambda c, ordr: (0, 0), pipeline_mode=pl.Buffered(1)),
                      pl.BlockSpec((S, TN), lambda c, ordr: (0, col(c, ordr)))],
            out_specs=[pl.BlockSpec((None, D, TN), lambda c, ordr: (ordr[c // W_PHASES], 0, c % W_PHASES)), hbm, hbm,
                       pl.BlockSpec((None, D, TN), sums_blk)],
            scratch_shapes=[pltpu.VMEM((2, D, TN), BF16), pltpu.VMEM((D, TN), BF16), pltpu.SemaphoreType.DMA((2,)),
                            pltpu.SemaphoreType.DMA((n_send,)), pltpu.SemaphoreType.DMA((n_send,)),
                            pltpu.SemaphoreType.DMA(())]),
        out_shape=[SDS((N_DEV, D, WB), F32), SDS((N_DEV, D, WB), BF16), SDS((n_send, D, WB), BF16),
                   SDS((n_send - 1, D, WB), BF16)],
        compiler_params=_params(("arbitrary",)),
    )(order, xt, dh)
    return partial, got, sums


def _grad_x(dz, dh, wgs, rider=None):
    tm = 1024

    def body(dz_ref, dh_ref, *rest):
        o_ref = rest[-1]

        @pl.when(pl.program_id(1) == 0)
        def _():
            o_ref[...] = ALPHA * dz_ref[...]

        acc = _dot(dh_ref[:, 0:TN], rest[0][...], NT)
        for s in range(1, W_PHASES):
            acc = acc + _dot(dh_ref[:, s * TN:(s + 1) * TN], rest[s][...], NT)
        o_ref[...] += acc

    outs, carried = _call_carrying(
        body, rider, (S // tm, N_DEV),
        [pl.BlockSpec((tm, D), lambda i, j: (i, 0)), pl.BlockSpec((tm, WB), lambda i, j: (i, j))]
        + [pl.BlockSpec((None, D, TN), lambda i, j: (j, 0, 0))] * W_PHASES,
        [pl.BlockSpec((tm, D), lambda i, j: (i, 0))], [SDS((S, D), F32)], [], [dz, dh] + list(wgs), "grad_x")
    return outs[0], carried


def _place():
    x, y, c = lax.axis_index("x"), lax.axis_index("y"), lax.axis_index("c")
    chips = [(x, y), (1 - x, y), (x, 1 - y), (1 - x, 1 - y)]
    return x, y, c, chips


def _blk(x, y, c):
    return 4 * x + 2 * y + c


def _adamw(w, g, m, v):
    m = ADAM_B1 * m + (1.0 - ADAM_B1) * g
    v = ADAM_B2 * v + (1.0 - ADAM_B2) * (g * g)
    m_hat = m / (1.0 - ADAM_B1 ** ADAM_STEP)
    v_hat = v / (1.0 - ADAM_B2 ** ADAM_STEP)
    delta = -ADAM_LR * (m_hat / (jnp.sqrt(v_hat) + ADAM_EPS) + ADAM_WD * w)
    return delta, m, v


class _Rider:
    def __init__(self, args, in_specs, out_shape, out_specs, scratch, start, finish):
        self.args, self.in_specs, self.out_shape, self.out_specs = args, in_specs, out_shape, out_specs
        self.scratch, self.start, self.finish = scratch, start, finish


def _carry(body, rider, n_in, n_out, first, last):
    if rider is None:
        return body
    r_in, r_out, r_scr = len(rider.args), len(rider.out_shape), len(rider.scratch)

    def carrying(*refs):
        o0 = n_in + r_in
        s0 = o0 + n_out + r_out
        s1 = len(refs) - r_scr
        theirs = (refs[n_in:o0], refs[o0 + n_out:s0], refs[s1:])
        pl.when(first())(lambda: rider.start(*theirs))
        body(*refs[:n_in], *refs[o0:o0 + n_out], *refs[s0:s1])
        pl.when(last())(lambda: rider.finish(*theirs))

    return carrying


def _call_carrying(body, rider, grid, in_specs, out_specs, out_shape, scratch, args, name, aliases=None):
    n_in, n_out = len(in_specs), len(out_specs)
    ids = lambda: [pl.program_id(a) for a in range(len(grid))]
    first = lambda: functools.reduce(jnp.logical_and, [i == 0 for i in ids()])
    last = lambda: functools.reduce(jnp.logical_and, [i == n - 1 for i, n in zip(ids(), grid)])
    if rider is not None:
        in_specs, args = in_specs + rider.in_specs, list(args) + rider.args
        out_specs, out_shape = out_specs + rider.out_specs, out_shape + rider.out_shape
        scratch = scratch + rider.scratch
    outs = pl.pallas_call(
        _carry(body, rider, n_in, n_out, first, last), name=name, grid=grid,
        in_specs=in_specs, out_specs=out_specs, out_shape=out_shape, scratch_shapes=scratch,
        input_output_aliases=aliases or {}, compiler_params=_params(("arbitrary",) * len(grid)),
    )(*args)
    return list(outs[:n_out]), list(outs[n_out:])


def _gather_copy(tensors, send_sems, recv_sems, t, k, block, to, src=None):
    dst = tensors[t][1](_blk(*block))
    return pltpu.make_async_remote_copy(
        src_ref=dst if src is None else src, dst_ref=dst,
        send_sem=send_sems.at[t, k], recv_sem=recv_sems.at[t, k], device_id=to, device_id_type=MESH)


def _gather_start(tensors, send_sems, recv_sems, local_sems):
    x, y, c, chips = _place()
    me, sib = (x, y, c), (x, y, 1 - c)
    for t, (src, dst) in enumerate(tensors):
        pltpu.make_async_copy(src, dst(_blk(*me)), local_sems.at[t]).start()
        _gather_copy(tensors, send_sems, recv_sems, t, 0, me, sib, src).start()
        for j in (1, 2, 3):
            _gather_copy(tensors, send_sems, recv_sems, t, j, me, (*chips[j], c), src).start()


def _gather_finish(tensors, send_sems, recv_sems, local_sems):
    x, y, c, chips = _place()
    me, sib = (x, y, c), (x, y, 1 - c)
    copy = functools.partial(_gather_copy, tensors, send_sems, recv_sems)
    for t in range(len(tensors)):
        for j in (1, 2, 3):
            copy(t, j, (*chips[j], c), me).wait_recv()
            copy(t, 3 + j, (*chips[j], c), sib).start()
    for t, (src, dst) in enumerate(tensors):
        copy(t, 0, sib, me).wait_recv()
        for j in (1, 2, 3):
            copy(t, 3 + j, (*chips[j], 1 - c), me).wait_recv()
        copy(t, 0, me, sib, src).wait_send()
        for j in (1, 2, 3):
            copy(t, j, me, (*chips[j], c), src).wait_send()
            copy(t, 3 + j, (*chips[j], c), sib).wait_send()
        pltpu.make_async_copy(src, dst(_blk(*me)), local_sems.at[t]).wait()


W_PHASES = WB // TN


def _cast_shards(w_in_s, w_out_s, w_pool_s):
    def body(*refs):
        for src, dst in zip(refs[:3], refs[3:]):
            dst[...] = src[...].astype(BF16)

    vmem = pl.BlockSpec(memory_space=pltpu.VMEM)
    return pl.pallas_call(
        body, name="cast_shards", in_specs=[vmem] * 3, out_specs=[vmem] * 3,
        out_shape=[SDS(w.shape, BF16) for w in (w_in_s, w_out_s, w_pool_s)],
        compiler_params=_params(),
    )(w_in_s, w_out_s, w_pool_s)


def _gather_w_in_rider(shard16, s):
    def tensors(ins, outs):
        return [(ins[0].at[:, pl.ds(s * TN, TN)], lambda b: outs[0].at[b])]

    hbm = pl.BlockSpec(memory_space=pl.ANY)
    return _Rider(
        args=[shard16], in_specs=[hbm], out_shape=[SDS((N_DEV, D, TN), BF16)], out_specs=[hbm],
        scratch=[pltpu.SemaphoreType.DMA((1, 7)), pltpu.SemaphoreType.DMA((1, 7)), pltpu.SemaphoreType.DMA((1,))],
        start=lambda ins, outs, scr: _gather_start(tensors(ins, outs), *scr),
        finish=lambda ins, outs, scr: _gather_finish(tensors(ins, outs), *scr))


def _gather_small_rider(w_out16, w_pool16):
    def tensors(ins, outs):
        gout_ref, gpool_ref = outs

        def pool_rows(b):
            return gpool_ref.at[:, pl.ds(pl.multiple_of(b * PB, PB), PB), :]

        return [(ins[0], lambda b: gout_ref.at[b]), (ins[1], pool_rows)]

    hbm = pl.BlockSpec(memory_space=pl.ANY)
    return _Rider(
        args=[w_out16, w_pool16], in_specs=[hbm, hbm],
        out_shape=[SDS((N_DEV, OB, D), BF16), SDS((N_GROUPS, GC, GC), BF16)], out_specs=[hbm, hbm],
        scratch=[pltpu.SemaphoreType.DMA((2, 7)), pltpu.SemaphoreType.DMA((2, 7)), pltpu.SemaphoreType.DMA((2,))],
        start=lambda ins, outs, scr: _gather_start(tensors(ins, outs), *scr),
        finish=lambda ins, outs, scr: _gather_finish(tensors(ins, outs), *scr))


def _block_table():
    x, y, c, chips = _place()
    return jnp.stack([_blk(*chip, c) for chip in chips]).astype(jnp.int32)


def _sibling_first_order():
    x, y, c, chips = _place()
    return jnp.stack([_blk(*chip, side) for side in (1 - c, c) for chip in chips]).astype(jnp.int32)


def _exchange_rider(p16):
    _, r_tot, cols = p16.shape

    def copies(ins, outs, scr):
        x, y, c, chips = _place()
        return [pltpu.make_async_remote_copy(
            src_ref=ins[0].at[_blk(*chips[k], 1 - c)], dst_ref=outs[0].at[k],
            send_sem=scr[0].at[k], recv_sem=scr[1].at[k], device_id=(x, y, 1 - c), device_id_type=MESH)
            for k in range(4)]

    def start(ins, outs, scr):
        for cp in copies(ins, outs, scr):
            cp.start()

    def finish(ins, outs, scr):
        for cp in copies(ins, outs, scr):
            cp.wait()

    hbm = pl.BlockSpec(memory_space=pl.ANY)
    return _Rider([p16], [hbm], [SDS((4, r_tot, cols), BF16)], [hbm],
                  [pltpu.SemaphoreType.DMA((4,)), pltpu.SemaphoreType.DMA((4,))], start, finish)


def _chip_sums(name, table, p32, r1, rc):
    _, r_tot, cols = p32.shape

    def body(tbl_ref, p_ref, r_ref, o_ref):
        o_ref[...] = (p_ref[...] + r_ref[...].astype(F32)).astype(BF16)

    return pl.pallas_call(
        body, name=name,
        grid_spec=pltpu.PrefetchScalarGridSpec(
            num_scalar_prefetch=1, grid=(3, r_tot // rc),
            in_specs=[pl.BlockSpec((None, rc, cols), lambda k, ch, tbl: (tbl[k + 1], ch, 0)),
                      pl.BlockSpec((None, rc, cols), lambda k, ch, tbl: (k + 1, ch, 0))],
            out_specs=pl.BlockSpec((None, rc, cols), lambda k, ch, tbl: (k, ch, 0))),
        out_shape=SDS((3, r_tot, cols), BF16),
        compiler_params=_params(("arbitrary", "arbitrary")),
    )(table, p32, r1)


def _stage2_rider(sums, stats=None):
    n_t = len(sums)

    def copies(ins, outs, scr):
        x, y, c, chips = _place()
        out = []
        for t in range(n_t):
            for k in (1, 2, 3):
                out.append(pltpu.make_async_remote_copy(
                    src_ref=ins[t].at[k - 1], dst_ref=outs[t].at[k - 1],
                    send_sem=scr[0].at[t, k - 1], recv_sem=scr[1].at[t, k - 1],
                    device_id=(*chips[k], c), device_id_type=MESH))
        if stats is not None:
            for k in range(1, N_DEV):
                peer = (x ^ ((k >> 2) & 1), y ^ ((k >> 1) & 1), c ^ (k & 1))
                out.append(pltpu.make_async_remote_copy(
                    src_ref=scr[4], dst_ref=outs[n_t].at[_blk(x, y, c)],
                    send_sem=scr[2].at[k - 1], recv_sem=scr[3].at[k - 1], device_id=peer, device_id_type=MESH))
        return out

    def own_rows(outs, scr):
        x, y, c, _ = _place()
        return pltpu.make_async_copy(scr[4], outs[n_t].at[_blk(x, y, c)], scr[5])

    def start(ins, outs, scr):
        if stats is not None:
            scr[4][...] = ins[n_t][...]
            scr[4][3:4, 0:DP] = ins[n_t + 1][0:1, :]
            own_rows(outs, scr).start()
        for cp in copies(ins, outs, scr):
            cp.start()

    def finish(ins, outs, scr):
        for cp in copies(ins, outs, scr):
            cp.wait()
        if stats is not None:
            own_rows(outs, scr).wait()

    vmem = pl.BlockSpec(memory_space=pltpu.VMEM)
    hbm = pl.BlockSpec(memory_space=pl.ANY)
    scratch = [pltpu.SemaphoreType.DMA((n_t, 3)), pltpu.SemaphoreType.DMA((n_t, 3))]
    args, in_specs = list(sums), [hbm] * n_t
    out_shape, out_specs = [SDS(s.shape, BF16) for s in sums], [hbm] * n_t
    if stats is not None:
        scratch += [pltpu.SemaphoreType.DMA((N_DEV - 1,)), pltpu.SemaphoreType.DMA((N_DEV - 1,)),
                    pltpu.VMEM((8, D), F32), pltpu.SemaphoreType.DMA(())]
        args, in_specs = args + list(stats), in_specs + [vmem, vmem]
        out_shape, out_specs = out_shape + [SDS((N_DEV, 8, D), F32)], out_specs + [hbm]
    return _Rider(args, in_specs, out_shape, out_specs, scratch, start, finish)


def _adamw_shard(name, table, p32, r1, r2, w, m, v, rc):
    _, r_tot, cols = p32.shape

    def body(tbl_ref, p_ref, r1_ref, r2_ref, w_ref, m_ref, v_ref, g_ref, d_ref, nm_ref, nv_ref):
        g = p_ref[...] + r1_ref[...].astype(F32)
        for k in range(3):
            g = g + r2_ref[k].astype(F32)
        delta, nm, nv = _adamw(w_ref[...], g, m_ref[...], v_ref[...])
        g_ref[...] = g
        d_ref[...] = delta
        nm_ref[...] = nm
        nv_ref[...] = nv

    rows = pl.BlockSpec((rc, cols), lambda ch, tbl: (ch, 0))
    shard = SDS((r_tot, cols), F32)
    return pl.pallas_call(
        body, name=name,
        grid_spec=pltpu.PrefetchScalarGridSpec(
            num_scalar_prefetch=1, grid=(r_tot // rc,),
            in_specs=[pl.BlockSpec((None, rc, cols), lambda ch, tbl: (tbl[0], ch, 0)),
                      pl.BlockSpec((None, rc, cols), lambda ch, tbl: (0, ch, 0)),
                      pl.BlockSpec((3, rc, cols), lambda ch, tbl: (0, ch, 0)), rows, rows, rows],
            out_specs=[rows, rows, rows, rows]),
        out_shape=[shard, shard, shard, shard],
        compiler_params=_params(("arbitrary",)),
    )(table, p32, r1, r2, w, m, v)


def _replicated_adamw(gathered, gain, bias, scale, m_gain, m_bias, m_scale, v_gain, v_bias, v_scale):
    def body(all_ref, s_ref, g_ref, b_ref, ms_ref, mg_ref, mb_ref, vs_ref, vg_ref, vb_ref, loss_ref, *outs):
        tot = all_ref[0]
        for b in range(1, N_DEV):
            tot = tot + all_ref[b]
        loss_ref[...] = tot[2:3, 0:LANES]
        for n, (row, width, w_r, m_r, v_r) in enumerate(((3, DP, s_ref, ms_ref, vs_ref), (0, D, g_ref, mg_ref, vg_ref),
                                                         (1, D, b_ref, mb_ref, vb_ref))):
            g = tot[row:row + 1, 0:width]
            delta, nm, nv = _adamw(w_r[...], g, m_r[...], v_r[...])
            for out, val in zip(outs[4 * n:4 * n + 4], (g, delta, nm, nv)):
                out[...] = val

    vmem = pl.BlockSpec(memory_space=pltpu.VMEM)
    shapes = [SDS((1, LANES), F32)] + [SDS((1, width), F32) for width in (DP, D, D) for _ in range(4)]
    return pl.pallas_call(
        body, name="replicated_adamw",
        in_specs=[vmem] * 10, out_specs=[vmem] * len(shapes), out_shape=shapes,
        compiler_params=_params(),
    )(gathered, scale, gain, bias, m_scale, m_gain, m_bias, v_scale, v_gain, v_bias)


def kernel(x, w_in, w_pool, pool_scale, w_out, ln_gain, ln_bias, loss_target, m_w_in, m_w_pool, m_pool_scale, m_w_out, m_ln_gain, m_ln_bias, v_w_in, v_w_pool, v_pool_scale, v_w_out, v_ln_gain, v_ln_bias):
    pool_rows = (N_GROUPS * PB, GC)
    x2, target = x[0], loss_target[0]
    rope = _rope_tables()
    table = _block_table()

    shard16, w_out16, w_pool16 = _cast_shards(w_in[0], w_out[0], w_pool[0])
    (xb, xt), (wg0,) = _prep_x(x2, _gather_w_in_rider(shard16, 0))
    h, (wg1,) = _proj_phase(xb, wg0, 0, rope, None, _gather_w_in_rider(shard16, 1))
    h, (wg2,) = _proj_phase(xb, wg1, 1, rope, h, _gather_w_in_rider(shard16, 2))
    h, (wg_out, wg_pool) = _proj_phase(xb, wg2, 2, rope, h, _gather_small_rider(w_out16, w_pool16))
    wg_out = wg_out.reshape(D, D)
    h16, hug = [h[0][None], h[1], h[2]], h[3]
    fwd = [_attn_fwd(h16[n], f"attn_fwd_d{d}") for n, d in enumerate(DILATIONS)]
    y, yt, attn, lse1, lse4, lse16 = _mix(fwd[0][0][0], fwd[0][1][0], *fwd[1], *fwd[2], hug, wg_pool, pool_scale)
    dz, dzb, stats = _outproj_ln(y, wg_out, x2, target, ln_gain, ln_bias)

    dwout, dwout16 = _grad_w_out(yt, dzb)
    mid, (r1_out,) = _bwd_mid(dzb, wg_out, hug, attn, wg_pool, pool_scale, rider=_exchange_rider(dwout16))
    dh, do1, do4, do16, dd1, dd4, dd16, dwp, dwp16, gps = mid
    s2_out = _chip_sums("rs_out_sums", table, dwout, r1_out, OB)

    do, lse, dd = [do1[None], do4, do16], [lse1[None], lse4, lse16], [dd1[None], dd4, dd16]
    acc4, (r2_out,) = _attn_bwd(h16[1], do[1], lse[1], dd[1], rope, "attn_bwd_d4", rider=_stage2_rider([s2_out]))
    g_out, d_out, nm_out, nv_out = _adamw_shard(
        "adamw_w_out", table, dwout, r1_out, r2_out, w_out[0], m_w_out[0], v_w_out[0], OB // 2)
    acc16, (r1_pool,) = _attn_bwd(h16[2], do[2], lse[2], dd[2], rope, "attn_bwd_d16", rider=_exchange_rider(dwp16))
    s2_pool = _chip_sums("rs_pool_sums", table, dwp, r1_pool, N_GROUPS * PB)
    dh = _attn_bwd(h16[0], do[0], lse[0], dd[0], rope, "attn_bwd_d1", others=(acc4, acc16), dh=dh)[0]

    dwin, r1_in, s2_in = _grad_w_in(xt, dh, _sibling_first_order())
    grad_x, (r2_in, r2_pool, gathered) = _grad_x(
        dz, dh, (wg0, wg1, wg2), rider=_stage2_rider([s2_in, s2_pool], stats=(stats, gps)))
    g_in, d_in, nm_in, nv_in = _adamw_shard(
        "adamw_w_in", table, dwin, r1_in, r2_in, w_in[0], m_w_in[0], v_w_in[0], 256)
    g_pool, d_pool, nm_pool, nv_pool = _adamw_shard(
        "adamw_w_pool", table, dwp, r1_pool, r2_pool, w_pool[0].reshape(pool_rows), m_w_pool[0].reshape(pool_rows),
        v_w_pool[0].reshape(pool_rows), N_GROUPS * PB)
    loss, *small = _replicated_adamw(gathered, ln_gain, ln_bias, pool_scale, m_ln_gain, m_ln_bias, m_pool_scale,
                                     v_ln_gain, v_ln_bias, v_pool_scale)
    (g_ps, d_ps, nm_ps, nv_ps, g_gain, d_gain, nm_gain, nv_gain, g_bias, d_bias, nm_bias, nv_bias) = small

    shard4 = lambda t: t.reshape(1, N_GROUPS, PB, GC)
    lead = lambda t: t[None]
    return (loss[0, 0], lead(grad_x),
            lead(g_in), shard4(g_pool), g_ps, lead(g_out), g_gain, g_bias,
            lead(d_in), shard4(d_pool), d_ps, lead(d_out), d_gain, d_bias,
            lead(nm_in), shard4(nm_pool), nm_ps, lead(nm_out), nm_gain, nm_bias,
            lead(nv_in), shard4(nv_pool), nv_ps, lead(nv_out), nv_gain, nv_bias)
```

```python
import functools

import jax
import jax.numpy as jnp
from jax import lax
from jax.experimental import pallas as pl
from jax.experimental.pallas import tpu as pltpu

F32 = jnp.float32
BF16 = jnp.bfloat16
SDS = jax.ShapeDtypeStruct
MESH = pl.DeviceIdType.MESH

N_DEV = 8
S = 4096
D = 2048
N_HEADS = 8
DH = 128
DA = N_HEADS * DH
DP = 1024
N_GROUPS = 4
GC = DP // N_GROUPS
POOL_WINDOWS = (2, 4, 8, 16)
HALO = 16
D_IN = 3 * DA + DP + D
WB = D_IN // N_DEV
TN = 256
HW = 3 * DA
OB = D // N_DEV
PB = GC // N_DEV
ROPE_DIM = DH // 4
ROPE_HALF = ROPE_DIM // 2
ROPE_THETA = 500000.0
DILATIONS = (1, 4, 16)
KB = 128
LN_EPS = 1e-5
ALPHA = 2.0 ** 0.25
SCALE = DH ** -0.5
NEG = -1e30
ADAM_LR, ADAM_B1, ADAM_B2, ADAM_EPS, ADAM_WD, ADAM_STEP = 0.001, 0.9, 0.999, 1e-08, 0.01, 10

VMEM_LIMIT_V7X = 61 * 1024 * 1024

NT = (((1,), (1,)), ((), ()))
T_N = (((0,), (0,)), ((), ()))


def _params(sem=None):
    return pltpu.CompilerParams(dimension_semantics=sem, vmem_limit_bytes=VMEM_LIMIT_V7X)


def _dot(a, b, dims=None):
    if dims is None:
        return jnp.dot(a, b, preferred_element_type=F32)
    return lax.dot_general(a, b, dims, preferred_element_type=F32)


def _rope_tables():
    inv_freq = ROPE_THETA ** (-(2.0 * jnp.arange(ROPE_HALF, dtype=F32)) / ROPE_DIM)
    ang = jnp.arange(S, dtype=jnp.int32).astype(F32)[:, None] * inv_freq[None, :]
    cos, sin = lax.optimization_barrier((jnp.cos(ang), jnp.sin(ang)))
    rest = DH - ROPE_DIM
    c = jnp.concatenate([cos, cos, jnp.ones((S, rest), F32)], axis=1)
    sn = jnp.concatenate([-sin, sin, jnp.zeros((S, rest), F32)], axis=1)
    return c, sn


def _rope_partner(t):
    lane = lax.broadcasted_iota(jnp.int32, t.shape, 1)
    return jnp.where(lane < ROPE_HALF, pltpu.roll(t, DH - ROPE_HALF, 1), pltpu.roll(t, ROPE_HALF, 1))


def _rope(t, c, sn):
    return t * c + _rope_partner(t) * sn


def _rope_t(g, c, sn):
    return g * c - _rope_partner(g) * sn


def _prep_x(x, rider):
    tm = 512

    def body(x_ref, xb_ref, xt_ref):
        xv = x_ref[...]
        xb_ref[...] = xv.astype(BF16)
        xt_ref[...] = xv.T.astype(BF16)

    return _call_carrying(
        body, rider, (S // tm,), [pl.BlockSpec((tm, D), lambda i: (i, 0))],
        [pl.BlockSpec((tm, D), lambda i: (i, 0)), pl.BlockSpec((D, tm), lambda i: (0, i))],
        [SDS((S, D), BF16), SDS((D, S), BF16)], [], [x], "prep_x")


def _residues(slab_ref, idx, d, r, n):
    return slab_ref[(*idx, pl.ds(r, n, stride=d), slice(None))]


def _proj_phase(xb, wg, s, rope, prev, rider):
    n_heads = TN // DH
    n16 = HW // TN
    j16 = n16 // W_PHASES
    n_rope = 2 * DA // TN
    once = pl.Buffered(1)

    def body(x_ref, w_ref, c_ref, sn_ref, *rest):
        h16_ref, h4_ref, h16r_ref, hug_ref, slab_ref, res_ref = rest[-6:]
        j = pl.program_id(0)
        d4, d16 = DILATIONS[1], DILATIONS[2]
        acc = _dot(x_ref[...], w_ref[...])

        def heads(with_rope):
            for hh in range(n_heads):
                hs = slice(hh * DH, (hh + 1) * DH)
                t = _rope(acc[:, hs], c_ref[...], sn_ref[...]) if with_rope else acc[:, hs]
                h16_ref[:, hs] = t.astype(BF16)
                slab_ref[...] = t
                for q in range(d4):
                    t4 = _residues(slab_ref, (), d4, q, S // d4)
                    h4_ref[q, :, hs] = t4.astype(BF16)
                    res_ref[...] = t4
                    for m in range(d16 // d4):
                        h16r_ref[d4 * m + q, :, hs] = _residues(res_ref, (), d4, m, S // d16).astype(BF16)

        pl.when(W_PHASES * j + s < n_rope)(lambda: heads(True))
        pl.when((W_PHASES * j + s >= n_rope) & (j < j16))(lambda: heads(False))

        @pl.when(j >= j16)
        def _():
            hug_ref[...] = acc.astype(BF16)

    col16 = lambda j: W_PHASES * jnp.minimum(j, j16 - 1) + s
    col32 = lambda j: W_PHASES * jnp.maximum(j, j16) + s - n16
    hbm = pl.BlockSpec(memory_space=pl.ANY)
    in_specs = [pl.BlockSpec((S, D), lambda j: (0, 0), pipeline_mode=once),
                pl.BlockSpec((None, D, TN), lambda j: (j, 0, 0))]
    in_specs += [pl.BlockSpec((S, DH), lambda j: (0, 0), pipeline_mode=once)] * 2
    args = [xb, wg] + list(rope)
    aliases = {}
    if prev is not None:
        aliases = {len(args) + n: n for n in range(4)}
        in_specs, args = in_specs + [hbm] * 4, args + list(prev)
    out_specs = [pl.BlockSpec((S, TN), lambda j: (0, col16(j)))]
    out_specs += [pl.BlockSpec((d, S // d, TN), lambda j: (0, 0, col16(j))) for d in DILATIONS[1:]]
    out_specs += [pl.BlockSpec((S, TN), lambda j: (0, col32(j)))]
    out_shape = [SDS((S, HW), BF16)] + [SDS((d, S // d, HW), BF16) for d in DILATIONS[1:]] + [SDS((S, HW), BF16)]
    return _call_carrying(
        body, rider, (N_DEV,), in_specs, out_specs, out_shape,
        [pltpu.VMEM((S, DH), F32), pltpu.VMEM((S // DILATIONS[1], DH), F32)], args,
        f"proj_phase{s}", aliases)


BQ = 2 * KB
LANES = 128


def _to_lane(acc, col, h):
    lane = lax.broadcasted_iota(jnp.int32, acc.shape, 1)
    return jnp.where(lane == h, col, acc)


def _attn_fwd(h16, name):
    d, n_sub, _ = h16.shape
    n_i = n_sub // BQ
    kw = KB + BQ

    def prev(i):
        return jnp.maximum(2 * i - 1, 0)

    def body(q_ref, kc_ref, kp_ref, vc_ref, vp_ref, o_ref, l_ref, kw_ref, vw_ref):
        i = pl.program_id(1)
        kw_ref[0:KB, :] = kp_ref[...]
        kw_ref[KB:kw, :] = kc_ref[...]
        vw_ref[0:KB, :] = vp_ref[...]
        vw_ref[KB:kw, :] = vc_ref[...]
        a = lax.broadcasted_iota(jnp.int32, (KB, 2 * KB), 0)
        b = lax.broadcasted_iota(jnp.int32, (KB, 2 * KB), 1)
        band = (b >= a) & (b <= a + KB)
        first_key = jnp.where(i == 0, KB, 0)
        masks = (band & (b >= first_key), band)
        for half in range(2):
            rs = slice(half * KB, (half + 1) * KB)
            ks = slice(half * KB, (half + 2) * KB)
            lse = jnp.zeros((KB, LANES), F32)
            for h in range(N_HEADS):
                hs = slice(h * DH, (h + 1) * DH)
                s = jnp.where(masks[half], _dot(q_ref[rs, hs], kw_ref[ks, hs], NT) * SCALE, NEG)
                m = jnp.max(s, axis=1, keepdims=True)
                p = jnp.exp(s - m)
                den = jnp.sum(p, axis=1, keepdims=True)
                o_ref[rs, hs] = (_dot(p.astype(BF16), vw_ref[ks, hs]) / den).astype(BF16)
                lse = _to_lane(lse, m + jnp.log(den), h)
            l_ref[rs, :] = lse

    return pl.pallas_call(
        body, name=name, grid=(d, n_i),
        in_specs=[pl.BlockSpec((None, BQ, DA), lambda r, i: (r, i, 0)),
                  pl.BlockSpec((None, BQ, DA), lambda r, i: (r, i, 1)),
                  pl.BlockSpec((None, KB, DA), lambda r, i: (r, prev(i), 1)),
                  pl.BlockSpec((None, BQ, DA), lambda r, i: (r, i, 2)),
                  pl.BlockSpec((None, KB, DA), lambda r, i: (r, prev(i), 2))],
        out_specs=[pl.BlockSpec((None, BQ, DA), lambda r, i: (r, i, 0)),
                   pl.BlockSpec((None, BQ, LANES), lambda r, i: (r, i, 0))],
        out_shape=[SDS((d, n_sub, DA), BF16), SDS((d, n_sub, LANES), F32)],
        scratch_shapes=[pltpu.VMEM((kw, DA), BF16), pltpu.VMEM((kw, DA), BF16)],
        compiler_params=_params(("arbitrary", "arbitrary")),
    )(h16, h16, h16, h16, h16)


def _pooled(ext_ref, g, rows, tm):
    w = POOL_WINDOWS[g]
    cs = slice(g * GC, (g + 1) * GC)
    cur = ext_ref[HALO:HALO + tm, cs]
    win = cur
    for j in range(1, w):
        win = win + ext_ref[HALO - j:HALO - j + tm, cs]
    cnt = jnp.minimum(rows + 1, w).astype(F32)
    return win / cnt - cur, cnt


def _fill_ext(ext_ref, u_ref, uh_ref, blk, tm):
    @pl.when(blk == 0)
    def _():
        ext_ref[0:HALO, :] = jnp.zeros((HALO, DP), F32)

    @pl.when(blk > 0)
    def _():
        ext_ref[0:HALO, :] = uh_ref[...].astype(F32)

    ext_ref[HALO:HALO + tm, :] = u_ref[...].astype(F32)


def _residue_specs(tm, width):
    return [pl.BlockSpec((d, tm // d, width), lambda i: (0, i, 0)) for d in DILATIONS[1:]]


def _mix(o1, l1, o4, l4, o16, l16, hug, wp, scale):
    tm = 256
    n_slab = N_HEADS + 1

    def body(o1r, l1r, o4r, l4r, o16r, l16r, u_ref, uh_ref, ga_ref, gp_ref, wp_ref, sc_ref,
             y_ref, yt_ref, attn_ref, lse_ref, lse4_ref, lse16_ref, ext_ref, ys_ref, nat_ref, ls_ref):
        i = pl.program_id(0)
        for n, (d, o_r, l_r) in enumerate(((DILATIONS[1], o4r, l4r), (DILATIONS[2], o16r, l16r))):
            for r in range(d):
                rows = pl.ds(r, tm // d, stride=d)
                for h in range(N_HEADS):
                    nat_ref[n, h, rows, :] = o_r[r, :, h * DH:(h + 1) * DH].astype(F32)
                nat_ref[n, N_HEADS, rows, :] = l_r[r]
        la, lb, lc = l1r[...], nat_ref[0, N_HEADS], nat_ref[1, N_HEADS]
        mx = jnp.maximum(jnp.maximum(la, lb), lc)
        ea, eb, ec = jnp.exp(la - mx), jnp.exp(lb - mx), jnp.exp(lc - mx)
        z = ea + eb + ec
        wa, wb, wc = ea / z, eb / z, ec / z
        lse = mx + jnp.log(z)
        lse_ref[...] = lse
        ls_ref[...] = lse
        for d, out in ((DILATIONS[1], lse4_ref), (DILATIONS[2], lse16_ref)):
            for r in range(d):
                out[r] = ls_ref[pl.ds(r, tm // d, stride=d), :]
        for h in range(N_HEADS):
            hs = slice(h * DH, (h + 1) * DH)
            hc = slice(h, h + 1)
            attn = wa[:, hc] * o1r[:, hs].astype(F32) + wb[:, hc] * nat_ref[0, h] + wc[:, hc] * nat_ref[1, h]
            attn_ref[:, hs] = attn.astype(BF16)
            gt = ga_ref[:, hs].astype(F32)
            ys_ref[:, hs] = attn * (gt * jax.nn.sigmoid(gt))

        _fill_ext(ext_ref, u_ref, uh_ref, i, tm)
        rows = i * tm + lax.broadcasted_iota(jnp.int32, (tm, 1), 0)
        for g in range(N_GROUPS):
            cs = slice(g * GC, (g + 1) * GC)
            gs = slice(DA + g * GC, DA + (g + 1) * GC)
            pooled, _ = _pooled(ext_ref, g, rows, tm)
            po = _dot(pooled.astype(BF16), wp_ref[g]) * sc_ref[:, cs]
            gt = gp_ref[:, cs].astype(F32)
            ys_ref[:, gs] = po * (gt * jax.nn.sigmoid(gt))
        yv = ys_ref[...]
        y_ref[...] = yv.astype(BF16)
        yt_ref[...] = yv.T.astype(BF16)

    row = lambda i: (i, 0)
    blk = pl.BlockSpec((tm, DA), row)
    lanes = pl.BlockSpec((tm, LANES), row)
    o_res, l_res = _residue_specs(tm, DA), _residue_specs(tm, LANES)
    return pl.pallas_call(
        body, name="mix", grid=(S // tm,),
        in_specs=[blk, lanes, o_res[0], l_res[0], o_res[1], l_res[1],
                  pl.BlockSpec((tm, DP), row),
                  pl.BlockSpec((HALO, DP), lambda i: (jnp.maximum(i * (tm // HALO) - 1, 0), 0)),
                  pl.BlockSpec((tm, DA), lambda i: (i, 1)), pl.BlockSpec((tm, DP), lambda i: (i, 2)),
                  pl.BlockSpec((N_GROUPS, GC, GC), lambda i: (0, 0, 0)),
                  pl.BlockSpec((1, DP), lambda i: (0, 0))],
        out_specs=[pl.BlockSpec((tm, D), row), pl.BlockSpec((D, tm), lambda i: (0, i)), blk, lanes] + l_res,
        out_shape=[SDS((S, D), BF16), SDS((D, S), BF16), SDS((S, DA), BF16), SDS((S, LANES), F32)]
        + [SDS((d, S // d, LANES), F32) for d in DILATIONS[1:]],
        scratch_shapes=[pltpu.VMEM((HALO + tm, DP), F32), pltpu.VMEM((tm, D), F32),
                        pltpu.VMEM((2, n_slab, tm, DH), F32), pltpu.VMEM((tm, LANES), F32)],
        compiler_params=_params(("arbitrary",)),
    )(o1, l1, o4, l4, o16, l16, hug, hug, hug, hug, wp, scale)


def _outproj_ln(y, wout, x, target, gain, bias):
    tm = 512
    te = 128

    def body(y_ref, w_ref, x_ref, t_ref, g_ref, b_ref, dz_ref, dzb_ref, st_ref, out_ref):
        i = pl.program_id(0)

        @pl.when(i == 0)
        def _():
            st_ref[...] = jnp.zeros((8, D), F32)

        out_ref[...] = _dot(y_ref[...], w_ref[...])
        gn = g_ref[...]
        for e in range(tm // te):
            rs = slice(e * te, (e + 1) * te)
            z = ALPHA * x_ref[rs, :] + out_ref[rs, :]
            mu = jnp.mean(z, axis=1, keepdims=True)
            zc = z - mu
            var = jnp.mean(zc * zc, axis=1, keepdims=True)
            rstd = lax.rsqrt(var + LN_EPS)
            xhat = zc * rstd
            diff = xhat * gn + b_ref[...] - t_ref[rs, :]
            dyln = diff / D
            st_ref[0:1, :] += jnp.sum(dyln * xhat, axis=0, keepdims=True)
            st_ref[1:2, :] += jnp.sum(dyln, axis=0, keepdims=True)
            row_loss = jnp.sum(diff * diff, axis=1, keepdims=True) / D
            st_ref[2:3, :] += jnp.broadcast_to(0.5 * jnp.sum(row_loss, axis=0, keepdims=True), (1, D))
            dxh = dyln * gn
            m1 = jnp.mean(dxh, axis=1, keepdims=True)
            m2 = jnp.mean(dxh * xhat, axis=1, keepdims=True)
            dz = rstd * (dxh - m1 - xhat * m2)
            dz_ref[rs, :] = dz
            dzb_ref[rs, :] = dz.astype(BF16)

    row = lambda i: (i, 0)
    const = lambda i: (0, 0)
    return pl.pallas_call(
        body, name="outproj_ln", grid=(S // tm,),
        in_specs=[pl.BlockSpec((tm, D), row),
                  pl.BlockSpec((D, D), const, pipeline_mode=pl.Buffered(1)),
                  pl.BlockSpec((tm, D), row), pl.BlockSpec((tm, D), row),
                  pl.BlockSpec((1, D), const), pl.BlockSpec((1, D), const)],
        out_specs=[pl.BlockSpec((tm, D), row), pl.BlockSpec((tm, D), row), pl.BlockSpec((8, D), const)],
        out_shape=[SDS((S, D), F32), SDS((S, D), BF16), SDS((8, D), F32)],
        scratch_shapes=[pltpu.VMEM((tm, D), F32)],
        compiler_params=_params(("arbitrary",)),
    )(y, wout, x, target, gain, bias)


def _bwd_mid(dzb, wout, hug, attn, wp, scale, rider=None):
    tm = 256
    n = S // tm

    def body(dz_ref, w_ref, ga_ref, gp_ref, at_ref, u_ref, uh_ref, wp_ref, sc_ref,
             dh_ref, do_ref, do4_ref, do16_ref, dd_ref, dd4_ref, dd16_ref, dwp_ref, dwp16_ref, gps_ref,
             ext_ref, eext_ref, acc_ref, nat_ref, ds_ref):
        i = pl.program_id(0)
        ib = n - 1 - i

        @pl.when(i == 0)
        def _():
            eext_ref[tm:tm + HALO, :] = jnp.zeros((HALO, DP), F32)
            acc_ref[...] = jnp.zeros((N_GROUPS, GC, GC), F32)
            gps_ref[...] = jnp.zeros((8, DP), F32)

        dy = _dot(dz_ref[...], w_ref[...], NT)

        def through_gate(dy_part, gt):
            sg = jax.nn.sigmoid(gt)
            return dy_part * (gt * sg), dy_part * (sg * (1.0 + gt * (1.0 - sg)))

        dat, dyg_a = through_gate(dy[:, 0:DA], ga_ref[...].astype(F32))
        dmix_p, dyg_p = through_gate(dy[:, DA:D], gp_ref[...].astype(F32))

        at = at_ref[...].astype(F32)
        do_ref[...] = dat.astype(BF16)
        dh_ref[:, DP:DP + DA] = (dyg_a * at).astype(BF16)
        prod = dat * at
        dd = jnp.zeros((tm, LANES), F32)
        for h in range(N_HEADS):
            hs = slice(h * DH, (h + 1) * DH)
            dd = _to_lane(dd, jnp.sum(prod[:, hs], axis=1, keepdims=True), h)
            nat_ref[h] = dat[:, hs]
        dd_ref[...] = dd
        ds_ref[...] = dd
        for d, do_out, dd_out in ((DILATIONS[1], do4_ref, dd4_ref), (DILATIONS[2], do16_ref, dd16_ref)):
            for r in range(d):
                dd_out[r] = _residues(ds_ref, (), d, r, tm // d)
                for h in range(N_HEADS):
                    do_out[r, :, h * DH:(h + 1) * DH] = _residues(nat_ref, (h,), d, r, tm // d).astype(BF16)

        _fill_ext(ext_ref, u_ref, uh_ref, ib, tm)
        rows = ib * tm + lax.broadcasted_iota(jnp.int32, (tm, 1), 0)
        for g in range(N_GROUPS):
            w = POOL_WINDOWS[g]
            cs = slice(g * GC, (g + 1) * GC)
            pooled, cnt = _pooled(ext_ref, g, rows, tm)
            pre = _dot(pooled.astype(BF16), wp_ref[g])
            sc = sc_ref[:, cs]
            dpo = dmix_p[:, cs]
            gps_ref[0:1, cs] += jnp.sum(dpo * pre, axis=0, keepdims=True)
            dh_ref[:, DP + DA + g * GC:DP + DA + (g + 1) * GC] = (dyg_p[:, cs] * (pre * sc)).astype(BF16)
            dpre = (dpo * sc).astype(BF16)
            acc_ref[g] += _dot(pooled.T.astype(BF16), dpre)
            dpooled = _dot(dpre, wp_ref[g], NT)
            eext_ref[0:tm, cs] = dpooled / cnt
            du = eext_ref[0:tm, cs]
            for j in range(1, w):
                du = du + eext_ref[j:j + tm, cs]
            dh_ref[:, cs] = (du - dpooled).astype(BF16)
        eext_ref[tm:tm + HALO, :] = eext_ref[0:HALO, :]

        @pl.when(i == n - 1)
        def _():
            for j in range(N_DEV):
                for g in range(N_GROUPS):
                    blk = acc_ref[g, j * PB:(j + 1) * PB, :]
                    dwp_ref[j, g * PB:(g + 1) * PB, :] = blk
                    dwp16_ref[j, g * PB:(g + 1) * PB, :] = blk.astype(BF16)

    rev = lambda i: (n - 1 - i, 0)
    const = lambda i: (0, 0)
    res = lambda width: [pl.BlockSpec((d, tm // d, width), lambda i: (0, n - 1 - i, 0)) for d in DILATIONS[1:]]
    pool_blocks = pl.BlockSpec((N_DEV, N_GROUPS * PB, GC), lambda i: (0, 0, 0))
    outs, carried = _call_carrying(
        body, rider, (n,),
        [pl.BlockSpec((tm, D), rev),
         pl.BlockSpec((D, D), const, pipeline_mode=pl.Buffered(1)),
         pl.BlockSpec((tm, DA), lambda i: (n - 1 - i, 1)), pl.BlockSpec((tm, DP), lambda i: (n - 1 - i, 2)),
         pl.BlockSpec((tm, DA), rev), pl.BlockSpec((tm, DP), rev),
         pl.BlockSpec((HALO, DP), lambda i: (jnp.maximum((n - 1 - i) * (tm // HALO) - 1, 0), 0)),
         pl.BlockSpec((N_GROUPS, GC, GC), lambda i: (0, 0, 0)),
         pl.BlockSpec((1, DP), const)],
        [pl.BlockSpec((tm, DP + D), lambda i: (n - 1 - i, 1)), pl.BlockSpec((tm, DA), rev)] + res(DA)
        + [pl.BlockSpec((tm, LANES), rev)] + res(LANES) + [pool_blocks, pool_blocks, pl.BlockSpec((8, DP), const)],
        [SDS((S, D_IN), BF16), SDS((S, DA), BF16)] + [SDS((d, S // d, DA), BF16) for d in DILATIONS[1:]]
        + [SDS((S, LANES), F32)] + [SDS((d, S // d, LANES), F32) for d in DILATIONS[1:]]
        + [SDS((N_DEV, N_GROUPS * PB, GC), F32), SDS((N_DEV, N_GROUPS * PB, GC), BF16), SDS((8, DP), F32)],
        [pltpu.VMEM((HALO + tm, DP), F32), pltpu.VMEM((tm + HALO, DP), F32),
         pltpu.VMEM((N_GROUPS, GC, GC), F32), pltpu.VMEM((N_HEADS, tm, DH), F32), pltpu.VMEM((tm, LANES), F32)],
        [dzb, wout, hug, hug, attn, hug, hug, wp, scale], "bwd_mid")
    return outs, carried


def _attn_bwd(h16, do, lse, dd, rope, name, others=None, dh=None, rider=None):
    d, n_sub, _ = h16.shape
    n_i = n_sub // BQ
    n_kb = n_sub // KB
    qw = BQ + KB
    final = others is not None
    out_dtype = BF16
    n_cb = 3 * DA // DH

    def nxt(i):
        return jnp.minimum(2 * i + 2, n_kb - 1)

    def body(qc_ref, qn_ref, kc_ref, vc_ref, doc_ref, don_ref, lc_ref, ln_ref, dc_ref, dn_ref,
             c_ref, sn_ref, *rest):
        if final:
            acc4_ref, acc16_ref, _, out_ref, carry_ref, qw_ref, dow_ref, lw_ref, dw_ref, nat_ref = rest
        else:
            out_ref, carry_ref, qw_ref, dow_ref, lw_ref, dw_ref = rest
        i = pl.program_id(1)

        @pl.when(i == 0)
        def _():
            carry_ref[...] = jnp.zeros((KB, DA), F32)

        for win, own, after in ((qw_ref, qc_ref, qn_ref), (dow_ref, doc_ref, don_ref), (lw_ref, lc_ref, ln_ref),
                                (dw_ref, dc_ref, dn_ref)):
            win[0:BQ, :] = own[...]
            win[BQ:qw, :] = after[...]
        if final:
            for n, (dil, acc) in enumerate(((DILATIONS[1], acc4_ref), (DILATIONS[2], acc16_ref))):
                for r in range(dil):
                    for cb in range(n_cb):
                        nat_ref[n, cb, pl.ds(r, BQ // dil, stride=dil), :] = acc[r, :, cb * DH:(cb + 1) * DH].astype(F32)

        a = lax.broadcasted_iota(jnp.int32, (qw, BQ), 0)
        b = lax.broadcasted_iota(jnp.int32, (qw, BQ), 1)
        n_q = jnp.where(i == n_i - 1, BQ, qw)
        mask = (b <= a) & (a <= b + KB) & (a < n_q)
        tok = pl.ds(pl.program_id(0) + d * BQ * i, BQ, stride=d)
        tabs = (c_ref[tok, :], sn_ref[tok, :])
        for h in range(N_HEADS):
            hs = slice(h * DH, (h + 1) * DH)
            hc = slice(h, h + 1)
            q, k, vv, dob = qw_ref[:, hs], kc_ref[:, hs], vc_ref[:, hs], dow_ref[:, hs]
            s = _dot(q, k, NT) * SCALE
            p = jnp.exp(jnp.where(mask, s - lw_ref[:, hc], NEG))
            dp = _dot(dob, vv, NT)
            ds = (p * (dp - dw_ref[:, hc]) * SCALE).astype(BF16)
            dq = _dot(ds, k)
            dk = _dot(ds, q, T_N)
            dv = _dot(p.astype(BF16), dob, T_N)
            dq_lo = dq[0:KB] + carry_ref[:, hs]
            carry_ref[:, hs] = dq[BQ:qw]
            dq_own = _rope_t(jnp.concatenate([dq_lo, dq[KB:BQ]], axis=0), *tabs)
            for base, gv in ((0, dq_own), (DA, _rope_t(dk, *tabs)), (2 * DA, dv)):
                if final:
                    cb = base // DH + h
                    gv = gv + nat_ref[0, cb] + nat_ref[1, cb]
                out_ref[:, base + h * DH:base + (h + 1) * DH] = gv.astype(out_dtype)

    def cur(rows, width, col=0):
        return pl.BlockSpec((None, rows, width), lambda r, i: (r, i, col))

    def nx(width, col=0):
        return pl.BlockSpec((None, KB, width), lambda r, i: (r, nxt(i), col))

    in_specs = [cur(BQ, DA), nx(DA), cur(BQ, DA, 1), cur(BQ, DA, 2), cur(BQ, DA), nx(DA),
                cur(BQ, LANES), nx(LANES), cur(BQ, LANES), nx(LANES)]
    in_specs += [pl.BlockSpec((S, DH), lambda r, i: (0, 0), pipeline_mode=pl.Buffered(1))] * 2
    args = [h16, h16, h16, h16, do, do, lse, lse, dd, dd] + list(rope)
    scratch = [pltpu.VMEM((KB, DA), F32), pltpu.VMEM((qw, DA), BF16), pltpu.VMEM((qw, DA), BF16),
               pltpu.VMEM((qw, LANES), F32), pltpu.VMEM((qw, LANES), F32)]
    if final:
        assert d == 1
        in_specs += [pl.BlockSpec((dil, BQ // dil, 3 * DA), lambda r, i: (0, i, 0)) for dil in DILATIONS[1:]]
        in_specs.append(pl.BlockSpec(memory_space=pl.ANY))
        args += list(others) + [dh[None]]
        scratch.append(pltpu.VMEM((2, n_cb, BQ, DH), F32))
    outs, carried = _call_carrying(
        body, rider, (d, n_i), in_specs, [cur(BQ, 3 * DA)], [SDS((d, n_sub, D_IN if final else 3 * DA), out_dtype)],
        scratch, args, name, {len(args) - 1: 0} if final else {})
    return outs[0] if rider is None else (outs[0], carried)


def _grad_w_out(yt, dzb):
    tn = 256

    def body(yt_ref, dz_ref, o_ref, o16_ref):
        acc = _dot(yt_ref[...], dz_ref[...])
        o_ref[...] = acc
        o16_ref[...] = acc.astype(BF16)

    cols = pl.BlockSpec((D, tn), lambda c: (0, c))
    o, o16 = pl.pallas_call(
        body, name="grad_w_out", grid=(D // tn,),
        in_specs=[pl.BlockSpec((D, S), lambda c: (0, 0), pipeline_mode=pl.Buffered(1)),
                  pl.BlockSpec((S, tn), lambda c: (0, c))],
        out_specs=[cols, cols],
        out_shape=[SDS((D, D), F32), SDS((D, D), BF16)],
        compiler_params=_params(("arbitrary",)),
    )(yt, dzb)
    return o.reshape(N_DEV, OB, D), o16.reshape(N_DEV, OB, D)


def _grad_w_in(xt, dh, order):
    n_send = N_DEV // 2
    n_steps = D_IN // TN
    own0 = n_send * W_PHASES
    sums0 = own0 + W_PHASES
    step_of_send = [W_PHASES * k + W_PHASES + 1 for k in range(n_send)]

    def body(ord_ref, xt_ref, dh_ref, o_ref, o16_ref, r1_ref, s2_ref, st_ref, got_ref,
             piece_sems, send_sems, recv_sems, got_sem):
        c = pl.program_id(0)
        x, y, core, _ = _place()

        def piece(cc):
            blk, sub = ord_ref[cc // W_PHASES], cc % W_PHASES
            return pltpu.make_async_copy(st_ref.at[cc % 2], o16_ref.at[blk, :, pl.ds(sub * TN, TN)], piece_sems.at[cc % 2])

        def send(k):
            return pltpu.make_async_remote_copy(
                src_ref=o16_ref.at[ord_ref[k]], dst_ref=r1_ref.at[k], send_sem=send_sems.at[k], recv_sem=recv_sems.at[k],
                device_id=(x, y, 1 - core), device_id_type=MESH)

        def sibling_part(cc):
            k, sub = cc // W_PHASES - n_send, cc % W_PHASES
            return pltpu.make_async_copy(r1_ref.at[k, :, pl.ds(sub * TN, TN)], got_ref, got_sem)

        @pl.when((c >= 2) & (c < own0 + 2))
        def _():
            piece(c - 2).wait()

        for k in range(n_send):
            pl.when(c == step_of_send[k])(lambda k=k: send(k).start())
        for k in range(1, n_send):
            pl.when(c == own0 + W_PHASES * k)(lambda k=k: send(k).wait_recv())
        pl.when(c >= sums0)(lambda: sibling_part(c).start())

        acc = _dot(xt_ref[...], dh_ref[...])
        o_ref[...] = acc

        @pl.when(c < own0)
        def _():
            st_ref[c % 2] = acc.astype(BF16)
            piece(c).start()

        @pl.when(c >= sums0)
        def _():
            sibling_part(c).wait()
            s2_ref[...] = (acc + got_ref[...].astype(F32)).astype(BF16)

        @pl.when(c == n_steps - 1)
        def _():
            send(0).wait_recv()
            for k in range(n_send):
                send(k).wait_send()

    col = lambda c, ordr: W_PHASES * ordr[c // W_PHASES] + c % W_PHASES
    hbm = pl.BlockSpec(memory_space=pl.ANY)
    sums_blk = lambda c, ordr: (jnp.maximum(c // W_PHASES - (n_send + 1), 0), 0, jnp.where(c >= sums0, c % W_PHASES, 0))
    partial, _, got, sums = pl.pallas_call(
        body, name="grad_w_in",
        grid_spec=pltpu.PrefetchScalarGridSpec(
            num_scalar_prefetch=1, grid=(n_steps,),
            in_specs=[pl.BlockSpec((D, S), lambda c, ordr: (0, 0), pipeline_mode=pl.Buffered(1)),
                      pl.BlockSpec((S, TN), lambda c, ordr: (0, col(c, ordr)))],
            out_specs=[pl.BlockSpec((None, D, TN), lambda c, ordr: (ordr[c // W_PHASES], 0, c % W_PHASES)), hbm, hbm,
                       pl.BlockSpec((None, D, TN), sums_blk)],
            scratch_shapes=[pltpu.VMEM((2, D, TN), BF16), pltpu.VMEM((D, TN), BF16), pltpu.SemaphoreType.DMA((2,)),
                            pltpu.SemaphoreType.DMA((n_send,)), pltpu.SemaphoreType.DMA((n_send,)),
                            pltpu.SemaphoreType.DMA(())]),
        out_shape=[SDS((N_DEV, D, WB), F32), SDS((N_DEV, D, WB), BF16), SDS((n_send, D, WB), BF16),
                   SDS((n_send - 1, D, WB), BF16)],
        compiler_params=_params(("arbitrary",)),
    )(order, xt, dh)
    return partial, got, sums


def _grad_x(dz, dh, wgs, rider=None):
    tm = 1024

    def body(dz_ref, dh_ref, *rest):
        o_ref = rest[-1]

        @pl.when(pl.program_id(1) == 0)
        def _():
            o_ref[...] = ALPHA * dz_ref[...]

        acc = _dot(dh_ref[:, 0:TN], rest[0][...], NT)
        for s in range(1, W_PHASES):
            acc = acc + _dot(dh_ref[:, s * TN:(s + 1) * TN], rest[s][...], NT)
        o_ref[...] += acc

    outs, carried = _call_carrying(
        body, rider, (S // tm, N_DEV),
        [pl.BlockSpec((tm, D), lambda i, j: (i, 0)), pl.BlockSpec((tm, WB), lambda i, j: (i, j))]
        + [pl.BlockSpec((None, D, TN), lambda i, j: (j, 0, 0))] * W_PHASES,
        [pl.BlockSpec((tm, D), lambda i, j: (i, 0))], [SDS((S, D), F32)], [], [dz, dh] + list(wgs), "grad_x")
    return outs[0], carried


def _place():
    x, y, c = lax.axis_index("x"), lax.axis_index("y"), lax.axis_index("c")
    chips = [(x, y), (1 - x, y), (x, 1 - y), (1 - x, 1 - y)]
    return x, y, c, chips


def _blk(x, y, c):
    return 4 * x + 2 * y + c


def _adamw(w, g, m, v):
    m = ADAM_B1 * m + (1.0 - ADAM_B1) * g
    v = ADAM_B2 * v + (1.0 - ADAM_B2) * (g * g)
    m_hat = m / (1.0 - ADAM_B1 ** ADAM_STEP)
    v_hat = v / (1.0 - ADAM_B2 ** ADAM_STEP)
    delta = -ADAM_LR * (m_hat / (jnp.sqrt(v_hat) + ADAM_EPS) + ADAM_WD * w)
    return delta, m, v


class _Rider:
    def __init__(self, args, in_specs, out_shape, out_specs, scratch, start, finish):
        self.args, self.in_specs, self.out_shape, self.out_specs = args, in_specs, out_shape, out_specs
        self.scratch, self.start, self.finish = scratch, start, finish


def _carry(body, rider, n_in, n_out, first, last):
    if rider is None:
        return body
    r_in, r_out, r_scr = len(rider.args), len(rider.out_shape), len(rider.scratch)

    def carrying(*refs):
        o0 = n_in + r_in
        s0 = o0 + n_out + r_out
        s1 = len(refs) - r_scr
        theirs = (refs[n_in:o0], refs[o0 + n_out:s0], refs[s1:])
        pl.when(first())(lambda: rider.start(*theirs))
        body(*refs[:n_in], *refs[o0:o0 + n_out], *refs[s0:s1])
        pl.when(last())(lambda: rider.finish(*theirs))

    return carrying


def _call_carrying(body, rider, grid, in_specs, out_specs, out_shape, scratch, args, name, aliases=None):
    n_in, n_out = len(in_specs), len(out_specs)
    ids = lambda: [pl.program_id(a) for a in range(len(grid))]
    first = lambda: functools.reduce(jnp.logical_and, [i == 0 for i in ids()])
    last = lambda: functools.reduce(jnp.logical_and, [i == n - 1 for i, n in zip(ids(), grid)])
    if rider is not None:
        in_specs, args = in_specs + rider.in_specs, list(args) + rider.args
        out_specs, out_shape = out_specs + rider.out_specs, out_shape + rider.out_shape
        scratch = scratch + rider.scratch
    outs = pl.pallas_call(
        _carry(body, rider, n_in, n_out, first, last), name=name, grid=grid,
        in_specs=in_specs, out_specs=out_specs, out_shape=out_shape, scratch_shapes=scratch,
        input_output_aliases=aliases or {}, compiler_params=_params(("arbitrary",) * len(grid)),
    )(*args)
    return list(outs[:n_out]), list(outs[n_out:])


def _gather_copy(tensors, send_sems, recv_sems, t, k, block, to, src=None):
    dst = tensors[t][1](_blk(*block))
    return pltpu.make_async_remote_copy(
        src_ref=dst if src is None else src, dst_ref=dst,
        send_sem=send_sems.at[t, k], recv_sem=recv_sems.at[t, k], device_id=to, device_id_type=MESH)


def _gather_start(tensors, send_sems, recv_sems, local_sems):
    x, y, c, chips = _place()
    me, sib = (x, y, c), (x, y, 1 - c)
    for t, (src, dst) in enumerate(tensors):
        pltpu.make_async_copy(src, dst(_blk(*me)), local_sems.at[t]).start()
        _gather_copy(tensors, send_sems, recv_sems, t, 0, me, sib, src).start()
        for j in (1, 2, 3):
            _gather_copy(tensors, send_sems, recv_sems, t, j, me, (*chips[j], c), src).start()


def _gather_finish(tensors, send_sems, recv_sems, local_sems):
    x, y, c, chips = _place()
    me, sib = (x, y, c), (x, y, 1 - c)
    copy = functools.partial(_gather_copy, tensors, send_sems, recv_sems)
    for t in range(len(tensors)):
        for j in (1, 2, 3):
            copy(t, j, (*chips[j], c), me).wait_recv()
            copy(t, 3 + j, (*chips[j], c), sib).start()
    for t, (src, dst) in enumerate(tensors):
        copy(t, 0, sib, me).wait_recv()
        for j in (1, 2, 3):
            copy(t, 3 + j, (*chips[j], 1 - c), me).wait_recv()
        copy(t, 0, me, sib, src).wait_send()
        for j in (1, 2, 3):
            copy(t, j, me, (*chips[j], c), src).wait_send()
            copy(t, 3 + j, (*chips[j], c), sib).wait_send()
        pltpu.make_async_copy(src, dst(_blk(*me)), local_sems.at[t]).wait()


W_PHASES = WB // TN


def _gather_first_rider(w_in_s, w_out_s, w_pool_s):
    shapes = [w_in_s.shape, w_out_s.shape, w_pool_s.shape]

    def keeps(outs, scr):
        return [pltpu.make_async_copy(scr[n], outs[n], scr[6].at[n]) for n in range(3)]

    def tensors(outs, scr):
        return [(scr[0].at[:, pl.ds(0, TN)], lambda b: outs[3].at[b])]

    def start(ins, outs, scr):
        for n in range(3):
            scr[n][...] = ins[n][...].astype(BF16)
        for keep in keeps(outs, scr):
            keep.start()
        _gather_start(tensors(outs, scr), *scr[3:6])

    def finish(ins, outs, scr):
        _gather_finish(tensors(outs, scr), *scr[3:6])
        for keep in keeps(outs, scr):
            keep.wait()

    vmem = pl.BlockSpec(memory_space=pltpu.VMEM)
    hbm = pl.BlockSpec(memory_space=pl.ANY)
    return _Rider(
        args=[w_in_s, w_out_s, w_pool_s], in_specs=[vmem] * 3,
        out_shape=[SDS(sh, BF16) for sh in shapes] + [SDS((N_DEV, D, TN), BF16)], out_specs=[hbm] * 4,
        scratch=[pltpu.VMEM(sh, BF16) for sh in shapes]
        + [pltpu.SemaphoreType.DMA((1, 7)), pltpu.SemaphoreType.DMA((1, 7)), pltpu.SemaphoreType.DMA((1,)),
           pltpu.SemaphoreType.DMA((3,))],
        start=start, finish=finish)


def _gather_w_in_rider(shard16, s):
    def tensors(ins, outs):
        return [(ins[0].at[:, pl.ds(s * TN, TN)], lambda b: outs[0].at[b])]

    hbm = pl.BlockSpec(memory_space=pl.ANY)
    return _Rider(
        args=[shard16], in_specs=[hbm], out_shape=[SDS((N_DEV, D, TN), BF16)], out_specs=[hbm],
        scratch=[pltpu.SemaphoreType.DMA((1, 7)), pltpu.SemaphoreType.DMA((1, 7)), pltpu.SemaphoreType.DMA((1,))],
        start=lambda ins, outs, scr: _gather_start(tensors(ins, outs), *scr),
        finish=lambda ins, outs, scr: _gather_finish(tensors(ins, outs), *scr))


def _gather_small_rider(w_out16, w_pool16):
    def tensors(ins, outs):
        gout_ref, gpool_ref = outs

        def pool_rows(b):
            return gpool_ref.at[:, pl.ds(pl.multiple_of(b * PB, PB), PB), :]

        return [(ins[0], lambda b: gout_ref.at[b]), (ins[1], pool_rows)]

    hbm = pl.BlockSpec(memory_space=pl.ANY)
    return _Rider(
        args=[w_out16, w_pool16], in_specs=[hbm, hbm],
        out_shape=[SDS((N_DEV, OB, D), BF16), SDS((N_GROUPS, GC, GC), BF16)], out_specs=[hbm, hbm],
        scratch=[pltpu.SemaphoreType.DMA((2, 7)), pltpu.SemaphoreType.DMA((2, 7)), pltpu.SemaphoreType.DMA((2,))],
        start=lambda ins, outs, scr: _gather_start(tensors(ins, outs), *scr),
        finish=lambda ins, outs, scr: _gather_finish(tensors(ins, outs), *scr))


def _block_table():
    x, y, c, chips = _place()
    return jnp.stack([_blk(*chip, c) for chip in chips]).astype(jnp.int32)


def _sibling_first_order():
    x, y, c, chips = _place()
    return jnp.stack([_blk(*chip, side) for side in (1 - c, c) for chip in chips]).astype(jnp.int32)


def _exchange_rider(p16):
    _, r_tot, cols = p16.shape

    def copies(ins, outs, scr):
        x, y, c, chips = _place()
        return [pltpu.make_async_remote_copy(
            src_ref=ins[0].at[_blk(*chips[k], 1 - c)], dst_ref=outs[0].at[k],
            send_sem=scr[0].at[k], recv_sem=scr[1].at[k], device_id=(x, y, 1 - c), device_id_type=MESH)
            for k in range(4)]

    def start(ins, outs, scr):
        for cp in copies(ins, outs, scr):
            cp.start()

    def finish(ins, outs, scr):
        for cp in copies(ins, outs, scr):
            cp.wait()

    hbm = pl.BlockSpec(memory_space=pl.ANY)
    return _Rider([p16], [hbm], [SDS((4, r_tot, cols), BF16)], [hbm],
                  [pltpu.SemaphoreType.DMA((4,)), pltpu.SemaphoreType.DMA((4,))], start, finish)


def _chip_sums(name, table, p32, r1, rc):
    _, r_tot, cols = p32.shape

    def body(tbl_ref, p_ref, r_ref, o_ref):
        o_ref[...] = (p_ref[...] + r_ref[...].astype(F32)).astype(BF16)

    return pl.pallas_call(
        body, name=name,
        grid_spec=pltpu.PrefetchScalarGridSpec(
            num_scalar_prefetch=1, grid=(3, r_tot // rc),
            in_specs=[pl.BlockSpec((None, rc, cols), lambda k, ch, tbl: (tbl[k + 1], ch, 0)),
                      pl.BlockSpec((None, rc, cols), lambda k, ch, tbl: (k + 1, ch, 0))],
            out_specs=pl.BlockSpec((None, rc, cols), lambda k, ch, tbl: (k, ch, 0))),
        out_shape=SDS((3, r_tot, cols), BF16),
        compiler_params=_params(("arbitrary", "arbitrary")),
    )(table, p32, r1)


def _stage2_rider(sums, stats=None):
    n_t = len(sums)

    def copies(ins, outs, scr):
        x, y, c, chips = _place()
        out = []
        for t in range(n_t):
            for k in (1, 2, 3):
                out.append(pltpu.make_async_remote_copy(
                    src_ref=ins[t].at[k - 1], dst_ref=outs[t].at[k - 1],
                    send_sem=scr[0].at[t, k - 1], recv_sem=scr[1].at[t, k - 1],
                    device_id=(*chips[k], c), device_id_type=MESH))
        if stats is not None:
            for k in range(1, N_DEV):
                peer = (x ^ ((k >> 2) & 1), y ^ ((k >> 1) & 1), c ^ (k & 1))
                out.append(pltpu.make_async_remote_copy(
                    src_ref=scr[4], dst_ref=outs[n_t].at[_blk(x, y, c)],
                    send_sem=scr[2].at[k - 1], recv_sem=scr[3].at[k - 1], device_id=peer, device_id_type=MESH))
        return out

    def own_rows(outs, scr):
        x, y, c, _ = _place()
        return pltpu.make_async_copy(scr[4], outs[n_t].at[_blk(x, y, c)], scr[5])

    def start(ins, outs, scr):
        if stats is not None:
            scr[4][...] = ins[n_t][...]
            scr[4][3:4, 0:DP] = ins[n_t + 1][0:1, :]
            own_rows(outs, scr).start()
        for cp in copies(ins, outs, scr):
            cp.start()

    def finish(ins, outs, scr):
        for cp in copies(ins, outs, scr):
            cp.wait()
        if stats is not None:
            own_rows(outs, scr).wait()

    vmem = pl.BlockSpec(memory_space=pltpu.VMEM)
    hbm = pl.BlockSpec(memory_space=pl.ANY)
    scratch = [pltpu.SemaphoreType.DMA((n_t, 3)), pltpu.SemaphoreType.DMA((n_t, 3))]
    args, in_specs = list(sums), [hbm] * n_t
    out_shape, out_specs = [SDS(s.shape, BF16) for s in sums], [hbm] * n_t
    if stats is not None:
        scratch += [pltpu.SemaphoreType.DMA((N_DEV - 1,)), pltpu.SemaphoreType.DMA((N_DEV - 1,)),
                    pltpu.VMEM((8, D), F32), pltpu.SemaphoreType.DMA(())]
        args, in_specs = args + list(stats), in_specs + [vmem, vmem]
        out_shape, out_specs = out_shape + [SDS((N_DEV, 8, D), F32)], out_specs + [hbm]
    return _Rider(args, in_specs, out_shape, out_specs, scratch, start, finish)


def _adamw_shard(name, table, p32, r1, r2, w, m, v, rc):
    _, r_tot, cols = p32.shape

    def body(tbl_ref, p_ref, r1_ref, r2_ref, w_ref, m_ref, v_ref, g_ref, d_ref, nm_ref, nv_ref):
        g = p_ref[...] + r1_ref[...].astype(F32)
        for k in range(3):
            g = g + r2_ref[k].astype(F32)
        delta, nm, nv = _adamw(w_ref[...], g, m_ref[...], v_ref[...])
        g_ref[...] = g
        d_ref[...] = delta
        nm_ref[...] = nm
        nv_ref[...] = nv

    rows = pl.BlockSpec((rc, cols), lambda ch, tbl: (ch, 0))
    shard = SDS((r_tot, cols), F32)
    return pl.pallas_call(
        body, name=name,
        grid_spec=pltpu.PrefetchScalarGridSpec(
            num_scalar_prefetch=1, grid=(r_tot // rc,),
            in_specs=[pl.BlockSpec((None, rc, cols), lambda ch, tbl: (tbl[0], ch, 0)),
                      pl.BlockSpec((None, rc, cols), lambda ch, tbl: (0, ch, 0)),
                      pl.BlockSpec((3, rc, cols), lambda ch, tbl: (0, ch, 0)), rows, rows, rows],
            out_specs=[rows, rows, rows, rows]),
        out_shape=[shard, shard, shard, shard],
        compiler_params=_params(("arbitrary",)),
    )(table, p32, r1, r2, w, m, v)


def _replicated_adamw(gathered, gain, bias, scale, m_gain, m_bias, m_scale, v_gain, v_bias, v_scale):
    def body(all_ref, s_ref, g_ref, b_ref, ms_ref, mg_ref, mb_ref, vs_ref, vg_ref, vb_ref, loss_ref, *outs):
        tot = all_ref[0]
        for b in range(1, N_DEV):
            tot = tot + all_ref[b]
        loss_ref[...] = tot[2:3, 0:LANES]
        for n, (row, width, w_r, m_r, v_r) in enumerate(((3, DP, s_ref, ms_ref, vs_ref), (0, D, g_ref, mg_ref, vg_ref),
                                                         (1, D, b_ref, mb_ref, vb_ref))):
            g = tot[row:row + 1, 0:width]
            delta, nm, nv = _adamw(w_r[...], g, m_r[...], v_r[...])
            for out, val in zip(outs[4 * n:4 * n + 4], (g, delta, nm, nv)):
                out[...] = val

    vmem = pl.BlockSpec(memory_space=pltpu.VMEM)
    shapes = [SDS((1, LANES), F32)] + [SDS((1, width), F32) for width in (DP, D, D) for _ in range(4)]
    return pl.pallas_call(
        body, name="replicated_adamw",
        in_specs=[vmem] * 10, out_specs=[vmem] * len(shapes), out_shape=shapes,
        compiler_params=_params(),
    )(gathered, scale, gain, bias, m_scale, m_gain, m_bias, v_scale, v_gain, v_bias)


def kernel(x, w_in, w_pool, pool_scale, w_out, ln_gain, ln_bias, loss_target, m_w_in, m_w_pool, m_pool_scale, m_w_out, m_ln_gain, m_ln_bias, v_w_in, v_w_pool, v_pool_scale, v_w_out, v_ln_gain, v_ln_bias):
    pool_rows = (N_GROUPS * PB, GC)
    x2, target = x[0], loss_target[0]
    rope = _rope_tables()
    table = _block_table()

    (xb, xt), (shard16, w_out16, w_pool16, wg0) = _prep_x(x2, _gather_first_rider(w_in[0], w_out[0], w_pool[0]))
    h, (wg1,) = _proj_phase(xb, wg0, 0, rope, None, _gather_w_in_rider(shard16, 1))
    h, (wg2,) = _proj_phase(xb, wg1, 1, rope, h, _gather_w_in_rider(shard16, 2))
    h, (wg_out, wg_pool) = _proj_phase(xb, wg2, 2, rope, h, _gather_small_rider(w_out16, w_pool16))
    wg_out = wg_out.reshape(D, D)
    h16, hug = [h[0][None], h[1], h[2]], h[3]
    fwd = [_attn_fwd(h16[n], f"attn_fwd_d{d}") for n, d in enumerate(DILATIONS)]
    y, yt, attn, lse1, lse4, lse16 = _mix(fwd[0][0][0], fwd[0][1][0], *fwd[1], *fwd[2], hug, wg_pool, pool_scale)
    dz, dzb, stats = _outproj_ln(y, wg_out, x2, target, ln_gain, ln_bias)

    dwout, dwout16 = _grad_w_out(yt, dzb)
    mid, (r1_out,) = _bwd_mid(dzb, wg_out, hug, attn, wg_pool, pool_scale, rider=_exchange_rider(dwout16))
    dh, do1, do4, do16, dd1, dd4, dd16, dwp, dwp16, gps = mid
    s2_out = _chip_sums("rs_out_sums", table, dwout, r1_out, OB)

    do, lse, dd = [do1[None], do4, do16], [lse1[None], lse4, lse16], [dd1[None], dd4, dd16]
    acc4, (r2_out,) = _attn_bwd(h16[1], do[1], lse[1], dd[1], rope, "attn_bwd_d4", rider=_stage2_rider([s2_out]))
    g_out, d_out, nm_out, nv_out = _adamw_shard(
        "adamw_w_out", table, dwout, r1_out, r2_out, w_out[0], m_w_out[0], v_w_out[0], OB // 2)
    acc16, (r1_pool,) = _attn_bwd(h16[2], do[2], lse[2], dd[2], rope, "attn_bwd_d16", rider=_exchange_rider(dwp16))
    s2_pool = _chip_sums("rs_pool_sums", table, dwp, r1_pool, N_GROUPS * PB)
    dh = _attn_bwd(h16[0], do[0], lse[0], dd[0], rope, "attn_bwd_d1", others=(acc4, acc16), dh=dh)[0]

    dwin, r1_in, s2_in = _grad_w_in(xt, dh, _sibling_first_order())
    grad_x, (r2_in, r2_pool, gathered) = _grad_x(
        dz, dh, (wg0, wg1, wg2), rider=_stage2_rider([s2_in, s2_pool], stats=(stats, gps)))
    g_in, d_in, nm_in, nv_in = _adamw_shard(
        "adamw_w_in", table, dwin, r1_in, r2_in, w_in[0], m_w_in[0], v_w_in[0], 256)
    g_pool, d_pool, nm_pool, nv_pool = _adamw_shard(
        "adamw_w_pool", table, dwp, r1_pool, r2_pool, w_pool[0].reshape(pool_rows), m_w_pool[0].reshape(pool_rows),
        v_w_pool[0].reshape(pool_rows), N_GROUPS * PB)
    loss, *small = _replicated_adamw(gathered, ln_gain, ln_bias, pool_scale, m_ln_gain, m_ln_bias, m_pool_scale,
                                     v_ln_gain, v_ln_bias, v_pool_scale)
    (g_ps, d_ps, nm_ps, nv_ps, g_gain, d_gain, nm_gain, nv_gain, g_bias, d_bias, nm_bias, nv_bias) = small

    shard4 = lambda t: t.reshape(1, N_GROUPS, PB, GC)
    lead = lambda t: t[None]
    return (loss[0, 0], lead(grad_x),
            lead(g_in), shard4(g_pool), g_ps, lead(g_out), g_gain, g_bias,
            lead(d_in), shard4(d_pool), d_ps, lead(d_out), d_gain, d_bias,
            lead(nm_in), shard4(nm_pool), nm_ps, lead(nm_out), nm_gain, nm_bias,
            lead(nv_in), shard4(nv_pool), nv_ps, lead(nv_out), nv_gain, nv_bias)
```

```python
import functools

import jax
import jax.numpy as jnp
from jax import lax
from jax.experimental import pallas as pl
from jax.experimental.pallas import tpu as pltpu

F32 = jnp.float32
BF16 = jnp.bfloat16
SDS = jax.ShapeDtypeStruct
MESH = pl.DeviceIdType.MESH

N_DEV = 8
S = 4096
D = 2048
N_HEADS = 8
DH = 128
DA = N_HEADS * DH
DP = 1024
N_GROUPS = 4
GC = DP // N_GROUPS
POOL_WINDOWS = (2, 4, 8, 16)
HALO = 16
D_IN = 3 * DA + DP + D
WB = D_IN // N_DEV
TN = 256
HW = 3 * DA
OB = D // N_DEV
PB = GC // N_DEV
ROPE_DIM = DH // 4
ROPE_HALF = ROPE_DIM // 2
ROPE_THETA = 500000.0
DILATIONS = (1, 4, 16)
KB = 128
LN_EPS = 1e-5
ALPHA = 2.0 ** 0.25
SCALE = DH ** -0.5
NEG = -1e30
ADAM_LR, ADAM_B1, ADAM_B2, ADAM_EPS, ADAM_WD, ADAM_STEP = 0.001, 0.9, 0.999, 1e-08, 0.01, 10

VMEM_LIMIT_V7X = 61 * 1024 * 1024

NT = (((1,), (1,)), ((), ()))
T_N = (((0,), (0,)), ((), ()))


def _params(sem=None):
    return pltpu.CompilerParams(dimension_semantics=sem, vmem_limit_bytes=VMEM_LIMIT_V7X)


def _dot(a, b, dims=None):
    if dims is None:
        return jnp.dot(a, b, preferred_element_type=F32)
    return lax.dot_general(a, b, dims, preferred_element_type=F32)


def _rope_tables():
    inv_freq = ROPE_THETA ** (-(2.0 * jnp.arange(ROPE_HALF, dtype=F32)) / ROPE_DIM)
    ang = jnp.arange(S, dtype=jnp.int32).astype(F32)[:, None] * inv_freq[None, :]
    cos, sin = lax.optimization_barrier((jnp.cos(ang), jnp.sin(ang)))
    rest = DH - ROPE_DIM
    c = jnp.concatenate([cos, cos, jnp.ones((S, rest), F32)], axis=1)
    sn = jnp.concatenate([-sin, sin, jnp.zeros((S, rest), F32)], axis=1)
    return c, sn


def _rope_partner(t):
    lane = lax.broadcasted_iota(jnp.int32, t.shape, 1)
    return jnp.where(lane < ROPE_HALF, pltpu.roll(t, DH - ROPE_HALF, 1), pltpu.roll(t, ROPE_HALF, 1))


def _rope(t, c, sn):
    return t * c + _rope_partner(t) * sn


def _rope_t(g, c, sn):
    return g * c - _rope_partner(g) * sn


def _prep_x(x, rider):
    tm = 512

    def body(x_ref, xb_ref, xt_ref):
        xv = x_ref[...]
        xb_ref[...] = xv.astype(BF16)
        xt_ref[...] = xv.T.astype(BF16)

    return _call_carrying(
        body, rider, (S // tm,), [pl.BlockSpec((tm, D), lambda i: (i, 0))],
        [pl.BlockSpec((tm, D), lambda i: (i, 0)), pl.BlockSpec((D, tm), lambda i: (0, i))],
        [SDS((S, D), BF16), SDS((D, S), BF16)], [], [x], "prep_x")


def _residues(slab_ref, idx, d, r, n):
    return slab_ref[(*idx, pl.ds(r, n, stride=d), slice(None))]


def _proj_phase(xb, wg, s, rope, prev, rider):
    n_heads = TN // DH
    n16 = HW // TN
    j16 = n16 // W_PHASES
    n_rope = 2 * DA // TN
    once = pl.Buffered(1)

    def body(x_ref, w_ref, c_ref, sn_ref, *rest):
        h16_ref, h4_ref, h16r_ref, hug_ref, slab_ref, res_ref = rest[-6:]
        j = pl.program_id(0)
        d4, d16 = DILATIONS[1], DILATIONS[2]
        acc = _dot(x_ref[...], w_ref[...])

        def heads(with_rope):
            for hh in range(n_heads):
                hs = slice(hh * DH, (hh + 1) * DH)
                t = _rope(acc[:, hs], c_ref[...], sn_ref[...]) if with_rope else acc[:, hs]
                h16_ref[:, hs] = t.astype(BF16)
                slab_ref[...] = t
                for q in range(d4):
                    t4 = _residues(slab_ref, (), d4, q, S // d4)
                    h4_ref[q, :, hs] = t4.astype(BF16)
                    res_ref[...] = t4
                    for m in range(d16 // d4):
                        h16r_ref[d4 * m + q, :, hs] = _residues(res_ref, (), d4, m, S // d16).astype(BF16)

        pl.when(W_PHASES * j + s < n_rope)(lambda: heads(True))
        pl.when((W_PHASES * j + s >= n_rope) & (j < j16))(lambda: heads(False))

        @pl.when(j >= j16)
        def _():
            hug_ref[...] = acc.astype(BF16)

    col16 = lambda j: W_PHASES * jnp.minimum(j, j16 - 1) + s
    col32 = lambda j: W_PHASES * jnp.maximum(j, j16) + s - n16
    hbm = pl.BlockSpec(memory_space=pl.ANY)
    in_specs = [pl.BlockSpec((S, D), lambda j: (0, 0), pipeline_mode=once),
                pl.BlockSpec((None, D, TN), lambda j: (j, 0, 0))]
    in_specs += [pl.BlockSpec((S, DH), lambda j: (0, 0), pipeline_mode=once)] * 2
    args = [xb, wg] + list(rope)
    aliases = {}
    if prev is not None:
        aliases = {len(args) + n: n for n in range(4)}
        in_specs, args = in_specs + [hbm] * 4, args + list(prev)
    out_specs = [pl.BlockSpec((S, TN), lambda j: (0, col16(j)))]
    out_specs += [pl.BlockSpec((d, S // d, TN), lambda j: (0, 0, col16(j))) for d in DILATIONS[1:]]
    out_specs += [pl.BlockSpec((S, TN), lambda j: (0, col32(j)))]
    out_shape = [SDS((S, HW), BF16)] + [SDS((d, S // d, HW), BF16) for d in DILATIONS[1:]] + [SDS((S, HW), BF16)]
    return _call_carrying(
        body, rider, (N_DEV,), in_specs, out_specs, out_shape,
        [pltpu.VMEM((S, DH), F32), pltpu.VMEM((S // DILATIONS[1], DH), F32)], args,
        f"proj_phase{s}", aliases)


BQ = 2 * KB
LANES = 128


def _to_lane(acc, col, h):
    lane = lax.broadcasted_iota(jnp.int32, acc.shape, 1)
    return jnp.where(lane == h, col, acc)


def _attn_fwd(h16, name):
    d, n_sub, _ = h16.shape
    n_i = n_sub // BQ
    kw = KB + BQ

    def prev(i):
        return jnp.maximum(2 * i - 1, 0)

    def body(q_ref, kc_ref, kp_ref, vc_ref, vp_ref, o_ref, l_ref, kw_ref, vw_ref):
        i = pl.program_id(1)
        kw_ref[0:KB, :] = kp_ref[...]
        kw_ref[KB:kw, :] = kc_ref[...]
        vw_ref[0:KB, :] = vp_ref[...]
        vw_ref[KB:kw, :] = vc_ref[...]
        a = lax.broadcasted_iota(jnp.int32, (KB, 2 * KB), 0)
        b = lax.broadcasted_iota(jnp.int32, (KB, 2 * KB), 1)
        band = (b >= a) & (b <= a + KB)
        first_key = jnp.where(i == 0, KB, 0)
        masks = (band & (b >= first_key), band)
        for half in range(2):
            rs = slice(half * KB, (half + 1) * KB)
            ks = slice(half * KB, (half + 2) * KB)
            lse = jnp.zeros((KB, LANES), F32)
            for h in range(N_HEADS):
                hs = slice(h * DH, (h + 1) * DH)
                s = jnp.where(masks[half], _dot(q_ref[rs, hs], kw_ref[ks, hs], NT) * SCALE, NEG)
                m = jnp.max(s, axis=1, keepdims=True)
                p = jnp.exp(s - m)
                den = jnp.sum(p, axis=1, keepdims=True)
                o_ref[rs, hs] = (_dot(p.astype(BF16), vw_ref[ks, hs]) / den).astype(BF16)
                lse = _to_lane(lse, m + jnp.log(den), h)
            l_ref[rs, :] = lse

    return pl.pallas_call(
        body, name=name, grid=(d, n_i),
        in_specs=[pl.BlockSpec((None, BQ, DA), lambda r, i: (r, i, 0)),
                  pl.BlockSpec((None, BQ, DA), lambda r, i: (r, i, 1)),
                  pl.BlockSpec((None, KB, DA), lambda r, i: (r, prev(i), 1)),
                  pl.BlockSpec((None, BQ, DA), lambda r, i: (r, i, 2)),
                  pl.BlockSpec((None, KB, DA), lambda r, i: (r, prev(i), 2))],
        out_specs=[pl.BlockSpec((None, BQ, DA), lambda r, i: (r, i, 0)),
                   pl.BlockSpec((None, BQ, LANES), lambda r, i: (r, i, 0))],
        out_shape=[SDS((d, n_sub, DA), BF16), SDS((d, n_sub, LANES), F32)],
        scratch_shapes=[pltpu.VMEM((kw, DA), BF16), pltpu.VMEM((kw, DA), BF16)],
        compiler_params=_params(("arbitrary", "arbitrary")),
    )(h16, h16, h16, h16, h16)


def _pooled(ext_ref, g, rows, tm):
    w = POOL_WINDOWS[g]
    cs = slice(g * GC, (g + 1) * GC)
    cur = ext_ref[HALO:HALO + tm, cs]
    win = cur
    for j in range(1, w):
        win = win + ext_ref[HALO - j:HALO - j + tm, cs]
    cnt = jnp.minimum(rows + 1, w).astype(F32)
    return win / cnt - cur, cnt


def _fill_ext(ext_ref, u_ref, uh_ref, blk, tm):
    @pl.when(blk == 0)
    def _():
        ext_ref[0:HALO, :] = jnp.zeros((HALO, DP), F32)

    @pl.when(blk > 0)
    def _():
        ext_ref[0:HALO, :] = uh_ref[...].astype(F32)

    ext_ref[HALO:HALO + tm, :] = u_ref[...].astype(F32)


def _residue_specs(tm, width):
    return [pl.BlockSpec((d, tm // d, width), lambda i: (0, i, 0)) for d in DILATIONS[1:]]


def _mix(o1, l1, o4, l4, o16, l16, hug, wp, scale):
    tm = 512
    n_slab = N_HEADS + 1

    def body(o1r, l1r, o4r, l4r, o16r, l16r, u_ref, uh_ref, ga_ref, gp_ref, wp_ref, sc_ref,
             y_ref, yt_ref, attn_ref, lse_ref, lse4_ref, lse16_ref, ext_ref, ys_ref, nat_ref, ls_ref):
        i = pl.program_id(0)
        for n, (d, o_r, l_r) in enumerate(((DILATIONS[1], o4r, l4r), (DILATIONS[2], o16r, l16r))):
            for r in range(d):
                rows = pl.ds(r, tm // d, stride=d)
                for h in range(N_HEADS):
                    nat_ref[n, h, rows, :] = o_r[r, :, h * DH:(h + 1) * DH].astype(F32)
                nat_ref[n, N_HEADS, rows, :] = l_r[r]
        la, lb, lc = l1r[...], nat_ref[0, N_HEADS], nat_ref[1, N_HEADS]
        mx = jnp.maximum(jnp.maximum(la, lb), lc)
        ea, eb, ec = jnp.exp(la - mx), jnp.exp(lb - mx), jnp.exp(lc - mx)
        z = ea + eb + ec
        wa, wb, wc = ea / z, eb / z, ec / z
        lse = mx + jnp.log(z)
        lse_ref[...] = lse
        ls_ref[...] = lse
        for d, out in ((DILATIONS[1], lse4_ref), (DILATIONS[2], lse16_ref)):
            for r in range(d):
                out[r] = ls_ref[pl.ds(r, tm // d, stride=d), :]
        for h in range(N_HEADS):
            hs = slice(h * DH, (h + 1) * DH)
            hc = slice(h, h + 1)
            attn = wa[:, hc] * o1r[:, hs].astype(F32) + wb[:, hc] * nat_ref[0, h] + wc[:, hc] * nat_ref[1, h]
            attn_ref[:, hs] = attn.astype(BF16)
            gt = ga_ref[:, hs].astype(F32)
            ys_ref[:, hs] = attn * (gt * jax.nn.sigmoid(gt))

        _fill_ext(ext_ref, u_ref, uh_ref, i, tm)
        rows = i * tm + lax.broadcasted_iota(jnp.int32, (tm, 1), 0)
        for g in range(N_GROUPS):
            cs = slice(g * GC, (g + 1) * GC)
            gs = slice(DA + g * GC, DA + (g + 1) * GC)
            pooled, _ = _pooled(ext_ref, g, rows, tm)
            po = _dot(pooled.astype(BF16), wp_ref[g]) * sc_ref[:, cs]
            gt = gp_ref[:, cs].astype(F32)
            ys_ref[:, gs] = po * (gt * jax.nn.sigmoid(gt))
        yv = ys_ref[...]
        y_ref[...] = yv.astype(BF16)
        yt_ref[...] = yv.T.astype(BF16)

    row = lambda i: (i, 0)
    blk = pl.BlockSpec((tm, DA), row)
    lanes = pl.BlockSpec((tm, LANES), row)
    o_res, l_res = _residue_specs(tm, DA), _residue_specs(tm, LANES)
    return pl.pallas_call(
        body, name="mix", grid=(S // tm,),
        in_specs=[blk, lanes, o_res[0], l_res[0], o_res[1], l_res[1],
                  pl.BlockSpec((tm, DP), row),
                  pl.BlockSpec((HALO, DP), lambda i: (jnp.maximum(i * (tm // HALO) - 1, 0), 0)),
                  pl.BlockSpec((tm, DA), lambda i: (i, 1)), pl.BlockSpec((tm, DP), lambda i: (i, 2)),
                  pl.BlockSpec((N_GROUPS, GC, GC), lambda i: (0, 0, 0)),
                  pl.BlockSpec((1, DP), lambda i: (0, 0))],
        out_specs=[pl.BlockSpec((tm, D), row), pl.BlockSpec((D, tm), lambda i: (0, i)), blk, lanes] + l_res,
        out_shape=[SDS((S, D), BF16), SDS((D, S), BF16), SDS((S, DA), BF16), SDS((S, LANES), F32)]
        + [SDS((d, S // d, LANES), F32) for d in DILATIONS[1:]],
        scratch_shapes=[pltpu.VMEM((HALO + tm, DP), F32), pltpu.VMEM((tm, D), F32),
                        pltpu.VMEM((2, n_slab, tm, DH), F32), pltpu.VMEM((tm, LANES), F32)],
        compiler_params=_params(("arbitrary",)),
    )(o1, l1, o4, l4, o16, l16, hug, hug, hug, hug, wp, scale)


def _outproj_ln(y, wout, x, target, gain, bias):
    tm = 512
    te = 128

    def body(y_ref, w_ref, x_ref, t_ref, g_ref, b_ref, dz_ref, dzb_ref, st_ref, out_ref):
        i = pl.program_id(0)

        @pl.when(i == 0)
        def _():
            st_ref[...] = jnp.zeros((8, D), F32)

        out_ref[...] = _dot(y_ref[...], w_ref[...])
        gn = g_ref[...]
        for e in range(tm // te):
            rs = slice(e * te, (e + 1) * te)
            z = ALPHA * x_ref[rs, :] + out_ref[rs, :]
            mu = jnp.mean(z, axis=1, keepdims=True)
            zc = z - mu
            var = jnp.mean(zc * zc, axis=1, keepdims=True)
            rstd = lax.rsqrt(var + LN_EPS)
            xhat = zc * rstd
            diff = xhat * gn + b_ref[...] - t_ref[rs, :]
            dyln = diff / D
            st_ref[0:1, :] += jnp.sum(dyln * xhat, axis=0, keepdims=True)
            st_ref[1:2, :] += jnp.sum(dyln, axis=0, keepdims=True)
            row_loss = jnp.sum(diff * diff, axis=1, keepdims=True) / D
            st_ref[2:3, :] += jnp.broadcast_to(0.5 * jnp.sum(row_loss, axis=0, keepdims=True), (1, D))
            dxh = dyln * gn
            m1 = jnp.mean(dxh, axis=1, keepdims=True)
            m2 = jnp.mean(dxh * xhat, axis=1, keepdims=True)
            dz = rstd * (dxh - m1 - xhat * m2)
            dz_ref[rs, :] = dz
            dzb_ref[rs, :] = dz.astype(BF16)

    row = lambda i: (i, 0)
    const = lambda i: (0, 0)
    return pl.pallas_call(
        body, name="outproj_ln", grid=(S // tm,),
        in_specs=[pl.BlockSpec((tm, D), row),
                  pl.BlockSpec((D, D), const, pipeline_mode=pl.Buffered(1)),
                  pl.BlockSpec((tm, D), row), pl.BlockSpec((tm, D), row),
                  pl.BlockSpec((1, D), const), pl.BlockSpec((1, D), const)],
        out_specs=[pl.BlockSpec((tm, D), row), pl.BlockSpec((tm, D), row), pl.BlockSpec((8, D), const)],
        out_shape=[SDS((S, D), F32), SDS((S, D), BF16), SDS((8, D), F32)],
        scratch_shapes=[pltpu.VMEM((tm, D), F32)],
        compiler_params=_params(("arbitrary",)),
    )(y, wout, x, target, gain, bias)


def _bwd_mid(dzb, wout, hug, attn, wp, scale, rider=None):
    tm = 256
    n = S // tm

    def body(dz_ref, w_ref, ga_ref, gp_ref, at_ref, u_ref, uh_ref, wp_ref, sc_ref,
             dh_ref, do_ref, do4_ref, do16_ref, dd_ref, dd4_ref, dd16_ref, dwp_ref, dwp16_ref, gps_ref,
             ext_ref, eext_ref, acc_ref, nat_ref, ds_ref):
        i = pl.program_id(0)
        ib = n - 1 - i

        @pl.when(i == 0)
        def _():
            eext_ref[tm:tm + HALO, :] = jnp.zeros((HALO, DP), F32)
            acc_ref[...] = jnp.zeros((N_GROUPS, GC, GC), F32)
            gps_ref[...] = jnp.zeros((8, DP), F32)

        dy = _dot(dz_ref[...], w_ref[...], NT)

        def through_gate(dy_part, gt):
            sg = jax.nn.sigmoid(gt)
            return dy_part * (gt * sg), dy_part * (sg * (1.0 + gt * (1.0 - sg)))

        dat, dyg_a = through_gate(dy[:, 0:DA], ga_ref[...].astype(F32))
        dmix_p, dyg_p = through_gate(dy[:, DA:D], gp_ref[...].astype(F32))

        at = at_ref[...].astype(F32)
        do_ref[...] = dat.astype(BF16)
        dh_ref[:, DP:DP + DA] = (dyg_a * at).astype(BF16)
        prod = dat * at
        dd = jnp.zeros((tm, LANES), F32)
        for h in range(N_HEADS):
            hs = slice(h * DH, (h + 1) * DH)
            dd = _to_lane(dd, jnp.sum(prod[:, hs], axis=1, keepdims=True), h)
            nat_ref[h] = dat[:, hs]
        dd_ref[...] = dd
        ds_ref[...] = dd
        for d, do_out, dd_out in ((DILATIONS[1], do4_ref, dd4_ref), (DILATIONS[2], do16_ref, dd16_ref)):
            for r in range(d):
                dd_out[r] = _residues(ds_ref, (), d, r, tm // d)
                for h in range(N_HEADS):
                    do_out[r, :, h * DH:(h + 1) * DH] = _residues(nat_ref, (h,), d, r, tm // d).astype(BF16)

        _fill_ext(ext_ref, u_ref, uh_ref, ib, tm)
        rows = ib * tm + lax.broadcasted_iota(jnp.int32, (tm, 1), 0)
        for g in range(N_GROUPS):
            w = POOL_WINDOWS[g]
            cs = slice(g * GC, (g + 1) * GC)
            pooled, cnt = _pooled(ext_ref, g, rows, tm)
            pre = _dot(pooled.astype(BF16), wp_ref[g])
            sc = sc_ref[:, cs]
            dpo = dmix_p[:, cs]
            gps_ref[0:1, cs] += jnp.sum(dpo * pre, axis=0, keepdims=True)
            dh_ref[:, DP + DA + g * GC:DP + DA + (g + 1) * GC] = (dyg_p[:, cs] * (pre * sc)).astype(BF16)
            dpre = (dpo * sc).astype(BF16)
            acc_ref[g] += _dot(pooled.T.astype(BF16), dpre)
            dpooled = _dot(dpre, wp_ref[g], NT)
            eext_ref[0:tm, cs] = dpooled / cnt
            du = eext_ref[0:tm, cs]
            for j in range(1, w):
                du = du + eext_ref[j:j + tm, cs]
            dh_ref[:, cs] = (du - dpooled).astype(BF16)
        eext_ref[tm:tm + HALO, :] = eext_ref[0:HALO, :]

        @pl.when(i == n - 1)
        def _():
            for j in range(N_DEV):
                for g in range(N_GROUPS):
                    blk = acc_ref[g, j * PB:(j + 1) * PB, :]
                    dwp_ref[j, g * PB:(g + 1) * PB, :] = blk
                    dwp16_ref[j, g * PB:(g + 1) * PB, :] = blk.astype(BF16)

    rev = lambda i: (n - 1 - i, 0)
    const = lambda i: (0, 0)
    res = lambda width: [pl.BlockSpec((d, tm // d, width), lambda i: (0, n - 1 - i, 0)) for d in DILATIONS[1:]]
    pool_blocks = pl.BlockSpec((N_DEV, N_GROUPS * PB, GC), lambda i: (0, 0, 0))
    outs, carried = _call_carrying(
        body, rider, (n,),
        [pl.BlockSpec((tm, D), rev),
         pl.BlockSpec((D, D), const, pipeline_mode=pl.Buffered(1)),
         pl.BlockSpec((tm, DA), lambda i: (n - 1 - i, 1)), pl.BlockSpec((tm, DP), lambda i: (n - 1 - i, 2)),
         pl.BlockSpec((tm, DA), rev), pl.BlockSpec((tm, DP), rev),
         pl.BlockSpec((HALO, DP), lambda i: (jnp.maximum((n - 1 - i) * (tm // HALO) - 1, 0), 0)),
         pl.BlockSpec((N_GROUPS, GC, GC), lambda i: (0, 0, 0)),
         pl.BlockSpec((1, DP), const)],
        [pl.BlockSpec((tm, DP + D), lambda i: (n - 1 - i, 1)), pl.BlockSpec((tm, DA), rev)] + res(DA)
        + [pl.BlockSpec((tm, LANES), rev)] + res(LANES) + [pool_blocks, pool_blocks, pl.BlockSpec((8, DP), const)],
        [SDS((S, D_IN), BF16), SDS((S, DA), BF16)] + [SDS((d, S // d, DA), BF16) for d in DILATIONS[1:]]
        + [SDS((S, LANES), F32)] + [SDS((d, S // d, LANES), F32) for d in DILATIONS[1:]]
        + [SDS((N_DEV, N_GROUPS * PB, GC), F32), SDS((N_DEV, N_GROUPS * PB, GC), BF16), SDS((8, DP), F32)],
        [pltpu.VMEM((HALO + tm, DP), F32), pltpu.VMEM((tm + HALO, DP), F32),
         pltpu.VMEM((N_GROUPS, GC, GC), F32), pltpu.VMEM((N_HEADS, tm, DH), F32), pltpu.VMEM((tm, LANES), F32)],
        [dzb, wout, hug, hug, attn, hug, hug, wp, scale], "bwd_mid")
    return outs, carried


def _attn_bwd(h16, do, lse, dd, rope, name, others=None, dh=None, rider=None):
    d, n_sub, _ = h16.shape
    n_i = n_sub // BQ
    n_kb = n_sub // KB
    qw = BQ + KB
    final = others is not None
    out_dtype = BF16
    n_cb = 3 * DA // DH

    def nxt(i):
        return jnp.minimum(2 * i + 2, n_kb - 1)

    def body(qc_ref, qn_ref, kc_ref, vc_ref, doc_ref, don_ref, lc_ref, ln_ref, dc_ref, dn_ref,
             c_ref, sn_ref, *rest):
        if final:
            acc4_ref, acc16_ref, _, out_ref, carry_ref, qw_ref, dow_ref, lw_ref, dw_ref, nat_ref = rest
        else:
            out_ref, carry_ref, qw_ref, dow_ref, lw_ref, dw_ref = rest
        i = pl.program_id(1)

        @pl.when(i == 0)
        def _():
            carry_ref[...] = jnp.zeros((KB, DA), F32)

        for win, own, after in ((qw_ref, qc_ref, qn_ref), (dow_ref, doc_ref, don_ref), (lw_ref, lc_ref, ln_ref),
                                (dw_ref, dc_ref, dn_ref)):
            win[0:BQ, :] = own[...]
            win[BQ:qw, :] = after[...]
        if final:
            for n, (dil, acc) in enumerate(((DILATIONS[1], acc4_ref), (DILATIONS[2], acc16_ref))):
                for r in range(dil):
                    for cb in range(n_cb):
                        nat_ref[n, cb, pl.ds(r, BQ // dil, stride=dil), :] = acc[r, :, cb * DH:(cb + 1) * DH].astype(F32)

        a = lax.broadcasted_iota(jnp.int32, (qw, BQ), 0)
        b = lax.broadcasted_iota(jnp.int32, (qw, BQ), 1)
        n_q = jnp.where(i == n_i - 1, BQ, qw)
        mask = (b <= a) & (a <= b + KB) & (a < n_q)
        tok = pl.ds(pl.program_id(0) + d * BQ * i, BQ, stride=d)
        tabs = (c_ref[tok, :], sn_ref[tok, :])
        for h in range(N_HEADS):
            hs = slice(h * DH, (h + 1) * DH)
            hc = slice(h, h + 1)
            q, k, vv, dob = qw_ref[:, hs], kc_ref[:, hs], vc_ref[:, hs], dow_ref[:, hs]
            s = _dot(q, k, NT) * SCALE
            p = jnp.exp(jnp.where(mask, s - lw_ref[:, hc], NEG))
            dp = _dot(dob, vv, NT)
            ds = (p * (dp - dw_ref[:, hc]) * SCALE).astype(BF16)
            dq = _dot(ds, k)
            dk = _dot(ds, q, T_N)
            dv = _dot(p.astype(BF16), dob, T_N)
            dq_lo = dq[0:KB] + carry_ref[:, hs]
            carry_ref[:, hs] = dq[BQ:qw]
            dq_own = _rope_t(jnp.concatenate([dq_lo, dq[KB:BQ]], axis=0), *tabs)
            for base, gv in ((0, dq_own), (DA, _rope_t(dk, *tabs)), (2 * DA, dv)):
                if final:
                    cb = base // DH + h
                    gv = gv + nat_ref[0, cb] + nat_ref[1, cb]
                out_ref[:, base + h * DH:base + (h + 1) * DH] = gv.astype(out_dtype)

    def cur(rows, width, col=0):
        return pl.BlockSpec((None, rows, width), lambda r, i: (r, i, col))

    def nx(width, col=0):
        return pl.BlockSpec((None, KB, width), lambda r, i: (r, nxt(i), col))

    in_specs = [cur(BQ, DA), nx(DA), cur(BQ, DA, 1), cur(BQ, DA, 2), cur(BQ, DA), nx(DA),
                cur(BQ, LANES), nx(LANES), cur(BQ, LANES), nx(LANES)]
    in_specs += [pl.BlockSpec((S, DH), lambda r, i: (0, 0), pipeline_mode=pl.Buffered(1))] * 2
    args = [h16, h16, h16, h16, do, do, lse, lse, dd, dd] + list(rope)
    scratch = [pltpu.VMEM((KB, DA), F32), pltpu.VMEM((qw, DA), BF16), pltpu.VMEM((qw, DA), BF16),
               pltpu.VMEM((qw, LANES), F32), pltpu.VMEM((qw, LANES), F32)]
    if final:
        assert d == 1
        in_specs += [pl.BlockSpec((dil, BQ // dil, 3 * DA), lambda r, i: (0, i, 0)) for dil in DILATIONS[1:]]
        in_specs.append(pl.BlockSpec(memory_space=pl.ANY))
        args += list(others) + [dh[None]]
        scratch.append(pltpu.VMEM((2, n_cb, BQ, DH), F32))
    outs, carried = _call_carrying(
        body, rider, (d, n_i), in_specs, [cur(BQ, 3 * DA)], [SDS((d, n_sub, D_IN if final else 3 * DA), out_dtype)],
        scratch, args, name, {len(args) - 1: 0} if final else {})
    return outs[0] if rider is None else (outs[0], carried)


def _grad_w_out(yt, dzb):
    tn = 256

    def body(yt_ref, dz_ref, o_ref, o16_ref):
        acc = _dot(yt_ref[...], dz_ref[...])
        o_ref[...] = acc
        o16_ref[...] = acc.astype(BF16)

    cols = pl.BlockSpec((D, tn), lambda c: (0, c))
    o, o16 = pl.pallas_call(
        body, name="grad_w_out", grid=(D // tn,),
        in_specs=[pl.BlockSpec((D, S), lambda c: (0, 0), pipeline_mode=pl.Buffered(1)),
                  pl.BlockSpec((S, tn), lambda c: (0, c))],
        out_specs=[cols, cols],
        out_shape=[SDS((D, D), F32), SDS((D, D), BF16)],
        compiler_params=_params(("arbitrary",)),
    )(yt, dzb)
    return o.reshape(N_DEV, OB, D), o16.reshape(N_DEV, OB, D)


def _grad_w_in(xt, dh, order):
    n_send = N_DEV // 2
    n_steps = D_IN // TN
    own0 = n_send * W_PHASES
    sums0 = own0 + W_PHASES
    step_of_send = [W_PHASES * k + W_PHASES + 1 for k in range(n_send)]

    def body(ord_ref, xt_ref, dh_ref, o_ref, o16_ref, r1_ref, s2_ref, st_ref, got_ref,
             piece_sems, send_sems, recv_sems, got_sem):
        c = pl.program_id(0)
        x, y, core, _ = _place()

        def piece(cc):
            blk, sub = ord_ref[cc // W_PHASES], cc % W_PHASES
            return pltpu.make_async_copy(st_ref.at[cc % 2], o16_ref.at[blk, :, pl.ds(sub * TN, TN)], piece_sems.at[cc % 2])

        def send(k):
            return pltpu.make_async_remote_copy(
                src_ref=o16_ref.at[ord_ref[k]], dst_ref=r1_ref.at[k], send_sem=send_sems.at[k], recv_sem=recv_sems.at[k],
                device_id=(x, y, 1 - core), device_id_type=MESH)

        def sibling_part(cc):
            k, sub = cc // W_PHASES - n_send, cc % W_PHASES
            return pltpu.make_async_copy(r1_ref.at[k, :, pl.ds(sub * TN, TN)], got_ref, got_sem)

        @pl.when((c >= 2) & (c < own0 + 2))
        def _():
            piece(c - 2).wait()

        for k in range(n_send):
            pl.when(c == step_of_send[k])(lambda k=k: send(k).start())
        for k in range(1, n_send):
            pl.when(c == own0 + W_PHASES * k)(lambda k=k: send(k).wait_recv())
        pl.when(c >= sums0)(lambda: sibling_part(c).start())

        acc = _dot(xt_ref[...], dh_ref[...])
        o_ref[...] = acc

        @pl.when(c < own0)
        def _():
            st_ref[c % 2] = acc.astype(BF16)
            piece(c).start()

        @pl.when(c >= sums0)
        def _():
            sibling_part(c).wait()
            s2_ref[...] = (acc + got_ref[...].astype(F32)).astype(BF16)

        @pl.when(c == n_steps - 1)
        def _():
            send(0).wait_recv()
            for k in range(n_send):
                send(k).wait_send()

    col = lambda c, ordr: W_PHASES * ordr[c // W_PHASES] + c % W_PHASES
    hbm = pl.BlockSpec(memory_space=pl.ANY)
    sums_blk = lambda c, ordr: (jnp.maximum(c // W_PHASES - (n_send + 1), 0), 0, jnp.where(c >= sums0, c % W_PHASES, 0))
    partial, _, got, sums = pl.pallas_call(
        body, name="grad_w_in",
        grid_spec=pltpu.PrefetchScalarGridSpec(
            num_scalar_prefetch=1, grid=(n_steps,),
            in_specs=[pl.BlockSpec((D, S), lambda c, ordr: (0, 0), pipeline_mode=pl.Buffered(1)),
                      pl.BlockSpec((S, TN), lambda c, ordr: (0, col(c, ordr)))],
            out_specs=[pl.BlockSpec((None, D, TN), lambda c, ordr: (ordr[c // W_PHASES], 0, c % W_PHASES)), hbm, hbm,
                       pl.BlockSpec((None, D, TN), sums_blk)],
            scratch_shapes=[pltpu.VMEM((2, D, TN), BF16), pltpu.VMEM((D, TN), BF16), pltpu.SemaphoreType.DMA((2,)),
                            pltpu.SemaphoreType.DMA((n_send,)), pltpu.SemaphoreType.DMA((n_send,)),
                            pltpu.SemaphoreType.DMA(())]),
        out_shape=[SDS((N_DEV, D, WB), F32), SDS((N_DEV, D, WB), BF16), SDS((n_send, D, WB), BF16),
                   SDS((n_send - 1, D, WB), BF16)],
        compiler_params=_params(("arbitrary",)),
    )(order, xt, dh)
    return partial, got, sums


def _grad_x(dz, dh, wgs, rider=None):
    tm = 1024

    def body(dz_ref, dh_ref, *rest):
        o_ref = rest[-1]

        @pl.when(pl.program_id(1) == 0)
        def _():
            o_ref[...] = ALPHA * dz_ref[...]

        acc = _dot(dh_ref[:, 0:TN], rest[0][...], NT)
        for s in range(1, W_PHASES):
            acc = acc + _dot(dh_ref[:, s * TN:(s + 1) * TN], rest[s][...], NT)
        o_ref[...] += acc

    outs, carried = _call_carrying(
        body, rider, (S // tm, N_DEV),
        [pl.BlockSpec((tm, D), lambda i, j: (i, 0)), pl.BlockSpec((tm, WB), lambda i, j: (i, j))]
        + [pl.BlockSpec((None, D, TN), lambda i, j: (j, 0, 0))] * W_PHASES,
        [pl.BlockSpec((tm, D), lambda i, j: (i, 0))], [SDS((S, D), F32)], [], [dz, dh] + list(wgs), "grad_x")
    return outs[0], carried


def _place():
    x, y, c = lax.axis_index("x"), lax.axis_index("y"), lax.axis_index("c")
    chips = [(x, y), (1 - x, y), (x, 1 - y), (1 - x, 1 - y)]
    return x, y, c, chips


def _blk(x, y, c):
    return 4 * x + 2 * y + c


def _adamw(w, g, m, v):
    m = ADAM_B1 * m + (1.0 - ADAM_B1) * g
    v = ADAM_B2 * v + (1.0 - ADAM_B2) * (g * g)
    m_hat = m / (1.0 - ADAM_B1 ** ADAM_STEP)
    v_hat = v / (1.0 - ADAM_B2 ** ADAM_STEP)
    delta = -ADAM_LR * (m_hat / (jnp.sqrt(v_hat) + ADAM_EPS) + ADAM_WD * w)
    return delta, m, v


class _Rider:
    def __init__(self, args, in_specs, out_shape, out_specs, scratch, start, finish):
        self.args, self.in_specs, self.out_shape, self.out_specs = args, in_specs, out_shape, out_specs
        self.scratch, self.start, self.finish = scratch, start, finish


def _carry(body, rider, n_in, n_out, first, last):
    if rider is None:
        return body
    r_in, r_out, r_scr = len(rider.args), len(rider.out_shape), len(rider.scratch)

    def carrying(*refs):
        o0 = n_in + r_in
        s0 = o0 + n_out + r_out
        s1 = len(refs) - r_scr
        theirs = (refs[n_in:o0], refs[o0 + n_out:s0], refs[s1:])
        pl.when(first())(lambda: rider.start(*theirs))
        body(*refs[:n_in], *refs[o0:o0 + n_out], *refs[s0:s1])
        pl.when(last())(lambda: rider.finish(*theirs))

    return carrying


def _call_carrying(body, rider, grid, in_specs, out_specs, out_shape, scratch, args, name, aliases=None):
    n_in, n_out = len(in_specs), len(out_specs)
    ids = lambda: [pl.program_id(a) for a in range(len(grid))]
    first = lambda: functools.reduce(jnp.logical_and, [i == 0 for i in ids()])
    last = lambda: functools.reduce(jnp.logical_and, [i == n - 1 for i, n in zip(ids(), grid)])
    if rider is not None:
        in_specs, args = in_specs + rider.in_specs, list(args) + rider.args
        out_specs, out_shape = out_specs + rider.out_specs, out_shape + rider.out_shape
        scratch = scratch + rider.scratch
    outs = pl.pallas_call(
        _carry(body, rider, n_in, n_out, first, last), name=name, grid=grid,
        in_specs=in_specs, out_specs=out_specs, out_shape=out_shape, scratch_shapes=scratch,
        input_output_aliases=aliases or {}, compiler_params=_params(("arbitrary",) * len(grid)),
    )(*args)
    return list(outs[:n_out]), list(outs[n_out:])


def _gather_copy(tensors, send_sems, recv_sems, t, k, block, to, src=None):
    dst = tensors[t][1](_blk(*block))
    return pltpu.make_async_remote_copy(
        src_ref=dst if src is None else src, dst_ref=dst,
        send_sem=send_sems.at[t, k], recv_sem=recv_sems.at[t, k], device_id=to, device_id_type=MESH)


def _gather_start(tensors, send_sems, recv_sems, local_sems):
    x, y, c, chips = _place()
    me, sib = (x, y, c), (x, y, 1 - c)
    for t, (src, dst) in enumerate(tensors):
        pltpu.make_async_copy(src, dst(_blk(*me)), local_sems.at[t]).start()
        _gather_copy(tensors, send_sems, recv_sems, t, 0, me, sib, src).start()
        for j in (1, 2, 3):
            _gather_copy(tensors, send_sems, recv_sems, t, j, me, (*chips[j], c), src).start()


def _gather_finish(tensors, send_sems, recv_sems, local_sems):
    x, y, c, chips = _place()
    me, sib = (x, y, c), (x, y, 1 - c)
    copy = functools.partial(_gather_copy, tensors, send_sems, recv_sems)
    for t in range(len(tensors)):
        for j in (1, 2, 3):
            copy(t, j, (*chips[j], c), me).wait_recv()
            copy(t, 3 + j, (*chips[j], c), sib).start()
    for t, (src, dst) in enumerate(tensors):
        copy(t, 0, sib, me).wait_recv()
        for j in (1, 2, 3):
            copy(t, 3 + j, (*chips[j], 1 - c), me).wait_recv()
        copy(t, 0, me, sib, src).wait_send()
        for j in (1, 2, 3):
            copy(t, j, me, (*chips[j], c), src).wait_send()
            copy(t, 3 + j, (*chips[j], c), sib).wait_send()
        pltpu.make_async_copy(src, dst(_blk(*me)), local_sems.at[t]).wait()


W_PHASES = WB // TN


def _gather_first_rider(w_in_s, w_out_s, w_pool_s):
    shapes = [w_in_s.shape, w_out_s.shape, w_pool_s.shape]

    def keeps(outs, scr):
        return [pltpu.make_async_copy(scr[n], outs[n], scr[6].at[n]) for n in range(3)]

    def tensors(outs, scr):
        return [(scr[0].at[:, pl.ds(0, TN)], lambda b: outs[3].at[b])]

    def start(ins, outs, scr):
        for n in range(3):
            scr[n][...] = ins[n][...].astype(BF16)
        for keep in keeps(outs, scr):
            keep.start()
        _gather_start(tensors(outs, scr), *scr[3:6])

    def finish(ins, outs, scr):
        _gather_finish(tensors(outs, scr), *scr[3:6])
        for keep in keeps(outs, scr):
            keep.wait()

    vmem = pl.BlockSpec(memory_space=pltpu.VMEM)
    hbm = pl.BlockSpec(memory_space=pl.ANY)
    return _Rider(
        args=[w_in_s, w_out_s, w_pool_s], in_specs=[vmem] * 3,
        out_shape=[SDS(sh, BF16) for sh in shapes] + [SDS((N_DEV, D, TN), BF16)], out_specs=[hbm] * 4,
        scratch=[pltpu.VMEM(sh, BF16) for sh in shapes]
        + [pltpu.SemaphoreType.DMA((1, 7)), pltpu.SemaphoreType.DMA((1, 7)), pltpu.SemaphoreType.DMA((1,)),
           pltpu.SemaphoreType.DMA((3,))],
        start=start, finish=finish)


def _gather_w_in_rider(shard16, s):
    def tensors(ins, outs):
        return [(ins[0].at[:, pl.ds(s * TN, TN)], lambda b: outs[0].at[b])]

    hbm = pl.BlockSpec(memory_space=pl.ANY)
    return _Rider(
        args=[shard16], in_specs=[hbm], out_shape=[SDS((N_DEV, D, TN), BF16)], out_specs=[hbm],
        scratch=[pltpu.SemaphoreType.DMA((1, 7)), pltpu.SemaphoreType.DMA((1, 7)), pltpu.SemaphoreType.DMA((1,))],
        start=lambda ins, outs, scr: _gather_start(tensors(ins, outs), *scr),
        finish=lambda ins, outs, scr: _gather_finish(tensors(ins, outs), *scr))


def _gather_small_rider(w_out16, w_pool16):
    def tensors(ins, outs):
        gout_ref, gpool_ref = outs

        def pool_rows(b):
            return gpool_ref.at[:, pl.ds(pl.multiple_of(b * PB, PB), PB), :]

        return [(ins[0], lambda b: gout_ref.at[b]), (ins[1], pool_rows)]

    hbm = pl.BlockSpec(memory_space=pl.ANY)
    return _Rider(
        args=[w_out16, w_pool16], in_specs=[hbm, hbm],
        out_shape=[SDS((N_DEV, OB, D), BF16), SDS((N_GROUPS, GC, GC), BF16)], out_specs=[hbm, hbm],
        scratch=[pltpu.SemaphoreType.DMA((2, 7)), pltpu.SemaphoreType.DMA((2, 7)), pltpu.SemaphoreType.DMA((2,))],
        start=lambda ins, outs, scr: _gather_start(tensors(ins, outs), *scr),
        finish=lambda ins, outs, scr: _gather_finish(tensors(ins, outs), *scr))


def _block_table():
    x, y, c, chips = _place()
    return jnp.stack([_blk(*chip, c) for chip in chips]).astype(jnp.int32)


def _sibling_first_order():
    x, y, c, chips = _place()
    return jnp.stack([_blk(*chip, side) for side in (1 - c, c) for chip in chips]).astype(jnp.int32)


def _exchange_rider(p16):
    _, r_tot, cols = p16.shape

    def copies(ins, outs, scr):
        x, y, c, chips = _place()
        return [pltpu.make_async_remote_copy(
            src_ref=ins[0].at[_blk(*chips[k], 1 - c)], dst_ref=outs[0].at[k],
            send_sem=scr[0].at[k], recv_sem=scr[1].at[k], device_id=(x, y, 1 - c), device_id_type=MESH)
            for k in range(4)]

    def start(ins, outs, scr):
        for cp in copies(ins, outs, scr):
            cp.start()

    def finish(ins, outs, scr):
        for cp in copies(ins, outs, scr):
            cp.wait()

    hbm = pl.BlockSpec(memory_space=pl.ANY)
    return _Rider([p16], [hbm], [SDS((4, r_tot, cols), BF16)], [hbm],
                  [pltpu.SemaphoreType.DMA((4,)), pltpu.SemaphoreType.DMA((4,))], start, finish)


def _chip_sums(name, table, p32, r1, rc):
    _, r_tot, cols = p32.shape

    def body(tbl_ref, p_ref, r_ref, o_ref):
        o_ref[...] = (p_ref[...] + r_ref[...].astype(F32)).astype(BF16)

    return pl.pallas_call(
        body, name=name,
        grid_spec=pltpu.PrefetchScalarGridSpec(
            num_scalar_prefetch=1, grid=(3, r_tot // rc),
            in_specs=[pl.BlockSpec((None, rc, cols), lambda k, ch, tbl: (tbl[k + 1], ch, 0)),
                      pl.BlockSpec((None, rc, cols), lambda k, ch, tbl: (k + 1, ch, 0))],
            out_specs=pl.BlockSpec((None, rc, cols), lambda k, ch, tbl: (k, ch, 0))),
        out_shape=SDS((3, r_tot, cols), BF16),
        compiler_params=_params(("arbitrary", "arbitrary")),
    )(table, p32, r1)


def _stage2_rider(sums, stats=None):
    n_t = len(sums)

    def copies(ins, outs, scr):
        x, y, c, chips = _place()
        out = []
        for t in range(n_t):
            for k in (1, 2, 3):
                out.append(pltpu.make_async_remote_copy(
                    src_ref=ins[t].at[k - 1], dst_ref=outs[t].at[k - 1],
                    send_sem=scr[0].at[t, k - 1], recv_sem=scr[1].at[t, k - 1],
                    device_id=(*chips[k], c), device_id_type=MESH))
        if stats is not None:
            for k in range(1, N_DEV):
                peer = (x ^ ((k >> 2) & 1), y ^ ((k >> 1) & 1), c ^ (k & 1))
                out.append(pltpu.make_async_remote_copy(
                    src_ref=scr[4], dst_ref=outs[n_t].at[_blk(x, y, c)],
                    send_sem=scr[2].at[k - 1], recv_sem=scr[3].at[k - 1], device_id=peer, device_id_type=MESH))
        return out

    def own_rows(outs, scr):
        x, y, c, _ = _place()
        return pltpu.make_async_copy(scr[4], outs[n_t].at[_blk(x, y, c)], scr[5])

    def start(ins, outs, scr):
        if stats is not None:
            scr[4][...] = ins[n_t][...]
            scr[4][3:4, 0:DP] = ins[n_t + 1][0:1, :]
            own_rows(outs, scr).start()
        for cp in copies(ins, outs, scr):
            cp.start()

    def finish(ins, outs, scr):
        for cp in copies(ins, outs, scr):
            cp.wait()
        if stats is not None:
            own_rows(outs, scr).wait()

    vmem = pl.BlockSpec(memory_space=pltpu.VMEM)
    hbm = pl.BlockSpec(memory_space=pl.ANY)
    scratch = [pltpu.SemaphoreType.DMA((n_t, 3)), pltpu.SemaphoreType.DMA((n_t, 3))]
    args, in_specs = list(sums), [hbm] * n_t
    out_shape, out_specs = [SDS(s.shape, BF16) for s in sums], [hbm] * n_t
    if stats is not None:
        scratch += [pltpu.SemaphoreType.DMA((N_DEV - 1,)), pltpu.SemaphoreType.DMA((N_DEV - 1,)),
                    pltpu.VMEM((8, D), F32), pltpu.SemaphoreType.DMA(())]
        args, in_specs = args + list(stats), in_specs + [vmem, vmem]
        out_shape, out_specs = out_shape + [SDS((N_DEV, 8, D), F32)], out_specs + [hbm]
    return _Rider(args, in_specs, out_shape, out_specs, scratch, start, finish)


def _adamw_shard(name, table, p32, r1, r2, w, m, v, rc):
    _, r_tot, cols = p32.shape

    def body(tbl_ref, p_ref, r1_ref, r2_ref, w_ref, m_ref, v_ref, g_ref, d_ref, nm_ref, nv_ref):
        g = p_ref[...] + r1_ref[...].astype(F32)
        for k in range(3):
            g = g + r2_ref[k].astype(F32)
        delta, nm, nv = _adamw(w_ref[...], g, m_ref[...], v_ref[...])
        g_ref[...] = g
        d_ref[...] = delta
        nm_ref[...] = nm
        nv_ref[...] = nv

    rows = pl.BlockSpec((rc, cols), lambda ch, tbl: (ch, 0))
    shard = SDS((r_tot, cols), F32)
    return pl.pallas_call(
        body, name=name,
        grid_spec=pltpu.PrefetchScalarGridSpec(
            num_scalar_prefetch=1, grid=(r_tot // rc,),
            in_specs=[pl.BlockSpec((None, rc, cols), lambda ch, tbl: (tbl[0], ch, 0)),
                      pl.BlockSpec((None, rc, cols), lambda ch, tbl: (0, ch, 0)),
                      pl.BlockSpec((3, rc, cols), lambda ch, tbl: (0, ch, 0)), rows, rows, rows],
            out_specs=[rows, rows, rows, rows]),
        out_shape=[shard, shard, shard, shard],
        compiler_params=_params(("arbitrary",)),
    )(table, p32, r1, r2, w, m, v)


def _replicated_adamw(gathered, gain, bias, scale, m_gain, m_bias, m_scale, v_gain, v_bias, v_scale):
    def body(all_ref, s_ref, g_ref, b_ref, ms_ref, mg_ref, mb_ref, vs_ref, vg_ref, vb_ref, loss_ref, *outs):
        tot = all_ref[0]
        for b in range(1, N_DEV):
            tot = tot + all_ref[b]
        loss_ref[...] = tot[2:3, 0:LANES]
        for n, (row, width, w_r, m_r, v_r) in enumerate(((3, DP, s_ref, ms_ref, vs_ref), (0, D, g_ref, mg_ref, vg_ref),
                                                         (1, D, b_ref, mb_ref, vb_ref))):
            g = tot[row:row + 1, 0:width]
            delta, nm, nv = _adamw(w_r[...], g, m_r[...], v_r[...])
            for out, val in zip(outs[4 * n:4 * n + 4], (g, delta, nm, nv)):
                out[...] = val

    vmem = pl.BlockSpec(memory_space=pltpu.VMEM)
    shapes = [SDS((1, LANES), F32)] + [SDS((1, width), F32) for width in (DP, D, D) for _ in range(4)]
    return pl.pallas_call(
        body, name="replicated_adamw",
        in_specs=[vmem] * 10, out_specs=[vmem] * len(shapes), out_shape=shapes,
        compiler_params=_params(),
    )(gathered, scale, gain, bias, m_scale, m_gain, m_bias, v_scale, v_gain, v_bias)


def kernel(x, w_in, w_pool, pool_scale, w_out, ln_gain, ln_bias, loss_target, m_w_in, m_w_pool, m_pool_scale, m_w_out, m_ln_gain, m_ln_bias, v_w_in, v_w_pool, v_pool_scale, v_w_out, v_ln_gain, v_ln_bias):
    pool_rows = (N_GROUPS * PB, GC)
    x2, target = x[0], loss_target[0]
    rope = _rope_tables()
    table = _block_table()

    (xb, xt), (shard16, w_out16, w_pool16, wg0) = _prep_x(x2, _gather_first_rider(w_in[0], w_out[0], w_pool[0]))
    h, (wg1,) = _proj_phase(xb, wg0, 0, rope, None, _gather_w_in_rider(shard16, 1))
    h, (wg2,) = _proj_phase(xb, wg1, 1, rope, h, _gather_w_in_rider(shard16, 2))
    h, (wg_out, wg_pool) = _proj_phase(xb, wg2, 2, rope, h, _gather_small_rider(w_out16, w_pool16))
    wg_out = wg_out.reshape(D, D)
    h16, hug = [h[0][None], h[1], h[2]], h[3]
    fwd = [_attn_fwd(h16[n], f"attn_fwd_d{d}") for n, d in enumerate(DILATIONS)]
    y, yt, attn, lse1, lse4, lse16 = _mix(fwd[0][0][0], fwd[0][1][0], *fwd[1], *fwd[2], hug, wg_pool, pool_scale)
    dz, dzb, stats = _outproj_ln(y, wg_out, x2, target, ln_gain, ln_bias)

    dwout, dwout16 = _grad_w_out(yt, dzb)
    mid, (r1_out,) = _bwd_mid(dzb, wg_out, hug, attn, wg_pool, pool_scale, rider=_exchange_rider(dwout16))
    dh, do1, do4, do16, dd1, dd4, dd16, dwp, dwp16, gps = mid
    s2_out = _chip_sums("rs_out_sums", table, dwout, r1_out, OB)

    do, lse, dd = [do1[None], do4, do16], [lse1[None], lse4, lse16], [dd1[None], dd4, dd16]
    acc4, (r2_out,) = _attn_bwd(h16[1], do[1], lse[1], dd[1], rope, "attn_bwd_d4", rider=_stage2_rider([s2_out]))
    g_out, d_out, nm_out, nv_out = _adamw_shard(
        "adamw_w_out", table, dwout, r1_out, r2_out, w_out[0], m_w_out[0], v_w_out[0], OB // 2)
    acc16, (r1_pool,) = _attn_bwd(h16[2], do[2], lse[2], dd[2], rope, "attn_bwd_d16", rider=_exchange_rider(dwp16))
    s2_pool = _chip_sums("rs_pool_sums", table, dwp, r1_pool, N_GROUPS * PB)
    dh = _attn_bwd(h16[0], do[0], lse[0], dd[0], rope, "attn_bwd_d1", others=(acc4, acc16), dh=dh)[0]

    dwin, r1_in, s2_in = _grad_w_in(xt, dh, _sibling_first_order())
    grad_x, (r2_in, r2_pool, gathered) = _grad_x(
        dz, dh, (wg0, wg1, wg2), rider=_stage2_rider([s2_in, s2_pool], stats=(stats, gps)))
    g_in, d_in, nm_in, nv_in = _adamw_shard(
        "adamw_w_in", table, dwin, r1_in, r2_in, w_in[0], m_w_in[0], v_w_in[0], 256)
    g_pool, d_pool, nm_pool, nv_pool = _adamw_shard(
        "adamw_w_pool", table, dwp, r1_pool, r2_pool, w_pool[0].reshape(pool_rows), m_w_pool[0].reshape(pool_rows),
        v_w_pool[0].reshape(pool_rows), N_GROUPS * PB)
    loss, *small = _replicated_adamw(gathered, ln_gain, ln_bias, pool_scale, m_ln_gain, m_ln_bias, m_pool_scale,
                                     v_ln_gain, v_ln_bias, v_pool_scale)
    (g_ps, d_ps, nm_ps, nv_ps, g_gain, d_gain, nm_gain, nv_gain, g_bias, d_bias, nm_bias, nv_bias) = small

    shard4 = lambda t: t.reshape(1, N_GROUPS, PB, GC)
    lead = lambda t: t[None]
    return (loss[0, 0], lead(grad_x),
            lead(g_in), shard4(g_pool), g_ps, lead(g_out), g_gain, g_bias,
            lead(d_in), shard4(d_pool), d_ps, lead(d_out), d_gain, d_bias,
            lead(nm_in), shard4(nm_pool), nm_ps, lead(nm_out), nm_gain, nm_bias,
            lead(nv_in), shard4(nv_pool), nv_ps, lead(nv_out), nv_gain, nv_bias)
```

```python
import functools

import jax
import jax.numpy as jnp
from jax import lax
from jax.experimental import pallas as pl
from jax.experimental.pallas import tpu as pltpu

F32 = jnp.float32
BF16 = jnp.bfloat16
SDS = jax.ShapeDtypeStruct
MESH = pl.DeviceIdType.MESH

N_DEV = 8
S = 4096
D = 2048
N_HEADS = 8
DH = 128
DA = N_HEADS * DH
DP = 1024
N_GROUPS = 4
GC = DP // N_GROUPS
POOL_WINDOWS = (2, 4, 8, 16)
HALO = 16
D_IN = 3 * DA + DP + D
WB = D_IN // N_DEV
TN = 256
HW = 3 * DA
OB = D // N_DEV
PB = GC // N_DEV
ROPE_DIM = DH // 4
ROPE_HALF = ROPE_DIM // 2
ROPE_THETA = 500000.0
DILATIONS = (1, 4, 16)
KB = 128
LN_EPS = 1e-5
ALPHA = 2.0 ** 0.25
SCALE = DH ** -0.5
NEG = -1e30
ADAM_LR, ADAM_B1, ADAM_B2, ADAM_EPS, ADAM_WD, ADAM_STEP = 0.001, 0.9, 0.999, 1e-08, 0.01, 10

VMEM_LIMIT_V7X = 61 * 1024 * 1024

NT = (((1,), (1,)), ((), ()))
T_N = (((0,), (0,)), ((), ()))


def _params(sem=None):
    return pltpu.CompilerParams(dimension_semantics=sem, vmem_limit_bytes=VMEM_LIMIT_V7X)


def _dot(a, b, dims=None):
    if dims is None:
        return jnp.dot(a, b, preferred_element_type=F32)
    return lax.dot_general(a, b, dims, preferred_element_type=F32)


def _rope_tables():
    inv_freq = ROPE_THETA ** (-(2.0 * jnp.arange(ROPE_HALF, dtype=F32)) / ROPE_DIM)
    ang = jnp.arange(S, dtype=jnp.int32).astype(F32)[:, None] * inv_freq[None, :]
    cos, sin = lax.optimization_barrier((jnp.cos(ang), jnp.sin(ang)))
    rest = DH - ROPE_DIM
    c = jnp.concatenate([cos, cos, jnp.ones((S, rest), F32)], axis=1)
    sn = jnp.concatenate([-sin, sin, jnp.zeros((S, rest), F32)], axis=1)
    return c, sn


def _rope_partner(t):
    lane = lax.broadcasted_iota(jnp.int32, t.shape, 1)
    return jnp.where(lane < ROPE_HALF, pltpu.roll(t, DH - ROPE_HALF, 1), pltpu.roll(t, ROPE_HALF, 1))


def _rope(t, c, sn):
    return t * c + _rope_partner(t) * sn


def _rope_t(g, c, sn):
    return g * c - _rope_partner(g) * sn


def _prep_x(x, rider):
    tm = 512

    def body(x_ref, xb_ref, xt_ref):
        xv = x_ref[...]
        xb_ref[...] = xv.astype(BF16)
        xt_ref[...] = xv.T.astype(BF16)

    return _call_carrying(
        body, rider, (S // tm,), [pl.BlockSpec((tm, D), lambda i: (i, 0))],
        [pl.BlockSpec((tm, D), lambda i: (i, 0)), pl.BlockSpec((D, tm), lambda i: (0, i))],
        [SDS((S, D), BF16), SDS((D, S), BF16)], [], [x], "prep_x")


def _residues(slab_ref, idx, d, r, n):
    return slab_ref[(*idx, pl.ds(r, n, stride=d), slice(None))]


def _proj_phase(xb, wg, s, rope, prev, rider):
    n_heads = TN // DH
    n16 = HW // TN
    j16 = n16 // W_PHASES
    n_rope = 2 * DA // TN
    once = pl.Buffered(1)

    def body(x_ref, w_ref, c_ref, sn_ref, *rest):
        h16_ref, h4_ref, h16r_ref, hug_ref, slab_ref, res_ref = rest[-6:]
        j = pl.program_id(0)
        d4, d16 = DILATIONS[1], DILATIONS[2]
        acc = _dot(x_ref[...], w_ref[...])

        def heads(with_rope):
            for hh in range(n_heads):
                hs = slice(hh * DH, (hh + 1) * DH)
                t = _rope(acc[:, hs], c_ref[...], sn_ref[...]) if with_rope else acc[:, hs]
                h16_ref[:, hs] = t.astype(BF16)
                slab_ref[...] = t
                for q in range(d4):
                    t4 = _residues(slab_ref, (), d4, q, S // d4)
                    h4_ref[q, :, hs] = t4.astype(BF16)
                    res_ref[...] = t4
                    for m in range(d16 // d4):
                        h16r_ref[d4 * m + q, :, hs] = _residues(res_ref, (), d4, m, S // d16).astype(BF16)

        pl.when(W_PHASES * j + s < n_rope)(lambda: heads(True))
        pl.when((W_PHASES * j + s >= n_rope) & (j < j16))(lambda: heads(False))

        @pl.when(j >= j16)
        def _():
            hug_ref[...] = acc.astype(BF16)

    col16 = lambda j: W_PHASES * jnp.minimum(j, j16 - 1) + s
    col32 = lambda j: W_PHASES * jnp.maximum(j, j16) + s - n16
    hbm = pl.BlockSpec(memory_space=pl.ANY)
    in_specs = [pl.BlockSpec((S, D), lambda j: (0, 0), pipeline_mode=once),
                pl.BlockSpec((None, D, TN), lambda j: (j, 0, 0))]
    in_specs += [pl.BlockSpec((S, DH), lambda j: (0, 0), pipeline_mode=once)] * 2
    args = [xb, wg] + list(rope)
    aliases = {}
    if prev is not None:
        aliases = {len(args) + n: n for n in range(4)}
        in_specs, args = in_specs + [hbm] * 4, args + list(prev)
    out_specs = [pl.BlockSpec((S, TN), lambda j: (0, col16(j)))]
    out_specs += [pl.BlockSpec((d, S // d, TN), lambda j: (0, 0, col16(j))) for d in DILATIONS[1:]]
    out_specs += [pl.BlockSpec((S, TN), lambda j: (0, col32(j)))]
    out_shape = [SDS((S, HW), BF16)] + [SDS((d, S // d, HW), BF16) for d in DILATIONS[1:]] + [SDS((S, HW), BF16)]
    return _call_carrying(
        body, rider, (N_DEV,), in_specs, out_specs, out_shape,
        [pltpu.VMEM((S, DH), F32), pltpu.VMEM((S // DILATIONS[1], DH), F32)], args,
        f"proj_phase{s}", aliases)


def _proj_all(xb, rope, wg0, shard16, w_out16, w_pool16):
    n_heads = TN // DH
    n16 = HW // TN
    j16 = n16 // W_PHASES
    n_rope = 2 * DA // TN
    once = pl.Buffered(1)
    riders = [_gather_w_in_rider(shard16, s) for s in range(1, W_PHASES)] + [_gather_small_rider(w_out16, w_pool16)]
    last_j = N_DEV - 1

    def body(*refs):
        x_ref, c_ref, sn_ref, g0_ref = refs[:4]
        pos = 4
        r_ins, r_outs, r_scrs = [], [], []
        for r in riders:
            r_ins.append(refs[pos:pos + len(r.args)])
            pos += len(r.args)
        h16_ref, h4_ref, h16r_ref, hug_ref = refs[pos:pos + 4]
        pos += 4
        for r in riders:
            r_outs.append(refs[pos:pos + len(r.out_shape)])
            pos += len(r.out_shape)
        slab_ref, res_ref, w_ref, w_sems = refs[pos:pos + 4]
        pos += 4
        for r in riders:
            r_scrs.append(refs[pos:pos + len(r.scratch)])
            pos += len(r.scratch)
        gathered = [g0_ref] + [r_outs[s][0] for s in range(W_PHASES - 1)]
        s, j = pl.program_id(0), pl.program_id(1)
        d4, d16 = DILATIONS[1], DILATIONS[2]

        def fetch(ph, blk, slot):
            return pltpu.make_async_copy(gathered[ph].at[blk], w_ref.at[slot], w_sems.at[slot])

        pl.when((s == 0) & (j == 0))(lambda: fetch(0, 0, 0).start())
        for ph in range(W_PHASES):
            pl.when((s == ph) & (j == 0))(lambda ph=ph: riders[ph].start(r_ins[ph], r_outs[ph], r_scrs[ph]))
            pl.when(s == ph)(lambda ph=ph: fetch(ph, j, j % 2).wait())
            pl.when((s == ph) & (j == last_j))(lambda ph=ph: riders[ph].finish(r_ins[ph], r_outs[ph], r_scrs[ph]))
            pl.when((s == ph) & (j < last_j))(lambda ph=ph: fetch(ph, j + 1, (j + 1) % 2).start())
            if ph + 1 < W_PHASES:
                pl.when((s == ph) & (j == last_j))(lambda ph=ph: fetch(ph + 1, 0, 0).start())

        acc = _dot(x_ref[...], w_ref[j % 2])

        def heads(with_rope):
            for hh in range(n_heads):
                hs = slice(hh * DH, (hh + 1) * DH)
                t = _rope(acc[:, hs], c_ref[...], sn_ref[...]) if with_rope else acc[:, hs]
                h16_ref[:, hs] = t.astype(BF16)
                slab_ref[...] = t
                for q in range(d4):
                    t4 = _residues(slab_ref, (), d4, q, S // d4)
                    h4_ref[q, :, hs] = t4.astype(BF16)
                    res_ref[...] = t4
                    for m in range(d16 // d4):
                        h16r_ref[d4 * m + q, :, hs] = _residues(res_ref, (), d4, m, S // d16).astype(BF16)

        pl.when(W_PHASES * j + s < n_rope)(lambda: heads(True))
        pl.when((W_PHASES * j + s >= n_rope) & (j < j16))(lambda: heads(False))

        @pl.when(j >= j16)
        def _():
            hug_ref[...] = acc.astype(BF16)

    col16 = lambda s, j: W_PHASES * jnp.minimum(j, j16 - 1) + s
    colug = lambda s, j: W_PHASES * jnp.maximum(j, j16) + s - n16
    hbm = pl.BlockSpec(memory_space=pl.ANY)
    resident = lambda shape: pl.BlockSpec(shape, lambda s, j: (0, 0), pipeline_mode=once)
    in_specs = [resident((S, D)), resident((S, DH)), resident((S, DH)), hbm]
    args = [xb, *rope, wg0]
    out_specs = [pl.BlockSpec((S, TN), lambda s, j: (0, col16(s, j)))]
    out_specs += [pl.BlockSpec((d, S // d, TN), lambda s, j: (0, 0, col16(s, j))) for d in DILATIONS[1:]]
    out_specs += [pl.BlockSpec((S, TN), lambda s, j: (0, colug(s, j)))]
    out_shape = [SDS((S, HW), BF16)] + [SDS((d, S // d, HW), BF16) for d in DILATIONS[1:]] + [SDS((S, HW), BF16)]
    scratch = [pltpu.VMEM((S, DH), F32), pltpu.VMEM((S // DILATIONS[1], DH), F32), pltpu.VMEM((2, D, TN), BF16),
               pltpu.SemaphoreType.DMA((2,))]
    for r in riders:
        in_specs, args = in_specs + r.in_specs, args + r.args
        out_specs, out_shape = out_specs + r.out_specs, out_shape + r.out_shape
        scratch = scratch + r.scratch
    outs = pl.pallas_call(
        body, name="proj_all", grid=(W_PHASES, N_DEV),
        in_specs=in_specs, out_specs=out_specs, out_shape=out_shape, scratch_shapes=scratch,
        compiler_params=_params(("arbitrary", "arbitrary")),
    )(*args)
    return list(outs[:4]), list(outs[4:])


BQ = 2 * KB
LANES = 128


def _to_lane(acc, col, h):
    lane = lax.broadcasted_iota(jnp.int32, acc.shape, 1)
    return jnp.where(lane == h, col, acc)


def _attn_fwd(h16, name):
    d, n_sub, _ = h16.shape
    n_i = n_sub // BQ
    kw = KB + BQ

    def prev(i):
        return jnp.maximum(2 * i - 1, 0)

    def body(q_ref, kc_ref, kp_ref, vc_ref, vp_ref, o_ref, l_ref, kw_ref, vw_ref):
        i = pl.program_id(1)
        kw_ref[0:KB, :] = kp_ref[...]
        kw_ref[KB:kw, :] = kc_ref[...]
        vw_ref[0:KB, :] = vp_ref[...]
        vw_ref[KB:kw, :] = vc_ref[...]
        a = lax.broadcasted_iota(jnp.int32, (KB, 2 * KB), 0)
        b = lax.broadcasted_iota(jnp.int32, (KB, 2 * KB), 1)
        band = (b >= a) & (b <= a + KB)
        first_key = jnp.where(i == 0, KB, 0)
        masks = (band & (b >= first_key), band)
        for half in range(2):
            rs = slice(half * KB, (half + 1) * KB)
            ks = slice(half * KB, (half + 2) * KB)
            lse = jnp.zeros((KB, LANES), F32)
            for h in range(N_HEADS):
                hs = slice(h * DH, (h + 1) * DH)
                s = jnp.where(masks[half], _dot(q_ref[rs, hs], kw_ref[ks, hs], NT) * SCALE, NEG)
                m = jnp.max(s, axis=1, keepdims=True)
                p = jnp.exp(s - m)
                den = jnp.sum(p, axis=1, keepdims=True)
                o_ref[rs, hs] = (_dot(p.astype(BF16), vw_ref[ks, hs]) / den).astype(BF16)
                lse = _to_lane(lse, m + jnp.log(den), h)
            l_ref[rs, :] = lse

    return pl.pallas_call(
        body, name=name, grid=(d, n_i),
        in_specs=[pl.BlockSpec((None, BQ, DA), lambda r, i: (r, i, 0)),
                  pl.BlockSpec((None, BQ, DA), lambda r, i: (r, i, 1)),
                  pl.BlockSpec((None, KB, DA), lambda r, i: (r, prev(i), 1)),
                  pl.BlockSpec((None, BQ, DA), lambda r, i: (r, i, 2)),
                  pl.BlockSpec((None, KB, DA), lambda r, i: (r, prev(i), 2))],
        out_specs=[pl.BlockSpec((None, BQ, DA), lambda r, i: (r, i, 0)),
                   pl.BlockSpec((None, BQ, LANES), lambda r, i: (r, i, 0))],
        out_shape=[SDS((d, n_sub, DA), BF16), SDS((d, n_sub, LANES), F32)],
        scratch_shapes=[pltpu.VMEM((kw, DA), BF16), pltpu.VMEM((kw, DA), BF16)],
        compiler_params=_params(("arbitrary", "arbitrary")),
    )(h16, h16, h16, h16, h16)


def _pooled(ext_ref, g, rows, tm):
    w = POOL_WINDOWS[g]
    cs = slice(g * GC, (g + 1) * GC)
    cur = ext_ref[HALO:HALO + tm, cs]
    win = cur
    for j in range(1, w):
        win = win + ext_ref[HALO - j:HALO - j + tm, cs]
    cnt = jnp.minimum(rows + 1, w).astype(F32)
    return win / cnt - cur, cnt


def _fill_ext(ext_ref, u_ref, uh_ref, blk, tm):
    @pl.when(blk == 0)
    def _():
        ext_ref[0:HALO, :] = jnp.zeros((HALO, DP), F32)

    @pl.when(blk > 0)
    def _():
        ext_ref[0:HALO, :] = uh_ref[...].astype(F32)

    ext_ref[HALO:HALO + tm, :] = u_ref[...].astype(F32)


def _residue_specs(tm, width):
    return [pl.BlockSpec((d, tm // d, width), lambda i: (0, i, 0)) for d in DILATIONS[1:]]


def _mix(o1, l1, o4, l4, o16, l16, hug, wp, scale):
    tm = 256
    n_slab = N_HEADS + 1

    def body(o1r, l1r, o4r, l4r, o16r, l16r, u_ref, uh_ref, ga_ref, gp_ref, wp_ref, sc_ref,
             y_ref, yt_ref, attn_ref, lse_ref, lse4_ref, lse16_ref, ext_ref, ys_ref, nat_ref, ls_ref):
        i = pl.program_id(0)
        for n, (d, o_r, l_r) in enumerate(((DILATIONS[1], o4r, l4r), (DILATIONS[2], o16r, l16r))):
            for r in range(d):
                rows = pl.ds(r, tm // d, stride=d)
                for h in range(N_HEADS):
                    nat_ref[n, h, rows, :] = o_r[r, :, h * DH:(h + 1) * DH].astype(F32)
                nat_ref[n, N_HEADS, rows, :] = l_r[r]
        la, lb, lc = l1r[...], nat_ref[0, N_HEADS], nat_ref[1, N_HEADS]
        mx = jnp.maximum(jnp.maximum(la, lb), lc)
        ea, eb, ec = jnp.exp(la - mx), jnp.exp(lb - mx), jnp.exp(lc - mx)
        z = ea + eb + ec
        wa, wb, wc = ea / z, eb / z, ec / z
        lse = mx + jnp.log(z)
        lse_ref[...] = lse
        ls_ref[...] = lse
        for d, out in ((DILATIONS[1], lse4_ref), (DILATIONS[2], lse16_ref)):
            for r in range(d):
                out[r] = ls_ref[pl.ds(r, tm // d, stride=d), :]
        for h in range(N_HEADS):
            hs = slice(h * DH, (h + 1) * DH)
            hc = slice(h, h + 1)
            attn = wa[:, hc] * o1r[:, hs].astype(F32) + wb[:, hc] * nat_ref[0, h] + wc[:, hc] * nat_ref[1, h]
            attn_ref[:, hs] = attn.astype(BF16)
            gt = ga_ref[:, hs].astype(F32)
            ys_ref[:, hs] = attn * (gt * jax.nn.sigmoid(gt))

        _fill_ext(ext_ref, u_ref, uh_ref, i, tm)
        rows = i * tm + lax.broadcasted_iota(jnp.int32, (tm, 1), 0)
        for g in range(N_GROUPS):
            cs = slice(g * GC, (g + 1) * GC)
            gs = slice(DA + g * GC, DA + (g + 1) * GC)
            pooled, _ = _pooled(ext_ref, g, rows, tm)
            po = _dot(pooled.astype(BF16), wp_ref[g]) * sc_ref[:, cs]
            gt = gp_ref[:, cs].astype(F32)
            ys_ref[:, gs] = po * (gt * jax.nn.sigmoid(gt))
        yv = ys_ref[...]
        y_ref[...] = yv.astype(BF16)
        yt_ref[...] = yv.T.astype(BF16)

    row = lambda i: (i, 0)
    blk = pl.BlockSpec((tm, DA), row)
    lanes = pl.BlockSpec((tm, LANES), row)
    o_res, l_res = _residue_specs(tm, DA), _residue_specs(tm, LANES)
    return pl.pallas_call(
        body, name="mix", grid=(S // tm,),
        in_specs=[blk, lanes, o_res[0], l_res[0], o_res[1], l_res[1],
                  pl.BlockSpec((tm, DP), row),
                  pl.BlockSpec((HALO, DP), lambda i: (jnp.maximum(i * (tm // HALO) - 1, 0), 0)),
                  pl.BlockSpec((tm, DA), lambda i: (i, 1)), pl.BlockSpec((tm, DP), lambda i: (i, 2)),
                  pl.BlockSpec((N_GROUPS, GC, GC), lambda i: (0, 0, 0)),
                  pl.BlockSpec((1, DP), lambda i: (0, 0))],
        out_specs=[pl.BlockSpec((tm, D), row), pl.BlockSpec((D, tm), lambda i: (0, i)), blk, lanes] + l_res,
        out_shape=[SDS((S, D), BF16), SDS((D, S), BF16), SDS((S, DA), BF16), SDS((S, LANES), F32)]
        + [SDS((d, S // d, LANES), F32) for d in DILATIONS[1:]],
        scratch_shapes=[pltpu.VMEM((HALO + tm, DP), F32), pltpu.VMEM((tm, D), F32),
                        pltpu.VMEM((2, n_slab, tm, DH), F32), pltpu.VMEM((tm, LANES), F32)],
        compiler_params=_params(("arbitrary",)),
    )(o1, l1, o4, l4, o16, l16, hug, hug, hug, hug, wp, scale)


def _outproj_ln(y, wout, x, target, gain, bias):
    tm = 512
    te = 128

    def body(y_ref, w_ref, x_ref, t_ref, g_ref, b_ref, dz_ref, dzb_ref, st_ref, out_ref):
        i = pl.program_id(0)

        @pl.when(i == 0)
        def _():
            st_ref[...] = jnp.zeros((8, D), F32)

        out_ref[...] = _dot(y_ref[...], w_ref[...])
        gn = g_ref[...]
        for e in range(tm // te):
            rs = slice(e * te, (e + 1) * te)
            z = ALPHA * x_ref[rs, :] + out_ref[rs, :]
            mu = jnp.mean(z, axis=1, keepdims=True)
            zc = z - mu
            var = jnp.mean(zc * zc, axis=1, keepdims=True)
            rstd = lax.rsqrt(var + LN_EPS)
            xhat = zc * rstd
            diff = xhat * gn + b_ref[...] - t_ref[rs, :]
            dyln = diff / D
            st_ref[0:1, :] += jnp.sum(dyln * xhat, axis=0, keepdims=True)
            st_ref[1:2, :] += jnp.sum(dyln, axis=0, keepdims=True)
            row_loss = jnp.sum(diff * diff, axis=1, keepdims=True) / D
            st_ref[2:3, :] += jnp.broadcast_to(0.5 * jnp.sum(row_loss, axis=0, keepdims=True), (1, D))
            dxh = dyln * gn
            m1 = jnp.mean(dxh, axis=1, keepdims=True)
            m2 = jnp.mean(dxh * xhat, axis=1, keepdims=True)
            dz = rstd * (dxh - m1 - xhat * m2)
            dz_ref[rs, :] = dz
            dzb_ref[rs, :] = dz.astype(BF16)

    row = lambda i: (i, 0)
    const = lambda i: (0, 0)
    return pl.pallas_call(
        body, name="outproj_ln", grid=(S // tm,),
        in_specs=[pl.BlockSpec((tm, D), row),
                  pl.BlockSpec((D, D), const, pipeline_mode=pl.Buffered(1)),
                  pl.BlockSpec((tm, D), row), pl.BlockSpec((tm, D), row),
                  pl.BlockSpec((1, D), const), pl.BlockSpec((1, D), const)],
        out_specs=[pl.BlockSpec((tm, D), row), pl.BlockSpec((tm, D), row), pl.BlockSpec((8, D), const)],
        out_shape=[SDS((S, D), F32), SDS((S, D), BF16), SDS((8, D), F32)],
        scratch_shapes=[pltpu.VMEM((tm, D), F32)],
        compiler_params=_params(("arbitrary",)),
    )(y, wout, x, target, gain, bias)


def _bwd_mid(dzb, wout, hug, attn, wp, scale, rider=None):
    tm = 256
    n = S // tm

    def body(dz_ref, w_ref, ga_ref, gp_ref, at_ref, u_ref, uh_ref, wp_ref, sc_ref,
             dh_ref, do_ref, do4_ref, do16_ref, dd_ref, dd4_ref, dd16_ref, dwp_ref, dwp16_ref, gps_ref,
             ext_ref, eext_ref, acc_ref, nat_ref, ds_ref):
        i = pl.program_id(0)
        ib = n - 1 - i

        @pl.when(i == 0)
        def _():
            eext_ref[tm:tm + HALO, :] = jnp.zeros((HALO, DP), F32)
            acc_ref[...] = jnp.zeros((N_GROUPS, GC, GC), F32)
            gps_ref[...] = jnp.zeros((8, DP), F32)

        dy = _dot(dz_ref[...], w_ref[...], NT)

        def through_gate(dy_part, gt):
            sg = jax.nn.sigmoid(gt)
            return dy_part * (gt * sg), dy_part * (sg * (1.0 + gt * (1.0 - sg)))

        dat, dyg_a = through_gate(dy[:, 0:DA], ga_ref[...].astype(F32))
        dmix_p, dyg_p = through_gate(dy[:, DA:D], gp_ref[...].astype(F32))

        at = at_ref[...].astype(F32)
        do_ref[...] = dat.astype(BF16)
        dh_ref[:, DP:DP + DA] = (dyg_a * at).astype(BF16)
        prod = dat * at
        dd = jnp.zeros((tm, LANES), F32)
        for h in range(N_HEADS):
            hs = slice(h * DH, (h + 1) * DH)
            dd = _to_lane(dd, jnp.sum(prod[:, hs], axis=1, keepdims=True), h)
            nat_ref[h] = dat[:, hs]
        dd_ref[...] = dd
        ds_ref[...] = dd
        for d, do_out, dd_out in ((DILATIONS[1], do4_ref, dd4_ref), (DILATIONS[2], do16_ref, dd16_ref)):
            for r in range(d):
                dd_out[r] = _residues(ds_ref, (), d, r, tm // d)
                for h in range(N_HEADS):
                    do_out[r, :, h * DH:(h + 1) * DH] = _residues(nat_ref, (h,), d, r, tm // d).astype(BF16)

        _fill_ext(ext_ref, u_ref, uh_ref, ib, tm)
        rows = ib * tm + lax.broadcasted_iota(jnp.int32, (tm, 1), 0)
        for g in range(N_GROUPS):
            w = POOL_WINDOWS[g]
            cs = slice(g * GC, (g + 1) * GC)
            pooled, cnt = _pooled(ext_ref, g, rows, tm)
            pre = _dot(pooled.astype(BF16), wp_ref[g])
            sc = sc_ref[:, cs]
            dpo = dmix_p[:, cs]
            gps_ref[0:1, cs] += jnp.sum(dpo * pre, axis=0, keepdims=True)
            dh_ref[:, DP + DA + g * GC:DP + DA + (g + 1) * GC] = (dyg_p[:, cs] * (pre * sc)).astype(BF16)
            dpre = (dpo * sc).astype(BF16)
            acc_ref[g] += _dot(pooled.T.astype(BF16), dpre)
            dpooled = _dot(dpre, wp_ref[g], NT)
            eext_ref[0:tm, cs] = dpooled / cnt
            du = eext_ref[0:tm, cs]
            for j in range(1, w):
                du = du + eext_ref[j:j + tm, cs]
            dh_ref[:, cs] = (du - dpooled).astype(BF16)
        eext_ref[tm:tm + HALO, :] = eext_ref[0:HALO, :]

        @pl.when(i == n - 1)
        def _():
            for j in range(N_DEV):
                for g in range(N_GROUPS):
                    blk = acc_ref[g, j * PB:(j + 1) * PB, :]
                    dwp_ref[j, g * PB:(g + 1) * PB, :] = blk
                    dwp16_ref[j, g * PB:(g + 1) * PB, :] = blk.astype(BF16)

    rev = lambda i: (n - 1 - i, 0)
    const = lambda i: (0, 0)
    res = lambda width: [pl.BlockSpec((d, tm // d, width), lambda i: (0, n - 1 - i, 0)) for d in DILATIONS[1:]]
    pool_blocks = pl.BlockSpec((N_DEV, N_GROUPS * PB, GC), lambda i: (0, 0, 0))
    outs, carried = _call_carrying(
        body, rider, (n,),
        [pl.BlockSpec((tm, D), rev),
         pl.BlockSpec((D, D), const, pipeline_mode=pl.Buffered(1)),
         pl.BlockSpec((tm, DA), lambda i: (n - 1 - i, 1)), pl.BlockSpec((tm, DP), lambda i: (n - 1 - i, 2)),
         pl.BlockSpec((tm, DA), rev), pl.BlockSpec((tm, DP), rev),
         pl.BlockSpec((HALO, DP), lambda i: (jnp.maximum((n - 1 - i) * (tm // HALO) - 1, 0), 0)),
         pl.BlockSpec((N_GROUPS, GC, GC), lambda i: (0, 0, 0)),
         pl.BlockSpec((1, DP), const)],
        [pl.BlockSpec((tm, DP + D), lambda i: (n - 1 - i, 1)), pl.BlockSpec((tm, DA), rev)] + res(DA)
        + [pl.BlockSpec((tm, LANES), rev)] + res(LANES) + [pool_blocks, pool_blocks, pl.BlockSpec((8, DP), const)],
        [SDS((S, D_IN), BF16), SDS((S, DA), BF16)] + [SDS((d, S // d, DA), BF16) for d in DILATIONS[1:]]
        + [SDS((S, LANES), F32)] + [SDS((d, S // d, LANES), F32) for d in DILATIONS[1:]]
        + [SDS((N_DEV, N_GROUPS * PB, GC), F32), SDS((N_DEV, N_GROUPS * PB, GC), BF16), SDS((8, DP), F32)],
        [pltpu.VMEM((HALO + tm, DP), F32), pltpu.VMEM((tm + HALO, DP), F32),
         pltpu.VMEM((N_GROUPS, GC, GC), F32), pltpu.VMEM((N_HEADS, tm, DH), F32), pltpu.VMEM((tm, LANES), F32)],
        [dzb, wout, hug, hug, attn, hug, hug, wp, scale], "bwd_mid")
    return outs, carried


def _attn_bwd(h16, do, lse, dd, rope, name, others=None, dh=None, rider=None):
    d, n_sub, _ = h16.shape
    n_i = n_sub // BQ
    n_kb = n_sub // KB
    qw = BQ + KB
    final = others is not None
    out_dtype = BF16
    n_cb = 3 * DA // DH

    def nxt(i):
        return jnp.minimum(2 * i + 2, n_kb - 1)

    def body(qc_ref, qn_ref, kc_ref, vc_ref, doc_ref, don_ref, lc_ref, ln_ref, dc_ref, dn_ref,
             c_ref, sn_ref, *rest):
        if final:
            acc4_ref, acc16_ref, _, out_ref, carry_ref, qw_ref, dow_ref, lw_ref, dw_ref, nat_ref = rest
        else:
            out_ref, carry_ref, qw_ref, dow_ref, lw_ref, dw_ref = rest
        i = pl.program_id(1)

        @pl.when(i == 0)
        def _():
            carry_ref[...] = jnp.zeros((KB, DA), F32)

        for win, own, after in ((qw_ref, qc_ref, qn_ref), (dow_ref, doc_ref, don_ref), (lw_ref, lc_ref, ln_ref),
                                (dw_ref, dc_ref, dn_ref)):
            win[0:BQ, :] = own[...]
            win[BQ:qw, :] = after[...]
        if final:
            for n, (dil, acc) in enumerate(((DILATIONS[1], acc4_ref), (DILATIONS[2], acc16_ref))):
                for r in range(dil):
                    for cb in range(n_cb):
                        nat_ref[n, cb, pl.ds(r, BQ // dil, stride=dil), :] = acc[r, :, cb * DH:(cb + 1) * DH].astype(F32)

        a = lax.broadcasted_iota(jnp.int32, (qw, BQ), 0)
        b = lax.broadcasted_iota(jnp.int32, (qw, BQ), 1)
        n_q = jnp.where(i == n_i - 1, BQ, qw)
        mask = (b <= a) & (a <= b + KB) & (a < n_q)
        tok = pl.ds(pl.program_id(0) + d * BQ * i, BQ, stride=d)
        tabs = (c_ref[tok, :], sn_ref[tok, :])
        for h in range(N_HEADS):
            hs = slice(h * DH, (h + 1) * DH)
            hc = slice(h, h + 1)
            q, k, vv, dob = qw_ref[:, hs], kc_ref[:, hs], vc_ref[:, hs], dow_ref[:, hs]
            s = _dot(q, k, NT) * SCALE
            p = jnp.exp(jnp.where(mask, s - lw_ref[:, hc], NEG))
            dp = _dot(dob, vv, NT)
            ds = (p * (dp - dw_ref[:, hc]) * SCALE).astype(BF16)
            dq = _dot(ds, k)
            dk = _dot(ds, q, T_N)
            dv = _dot(p.astype(BF16), dob, T_N)
            dq_lo = dq[0:KB] + carry_ref[:, hs]
            carry_ref[:, hs] = dq[BQ:qw]
            dq_own = _rope_t(jnp.concatenate([dq_lo, dq[KB:BQ]], axis=0), *tabs)
            for base, gv in ((0, dq_own), (DA, _rope_t(dk, *tabs)), (2 * DA, dv)):
                if final:
                    cb = base // DH + h
                    gv = gv + nat_ref[0, cb] + nat_ref[1, cb]
                out_ref[:, base + h * DH:base + (h + 1) * DH] = gv.astype(out_dtype)

    def cur(rows, width, col=0):
        return pl.BlockSpec((None, rows, width), lambda r, i: (r, i, col))

    def nx(width, col=0):
        return pl.BlockSpec((None, KB, width), lambda r, i: (r, nxt(i), col))

    in_specs = [cur(BQ, DA), nx(DA), cur(BQ, DA, 1), cur(BQ, DA, 2), cur(BQ, DA), nx(DA),
                cur(BQ, LANES), nx(LANES), cur(BQ, LANES), nx(LANES)]
    in_specs += [pl.BlockSpec((S, DH), lambda r, i: (0, 0), pipeline_mode=pl.Buffered(1))] * 2
    args = [h16, h16, h16, h16, do, do, lse, lse, dd, dd] + list(rope)
    scratch = [pltpu.VMEM((KB, DA), F32), pltpu.VMEM((qw, DA), BF16), pltpu.VMEM((qw, DA), BF16),
               pltpu.VMEM((qw, LANES), F32), pltpu.VMEM((qw, LANES), F32)]
    if final:
        assert d == 1
        in_specs += [pl.BlockSpec((dil, BQ // dil, 3 * DA), lambda r, i: (0, i, 0)) for dil in DILATIONS[1:]]
        in_specs.append(pl.BlockSpec(memory_space=pl.ANY))
        args += list(others) + [dh[None]]
        scratch.append(pltpu.VMEM((2, n_cb, BQ, DH), F32))
    outs, carried = _call_carrying(
        body, rider, (d, n_i), in_specs, [cur(BQ, 3 * DA)], [SDS((d, n_sub, D_IN if final else 3 * DA), out_dtype)],
        scratch, args, name, {len(args) - 1: 0} if final else {})
    return outs[0] if rider is None else (outs[0], carried)


def _grad_w_out(yt, dzb):
    tn = 256

    def body(yt_ref, dz_ref, o_ref, o16_ref):
        acc = _dot(yt_ref[...], dz_ref[...])
        o_ref[...] = acc
        o16_ref[...] = acc.astype(BF16)

    cols = pl.BlockSpec((D, tn), lambda c: (0, c))
    o, o16 = pl.pallas_call(
        body, name="grad_w_out", grid=(D // tn,),
        in_specs=[pl.BlockSpec((D, S), lambda c: (0, 0), pipeline_mode=pl.Buffered(1)),
                  pl.BlockSpec((S, tn), lambda c: (0, c))],
        out_specs=[cols, cols],
        out_shape=[SDS((D, D), F32), SDS((D, D), BF16)],
        compiler_params=_params(("arbitrary",)),
    )(yt, dzb)
    return o.reshape(N_DEV, OB, D), o16.reshape(N_DEV, OB, D)


def _grad_w_in(xt, dh, order):
    n_send = N_DEV // 2
    n_steps = D_IN // TN
    own0 = n_send * W_PHASES
    sums0 = own0 + W_PHASES
    step_of_send = [W_PHASES * k + W_PHASES + 1 for k in range(n_send)]

    def body(ord_ref, xt_ref, dh_ref, o_ref, o16_ref, r1_ref, s2_ref, st_ref, got_ref,
             piece_sems, send_sems, recv_sems, got_sem):
        c = pl.program_id(0)
        x, y, core, _ = _place()

        def piece(cc):
            blk, sub = ord_ref[cc // W_PHASES], cc % W_PHASES
            return pltpu.make_async_copy(st_ref.at[cc % 2], o16_ref.at[blk, :, pl.ds(sub * TN, TN)], piece_sems.at[cc % 2])

        def send(k):
            return pltpu.make_async_remote_copy(
                src_ref=o16_ref.at[ord_ref[k]], dst_ref=r1_ref.at[k], send_sem=send_sems.at[k], recv_sem=recv_sems.at[k],
                device_id=(x, y, 1 - core), device_id_type=MESH)

        def sibling_part(cc):
            k, sub = cc // W_PHASES - n_send, cc % W_PHASES
            return pltpu.make_async_copy(r1_ref.at[k, :, pl.ds(sub * TN, TN)], got_ref, got_sem)

        @pl.when((c >= 2) & (c < own0 + 2))
        def _():
            piece(c - 2).wait()

        for k in range(n_send):
            pl.when(c == step_of_send[k])(lambda k=k: send(k).start())
        for k in range(1, n_send):
            pl.when(c == own0 + W_PHASES * k)(lambda k=k: send(k).wait_recv())
        pl.when(c >= sums0)(lambda: sibling_part(c).start())

        acc = _dot(xt_ref[...], dh_ref[...])
        o_ref[...] = acc

        @pl.when(c < own0)
        def _():
            st_ref[c % 2] = acc.astype(BF16)
            piece(c).start()

        @pl.when(c >= sums0)
        def _():
            sibling_part(c).wait()
            s2_ref[...] = (acc + got_ref[...].astype(F32)).astype(BF16)

        @pl.when(c == n_steps - 1)
        def _():
            send(0).wait_recv()
            for k in range(n_send):
                send(k).wait_send()

    col = lambda c, ordr: W_PHASES * ordr[c // W_PHASES] + c % W_PHASES
    hbm = pl.BlockSpec(memory_space=pl.ANY)
    sums_blk = lambda c, ordr: (jnp.maximum(c // W_PHASES - (n_send + 1), 0), 0, jnp.where(c >= sums0, c % W_PHASES, 0))
    partial, _, got, sums = pl.pallas_call(
        body, name="grad_w_in",
        grid_spec=pltpu.PrefetchScalarGridSpec(
            num_scalar_prefetch=1, grid=(n_steps,),
            in_specs=[pl.BlockSpec((D, S), lambda c, ordr: (0, 0), pipeline_mode=pl.Buffered(1)),
                      pl.BlockSpec((S, TN), lambda c, ordr: (0, col(c, ordr)))],
            out_specs=[pl.BlockSpec((None, D, TN), lambda c, ordr: (ordr[c // W_PHASES], 0, c % W_PHASES)), hbm, hbm,
                       pl.BlockSpec((None, D, TN), sums_blk)],
            scratch_shapes=[pltpu.VMEM((2, D, TN), BF16), pltpu.VMEM((D, TN), BF16), pltpu.SemaphoreType.DMA((2,)),
                            pltpu.SemaphoreType.DMA((n_send,)), pltpu.SemaphoreType.DMA((n_send,)),
                            pltpu.SemaphoreType.DMA(())]),
        out_shape=[SDS((N_DEV, D, WB), F32), SDS((N_DEV, D, WB), BF16), SDS((n_send, D, WB), BF16),
                   SDS((n_send - 1, D, WB), BF16)],
        compiler_params=_params(("arbitrary",)),
    )(order, xt, dh)
    return partial, got, sums


def _grad_x(dz, dh, wgs, rider=None):
    tm = 1024

    def body(dz_ref, dh_ref, *rest):
        o_ref = rest[-1]

        @pl.when(pl.program_id(1) == 0)
        def _():
            o_ref[...] = ALPHA * dz_ref[...]

        acc = _dot(dh_ref[:, 0:TN], rest[0][...], NT)
        for s in range(1, W_PHASES):
            acc = acc + _dot(dh_ref[:, s * TN:(s + 1) * TN], rest[s][...], NT)
        o_ref[...] += acc

    outs, carried = _call_carrying(
        body, rider, (S // tm, N_DEV),
        [pl.BlockSpec((tm, D), lambda i, j: (i, 0)), pl.BlockSpec((tm, WB), lambda i, j: (i, j))]
        + [pl.BlockSpec((None, D, TN), lambda i, j: (j, 0, 0))] * W_PHASES,
        [pl.BlockSpec((tm, D), lambda i, j: (i, 0))], [SDS((S, D), F32)], [], [dz, dh] + list(wgs), "grad_x")
    return outs[0], carried


def _place():
    x, y, c = lax.axis_index("x"), lax.axis_index("y"), lax.axis_index("c")
    chips = [(x, y), (1 - x, y), (x, 1 - y), (1 - x, 1 - y)]
    return x, y, c, chips


def _blk(x, y, c):
    return 4 * x + 2 * y + c


def _adamw(w, g, m, v):
    m = ADAM_B1 * m + (1.0 - ADAM_B1) * g
    v = ADAM_B2 * v + (1.0 - ADAM_B2) * (g * g)
    m_hat = m / (1.0 - ADAM_B1 ** ADAM_STEP)
    v_hat = v / (1.0 - ADAM_B2 ** ADAM_STEP)
    delta = -ADAM_LR * (m_hat / (jnp.sqrt(v_hat) + ADAM_EPS) + ADAM_WD * w)
    return delta, m, v


class _Rider:
    def __init__(self, args, in_specs, out_shape, out_specs, scratch, start, finish):
        self.args, self.in_specs, self.out_shape, self.out_specs = args, in_specs, out_shape, out_specs
        self.scratch, self.start, self.finish = scratch, start, finish


def _carry(body, rider, n_in, n_out, first, last):
    if rider is None:
        return body
    r_in, r_out, r_scr = len(rider.args), len(rider.out_shape), len(rider.scratch)

    def carrying(*refs):
        o0 = n_in + r_in
        s0 = o0 + n_out + r_out
        s1 = len(refs) - r_scr
        theirs = (refs[n_in:o0], refs[o0 + n_out:s0], refs[s1:])
        pl.when(first())(lambda: rider.start(*theirs))
        body(*refs[:n_in], *refs[o0:o0 + n_out], *refs[s0:s1])
        pl.when(last())(lambda: rider.finish(*theirs))

    return carrying


def _call_carrying(body, rider, grid, in_specs, out_specs, out_shape, scratch, args, name, aliases=None):
    n_in, n_out = len(in_specs), len(out_specs)
    ids = lambda: [pl.program_id(a) for a in range(len(grid))]
    first = lambda: functools.reduce(jnp.logical_and, [i == 0 for i in ids()])
    last = lambda: functools.reduce(jnp.logical_and, [i == n - 1 for i, n in zip(ids(), grid)])
    if rider is not None:
        in_specs, args = in_specs + rider.in_specs, list(args) + rider.args
        out_specs, out_shape = out_specs + rider.out_specs, out_shape + rider.out_shape
        scratch = scratch + rider.scratch
    outs = pl.pallas_call(
        _carry(body, rider, n_in, n_out, first, last), name=name, grid=grid,
        in_specs=in_specs, out_specs=out_specs, out_shape=out_shape, scratch_shapes=scratch,
        input_output_aliases=aliases or {}, compiler_params=_params(("arbitrary",) * len(grid)),
    )(*args)
    return list(outs[:n_out]), list(outs[n_out:])


def _gather_copy(tensors, send_sems, recv_sems, t, k, block, to, src=None):
    dst = tensors[t][1](_blk(*block))
    return pltpu.make_async_remote_copy(
        src_ref=dst if src is None else src, dst_ref=dst,
        send_sem=send_sems.at[t, k], recv_sem=recv_sems.at[t, k], device_id=to, device_id_type=MESH)


def _gather_start(tensors, send_sems, recv_sems, local_sems):
    x, y, c, chips = _place()
    me, sib = (x, y, c), (x, y, 1 - c)
    for t, (src, dst) in enumerate(tensors):
        pltpu.make_async_copy(src, dst(_blk(*me)), local_sems.at[t]).start()
        _gather_copy(tensors, send_sems, recv_sems, t, 0, me, sib, src).start()
        for j in (1, 2, 3):
            _gather_copy(tensors, send_sems, recv_sems, t, j, me, (*chips[j], c), src).start()


def _gather_finish(tensors, send_sems, recv_sems, local_sems):
    x, y, c, chips = _place()
    me, sib = (x, y, c), (x, y, 1 - c)
    copy = functools.partial(_gather_copy, tensors, send_sems, recv_sems)
    for t in range(len(tensors)):
        for j in (1, 2, 3):
            copy(t, j, (*chips[j], c), me).wait_recv()
            copy(t, 3 + j, (*chips[j], c), sib).start()
    for t, (src, dst) in enumerate(tensors):
        copy(t, 0, sib, me).wait_recv()
        for j in (1, 2, 3):
            copy(t, 3 + j, (*chips[j], 1 - c), me).wait_recv()
        copy(t, 0, me, sib, src).wait_send()
        for j in (1, 2, 3):
            copy(t, j, me, (*chips[j], c), src).wait_send()
            copy(t, 3 + j, (*chips[j], c), sib).wait_send()
        pltpu.make_async_copy(src, dst(_blk(*me)), local_sems.at[t]).wait()


W_PHASES = WB // TN


def _gather_first_rider(w_in_s, w_out_s, w_pool_s):
    shapes = [w_in_s.shape, w_out_s.shape, w_pool_s.shape]

    def keeps(outs, scr):
        return [pltpu.make_async_copy(scr[n], outs[n], scr[6].at[n]) for n in range(3)]

    def tensors(outs, scr):
        return [(scr[0].at[:, pl.ds(0, TN)], lambda b: outs[3].at[b])]

    def start(ins, outs, scr):
        for n in range(3):
            scr[n][...] = ins[n][...].astype(BF16)
        for keep in keeps(outs, scr):
            keep.start()
        _gather_start(tensors(outs, scr), *scr[3:6])

    def finish(ins, outs, scr):
        _gather_finish(tensors(outs, scr), *scr[3:6])
        for keep in keeps(outs, scr):
            keep.wait()

    vmem = pl.BlockSpec(memory_space=pltpu.VMEM)
    hbm = pl.BlockSpec(memory_space=pl.ANY)
    return _Rider(
        args=[w_in_s, w_out_s, w_pool_s], in_specs=[vmem] * 3,
        out_shape=[SDS(sh, BF16) for sh in shapes] + [SDS((N_DEV, D, TN), BF16)], out_specs=[hbm] * 4,
        scratch=[pltpu.VMEM(sh, BF16) for sh in shapes]
        + [pltpu.SemaphoreType.DMA((1, 7)), pltpu.SemaphoreType.DMA((1, 7)), pltpu.SemaphoreType.DMA((1,)),
           pltpu.SemaphoreType.DMA((3,))],
        start=start, finish=finish)


def _gather_w_in_rider(shard16, s):
    def tensors(ins, outs):
        return [(ins[0].at[:, pl.ds(s * TN, TN)], lambda b: outs[0].at[b])]

    hbm = pl.BlockSpec(memory_space=pl.ANY)
    return _Rider(
        args=[shard16], in_specs=[hbm], out_shape=[SDS((N_DEV, D, TN), BF16)], out_specs=[hbm],
        scratch=[pltpu.SemaphoreType.DMA((1, 7)), pltpu.SemaphoreType.DMA((1, 7)), pltpu.SemaphoreType.DMA((1,))],
        start=lambda ins, outs, scr: _gather_start(tensors(ins, outs), *scr),
        finish=lambda ins, outs, scr: _gather_finish(tensors(ins, outs), *scr))


def _gather_small_rider(w_out16, w_pool16):
    def tensors(ins, outs):
        gout_ref, gpool_ref = outs

        def pool_rows(b):
            return gpool_ref.at[:, pl.ds(pl.multiple_of(b * PB, PB), PB), :]

        return [(ins[0], lambda b: gout_ref.at[b]), (ins[1], pool_rows)]

    hbm = pl.BlockSpec(memory_space=pl.ANY)
    return _Rider(
        args=[w_out16, w_pool16], in_specs=[hbm, hbm],
        out_shape=[SDS((N_DEV, OB, D), BF16), SDS((N_GROUPS, GC, GC), BF16)], out_specs=[hbm, hbm],
        scratch=[pltpu.SemaphoreType.DMA((2, 7)), pltpu.SemaphoreType.DMA((2, 7)), pltpu.SemaphoreType.DMA((2,))],
        start=lambda ins, outs, scr: _gather_start(tensors(ins, outs), *scr),
        finish=lambda ins, outs, scr: _gather_finish(tensors(ins, outs), *scr))


def _block_table():
    x, y, c, chips = _place()
    return jnp.stack([_blk(*chip, c) for chip in chips]).astype(jnp.int32)


def _sibling_first_order():
    x, y, c, chips = _place()
    return jnp.stack([_blk(*chip, side) for side in (1 - c, c) for chip in chips]).astype(jnp.int32)


def _exchange_rider(p16):
    _, r_tot, cols = p16.shape

    def copies(ins, outs, scr):
        x, y, c, chips = _place()
        return [pltpu.make_async_remote_copy(
            src_ref=ins[0].at[_blk(*chips[k], 1 - c)], dst_ref=outs[0].at[k],
            send_sem=scr[0].at[k], recv_sem=scr[1].at[k], device_id=(x, y, 1 - c), device_id_type=MESH)
            for k in range(4)]

    def start(ins, outs, scr):
        for cp in copies(ins, outs, scr):
            cp.start()

    def finish(ins, outs, scr):
        for cp in copies(ins, outs, scr):
            cp.wait()

    hbm = pl.BlockSpec(memory_space=pl.ANY)
    return _Rider([p16], [hbm], [SDS((4, r_tot, cols), BF16)], [hbm],
                  [pltpu.SemaphoreType.DMA((4,)), pltpu.SemaphoreType.DMA((4,))], start, finish)


def _chip_sums(name, table, p32, r1, rc):
    _, r_tot, cols = p32.shape

    def body(tbl_ref, p_ref, r_ref, o_ref):
        o_ref[...] = (p_ref[...] + r_ref[...].astype(F32)).astype(BF16)

    return pl.pallas_call(
        body, name=name,
        grid_spec=pltpu.PrefetchScalarGridSpec(
            num_scalar_prefetch=1, grid=(3, r_tot // rc),
            in_specs=[pl.BlockSpec((None, rc, cols), lambda k, ch, tbl: (tbl[k + 1], ch, 0)),
                      pl.BlockSpec((None, rc, cols), lambda k, ch, tbl: (k + 1, ch, 0))],
            out_specs=pl.BlockSpec((None, rc, cols), lambda k, ch, tbl: (k, ch, 0))),
        out_shape=SDS((3, r_tot, cols), BF16),
        compiler_params=_params(("arbitrary", "arbitrary")),
    )(table, p32, r1)


def _stage2_rider(sums, stats=None):
    n_t = len(sums)

    def copies(ins, outs, scr):
        x, y, c, chips = _place()
        out = []
        for t in range(n_t):
            for k in (1, 2, 3):
                out.append(pltpu.make_async_remote_copy(
                    src_ref=ins[t].at[k - 1], dst_ref=outs[t].at[k - 1],
                    send_sem=scr[0].at[t, k - 1], recv_sem=scr[1].at[t, k - 1],
                    device_id=(*chips[k], c), device_id_type=MESH))
        if stats is not None:
            for k in range(1, N_DEV):
                peer = (x ^ ((k >> 2) & 1), y ^ ((k >> 1) & 1), c ^ (k & 1))
                out.append(pltpu.make_async_remote_copy(
                    src_ref=scr[4], dst_ref=outs[n_t].at[_blk(x, y, c)],
                    send_sem=scr[2].at[k - 1], recv_sem=scr[3].at[k - 1], device_id=peer, device_id_type=MESH))
        return out

    def own_rows(outs, scr):
        x, y, c, _ = _place()
        return pltpu.make_async_copy(scr[4], outs[n_t].at[_blk(x, y, c)], scr[5])

    def start(ins, outs, scr):
        if stats is not None:
            scr[4][...] = ins[n_t][...]
            scr[4][3:4, 0:DP] = ins[n_t + 1][0:1, :]
            own_rows(outs, scr).start()
        for cp in copies(ins, outs, scr):
            cp.start()

    def finish(ins, outs, scr):
        for cp in copies(ins, outs, scr):
            cp.wait()
        if stats is not None:
            own_rows(outs, scr).wait()

    vmem = pl.BlockSpec(memory_space=pltpu.VMEM)
    hbm = pl.BlockSpec(memory_space=pl.ANY)
    scratch = [pltpu.SemaphoreType.DMA((n_t, 3)), pltpu.SemaphoreType.DMA((n_t, 3))]
    args, in_specs = list(sums), [hbm] * n_t
    out_shape, out_specs = [SDS(s.shape, BF16) for s in sums], [hbm] * n_t
    if stats is not None:
        scratch += [pltpu.SemaphoreType.DMA((N_DEV - 1,)), pltpu.SemaphoreType.DMA((N_DEV - 1,)),
                    pltpu.VMEM((8, D), F32), pltpu.SemaphoreType.DMA(())]
        args, in_specs = args + list(stats), in_specs + [vmem, vmem]
        out_shape, out_specs = out_shape + [SDS((N_DEV, 8, D), F32)], out_specs + [hbm]
    return _Rider(args, in_specs, out_shape, out_specs, scratch, start, finish)


def _adamw_shard(name, table, p32, r1, r2, w, m, v, rc):
    _, r_tot, cols = p32.shape

    def body(tbl_ref, p_ref, r1_ref, r2_ref, w_ref, m_ref, v_ref, g_ref, d_ref, nm_ref, nv_ref):
        g = p_ref[...] + r1_ref[...].astype(F32)
        for k in range(3):
            g = g + r2_ref[k].astype(F32)
        delta, nm, nv = _adamw(w_ref[...], g, m_ref[...], v_ref[...])
        g_ref[...] = g
        d_ref[...] = delta
        nm_ref[...] = nm
        nv_ref[...] = nv

    rows = pl.BlockSpec((rc, cols), lambda ch, tbl: (ch, 0))
    shard = SDS((r_tot, cols), F32)
    return pl.pallas_call(
        body, name=name,
        grid_spec=pltpu.PrefetchScalarGridSpec(
            num_scalar_prefetch=1, grid=(r_tot // rc,),
            in_specs=[pl.BlockSpec((None, rc, cols), lambda ch, tbl: (tbl[0], ch, 0)),
                      pl.BlockSpec((None, rc, cols), lambda ch, tbl: (0, ch, 0)),
                      pl.BlockSpec((3, rc, cols), lambda ch, tbl: (0, ch, 0)), rows, rows, rows],
            out_specs=[rows, rows, rows, rows]),
        out_shape=[shard, shard, shard, shard],
        compiler_params=_params(("arbitrary",)),
    )(table, p32, r1, r2, w, m, v)


def _replicated_adamw(gathered, gain, bias, scale, m_gain, m_bias, m_scale, v_gain, v_bias, v_scale):
    def body(all_ref, s_ref, g_ref, b_ref, ms_ref, mg_ref, mb_ref, vs_ref, vg_ref, vb_ref, loss_ref, *outs):
        tot = all_ref[0]
        for b in range(1, N_DEV):
            tot = tot + all_ref[b]
        loss_ref[...] = tot[2:3, 0:LANES]
        for n, (row, width, w_r, m_r, v_r) in enumerate(((3, DP, s_ref, ms_ref, vs_ref), (0, D, g_ref, mg_ref, vg_ref),
                                                         (1, D, b_ref, mb_ref, vb_ref))):
            g = tot[row:row + 1, 0:width]
            delta, nm, nv = _adamw(w_r[...], g, m_r[...], v_r[...])
            for out, val in zip(outs[4 * n:4 * n + 4], (g, delta, nm, nv)):
                out[...] = val

    vmem = pl.BlockSpec(memory_space=pltpu.VMEM)
    shapes = [SDS((1, LANES), F32)] + [SDS((1, width), F32) for width in (DP, D, D) for _ in range(4)]
    return pl.pallas_call(
        body, name="replicated_adamw",
        in_specs=[vmem] * 10, out_specs=[vmem] * len(shapes), out_shape=shapes,
        compiler_params=_params(),
    )(gathered, scale, gain, bias, m_scale, m_gain, m_bias, v_scale, v_gain, v_bias)


def kernel(x, w_in, w_pool, pool_scale, w_out, ln_gain, ln_bias, loss_target, m_w_in, m_w_pool, m_pool_scale, m_w_out, m_ln_gain, m_ln_bias, v_w_in, v_w_pool, v_pool_scale, v_w_out, v_ln_gain, v_ln_bias):
    pool_rows = (N_GROUPS * PB, GC)
    x2, target = x[0], loss_target[0]
    rope = _rope_tables()
    table = _block_table()

    (xb, xt), (shard16, w_out16, w_pool16, wg0) = _prep_x(x2, _gather_first_rider(w_in[0], w_out[0], w_pool[0]))
    h, (wg1, wg2, wg_out, wg_pool) = _proj_all(xb, rope, wg0, shard16, w_out16, w_pool16)
    wg_out = wg_out.reshape(D, D)
    h16, hug = [h[0][None], h[1], h[2]], h[3]
    fwd = [_attn_fwd(h16[n], f"attn_fwd_d{d}") for n, d in enumerate(DILATIONS)]
    y, yt, attn, lse1, lse4, lse16 = _mix(fwd[0][0][0], fwd[0][1][0], *fwd[1], *fwd[2], hug, wg_pool, pool_scale)
    dz, dzb, stats = _outproj_ln(y, wg_out, x2, target, ln_gain, ln_bias)

    dwout, dwout16 = _grad_w_out(yt, dzb)
    mid, (r1_out,) = _bwd_mid(dzb, wg_out, hug, attn, wg_pool, pool_scale, rider=_exchange_rider(dwout16))
    dh, do1, do4, do16, dd1, dd4, dd16, dwp, dwp16, gps = mid
    s2_out = _chip_sums("rs_out_sums", table, dwout, r1_out, OB)

    do, lse, dd = [do1[None], do4, do16], [lse1[None], lse4, lse16], [dd1[None], dd4, dd16]
    acc4, (r2_out,) = _attn_bwd(h16[1], do[1], lse[1], dd[1], rope, "attn_bwd_d4", rider=_stage2_rider([s2_out]))
    g_out, d_out, nm_out, nv_out = _adamw_shard(
        "adamw_w_out", table, dwout, r1_out, r2_out, w_out[0], m_w_out[0], v_w_out[0], OB // 2)
    acc16, (r1_pool,) = _attn_bwd(h16[2], do[2], lse[2], dd[2], rope, "attn_bwd_d16", rider=_exchange_rider(dwp16))
    s2_pool = _chip_sums("rs_pool_sums", table, dwp, r1_pool, N_GROUPS * PB)
    dh = _attn_bwd(h16[0], do[0], lse[0], dd[0], rope, "attn_bwd_d1", others=(acc4, acc16), dh=dh)[0]

    dwin, r1_in, s2_in = _grad_w_in(xt, dh, _sibling_first_order())
    grad_x, (r2_in, r2_pool, gathered) = _grad_x(
        dz, dh, (wg0, wg1, wg2), rider=_stage2_rider([s2_in, s2_pool], stats=(stats, gps)))
    g_in, d_in, nm_in, nv_in = _adamw_shard(
        "adamw_w_in", table, dwin, r1_in, r2_in, w_in[0], m_w_in[0], v_w_in[0], 256)
    g_pool, d_pool, nm_pool, nv_pool = _adamw_shard(
        "adamw_w_pool", table, dwp, r1_pool, r2_pool, w_pool[0].reshape(pool_rows), m_w_pool[0].reshape(pool_rows),
        v_w_pool[0].reshape(pool_rows), N_GROUPS * PB)
    loss, *small = _replicated_adamw(gathered, ln_gain, ln_bias, pool_scale, m_ln_gain, m_ln_bias, m_pool_scale,
                                     v_ln_gain, v_ln_bias, v_pool_scale)
    (g_ps, d_ps, nm_ps, nv_ps, g_gain, d_gain, nm_gain, nv_gain, g_bias, d_bias, nm_bias, nv_bias) = small

    shard4 = lambda t: t.reshape(1, N_GROUPS, PB, GC)
    lead = lambda t: t[None]
    return (loss[0, 0], lead(grad_x),
            lead(g_in), shard4(g_pool), g_ps, lead(g_out), g_gain, g_bias,
            lead(d_in), shard4(d_pool), d_ps, lead(d_out), d_gain, d_bias,
            lead(nm_in), shard4(nm_pool), nm_ps, lead(nm_out), nm_gain, nm_bias,
            lead(nv_in), shard4(nv_pool), nv_ps, lead(nv_out), nv_gain, nv_bias)
```

```python
import functools

import jax
import jax.numpy as jnp
from jax import lax
from jax.experimental import pallas as pl
from jax.experimental.pallas import tpu as pltpu

F32 = jnp.float32
BF16 = jnp.bfloat16
SDS = jax.ShapeDtypeStruct
MESH = pl.DeviceIdType.MESH

N_DEV = 8
S = 4096
D = 2048
N_HEADS = 8
DH = 128
DA = N_HEADS * DH
DP = 1024
N_GROUPS = 4
GC = DP // N_GROUPS
POOL_WINDOWS = (2, 4, 8, 16)
HALO = 16
D_IN = 3 * DA + DP + D
WB = D_IN // N_DEV
TN = 256
HW = 3 * DA
OB = D // N_DEV
PB = GC // N_DEV
ROPE_DIM = DH // 4
ROPE_HALF = ROPE_DIM // 2
ROPE_THETA = 500000.0
DILATIONS = (1, 4, 16)
KB = 128
LN_EPS = 1e-5
ALPHA = 2.0 ** 0.25
SCALE = DH ** -0.5
NEG = -1e30
ADAM_LR, ADAM_B1, ADAM_B2, ADAM_EPS, ADAM_WD, ADAM_STEP = 0.001, 0.9, 0.999, 1e-08, 0.01, 10

VMEM_LIMIT_V7X = 61 * 1024 * 1024

NT = (((1,), (1,)), ((), ()))
T_N = (((0,), (0,)), ((), ()))


def _params(sem=None):
    return pltpu.CompilerParams(dimension_semantics=sem, vmem_limit_bytes=VMEM_LIMIT_V7X)


def _dot(a, b, dims=None):
    if dims is None:
        return jnp.dot(a, b, preferred_element_type=F32)
    return lax.dot_general(a, b, dims, preferred_element_type=F32)


def _rope_tables():
    inv_freq = ROPE_THETA ** (-(2.0 * jnp.arange(ROPE_HALF, dtype=F32)) / ROPE_DIM)
    ang = jnp.arange(S, dtype=jnp.int32).astype(F32)[:, None] * inv_freq[None, :]
    cos, sin = lax.optimization_barrier((jnp.cos(ang), jnp.sin(ang)))
    rest = DH - ROPE_DIM
    c = jnp.concatenate([cos, cos, jnp.ones((S, rest), F32)], axis=1)
    sn = jnp.concatenate([-sin, sin, jnp.zeros((S, rest), F32)], axis=1)
    return c, sn


def _rope_partner(t):
    lane = lax.broadcasted_iota(jnp.int32, t.shape, 1)
    return jnp.where(lane < ROPE_HALF, pltpu.roll(t, DH - ROPE_HALF, 1), pltpu.roll(t, ROPE_HALF, 1))


def _rope(t, c, sn):
    return t * c + _rope_partner(t) * sn


def _rope_t(g, c, sn):
    return g * c - _rope_partner(g) * sn


def _prep_x(x, rider):
    tm = 512

    def body(x_ref, xb_ref, xt_ref):
        xv = x_ref[...]
        xb_ref[...] = xv.astype(BF16)
        xt_ref[...] = xv.T.astype(BF16)

    return _call_carrying(
        body, rider, (S // tm,), [pl.BlockSpec((tm, D), lambda i: (i, 0))],
        [pl.BlockSpec((tm, D), lambda i: (i, 0)), pl.BlockSpec((D, tm), lambda i: (0, i))],
        [SDS((S, D), BF16), SDS((D, S), BF16)], [], [x], "prep_x")


def _residues(slab_ref, idx, d, r, n):
    return slab_ref[(*idx, pl.ds(r, n, stride=d), slice(None))]


def _proj_all(xb, rope, wg0, shard16, w_out16, w_pool16):
    n_heads = TN // DH
    n16 = HW // TN
    j16 = n16 // W_PHASES
    n_rope = 2 * DA // TN
    once = pl.Buffered(1)
    riders = [_gather_w_in_rider(shard16, s) for s in range(1, W_PHASES)] + [_gather_small_rider(w_out16, w_pool16)]
    last_j = N_DEV - 1

    def body(*refs):
        x_ref, c_ref, sn_ref, g0_ref = refs[:4]
        pos = 4
        r_ins, r_outs, r_scrs = [], [], []
        for r in riders:
            r_ins.append(refs[pos:pos + len(r.args)])
            pos += len(r.args)
        h16_ref, h4_ref, h16r_ref, hug_ref = refs[pos:pos + 4]
        pos += 4
        for r in riders:
            r_outs.append(refs[pos:pos + len(r.out_shape)])
            pos += len(r.out_shape)
        slab_ref, res_ref, w_ref, w_sems = refs[pos:pos + 4]
        pos += 4
        for r in riders:
            r_scrs.append(refs[pos:pos + len(r.scratch)])
            pos += len(r.scratch)
        gathered = [g0_ref] + [r_outs[s][0] for s in range(W_PHASES - 1)]
        s, j = pl.program_id(0), pl.program_id(1)
        d4, d16 = DILATIONS[1], DILATIONS[2]

        def fetch(ph, blk, slot):
            return pltpu.make_async_copy(gathered[ph].at[blk], w_ref.at[slot], w_sems.at[slot])

        pl.when((s == 0) & (j == 0))(lambda: fetch(0, 0, 0).start())
        for ph in range(W_PHASES):
            pl.when((s == ph) & (j == 0))(lambda ph=ph: riders[ph].start(r_ins[ph], r_outs[ph], r_scrs[ph]))
            pl.when(s == ph)(lambda ph=ph: fetch(ph, j, j % 2).wait())
            pl.when((s == ph) & (j == last_j))(lambda ph=ph: riders[ph].finish(r_ins[ph], r_outs[ph], r_scrs[ph]))
            pl.when((s == ph) & (j < last_j))(lambda ph=ph: fetch(ph, j + 1, (j + 1) % 2).start())
            if ph + 1 < W_PHASES:
                pl.when((s == ph) & (j == last_j))(lambda ph=ph: fetch(ph + 1, 0, 0).start())

        acc = _dot(x_ref[...], w_ref[j % 2])

        def heads(with_rope):
            for hh in range(n_heads):
                hs = slice(hh * DH, (hh + 1) * DH)
                t = _rope(acc[:, hs], c_ref[...], sn_ref[...]) if with_rope else acc[:, hs]
                h16_ref[:, hs] = t.astype(BF16)
                slab_ref[...] = t
                for q in range(d4):
                    t4 = _residues(slab_ref, (), d4, q, S // d4)
                    h4_ref[q, :, hs] = t4.astype(BF16)
                    res_ref[...] = t4
                    for m in range(d16 // d4):
                        h16r_ref[d4 * m + q, :, hs] = _residues(res_ref, (), d4, m, S // d16).astype(BF16)

        pl.when(W_PHASES * j + s < n_rope)(lambda: heads(True))
        pl.when((W_PHASES * j + s >= n_rope) & (j < j16))(lambda: heads(False))

        @pl.when(j >= j16)
        def _():
            hug_ref[...] = acc.astype(BF16)

    col16 = lambda s, j: W_PHASES * jnp.minimum(j, j16 - 1) + s
    colug = lambda s, j: W_PHASES * jnp.maximum(j, j16) + s - n16
    hbm = pl.BlockSpec(memory_space=pl.ANY)
    resident = lambda shape: pl.BlockSpec(shape, lambda s, j: (0, 0), pipeline_mode=once)
    in_specs = [resident((S, D)), resident((S, DH)), resident((S, DH)), hbm]
    args = [xb, *rope, wg0]
    out_specs = [pl.BlockSpec((S, TN), lambda s, j: (0, col16(s, j)))]
    out_specs += [pl.BlockSpec((d, S // d, TN), lambda s, j: (0, 0, col16(s, j))) for d in DILATIONS[1:]]
    out_specs += [pl.BlockSpec((S, TN), lambda s, j: (0, colug(s, j)))]
    out_shape = [SDS((S, HW), BF16)] + [SDS((d, S // d, HW), BF16) for d in DILATIONS[1:]] + [SDS((S, HW), BF16)]
    scratch = [pltpu.VMEM((S, DH), F32), pltpu.VMEM((S // DILATIONS[1], DH), F32), pltpu.VMEM((2, D, TN), BF16),
               pltpu.SemaphoreType.DMA((2,))]
    for r in riders:
        in_specs, args = in_specs + r.in_specs, args + r.args
        out_specs, out_shape = out_specs + r.out_specs, out_shape + r.out_shape
        scratch = scratch + r.scratch
    outs = pl.pallas_call(
        body, name="proj_all", grid=(W_PHASES, N_DEV),
        in_specs=in_specs, out_specs=out_specs, out_shape=out_shape, scratch_shapes=scratch,
        compiler_params=_params(("arbitrary", "arbitrary")),
    )(*args)
    return list(outs[:4]), list(outs[4:])


BQ = 2 * KB
LANES = 128


def _to_lane(acc, col, h):
    lane = lax.broadcasted_iota(jnp.int32, acc.shape, 1)
    return jnp.where(lane == h, col, acc)


def _attn_fwd(h16, name):
    d, n_sub, _ = h16.shape
    n_i = n_sub // BQ
    kw = KB + BQ

    def prev(i):
        return jnp.maximum(2 * i - 1, 0)

    def body(q_ref, kc_ref, kp_ref, vc_ref, vp_ref, o_ref, l_ref, kw_ref, vw_ref):
        i = pl.program_id(1)
        kw_ref[0:KB, :] = kp_ref[...]
        kw_ref[KB:kw, :] = kc_ref[...]
        vw_ref[0:KB, :] = vp_ref[...]
        vw_ref[KB:kw, :] = vc_ref[...]
        a = lax.broadcasted_iota(jnp.int32, (KB, 2 * KB), 0)
        b = lax.broadcasted_iota(jnp.int32, (KB, 2 * KB), 1)
        band = (b >= a) & (b <= a + KB)
        first_key = jnp.where(i == 0, KB, 0)
        masks = (band & (b >= first_key), band)
        for half in range(2):
            rs = slice(half * KB, (half + 1) * KB)
            ks = slice(half * KB, (half + 2) * KB)
            lse = jnp.zeros((KB, LANES), F32)
            for h in range(N_HEADS):
                hs = slice(h * DH, (h + 1) * DH)
                s = jnp.where(masks[half], _dot(q_ref[rs, hs], kw_ref[ks, hs], NT) * SCALE, NEG)
                m = jnp.max(s, axis=1, keepdims=True)
                p = jnp.exp(s - m)
                den = jnp.sum(p, axis=1, keepdims=True)
                o_ref[rs, hs] = (_dot(p.astype(BF16), vw_ref[ks, hs]) / den).astype(BF16)
                lse = _to_lane(lse, m + jnp.log(den), h)
            l_ref[rs, :] = lse

    return pl.pallas_call(
        body, name=name, grid=(d, n_i),
        in_specs=[pl.BlockSpec((None, BQ, DA), lambda r, i: (r, i, 0)),
                  pl.BlockSpec((None, BQ, DA), lambda r, i: (r, i, 1)),
                  pl.BlockSpec((None, KB, DA), lambda r, i: (r, prev(i), 1)),
                  pl.BlockSpec((None, BQ, DA), lambda r, i: (r, i, 2)),
                  pl.BlockSpec((None, KB, DA), lambda r, i: (r, prev(i), 2))],
        out_specs=[pl.BlockSpec((None, BQ, DA), lambda r, i: (r, i, 0)),
                   pl.BlockSpec((None, BQ, LANES), lambda r, i: (r, i, 0))],
        out_shape=[SDS((d, n_sub, DA), BF16), SDS((d, n_sub, LANES), F32)],
        scratch_shapes=[pltpu.VMEM((kw, DA), BF16), pltpu.VMEM((kw, DA), BF16)],
        compiler_params=_params(("arbitrary", "arbitrary")),
    )(h16, h16, h16, h16, h16)


def _pooled(ext_ref, g, rows, tm):
    w = POOL_WINDOWS[g]
    cs = slice(g * GC, (g + 1) * GC)
    cur = ext_ref[HALO:HALO + tm, cs]
    win = cur
    for j in range(1, w):
        win = win + ext_ref[HALO - j:HALO - j + tm, cs]
    cnt = jnp.minimum(rows + 1, w).astype(F32)
    return win / cnt - cur, cnt


def _fill_ext(ext_ref, u_ref, uh_ref, blk, tm):
    @pl.when(blk == 0)
    def _():
        ext_ref[0:HALO, :] = jnp.zeros((HALO, DP), F32)

    @pl.when(blk > 0)
    def _():
        ext_ref[0:HALO, :] = uh_ref[...].astype(F32)

    ext_ref[HALO:HALO + tm, :] = u_ref[...].astype(F32)


def _residue_specs(tm, width):
    return [pl.BlockSpec((d, tm // d, width), lambda i: (0, i, 0)) for d in DILATIONS[1:]]


def _mix(o1, l1, o4, l4, o16, l16, hug, wp, scale):
    tm = 256
    n_slab = N_HEADS + 1

    def body(o1r, l1r, o4r, l4r, o16r, l16r, u_ref, uh_ref, ga_ref, gp_ref, wp_ref, sc_ref,
             y_ref, yt_ref, attn_ref, lse_ref, lse4_ref, lse16_ref, ext_ref, ys_ref, nat_ref, ls_ref):
        i = pl.program_id(0)
        for n, (d, o_r, l_r) in enumerate(((DILATIONS[1], o4r, l4r), (DILATIONS[2], o16r, l16r))):
            for r in range(d):
                rows = pl.ds(r, tm // d, stride=d)
                for h in range(N_HEADS):
                    nat_ref[n, h, rows, :] = o_r[r, :, h * DH:(h + 1) * DH].astype(F32)
                nat_ref[n, N_HEADS, rows, :] = l_r[r]
        la, lb, lc = l1r[...], nat_ref[0, N_HEADS], nat_ref[1, N_HEADS]
        mx = jnp.maximum(jnp.maximum(la, lb), lc)
        ea, eb, ec = jnp.exp(la - mx), jnp.exp(lb - mx), jnp.exp(lc - mx)
        z = ea + eb + ec
        wa, wb, wc = ea / z, eb / z, ec / z
        lse = mx + jnp.log(z)
        lse_ref[...] = lse
        ls_ref[...] = lse
        for d, out in ((DILATIONS[1], lse4_ref), (DILATIONS[2], lse16_ref)):
            for r in range(d):
                out[r] = ls_ref[pl.ds(r, tm // d, stride=d), :]
        for h in range(N_HEADS):
            hs = slice(h * DH, (h + 1) * DH)
            hc = slice(h, h + 1)
            attn = wa[:, hc] * o1r[:, hs].astype(F32) + wb[:, hc] * nat_ref[0, h] + wc[:, hc] * nat_ref[1, h]
            attn_ref[:, hs] = attn.astype(BF16)
            gt = ga_ref[:, hs].astype(F32)
            ys_ref[:, hs] = attn * (gt * jax.nn.sigmoid(gt))

        _fill_ext(ext_ref, u_ref, uh_ref, i, tm)
        rows = i * tm + lax.broadcasted_iota(jnp.int32, (tm, 1), 0)
        for g in range(N_GROUPS):
            cs = slice(g * GC, (g + 1) * GC)
            gs = slice(DA + g * GC, DA + (g + 1) * GC)
            pooled, _ = _pooled(ext_ref, g, rows, tm)
            po = _dot(pooled.astype(BF16), wp_ref[g]) * sc_ref[:, cs]
            gt = gp_ref[:, cs].astype(F32)
            ys_ref[:, gs] = po * (gt * jax.nn.sigmoid(gt))
        yv = ys_ref[...]
        y_ref[...] = yv.astype(BF16)
        yt_ref[...] = yv.T.astype(BF16)

    row = lambda i: (i, 0)
    blk = pl.BlockSpec((tm, DA), row)
    lanes = pl.BlockSpec((tm, LANES), row)
    o_res, l_res = _residue_specs(tm, DA), _residue_specs(tm, LANES)
    return pl.pallas_call(
        body, name="mix", grid=(S // tm,),
        in_specs=[blk, lanes, o_res[0], l_res[0], o_res[1], l_res[1],
                  pl.BlockSpec((tm, DP), row),
                  pl.BlockSpec((HALO, DP), lambda i: (jnp.maximum(i * (tm // HALO) - 1, 0), 0)),
                  pl.BlockSpec((tm, DA), lambda i: (i, 1)), pl.BlockSpec((tm, DP), lambda i: (i, 2)),
                  pl.BlockSpec((N_GROUPS, GC, GC), lambda i: (0, 0, 0)),
                  pl.BlockSpec((1, DP), lambda i: (0, 0))],
        out_specs=[pl.BlockSpec((tm, D), row), pl.BlockSpec((D, tm), lambda i: (0, i)), blk, lanes] + l_res,
        out_shape=[SDS((S, D), BF16), SDS((D, S), BF16), SDS((S, DA), BF16), SDS((S, LANES), F32)]
        + [SDS((d, S // d, LANES), F32) for d in DILATIONS[1:]],
        scratch_shapes=[pltpu.VMEM((HALO + tm, DP), F32), pltpu.VMEM((tm, D), F32),
                        pltpu.VMEM((2, n_slab, tm, DH), F32), pltpu.VMEM((tm, LANES), F32)],
        compiler_params=_params(("arbitrary",)),
    )(o1, l1, o4, l4, o16, l16, hug, hug, hug, hug, wp, scale)


def _outproj_ln(y, wout, x, target, gain, bias):
    tm = 512
    te = 128

    def body(y_ref, w_ref, x_ref, t_ref, g_ref, b_ref, dz_ref, dzb_ref, st_ref, out_ref):
        i = pl.program_id(0)

        @pl.when(i == 0)
        def _():
            st_ref[...] = jnp.zeros((8, D), F32)

        out_ref[...] = _dot(y_ref[...], w_ref[...])
        gn = g_ref[...]
        for e in range(tm // te):
            rs = slice(e * te, (e + 1) * te)
            z = ALPHA * x_ref[rs, :] + out_ref[rs, :]
            mu = jnp.mean(z, axis=1, keepdims=True)
            zc = z - mu
            var = jnp.mean(zc * zc, axis=1, keepdims=True)
            rstd = lax.rsqrt(var + LN_EPS)
            xhat = zc * rstd
            diff = xhat * gn + b_ref[...] - t_ref[rs, :]
            dyln = diff / D
            st_ref[0:1, :] += jnp.sum(dyln * xhat, axis=0, keepdims=True)
            st_ref[1:2, :] += jnp.sum(dyln, axis=0, keepdims=True)
            row_loss = jnp.sum(diff * diff, axis=1, keepdims=True) / D
            st_ref[2:3, :] += jnp.broadcast_to(0.5 * jnp.sum(row_loss, axis=0, keepdims=True), (1, D))
            dxh = dyln * gn
            m1 = jnp.mean(dxh, axis=1, keepdims=True)
            m2 = jnp.mean(dxh * xhat, axis=1, keepdims=True)
            dz = rstd * (dxh - m1 - xhat * m2)
            dz_ref[rs, :] = dz
            dzb_ref[rs, :] = dz.astype(BF16)

    row = lambda i: (i, 0)
    const = lambda i: (0, 0)
    return pl.pallas_call(
        body, name="outproj_ln", grid=(S // tm,),
        in_specs=[pl.BlockSpec((tm, D), row),
                  pl.BlockSpec((D, D), const, pipeline_mode=pl.Buffered(1)),
                  pl.BlockSpec((tm, D), row), pl.BlockSpec((tm, D), row),
                  pl.BlockSpec((1, D), const), pl.BlockSpec((1, D), const)],
        out_specs=[pl.BlockSpec((tm, D), row), pl.BlockSpec((tm, D), row), pl.BlockSpec((8, D), const)],
        out_shape=[SDS((S, D), F32), SDS((S, D), BF16), SDS((8, D), F32)],
        scratch_shapes=[pltpu.VMEM((tm, D), F32)],
        compiler_params=_params(("arbitrary",)),
    )(y, wout, x, target, gain, bias)


def _bwd_mid(dzb, wout, hug, attn, wp, scale, rider=None):
    tm = 256
    n = S // tm

    def body(dz_ref, w_ref, ga_ref, gp_ref, at_ref, u_ref, uh_ref, wp_ref, sc_ref,
             dh_ref, do_ref, do4_ref, do16_ref, dd_ref, dd4_ref, dd16_ref, dwp_ref, dwp16_ref, gps_ref,
             ext_ref, eext_ref, acc_ref, nat_ref, ds_ref):
        i = pl.program_id(0)
        ib = n - 1 - i

        @pl.when(i == 0)
        def _():
            eext_ref[tm:tm + HALO, :] = jnp.zeros((HALO, DP), F32)
            acc_ref[...] = jnp.zeros((N_GROUPS, GC, GC), F32)
            gps_ref[...] = jnp.zeros((8, DP), F32)

        dy = _dot(dz_ref[...], w_ref[...], NT)

        def through_gate(dy_part, gt):
            sg = jax.nn.sigmoid(gt)
            return dy_part * (gt * sg), dy_part * (sg * (1.0 + gt * (1.0 - sg)))

        dat, dyg_a = through_gate(dy[:, 0:DA], ga_ref[...].astype(F32))
        dmix_p, dyg_p = through_gate(dy[:, DA:D], gp_ref[...].astype(F32))

        at = at_ref[...].astype(F32)
        do_ref[...] = dat.astype(BF16)
        dh_ref[:, DP:DP + DA] = (dyg_a * at).astype(BF16)
        prod = dat * at
        dd = jnp.zeros((tm, LANES), F32)
        for h in range(N_HEADS):
            hs = slice(h * DH, (h + 1) * DH)
            dd = _to_lane(dd, jnp.sum(prod[:, hs], axis=1, keepdims=True), h)
            nat_ref[h] = dat[:, hs]
        dd_ref[...] = dd
        ds_ref[...] = dd
        for d, do_out, dd_out in ((DILATIONS[1], do4_ref, dd4_ref), (DILATIONS[2], do16_ref, dd16_ref)):
            for r in range(d):
                dd_out[r] = _residues(ds_ref, (), d, r, tm // d)
                for h in range(N_HEADS):
                    do_out[r, :, h * DH:(h + 1) * DH] = _residues(nat_ref, (h,), d, r, tm // d).astype(BF16)

        _fill_ext(ext_ref, u_ref, uh_ref, ib, tm)
        rows = ib * tm + lax.broadcasted_iota(jnp.int32, (tm, 1), 0)
        for g in range(N_GROUPS):
            w = POOL_WINDOWS[g]
            cs = slice(g * GC, (g + 1) * GC)
            pooled, cnt = _pooled(ext_ref, g, rows, tm)
            pre = _dot(pooled.astype(BF16), wp_ref[g])
            sc = sc_ref[:, cs]
            dpo = dmix_p[:, cs]
            gps_ref[0:1, cs] += jnp.sum(dpo * pre, axis=0, keepdims=True)
            dh_ref[:, DP + DA + g * GC:DP + DA + (g + 1) * GC] = (dyg_p[:, cs] * (pre * sc)).astype(BF16)
            dpre = (dpo * sc).astype(BF16)
            acc_ref[g] += _dot(pooled.T.astype(BF16), dpre)
            dpooled = _dot(dpre, wp_ref[g], NT)
            eext_ref[0:tm, cs] = dpooled / cnt
            du = eext_ref[0:tm, cs]
            for j in range(1, w):
                du = du + eext_ref[j:j + tm, cs]
            dh_ref[:, cs] = (du - dpooled).astype(BF16)
        eext_ref[tm:tm + HALO, :] = eext_ref[0:HALO, :]

        @pl.when(i == n - 1)
        def _():
            for j in range(N_DEV):
                for g in range(N_GROUPS):
                    blk = acc_ref[g, j * PB:(j + 1) * PB, :]
                    dwp_ref[j, g * PB:(g + 1) * PB, :] = blk
                    dwp16_ref[j, g * PB:(g + 1) * PB, :] = blk.astype(BF16)

    rev = lambda i: (n - 1 - i, 0)
    const = lambda i: (0, 0)
    res = lambda width: [pl.BlockSpec((d, tm // d, width), lambda i: (0, n - 1 - i, 0)) for d in DILATIONS[1:]]
    pool_blocks = pl.BlockSpec((N_DEV, N_GROUPS * PB, GC), lambda i: (0, 0, 0))
    outs, carried = _call_carrying(
        body, rider, (n,),
        [pl.BlockSpec((tm, D), rev),
         pl.BlockSpec((D, D), const, pipeline_mode=pl.Buffered(1)),
         pl.BlockSpec((tm, DA), lambda i: (n - 1 - i, 1)), pl.BlockSpec((tm, DP), lambda i: (n - 1 - i, 2)),
         pl.BlockSpec((tm, DA), rev), pl.BlockSpec((tm, DP), rev),
         pl.BlockSpec((HALO, DP), lambda i: (jnp.maximum((n - 1 - i) * (tm // HALO) - 1, 0), 0)),
         pl.BlockSpec((N_GROUPS, GC, GC), lambda i: (0, 0, 0)),
         pl.BlockSpec((1, DP), const)],
        [pl.BlockSpec((tm, DP + D), lambda i: (n - 1 - i, 1)), pl.BlockSpec((tm, DA), rev)] + res(DA)
        + [pl.BlockSpec((tm, LANES), rev)] + res(LANES) + [pool_blocks, pool_blocks, pl.BlockSpec((8, DP), const)],
        [SDS((S, D_IN), BF16), SDS((S, DA), BF16)] + [SDS((d, S // d, DA), BF16) for d in DILATIONS[1:]]
        + [SDS((S, LANES), F32)] + [SDS((d, S // d, LANES), F32) for d in DILATIONS[1:]]
        + [SDS((N_DEV, N_GROUPS * PB, GC), F32), SDS((N_DEV, N_GROUPS * PB, GC), BF16), SDS((8, DP), F32)],
        [pltpu.VMEM((HALO + tm, DP), F32), pltpu.VMEM((tm + HALO, DP), F32),
         pltpu.VMEM((N_GROUPS, GC, GC), F32), pltpu.VMEM((N_HEADS, tm, DH), F32), pltpu.VMEM((tm, LANES), F32)],
        [dzb, wout, hug, hug, attn, hug, hug, wp, scale], "bwd_mid")
    return outs, carried


def _attn_bwd(h16, do, lse, dd, rope, name, others=None, dh=None, rider=None):
    d, n_sub, _ = h16.shape
    n_i = n_sub // BQ
    n_kb = n_sub // KB
    qw = BQ + KB
    final = others is not None
    out_dtype = BF16
    n_cb = 3 * DA // DH

    def nxt(i):
        return jnp.minimum(2 * i + 2, n_kb - 1)

    def body(qc_ref, qn_ref, kc_ref, vc_ref, doc_ref, don_ref, lc_ref, ln_ref, dc_ref, dn_ref,
             c_ref, sn_ref, *rest):
        if final:
            acc4_ref, acc16_ref, _, out_ref, carry_ref, qw_ref, dow_ref, lw_ref, dw_ref, nat_ref = rest
        else:
            out_ref, carry_ref, qw_ref, dow_ref, lw_ref, dw_ref = rest
        i = pl.program_id(1)

        @pl.when(i == 0)
        def _():
            carry_ref[...] = jnp.zeros((KB, DA), F32)

        for win, own, after in ((qw_ref, qc_ref, qn_ref), (dow_ref, doc_ref, don_ref), (lw_ref, lc_ref, ln_ref),
                                (dw_ref, dc_ref, dn_ref)):
            win[0:BQ, :] = own[...]
            win[BQ:qw, :] = after[...]
        if final:
            for n, (dil, acc) in enumerate(((DILATIONS[1], acc4_ref), (DILATIONS[2], acc16_ref))):
                for r in range(dil):
                    for cb in range(n_cb):
                        nat_ref[n, cb, pl.ds(r, BQ // dil, stride=dil), :] = acc[r, :, cb * DH:(cb + 1) * DH].astype(F32)

        a = lax.broadcasted_iota(jnp.int32, (qw, BQ), 0)
        b = lax.broadcasted_iota(jnp.int32, (qw, BQ), 1)
        n_q = jnp.where(i == n_i - 1, BQ, qw)
        mask = (b <= a) & (a <= b + KB) & (a < n_q)
        tok = pl.ds(pl.program_id(0) + d * BQ * i, BQ, stride=d)
        tabs = (c_ref[tok, :], sn_ref[tok, :])
        for h in range(N_HEADS):
            hs = slice(h * DH, (h + 1) * DH)
            hc = slice(h, h + 1)
            q, k, vv, dob = qw_ref[:, hs], kc_ref[:, hs], vc_ref[:, hs], dow_ref[:, hs]
            s = _dot(q, k, NT) * SCALE
            p = jnp.exp(jnp.where(mask, s - lw_ref[:, hc], NEG))
            dp = _dot(dob, vv, NT)
            ds = (p * (dp - dw_ref[:, hc]) * SCALE).astype(BF16)
            dq = _dot(ds, k)
            dk = _dot(ds, q, T_N)
            dv = _dot(p.astype(BF16), dob, T_N)
            dq_lo = dq[0:KB] + carry_ref[:, hs]
            carry_ref[:, hs] = dq[BQ:qw]
            dq_own = _rope_t(jnp.concatenate([dq_lo, dq[KB:BQ]], axis=0), *tabs)
            for base, gv in ((0, dq_own), (DA, _rope_t(dk, *tabs)), (2 * DA, dv)):
                if final:
                    cb = base // DH + h
                    gv = gv + nat_ref[0, cb] + nat_ref[1, cb]
                out_ref[:, base + h * DH:base + (h + 1) * DH] = gv.astype(out_dtype)

    def cur(rows, width, col=0):
        return pl.BlockSpec((None, rows, width), lambda r, i: (r, i, col))

    def nx(width, col=0):
        return pl.BlockSpec((None, KB, width), lambda r, i: (r, nxt(i), col))

    in_specs = [cur(BQ, DA), nx(DA), cur(BQ, DA, 1), cur(BQ, DA, 2), cur(BQ, DA), nx(DA),
                cur(BQ, LANES), nx(LANES), cur(BQ, LANES), nx(LANES)]
    in_specs += [pl.BlockSpec((S, DH), lambda r, i: (0, 0), pipeline_mode=pl.Buffered(1))] * 2
    args = [h16, h16, h16, h16, do, do, lse, lse, dd, dd] + list(rope)
    scratch = [pltpu.VMEM((KB, DA), F32), pltpu.VMEM((qw, DA), BF16), pltpu.VMEM((qw, DA), BF16),
               pltpu.VMEM((qw, LANES), F32), pltpu.VMEM((qw, LANES), F32)]
    if final:
        assert d == 1
        in_specs += [pl.BlockSpec((dil, BQ // dil, 3 * DA), lambda r, i: (0, i, 0)) for dil in DILATIONS[1:]]
        in_specs.append(pl.BlockSpec(memory_space=pl.ANY))
        args += list(others) + [dh[None]]
        scratch.append(pltpu.VMEM((2, n_cb, BQ, DH), F32))
    outs, carried = _call_carrying(
        body, rider, (d, n_i), in_specs, [cur(BQ, 3 * DA)], [SDS((d, n_sub, D_IN if final else 3 * DA), out_dtype)],
        scratch, args, name, {len(args) - 1: 0} if final else {})
    return outs[0] if rider is None else (outs[0], carried)


def _grad_w_out(yt, dzb):
    tn = 256

    def body(yt_ref, dz_ref, o_ref, o16_ref):
        acc = _dot(yt_ref[...], dz_ref[...])
        o_ref[...] = acc
        o16_ref[...] = acc.astype(BF16)

    cols = pl.BlockSpec((D, tn), lambda c: (0, c))
    o, o16 = pl.pallas_call(
        body, name="grad_w_out", grid=(D // tn,),
        in_specs=[pl.BlockSpec((D, S), lambda c: (0, 0), pipeline_mode=pl.Buffered(1)),
                  pl.BlockSpec((S, tn), lambda c: (0, c))],
        out_specs=[cols, cols],
        out_shape=[SDS((D, D), F32), SDS((D, D), BF16)],
        compiler_params=_params(("arbitrary",)),
    )(yt, dzb)
    return o.reshape(N_DEV, OB, D), o16.reshape(N_DEV, OB, D)


def _grad_w_in(xt, dh, order):
    n_send = N_DEV // 2
    n_steps = D_IN // TN
    own0 = n_send * W_PHASES
    sums0 = own0 + W_PHASES
    step_of_send = [W_PHASES * k + W_PHASES + 1 for k in range(n_send)]

    def body(ord_ref, xt_ref, dh_ref, o_ref, o16_ref, r1_ref, s2_ref, st_ref, got_ref,
             piece_sems, send_sems, recv_sems, got_sem):
        c = pl.program_id(0)
        x, y, core, _ = _place()

        def piece(cc):
            blk, sub = ord_ref[cc // W_PHASES], cc % W_PHASES
            return pltpu.make_async_copy(st_ref.at[cc % 2], o16_ref.at[blk, :, pl.ds(sub * TN, TN)], piece_sems.at[cc % 2])

        def send(k):
            return pltpu.make_async_remote_copy(
                src_ref=o16_ref.at[ord_ref[k]], dst_ref=r1_ref.at[k], send_sem=send_sems.at[k], recv_sem=recv_sems.at[k],
                device_id=(x, y, 1 - core), device_id_type=MESH)

        def sibling_part(cc):
            k, sub = cc // W_PHASES - n_send, cc % W_PHASES
            return pltpu.make_async_copy(r1_ref.at[k, :, pl.ds(sub * TN, TN)], got_ref, got_sem)

        @pl.when((c >= 2) & (c < own0 + 2))
        def _():
            piece(c - 2).wait()

        for k in range(n_send):
            pl.when(c == step_of_send[k])(lambda k=k: send(k).start())
        for k in range(1, n_send):
            pl.when(c == own0 + W_PHASES * k)(lambda k=k: send(k).wait_recv())
        pl.when(c >= sums0)(lambda: sibling_part(c).start())

        acc = _dot(xt_ref[...], dh_ref[...])
        o_ref[...] = acc

        @pl.when(c < own0)
        def _():
            st_ref[c % 2] = acc.astype(BF16)
            piece(c).start()

        @pl.when(c >= sums0)
        def _():
            sibling_part(c).wait()
            s2_ref[...] = (acc + got_ref[...].astype(F32)).astype(BF16)

        @pl.when(c == n_steps - 1)
        def _():
            send(0).wait_recv()
            for k in range(n_send):
                send(k).wait_send()

    col = lambda c, ordr: W_PHASES * ordr[c // W_PHASES] + c % W_PHASES
    hbm = pl.BlockSpec(memory_space=pl.ANY)
    sums_blk = lambda c, ordr: (jnp.maximum(c // W_PHASES - (n_send + 1), 0), 0, jnp.where(c >= sums0, c % W_PHASES, 0))
    partial, _, got, sums = pl.pallas_call(
        body, name="grad_w_in",
        grid_spec=pltpu.PrefetchScalarGridSpec(
            num_scalar_prefetch=1, grid=(n_steps,),
            in_specs=[pl.BlockSpec((D, S), lambda c, ordr: (0, 0), pipeline_mode=pl.Buffered(1)),
                      pl.BlockSpec((S, TN), lambda c, ordr: (0, col(c, ordr)))],
            out_specs=[pl.BlockSpec((None, D, TN), lambda c, ordr: (ordr[c // W_PHASES], 0, c % W_PHASES)), hbm, hbm,
                       pl.BlockSpec((None, D, TN), sums_blk)],
            scratch_shapes=[pltpu.VMEM((2, D, TN), BF16), pltpu.VMEM((D, TN), BF16), pltpu.SemaphoreType.DMA((2,)),
                            pltpu.SemaphoreType.DMA((n_send,)), pltpu.SemaphoreType.DMA((n_send,)),
                            pltpu.SemaphoreType.DMA(())]),
        out_shape=[SDS((N_DEV, D, WB), F32), SDS((N_DEV, D, WB), BF16), SDS((n_send, D, WB), BF16),
                   SDS((n_send - 1, D, WB), BF16)],
        compiler_params=_params(("arbitrary",)),
    )(order, xt, dh)
    return partial, got, sums


def _grad_x(dz, dh, wgs, rider=None):
    tm = 1024

    def body(dz_ref, dh_ref, *rest):
        o_ref = rest[-1]

        @pl.when(pl.program_id(1) == 0)
        def _():
            o_ref[...] = ALPHA * dz_ref[...]

        acc = _dot(dh_ref[:, 0:TN], rest[0][...], NT)
        for s in range(1, W_PHASES):
            acc = acc + _dot(dh_ref[:, s * TN:(s + 1) * TN], rest[s][...], NT)
        o_ref[...] += acc

    outs, carried = _call_carrying(
        body, rider, (S // tm, N_DEV),
        [pl.BlockSpec((tm, D), lambda i, j: (i, 0)), pl.BlockSpec((tm, WB), lambda i, j: (i, j))]
        + [pl.BlockSpec((None, D, TN), lambda i, j: (j, 0, 0))] * W_PHASES,
        [pl.BlockSpec((tm, D), lambda i, j: (i, 0))], [SDS((S, D), F32)], [], [dz, dh] + list(wgs), "grad_x")
    return outs[0], carried


def _place():
    x, y, c = lax.axis_index("x"), lax.axis_index("y"), lax.axis_index("c")
    chips = [(x, y), (1 - x, y), (x, 1 - y), (1 - x, 1 - y)]
    return x, y, c, chips


def _blk(x, y, c):
    return 4 * x + 2 * y + c


def _adamw(w, g, m, v):
    m = ADAM_B1 * m + (1.0 - ADAM_B1) * g
    v = ADAM_B2 * v + (1.0 - ADAM_B2) * (g * g)
    m_hat = m / (1.0 - ADAM_B1 ** ADAM_STEP)
    v_hat = v / (1.0 - ADAM_B2 ** ADAM_STEP)
    delta = -ADAM_LR * (m_hat / (jnp.sqrt(v_hat) + ADAM_EPS) + ADAM_WD * w)
    return delta, m, v


class _Rider:
    def __init__(self, args, in_specs, out_shape, out_specs, scratch, start, finish):
        self.args, self.in_specs, self.out_shape, self.out_specs = args, in_specs, out_shape, out_specs
        self.scratch, self.start, self.finish = scratch, start, finish


def _carry(body, rider, n_in, n_out, first, last):
    if rider is None:
        return body
    r_in, r_out, r_scr = len(rider.args), len(rider.out_shape), len(rider.scratch)

    def carrying(*refs):
        o0 = n_in + r_in
        s0 = o0 + n_out + r_out
        s1 = len(refs) - r_scr
        theirs = (refs[n_in:o0], refs[o0 + n_out:s0], refs[s1:])
        pl.when(first())(lambda: rider.start(*theirs))
        body(*refs[:n_in], *refs[o0:o0 + n_out], *refs[s0:s1])
        pl.when(last())(lambda: rider.finish(*theirs))

    return carrying


def _call_carrying(body, rider, grid, in_specs, out_specs, out_shape, scratch, args, name, aliases=None):
    n_in, n_out = len(in_specs), len(out_specs)
    ids = lambda: [pl.program_id(a) for a in range(len(grid))]
    first = lambda: functools.reduce(jnp.logical_and, [i == 0 for i in ids()])
    last = lambda: functools.reduce(jnp.logical_and, [i == n - 1 for i, n in zip(ids(), grid)])
    if rider is not None:
        in_specs, args = in_specs + rider.in_specs, list(args) + rider.args
        out_specs, out_shape = out_specs + rider.out_specs, out_shape + rider.out_shape
        scratch = scratch + rider.scratch
    outs = pl.pallas_call(
        _carry(body, rider, n_in, n_out, first, last), name=name, grid=grid,
        in_specs=in_specs, out_specs=out_specs, out_shape=out_shape, scratch_shapes=scratch,
        input_output_aliases=aliases or {}, compiler_params=_params(("arbitrary",) * len(grid)),
    )(*args)
    return list(outs[:n_out]), list(outs[n_out:])


def _gather_copy(tensors, send_sems, recv_sems, t, k, block, to, src=None):
    dst = tensors[t][1](_blk(*block))
    return pltpu.make_async_remote_copy(
        src_ref=dst if src is None else src, dst_ref=dst,
        send_sem=send_sems.at[t, k], recv_sem=recv_sems.at[t, k], device_id=to, device_id_type=MESH)


def _gather_start(tensors, send_sems, recv_sems, local_sems):
    x, y, c, chips = _place()
    me, sib = (x, y, c), (x, y, 1 - c)
    for t, (src, dst) in enumerate(tensors):
        pltpu.make_async_copy(src, dst(_blk(*me)), local_sems.at[t]).start()
        _gather_copy(tensors, send_sems, recv_sems, t, 0, me, sib, src).start()
        for j in (1, 2, 3):
            _gather_copy(tensors, send_sems, recv_sems, t, j, me, (*chips[j], c), src).start()


def _gather_finish(tensors, send_sems, recv_sems, local_sems):
    x, y, c, chips = _place()
    me, sib = (x, y, c), (x, y, 1 - c)
    copy = functools.partial(_gather_copy, tensors, send_sems, recv_sems)
    for t in range(len(tensors)):
        for j in (1, 2, 3):
            copy(t, j, (*chips[j], c), me).wait_recv()
            copy(t, 3 + j, (*chips[j], c), sib).start()
    for t, (src, dst) in enumerate(tensors):
        copy(t, 0, sib, me).wait_recv()
        for j in (1, 2, 3):
            copy(t, 3 + j, (*chips[j], 1 - c), me).wait_recv()
        copy(t, 0, me, sib, src).wait_send()
        for j in (1, 2, 3):
            copy(t, j, me, (*chips[j], c), src).wait_send()
            copy(t, 3 + j, (*chips[j], c), sib).wait_send()
        pltpu.make_async_copy(src, dst(_blk(*me)), local_sems.at[t]).wait()


W_PHASES = WB // TN


def _gather_first_rider(w_in_s, w_out_s, w_pool_s):
    shapes = [w_in_s.shape, w_out_s.shape, w_pool_s.shape]

    def keeps(outs, scr):
        return [pltpu.make_async_copy(scr[n], outs[n], scr[6].at[n]) for n in range(3)]

    def tensors(outs, scr):
        return [(scr[0].at[:, pl.ds(0, TN)], lambda b: outs[3].at[b])]

    def start(ins, outs, scr):
        for n in range(3):
            scr[n][...] = ins[n][...].astype(BF16)
        for keep in keeps(outs, scr):
            keep.start()
        _gather_start(tensors(outs, scr), *scr[3:6])

    def finish(ins, outs, scr):
        _gather_finish(tensors(outs, scr), *scr[3:6])
        for keep in keeps(outs, scr):
            keep.wait()

    vmem = pl.BlockSpec(memory_space=pltpu.VMEM)
    hbm = pl.BlockSpec(memory_space=pl.ANY)
    return _Rider(
        args=[w_in_s, w_out_s, w_pool_s], in_specs=[vmem] * 3,
        out_shape=[SDS(sh, BF16) for sh in shapes] + [SDS((N_DEV, D, TN), BF16)], out_specs=[hbm] * 4,
        scratch=[pltpu.VMEM(sh, BF16) for sh in shapes]
        + [pltpu.SemaphoreType.DMA((1, 7)), pltpu.SemaphoreType.DMA((1, 7)), pltpu.SemaphoreType.DMA((1,)),
           pltpu.SemaphoreType.DMA((3,))],
        start=start, finish=finish)


def _gather_w_in_rider(shard16, s):
    def tensors(ins, outs):
        return [(ins[0].at[:, pl.ds(s * TN, TN)], lambda b: outs[0].at[b])]

    hbm = pl.BlockSpec(memory_space=pl.ANY)
    return _Rider(
        args=[shard16], in_specs=[hbm], out_shape=[SDS((N_DEV, D, TN), BF16)], out_specs=[hbm],
        scratch=[pltpu.SemaphoreType.DMA((1, 7)), pltpu.SemaphoreType.DMA((1, 7)), pltpu.SemaphoreType.DMA((1,))],
        start=lambda ins, outs, scr: _gather_start(tensors(ins, outs), *scr),
        finish=lambda ins, outs, scr: _gather_finish(tensors(ins, outs), *scr))


def _gather_small_rider(w_out16, w_pool16):
    def tensors(ins, outs):
        gout_ref, gpool_ref = outs

        def pool_rows(b):
            return gpool_ref.at[:, pl.ds(pl.multiple_of(b * PB, PB), PB), :]

        return [(ins[0], lambda b: gout_ref.at[b]), (ins[1], pool_rows)]

    hbm = pl.BlockSpec(memory_space=pl.ANY)
    return _Rider(
        args=[w_out16, w_pool16], in_specs=[hbm, hbm],
        out_shape=[SDS((N_DEV, OB, D), BF16), SDS((N_GROUPS, GC, GC), BF16)], out_specs=[hbm, hbm],
        scratch=[pltpu.SemaphoreType.DMA((2, 7)), pltpu.SemaphoreType.DMA((2, 7)), pltpu.SemaphoreType.DMA((2,))],
        start=lambda ins, outs, scr: _gather_start(tensors(ins, outs), *scr),
        finish=lambda ins, outs, scr: _gather_finish(tensors(ins, outs), *scr))


def _block_table():
    x, y, c, chips = _place()
    return jnp.stack([_blk(*chip, c) for chip in chips]).astype(jnp.int32)


def _sibling_first_order():
    x, y, c, chips = _place()
    return jnp.stack([_blk(*chip, side) for side in (1 - c, c) for chip in chips]).astype(jnp.int32)


def _exchange_rider(p16):
    _, r_tot, cols = p16.shape

    def copies(ins, outs, scr):
        x, y, c, chips = _place()
        return [pltpu.make_async_remote_copy(
            src_ref=ins[0].at[_blk(*chips[k], 1 - c)], dst_ref=outs[0].at[k],
            send_sem=scr[0].at[k], recv_sem=scr[1].at[k], device_id=(x, y, 1 - c), device_id_type=MESH)
            for k in range(4)]

    def start(ins, outs, scr):
        for cp in copies(ins, outs, scr):
            cp.start()

    def finish(ins, outs, scr):
        for cp in copies(ins, outs, scr):
            cp.wait()

    hbm = pl.BlockSpec(memory_space=pl.ANY)
    return _Rider([p16], [hbm], [SDS((4, r_tot, cols), BF16)], [hbm],
                  [pltpu.SemaphoreType.DMA((4,)), pltpu.SemaphoreType.DMA((4,))], start, finish)


def _chip_sums(name, table, p32, r1, rc):
    _, r_tot, cols = p32.shape

    def body(tbl_ref, p_ref, r_ref, o_ref):
        o_ref[...] = (p_ref[...] + r_ref[...].astype(F32)).astype(BF16)

    return pl.pallas_call(
        body, name=name,
        grid_spec=pltpu.PrefetchScalarGridSpec(
            num_scalar_prefetch=1, grid=(3, r_tot // rc),
            in_specs=[pl.BlockSpec((None, rc, cols), lambda k, ch, tbl: (tbl[k + 1], ch, 0)),
                      pl.BlockSpec((None, rc, cols), lambda k, ch, tbl: (k + 1, ch, 0))],
            out_specs=pl.BlockSpec((None, rc, cols), lambda k, ch, tbl: (k, ch, 0))),
        out_shape=SDS((3, r_tot, cols), BF16),
        compiler_params=_params(("arbitrary", "arbitrary")),
    )(table, p32, r1)


def _stage2_rider(sums, stats=None):
    n_t = len(sums)

    def copies(ins, outs, scr):
        x, y, c, chips = _place()
        out = []
        for t in range(n_t):
            for k in (1, 2, 3):
                out.append(pltpu.make_async_remote_copy(
                    src_ref=ins[t].at[k - 1], dst_ref=outs[t].at[k - 1],
                    send_sem=scr[0].at[t, k - 1], recv_sem=scr[1].at[t, k - 1],
                    device_id=(*chips[k], c), device_id_type=MESH))
        if stats is not None:
            for k in range(1, N_DEV):
                peer = (x ^ ((k >> 2) & 1), y ^ ((k >> 1) & 1), c ^ (k & 1))
                out.append(pltpu.make_async_remote_copy(
                    src_ref=scr[4], dst_ref=outs[n_t].at[_blk(x, y, c)],
                    send_sem=scr[2].at[k - 1], recv_sem=scr[3].at[k - 1], device_id=peer, device_id_type=MESH))
        return out

    def own_rows(outs, scr):
        x, y, c, _ = _place()
        return pltpu.make_async_copy(scr[4], outs[n_t].at[_blk(x, y, c)], scr[5])

    def start(ins, outs, scr):
        if stats is not None:
            scr[4][...] = ins[n_t][...]
            scr[4][3:4, 0:DP] = ins[n_t + 1][0:1, :]
            own_rows(outs, scr).start()
        for cp in copies(ins, outs, scr):
            cp.start()

    def finish(ins, outs, scr):
        for cp in copies(ins, outs, scr):
            cp.wait()
        if stats is not None:
            own_rows(outs, scr).wait()

    vmem = pl.BlockSpec(memory_space=pltpu.VMEM)
    hbm = pl.BlockSpec(memory_space=pl.ANY)
    scratch = [pltpu.SemaphoreType.DMA((n_t, 3)), pltpu.SemaphoreType.DMA((n_t, 3))]
    args, in_specs = list(sums), [hbm] * n_t
    out_shape, out_specs = [SDS(s.shape, BF16) for s in sums], [hbm] * n_t
    if stats is not None:
        scratch += [pltpu.SemaphoreType.DMA((N_DEV - 1,)), pltpu.SemaphoreType.DMA((N_DEV - 1,)),
                    pltpu.VMEM((8, D), F32), pltpu.SemaphoreType.DMA(())]
        args, in_specs = args + list(stats), in_specs + [vmem, vmem]
        out_shape, out_specs = out_shape + [SDS((N_DEV, 8, D), F32)], out_specs + [hbm]
    return _Rider(args, in_specs, out_shape, out_specs, scratch, start, finish)


def _adamw_shard(name, table, p32, r1, r2, w, m, v, rc):
    _, r_tot, cols = p32.shape

    def body(tbl_ref, p_ref, r1_ref, r2_ref, w_ref, m_ref, v_ref, g_ref, d_ref, nm_ref, nv_ref):
        g = p_ref[...] + r1_ref[...].astype(F32)
        for k in range(3):
            g = g + r2_ref[k].astype(F32)
        delta, nm, nv = _adamw(w_ref[...], g, m_ref[...], v_ref[...])
        g_ref[...] = g
        d_ref[...] = delta
        nm_ref[...] = nm
        nv_ref[...] = nv

    rows = pl.BlockSpec((rc, cols), lambda ch, tbl: (ch, 0))
    shard = SDS((r_tot, cols), F32)
    return pl.pallas_call(
        body, name=name,
        grid_spec=pltpu.PrefetchScalarGridSpec(
            num_scalar_prefetch=1, grid=(r_tot // rc,),
            in_specs=[pl.BlockSpec((None, rc, cols), lambda ch, tbl: (tbl[0], ch, 0)),
                      pl.BlockSpec((None, rc, cols), lambda ch, tbl: (0, ch, 0)),
                      pl.BlockSpec((3, rc, cols), lambda ch, tbl: (0, ch, 0)), rows, rows, rows],
            out_specs=[rows, rows, rows, rows]),
        out_shape=[shard, shard, shard, shard],
        compiler_params=_params(("arbitrary",)),
    )(table, p32, r1, r2, w, m, v)


def _replicated_adamw(gathered, gain, bias, scale, m_gain, m_bias, m_scale, v_gain, v_bias, v_scale):
    def body(all_ref, s_ref, g_ref, b_ref, ms_ref, mg_ref, mb_ref, vs_ref, vg_ref, vb_ref, loss_ref, *outs):
        tot = all_ref[0]
        for b in range(1, N_DEV):
            tot = tot + all_ref[b]
        loss_ref[...] = tot[2:3, 0:LANES]
        for n, (row, width, w_r, m_r, v_r) in enumerate(((3, DP, s_ref, ms_ref, vs_ref), (0, D, g_ref, mg_ref, vg_ref),
                                                         (1, D, b_ref, mb_ref, vb_ref))):
            g = tot[row:row + 1, 0:width]
            delta, nm, nv = _adamw(w_r[...], g, m_r[...], v_r[...])
            for out, val in zip(outs[4 * n:4 * n + 4], (g, delta, nm, nv)):
                out[...] = val

    vmem = pl.BlockSpec(memory_space=pltpu.VMEM)
    shapes = [SDS((1, LANES), F32)] + [SDS((1, width), F32) for width in (DP, D, D) for _ in range(4)]
    return pl.pallas_call(
        body, name="replicated_adamw",
        in_specs=[vmem] * 10, out_specs=[vmem] * len(shapes), out_shape=shapes,
        compiler_params=_params(),
    )(gathered, scale, gain, bias, m_scale, m_gain, m_bias, v_scale, v_gain, v_bias)


def kernel(x, w_in, w_pool, pool_scale, w_out, ln_gain, ln_bias, loss_target, m_w_in, m_w_pool, m_pool_scale, m_w_out, m_ln_gain, m_ln_bias, v_w_in, v_w_pool, v_pool_scale, v_w_out, v_ln_gain, v_ln_bias):
    pool_rows = (N_GROUPS * PB, GC)
    x2, target = x[0], loss_target[0]
    rope = _rope_tables()
    table = _block_table()

    (xb, xt), (shard16, w_out16, w_pool16, wg0) = _prep_x(x2, _gather_first_rider(w_in[0], w_out[0], w_pool[0]))
    h, (wg1, wg2, wg_out, wg_pool) = _proj_all(xb, rope, wg0, shard16, w_out16, w_pool16)
    wg_out = wg_out.reshape(D, D)
    h16, hug = [h[0][None], h[1], h[2]], h[3]
    fwd = [_attn_fwd(h16[n], f"attn_fwd_d{d}") for n, d in enumerate(DILATIONS)]
    y, yt, attn, lse1, lse4, lse16 = _mix(fwd[0][0][0], fwd[0][1][0], *fwd[1], *fwd[2], hug, wg_pool, pool_scale)
    dz, dzb, stats = _outproj_ln(y, wg_out, x2, target, ln_gain, ln_bias)

    dwout, dwout16 = _grad_w_out(yt, dzb)
    mid, (r1_out,) = _bwd_mid(dzb, wg_out, hug, attn, wg_pool, pool_scale, rider=_exchange_rider(dwout16))
    dh, do1, do4, do16, dd1, dd4, dd16, dwp, dwp16, gps = mid
    s2_out = _chip_sums("rs_out_sums", table, dwout, r1_out, OB)

    do, lse, dd = [do1[None], do4, do16], [lse1[None], lse4, lse16], [dd1[None], dd4, dd16]
    acc4, (r2_out,) = _attn_bwd(h16[1], do[1], lse[1], dd[1], rope, "attn_bwd_d4", rider=_stage2_rider([s2_out]))
    g_out, d_out, nm_out, nv_out = _adamw_shard(
        "adamw_w_out", table, dwout, r1_out, r2_out, w_out[0], m_w_out[0], v_w_out[0], OB // 2)
    acc16, (r1_pool,) = _attn_bwd(h16[2], do[2], lse[2], dd[2], rope, "attn_bwd_d16", rider=_exchange_rider(dwp16))
    s2_pool = _chip_sums("rs_pool_sums", table, dwp, r1_pool, N_GROUPS * PB)
    dh = _attn_bwd(h16[0], do[0], lse[0], dd[0], rope, "attn_bwd_d1", others=(acc4, acc16), dh=dh)[0]

    dwin, r1_in, s2_in = _grad_w_in(xt, dh, _sibling_first_order())
    grad_x, (r2_in, r2_pool, gathered) = _grad_x(
        dz, dh, (wg0, wg1, wg2), rider=_stage2_rider([s2_in, s2_pool], stats=(stats, gps)))
    g_in, d_in, nm_in, nv_in = _adamw_shard(
        "adamw_w_in", table, dwin, r1_in, r2_in, w_in[0], m_w_in[0], v_w_in[0], 512)
    g_pool, d_pool, nm_pool, nv_pool = _adamw_shard(
        "adamw_w_pool", table, dwp, r1_pool, r2_pool, w_pool[0].reshape(pool_rows), m_w_pool[0].reshape(pool_rows),
        v_w_pool[0].reshape(pool_rows), N_GROUPS * PB)
    loss, *small = _replicated_adamw(gathered, ln_gain, ln_bias, pool_scale, m_ln_gain, m_ln_bias, m_pool_scale,
                                     v_ln_gain, v_ln_bias, v_pool_scale)
    (g_ps, d_ps, nm_ps, nv_ps, g_gain, d_gain, nm_gain, nv_gain, g_bias, d_bias, nm_bias, nv_bias) = small

    shard4 = lambda t: t.reshape(1, N_GROUPS, PB, GC)
    lead = lambda t: t[None]
    return (loss[0, 0], lead(grad_x),
            lead(g_in), shard4(g_pool), g_ps, lead(g_out), g_gain, g_bias,
            lead(d_in), shard4(d_pool), d_ps, lead(d_out), d_gain, d_bias,
            lead(nm_in), shard4(nm_pool), nm_ps, lead(nm_out), nm_gain, nm_bias,
            lead(nv_in), shard4(nv_pool), nv_ps, lead(nv_out), nv_gain, nv_bias)
```

```python
import functools

import jax
import jax.numpy as jnp
from jax import lax
from jax.experimental import pallas as pl
from jax.experimental.pallas import tpu as pltpu

F32 = jnp.float32
BF16 = jnp.bfloat16
SDS = jax.ShapeDtypeStruct
MESH = pl.DeviceIdType.MESH

N_DEV = 8
S = 4096
D = 2048
N_HEADS = 8
DH = 128
DA = N_HEADS * DH
DP = 1024
N_GROUPS = 4
GC = DP // N_GROUPS
POOL_WINDOWS = (2, 4, 8, 16)
HALO = 16
D_IN = 3 * DA + DP + D
WB = D_IN // N_DEV
TN = 256
HW = 3 * DA
OB = D // N_DEV
PB = GC // N_DEV
ROPE_DIM = DH // 4
ROPE_HALF = ROPE_DIM // 2
ROPE_THETA = 500000.0
DILATIONS = (1, 4, 16)
KB = 128
LN_EPS = 1e-5
ALPHA = 2.0 ** 0.25
SCALE = DH ** -0.5
NEG = -1e30
ADAM_LR, ADAM_B1, ADAM_B2, ADAM_EPS, ADAM_WD, ADAM_STEP = 0.001, 0.9, 0.999, 1e-08, 0.01, 10

VMEM_LIMIT_V7X = 61 * 1024 * 1024

NT = (((1,), (1,)), ((), ()))
T_N = (((0,), (0,)), ((), ()))


def _params(sem=None):
    return pltpu.CompilerParams(dimension_semantics=sem, vmem_limit_bytes=VMEM_LIMIT_V7X)


def _dot(a, b, dims=None):
    if dims is None:
        return jnp.dot(a, b, preferred_element_type=F32)
    return lax.dot_general(a, b, dims, preferred_element_type=F32)


def _rope_rates():
    inv_freq = ROPE_THETA ** (-(2.0 * jnp.arange(ROPE_HALF, dtype=F32)) / ROPE_DIM)
    return jnp.concatenate([inv_freq, inv_freq, jnp.zeros((DH - ROPE_DIM,), F32)])[None, :]


def _rope_table_rows(rates, first_row, n):
    pos = (first_row + lax.broadcasted_iota(jnp.int32, (n, DH), 0)).astype(F32)
    lane = lax.broadcasted_iota(jnp.int32, (n, DH), 1)
    ang = pos * rates
    sin = jnp.sin(ang)
    c = jnp.where(lane < ROPE_DIM, jnp.cos(ang), 1.0)
    sn = jnp.where(lane < ROPE_HALF, -sin, jnp.where(lane < ROPE_DIM, sin, 0.0))
    return c, sn


def _rope_partner(t):
    lane = lax.broadcasted_iota(jnp.int32, t.shape, 1)
    return jnp.where(lane < ROPE_HALF, pltpu.roll(t, DH - ROPE_HALF, 1), pltpu.roll(t, ROPE_HALF, 1))


def _rope(t, c, sn):
    return t * c + _rope_partner(t) * sn


def _rope_t(g, c, sn):
    return g * c - _rope_partner(g) * sn


def _prep_x(x, rates, rider):
    tm = 512

    def body(x_ref, f_ref, xb_ref, xt_ref, c_ref, sn_ref):
        xv = x_ref[...]
        xb_ref[...] = xv.astype(BF16)
        xt_ref[...] = xv.T.astype(BF16)
        c_ref[...], sn_ref[...] = _rope_table_rows(f_ref[...], pl.program_id(0) * tm, tm)

    row = lambda i: (i, 0)
    return _call_carrying(
        body, rider, (S // tm,), [pl.BlockSpec((tm, D), row), pl.BlockSpec((1, DH), lambda i: (0, 0))],
        [pl.BlockSpec((tm, D), row), pl.BlockSpec((D, tm), lambda i: (0, i)),
         pl.BlockSpec((tm, DH), row), pl.BlockSpec((tm, DH), row)],
        [SDS((S, D), BF16), SDS((D, S), BF16), SDS((S, DH), F32), SDS((S, DH), F32)], [], [x, rates], "prep_x")


def _residues(slab_ref, idx, d, r, n):
    return slab_ref[(*idx, pl.ds(r, n, stride=d), slice(None))]


def _proj_all(xb, rope, wg0, shard16, w_out16, w_pool16):
    n_heads = TN // DH
    n16 = HW // TN
    j16 = n16 // W_PHASES
    n_rope = 2 * DA // TN
    once = pl.Buffered(1)
    riders = [_gather_w_in_rider(shard16, s) for s in range(1, W_PHASES)] + [_gather_small_rider(w_out16, w_pool16)]
    last_j = N_DEV - 1

    def body(*refs):
        x_ref, c_ref, sn_ref, g0_ref = refs[:4]
        pos = 4
        r_ins, r_outs, r_scrs = [], [], []
        for r in riders:
            r_ins.append(refs[pos:pos + len(r.args)])
            pos += len(r.args)
        h16_ref, h4_ref, h16r_ref, hug_ref = refs[pos:pos + 4]
        pos += 4
        for r in riders:
            r_outs.append(refs[pos:pos + len(r.out_shape)])
            pos += len(r.out_shape)
        slab_ref, res_ref, w_ref, w_sems = refs[pos:pos + 4]
        pos += 4
        for r in riders:
            r_scrs.append(refs[pos:pos + len(r.scratch)])
            pos += len(r.scratch)
        gathered = [g0_ref] + [r_outs[s][0] for s in range(W_PHASES - 1)]
        s, j = pl.program_id(0), pl.program_id(1)
        d4, d16 = DILATIONS[1], DILATIONS[2]

        def fetch(ph, blk, slot):
            return pltpu.make_async_copy(gathered[ph].at[blk], w_ref.at[slot], w_sems.at[slot])

        pl.when((s == 0) & (j == 0))(lambda: fetch(0, 0, 0).start())
        for ph in range(W_PHASES):
            pl.when((s == ph) & (j == 0))(lambda ph=ph: riders[ph].start(r_ins[ph], r_outs[ph], r_scrs[ph]))
            pl.when(s == ph)(lambda ph=ph: fetch(ph, j, j % 2).wait())
            pl.when((s == ph) & (j == last_j))(lambda ph=ph: riders[ph].finish(r_ins[ph], r_outs[ph], r_scrs[ph]))
            pl.when((s == ph) & (j < last_j))(lambda ph=ph: fetch(ph, j + 1, (j + 1) % 2).start())
            if ph + 1 < W_PHASES:
                pl.when((s == ph) & (j == last_j))(lambda ph=ph: fetch(ph + 1, 0, 0).start())

        acc = _dot(x_ref[...], w_ref[j % 2])

        def heads(with_rope):
            for hh in range(n_heads):
                hs = slice(hh * DH, (hh + 1) * DH)
                t = _rope(acc[:, hs], c_ref[...], sn_ref[...]) if with_rope else acc[:, hs]
                h16_ref[:, hs] = t.astype(BF16)
                slab_ref[...] = t
                for q in range(d4):
                    t4 = _residues(slab_ref, (), d4, q, S // d4)
                    h4_ref[q, :, hs] = t4.astype(BF16)
                    res_ref[...] = t4
                    for m in range(d16 // d4):
                        h16r_ref[d4 * m + q, :, hs] = _residues(res_ref, (), d4, m, S // d16).astype(BF16)

        pl.when(W_PHASES * j + s < n_rope)(lambda: heads(True))
        pl.when((W_PHASES * j + s >= n_rope) & (j < j16))(lambda: heads(False))

        @pl.when(j >= j16)
        def _():
            hug_ref[...] = acc.astype(BF16)

    col16 = lambda s, j: W_PHASES * jnp.minimum(j, j16 - 1) + s
    colug = lambda s, j: W_PHASES * jnp.maximum(j, j16) + s - n16
    hbm = pl.BlockSpec(memory_space=pl.ANY)
    resident = lambda shape: pl.BlockSpec(shape, lambda s, j: (0, 0), pipeline_mode=once)
    in_specs = [resident((S, D)), resident((S, DH)), resident((S, DH)), hbm]
    args = [xb, *rope, wg0]
    out_specs = [pl.BlockSpec((S, TN), lambda s, j: (0, col16(s, j)))]
    out_specs += [pl.BlockSpec((d, S // d, TN), lambda s, j: (0, 0, col16(s, j))) for d in DILATIONS[1:]]
    out_specs += [pl.BlockSpec((S, TN), lambda s, j: (0, colug(s, j)))]
    out_shape = [SDS((S, HW), BF16)] + [SDS((d, S // d, HW), BF16) for d in DILATIONS[1:]] + [SDS((S, HW), BF16)]
    scratch = [pltpu.VMEM((S, DH), F32), pltpu.VMEM((S // DILATIONS[1], DH), F32), pltpu.VMEM((2, D, TN), BF16),
               pltpu.SemaphoreType.DMA((2,))]
    for r in riders:
        in_specs, args = in_specs + r.in_specs, args + r.args
        out_specs, out_shape = out_specs + r.out_specs, out_shape + r.out_shape
        scratch = scratch + r.scratch
    outs = pl.pallas_call(
        body, name="proj_all", grid=(W_PHASES, N_DEV),
        in_specs=in_specs, out_specs=out_specs, out_shape=out_shape, scratch_shapes=scratch,
        compiler_params=_params(("arbitrary", "arbitrary")),
    )(*args)
    return list(outs[:4]), list(outs[4:])


BQ = 2 * KB
LANES = 128


def _to_lane(acc, col, h):
    lane = lax.broadcasted_iota(jnp.int32, acc.shape, 1)
    return jnp.where(lane == h, col, acc)


def _attn_fwd(h16, name):
    d, n_sub, _ = h16.shape
    n_i = n_sub // BQ
    kw = KB + BQ

    def prev(i):
        return jnp.maximum(2 * i - 1, 0)

    def body(q_ref, kc_ref, kp_ref, vc_ref, vp_ref, o_ref, l_ref, kw_ref, vw_ref):
        i = pl.program_id(1)
        kw_ref[0:KB, :] = kp_ref[...]
        kw_ref[KB:kw, :] = kc_ref[...]
        vw_ref[0:KB, :] = vp_ref[...]
        vw_ref[KB:kw, :] = vc_ref[...]
        a = lax.broadcasted_iota(jnp.int32, (KB, 2 * KB), 0)
        b = lax.broadcasted_iota(jnp.int32, (KB, 2 * KB), 1)
        band = (b >= a) & (b <= a + KB)
        first_key = jnp.where(i == 0, KB, 0)
        masks = (band & (b >= first_key), band)
        for half in range(2):
            rs = slice(half * KB, (half + 1) * KB)
            ks = slice(half * KB, (half + 2) * KB)
            lse = jnp.zeros((KB, LANES), F32)
            for h in range(N_HEADS):
                hs = slice(h * DH, (h + 1) * DH)
                s = jnp.where(masks[half], _dot(q_ref[rs, hs], kw_ref[ks, hs], NT) * SCALE, NEG)
                m = jnp.max(s, axis=1, keepdims=True)
                p = jnp.exp(s - m)
                den = jnp.sum(p, axis=1, keepdims=True)
                o_ref[rs, hs] = (_dot(p.astype(BF16), vw_ref[ks, hs]) / den).astype(BF16)
                lse = _to_lane(lse, m + jnp.log(den), h)
            l_ref[rs, :] = lse

    return pl.pallas_call(
        body, name=name, grid=(d, n_i),
        in_specs=[pl.BlockSpec((None, BQ, DA), lambda r, i: (r, i, 0)),
                  pl.BlockSpec((None, BQ, DA), lambda r, i: (r, i, 1)),
                  pl.BlockSpec((None, KB, DA), lambda r, i: (r, prev(i), 1)),
                  pl.BlockSpec((None, BQ, DA), lambda r, i: (r, i, 2)),
                  pl.BlockSpec((None, KB, DA), lambda r, i: (r, prev(i), 2))],
        out_specs=[pl.BlockSpec((None, BQ, DA), lambda r, i: (r, i, 0)),
                   pl.BlockSpec((None, BQ, LANES), lambda r, i: (r, i, 0))],
        out_shape=[SDS((d, n_sub, DA), BF16), SDS((d, n_sub, LANES), F32)],
        scratch_shapes=[pltpu.VMEM((kw, DA), BF16), pltpu.VMEM((kw, DA), BF16)],
        compiler_params=_params(("arbitrary", "arbitrary")),
    )(h16, h16, h16, h16, h16)


def _pooled(ext_ref, g, rows, tm):
    w = POOL_WINDOWS[g]
    cs = slice(g * GC, (g + 1) * GC)
    cur = ext_ref[HALO:HALO + tm, cs]
    win = cur
    for j in range(1, w):
        win = win + ext_ref[HALO - j:HALO - j + tm, cs]
    cnt = jnp.minimum(rows + 1, w).astype(F32)
    return win / cnt - cur, cnt


def _fill_ext(ext_ref, u_ref, uh_ref, blk, tm):
    @pl.when(blk == 0)
    def _():
        ext_ref[0:HALO, :] = jnp.zeros((HALO, DP), F32)

    @pl.when(blk > 0)
    def _():
        ext_ref[0:HALO, :] = uh_ref[...].astype(F32)

    ext_ref[HALO:HALO + tm, :] = u_ref[...].astype(F32)


def _residue_specs(tm, width):
    return [pl.BlockSpec((d, tm // d, width), lambda i: (0, i, 0)) for d in DILATIONS[1:]]


def _mix(o1, l1, o4, l4, o16, l16, hug, wp, scale):
    tm = 256
    n_slab = N_HEADS + 1

    def body(o1r, l1r, o4r, l4r, o16r, l16r, u_ref, uh_ref, ga_ref, gp_ref, wp_ref, sc_ref,
             y_ref, yt_ref, attn_ref, lse_ref, lse4_ref, lse16_ref, ext_ref, ys_ref, nat_ref, ls_ref):
        i = pl.program_id(0)
        for n, (d, o_r, l_r) in enumerate(((DILATIONS[1], o4r, l4r), (DILATIONS[2], o16r, l16r))):
            for r in range(d):
                rows = pl.ds(r, tm // d, stride=d)
                for h in range(N_HEADS):
                    nat_ref[n, h, rows, :] = o_r[r, :, h * DH:(h + 1) * DH].astype(F32)
                nat_ref[n, N_HEADS, rows, :] = l_r[r]
        la, lb, lc = l1r[...], nat_ref[0, N_HEADS], nat_ref[1, N_HEADS]
        mx = jnp.maximum(jnp.maximum(la, lb), lc)
        ea, eb, ec = jnp.exp(la - mx), jnp.exp(lb - mx), jnp.exp(lc - mx)
        z = ea + eb + ec
        wa, wb, wc = ea / z, eb / z, ec / z
        lse = mx + jnp.log(z)
        lse_ref[...] = lse
        ls_ref[...] = lse
        for d, out in ((DILATIONS[1], lse4_ref), (DILATIONS[2], lse16_ref)):
            for r in range(d):
                out[r] = ls_ref[pl.ds(r, tm // d, stride=d), :]
        for h in range(N_HEADS):
            hs = slice(h * DH, (h + 1) * DH)
            hc = slice(h, h + 1)
            attn = wa[:, hc] * o1r[:, hs].astype(F32) + wb[:, hc] * nat_ref[0, h] + wc[:, hc] * nat_ref[1, h]
            attn_ref[:, hs] = attn.astype(BF16)
            gt = ga_ref[:, hs].astype(F32)
            ys_ref[:, hs] = attn * (gt * jax.nn.sigmoid(gt))

        _fill_ext(ext_ref, u_ref, uh_ref, i, tm)
        rows = i * tm + lax.broadcasted_iota(jnp.int32, (tm, 1), 0)
        for g in range(N_GROUPS):
            cs = slice(g * GC, (g + 1) * GC)
            gs = slice(DA + g * GC, DA + (g + 1) * GC)
            pooled, _ = _pooled(ext_ref, g, rows, tm)
            po = _dot(pooled.astype(BF16), wp_ref[g]) * sc_ref[:, cs]
            gt = gp_ref[:, cs].astype(F32)
            ys_ref[:, gs] = po * (gt * jax.nn.sigmoid(gt))
        yv = ys_ref[...]
        y_ref[...] = yv.astype(BF16)
        yt_ref[...] = yv.T.astype(BF16)

    row = lambda i: (i, 0)
    blk = pl.BlockSpec((tm, DA), row)
    lanes = pl.BlockSpec((tm, LANES), row)
    o_res, l_res = _residue_specs(tm, DA), _residue_specs(tm, LANES)
    return pl.pallas_call(
        body, name="mix", grid=(S // tm,),
        in_specs=[blk, lanes, o_res[0], l_res[0], o_res[1], l_res[1],
                  pl.BlockSpec((tm, DP), row),
                  pl.BlockSpec((HALO, DP), lambda i: (jnp.maximum(i * (tm // HALO) - 1, 0), 0)),
                  pl.BlockSpec((tm, DA), lambda i: (i, 1)), pl.BlockSpec((tm, DP), lambda i: (i, 2)),
                  pl.BlockSpec((N_GROUPS, GC, GC), lambda i: (0, 0, 0)),
                  pl.BlockSpec((1, DP), lambda i: (0, 0))],
        out_specs=[pl.BlockSpec((tm, D), row), pl.BlockSpec((D, tm), lambda i: (0, i)), blk, lanes] + l_res,
        out_shape=[SDS((S, D), BF16), SDS((D, S), BF16), SDS((S, DA), BF16), SDS((S, LANES), F32)]
        + [SDS((d, S // d, LANES), F32) for d in DILATIONS[1:]],
        scratch_shapes=[pltpu.VMEM((HALO + tm, DP), F32), pltpu.VMEM((tm, D), F32),
                        pltpu.VMEM((2, n_slab, tm, DH), F32), pltpu.VMEM((tm, LANES), F32)],
        compiler_params=_params(("arbitrary",)),
    )(o1, l1, o4, l4, o16, l16, hug, hug, hug, hug, wp, scale)


def _outproj_ln(y, wout, x, target, gain, bias):
    tm = 512
    te = 128

    def body(y_ref, w_ref, x_ref, t_ref, g_ref, b_ref, dz_ref, dzb_ref, st_ref, out_ref):
        i = pl.program_id(0)

        @pl.when(i == 0)
        def _():
            st_ref[...] = jnp.zeros((8, D), F32)

        out_ref[...] = _dot(y_ref[...], w_ref[...])
        gn = g_ref[...]
        for e in range(tm // te):
            rs = slice(e * te, (e + 1) * te)
            z = ALPHA * x_ref[rs, :] + out_ref[rs, :]
            mu = jnp.mean(z, axis=1, keepdims=True)
            zc = z - mu
            var = jnp.mean(zc * zc, axis=1, keepdims=True)
            rstd = lax.rsqrt(var + LN_EPS)
            xhat = zc * rstd
            diff = xhat * gn + b_ref[...] - t_ref[rs, :]
            dyln = diff / D
            st_ref[0:1, :] += jnp.sum(dyln * xhat, axis=0, keepdims=True)
            st_ref[1:2, :] += jnp.sum(dyln, axis=0, keepdims=True)
            row_loss = jnp.sum(diff * diff, axis=1, keepdims=True) / D
            st_ref[2:3, :] += jnp.broadcast_to(0.5 * jnp.sum(row_loss, axis=0, keepdims=True), (1, D))
            dxh = dyln * gn
            m1 = jnp.mean(dxh, axis=1, keepdims=True)
            m2 = jnp.mean(dxh * xhat, axis=1, keepdims=True)
            dz = rstd * (dxh - m1 - xhat * m2)
            dz_ref[rs, :] = dz
            dzb_ref[rs, :] = dz.astype(BF16)

    row = lambda i: (i, 0)
    const = lambda i: (0, 0)
    return pl.pallas_call(
        body, name="outproj_ln", grid=(S // tm,),
        in_specs=[pl.BlockSpec((tm, D), row),
                  pl.BlockSpec((D, D), const, pipeline_mode=pl.Buffered(1)),
                  pl.BlockSpec((tm, D), row), pl.BlockSpec((tm, D), row),
                  pl.BlockSpec((1, D), const), pl.BlockSpec((1, D), const)],
        out_specs=[pl.BlockSpec((tm, D), row), pl.BlockSpec((tm, D), row), pl.BlockSpec((8, D), const)],
        out_shape=[SDS((S, D), F32), SDS((S, D), BF16), SDS((8, D), F32)],
        scratch_shapes=[pltpu.VMEM((tm, D), F32)],
        compiler_params=_params(("arbitrary",)),
    )(y, wout, x, target, gain, bias)


def _bwd_mid(dzb, wout, hug, attn, wp, scale, rider=None):
    tm = 256
    n = S // tm

    def body(dz_ref, w_ref, ga_ref, gp_ref, at_ref, u_ref, uh_ref, wp_ref, sc_ref,
             dh_ref, do_ref, do4_ref, do16_ref, dd_ref, dd4_ref, dd16_ref, dwp_ref, dwp16_ref, gps_ref,
             ext_ref, eext_ref, acc_ref, nat_ref, ds_ref):
        i = pl.program_id(0)
        ib = n - 1 - i

        @pl.when(i == 0)
        def _():
            eext_ref[tm:tm + HALO, :] = jnp.zeros((HALO, DP), F32)
            acc_ref[...] = jnp.zeros((N_GROUPS, GC, GC), F32)
            gps_ref[...] = jnp.zeros((8, DP), F32)

        dy = _dot(dz_ref[...], w_ref[...], NT)

        def through_gate(dy_part, gt):
            sg = jax.nn.sigmoid(gt)
            return dy_part * (gt * sg), dy_part * (sg * (1.0 + gt * (1.0 - sg)))

        dat, dyg_a = through_gate(dy[:, 0:DA], ga_ref[...].astype(F32))
        dmix_p, dyg_p = through_gate(dy[:, DA:D], gp_ref[...].astype(F32))

        at = at_ref[...].astype(F32)
        do_ref[...] = dat.astype(BF16)
        dh_ref[:, DP:DP + DA] = (dyg_a * at).astype(BF16)
        prod = dat * at
        dd = jnp.zeros((tm, LANES), F32)
        for h in range(N_HEADS):
            hs = slice(h * DH, (h + 1) * DH)
            dd = _to_lane(dd, jnp.sum(prod[:, hs], axis=1, keepdims=True), h)
            nat_ref[h] = dat[:, hs]
        dd_ref[...] = dd
        ds_ref[...] = dd
        for d, do_out, dd_out in ((DILATIONS[1], do4_ref, dd4_ref), (DILATIONS[2], do16_ref, dd16_ref)):
            for r in range(d):
                dd_out[r] = _residues(ds_ref, (), d, r, tm // d)
                for h in range(N_HEADS):
                    do_out[r, :, h * DH:(h + 1) * DH] = _residues(nat_ref, (h,), d, r, tm // d).astype(BF16)

        _fill_ext(ext_ref, u_ref, uh_ref, ib, tm)
        rows = ib * tm + lax.broadcasted_iota(jnp.int32, (tm, 1), 0)
        for g in range(N_GROUPS):
            w = POOL_WINDOWS[g]
            cs = slice(g * GC, (g + 1) * GC)
            pooled, cnt = _pooled(ext_ref, g, rows, tm)
            pre = _dot(pooled.astype(BF16), wp_ref[g])
            sc = sc_ref[:, cs]
            dpo = dmix_p[:, cs]
            gps_ref[0:1, cs] += jnp.sum(dpo * pre, axis=0, keepdims=True)
            dh_ref[:, DP + DA + g * GC:DP + DA + (g + 1) * GC] = (dyg_p[:, cs] * (pre * sc)).astype(BF16)
            dpre = (dpo * sc).astype(BF16)
            acc_ref[g] += _dot(pooled.T.astype(BF16), dpre)
            dpooled = _dot(dpre, wp_ref[g], NT)
            eext_ref[0:tm, cs] = dpooled / cnt
            du = eext_ref[0:tm, cs]
            for j in range(1, w):
                du = du + eext_ref[j:j + tm, cs]
            dh_ref[:, cs] = (du - dpooled).astype(BF16)
        eext_ref[tm:tm + HALO, :] = eext_ref[0:HALO, :]

        @pl.when(i == n - 1)
        def _():
            for j in range(N_DEV):
                for g in range(N_GROUPS):
                    blk = acc_ref[g, j * PB:(j + 1) * PB, :]
                    dwp_ref[j, g * PB:(g + 1) * PB, :] = blk
                    dwp16_ref[j, g * PB:(g + 1) * PB, :] = blk.astype(BF16)

    rev = lambda i: (n - 1 - i, 0)
    const = lambda i: (0, 0)
    res = lambda width: [pl.BlockSpec((d, tm // d, width), lambda i: (0, n - 1 - i, 0)) for d in DILATIONS[1:]]
    pool_blocks = pl.BlockSpec((N_DEV, N_GROUPS * PB, GC), lambda i: (0, 0, 0))
    outs, carried = _call_carrying(
        body, rider, (n,),
        [pl.BlockSpec((tm, D), rev),
         pl.BlockSpec((D, D), const, pipeline_mode=pl.Buffered(1)),
         pl.BlockSpec((tm, DA), lambda i: (n - 1 - i, 1)), pl.BlockSpec((tm, DP), lambda i: (n - 1 - i, 2)),
         pl.BlockSpec((tm, DA), rev), pl.BlockSpec((tm, DP), rev),
         pl.BlockSpec((HALO, DP), lambda i: (jnp.maximum((n - 1 - i) * (tm // HALO) - 1, 0), 0)),
         pl.BlockSpec((N_GROUPS, GC, GC), lambda i: (0, 0, 0)),
         pl.BlockSpec((1, DP), const)],
        [pl.BlockSpec((tm, DP + D), lambda i: (n - 1 - i, 1)), pl.BlockSpec((tm, DA), rev)] + res(DA)
        + [pl.BlockSpec((tm, LANES), rev)] + res(LANES) + [pool_blocks, pool_blocks, pl.BlockSpec((8, DP), const)],
        [SDS((S, D_IN), BF16), SDS((S, DA), BF16)] + [SDS((d, S // d, DA), BF16) for d in DILATIONS[1:]]
        + [SDS((S, LANES), F32)] + [SDS((d, S // d, LANES), F32) for d in DILATIONS[1:]]
        + [SDS((N_DEV, N_GROUPS * PB, GC), F32), SDS((N_DEV, N_GROUPS * PB, GC), BF16), SDS((8, DP), F32)],
        [pltpu.VMEM((HALO + tm, DP), F32), pltpu.VMEM((tm + HALO, DP), F32),
         pltpu.VMEM((N_GROUPS, GC, GC), F32), pltpu.VMEM((N_HEADS, tm, DH), F32), pltpu.VMEM((tm, LANES), F32)],
        [dzb, wout, hug, hug, attn, hug, hug, wp, scale], "bwd_mid")
    return outs, carried


def _attn_bwd(h16, do, lse, dd, rope, name, others=None, dh=None, rider=None):
    d, n_sub, _ = h16.shape
    n_i = n_sub // BQ
    n_kb = n_sub // KB
    qw = BQ + KB
    final = others is not None
    out_dtype = BF16
    n_cb = 3 * DA // DH

    def nxt(i):
        return jnp.minimum(2 * i + 2, n_kb - 1)

    def body(qc_ref, qn_ref, kc_ref, vc_ref, doc_ref, don_ref, lc_ref, ln_ref, dc_ref, dn_ref,
             c_ref, sn_ref, *rest):
        if final:
            acc4_ref, acc16_ref, _, out_ref, carry_ref, qw_ref, dow_ref, lw_ref, dw_ref, nat_ref = rest
        else:
            out_ref, carry_ref, qw_ref, dow_ref, lw_ref, dw_ref = rest
        i = pl.program_id(1)

        @pl.when(i == 0)
        def _():
            carry_ref[...] = jnp.zeros((KB, DA), F32)

        for win, own, after in ((qw_ref, qc_ref, qn_ref), (dow_ref, doc_ref, don_ref), (lw_ref, lc_ref, ln_ref),
                                (dw_ref, dc_ref, dn_ref)):
            win[0:BQ, :] = own[...]
            win[BQ:qw, :] = after[...]
        if final:
            for n, (dil, acc) in enumerate(((DILATIONS[1], acc4_ref), (DILATIONS[2], acc16_ref))):
                for r in range(dil):
                    for cb in range(n_cb):
                        nat_ref[n, cb, pl.ds(r, BQ // dil, stride=dil), :] = acc[r, :, cb * DH:(cb + 1) * DH].astype(F32)

        a = lax.broadcasted_iota(jnp.int32, (qw, BQ), 0)
        b = lax.broadcasted_iota(jnp.int32, (qw, BQ), 1)
        n_q = jnp.where(i == n_i - 1, BQ, qw)
        mask = (b <= a) & (a <= b + KB) & (a < n_q)
        tok = pl.ds(pl.program_id(0) + d * BQ * i, BQ, stride=d)
        tabs = (c_ref[tok, :], sn_ref[tok, :])
        for h in range(N_HEADS):
            hs = slice(h * DH, (h + 1) * DH)
            hc = slice(h, h + 1)
            q, k, vv, dob = qw_ref[:, hs], kc_ref[:, hs], vc_ref[:, hs], dow_ref[:, hs]
            s = _dot(q, k, NT) * SCALE
            p = jnp.exp(jnp.where(mask, s - lw_ref[:, hc], NEG))
            dp = _dot(dob, vv, NT)
            ds = (p * (dp - dw_ref[:, hc]) * SCALE).astype(BF16)
            dq = _dot(ds, k)
            dk = _dot(ds, q, T_N)
            dv = _dot(p.astype(BF16), dob, T_N)
            dq_lo = dq[0:KB] + carry_ref[:, hs]
            carry_ref[:, hs] = dq[BQ:qw]
            dq_own = _rope_t(jnp.concatenate([dq_lo, dq[KB:BQ]], axis=0), *tabs)
            for base, gv in ((0, dq_own), (DA, _rope_t(dk, *tabs)), (2 * DA, dv)):
                if final:
                    cb = base // DH + h
                    gv = gv + nat_ref[0, cb] + nat_ref[1, cb]
                out_ref[:, base + h * DH:base + (h + 1) * DH] = gv.astype(out_dtype)

    def cur(rows, width, col=0):
        return pl.BlockSpec((None, rows, width), lambda r, i: (r, i, col))

    def nx(width, col=0):
        return pl.BlockSpec((None, KB, width), lambda r, i: (r, nxt(i), col))

    in_specs = [cur(BQ, DA), nx(DA), cur(BQ, DA, 1), cur(BQ, DA, 2), cur(BQ, DA), nx(DA),
                cur(BQ, LANES), nx(LANES), cur(BQ, LANES), nx(LANES)]
    in_specs += [pl.BlockSpec((S, DH), lambda r, i: (0, 0), pipeline_mode=pl.Buffered(1))] * 2
    args = [h16, h16, h16, h16, do, do, lse, lse, dd, dd] + list(rope)
    scratch = [pltpu.VMEM((KB, DA), F32), pltpu.VMEM((qw, DA), BF16), pltpu.VMEM((qw, DA), BF16),
               pltpu.VMEM((qw, LANES), F32), pltpu.VMEM((qw, LANES), F32)]
    if final:
        assert d == 1
        in_specs += [pl.BlockSpec((dil, BQ // dil, 3 * DA), lambda r, i: (0, i, 0)) for dil in DILATIONS[1:]]
        in_specs.append(pl.BlockSpec(memory_space=pl.ANY))
        args += list(others) + [dh[None]]
        scratch.append(pltpu.VMEM((2, n_cb, BQ, DH), F32))
    outs, carried = _call_carrying(
        body, rider, (d, n_i), in_specs, [cur(BQ, 3 * DA)], [SDS((d, n_sub, D_IN if final else 3 * DA), out_dtype)],
        scratch, args, name, {len(args) - 1: 0} if final else {})
    return outs[0] if rider is None else (outs[0], carried)


def _grad_w_out(yt, dzb):
    tn = 256

    def body(yt_ref, dz_ref, o_ref, o16_ref):
        acc = _dot(yt_ref[...], dz_ref[...])
        o_ref[...] = acc
        o16_ref[...] = acc.astype(BF16)

    cols = pl.BlockSpec((D, tn), lambda c: (0, c))
    o, o16 = pl.pallas_call(
        body, name="grad_w_out", grid=(D // tn,),
        in_specs=[pl.BlockSpec((D, S), lambda c: (0, 0), pipeline_mode=pl.Buffered(1)),
                  pl.BlockSpec((S, tn), lambda c: (0, c))],
        out_specs=[cols, cols],
        out_shape=[SDS((D, D), F32), SDS((D, D), BF16)],
        compiler_params=_params(("arbitrary",)),
    )(yt, dzb)
    return o.reshape(N_DEV, OB, D), o16.reshape(N_DEV, OB, D)


def _grad_w_in(xt, dh, order):
    n_send = N_DEV // 2
    n_steps = D_IN // TN
    own0 = n_send * W_PHASES
    sums0 = own0 + W_PHASES
    step_of_send = [W_PHASES * k + W_PHASES + 1 for k in range(n_send)]

    def body(ord_ref, xt_ref, dh_ref, o_ref, o16_ref, r1_ref, s2_ref, st_ref, got_ref,
             piece_sems, send_sems, recv_sems, got_sem):
        c = pl.program_id(0)
        x, y, core, _ = _place()

        def piece(cc):
            blk, sub = ord_ref[cc // W_PHASES], cc % W_PHASES
            return pltpu.make_async_copy(st_ref.at[cc % 2], o16_ref.at[blk, :, pl.ds(sub * TN, TN)], piece_sems.at[cc % 2])

        def send(k):
            return pltpu.make_async_remote_copy(
                src_ref=o16_ref.at[ord_ref[k]], dst_ref=r1_ref.at[k], send_sem=send_sems.at[k], recv_sem=recv_sems.at[k],
                device_id=(x, y, 1 - core), device_id_type=MESH)

        def sibling_part(cc):
            k, sub = cc // W_PHASES - n_send, cc % W_PHASES
            return pltpu.make_async_copy(r1_ref.at[k, :, pl.ds(sub * TN, TN)], got_ref, got_sem)

        @pl.when((c >= 2) & (c < own0 + 2))
        def _():
            piece(c - 2).wait()

        for k in range(n_send):
            pl.when(c == step_of_send[k])(lambda k=k: send(k).start())
        for k in range(1, n_send):
            pl.when(c == own0 + W_PHASES * k)(lambda k=k: send(k).wait_recv())
        pl.when(c >= sums0)(lambda: sibling_part(c).start())

        acc = _dot(xt_ref[...], dh_ref[...])
        o_ref[...] = acc

        @pl.when(c < own0)
        def _():
            st_ref[c % 2] = acc.astype(BF16)
            piece(c).start()

        @pl.when(c >= sums0)
        def _():
            sibling_part(c).wait()
            s2_ref[...] = (acc + got_ref[...].astype(F32)).astype(BF16)

        @pl.when(c == n_steps - 1)
        def _():
            send(0).wait_recv()
            for k in range(n_send):
                send(k).wait_send()

    col = lambda c, ordr: W_PHASES * ordr[c // W_PHASES] + c % W_PHASES
    hbm = pl.BlockSpec(memory_space=pl.ANY)
    sums_blk = lambda c, ordr: (jnp.maximum(c // W_PHASES - (n_send + 1), 0), 0, jnp.where(c >= sums0, c % W_PHASES, 0))
    partial, _, got, sums = pl.pallas_call(
        body, name="grad_w_in",
        grid_spec=pltpu.PrefetchScalarGridSpec(
            num_scalar_prefetch=1, grid=(n_steps,),
            in_specs=[pl.BlockSpec((D, S), lambda c, ordr: (0, 0), pipeline_mode=pl.Buffered(1)),
                      pl.BlockSpec((S, TN), lambda c, ordr: (0, col(c, ordr)))],
            out_specs=[pl.BlockSpec((None, D, TN), lambda c, ordr: (ordr[c // W_PHASES], 0, c % W_PHASES)), hbm, hbm,
                       pl.BlockSpec((None, D, TN), sums_blk)],
            scratch_shapes=[pltpu.VMEM((2, D, TN), BF16), pltpu.VMEM((D, TN), BF16), pltpu.SemaphoreType.DMA((2,)),
                            pltpu.SemaphoreType.DMA((n_send,)), pltpu.SemaphoreType.DMA((n_send,)),
                            pltpu.SemaphoreType.DMA(())]),
        out_shape=[SDS((N_DEV, D, WB), F32), SDS((N_DEV, D, WB), BF16), SDS((n_send, D, WB), BF16),
                   SDS((n_send - 1, D, WB), BF16)],
        compiler_params=_params(("arbitrary",)),
    )(order, xt, dh)
    return partial, got, sums


def _grad_x(dz, dh, wgs, rider=None):
    tm = 1024

    def body(dz_ref, dh_ref, *rest):
        o_ref = rest[-1]

        @pl.when(pl.program_id(1) == 0)
        def _():
            o_ref[...] = ALPHA * dz_ref[...]

        acc = _dot(dh_ref[:, 0:TN], rest[0][...], NT)
        for s in range(1, W_PHASES):
            acc = acc + _dot(dh_ref[:, s * TN:(s + 1) * TN], rest[s][...], NT)
        o_ref[...] += acc

    outs, carried = _call_carrying(
        body, rider, (S // tm, N_DEV),
        [pl.BlockSpec((tm, D), lambda i, j: (i, 0)), pl.BlockSpec((tm, WB), lambda i, j: (i, j))]
        + [pl.BlockSpec((None, D, TN), lambda i, j: (j, 0, 0))] * W_PHASES,
        [pl.BlockSpec((tm, D), lambda i, j: (i, 0))], [SDS((S, D), F32)], [], [dz, dh] + list(wgs), "grad_x")
    return outs[0], carried


def _place():
    x, y, c = lax.axis_index("x"), lax.axis_index("y"), lax.axis_index("c")
    chips = [(x, y), (1 - x, y), (x, 1 - y), (1 - x, 1 - y)]
    return x, y, c, chips


def _blk(x, y, c):
    return 4 * x + 2 * y + c


def _adamw(w, g, m, v):
    m = ADAM_B1 * m + (1.0 - ADAM_B1) * g
    v = ADAM_B2 * v + (1.0 - ADAM_B2) * (g * g)
    m_hat = m / (1.0 - ADAM_B1 ** ADAM_STEP)
    v_hat = v / (1.0 - ADAM_B2 ** ADAM_STEP)
    delta = -ADAM_LR * (m_hat / (jnp.sqrt(v_hat) + ADAM_EPS) + ADAM_WD * w)
    return delta, m, v


class _Rider:
    def __init__(self, args, in_specs, out_shape, out_specs, scratch, start, finish):
        self.args, self.in_specs, self.out_shape, self.out_specs = args, in_specs, out_shape, out_specs
        self.scratch, self.start, self.finish = scratch, start, finish


def _carry(body, rider, n_in, n_out, first, last):
    if rider is None:
        return body
    r_in, r_out, r_scr = len(rider.args), len(rider.out_shape), len(rider.scratch)

    def carrying(*refs):
        o0 = n_in + r_in
        s0 = o0 + n_out + r_out
        s1 = len(refs) - r_scr
        theirs = (refs[n_in:o0], refs[o0 + n_out:s0], refs[s1:])
        pl.when(first())(lambda: rider.start(*theirs))
        body(*refs[:n_in], *refs[o0:o0 + n_out], *refs[s0:s1])
        pl.when(last())(lambda: rider.finish(*theirs))

    return carrying


def _call_carrying(body, rider, grid, in_specs, out_specs, out_shape, scratch, args, name, aliases=None):
    n_in, n_out = len(in_specs), len(out_specs)
    ids = lambda: [pl.program_id(a) for a in range(len(grid))]
    first = lambda: functools.reduce(jnp.logical_and, [i == 0 for i in ids()])
    last = lambda: functools.reduce(jnp.logical_and, [i == n - 1 for i, n in zip(ids(), grid)])
    if rider is not None:
        in_specs, args = in_specs + rider.in_specs, list(args) + rider.args
        out_specs, out_shape = out_specs + rider.out_specs, out_shape + rider.out_shape
        scratch = scratch + rider.scratch
    outs = pl.pallas_call(
        _carry(body, rider, n_in, n_out, first, last), name=name, grid=grid,
        in_specs=in_specs, out_specs=out_specs, out_shape=out_shape, scratch_shapes=scratch,
        input_output_aliases=aliases or {}, compiler_params=_params(("arbitrary",) * len(grid)),
    )(*args)
    return list(outs[:n_out]), list(outs[n_out:])


def _gather_copy(tensors, send_sems, recv_sems, t, k, block, to, src=None):
    dst = tensors[t][1](_blk(*block))
    return pltpu.make_async_remote_copy(
        src_ref=dst if src is None else src, dst_ref=dst,
        send_sem=send_sems.at[t, k], recv_sem=recv_sems.at[t, k], device_id=to, device_id_type=MESH)


def _gather_start(tensors, send_sems, recv_sems, local_sems):
    x, y, c, chips = _place()
    me, sib = (x, y, c), (x, y, 1 - c)
    for t, (src, dst) in enumerate(tensors):
        pltpu.make_async_copy(src, dst(_blk(*me)), local_sems.at[t]).start()
        _gather_copy(tensors, send_sems, recv_sems, t, 0, me, sib, src).start()
        for j in (1, 2, 3):
            _gather_copy(tensors, send_sems, recv_sems, t, j, me, (*chips[j], c), src).start()


def _gather_finish(tensors, send_sems, recv_sems, local_sems):
    x, y, c, chips = _place()
    me, sib = (x, y, c), (x, y, 1 - c)
    copy = functools.partial(_gather_copy, tensors, send_sems, recv_sems)
    for t in range(len(tensors)):
        for j in (1, 2, 3):
            copy(t, j, (*chips[j], c), me).wait_recv()
            copy(t, 3 + j, (*chips[j], c), sib).start()
    for t, (src, dst) in enumerate(tensors):
        copy(t, 0, sib, me).wait_recv()
        for j in (1, 2, 3):
            copy(t, 3 + j, (*chips[j], 1 - c), me).wait_recv()
        copy(t, 0, me, sib, src).wait_send()
        for j in (1, 2, 3):
            copy(t, j, me, (*chips[j], c), src).wait_send()
            copy(t, 3 + j, (*chips[j], c), sib).wait_send()
        pltpu.make_async_copy(src, dst(_blk(*me)), local_sems.at[t]).wait()


W_PHASES = WB // TN


def _gather_first_rider(w_in_s, w_out_s, w_pool_s):
    shapes = [w_in_s.shape, w_out_s.shape, w_pool_s.shape]

    def keeps(outs, scr):
        return [pltpu.make_async_copy(scr[n], outs[n], scr[6].at[n]) for n in range(3)]

    def tensors(outs, scr):
        return [(scr[0].at[:, pl.ds(0, TN)], lambda b: outs[3].at[b])]

    def start(ins, outs, scr):
        for n in range(3):
            scr[n][...] = ins[n][...].astype(BF16)
        for keep in keeps(outs, scr):
            keep.start()
        _gather_start(tensors(outs, scr), *scr[3:6])

    def finish(ins, outs, scr):
        _gather_finish(tensors(outs, scr), *scr[3:6])
        for keep in keeps(outs, scr):
            keep.wait()

    vmem = pl.BlockSpec(memory_space=pltpu.VMEM)
    hbm = pl.BlockSpec(memory_space=pl.ANY)
    return _Rider(
        args=[w_in_s, w_out_s, w_pool_s], in_specs=[vmem] * 3,
        out_shape=[SDS(sh, BF16) for sh in shapes] + [SDS((N_DEV, D, TN), BF16)], out_specs=[hbm] * 4,
        scratch=[pltpu.VMEM(sh, BF16) for sh in shapes]
        + [pltpu.SemaphoreType.DMA((1, 7)), pltpu.SemaphoreType.DMA((1, 7)), pltpu.SemaphoreType.DMA((1,)),
           pltpu.SemaphoreType.DMA((3,))],
        start=start, finish=finish)


def _gather_w_in_rider(shard16, s):
    def tensors(ins, outs):
        return [(ins[0].at[:, pl.ds(s * TN, TN)], lambda b: outs[0].at[b])]

    hbm = pl.BlockSpec(memory_space=pl.ANY)
    return _Rider(
        args=[shard16], in_specs=[hbm], out_shape=[SDS((N_DEV, D, TN), BF16)], out_specs=[hbm],
        scratch=[pltpu.SemaphoreType.DMA((1, 7)), pltpu.SemaphoreType.DMA((1, 7)), pltpu.SemaphoreType.DMA((1,))],
        start=lambda ins, outs, scr: _gather_start(tensors(ins, outs), *scr),
        finish=lambda ins, outs, scr: _gather_finish(tensors(ins, outs), *scr))


def _gather_small_rider(w_out16, w_pool16):
    def tensors(ins, outs):
        gout_ref, gpool_ref = outs

        def pool_rows(b):
            return gpool_ref.at[:, pl.ds(pl.multiple_of(b * PB, PB), PB), :]

        return [(ins[0], lambda b: gout_ref.at[b]), (ins[1], pool_rows)]

    hbm = pl.BlockSpec(memory_space=pl.ANY)
    return _Rider(
        args=[w_out16, w_pool16], in_specs=[hbm, hbm],
        out_shape=[SDS((N_DEV, OB, D), BF16), SDS((N_GROUPS, GC, GC), BF16)], out_specs=[hbm, hbm],
        scratch=[pltpu.SemaphoreType.DMA((2, 7)), pltpu.SemaphoreType.DMA((2, 7)), pltpu.SemaphoreType.DMA((2,))],
        start=lambda ins, outs, scr: _gather_start(tensors(ins, outs), *scr),
        finish=lambda ins, outs, scr: _gather_finish(tensors(ins, outs), *scr))


def _block_table():
    x, y, c, chips = _place()
    return jnp.stack([_blk(*chip, c) for chip in chips]).astype(jnp.int32)


def _sibling_first_order():
    x, y, c, chips = _place()
    return jnp.stack([_blk(*chip, side) for side in (1 - c, c) for chip in chips]).astype(jnp.int32)


def _exchange_rider(p16):
    _, r_tot, cols = p16.shape

    def copies(ins, outs, scr):
        x, y, c, chips = _place()
        return [pltpu.make_async_remote_copy(
            src_ref=ins[0].at[_blk(*chips[k], 1 - c)], dst_ref=outs[0].at[k],
            send_sem=scr[0].at[k], recv_sem=scr[1].at[k], device_id=(x, y, 1 - c), device_id_type=MESH)
            for k in range(4)]

    def start(ins, outs, scr):
        for cp in copies(ins, outs, scr):
            cp.start()

    def finish(ins, outs, scr):
        for cp in copies(ins, outs, scr):
            cp.wait()

    hbm = pl.BlockSpec(memory_space=pl.ANY)
    return _Rider([p16], [hbm], [SDS((4, r_tot, cols), BF16)], [hbm],
                  [pltpu.SemaphoreType.DMA((4,)), pltpu.SemaphoreType.DMA((4,))], start, finish)


def _chip_sums(name, table, p32, r1, rc):
    _, r_tot, cols = p32.shape

    def body(tbl_ref, p_ref, r_ref, o_ref):
        o_ref[...] = (p_ref[...] + r_ref[...].astype(F32)).astype(BF16)

    return pl.pallas_call(
        body, name=name,
        grid_spec=pltpu.PrefetchScalarGridSpec(
            num_scalar_prefetch=1, grid=(3, r_tot // rc),
            in_specs=[pl.BlockSpec((None, rc, cols), lambda k, ch, tbl: (tbl[k + 1], ch, 0)),
                      pl.BlockSpec((None, rc, cols), lambda k, ch, tbl: (k + 1, ch, 0))],
            out_specs=pl.BlockSpec((None, rc, cols), lambda k, ch, tbl: (k, ch, 0))),
        out_shape=SDS((3, r_tot, cols), BF16),
        compiler_params=_params(("arbitrary", "arbitrary")),
    )(table, p32, r1)


def _stage2_rider(sums, stats=None):
    n_t = len(sums)

    def copies(ins, outs, scr):
        x, y, c, chips = _place()
        out = []
        for t in range(n_t):
            for k in (1, 2, 3):
                out.append(pltpu.make_async_remote_copy(
                    src_ref=ins[t].at[k - 1], dst_ref=outs[t].at[k - 1],
                    send_sem=scr[0].at[t, k - 1], recv_sem=scr[1].at[t, k - 1],
                    device_id=(*chips[k], c), device_id_type=MESH))
        if stats is not None:
            for k in range(1, N_DEV):
                peer = (x ^ ((k >> 2) & 1), y ^ ((k >> 1) & 1), c ^ (k & 1))
                out.append(pltpu.make_async_remote_copy(
                    src_ref=scr[4], dst_ref=outs[n_t].at[_blk(x, y, c)],
                    send_sem=scr[2].at[k - 1], recv_sem=scr[3].at[k - 1], device_id=peer, device_id_type=MESH))
        return out

    def own_rows(outs, scr):
        x, y, c, _ = _place()
        return pltpu.make_async_copy(scr[4], outs[n_t].at[_blk(x, y, c)], scr[5])

    def start(ins, outs, scr):
        if stats is not None:
            scr[4][...] = ins[n_t][...]
            scr[4][3:4, 0:DP] = ins[n_t + 1][0:1, :]
            own_rows(outs, scr).start()
        for cp in copies(ins, outs, scr):
            cp.start()

    def finish(ins, outs, scr):
        for cp in copies(ins, outs, scr):
            cp.wait()
        if stats is not None:
            own_rows(outs, scr).wait()

    vmem = pl.BlockSpec(memory_space=pltpu.VMEM)
    hbm = pl.BlockSpec(memory_space=pl.ANY)
    scratch = [pltpu.SemaphoreType.DMA((n_t, 3)), pltpu.SemaphoreType.DMA((n_t, 3))]
    args, in_specs = list(sums), [hbm] * n_t
    out_shape, out_specs = [SDS(s.shape, BF16) for s in sums], [hbm] * n_t
    if stats is not None:
        scratch += [pltpu.SemaphoreType.DMA((N_DEV - 1,)), pltpu.SemaphoreType.DMA((N_DEV - 1,)),
                    pltpu.VMEM((8, D), F32), pltpu.SemaphoreType.DMA(())]
        args, in_specs = args + list(stats), in_specs + [vmem, vmem]
        out_shape, out_specs = out_shape + [SDS((N_DEV, 8, D), F32)], out_specs + [hbm]
    return _Rider(args, in_specs, out_shape, out_specs, scratch, start, finish)


def _adamw_shard(name, table, p32, r1, r2, w, m, v, rc):
    _, r_tot, cols = p32.shape

    def body(tbl_ref, p_ref, r1_ref, r2_ref, w_ref, m_ref, v_ref, g_ref, d_ref, nm_ref, nv_ref):
        g = p_ref[...] + r1_ref[...].astype(F32)
        for k in range(3):
            g = g + r2_ref[k].astype(F32)
        delta, nm, nv = _adamw(w_ref[...], g, m_ref[...], v_ref[...])
        g_ref[...] = g
        d_ref[...] = delta
        nm_ref[...] = nm
        nv_ref[...] = nv

    rows = pl.BlockSpec((rc, cols), lambda ch, tbl: (ch, 0))
    shard = SDS((r_tot, cols), F32)
    return pl.pallas_call(
        body, name=name,
        grid_spec=pltpu.PrefetchScalarGridSpec(
            num_scalar_prefetch=1, grid=(r_tot // rc,),
            in_specs=[pl.BlockSpec((None, rc, cols), lambda ch, tbl: (tbl[0], ch, 0)),
                      pl.BlockSpec((None, rc, cols), lambda ch, tbl: (0, ch, 0)),
                      pl.BlockSpec((3, rc, cols), lambda ch, tbl: (0, ch, 0)), rows, rows, rows],
            out_specs=[rows, rows, rows, rows]),
        out_shape=[shard, shard, shard, shard],
        compiler_params=_params(("arbitrary",)),
    )(table, p32, r1, r2, w, m, v)


def _replicated_adamw(gathered, gain, bias, scale, m_gain, m_bias, m_scale, v_gain, v_bias, v_scale):
    def body(all_ref, s_ref, g_ref, b_ref, ms_ref, mg_ref, mb_ref, vs_ref, vg_ref, vb_ref, loss_ref, *outs):
        tot = all_ref[0]
        for b in range(1, N_DEV):
            tot = tot + all_ref[b]
        loss_ref[...] = tot[2:3, 0:LANES]
        for n, (row, width, w_r, m_r, v_r) in enumerate(((3, DP, s_ref, ms_ref, vs_ref), (0, D, g_ref, mg_ref, vg_ref),
                                                         (1, D, b_ref, mb_ref, vb_ref))):
            g = tot[row:row + 1, 0:width]
            delta, nm, nv = _adamw(w_r[...], g, m_r[...], v_r[...])
            for out, val in zip(outs[4 * n:4 * n + 4], (g, delta, nm, nv)):
                out[...] = val

    vmem = pl.BlockSpec(memory_space=pltpu.VMEM)
    shapes = [SDS((1, LANES), F32)] + [SDS((1, width), F32) for width in (DP, D, D) for _ in range(4)]
    return pl.pallas_call(
        body, name="replicated_adamw",
        in_specs=[vmem] * 10, out_specs=[vmem] * len(shapes), out_shape=shapes,
        compiler_params=_params(),
    )(gathered, scale, gain, bias, m_scale, m_gain, m_bias, v_scale, v_gain, v_bias)


def kernel(x, w_in, w_pool, pool_scale, w_out, ln_gain, ln_bias, loss_target, m_w_in, m_w_pool, m_pool_scale, m_w_out, m_ln_gain, m_ln_bias, v_w_in, v_w_pool, v_pool_scale, v_w_out, v_ln_gain, v_ln_bias):
    pool_rows = (N_GROUPS * PB, GC)
    x2, target = x[0], loss_target[0]
    table = _block_table()

    (xb, xt, *rope), (shard16, w_out16, w_pool16, wg0) = _prep_x(
        x2, _rope_rates(), _gather_first_rider(w_in[0], w_out[0], w_pool[0]))
    h, (wg1, wg2, wg_out, wg_pool) = _proj_all(xb, rope, wg0, shard16, w_out16, w_pool16)
    wg_out = wg_out.reshape(D, D)
    h16, hug = [h[0][None], h[1], h[2]], h[3]
    fwd = [_attn_fwd(h16[n], f"attn_fwd_d{d}") for n, d in enumerate(DILATIONS)]
    y, yt, attn, lse1, lse4, lse16 = _mix(fwd[0][0][0], fwd[0][1][0], *fwd[1], *fwd[2], hug, wg_pool, pool_scale)
    dz, dzb, stats = _outproj_ln(y, wg_out, x2, target, ln_gain, ln_bias)

    dwout, dwout16 = _grad_w_out(yt, dzb)
    mid, (r1_out,) = _bwd_mid(dzb, wg_out, hug, attn, wg_pool, pool_scale, rider=_exchange_rider(dwout16))
    dh, do1, do4, do16, dd1, dd4, dd16, dwp, dwp16, gps = mid
    s2_out = _chip_sums("rs_out_sums", table, dwout, r1_out, OB)

    do, lse, dd = [do1[None], do4, do16], [lse1[None], lse4, lse16], [dd1[None], dd4, dd16]
    acc4, (r2_out,) = _attn_bwd(h16[1], do[1], lse[1], dd[1], rope, "attn_bwd_d4", rider=_stage2_rider([s2_out]))
    g_out, d_out, nm_out, nv_out = _adamw_shard(
        "adamw_w_out", table, dwout, r1_out, r2_out, w_out[0], m_w_out[0], v_w_out[0], OB // 2)
    acc16, (r1_pool,) = _attn_bwd(h16[2], do[2], lse[2], dd[2], rope, "attn_bwd_d16", rider=_exchange_rider(dwp16))
    s2_pool = _chip_sums("rs_pool_sums", table, dwp, r1_pool, N_GROUPS * PB)
    dh = _attn_bwd(h16[0], do[0], lse[0], dd[0], rope, "attn_bwd_d1", others=(acc4, acc16), dh=dh)[0]

    dwin, r1_in, s2_in = _grad_w_in(xt, dh, _sibling_first_order())
    grad_x, (r2_in, r2_pool, gathered) = _grad_x(
        dz, dh, (wg0, wg1, wg2), rider=_stage2_rider([s2_in, s2_pool], stats=(stats, gps)))
    g_in, d_in, nm_in, nv_in = _adamw_shard(
        "adamw_w_in", table, dwin, r1_in, r2_in, w_in[0], m_w_in[0], v_w_in[0], 512)
    g_pool, d_pool, nm_pool, nv_pool = _adamw_shard(
        "adamw_w_pool", table, dwp, r1_pool, r2_pool, w_pool[0].reshape(pool_rows), m_w_pool[0].reshape(pool_rows),
        v_w_pool[0].reshape(pool_rows), N_GROUPS * PB)
    loss, *small = _replicated_adamw(gathered, ln_gain, ln_bias, pool_scale, m_ln_gain, m_ln_bias, m_pool_scale,
                                     v_ln_gain, v_ln_bias, v_pool_scale)
    (g_ps, d_ps, nm_ps, nv_ps, g_gain, d_gain, nm_gain, nv_gain, g_bias, d_bias, nm_bias, nv_bias) = small

    shard4 = lambda t: t.reshape(1, N_GROUPS, PB, GC)
    lead = lambda t: t[None]
    return (loss[0, 0], lead(grad_x),
            lead(g_in), shard4(g_pool), g_ps, lead(g_out), g_gain, g_bias,
            lead(d_in), shard4(d_pool), d_ps, lead(d_out), d_gain, d_bias,
            lead(nm_in), shard4(nm_pool), nm_ps, lead(nm_out), nm_gain, nm_bias,
            lead(nv_in), shard4(nv_pool), nv_ps, lead(nv_out), nv_gain, nv_bias)
```

```python
import functools

import jax
import jax.numpy as jnp
from jax import lax
from jax.experimental import pallas as pl
from jax.experimental.pallas import tpu as pltpu

F32 = jnp.float32
BF16 = jnp.bfloat16
SDS = jax.ShapeDtypeStruct
MESH = pl.DeviceIdType.MESH

N_DEV = 8
S = 4096
D = 2048
N_HEADS = 8
DH = 128
DA = N_HEADS * DH
DP = 1024
N_GROUPS = 4
GC = DP // N_GROUPS
POOL_WINDOWS = (2, 4, 8, 16)
HALO = 16
D_IN = 3 * DA + DP + D
WB = D_IN // N_DEV
TN = 256
HW = 3 * DA
OB = D // N_DEV
PB = GC // N_DEV
ROPE_DIM = DH // 4
ROPE_HALF = ROPE_DIM // 2
ROPE_THETA = 500000.0
DILATIONS = (1, 4, 16)
KB = 128
LN_EPS = 1e-5
ALPHA = 2.0 ** 0.25
SCALE = DH ** -0.5
NEG = -1e30
ADAM_LR, ADAM_B1, ADAM_B2, ADAM_EPS, ADAM_WD, ADAM_STEP = 0.001, 0.9, 0.999, 1e-08, 0.01, 10

VMEM_LIMIT_V7X = 61 * 1024 * 1024

NT = (((1,), (1,)), ((), ()))
T_N = (((0,), (0,)), ((), ()))


def _params(sem=None):
    return pltpu.CompilerParams(dimension_semantics=sem, vmem_limit_bytes=VMEM_LIMIT_V7X)


def _dot(a, b, dims=None):
    if dims is None:
        return jnp.dot(a, b, preferred_element_type=F32)
    return lax.dot_general(a, b, dims, preferred_element_type=F32)


def _rope_rates():
    inv_freq = ROPE_THETA ** (-(2.0 * jnp.arange(ROPE_HALF, dtype=F32)) / ROPE_DIM)
    return jnp.concatenate([inv_freq, inv_freq, jnp.zeros((DH - ROPE_DIM,), F32)])[None, :]


def _rope_table_rows(rates, first_row, n):
    pos = (first_row + lax.broadcasted_iota(jnp.int32, (n, DH), 0)).astype(F32)
    lane = lax.broadcasted_iota(jnp.int32, (n, DH), 1)
    ang = pos * rates
    sin = jnp.sin(ang)
    c = jnp.where(lane < ROPE_DIM, jnp.cos(ang), 1.0)
    sn = jnp.where(lane < ROPE_HALF, -sin, jnp.where(lane < ROPE_DIM, sin, 0.0))
    return c, sn


def _rope_partner(t):
    lane = lax.broadcasted_iota(jnp.int32, t.shape, 1)
    return jnp.where(lane < ROPE_HALF, pltpu.roll(t, DH - ROPE_HALF, 1), pltpu.roll(t, ROPE_HALF, 1))


def _rope(t, c, sn):
    return t * c + _rope_partner(t) * sn


def _rope_t(g, c, sn):
    return g * c - _rope_partner(g) * sn


def _prep_x(x, rates, rider):
    tm = 512

    def body(x_ref, f_ref, xb_ref, xt_ref, c_ref, sn_ref):
        xv = x_ref[...]
        xb_ref[...] = xv.astype(BF16)
        xt_ref[...] = xv.T.astype(BF16)
        c_ref[...], sn_ref[...] = _rope_table_rows(f_ref[...], pl.program_id(0) * tm, tm)

    row = lambda i: (i, 0)
    return _call_carrying(
        body, rider, (S // tm,), [pl.BlockSpec((tm, D), row), pl.BlockSpec((1, DH), lambda i: (0, 0))],
        [pl.BlockSpec((tm, D), row), pl.BlockSpec((D, tm), lambda i: (0, i)),
         pl.BlockSpec((tm, DH), row), pl.BlockSpec((tm, DH), row)],
        [SDS((S, D), BF16), SDS((D, S), BF16), SDS((S, DH), F32), SDS((S, DH), F32)], [], [x, rates], "prep_x")


def _residues(slab_ref, idx, d, r, n):
    return slab_ref[(*idx, pl.ds(r, n, stride=d), slice(None))]


def _proj_all(xb, rope, wg0, shard16, w_out16, w_pool16):
    n_heads = TN // DH
    n16 = HW // TN
    j16 = n16 // W_PHASES
    n_rope = 2 * DA // TN
    once = pl.Buffered(1)
    riders = [_gather_w_in_rider(shard16, s) for s in range(1, W_PHASES)] + [_gather_small_rider(w_out16, w_pool16)]
    last_j = N_DEV - 1

    def body(*refs):
        x_ref, c_ref, sn_ref, g0_ref = refs[:4]
        pos = 4
        r_ins, r_outs, r_scrs = [], [], []
        for r in riders:
            r_ins.append(refs[pos:pos + len(r.args)])
            pos += len(r.args)
        h16_ref, h4_ref, h16r_ref, hug_ref = refs[pos:pos + 4]
        pos += 4
        for r in riders:
            r_outs.append(refs[pos:pos + len(r.out_shape)])
            pos += len(r.out_shape)
        slab_ref, res_ref, w_ref, w_sems = refs[pos:pos + 4]
        pos += 4
        for r in riders:
            r_scrs.append(refs[pos:pos + len(r.scratch)])
            pos += len(r.scratch)
        gathered = [g0_ref] + [r_outs[s][0] for s in range(W_PHASES - 1)]
        s, j = pl.program_id(0), pl.program_id(1)
        d4, d16 = DILATIONS[1], DILATIONS[2]

        def fetch(ph, blk, slot):
            return pltpu.make_async_copy(gathered[ph].at[blk], w_ref.at[slot], w_sems.at[slot])

        pl.when((s == 0) & (j == 0))(lambda: fetch(0, 0, 0).start())
        for ph in range(W_PHASES):
            pl.when((s == ph) & (j == 0))(lambda ph=ph: riders[ph].start(r_ins[ph], r_outs[ph], r_scrs[ph]))
            pl.when(s == ph)(lambda ph=ph: fetch(ph, j, j % 2).wait())
            pl.when((s == ph) & (j == last_j))(lambda ph=ph: riders[ph].finish(r_ins[ph], r_outs[ph], r_scrs[ph]))
            pl.when((s == ph) & (j < last_j))(lambda ph=ph: fetch(ph, j + 1, (j + 1) % 2).start())
            if ph + 1 < W_PHASES:
                pl.when((s == ph) & (j == last_j))(lambda ph=ph: fetch(ph + 1, 0, 0).start())

        acc = _dot(x_ref[...], w_ref[j % 2])

        def heads(with_rope):
            for hh in range(n_heads):
                hs = slice(hh * DH, (hh + 1) * DH)
                t = _rope(acc[:, hs], c_ref[...], sn_ref[...]) if with_rope else acc[:, hs]
                h16_ref[:, hs] = t.astype(BF16)
                slab_ref[...] = t
                for q in range(d4):
                    t4 = _residues(slab_ref, (), d4, q, S // d4)
                    h4_ref[q, :, hs] = t4.astype(BF16)
                    res_ref[...] = t4
                    for m in range(d16 // d4):
                        h16r_ref[d4 * m + q, :, hs] = _residues(res_ref, (), d4, m, S // d16).astype(BF16)

        pl.when(W_PHASES * j + s < n_rope)(lambda: heads(True))
        pl.when((W_PHASES * j + s >= n_rope) & (j < j16))(lambda: heads(False))

        @pl.when(j >= j16)
        def _():
            hug_ref[...] = acc.astype(BF16)

    col16 = lambda s, j: W_PHASES * jnp.minimum(j, j16 - 1) + s
    colug = lambda s, j: W_PHASES * jnp.maximum(j, j16) + s - n16
    hbm = pl.BlockSpec(memory_space=pl.ANY)
    resident = lambda shape: pl.BlockSpec(shape, lambda s, j: (0, 0), pipeline_mode=once)
    in_specs = [resident((S, D)), resident((S, DH)), resident((S, DH)), hbm]
    args = [xb, *rope, wg0]
    out_specs = [pl.BlockSpec((S, TN), lambda s, j: (0, col16(s, j)))]
    out_specs += [pl.BlockSpec((d, S // d, TN), lambda s, j: (0, 0, col16(s, j))) for d in DILATIONS[1:]]
    out_specs += [pl.BlockSpec((S, TN), lambda s, j: (0, colug(s, j)))]
    out_shape = [SDS((S, HW), BF16)] + [SDS((d, S // d, HW), BF16) for d in DILATIONS[1:]] + [SDS((S, HW), BF16)]
    scratch = [pltpu.VMEM((S, DH), F32), pltpu.VMEM((S // DILATIONS[1], DH), F32), pltpu.VMEM((2, D, TN), BF16),
               pltpu.SemaphoreType.DMA((2,))]
    for r in riders:
        in_specs, args = in_specs + r.in_specs, args + r.args
        out_specs, out_shape = out_specs + r.out_specs, out_shape + r.out_shape
        scratch = scratch + r.scratch
    outs = pl.pallas_call(
        body, name="proj_all", grid=(W_PHASES, N_DEV),
        in_specs=in_specs, out_specs=out_specs, out_shape=out_shape, scratch_shapes=scratch,
        compiler_params=_params(("arbitrary", "arbitrary")),
    )(*args)
    return list(outs[:4]), list(outs[4:])


BQ = 2 * KB
LANES = 128


def _to_lane(acc, col, h):
    lane = lax.broadcasted_iota(jnp.int32, acc.shape, 1)
    return jnp.where(lane == h, col, acc)


def _attn_fwd(h16, name):
    d, n_sub, _ = h16.shape
    n_i = n_sub // BQ
    kw = KB + BQ

    def prev(i):
        return jnp.maximum(2 * i - 1, 0)

    def body(q_ref, kc_ref, kp_ref, vc_ref, vp_ref, o_ref, l_ref, kw_ref, vw_ref):
        i = pl.program_id(1)
        kw_ref[0:KB, :] = kp_ref[...]
        kw_ref[KB:kw, :] = kc_ref[...]
        vw_ref[0:KB, :] = vp_ref[...]
        vw_ref[KB:kw, :] = vc_ref[...]
        a = lax.broadcasted_iota(jnp.int32, (KB, 2 * KB), 0)
        b = lax.broadcasted_iota(jnp.int32, (KB, 2 * KB), 1)
        band = (b >= a) & (b <= a + KB)
        first_key = jnp.where(i == 0, KB, 0)
        masks = (band & (b >= first_key), band)
        for half in range(2):
            rs = slice(half * KB, (half + 1) * KB)
            ks = slice(half * KB, (half + 2) * KB)
            lse = jnp.zeros((KB, LANES), F32)
            for h in range(N_HEADS):
                hs = slice(h * DH, (h + 1) * DH)
                s = jnp.where(masks[half], _dot(q_ref[rs, hs], kw_ref[ks, hs], NT) * SCALE, NEG)
                m = jnp.max(s, axis=1, keepdims=True)
                p = jnp.exp(s - m)
                den = jnp.sum(p, axis=1, keepdims=True)
                o_ref[rs, hs] = (_dot(p.astype(BF16), vw_ref[ks, hs]) / den).astype(BF16)
                lse = _to_lane(lse, m + jnp.log(den), h)
            l_ref[rs, :] = lse

    return pl.pallas_call(
        body, name=name, grid=(d, n_i),
        in_specs=[pl.BlockSpec((None, BQ, DA), lambda r, i: (r, i, 0)),
                  pl.BlockSpec((None, BQ, DA), lambda r, i: (r, i, 1)),
                  pl.BlockSpec((None, KB, DA), lambda r, i: (r, prev(i), 1)),
                  pl.BlockSpec((None, BQ, DA), lambda r, i: (r, i, 2)),
                  pl.BlockSpec((None, KB, DA), lambda r, i: (r, prev(i), 2))],
        out_specs=[pl.BlockSpec((None, BQ, DA), lambda r, i: (r, i, 0)),
                   pl.BlockSpec((None, BQ, LANES), lambda r, i: (r, i, 0))],
        out_shape=[SDS((d, n_sub, DA), BF16), SDS((d, n_sub, LANES), F32)],
        scratch_shapes=[pltpu.VMEM((kw, DA), BF16), pltpu.VMEM((kw, DA), BF16)],
        compiler_params=_params(("arbitrary", "arbitrary")),
    )(h16, h16, h16, h16, h16)


def _pooled(ext_ref, g, rows, tm):
    w = POOL_WINDOWS[g]
    cs = slice(g * GC, (g + 1) * GC)
    cur = ext_ref[HALO:HALO + tm, cs]
    win = cur
    for j in range(1, w):
        win = win + ext_ref[HALO - j:HALO - j + tm, cs]
    cnt = jnp.minimum(rows + 1, w).astype(F32)
    return win / cnt - cur, cnt


def _fill_ext(ext_ref, u_ref, uh_ref, blk, tm):
    @pl.when(blk == 0)
    def _():
        ext_ref[0:HALO, :] = jnp.zeros((HALO, DP), F32)

    @pl.when(blk > 0)
    def _():
        ext_ref[0:HALO, :] = uh_ref[...].astype(F32)

    ext_ref[HALO:HALO + tm, :] = u_ref[...].astype(F32)


def _residue_specs(tm, width):
    return [pl.BlockSpec((d, tm // d, width), lambda i: (0, i, 0)) for d in DILATIONS[1:]]


def _mix(o1, l1, o4, l4, o16, l16, hug, wp, scale):
    tm = 256
    n_slab = N_HEADS + 1

    def body(o1r, l1r, o4r, l4r, o16r, l16r, u_ref, uh_ref, ga_ref, gp_ref, wp_ref, sc_ref,
             y_ref, yt_ref, attn_ref, lse_ref, lse4_ref, lse16_ref, ext_ref, ys_ref, nat_ref, ls_ref):
        i = pl.program_id(0)
        for n, (d, o_r, l_r) in enumerate(((DILATIONS[1], o4r, l4r), (DILATIONS[2], o16r, l16r))):
            for r in range(d):
                rows = pl.ds(r, tm // d, stride=d)
                for h in range(N_HEADS):
                    nat_ref[n, h, rows, :] = o_r[r, :, h * DH:(h + 1) * DH].astype(F32)
                nat_ref[n, N_HEADS, rows, :] = l_r[r]
        la, lb, lc = l1r[...], nat_ref[0, N_HEADS], nat_ref[1, N_HEADS]
        mx = jnp.maximum(jnp.maximum(la, lb), lc)
        ea, eb, ec = jnp.exp(la - mx), jnp.exp(lb - mx), jnp.exp(lc - mx)
        z = ea + eb + ec
        wa, wb, wc = ea / z, eb / z, ec / z
        lse = mx + jnp.log(z)
        lse_ref[...] = lse
        ls_ref[...] = lse
        for d, out in ((DILATIONS[1], lse4_ref), (DILATIONS[2], lse16_ref)):
            for r in range(d):
                out[r] = ls_ref[pl.ds(r, tm // d, stride=d), :]
        for h in range(N_HEADS):
            hs = slice(h * DH, (h + 1) * DH)
            hc = slice(h, h + 1)
            attn = wa[:, hc] * o1r[:, hs].astype(F32) + wb[:, hc] * nat_ref[0, h] + wc[:, hc] * nat_ref[1, h]
            attn_ref[:, hs] = attn.astype(BF16)
            gt = ga_ref[:, hs].astype(F32)
            ys_ref[:, hs] = attn * (gt * jax.nn.sigmoid(gt))

        _fill_ext(ext_ref, u_ref, uh_ref, i, tm)
        rows = i * tm + lax.broadcasted_iota(jnp.int32, (tm, 1), 0)
        for g in range(N_GROUPS):
            cs = slice(g * GC, (g + 1) * GC)
            gs = slice(DA + g * GC, DA + (g + 1) * GC)
            pooled, _ = _pooled(ext_ref, g, rows, tm)
            po = _dot(pooled.astype(BF16), wp_ref[g]) * sc_ref[:, cs]
            gt = gp_ref[:, cs].astype(F32)
            ys_ref[:, gs] = po * (gt * jax.nn.sigmoid(gt))
        yv = ys_ref[...]
        y_ref[...] = yv.astype(BF16)
        yt_ref[...] = yv.T.astype(BF16)

    row = lambda i: (i, 0)
    blk = pl.BlockSpec((tm, DA), row)
    lanes = pl.BlockSpec((tm, LANES), row)
    o_res, l_res = _residue_specs(tm, DA), _residue_specs(tm, LANES)
    return pl.pallas_call(
        body, name="mix", grid=(S // tm,),
        in_specs=[blk, lanes, o_res[0], l_res[0], o_res[1], l_res[1],
                  pl.BlockSpec((tm, DP), row),
                  pl.BlockSpec((HALO, DP), lambda i: (jnp.maximum(i * (tm // HALO) - 1, 0), 0)),
                  pl.BlockSpec((tm, DA), lambda i: (i, 1)), pl.BlockSpec((tm, DP), lambda i: (i, 2)),
                  pl.BlockSpec((N_GROUPS, GC, GC), lambda i: (0, 0, 0)),
                  pl.BlockSpec((1, DP), lambda i: (0, 0))],
        out_specs=[pl.BlockSpec((tm, D), row), pl.BlockSpec((D, tm), lambda i: (0, i)), blk, lanes] + l_res,
        out_shape=[SDS((S, D), BF16), SDS((D, S), BF16), SDS((S, DA), BF16), SDS((S, LANES), F32)]
        + [SDS((d, S // d, LANES), F32) for d in DILATIONS[1:]],
        scratch_shapes=[pltpu.VMEM((HALO + tm, DP), F32), pltpu.VMEM((tm, D), F32),
                        pltpu.VMEM((2, n_slab, tm, DH), F32), pltpu.VMEM((tm, LANES), F32)],
        compiler_params=_params(("arbitrary",)),
    )(o1, l1, o4, l4, o16, l16, hug, hug, hug, hug, wp, scale)


def _outproj_ln(y, wout, x, target, gain, bias):
    tm = 512
    te = 128

    def body(y_ref, w_ref, x_ref, t_ref, g_ref, b_ref, dz_ref, dzb_ref, st_ref, out_ref):
        i = pl.program_id(0)

        @pl.when(i == 0)
        def _():
            st_ref[...] = jnp.zeros((8, D), F32)

        out_ref[...] = _dot(y_ref[...], w_ref[...])
        gn = g_ref[...]
        for e in range(tm // te):
            rs = slice(e * te, (e + 1) * te)
            z = ALPHA * x_ref[rs, :] + out_ref[rs, :]
            mu = jnp.mean(z, axis=1, keepdims=True)
            zc = z - mu
            var = jnp.mean(zc * zc, axis=1, keepdims=True)
            rstd = lax.rsqrt(var + LN_EPS)
            xhat = zc * rstd
            diff = xhat * gn + b_ref[...] - t_ref[rs, :]
            dyln = diff / D
            st_ref[0:1, :] += jnp.sum(dyln * xhat, axis=0, keepdims=True)
            st_ref[1:2, :] += jnp.sum(dyln, axis=0, keepdims=True)
            row_loss = jnp.sum(diff * diff, axis=1, keepdims=True) / D
            st_ref[2:3, :] += jnp.broadcast_to(0.5 * jnp.sum(row_loss, axis=0, keepdims=True), (1, D))
            dxh = dyln * gn
            m1 = jnp.mean(dxh, axis=1, keepdims=True)
            m2 = jnp.mean(dxh * xhat, axis=1, keepdims=True)
            dz = rstd * (dxh - m1 - xhat * m2)
            dz_ref[rs, :] = dz
            dzb_ref[rs, :] = dz.astype(BF16)

    row = lambda i: (i, 0)
    const = lambda i: (0, 0)
    return pl.pallas_call(
        body, name="outproj_ln", grid=(S // tm,),
        in_specs=[pl.BlockSpec((tm, D), row),
                  pl.BlockSpec((D, D), const, pipeline_mode=pl.Buffered(1)),
                  pl.BlockSpec((tm, D), row), pl.BlockSpec((tm, D), row),
                  pl.BlockSpec((1, D), const), pl.BlockSpec((1, D), const)],
        out_specs=[pl.BlockSpec((tm, D), row), pl.BlockSpec((tm, D), row), pl.BlockSpec((8, D), const)],
        out_shape=[SDS((S, D), F32), SDS((S, D), BF16), SDS((8, D), F32)],
        scratch_shapes=[pltpu.VMEM((tm, D), F32)],
        compiler_params=_params(("arbitrary",)),
    )(y, wout, x, target, gain, bias)


def _bwd_mid(dzb, wout, hug, attn, wp, scale, rider=None):
    tm = 256
    n = S // tm

    def body(dz_ref, w_ref, ga_ref, gp_ref, at_ref, u_ref, uh_ref, wp_ref, sc_ref,
             dh_ref, do_ref, do4_ref, do16_ref, dd_ref, dd4_ref, dd16_ref, dwp_ref, dwp16_ref, gps_ref,
             ext_ref, eext_ref, acc_ref, nat_ref, ds_ref):
        i = pl.program_id(0)
        ib = n - 1 - i

        @pl.when(i == 0)
        def _():
            eext_ref[tm:tm + HALO, :] = jnp.zeros((HALO, DP), F32)
            acc_ref[...] = jnp.zeros((N_GROUPS, GC, GC), F32)
            gps_ref[...] = jnp.zeros((8, DP), F32)

        dy = _dot(dz_ref[...], w_ref[...], NT)

        def through_gate(dy_part, gt):
            sg = jax.nn.sigmoid(gt)
            return dy_part * (gt * sg), dy_part * (sg * (1.0 + gt * (1.0 - sg)))

        dat, dyg_a = through_gate(dy[:, 0:DA], ga_ref[...].astype(F32))
        dmix_p, dyg_p = through_gate(dy[:, DA:D], gp_ref[...].astype(F32))

        at = at_ref[...].astype(F32)
        do_ref[...] = dat.astype(BF16)
        dh_ref[:, DP:DP + DA] = (dyg_a * at).astype(BF16)
        prod = dat * at
        dd = jnp.zeros((tm, LANES), F32)
        for h in range(N_HEADS):
            hs = slice(h * DH, (h + 1) * DH)
            dd = _to_lane(dd, jnp.sum(prod[:, hs], axis=1, keepdims=True), h)
            nat_ref[h] = dat[:, hs]
        dd_ref[...] = dd
        ds_ref[...] = dd
        for d, do_out, dd_out in ((DILATIONS[1], do4_ref, dd4_ref), (DILATIONS[2], do16_ref, dd16_ref)):
            for r in range(d):
                dd_out[r] = _residues(ds_ref, (), d, r, tm // d)
                for h in range(N_HEADS):
                    do_out[r, :, h * DH:(h + 1) * DH] = _residues(nat_ref, (h,), d, r, tm // d).astype(BF16)

        _fill_ext(ext_ref, u_ref, uh_ref, ib, tm)
        rows = ib * tm + lax.broadcasted_iota(jnp.int32, (tm, 1), 0)
        for g in range(N_GROUPS):
            w = POOL_WINDOWS[g]
            cs = slice(g * GC, (g + 1) * GC)
            pooled, cnt = _pooled(ext_ref, g, rows, tm)
            pre = _dot(pooled.astype(BF16), wp_ref[g])
            sc = sc_ref[:, cs]
            dpo = dmix_p[:, cs]
            gps_ref[0:1, cs] += jnp.sum(dpo * pre, axis=0, keepdims=True)
            dh_ref[:, DP + DA + g * GC:DP + DA + (g + 1) * GC] = (dyg_p[:, cs] * (pre * sc)).astype(BF16)
            dpre = (dpo * sc).astype(BF16)
            acc_ref[g] += _dot(pooled.T.astype(BF16), dpre)
            dpooled = _dot(dpre, wp_ref[g], NT)
            eext_ref[0:tm, cs] = dpooled / cnt
            du = eext_ref[0:tm, cs]
            for j in range(1, w):
                du = du + eext_ref[j:j + tm, cs]
            dh_ref[:, cs] = (du - dpooled).astype(BF16)
        eext_ref[tm:tm + HALO, :] = eext_ref[0:HALO, :]

        @pl.when(i == n - 1)
        def _():
            for j in range(N_DEV):
                for g in range(N_GROUPS):
                    blk = acc_ref[g, j * PB:(j + 1) * PB, :]
                    dwp_ref[j, g * PB:(g + 1) * PB, :] = blk
                    dwp16_ref[j, g * PB:(g + 1) * PB, :] = blk.astype(BF16)

    rev = lambda i: (n - 1 - i, 0)
    const = lambda i: (0, 0)
    res = lambda width: [pl.BlockSpec((d, tm // d, width), lambda i: (0, n - 1 - i, 0)) for d in DILATIONS[1:]]
    pool_blocks = pl.BlockSpec((N_DEV, N_GROUPS * PB, GC), lambda i: (0, 0, 0))
    outs, carried = _call_carrying(
        body, rider, (n,),
        [pl.BlockSpec((tm, D), rev),
         pl.BlockSpec((D, D), const, pipeline_mode=pl.Buffered(1)),
         pl.BlockSpec((tm, DA), lambda i: (n - 1 - i, 1)), pl.BlockSpec((tm, DP), lambda i: (n - 1 - i, 2)),
         pl.BlockSpec((tm, DA), rev), pl.BlockSpec((tm, DP), rev),
         pl.BlockSpec((HALO, DP), lambda i: (jnp.maximum((n - 1 - i) * (tm // HALO) - 1, 0), 0)),
         pl.BlockSpec((N_GROUPS, GC, GC), lambda i: (0, 0, 0)),
         pl.BlockSpec((1, DP), const)],
        [pl.BlockSpec((tm, DP + D), lambda i: (n - 1 - i, 1)), pl.BlockSpec((tm, DA), rev)] + res(DA)
        + [pl.BlockSpec((tm, LANES), rev)] + res(LANES) + [pool_blocks, pool_blocks, pl.BlockSpec((8, DP), const)],
        [SDS((S, D_IN), BF16), SDS((S, DA), BF16)] + [SDS((d, S // d, DA), BF16) for d in DILATIONS[1:]]
        + [SDS((S, LANES), F32)] + [SDS((d, S // d, LANES), F32) for d in DILATIONS[1:]]
        + [SDS((N_DEV, N_GROUPS * PB, GC), F32), SDS((N_DEV, N_GROUPS * PB, GC), BF16), SDS((8, DP), F32)],
        [pltpu.VMEM((HALO + tm, DP), F32), pltpu.VMEM((tm + HALO, DP), F32),
         pltpu.VMEM((N_GROUPS, GC, GC), F32), pltpu.VMEM((N_HEADS, tm, DH), F32), pltpu.VMEM((tm, LANES), F32)],
        [dzb, wout, hug, hug, attn, hug, hug, wp, scale], "bwd_mid")
    return outs, carried


def _attn_bwd(h16, do, lse, dd, rope, name, others=None, dh=None, rider=None):
    d, n_sub, _ = h16.shape
    n_i = n_sub // BQ
    n_kb = n_sub // KB
    qw = BQ + KB
    final = others is not None
    out_dtype = BF16
    n_cb = 3 * DA // DH

    def nxt(i):
        return jnp.minimum(2 * i + 2, n_kb - 1)

    def body(qc_ref, qn_ref, kc_ref, vc_ref, doc_ref, don_ref, lc_ref, ln_ref, dc_ref, dn_ref,
             c_ref, sn_ref, *rest):
        if final:
            acc4_ref, acc16_ref, _, out_ref, carry_ref, qw_ref, dow_ref, lw_ref, dw_ref, nat_ref = rest
        else:
            out_ref, carry_ref, qw_ref, dow_ref, lw_ref, dw_ref = rest
        i = pl.program_id(1)

        @pl.when(i == 0)
        def _():
            carry_ref[...] = jnp.zeros((KB, DA), F32)

        for win, own, after in ((qw_ref, qc_ref, qn_ref), (dow_ref, doc_ref, don_ref), (lw_ref, lc_ref, ln_ref),
                                (dw_ref, dc_ref, dn_ref)):
            win[0:BQ, :] = own[...]
            win[BQ:qw, :] = after[...]
        if final:
            for n, (dil, acc) in enumerate(((DILATIONS[1], acc4_ref), (DILATIONS[2], acc16_ref))):
                for r in range(dil):
                    for cb in range(n_cb):
                        nat_ref[n, cb, pl.ds(r, BQ // dil, stride=dil), :] = acc[r, :, cb * DH:(cb + 1) * DH].astype(F32)

        a = lax.broadcasted_iota(jnp.int32, (qw, BQ), 0)
        b = lax.broadcasted_iota(jnp.int32, (qw, BQ), 1)
        n_q = jnp.where(i == n_i - 1, BQ, qw)
        mask = (b <= a) & (a <= b + KB) & (a < n_q)
        tok = pl.ds(pl.program_id(0) + d * BQ * i, BQ, stride=d)
        tabs = (c_ref[tok, :], sn_ref[tok, :])
        for h in range(N_HEADS):
            hs = slice(h * DH, (h + 1) * DH)
            hc = slice(h, h + 1)
            q, k, vv, dob = qw_ref[:, hs], kc_ref[:, hs], vc_ref[:, hs], dow_ref[:, hs]
            s = _dot(q, k, NT) * SCALE
            p = jnp.exp(jnp.where(mask, s - lw_ref[:, hc], NEG))
            dp = _dot(dob, vv, NT)
            ds = (p * (dp - dw_ref[:, hc]) * SCALE).astype(BF16)
            dq = _dot(ds, k)
            dk = _dot(ds, q, T_N)
            dv = _dot(p.astype(BF16), dob, T_N)
            dq_lo = dq[0:KB] + carry_ref[:, hs]
            carry_ref[:, hs] = dq[BQ:qw]
            dq_own = _rope_t(jnp.concatenate([dq_lo, dq[KB:BQ]], axis=0), *tabs)
            for base, gv in ((0, dq_own), (DA, _rope_t(dk, *tabs)), (2 * DA, dv)):
                if final:
                    cb = base // DH + h
                    gv = gv + nat_ref[0, cb] + nat_ref[1, cb]
                out_ref[:, base + h * DH:base + (h + 1) * DH] = gv.astype(out_dtype)

    def cur(rows, width, col=0):
        return pl.BlockSpec((None, rows, width), lambda r, i: (r, i, col))

    def nx(width, col=0):
        return pl.BlockSpec((None, KB, width), lambda r, i: (r, nxt(i), col))

    in_specs = [cur(BQ, DA), nx(DA), cur(BQ, DA, 1), cur(BQ, DA, 2), cur(BQ, DA), nx(DA),
                cur(BQ, LANES), nx(LANES), cur(BQ, LANES), nx(LANES)]
    in_specs += [pl.BlockSpec((S, DH), lambda r, i: (0, 0), pipeline_mode=pl.Buffered(1))] * 2
    args = [h16, h16, h16, h16, do, do, lse, lse, dd, dd] + list(rope)
    scratch = [pltpu.VMEM((KB, DA), F32), pltpu.VMEM((qw, DA), BF16), pltpu.VMEM((qw, DA), BF16),
               pltpu.VMEM((qw, LANES), F32), pltpu.VMEM((qw, LANES), F32)]
    if final:
        assert d == 1
        in_specs += [pl.BlockSpec((dil, BQ // dil, 3 * DA), lambda r, i: (0, i, 0)) for dil in DILATIONS[1:]]
        in_specs.append(pl.BlockSpec(memory_space=pl.ANY))
        args += list(others) + [dh[None]]
        scratch.append(pltpu.VMEM((2, n_cb, BQ, DH), F32))
    outs, carried = _call_carrying(
        body, rider, (d, n_i), in_specs, [cur(BQ, 3 * DA)], [SDS((d, n_sub, D_IN if final else 3 * DA), out_dtype)],
        scratch, args, name, {len(args) - 1: 0} if final else {})
    return outs[0] if rider is None else (outs[0], carried)


def _grad_w_out(yt, dzb):
    tn = 256

    def body(yt_ref, dz_ref, o_ref, o16_ref):
        acc = _dot(yt_ref[...], dz_ref[...])
        o_ref[...] = acc
        o16_ref[...] = acc.astype(BF16)

    cols = pl.BlockSpec((D, tn), lambda c: (0, c))
    o, o16 = pl.pallas_call(
        body, name="grad_w_out", grid=(D // tn,),
        in_specs=[pl.BlockSpec((D, S), lambda c: (0, 0), pipeline_mode=pl.Buffered(1)),
                  pl.BlockSpec((S, tn), lambda c: (0, c))],
        out_specs=[cols, cols],
        out_shape=[SDS((D, D), F32), SDS((D, D), BF16)],
        compiler_params=_params(("arbitrary",)),
    )(yt, dzb)
    return o.reshape(N_DEV, OB, D), o16.reshape(N_DEV, OB, D)


def _grad_w_in(xt, dh, order):
    n_send = N_DEV // 2
    n_steps = D_IN // TN
    own0 = n_send * W_PHASES
    sums0 = own0 + W_PHASES
    step_of_send = [W_PHASES * k + W_PHASES + 1 for k in range(n_send)]

    def body(ord_ref, xt_ref, dh_ref, o_ref, o16_ref, r1_ref, s2_ref, st_ref, got_ref,
             piece_sems, send_sems, recv_sems, got_sem):
        c = pl.program_id(0)
        x, y, core, _ = _place()

        def piece(cc):
            blk, sub = ord_ref[cc // W_PHASES], cc % W_PHASES
            return pltpu.make_async_copy(st_ref.at[cc % 2], o16_ref.at[blk, :, pl.ds(sub * TN, TN)], piece_sems.at[cc % 2])

        def send(k):
            return pltpu.make_async_remote_copy(
                src_ref=o16_ref.at[ord_ref[k]], dst_ref=r1_ref.at[k], send_sem=send_sems.at[k], recv_sem=recv_sems.at[k],
                device_id=(x, y, 1 - core), device_id_type=MESH)

        def sibling_part(cc):
            k, sub = cc // W_PHASES - n_send, cc % W_PHASES
            return pltpu.make_async_copy(r1_ref.at[k, :, pl.ds(sub * TN, TN)], got_ref, got_sem)

        @pl.when((c >= 2) & (c < own0 + 2))
        def _():
            piece(c - 2).wait()

        for k in range(n_send):
            pl.when(c == step_of_send[k])(lambda k=k: send(k).start())
        for k in range(1, n_send):
            pl.when(c == own0 + W_PHASES * k)(lambda k=k: send(k).wait_recv())
        pl.when(c >= sums0)(lambda: sibling_part(c).start())

        acc = _dot(xt_ref[...], dh_ref[...])

        @pl.when((c >= own0) & (c < sums0))
        def _():
            o_ref[...] = acc

        @pl.when(c < own0)
        def _():
            st_ref[c % 2] = acc.astype(BF16)
            piece(c).start()

        @pl.when(c >= sums0)
        def _():
            sibling_part(c).wait()
            s2_ref[...] = (acc + got_ref[...].astype(F32)).astype(BF16)

        @pl.when(c == n_steps - 1)
        def _():
            send(0).wait_recv()
            for k in range(n_send):
                send(k).wait_send()

    col = lambda c, ordr: W_PHASES * ordr[c // W_PHASES] + c % W_PHASES
    hbm = pl.BlockSpec(memory_space=pl.ANY)
    sums_blk = lambda c, ordr: (jnp.maximum(c // W_PHASES - (n_send + 1), 0), 0, jnp.where(c >= sums0, c % W_PHASES, 0))
    partial, _, got, sums = pl.pallas_call(
        body, name="grad_w_in",
        grid_spec=pltpu.PrefetchScalarGridSpec(
            num_scalar_prefetch=1, grid=(n_steps,),
            in_specs=[pl.BlockSpec((D, S), lambda c, ordr: (0, 0), pipeline_mode=pl.Buffered(1)),
                      pl.BlockSpec((S, TN), lambda c, ordr: (0, col(c, ordr)))],
            out_specs=[pl.BlockSpec((None, D, TN), lambda c, ordr: (0, 0, jnp.clip(c - own0, 0, W_PHASES - 1))), hbm, hbm,
                       pl.BlockSpec((None, D, TN), sums_blk)],
            scratch_shapes=[pltpu.VMEM((2, D, TN), BF16), pltpu.VMEM((D, TN), BF16), pltpu.SemaphoreType.DMA((2,)),
                            pltpu.SemaphoreType.DMA((n_send,)), pltpu.SemaphoreType.DMA((n_send,)),
                            pltpu.SemaphoreType.DMA(())]),
        out_shape=[SDS((1, D, WB), F32), SDS((N_DEV, D, WB), BF16), SDS((n_send, D, WB), BF16),
                   SDS((n_send - 1, D, WB), BF16)],
        compiler_params=_params(("arbitrary",)),
    )(order, xt, dh)
    return partial, got, sums


def _grad_x(dz, dh, wgs, rider=None):
    tm = 1024

    def body(dz_ref, dh_ref, *rest):
        o_ref = rest[-1]

        @pl.when(pl.program_id(1) == 0)
        def _():
            o_ref[...] = ALPHA * dz_ref[...]

        acc = _dot(dh_ref[:, 0:TN], rest[0][...], NT)
        for s in range(1, W_PHASES):
            acc = acc + _dot(dh_ref[:, s * TN:(s + 1) * TN], rest[s][...], NT)
        o_ref[...] += acc

    outs, carried = _call_carrying(
        body, rider, (S // tm, N_DEV),
        [pl.BlockSpec((tm, D), lambda i, j: (i, 0)), pl.BlockSpec((tm, WB), lambda i, j: (i, j))]
        + [pl.BlockSpec((None, D, TN), lambda i, j: (j, 0, 0))] * W_PHASES,
        [pl.BlockSpec((tm, D), lambda i, j: (i, 0))], [SDS((S, D), F32)], [], [dz, dh] + list(wgs), "grad_x")
    return outs[0], carried


def _place():
    x, y, c = lax.axis_index("x"), lax.axis_index("y"), lax.axis_index("c")
    chips = [(x, y), (1 - x, y), (x, 1 - y), (1 - x, 1 - y)]
    return x, y, c, chips


def _blk(x, y, c):
    return 4 * x + 2 * y + c


def _adamw(w, g, m, v):
    m = ADAM_B1 * m + (1.0 - ADAM_B1) * g
    v = ADAM_B2 * v + (1.0 - ADAM_B2) * (g * g)
    m_hat = m / (1.0 - ADAM_B1 ** ADAM_STEP)
    v_hat = v / (1.0 - ADAM_B2 ** ADAM_STEP)
    delta = -ADAM_LR * (m_hat / (jnp.sqrt(v_hat) + ADAM_EPS) + ADAM_WD * w)
    return delta, m, v


class _Rider:
    def __init__(self, args, in_specs, out_shape, out_specs, scratch, start, finish):
        self.args, self.in_specs, self.out_shape, self.out_specs = args, in_specs, out_shape, out_specs
        self.scratch, self.start, self.finish = scratch, start, finish


def _carry(body, rider, n_in, n_out, first, last):
    if rider is None:
        return body
    r_in, r_out, r_scr = len(rider.args), len(rider.out_shape), len(rider.scratch)

    def carrying(*refs):
        o0 = n_in + r_in
        s0 = o0 + n_out + r_out
        s1 = len(refs) - r_scr
        theirs = (refs[n_in:o0], refs[o0 + n_out:s0], refs[s1:])
        pl.when(first())(lambda: rider.start(*theirs))
        body(*refs[:n_in], *refs[o0:o0 + n_out], *refs[s0:s1])
        pl.when(last())(lambda: rider.finish(*theirs))

    return carrying


def _call_carrying(body, rider, grid, in_specs, out_specs, out_shape, scratch, args, name, aliases=None):
    n_in, n_out = len(in_specs), len(out_specs)
    ids = lambda: [pl.program_id(a) for a in range(len(grid))]
    first = lambda: functools.reduce(jnp.logical_and, [i == 0 for i in ids()])
    last = lambda: functools.reduce(jnp.logical_and, [i == n - 1 for i, n in zip(ids(), grid)])
    if rider is not None:
        in_specs, args = in_specs + rider.in_specs, list(args) + rider.args
        out_specs, out_shape = out_specs + rider.out_specs, out_shape + rider.out_shape
        scratch = scratch + rider.scratch
    outs = pl.pallas_call(
        _carry(body, rider, n_in, n_out, first, last), name=name, grid=grid,
        in_specs=in_specs, out_specs=out_specs, out_shape=out_shape, scratch_shapes=scratch,
        input_output_aliases=aliases or {}, compiler_params=_params(("arbitrary",) * len(grid)),
    )(*args)
    return list(outs[:n_out]), list(outs[n_out:])


def _gather_copy(tensors, send_sems, recv_sems, t, k, block, to, src=None):
    dst = tensors[t][1](_blk(*block))
    return pltpu.make_async_remote_copy(
        src_ref=dst if src is None else src, dst_ref=dst,
        send_sem=send_sems.at[t, k], recv_sem=recv_sems.at[t, k], device_id=to, device_id_type=MESH)


def _gather_start(tensors, send_sems, recv_sems, local_sems):
    x, y, c, chips = _place()
    me, sib = (x, y, c), (x, y, 1 - c)
    for t, (src, dst) in enumerate(tensors):
        pltpu.make_async_copy(src, dst(_blk(*me)), local_sems.at[t]).start()
        _gather_copy(tensors, send_sems, recv_sems, t, 0, me, sib, src).start()
        for j in (1, 2, 3):
            _gather_copy(tensors, send_sems, recv_sems, t, j, me, (*chips[j], c), src).start()


def _gather_finish(tensors, send_sems, recv_sems, local_sems):
    x, y, c, chips = _place()
    me, sib = (x, y, c), (x, y, 1 - c)
    copy = functools.partial(_gather_copy, tensors, send_sems, recv_sems)
    for t in range(len(tensors)):
        for j in (1, 2, 3):
            copy(t, j, (*chips[j], c), me).wait_recv()
            copy(t, 3 + j, (*chips[j], c), sib).start()
    for t, (src, dst) in enumerate(tensors):
        copy(t, 0, sib, me).wait_recv()
        for j in (1, 2, 3):
            copy(t, 3 + j, (*chips[j], 1 - c), me).wait_recv()
        copy(t, 0, me, sib, src).wait_send()
        for j in (1, 2, 3):
            copy(t, j, me, (*chips[j], c), src).wait_send()
            copy(t, 3 + j, (*chips[j], c), sib).wait_send()
        pltpu.make_async_copy(src, dst(_blk(*me)), local_sems.at[t]).wait()


W_PHASES = WB // TN


def _gather_first_rider(w_in_s, w_out_s, w_pool_s):
    shapes = [w_in_s.shape, w_out_s.shape, w_pool_s.shape]

    def keeps(outs, scr):
        return [pltpu.make_async_copy(scr[n], outs[n], scr[6].at[n]) for n in range(3)]

    def tensors(outs, scr):
        return [(scr[0].at[:, pl.ds(0, TN)], lambda b: outs[3].at[b])]

    def start(ins, outs, scr):
        for n in range(3):
            scr[n][...] = ins[n][...].astype(BF16)
        for keep in keeps(outs, scr):
            keep.start()
        _gather_start(tensors(outs, scr), *scr[3:6])

    def finish(ins, outs, scr):
        _gather_finish(tensors(outs, scr), *scr[3:6])
        for keep in keeps(outs, scr):
            keep.wait()

    vmem = pl.BlockSpec(memory_space=pltpu.VMEM)
    hbm = pl.BlockSpec(memory_space=pl.ANY)
    return _Rider(
        args=[w_in_s, w_out_s, w_pool_s], in_specs=[vmem] * 3,
        out_shape=[SDS(sh, BF16) for sh in shapes] + [SDS((N_DEV, D, TN), BF16)], out_specs=[hbm] * 4,
        scratch=[pltpu.VMEM(sh, BF16) for sh in shapes]
        + [pltpu.SemaphoreType.DMA((1, 7)), pltpu.SemaphoreType.DMA((1, 7)), pltpu.SemaphoreType.DMA((1,)),
           pltpu.SemaphoreType.DMA((3,))],
        start=start, finish=finish)


def _gather_w_in_rider(shard16, s):
    def tensors(ins, outs):
        return [(ins[0].at[:, pl.ds(s * TN, TN)], lambda b: outs[0].at[b])]

    hbm = pl.BlockSpec(memory_space=pl.ANY)
    return _Rider(
        args=[shard16], in_specs=[hbm], out_shape=[SDS((N_DEV, D, TN), BF16)], out_specs=[hbm],
        scratch=[pltpu.SemaphoreType.DMA((1, 7)), pltpu.SemaphoreType.DMA((1, 7)), pltpu.SemaphoreType.DMA((1,))],
        start=lambda ins, outs, scr: _gather_start(tensors(ins, outs), *scr),
        finish=lambda ins, outs, scr: _gather_finish(tensors(ins, outs), *scr))


def _gather_small_rider(w_out16, w_pool16):
    def tensors(ins, outs):
        gout_ref, gpool_ref = outs

        def pool_rows(b):
            return gpool_ref.at[:, pl.ds(pl.multiple_of(b * PB, PB), PB), :]

        return [(ins[0], lambda b: gout_ref.at[b]), (ins[1], pool_rows)]

    hbm = pl.BlockSpec(memory_space=pl.ANY)
    return _Rider(
        args=[w_out16, w_pool16], in_specs=[hbm, hbm],
        out_shape=[SDS((N_DEV, OB, D), BF16), SDS((N_GROUPS, GC, GC), BF16)], out_specs=[hbm, hbm],
        scratch=[pltpu.SemaphoreType.DMA((2, 7)), pltpu.SemaphoreType.DMA((2, 7)), pltpu.SemaphoreType.DMA((2,))],
        start=lambda ins, outs, scr: _gather_start(tensors(ins, outs), *scr),
        finish=lambda ins, outs, scr: _gather_finish(tensors(ins, outs), *scr))


def _block_table():
    x, y, c, chips = _place()
    return jnp.stack([_blk(*chip, c) for chip in chips]).astype(jnp.int32)


def _sibling_first_order():
    x, y, c, chips = _place()
    return jnp.stack([_blk(*chip, side) for side in (1 - c, c) for chip in chips]).astype(jnp.int32)


def _exchange_rider(p16):
    _, r_tot, cols = p16.shape

    def copies(ins, outs, scr):
        x, y, c, chips = _place()
        return [pltpu.make_async_remote_copy(
            src_ref=ins[0].at[_blk(*chips[k], 1 - c)], dst_ref=outs[0].at[k],
            send_sem=scr[0].at[k], recv_sem=scr[1].at[k], device_id=(x, y, 1 - c), device_id_type=MESH)
            for k in range(4)]

    def start(ins, outs, scr):
        for cp in copies(ins, outs, scr):
            cp.start()

    def finish(ins, outs, scr):
        for cp in copies(ins, outs, scr):
            cp.wait()

    hbm = pl.BlockSpec(memory_space=pl.ANY)
    return _Rider([p16], [hbm], [SDS((4, r_tot, cols), BF16)], [hbm],
                  [pltpu.SemaphoreType.DMA((4,)), pltpu.SemaphoreType.DMA((4,))], start, finish)


def _chip_sums(name, table, p32, r1, rc):
    _, r_tot, cols = p32.shape

    def body(tbl_ref, p_ref, r_ref, o_ref):
        o_ref[...] = (p_ref[...] + r_ref[...].astype(F32)).astype(BF16)

    return pl.pallas_call(
        body, name=name,
        grid_spec=pltpu.PrefetchScalarGridSpec(
            num_scalar_prefetch=1, grid=(3, r_tot // rc),
            in_specs=[pl.BlockSpec((None, rc, cols), lambda k, ch, tbl: (tbl[k + 1], ch, 0)),
                      pl.BlockSpec((None, rc, cols), lambda k, ch, tbl: (k + 1, ch, 0))],
            out_specs=pl.BlockSpec((None, rc, cols), lambda k, ch, tbl: (k, ch, 0))),
        out_shape=SDS((3, r_tot, cols), BF16),
        compiler_params=_params(("arbitrary", "arbitrary")),
    )(table, p32, r1)


def _stage2_rider(sums, stats=None):
    n_t = len(sums)

    def copies(ins, outs, scr):
        x, y, c, chips = _place()
        out = []
        for t in range(n_t):
            for k in (1, 2, 3):
                out.append(pltpu.make_async_remote_copy(
                    src_ref=ins[t].at[k - 1], dst_ref=outs[t].at[k - 1],
                    send_sem=scr[0].at[t, k - 1], recv_sem=scr[1].at[t, k - 1],
                    device_id=(*chips[k], c), device_id_type=MESH))
        if stats is not None:
            for k in range(1, N_DEV):
                peer = (x ^ ((k >> 2) & 1), y ^ ((k >> 1) & 1), c ^ (k & 1))
                out.append(pltpu.make_async_remote_copy(
                    src_ref=scr[4], dst_ref=outs[n_t].at[_blk(x, y, c)],
                    send_sem=scr[2].at[k - 1], recv_sem=scr[3].at[k - 1], device_id=peer, device_id_type=MESH))
        return out

    def own_rows(outs, scr):
        x, y, c, _ = _place()
        return pltpu.make_async_copy(scr[4], outs[n_t].at[_blk(x, y, c)], scr[5])

    def start(ins, outs, scr):
        if stats is not None:
            scr[4][...] = ins[n_t][...]
            scr[4][3:4, 0:DP] = ins[n_t + 1][0:1, :]
            own_rows(outs, scr).start()
        for cp in copies(ins, outs, scr):
            cp.start()

    def finish(ins, outs, scr):
        for cp in copies(ins, outs, scr):
            cp.wait()
        if stats is not None:
            own_rows(outs, scr).wait()

    vmem = pl.BlockSpec(memory_space=pltpu.VMEM)
    hbm = pl.BlockSpec(memory_space=pl.ANY)
    scratch = [pltpu.SemaphoreType.DMA((n_t, 3)), pltpu.SemaphoreType.DMA((n_t, 3))]
    args, in_specs = list(sums), [hbm] * n_t
    out_shape, out_specs = [SDS(s.shape, BF16) for s in sums], [hbm] * n_t
    if stats is not None:
        scratch += [pltpu.SemaphoreType.DMA((N_DEV - 1,)), pltpu.SemaphoreType.DMA((N_DEV - 1,)),
                    pltpu.VMEM((8, D), F32), pltpu.SemaphoreType.DMA(())]
        args, in_specs = args + list(stats), in_specs + [vmem, vmem]
        out_shape, out_specs = out_shape + [SDS((N_DEV, 8, D), F32)], out_specs + [hbm]
    return _Rider(args, in_specs, out_shape, out_specs, scratch, start, finish)


def _adamw_shard(name, table, p32, r1, r2, w, m, v, rc):
    _, r_tot, cols = p32.shape

    def body(tbl_ref, p_ref, r1_ref, r2_ref, w_ref, m_ref, v_ref, g_ref, d_ref, nm_ref, nv_ref):
        g = p_ref[...] + r1_ref[...].astype(F32)
        for k in range(3):
            g = g + r2_ref[k].astype(F32)
        delta, nm, nv = _adamw(w_ref[...], g, m_ref[...], v_ref[...])
        g_ref[...] = g
        d_ref[...] = delta
        nm_ref[...] = nm
        nv_ref[...] = nv

    rows = pl.BlockSpec((rc, cols), lambda ch, tbl: (ch, 0))
    shard = SDS((r_tot, cols), F32)
    return pl.pallas_call(
        body, name=name,
        grid_spec=pltpu.PrefetchScalarGridSpec(
            num_scalar_prefetch=1, grid=(r_tot // rc,),
            in_specs=[pl.BlockSpec((None, rc, cols), lambda ch, tbl: (tbl[0], ch, 0)),
                      pl.BlockSpec((None, rc, cols), lambda ch, tbl: (0, ch, 0)),
                      pl.BlockSpec((3, rc, cols), lambda ch, tbl: (0, ch, 0)), rows, rows, rows],
            out_specs=[rows, rows, rows, rows]),
        out_shape=[shard, shard, shard, shard],
        compiler_params=_params(("arbitrary",)),
    )(table, p32, r1, r2, w, m, v)


def _replicated_adamw(gathered, gain, bias, scale, m_gain, m_bias, m_scale, v_gain, v_bias, v_scale):
    def body(all_ref, s_ref, g_ref, b_ref, ms_ref, mg_ref, mb_ref, vs_ref, vg_ref, vb_ref, loss_ref, *outs):
        tot = all_ref[0]
        for b in range(1, N_DEV):
            tot = tot + all_ref[b]
        loss_ref[...] = tot[2:3, 0:LANES]
        for n, (row, width, w_r, m_r, v_r) in enumerate(((3, DP, s_ref, ms_ref, vs_ref), (0, D, g_ref, mg_ref, vg_ref),
                                                         (1, D, b_ref, mb_ref, vb_ref))):
            g = tot[row:row + 1, 0:width]
            delta, nm, nv = _adamw(w_r[...], g, m_r[...], v_r[...])
            for out, val in zip(outs[4 * n:4 * n + 4], (g, delta, nm, nv)):
                out[...] = val

    vmem = pl.BlockSpec(memory_space=pltpu.VMEM)
    shapes = [SDS((1, LANES), F32)] + [SDS((1, width), F32) for width in (DP, D, D) for _ in range(4)]
    return pl.pallas_call(
        body, name="replicated_adamw",
        in_specs=[vmem] * 10, out_specs=[vmem] * len(shapes), out_shape=shapes,
        compiler_params=_params(),
    )(gathered, scale, gain, bias, m_scale, m_gain, m_bias, v_scale, v_gain, v_bias)


def kernel(x, w_in, w_pool, pool_scale, w_out, ln_gain, ln_bias, loss_target, m_w_in, m_w_pool, m_pool_scale, m_w_out, m_ln_gain, m_ln_bias, v_w_in, v_w_pool, v_pool_scale, v_w_out, v_ln_gain, v_ln_bias):
    pool_rows = (N_GROUPS * PB, GC)
    x2, target = x[0], loss_target[0]
    table = _block_table()

    (xb, xt, *rope), (shard16, w_out16, w_pool16, wg0) = _prep_x(
        x2, _rope_rates(), _gather_first_rider(w_in[0], w_out[0], w_pool[0]))
    h, (wg1, wg2, wg_out, wg_pool) = _proj_all(xb, rope, wg0, shard16, w_out16, w_pool16)
    wg_out = wg_out.reshape(D, D)
    h16, hug = [h[0][None], h[1], h[2]], h[3]
    fwd = [_attn_fwd(h16[n], f"attn_fwd_d{d}") for n, d in enumerate(DILATIONS)]
    y, yt, attn, lse1, lse4, lse16 = _mix(fwd[0][0][0], fwd[0][1][0], *fwd[1], *fwd[2], hug, wg_pool, pool_scale)
    dz, dzb, stats = _outproj_ln(y, wg_out, x2, target, ln_gain, ln_bias)

    dwout, dwout16 = _grad_w_out(yt, dzb)
    mid, (r1_out,) = _bwd_mid(dzb, wg_out, hug, attn, wg_pool, pool_scale, rider=_exchange_rider(dwout16))
    dh, do1, do4, do16, dd1, dd4, dd16, dwp, dwp16, gps = mid
    s2_out = _chip_sums("rs_out_sums", table, dwout, r1_out, OB)

    do, lse, dd = [do1[None], do4, do16], [lse1[None], lse4, lse16], [dd1[None], dd4, dd16]
    acc4, (r2_out,) = _attn_bwd(h16[1], do[1], lse[1], dd[1], rope, "attn_bwd_d4", rider=_stage2_rider([s2_out]))
    g_out, d_out, nm_out, nv_out = _adamw_shard(
        "adamw_w_out", table, dwout, r1_out, r2_out, w_out[0], m_w_out[0], v_w_out[0], OB // 2)
    acc16, (r1_pool,) = _attn_bwd(h16[2], do[2], lse[2], dd[2], rope, "attn_bwd_d16", rider=_exchange_rider(dwp16))
    s2_pool = _chip_sums("rs_pool_sums", table, dwp, r1_pool, N_GROUPS * PB)
    dh = _attn_bwd(h16[0], do[0], lse[0], dd[0], rope, "attn_bwd_d1", others=(acc4, acc16), dh=dh)[0]

    dwin, r1_in, s2_in = _grad_w_in(xt, dh, _sibling_first_order())
    grad_x, (r2_in, r2_pool, gathered) = _grad_x(
        dz, dh, (wg0, wg1, wg2), rider=_stage2_rider([s2_in, s2_pool], stats=(stats, gps)))
    g_in, d_in, nm_in, nv_in = _adamw_shard(
        "adamw_w_in", jnp.zeros_like(table), dwin, r1_in, r2_in, w_in[0], m_w_in[0], v_w_in[0], 512)
    g_pool, d_pool, nm_pool, nv_pool = _adamw_shard(
        "adamw_w_pool", table, dwp, r1_pool, r2_pool, w_pool[0].reshape(pool_rows), m_w_pool[0].reshape(pool_rows),
        v_w_pool[0].reshape(pool_rows), N_GROUPS * PB)
    loss, *small = _replicated_adamw(gathered, ln_gain, ln_bias, pool_scale, m_ln_gain, m_ln_bias, m_pool_scale,
                                     v_ln_gain, v_ln_bias, v_pool_scale)
    (g_ps, d_ps, nm_ps, nv_ps, g_gain, d_gain, nm_gain, nv_gain, g_bias, d_bias, nm_bias, nv_bias) = small

    shard4 = lambda t: t.reshape(1, N_GROUPS, PB, GC)
    lead = lambda t: t[None]
    return (loss[0, 0], lead(grad_x),
            lead(g_in), shard4(g_pool), g_ps, lead(g_out), g_gain, g_bias,
            lead(d_in), shard4(d_pool), d_ps, lead(d_out), d_gain, d_bias,
            lead(nm_in), shard4(nm_pool), nm_ps, lead(nm_out), nm_gain, nm_bias,
            lead(nv_in), shard4(nv_pool), nv_ps, lead(nv_out), nv_gain, nv_bias)
```

```python
import functools

import jax
import jax.numpy as jnp
from jax import lax
from jax.experimental import pallas as pl
from jax.experimental.pallas import tpu as pltpu

F32 = jnp.float32
BF16 = jnp.bfloat16
SDS = jax.ShapeDtypeStruct
MESH = pl.DeviceIdType.MESH

N_DEV = 8
S = 4096
D = 2048
N_HEADS = 8
DH = 128
DA = N_HEADS * DH
DP = 1024
N_GROUPS = 4
GC = DP // N_GROUPS
POOL_WINDOWS = (2, 4, 8, 16)
HALO = 16
D_IN = 3 * DA + DP + D
WB = D_IN // N_DEV
TN = 256
HW = 3 * DA
OB = D // N_DEV
PB = GC // N_DEV
ROPE_DIM = DH // 4
ROPE_HALF = ROPE_DIM // 2
ROPE_THETA = 500000.0
DILATIONS = (1, 4, 16)
KB = 128
LN_EPS = 1e-5
ALPHA = 2.0 ** 0.25
SCALE = DH ** -0.5
NEG = -1e30
ADAM_LR, ADAM_B1, ADAM_B2, ADAM_EPS, ADAM_WD, ADAM_STEP = 0.001, 0.9, 0.999, 1e-08, 0.01, 10

VMEM_LIMIT_V7X = 61 * 1024 * 1024

NT = (((1,), (1,)), ((), ()))
T_N = (((0,), (0,)), ((), ()))


def _params(sem=None):
    return pltpu.CompilerParams(dimension_semantics=sem, vmem_limit_bytes=VMEM_LIMIT_V7X)


def _dot(a, b, dims=None):
    if dims is None:
        return jnp.dot(a, b, preferred_element_type=F32)
    return lax.dot_general(a, b, dims, preferred_element_type=F32)


def _rope_rates():
    inv_freq = ROPE_THETA ** (-(2.0 * jnp.arange(ROPE_HALF, dtype=F32)) / ROPE_DIM)
    return jnp.concatenate([inv_freq, inv_freq, jnp.zeros((DH - ROPE_DIM,), F32)])[None, :]


def _rope_table_rows(rates, first_row, n):
    pos = (first_row + lax.broadcasted_iota(jnp.int32, (n, DH), 0)).astype(F32)
    lane = lax.broadcasted_iota(jnp.int32, (n, DH), 1)
    ang = pos * rates
    sin = jnp.sin(ang)
    c = jnp.where(lane < ROPE_DIM, jnp.cos(ang), 1.0)
    sn = jnp.where(lane < ROPE_HALF, -sin, jnp.where(lane < ROPE_DIM, sin, 0.0))
    return c, sn


def _rope_partner(t):
    lane = lax.broadcasted_iota(jnp.int32, t.shape, 1)
    return jnp.where(lane < ROPE_HALF, pltpu.roll(t, DH - ROPE_HALF, 1), pltpu.roll(t, ROPE_HALF, 1))


def _rope(t, c, sn):
    return t * c + _rope_partner(t) * sn


def _rope_t(g, c, sn):
    return g * c - _rope_partner(g) * sn


def _prep_x(x, rates, rider):
    tm = 512

    def body(x_ref, f_ref, xb_ref, xt_ref, c_ref, sn_ref):
        xv = x_ref[...]
        xb_ref[...] = xv.astype(BF16)
        xt_ref[...] = xv.T.astype(BF16)
        c_ref[...], sn_ref[...] = _rope_table_rows(f_ref[...], pl.program_id(0) * tm, tm)

    row = lambda i: (i, 0)
    return _call_carrying(
        body, rider, (S // tm,), [pl.BlockSpec((tm, D), row), pl.BlockSpec((1, DH), lambda i: (0, 0))],
        [pl.BlockSpec((tm, D), row), pl.BlockSpec((D, tm), lambda i: (0, i)),
         pl.BlockSpec((tm, DH), row), pl.BlockSpec((tm, DH), row)],
        [SDS((S, D), BF16), SDS((D, S), BF16), SDS((S, DH), F32), SDS((S, DH), F32)], [], [x, rates], "prep_x")


def _residues(slab_ref, idx, d, r, n):
    return slab_ref[(*idx, pl.ds(r, n, stride=d), slice(None))]


def _proj_all(xb, rope, wg0, shard16, w_out16, w_pool16):
    n_heads = TN // DH
    n16 = HW // TN
    j16 = n16 // W_PHASES
    n_rope = 2 * DA // TN
    once = pl.Buffered(1)
    riders = [_gather_w_in_rider(shard16, s) for s in range(1, W_PHASES)] + [_gather_small_rider(w_out16, w_pool16)]
    last_j = N_DEV - 1

    def body(*refs):
        x_ref, c_ref, sn_ref, g0_ref = refs[:4]
        pos = 4
        r_ins, r_outs, r_scrs = [], [], []
        for r in riders:
            r_ins.append(refs[pos:pos + len(r.args)])
            pos += len(r.args)
        h16_ref, h4_ref, h16r_ref, hug_ref = refs[pos:pos + 4]
        pos += 4
        for r in riders:
            r_outs.append(refs[pos:pos + len(r.out_shape)])
            pos += len(r.out_shape)
        slab_ref, res_ref, w_ref, w_sems = refs[pos:pos + 4]
        pos += 4
        for r in riders:
            r_scrs.append(refs[pos:pos + len(r.scratch)])
            pos += len(r.scratch)
        gathered = [g0_ref] + [r_outs[s][0] for s in range(W_PHASES - 1)]
        s, j = pl.program_id(0), pl.program_id(1)
        d4, d16 = DILATIONS[1], DILATIONS[2]

        def fetch(ph, blk, slot):
            return pltpu.make_async_copy(gathered[ph].at[blk], w_ref.at[slot], w_sems.at[slot])

        pl.when((s == 0) & (j == 0))(lambda: fetch(0, 0, 0).start())
        for ph in range(W_PHASES):
            pl.when((s == ph) & (j == 0))(lambda ph=ph: riders[ph].start(r_ins[ph], r_outs[ph], r_scrs[ph]))
            pl.when(s == ph)(lambda ph=ph: fetch(ph, j, j % 2).wait())
            pl.when((s == ph) & (j == last_j))(lambda ph=ph: riders[ph].finish(r_ins[ph], r_outs[ph], r_scrs[ph]))
            pl.when((s == ph) & (j < last_j))(lambda ph=ph: fetch(ph, j + 1, (j + 1) % 2).start())
            if ph + 1 < W_PHASES:
                pl.when((s == ph) & (j == last_j))(lambda ph=ph: fetch(ph + 1, 0, 0).start())

        acc = _dot(x_ref[...], w_ref[j % 2])

        def heads(with_rope):
            for hh in range(n_heads):
                hs = slice(hh * DH, (hh + 1) * DH)
                t = _rope(acc[:, hs], c_ref[...], sn_ref[...]) if with_rope else acc[:, hs]
                h16_ref[:, hs] = t.astype(BF16)
                slab_ref[...] = t
                for q in range(d4):
                    t4 = _residues(slab_ref, (), d4, q, S // d4)
                    h4_ref[q, :, hs] = t4.astype(BF16)
                    res_ref[...] = t4
                    for m in range(d16 // d4):
                        h16r_ref[d4 * m + q, :, hs] = _residues(res_ref, (), d4, m, S // d16).astype(BF16)

        pl.when(W_PHASES * j + s < n_rope)(lambda: heads(True))
        pl.when((W_PHASES * j + s >= n_rope) & (j < j16))(lambda: heads(False))

        @pl.when(j >= j16)
        def _():
            hug_ref[...] = acc.astype(BF16)

    col16 = lambda s, j: W_PHASES * jnp.minimum(j, j16 - 1) + s
    colug = lambda s, j: W_PHASES * jnp.maximum(j, j16) + s - n16
    hbm = pl.BlockSpec(memory_space=pl.ANY)
    resident = lambda shape: pl.BlockSpec(shape, lambda s, j: (0, 0), pipeline_mode=once)
    in_specs = [resident((S, D)), resident((S, DH)), resident((S, DH)), hbm]
    args = [xb, *rope, wg0]
    out_specs = [pl.BlockSpec((S, TN), lambda s, j: (0, col16(s, j)))]
    out_specs += [pl.BlockSpec((d, S // d, TN), lambda s, j: (0, 0, col16(s, j))) for d in DILATIONS[1:]]
    out_specs += [pl.BlockSpec((S, TN), lambda s, j: (0, colug(s, j)))]
    out_shape = [SDS((S, HW), BF16)] + [SDS((d, S // d, HW), BF16) for d in DILATIONS[1:]] + [SDS((S, HW), BF16)]
    scratch = [pltpu.VMEM((S, DH), F32), pltpu.VMEM((S // DILATIONS[1], DH), F32), pltpu.VMEM((2, D, TN), BF16),
               pltpu.SemaphoreType.DMA((2,))]
    for r in riders:
        in_specs, args = in_specs + r.in_specs, args + r.args
        out_specs, out_shape = out_specs + r.out_specs, out_shape + r.out_shape
        scratch = scratch + r.scratch
    outs = pl.pallas_call(
        body, name="proj_all", grid=(W_PHASES, N_DEV),
        in_specs=in_specs, out_specs=out_specs, out_shape=out_shape, scratch_shapes=scratch,
        compiler_params=_params(("arbitrary", "arbitrary")),
    )(*args)
    return list(outs[:4]), list(outs[4:])


BQ = 2 * KB
LANES = 128


def _to_lane(acc, col, h):
    lane = lax.broadcasted_iota(jnp.int32, acc.shape, 1)
    return jnp.where(lane == h, col, acc)


def _attn_fwd(h16, name):
    d, n_sub, _ = h16.shape
    n_i = n_sub // BQ
    kw = KB + BQ

    def prev(i):
        return jnp.maximum(2 * i - 1, 0)

    def body(q_ref, kc_ref, kp_ref, vc_ref, vp_ref, o_ref, l_ref, kw_ref, vw_ref):
        i = pl.program_id(1)
        kw_ref[0:KB, :] = kp_ref[...]
        kw_ref[KB:kw, :] = kc_ref[...]
        vw_ref[0:KB, :] = vp_ref[...]
        vw_ref[KB:kw, :] = vc_ref[...]
        a = lax.broadcasted_iota(jnp.int32, (KB, 2 * KB), 0)
        b = lax.broadcasted_iota(jnp.int32, (KB, 2 * KB), 1)
        band = (b >= a) & (b <= a + KB)
        first_key = jnp.where(i == 0, KB, 0)
        masks = (band & (b >= first_key), band)
        for half in range(2):
            rs = slice(half * KB, (half + 1) * KB)
            ks = slice(half * KB, (half + 2) * KB)
            lse = jnp.zeros((KB, LANES), F32)
            for h in range(N_HEADS):
                hs = slice(h * DH, (h + 1) * DH)
                s = jnp.where(masks[half], _dot(q_ref[rs, hs], kw_ref[ks, hs], NT) * SCALE, NEG)
                m = jnp.max(s, axis=1, keepdims=True)
                p = jnp.exp(s - m)
                den = jnp.sum(p, axis=1, keepdims=True)
                o_ref[rs, hs] = (_dot(p.astype(BF16), vw_ref[ks, hs]) / den).astype(BF16)
                lse = _to_lane(lse, m + jnp.log(den), h)
            l_ref[rs, :] = lse

    return pl.pallas_call(
        body, name=name, grid=(d, n_i),
        in_specs=[pl.BlockSpec((None, BQ, DA), lambda r, i: (r, i, 0)),
                  pl.BlockSpec((None, BQ, DA), lambda r, i: (r, i, 1)),
                  pl.BlockSpec((None, KB, DA), lambda r, i: (r, prev(i), 1)),
                  pl.BlockSpec((None, BQ, DA), lambda r, i: (r, i, 2)),
                  pl.BlockSpec((None, KB, DA), lambda r, i: (r, prev(i), 2))],
        out_specs=[pl.BlockSpec((None, BQ, DA), lambda r, i: (r, i, 0)),
                   pl.BlockSpec((None, BQ, LANES), lambda r, i: (r, i, 0))],
        out_shape=[SDS((d, n_sub, DA), BF16), SDS((d, n_sub, LANES), F32)],
        scratch_shapes=[pltpu.VMEM((kw, DA), BF16), pltpu.VMEM((kw, DA), BF16)],
        compiler_params=_params(("arbitrary", "arbitrary")),
    )(h16, h16, h16, h16, h16)


def _pooled(ext_ref, g, rows, tm):
    w = POOL_WINDOWS[g]
    cs = slice(g * GC, (g + 1) * GC)
    cur = ext_ref[HALO:HALO + tm, cs]
    win = cur
    for j in range(1, w):
        win = win + ext_ref[HALO - j:HALO - j + tm, cs]
    cnt = jnp.minimum(rows + 1, w).astype(F32)
    return win / cnt - cur, cnt


def _fill_ext(ext_ref, u_ref, uh_ref, blk, tm):
    @pl.when(blk == 0)
    def _():
        ext_ref[0:HALO, :] = jnp.zeros((HALO, DP), F32)

    @pl.when(blk > 0)
    def _():
        ext_ref[0:HALO, :] = uh_ref[...].astype(F32)

    ext_ref[HALO:HALO + tm, :] = u_ref[...].astype(F32)


def _residue_specs(tm, width):
    return [pl.BlockSpec((d, tm // d, width), lambda i: (0, i, 0)) for d in DILATIONS[1:]]


def _mix(o1, l1, o4, l4, o16, l16, hug, wp, scale):
    tm = 256
    n_slab = N_HEADS + 1

    def body(o1r, l1r, o4r, l4r, o16r, l16r, u_ref, uh_ref, ga_ref, gp_ref, wp_ref, sc_ref,
             y_ref, yt_ref, attn_ref, lse_ref, lse4_ref, lse16_ref, ext_ref, ys_ref, nat_ref, ls_ref):
        i = pl.program_id(0)
        for n, (d, o_r, l_r) in enumerate(((DILATIONS[1], o4r, l4r), (DILATIONS[2], o16r, l16r))):
            for r in range(d):
                rows = pl.ds(r, tm // d, stride=d)
                for h in range(N_HEADS):
                    nat_ref[n, h, rows, :] = o_r[r, :, h * DH:(h + 1) * DH].astype(F32)
                nat_ref[n, N_HEADS, rows, :] = l_r[r]
        la, lb, lc = l1r[...], nat_ref[0, N_HEADS], nat_ref[1, N_HEADS]
        mx = jnp.maximum(jnp.maximum(la, lb), lc)
        ea, eb, ec = jnp.exp(la - mx), jnp.exp(lb - mx), jnp.exp(lc - mx)
        z = ea + eb + ec
        wa, wb, wc = ea / z, eb / z, ec / z
        lse = mx + jnp.log(z)
        lse_ref[...] = lse
        ls_ref[...] = lse
        for d, out in ((DILATIONS[1], lse4_ref), (DILATIONS[2], lse16_ref)):
            for r in range(d):
                out[r] = ls_ref[pl.ds(r, tm // d, stride=d), :]
        for h in range(N_HEADS):
            hs = slice(h * DH, (h + 1) * DH)
            hc = slice(h, h + 1)
            attn = wa[:, hc] * o1r[:, hs].astype(F32) + wb[:, hc] * nat_ref[0, h] + wc[:, hc] * nat_ref[1, h]
            attn_ref[:, hs] = attn.astype(BF16)
            gt = ga_ref[:, hs].astype(F32)
            ys_ref[:, hs] = attn * (gt * jax.nn.sigmoid(gt))

        _fill_ext(ext_ref, u_ref, uh_ref, i, tm)
        rows = i * tm + lax.broadcasted_iota(jnp.int32, (tm, 1), 0)
        for g in range(N_GROUPS):
            cs = slice(g * GC, (g + 1) * GC)
            gs = slice(DA + g * GC, DA + (g + 1) * GC)
            pooled, _ = _pooled(ext_ref, g, rows, tm)
            po = _dot(pooled.astype(BF16), wp_ref[g]) * sc_ref[:, cs]
            gt = gp_ref[:, cs].astype(F32)
            ys_ref[:, gs] = po * (gt * jax.nn.sigmoid(gt))
        yv = ys_ref[...]
        y_ref[...] = yv.astype(BF16)
        yt_ref[...] = yv.T.astype(BF16)

    row = lambda i: (i, 0)
    blk = pl.BlockSpec((tm, DA), row)
    lanes = pl.BlockSpec((tm, LANES), row)
    o_res, l_res = _residue_specs(tm, DA), _residue_specs(tm, LANES)
    return pl.pallas_call(
        body, name="mix", grid=(S // tm,),
        in_specs=[blk, lanes, o_res[0], l_res[0], o_res[1], l_res[1],
                  pl.BlockSpec((tm, DP), row),
                  pl.BlockSpec((HALO, DP), lambda i: (jnp.maximum(i * (tm // HALO) - 1, 0), 0)),
                  pl.BlockSpec((tm, DA), lambda i: (i, 1)), pl.BlockSpec((tm, DP), lambda i: (i, 2)),
                  pl.BlockSpec((N_GROUPS, GC, GC), lambda i: (0, 0, 0)),
                  pl.BlockSpec((1, DP), lambda i: (0, 0))],
        out_specs=[pl.BlockSpec((tm, D), row), pl.BlockSpec((D, tm), lambda i: (0, i)), blk, lanes] + l_res,
        out_shape=[SDS((S, D), BF16), SDS((D, S), BF16), SDS((S, DA), BF16), SDS((S, LANES), F32)]
        + [SDS((d, S // d, LANES), F32) for d in DILATIONS[1:]],
        scratch_shapes=[pltpu.VMEM((HALO + tm, DP), F32), pltpu.VMEM((tm, D), F32),
                        pltpu.VMEM((2, n_slab, tm, DH), F32), pltpu.VMEM((tm, LANES), F32)],
        compiler_params=_params(("arbitrary",)),
    )(o1, l1, o4, l4, o16, l16, hug, hug, hug, hug, wp, scale)


def _outproj_ln(y, wout, x, target, gain, bias):
    tm = 512
    te = 128

    def body(y_ref, w_ref, x_ref, t_ref, g_ref, b_ref, dz_ref, dzb_ref, st_ref, out_ref):
        i = pl.program_id(0)

        @pl.when(i == 0)
        def _():
            st_ref[...] = jnp.zeros((8, D), F32)

        out_ref[...] = _dot(y_ref[...], w_ref[...])
        gn = g_ref[...]
        for e in range(tm // te):
            rs = slice(e * te, (e + 1) * te)
            z = ALPHA * x_ref[rs, :] + out_ref[rs, :]
            mu = jnp.mean(z, axis=1, keepdims=True)
            zc = z - mu
            var = jnp.mean(zc * zc, axis=1, keepdims=True)
            rstd = lax.rsqrt(var + LN_EPS)
            xhat = zc * rstd
            diff = xhat * gn + b_ref[...] - t_ref[rs, :]
            dyln = diff / D
            st_ref[0:1, :] += jnp.sum(dyln * xhat, axis=0, keepdims=True)
            st_ref[1:2, :] += jnp.sum(dyln, axis=0, keepdims=True)
            row_loss = jnp.sum(diff * diff, axis=1, keepdims=True) / D
            st_ref[2:3, :] += jnp.broadcast_to(0.5 * jnp.sum(row_loss, axis=0, keepdims=True), (1, D))
            dxh = dyln * gn
            m1 = jnp.mean(dxh, axis=1, keepdims=True)
            m2 = jnp.mean(dxh * xhat, axis=1, keepdims=True)
            dz = rstd * (dxh - m1 - xhat * m2)
            dz_ref[rs, :] = dz
            dzb_ref[rs, :] = dz.astype(BF16)

    row = lambda i: (i, 0)
    const = lambda i: (0, 0)
    return pl.pallas_call(
        body, name="outproj_ln", grid=(S // tm,),
        in_specs=[pl.BlockSpec((tm, D), row),
                  pl.BlockSpec((D, D), const, pipeline_mode=pl.Buffered(1)),
                  pl.BlockSpec((tm, D), row), pl.BlockSpec((tm, D), row),
                  pl.BlockSpec((1, D), const), pl.BlockSpec((1, D), const)],
        out_specs=[pl.BlockSpec((tm, D), row), pl.BlockSpec((tm, D), row), pl.BlockSpec((8, D), const)],
        out_shape=[SDS((S, D), F32), SDS((S, D), BF16), SDS((8, D), F32)],
        scratch_shapes=[pltpu.VMEM((tm, D), F32)],
        compiler_params=_params(("arbitrary",)),
    )(y, wout, x, target, gain, bias)


def _bwd_mid(dzb, wout, hug, attn, wp, scale, rider=None):
    tm = 256
    n = S // tm

    def body(dz_ref, w_ref, ga_ref, gp_ref, at_ref, u_ref, uh_ref, wp_ref, sc_ref,
             dh_ref, do_ref, do4_ref, do16_ref, dd_ref, dd4_ref, dd16_ref, dwp_ref, dwp16_ref, gps_ref,
             ext_ref, eext_ref, acc_ref, nat_ref, ds_ref):
        i = pl.program_id(0)
        ib = n - 1 - i

        @pl.when(i == 0)
        def _():
            eext_ref[tm:tm + HALO, :] = jnp.zeros((HALO, DP), F32)
            acc_ref[...] = jnp.zeros((N_GROUPS, GC, GC), F32)
            gps_ref[...] = jnp.zeros((8, DP), F32)

        dy = _dot(dz_ref[...], w_ref[...], NT)

        def through_gate(dy_part, gt):
            sg = jax.nn.sigmoid(gt)
            return dy_part * (gt * sg), dy_part * (sg * (1.0 + gt * (1.0 - sg)))

        dat, dyg_a = through_gate(dy[:, 0:DA], ga_ref[...].astype(F32))
        dmix_p, dyg_p = through_gate(dy[:, DA:D], gp_ref[...].astype(F32))

        at = at_ref[...].astype(F32)
        do_ref[...] = dat.astype(BF16)
        dh_ref[:, DP:DP + DA] = (dyg_a * at).astype(BF16)
        prod = dat * at
        dd = jnp.zeros((tm, LANES), F32)
        for h in range(N_HEADS):
            hs = slice(h * DH, (h + 1) * DH)
            dd = _to_lane(dd, jnp.sum(prod[:, hs], axis=1, keepdims=True), h)
            nat_ref[h] = dat[:, hs]
        dd_ref[...] = dd
        ds_ref[...] = dd
        for d, do_out, dd_out in ((DILATIONS[1], do4_ref, dd4_ref), (DILATIONS[2], do16_ref, dd16_ref)):
            for r in range(d):
                dd_out[r] = _residues(ds_ref, (), d, r, tm // d)
                for h in range(N_HEADS):
                    do_out[r, :, h * DH:(h + 1) * DH] = _residues(nat_ref, (h,), d, r, tm // d).astype(BF16)

        _fill_ext(ext_ref, u_ref, uh_ref, ib, tm)
        rows = ib * tm + lax.broadcasted_iota(jnp.int32, (tm, 1), 0)
        for g in range(N_GROUPS):
            w = POOL_WINDOWS[g]
            cs = slice(g * GC, (g + 1) * GC)
            pooled, cnt = _pooled(ext_ref, g, rows, tm)
            pre = _dot(pooled.astype(BF16), wp_ref[g])
            sc = sc_ref[:, cs]
            dpo = dmix_p[:, cs]
            gps_ref[0:1, cs] += jnp.sum(dpo * pre, axis=0, keepdims=True)
            dh_ref[:, DP + DA + g * GC:DP + DA + (g + 1) * GC] = (dyg_p[:, cs] * (pre * sc)).astype(BF16)
            dpre = (dpo * sc).astype(BF16)
            acc_ref[g] += _dot(pooled.T.astype(BF16), dpre)
            dpooled = _dot(dpre, wp_ref[g], NT)
            eext_ref[0:tm, cs] = dpooled / cnt
            du = eext_ref[0:tm, cs]
            for j in range(1, w):
                du = du + eext_ref[j:j + tm, cs]
            dh_ref[:, cs] = (du - dpooled).astype(BF16)
        eext_ref[tm:tm + HALO, :] = eext_ref[0:HALO, :]

        @pl.when(i == n - 1)
        def _():
            for j in range(N_DEV):
                for g in range(N_GROUPS):
                    blk = acc_ref[g, j * PB:(j + 1) * PB, :]
                    dwp_ref[j, g * PB:(g + 1) * PB, :] = blk
                    dwp16_ref[j, g * PB:(g + 1) * PB, :] = blk.astype(BF16)

    rev = lambda i: (n - 1 - i, 0)
    const = lambda i: (0, 0)
    res = lambda width: [pl.BlockSpec((d, tm // d, width), lambda i: (0, n - 1 - i, 0)) for d in DILATIONS[1:]]
    pool_blocks = pl.BlockSpec((N_DEV, N_GROUPS * PB, GC), lambda i: (0, 0, 0))
    outs, carried = _call_carrying(
        body, rider, (n,),
        [pl.BlockSpec((tm, D), rev),
         pl.BlockSpec((D, D), const, pipeline_mode=pl.Buffered(1)),
         pl.BlockSpec((tm, DA), lambda i: (n - 1 - i, 1)), pl.BlockSpec((tm, DP), lambda i: (n - 1 - i, 2)),
         pl.BlockSpec((tm, DA), rev), pl.BlockSpec((tm, DP), rev),
         pl.BlockSpec((HALO, DP), lambda i: (jnp.maximum((n - 1 - i) * (tm // HALO) - 1, 0), 0)),
         pl.BlockSpec((N_GROUPS, GC, GC), lambda i: (0, 0, 0)),
         pl.BlockSpec((1, DP), const)],
        [pl.BlockSpec((tm, DP + D), lambda i: (n - 1 - i, 1)), pl.BlockSpec((tm, DA), rev)] + res(DA)
        + [pl.BlockSpec((tm, LANES), rev)] + res(LANES) + [pool_blocks, pool_blocks, pl.BlockSpec((8, DP), const)],
        [SDS((S, D_IN), BF16), SDS((S, DA), BF16)] + [SDS((d, S // d, DA), BF16) for d in DILATIONS[1:]]
        + [SDS((S, LANES), F32)] + [SDS((d, S // d, LANES), F32) for d in DILATIONS[1:]]
        + [SDS((N_DEV, N_GROUPS * PB, GC), F32), SDS((N_DEV, N_GROUPS * PB, GC), BF16), SDS((8, DP), F32)],
        [pltpu.VMEM((HALO + tm, DP), F32), pltpu.VMEM((tm + HALO, DP), F32),
         pltpu.VMEM((N_GROUPS, GC, GC), F32), pltpu.VMEM((N_HEADS, tm, DH), F32), pltpu.VMEM((tm, LANES), F32)],
        [dzb, wout, hug, hug, attn, hug, hug, wp, scale], "bwd_mid")
    return outs, carried


def _attn_bwd(h16, do, lse, dd, rope, name, others=None, dh=None, rider=None):
    d, n_sub, _ = h16.shape
    n_i = n_sub // BQ
    n_kb = n_sub // KB
    qw = BQ + KB
    final = others is not None
    out_dtype = BF16
    n_cb = 3 * DA // DH
    n_grp = 2

    def nxt(i):
        return jnp.minimum(2 * i + 2, n_kb - 1)

    def body(qc_ref, qn_ref, kc_ref, vc_ref, doc_ref, don_ref, lc_ref, ln_ref, dc_ref, dn_ref,
             c_ref, sn_ref, *rest):
        if final:
            acc4_ref, acc16_ref, _, out_ref, carry_ref, qw_ref, dow_ref, lw_ref, dw_ref, nat_ref = rest
        else:
            out_ref, carry_ref, qw_ref, dow_ref, lw_ref, dw_ref = rest
        i, grp = pl.program_id(1), pl.program_id(2)

        @pl.when((i == 0) & (grp == 0))
        def _():
            carry_ref[...] = jnp.zeros((KB, DA), F32)

        @pl.when(grp == 0)
        def _():
            for win, own, after in ((qw_ref, qc_ref, qn_ref), (dow_ref, doc_ref, don_ref), (lw_ref, lc_ref, ln_ref),
                                    (dw_ref, dc_ref, dn_ref)):
                win[0:BQ, :] = own[...]
                win[BQ:qw, :] = after[...]
            if final:
                for n, (dil, acc) in enumerate(((DILATIONS[1], acc4_ref), (DILATIONS[2], acc16_ref))):
                    for r in range(dil):
                        for cb in range(n_cb):
                            nat_ref[n, cb, pl.ds(r, BQ // dil, stride=dil), :] = (
                                acc[r, :, cb * DH:(cb + 1) * DH].astype(F32))

        a = lax.broadcasted_iota(jnp.int32, (qw, BQ), 0)
        b = lax.broadcasted_iota(jnp.int32, (qw, BQ), 1)
        n_q = jnp.where(i == n_i - 1, BQ, qw)
        mask = (b <= a) & (a <= b + KB) & (a < n_q)
        tok = pl.ds(pl.program_id(0) + d * BQ * i, BQ, stride=d)
        tabs = (c_ref[tok, :], sn_ref[tok, :])
        def head(h):
            hs = slice(h * DH, (h + 1) * DH)
            hc = slice(h, h + 1)
            q, k, vv, dob = qw_ref[:, hs], kc_ref[:, hs], vc_ref[:, hs], dow_ref[:, hs]
            s = _dot(q, k, NT) * SCALE
            p = jnp.exp(jnp.where(mask, s - lw_ref[:, hc], NEG))
            dp = _dot(dob, vv, NT)
            ds = (p * (dp - dw_ref[:, hc]) * SCALE).astype(BF16)
            dq = _dot(ds, k)
            dk = _dot(ds, q, T_N)
            dv = _dot(p.astype(BF16), dob, T_N)
            dq_lo = dq[0:KB] + carry_ref[:, hs]
            carry_ref[:, hs] = dq[BQ:qw]
            dq_own = _rope_t(jnp.concatenate([dq_lo, dq[KB:BQ]], axis=0), *tabs)
            for base, gv in ((0, dq_own), (DA, _rope_t(dk, *tabs)), (2 * DA, dv)):
                if final:
                    cb = base // DH + h
                    gv = gv + nat_ref[0, cb] + nat_ref[1, cb]
                out_ref[:, base + h * DH:base + (h + 1) * DH] = gv.astype(out_dtype)

        def heads_of(gg):
            for h in range(gg * N_HEADS // n_grp, (gg + 1) * N_HEADS // n_grp):
                head(h)

        for gg in range(n_grp):
            pl.when(grp == gg)(functools.partial(heads_of, gg))

    def cur(rows, width, col=0):
        return pl.BlockSpec((None, rows, width), lambda r, i, g: (r, i, col))

    def nx(width, col=0):
        return pl.BlockSpec((None, KB, width), lambda r, i, g: (r, nxt(i), col))

    in_specs = [cur(BQ, DA), nx(DA), cur(BQ, DA, 1), cur(BQ, DA, 2), cur(BQ, DA), nx(DA),
                cur(BQ, LANES), nx(LANES), cur(BQ, LANES), nx(LANES)]
    in_specs += [pl.BlockSpec((S, DH), lambda r, i, g: (0, 0), pipeline_mode=pl.Buffered(1))] * 2
    args = [h16, h16, h16, h16, do, do, lse, lse, dd, dd] + list(rope)
    scratch = [pltpu.VMEM((KB, DA), F32), pltpu.VMEM((qw, DA), BF16), pltpu.VMEM((qw, DA), BF16),
               pltpu.VMEM((qw, LANES), F32), pltpu.VMEM((qw, LANES), F32)]
    if final:
        assert d == 1
        in_specs += [pl.BlockSpec((dil, BQ // dil, 3 * DA), lambda r, i, g: (0, i, 0)) for dil in DILATIONS[1:]]
        in_specs.append(pl.BlockSpec(memory_space=pl.ANY))
        args += list(others) + [dh[None]]
        scratch.append(pltpu.VMEM((2, n_cb, BQ, DH), F32))
    outs, carried = _call_carrying(
        body, rider, (d, n_i, n_grp), in_specs, [cur(BQ, 3 * DA)],
        [SDS((d, n_sub, D_IN if final else 3 * DA), out_dtype)],
        scratch, args, name, {len(args) - 1: 0} if final else {})
    return outs[0] if rider is None else (outs[0], carried)


def _grad_w_out(yt, dzb):
    tn = 256

    def body(yt_ref, dz_ref, o_ref, o16_ref):
        acc = _dot(yt_ref[...], dz_ref[...])
        o_ref[...] = acc
        o16_ref[...] = acc.astype(BF16)

    cols = pl.BlockSpec((D, tn), lambda c: (0, c))
    o, o16 = pl.pallas_call(
        body, name="grad_w_out", grid=(D // tn,),
        in_specs=[pl.BlockSpec((D, S), lambda c: (0, 0), pipeline_mode=pl.Buffered(1)),
                  pl.BlockSpec((S, tn), lambda c: (0, c))],
        out_specs=[cols, cols],
        out_shape=[SDS((D, D), F32), SDS((D, D), BF16)],
        compiler_params=_params(("arbitrary",)),
    )(yt, dzb)
    return o.reshape(N_DEV, OB, D), o16.reshape(N_DEV, OB, D)


def _grad_w_in(xt, dh, order):
    n_send = N_DEV // 2
    n_steps = D_IN // TN
    own0 = n_send * W_PHASES
    sums0 = own0 + W_PHASES
    step_of_send = [W_PHASES * k + W_PHASES + 1 for k in range(n_send)]

    def body(ord_ref, xt_ref, dh_ref, o_ref, o16_ref, r1_ref, s2_ref, st_ref, got_ref,
             piece_sems, send_sems, recv_sems, got_sem):
        c = pl.program_id(0)
        x, y, core, _ = _place()

        def piece(cc):
            blk, sub = ord_ref[cc // W_PHASES], cc % W_PHASES
            return pltpu.make_async_copy(st_ref.at[cc % 2], o16_ref.at[blk, :, pl.ds(sub * TN, TN)], piece_sems.at[cc % 2])

        def send(k):
            return pltpu.make_async_remote_copy(
                src_ref=o16_ref.at[ord_ref[k]], dst_ref=r1_ref.at[k], send_sem=send_sems.at[k], recv_sem=recv_sems.at[k],
                device_id=(x, y, 1 - core), device_id_type=MESH)

        def sibling_part(cc):
            k, sub = cc // W_PHASES - n_send, cc % W_PHASES
            return pltpu.make_async_copy(r1_ref.at[k, :, pl.ds(sub * TN, TN)], got_ref, got_sem)

        @pl.when((c >= 2) & (c < own0 + 2))
        def _():
            piece(c - 2).wait()

        for k in range(n_send):
            pl.when(c == step_of_send[k])(lambda k=k: send(k).start())
        for k in range(1, n_send):
            pl.when(c == own0 + W_PHASES * k)(lambda k=k: send(k).wait_recv())
        pl.when(c >= sums0)(lambda: sibling_part(c).start())

        acc = _dot(xt_ref[...], dh_ref[...])

        @pl.when((c >= own0) & (c < sums0))
        def _():
            o_ref[...] = acc

        @pl.when(c < own0)
        def _():
            st_ref[c % 2] = acc.astype(BF16)
            piece(c).start()

        @pl.when(c >= sums0)
        def _():
            sibling_part(c).wait()
            s2_ref[...] = (acc + got_ref[...].astype(F32)).astype(BF16)

        @pl.when(c == n_steps - 1)
        def _():
            send(0).wait_recv()
            for k in range(n_send):
                send(k).wait_send()

    col = lambda c, ordr: W_PHASES * ordr[c // W_PHASES] + c % W_PHASES
    hbm = pl.BlockSpec(memory_space=pl.ANY)
    sums_blk = lambda c, ordr: (jnp.maximum(c // W_PHASES - (n_send + 1), 0), 0, jnp.where(c >= sums0, c % W_PHASES, 0))
    partial, _, got, sums = pl.pallas_call(
        body, name="grad_w_in",
        grid_spec=pltpu.PrefetchScalarGridSpec(
            num_scalar_prefetch=1, grid=(n_steps,),
            in_specs=[pl.BlockSpec((D, S), lambda c, ordr: (0, 0), pipeline_mode=pl.Buffered(1)),
                      pl.BlockSpec((S, TN), lambda c, ordr: (0, col(c, ordr)))],
            out_specs=[pl.BlockSpec((None, D, TN), lambda c, ordr: (0, 0, jnp.clip(c - own0, 0, W_PHASES - 1))), hbm, hbm,
                       pl.BlockSpec((None, D, TN), sums_blk)],
            scratch_shapes=[pltpu.VMEM((2, D, TN), BF16), pltpu.VMEM((D, TN), BF16), pltpu.SemaphoreType.DMA((2,)),
                            pltpu.SemaphoreType.DMA((n_send,)), pltpu.SemaphoreType.DMA((n_send,)),
                            pltpu.SemaphoreType.DMA(())]),
        out_shape=[SDS((1, D, WB), F32), SDS((N_DEV, D, WB), BF16), SDS((n_send, D, WB), BF16),
                   SDS((n_send - 1, D, WB), BF16)],
        compiler_params=_params(("arbitrary",)),
    )(order, xt, dh)
    return partial, got, sums


def _grad_x(dz, dh, wgs, rider=None):
    tm = 1024

    def body(dz_ref, dh_ref, *rest):
        o_ref = rest[-1]

        @pl.when(pl.program_id(1) == 0)
        def _():
            o_ref[...] = ALPHA * dz_ref[...]

        acc = _dot(dh_ref[:, 0:TN], rest[0][...], NT)
        for s in range(1, W_PHASES):
            acc = acc + _dot(dh_ref[:, s * TN:(s + 1) * TN], rest[s][...], NT)
        o_ref[...] += acc

    outs, carried = _call_carrying(
        body, rider, (S // tm, N_DEV),
        [pl.BlockSpec((tm, D), lambda i, j: (i, 0)), pl.BlockSpec((tm, WB), lambda i, j: (i, j))]
        + [pl.BlockSpec((None, D, TN), lambda i, j: (j, 0, 0))] * W_PHASES,
        [pl.BlockSpec((tm, D), lambda i, j: (i, 0))], [SDS((S, D), F32)], [], [dz, dh] + list(wgs), "grad_x")
    return outs[0], carried


def _place():
    x, y, c = lax.axis_index("x"), lax.axis_index("y"), lax.axis_index("c")
    chips = [(x, y), (1 - x, y), (x, 1 - y), (1 - x, 1 - y)]
    return x, y, c, chips


def _blk(x, y, c):
    return 4 * x + 2 * y + c


def _adamw(w, g, m, v):
    m = ADAM_B1 * m + (1.0 - ADAM_B1) * g
    v = ADAM_B2 * v + (1.0 - ADAM_B2) * (g * g)
    m_hat = m / (1.0 - ADAM_B1 ** ADAM_STEP)
    v_hat = v / (1.0 - ADAM_B2 ** ADAM_STEP)
    delta = -ADAM_LR * (m_hat / (jnp.sqrt(v_hat) + ADAM_EPS) + ADAM_WD * w)
    return delta, m, v


class _Rider:
    def __init__(self, args, in_specs, out_shape, out_specs, scratch, start, finish):
        self.args, self.in_specs, self.out_shape, self.out_specs = args, in_specs, out_shape, out_specs
        self.scratch, self.start, self.finish = scratch, start, finish


def _carry(body, rider, n_in, n_out, first, last):
    if rider is None:
        return body
    r_in, r_out, r_scr = len(rider.args), len(rider.out_shape), len(rider.scratch)

    def carrying(*refs):
        o0 = n_in + r_in
        s0 = o0 + n_out + r_out
        s1 = len(refs) - r_scr
        theirs = (refs[n_in:o0], refs[o0 + n_out:s0], refs[s1:])
        pl.when(first())(lambda: rider.start(*theirs))
        body(*refs[:n_in], *refs[o0:o0 + n_out], *refs[s0:s1])
        pl.when(last())(lambda: rider.finish(*theirs))

    return carrying


def _call_carrying(body, rider, grid, in_specs, out_specs, out_shape, scratch, args, name, aliases=None):
    n_in, n_out = len(in_specs), len(out_specs)
    ids = lambda: [pl.program_id(a) for a in range(len(grid))]
    first = lambda: functools.reduce(jnp.logical_and, [i == 0 for i in ids()])
    last = lambda: functools.reduce(jnp.logical_and, [i == n - 1 for i, n in zip(ids(), grid)])
    if rider is not None:
        in_specs, args = in_specs + rider.in_specs, list(args) + rider.args
        out_specs, out_shape = out_specs + rider.out_specs, out_shape + rider.out_shape
        scratch = scratch + rider.scratch
    outs = pl.pallas_call(
        _carry(body, rider, n_in, n_out, first, last), name=name, grid=grid,
        in_specs=in_specs, out_specs=out_specs, out_shape=out_shape, scratch_shapes=scratch,
        input_output_aliases=aliases or {}, compiler_params=_params(("arbitrary",) * len(grid)),
    )(*args)
    return list(outs[:n_out]), list(outs[n_out:])


def _gather_copy(tensors, send_sems, recv_sems, t, k, block, to, src=None):
    dst = tensors[t][1](_blk(*block))
    return pltpu.make_async_remote_copy(
        src_ref=dst if src is None else src, dst_ref=dst,
        send_sem=send_sems.at[t, k], recv_sem=recv_sems.at[t, k], device_id=to, device_id_type=MESH)


def _gather_start(tensors, send_sems, recv_sems, local_sems):
    x, y, c, chips = _place()
    me, sib = (x, y, c), (x, y, 1 - c)
    for t, (src, dst) in enumerate(tensors):
        pltpu.make_async_copy(src, dst(_blk(*me)), local_sems.at[t]).start()
        _gather_copy(tensors, send_sems, recv_sems, t, 0, me, sib, src).start()
        for j in (1, 2, 3):
            _gather_copy(tensors, send_sems, recv_sems, t, j, me, (*chips[j], c), src).start()


def _gather_finish(tensors, send_sems, recv_sems, local_sems):
    x, y, c, chips = _place()
    me, sib = (x, y, c), (x, y, 1 - c)
    copy = functools.partial(_gather_copy, tensors, send_sems, recv_sems)
    for t in range(len(tensors)):
        for j in (1, 2, 3):
            copy(t, j, (*chips[j], c), me).wait_recv()
            copy(t, 3 + j, (*chips[j], c), sib).start()
    for t, (src, dst) in enumerate(tensors):
        copy(t, 0, sib, me).wait_recv()
        for j in (1, 2, 3):
            copy(t, 3 + j, (*chips[j], 1 - c), me).wait_recv()
        copy(t, 0, me, sib, src).wait_send()
        for j in (1, 2, 3):
            copy(t, j, me, (*chips[j], c), src).wait_send()
            copy(t, 3 + j, (*chips[j], c), sib).wait_send()
        pltpu.make_async_copy(src, dst(_blk(*me)), local_sems.at[t]).wait()


W_PHASES = WB // TN


def _gather_first_rider(w_in_s, w_out_s, w_pool_s):
    shapes = [w_in_s.shape, w_out_s.shape, w_pool_s.shape]

    def keeps(outs, scr):
        return [pltpu.make_async_copy(scr[n], outs[n], scr[6].at[n]) for n in range(3)]

    def tensors(outs, scr):
        return [(scr[0].at[:, pl.ds(0, TN)], lambda b: outs[3].at[b])]

    def start(ins, outs, scr):
        for n in range(3):
            scr[n][...] = ins[n][...].astype(BF16)
        for keep in keeps(outs, scr):
            keep.start()
        _gather_start(tensors(outs, scr), *scr[3:6])

    def finish(ins, outs, scr):
        _gather_finish(tensors(outs, scr), *scr[3:6])
        for keep in keeps(outs, scr):
            keep.wait()

    vmem = pl.BlockSpec(memory_space=pltpu.VMEM)
    hbm = pl.BlockSpec(memory_space=pl.ANY)
    return _Rider(
        args=[w_in_s, w_out_s, w_pool_s], in_specs=[vmem] * 3,
        out_shape=[SDS(sh, BF16) for sh in shapes] + [SDS((N_DEV, D, TN), BF16)], out_specs=[hbm] * 4,
        scratch=[pltpu.VMEM(sh, BF16) for sh in shapes]
        + [pltpu.SemaphoreType.DMA((1, 7)), pltpu.SemaphoreType.DMA((1, 7)), pltpu.SemaphoreType.DMA((1,)),
           pltpu.SemaphoreType.DMA((3,))],
        start=start, finish=finish)


def _gather_w_in_rider(shard16, s):
    def tensors(ins, outs):
        return [(ins[0].at[:, pl.ds(s * TN, TN)], lambda b: outs[0].at[b])]

    hbm = pl.BlockSpec(memory_space=pl.ANY)
    return _Rider(
        args=[shard16], in_specs=[hbm], out_shape=[SDS((N_DEV, D, TN), BF16)], out_specs=[hbm],
        scratch=[pltpu.SemaphoreType.DMA((1, 7)), pltpu.SemaphoreType.DMA((1, 7)), pltpu.SemaphoreType.DMA((1,))],
        start=lambda ins, outs, scr: _gather_start(tensors(ins, outs), *scr),
        finish=lambda ins, outs, scr: _gather_finish(tensors(ins, outs), *scr))


def _gather_small_rider(w_out16, w_pool16):
    def tensors(ins, outs):
        gout_ref, gpool_ref = outs

        def pool_rows(b):
            return gpool_ref.at[:, pl.ds(pl.multiple_of(b * PB, PB), PB), :]

        return [(ins[0], lambda b: gout_ref.at[b]), (ins[1], pool_rows)]

    hbm = pl.BlockSpec(memory_space=pl.ANY)
    return _Rider(
        args=[w_out16, w_pool16], in_specs=[hbm, hbm],
        out_shape=[SDS((N_DEV, OB, D), BF16), SDS((N_GROUPS, GC, GC), BF16)], out_specs=[hbm, hbm],
        scratch=[pltpu.SemaphoreType.DMA((2, 7)), pltpu.SemaphoreType.DMA((2, 7)), pltpu.SemaphoreType.DMA((2,))],
        start=lambda ins, outs, scr: _gather_start(tensors(ins, outs), *scr),
        finish=lambda ins, outs, scr: _gather_finish(tensors(ins, outs), *scr))


def _block_table():
    x, y, c, chips = _place()
    return jnp.stack([_blk(*chip, c) for chip in chips]).astype(jnp.int32)


def _sibling_first_order():
    x, y, c, chips = _place()
    return jnp.stack([_blk(*chip, side) for side in (1 - c, c) for chip in chips]).astype(jnp.int32)


def _exchange_rider(p16):
    _, r_tot, cols = p16.shape

    def copies(ins, outs, scr):
        x, y, c, chips = _place()
        return [pltpu.make_async_remote_copy(
            src_ref=ins[0].at[_blk(*chips[k], 1 - c)], dst_ref=outs[0].at[k],
            send_sem=scr[0].at[k], recv_sem=scr[1].at[k], device_id=(x, y, 1 - c), device_id_type=MESH)
            for k in range(4)]

    def start(ins, outs, scr):
        for cp in copies(ins, outs, scr):
            cp.start()

    def finish(ins, outs, scr):
        for cp in copies(ins, outs, scr):
            cp.wait()

    hbm = pl.BlockSpec(memory_space=pl.ANY)
    return _Rider([p16], [hbm], [SDS((4, r_tot, cols), BF16)], [hbm],
                  [pltpu.SemaphoreType.DMA((4,)), pltpu.SemaphoreType.DMA((4,))], start, finish)


def _chip_sums(name, table, p32, r1, rc):
    _, r_tot, cols = p32.shape

    def body(tbl_ref, p_ref, r_ref, o_ref):
        o_ref[...] = (p_ref[...] + r_ref[...].astype(F32)).astype(BF16)

    return pl.pallas_call(
        body, name=name,
        grid_spec=pltpu.PrefetchScalarGridSpec(
            num_scalar_prefetch=1, grid=(3, r_tot // rc),
            in_specs=[pl.BlockSpec((None, rc, cols), lambda k, ch, tbl: (tbl[k + 1], ch, 0)),
                      pl.BlockSpec((None, rc, cols), lambda k, ch, tbl: (k + 1, ch, 0))],
            out_specs=pl.BlockSpec((None, rc, cols), lambda k, ch, tbl: (k, ch, 0))),
        out_shape=SDS((3, r_tot, cols), BF16),
        compiler_params=_params(("arbitrary", "arbitrary")),
    )(table, p32, r1)


def _stage2_rider(sums, stats=None):
    n_t = len(sums)

    def copies(ins, outs, scr):
        x, y, c, chips = _place()
        out = []
        for t in range(n_t):
            for k in (1, 2, 3):
                out.append(pltpu.make_async_remote_copy(
                    src_ref=ins[t].at[k - 1], dst_ref=outs[t].at[k - 1],
                    send_sem=scr[0].at[t, k - 1], recv_sem=scr[1].at[t, k - 1],
                    device_id=(*chips[k], c), device_id_type=MESH))
        if stats is not None:
            for k in range(1, N_DEV):
                peer = (x ^ ((k >> 2) & 1), y ^ ((k >> 1) & 1), c ^ (k & 1))
                out.append(pltpu.make_async_remote_copy(
                    src_ref=scr[4], dst_ref=outs[n_t].at[_blk(x, y, c)],
                    send_sem=scr[2].at[k - 1], recv_sem=scr[3].at[k - 1], device_id=peer, device_id_type=MESH))
        return out

    def own_rows(outs, scr):
        x, y, c, _ = _place()
        return pltpu.make_async_copy(scr[4], outs[n_t].at[_blk(x, y, c)], scr[5])

    def start(ins, outs, scr):
        if stats is not None:
            scr[4][...] = ins[n_t][...]
            scr[4][3:4, 0:DP] = ins[n_t + 1][0:1, :]
            own_rows(outs, scr).start()
        for cp in copies(ins, outs, scr):
            cp.start()

    def finish(ins, outs, scr):
        for cp in copies(ins, outs, scr):
            cp.wait()
        if stats is not None:
            own_rows(outs, scr).wait()

    vmem = pl.BlockSpec(memory_space=pltpu.VMEM)
    hbm = pl.BlockSpec(memory_space=pl.ANY)
    scratch = [pltpu.SemaphoreType.DMA((n_t, 3)), pltpu.SemaphoreType.DMA((n_t, 3))]
    args, in_specs = list(sums), [hbm] * n_t
    out_shape, out_specs = [SDS(s.shape, BF16) for s in sums], [hbm] * n_t
    if stats is not None:
        scratch += [pltpu.SemaphoreType.DMA((N_DEV - 1,)), pltpu.SemaphoreType.DMA((N_DEV - 1,)),
                    pltpu.VMEM((8, D), F32), pltpu.SemaphoreType.DMA(())]
        args, in_specs = args + list(stats), in_specs + [vmem, vmem]
        out_shape, out_specs = out_shape + [SDS((N_DEV, 8, D), F32)], out_specs + [hbm]
    return _Rider(args, in_specs, out_shape, out_specs, scratch, start, finish)


def _adamw_shard(name, table, p32, r1, r2, w, m, v, rc):
    _, r_tot, cols = p32.shape

    def body(tbl_ref, p_ref, r1_ref, r2_ref, w_ref, m_ref, v_ref, g_ref, d_ref, nm_ref, nv_ref):
        g = p_ref[...] + r1_ref[...].astype(F32)
        for k in range(3):
            g = g + r2_ref[k].astype(F32)
        delta, nm, nv = _adamw(w_ref[...], g, m_ref[...], v_ref[...])
        g_ref[...] = g
        d_ref[...] = delta
        nm_ref[...] = nm
        nv_ref[...] = nv

    rows = pl.BlockSpec((rc, cols), lambda ch, tbl: (ch, 0))
    shard = SDS((r_tot, cols), F32)
    return pl.pallas_call(
        body, name=name,
        grid_spec=pltpu.PrefetchScalarGridSpec(
            num_scalar_prefetch=1, grid=(r_tot // rc,),
            in_specs=[pl.BlockSpec((None, rc, cols), lambda ch, tbl: (tbl[0], ch, 0)),
                      pl.BlockSpec((None, rc, cols), lambda ch, tbl: (0, ch, 0)),
                      pl.BlockSpec((3, rc, cols), lambda ch, tbl: (0, ch, 0)), rows, rows, rows],
            out_specs=[rows, rows, rows, rows]),
        out_shape=[shard, shard, shard, shard],
        compiler_params=_params(("arbitrary",)),
    )(table, p32, r1, r2, w, m, v)


def _replicated_adamw(gathered, gain, bias, scale, m_gain, m_bias, m_scale, v_gain, v_bias, v_scale):
    def body(all_ref, s_ref, g_ref, b_ref, ms_ref, mg_ref, mb_ref, vs_ref, vg_ref, vb_ref, loss_ref, *outs):
        tot = all_ref[0]
        for b in range(1, N_DEV):
            tot = tot + all_ref[b]
        loss_ref[...] = tot[2:3, 0:LANES]
        for n, (row, width, w_r, m_r, v_r) in enumerate(((3, DP, s_ref, ms_ref, vs_ref), (0, D, g_ref, mg_ref, vg_ref),
                                                         (1, D, b_ref, mb_ref, vb_ref))):
            g = tot[row:row + 1, 0:width]
            delta, nm, nv = _adamw(w_r[...], g, m_r[...], v_r[...])
            for out, val in zip(outs[4 * n:4 * n + 4], (g, delta, nm, nv)):
                out[...] = val

    vmem = pl.BlockSpec(memory_space=pltpu.VMEM)
    shapes = [SDS((1, LANES), F32)] + [SDS((1, width), F32) for width in (DP, D, D) for _ in range(4)]
    return pl.pallas_call(
        body, name="replicated_adamw",
        in_specs=[vmem] * 10, out_specs=[vmem] * len(shapes), out_shape=shapes,
        compiler_params=_params(),
    )(gathered, scale, gain, bias, m_scale, m_gain, m_bias, v_scale, v_gain, v_bias)


def kernel(x, w_in, w_pool, pool_scale, w_out, ln_gain, ln_bias, loss_target, m_w_in, m_w_pool, m_pool_scale, m_w_out, m_ln_gain, m_ln_bias, v_w_in, v_w_pool, v_pool_scale, v_w_out, v_ln_gain, v_ln_bias):
    pool_rows = (N_GROUPS * PB, GC)
    x2, target = x[0], loss_target[0]
    table = _block_table()

    (xb, xt, *rope), (shard16, w_out16, w_pool16, wg0) = _prep_x(
        x2, _rope_rates(), _gather_first_rider(w_in[0], w_out[0], w_pool[0]))
    h, (wg1, wg2, wg_out, wg_pool) = _proj_all(xb, rope, wg0, shard16, w_out16, w_pool16)
    wg_out = wg_out.reshape(D, D)
    h16, hug = [h[0][None], h[1], h[2]], h[3]
    fwd = [_attn_fwd(h16[n], f"attn_fwd_d{d}") for n, d in enumerate(DILATIONS)]
    y, yt, attn, lse1, lse4, lse16 = _mix(fwd[0][0][0], fwd[0][1][0], *fwd[1], *fwd[2], hug, wg_pool, pool_scale)
    dz, dzb, stats = _outproj_ln(y, wg_out, x2, target, ln_gain, ln_bias)

    dwout, dwout16 = _grad_w_out(yt, dzb)
    mid, (r1_out,) = _bwd_mid(dzb, wg_out, hug, attn, wg_pool, pool_scale, rider=_exchange_rider(dwout16))
    dh, do1, do4, do16, dd1, dd4, dd16, dwp, dwp16, gps = mid
    s2_out = _chip_sums("rs_out_sums", table, dwout, r1_out, OB)

    do, lse, dd = [do1[None], do4, do16], [lse1[None], lse4, lse16], [dd1[None], dd4, dd16]
    acc4, (r2_out,) = _attn_bwd(h16[1], do[1], lse[1], dd[1], rope, "attn_bwd_d4", rider=_stage2_rider([s2_out]))
    g_out, d_out, nm_out, nv_out = _adamw_shard(
        "adamw_w_out", table, dwout, r1_out, r2_out, w_out[0], m_w_out[0], v_w_out[0], OB // 2)
    acc16, (r1_pool,) = _attn_bwd(h16[2], do[2], lse[2], dd[2], rope, "attn_bwd_d16", rider=_exchange_rider(dwp16))
    s2_pool = _chip_sums("rs_pool_sums", table, dwp, r1_pool, N_GROUPS * PB)
    dh = _attn_bwd(h16[0], do[0], lse[0], dd[0], rope, "attn_bwd_d1", others=(acc4, acc16), dh=dh)[0]

    dwin, r1_in, s2_in = _grad_w_in(xt, dh, _sibling_first_order())
    grad_x, (r2_in, r2_pool, gathered) = _grad_x(
        dz, dh, (wg0, wg1, wg2), rider=_stage2_rider([s2_in, s2_pool], stats=(stats, gps)))
    g_in, d_in, nm_in, nv_in = _adamw_shard(
        "adamw_w_in", jnp.zeros_like(table), dwin, r1_in, r2_in, w_in[0], m_w_in[0], v_w_in[0], 512)
    g_pool, d_pool, nm_pool, nv_pool = _adamw_shard(
        "adamw_w_pool", table, dwp, r1_pool, r2_pool, w_pool[0].reshape(pool_rows), m_w_pool[0].reshape(pool_rows),
        v_w_pool[0].reshape(pool_rows), N_GROUPS * PB)
    loss, *small = _replicated_adamw(gathered, ln_gain, ln_bias, pool_scale, m_ln_gain, m_ln_bias, m_pool_scale,
                                     v_ln_gain, v_ln_bias, v_pool_scale)
    (g_ps, d_ps, nm_ps, nv_ps, g_gain, d_gain, nm_gain, nv_gain, g_bias, d_bias, nm_bias, nv_bias) = small

    shard4 = lambda t: t.reshape(1, N_GROUPS, PB, GC)
    lead = lambda t: t[None]
    return (loss[0, 0], lead(grad_x),
            lead(g_in), shard4(g_pool), g_ps, lead(g_out), g_gain, g_bias,
            lead(d_in), shard4(d_pool), d_ps, lead(d_out), d_gain, d_bias,
            lead(nm_in), shard4(nm_pool), nm_ps, lead(nm_out), nm_gain, nm_bias,
            lead(nv_in), shard4(nv_pool), nv_ps, lead(nv_out), nv_gain, nv_bias)
```

```python
import functools

import jax
import jax.numpy as jnp
from jax import lax
from jax.experimental import pallas as pl
from jax.experimental.pallas import tpu as pltpu

F32 = jnp.float32
BF16 = jnp.bfloat16
SDS = jax.ShapeDtypeStruct
MESH = pl.DeviceIdType.MESH

N_DEV = 8
S = 4096
D = 2048
N_HEADS = 8
DH = 128
DA = N_HEADS * DH
DP = 1024
N_GROUPS = 4
GC = DP // N_GROUPS
POOL_WINDOWS = (2, 4, 8, 16)
HALO = 16
D_IN = 3 * DA + DP + D
WB = D_IN // N_DEV
TN = 256
HW = 3 * DA
OB = D // N_DEV
PB = GC // N_DEV
ROPE_DIM = DH // 4
ROPE_HALF = ROPE_DIM // 2
ROPE_THETA = 500000.0
DILATIONS = (1, 4, 16)
KB = 128
LN_EPS = 1e-5
ALPHA = 2.0 ** 0.25
SCALE = DH ** -0.5
NEG = -1e30
ADAM_LR, ADAM_B1, ADAM_B2, ADAM_EPS, ADAM_WD, ADAM_STEP = 0.001, 0.9, 0.999, 1e-08, 0.01, 10

VMEM_LIMIT_V7X = 61 * 1024 * 1024

NT = (((1,), (1,)), ((), ()))
T_N = (((0,), (0,)), ((), ()))


def _params(sem=None):
    return pltpu.CompilerParams(dimension_semantics=sem, vmem_limit_bytes=VMEM_LIMIT_V7X)


def _dot(a, b, dims=None):
    if dims is None:
        return jnp.dot(a, b, preferred_element_type=F32)
    return lax.dot_general(a, b, dims, preferred_element_type=F32)


def _rope_rates():
    inv_freq = ROPE_THETA ** (-(2.0 * jnp.arange(ROPE_HALF, dtype=F32)) / ROPE_DIM)
    return jnp.concatenate([inv_freq, inv_freq, jnp.zeros((DH - ROPE_DIM,), F32)])[None, :]


def _rope_table_rows(rates, first_row, n):
    pos = (first_row + lax.broadcasted_iota(jnp.int32, (n, DH), 0)).astype(F32)
    lane = lax.broadcasted_iota(jnp.int32, (n, DH), 1)
    ang = pos * rates
    sin = jnp.sin(ang)
    c = jnp.where(lane < ROPE_DIM, jnp.cos(ang), 1.0)
    sn = jnp.where(lane < ROPE_HALF, -sin, jnp.where(lane < ROPE_DIM, sin, 0.0))
    return c, sn


def _rope_partner(t):
    lane = lax.broadcasted_iota(jnp.int32, t.shape, 1)
    return jnp.where(lane < ROPE_HALF, pltpu.roll(t, DH - ROPE_HALF, 1), pltpu.roll(t, ROPE_HALF, 1))


def _rope(t, c, sn):
    return t * c + _rope_partner(t) * sn


def _rope_t(g, c, sn):
    return g * c - _rope_partner(g) * sn


def _prep_x(x, rates, rider):
    tm = 512

    def body(x_ref, f_ref, xb_ref, xt_ref, c_ref, sn_ref):
        xv = x_ref[...]
        xb_ref[...] = xv.astype(BF16)
        xt_ref[...] = xv.T.astype(BF16)
        c_ref[...], sn_ref[...] = _rope_table_rows(f_ref[...], pl.program_id(0) * tm, tm)

    row = lambda i: (i, 0)
    return _call_carrying(
        body, rider, (S // tm,), [pl.BlockSpec((tm, D), row), pl.BlockSpec((1, DH), lambda i: (0, 0))],
        [pl.BlockSpec((tm, D), row), pl.BlockSpec((D, tm), lambda i: (0, i)),
         pl.BlockSpec((tm, DH), row), pl.BlockSpec((tm, DH), row)],
        [SDS((S, D), BF16), SDS((D, S), BF16), SDS((S, DH), F32), SDS((S, DH), F32)], [], [x, rates], "prep_x")


def _residues(slab_ref, idx, d, r, n):
    return slab_ref[(*idx, pl.ds(r, n, stride=d), slice(None))]


def _proj_all(xb, rope, wg0, shard16, w_out16, w_pool16):
    n_heads = TN // DH
    n16 = HW // TN
    j16 = n16 // W_PHASES
    n_rope = 2 * DA // TN
    once = pl.Buffered(1)
    riders = [_gather_w_in_rider(shard16, s) for s in range(1, W_PHASES)] + [_gather_small_rider(w_out16, w_pool16)]
    last_j = N_DEV - 1

    def body(*refs):
        x_ref, c_ref, sn_ref, g0_ref = refs[:4]
        pos = 4
        r_ins, r_outs, r_scrs = [], [], []
        for r in riders:
            r_ins.append(refs[pos:pos + len(r.args)])
            pos += len(r.args)
        h16_ref, h4_ref, h16r_ref, hug_ref = refs[pos:pos + 4]
        pos += 4
        for r in riders:
            r_outs.append(refs[pos:pos + len(r.out_shape)])
            pos += len(r.out_shape)
        slab_ref, res_ref, w_ref, w_sems = refs[pos:pos + 4]
        pos += 4
        for r in riders:
            r_scrs.append(refs[pos:pos + len(r.scratch)])
            pos += len(r.scratch)
        gathered = [g0_ref] + [r_outs[s][0] for s in range(W_PHASES - 1)]
        s, j = pl.program_id(0), pl.program_id(1)
        d4, d16 = DILATIONS[1], DILATIONS[2]

        def fetch(ph, blk, slot):
            return pltpu.make_async_copy(gathered[ph].at[blk], w_ref.at[slot], w_sems.at[slot])

        pl.when((s == 0) & (j == 0))(lambda: fetch(0, 0, 0).start())
        for ph in range(W_PHASES):
            pl.when((s == ph) & (j == 0))(lambda ph=ph: riders[ph].start(r_ins[ph], r_outs[ph], r_scrs[ph]))
            pl.when(s == ph)(lambda ph=ph: fetch(ph, j, j % 2).wait())
            pl.when((s == ph) & (j == last_j))(lambda ph=ph: riders[ph].finish(r_ins[ph], r_outs[ph], r_scrs[ph]))
            pl.when((s == ph) & (j < last_j))(lambda ph=ph: fetch(ph, j + 1, (j + 1) % 2).start())
            if ph + 1 < W_PHASES:
                pl.when((s == ph) & (j == last_j))(lambda ph=ph: fetch(ph + 1, 0, 0).start())

        acc = _dot(x_ref[...], w_ref[j % 2])

        def heads(with_rope):
            for hh in range(n_heads):
                hs = slice(hh * DH, (hh + 1) * DH)
                t = _rope(acc[:, hs], c_ref[...], sn_ref[...]) if with_rope else acc[:, hs]
                h16_ref[:, hs] = t.astype(BF16)
                slab_ref[...] = t
                for q in range(d4):
                    t4 = _residues(slab_ref, (), d4, q, S // d4)
                    h4_ref[q, :, hs] = t4.astype(BF16)
                    res_ref[...] = t4
                    for m in range(d16 // d4):
                        h16r_ref[d4 * m + q, :, hs] = _residues(res_ref, (), d4, m, S // d16).astype(BF16)

        pl.when(W_PHASES * j + s < n_rope)(lambda: heads(True))
        pl.when((W_PHASES * j + s >= n_rope) & (j < j16))(lambda: heads(False))

        @pl.when(j >= j16)
        def _():
            hug_ref[...] = acc.astype(BF16)

    col16 = lambda s, j: W_PHASES * jnp.minimum(j, j16 - 1) + s
    colug = lambda s, j: W_PHASES * jnp.maximum(j, j16) + s - n16
    hbm = pl.BlockSpec(memory_space=pl.ANY)
    resident = lambda shape: pl.BlockSpec(shape, lambda s, j: (0, 0), pipeline_mode=once)
    in_specs = [resident((S, D)), resident((S, DH)), resident((S, DH)), hbm]
    args = [xb, *rope, wg0]
    out_specs = [pl.BlockSpec((S, TN), lambda s, j: (0, col16(s, j)))]
    out_specs += [pl.BlockSpec((d, S // d, TN), lambda s, j: (0, 0, col16(s, j))) for d in DILATIONS[1:]]
    out_specs += [pl.BlockSpec((S, TN), lambda s, j: (0, colug(s, j)))]
    out_shape = [SDS((S, HW), BF16)] + [SDS((d, S // d, HW), BF16) for d in DILATIONS[1:]] + [SDS((S, HW), BF16)]
    scratch = [pltpu.VMEM((S, DH), F32), pltpu.VMEM((S // DILATIONS[1], DH), F32), pltpu.VMEM((2, D, TN), BF16),
               pltpu.SemaphoreType.DMA((2,))]
    for r in riders:
        in_specs, args = in_specs + r.in_specs, args + r.args
        out_specs, out_shape = out_specs + r.out_specs, out_shape + r.out_shape
        scratch = scratch + r.scratch
    outs = pl.pallas_call(
        body, name="proj_all", grid=(W_PHASES, N_DEV),
        in_specs=in_specs, out_specs=out_specs, out_shape=out_shape, scratch_shapes=scratch,
        compiler_params=_params(("arbitrary", "arbitrary")),
    )(*args)
    return list(outs[:4]), list(outs[4:])


BQ = 2 * KB
LANES = 128


def _to_lane(acc, col, h):
    lane = lax.broadcasted_iota(jnp.int32, acc.shape, 1)
    return jnp.where(lane == h, col, acc)


def _attn_fwd(h16, name):
    d, n_sub, _ = h16.shape
    n_i = n_sub // BQ
    kw = KB + BQ

    def body(q_ref, kc_ref, vc_ref, o_ref, l_ref, kw_ref, vw_ref):
        i = pl.program_id(1)

        @pl.when(i == 0)
        def _():
            kw_ref[0:KB, :] = jnp.zeros((KB, DA), BF16)
            vw_ref[0:KB, :] = jnp.zeros((KB, DA), BF16)

        @pl.when(i > 0)
        def _():
            kw_ref[0:KB, :] = kw_ref[BQ:kw, :]
            vw_ref[0:KB, :] = vw_ref[BQ:kw, :]

        kw_ref[KB:kw, :] = kc_ref[...]
        vw_ref[KB:kw, :] = vc_ref[...]
        a = lax.broadcasted_iota(jnp.int32, (KB, 2 * KB), 0)
        b = lax.broadcasted_iota(jnp.int32, (KB, 2 * KB), 1)
        band = (b >= a) & (b <= a + KB)
        first_key = jnp.where(i == 0, KB, 0)
        masks = (band & (b >= first_key), band)
        for half in range(2):
            rs = slice(half * KB, (half + 1) * KB)
            ks = slice(half * KB, (half + 2) * KB)
            lse = jnp.zeros((KB, LANES), F32)
            for h in range(N_HEADS):
                hs = slice(h * DH, (h + 1) * DH)
                s = jnp.where(masks[half], _dot(q_ref[rs, hs], kw_ref[ks, hs], NT) * SCALE, NEG)
                m = jnp.max(s, axis=1, keepdims=True)
                p = jnp.exp(s - m)
                den = jnp.sum(p, axis=1, keepdims=True)
                o_ref[rs, hs] = (_dot(p.astype(BF16), vw_ref[ks, hs]) / den).astype(BF16)
                lse = _to_lane(lse, m + jnp.log(den), h)
            l_ref[rs, :] = lse

    return pl.pallas_call(
        body, name=name, grid=(d, n_i),
        in_specs=[pl.BlockSpec((None, BQ, DA), lambda r, i: (r, i, 0)),
                  pl.BlockSpec((None, BQ, DA), lambda r, i: (r, i, 1)),
                  pl.BlockSpec((None, BQ, DA), lambda r, i: (r, i, 2))],
        out_specs=[pl.BlockSpec((None, BQ, DA), lambda r, i: (r, i, 0)),
                   pl.BlockSpec((None, BQ, LANES), lambda r, i: (r, i, 0))],
        out_shape=[SDS((d, n_sub, DA), BF16), SDS((d, n_sub, LANES), F32)],
        scratch_shapes=[pltpu.VMEM((kw, DA), BF16), pltpu.VMEM((kw, DA), BF16)],
        compiler_params=_params(("arbitrary", "arbitrary")),
    )(h16, h16, h16)


def _pooled(ext_ref, g, rows, tm):
    w = POOL_WINDOWS[g]
    cs = slice(g * GC, (g + 1) * GC)
    cur = ext_ref[HALO:HALO + tm, cs]
    win = cur
    for j in range(1, w):
        win = win + ext_ref[HALO - j:HALO - j + tm, cs]
    cnt = jnp.minimum(rows + 1, w).astype(F32)
    return win / cnt - cur, cnt


def _fill_ext(ext_ref, u_ref, uh_ref, blk, tm):
    @pl.when(blk == 0)
    def _():
        ext_ref[0:HALO, :] = jnp.zeros((HALO, DP), F32)

    @pl.when(blk > 0)
    def _():
        ext_ref[0:HALO, :] = uh_ref[...].astype(F32)

    ext_ref[HALO:HALO + tm, :] = u_ref[...].astype(F32)


def _residue_specs(tm, width):
    return [pl.BlockSpec((d, tm // d, width), lambda i: (0, i, 0)) for d in DILATIONS[1:]]


def _mix(o1, l1, o4, l4, o16, l16, hug, wp, scale):
    tm = 256
    n_slab = N_HEADS + 1

    def body(o1r, l1r, o4r, l4r, o16r, l16r, u_ref, uh_ref, ga_ref, gp_ref, wp_ref, sc_ref,
             y_ref, yt_ref, attn_ref, lse_ref, lse4_ref, lse16_ref, ext_ref, ys_ref, nat_ref, ls_ref):
        i = pl.program_id(0)
        for n, (d, o_r, l_r) in enumerate(((DILATIONS[1], o4r, l4r), (DILATIONS[2], o16r, l16r))):
            for r in range(d):
                rows = pl.ds(r, tm // d, stride=d)
                for h in range(N_HEADS):
                    nat_ref[n, h, rows, :] = o_r[r, :, h * DH:(h + 1) * DH].astype(F32)
                nat_ref[n, N_HEADS, rows, :] = l_r[r]
        la, lb, lc = l1r[...], nat_ref[0, N_HEADS], nat_ref[1, N_HEADS]
        mx = jnp.maximum(jnp.maximum(la, lb), lc)
        ea, eb, ec = jnp.exp(la - mx), jnp.exp(lb - mx), jnp.exp(lc - mx)
        z = ea + eb + ec
        wa, wb, wc = ea / z, eb / z, ec / z
        lse = mx + jnp.log(z)
        lse_ref[...] = lse
        ls_ref[...] = lse
        for d, out in ((DILATIONS[1], lse4_ref), (DILATIONS[2], lse16_ref)):
            for r in range(d):
                out[r] = ls_ref[pl.ds(r, tm // d, stride=d), :]
        for h in range(N_HEADS):
            hs = slice(h * DH, (h + 1) * DH)
            hc = slice(h, h + 1)
            attn = wa[:, hc] * o1r[:, hs].astype(F32) + wb[:, hc] * nat_ref[0, h] + wc[:, hc] * nat_ref[1, h]
            attn_ref[:, hs] = attn.astype(BF16)
            gt = ga_ref[:, hs].astype(F32)
            ys_ref[:, hs] = attn * (gt * jax.nn.sigmoid(gt))

        _fill_ext(ext_ref, u_ref, uh_ref, i, tm)
        rows = i * tm + lax.broadcasted_iota(jnp.int32, (tm, 1), 0)
        for g in range(N_GROUPS):
            cs = slice(g * GC, (g + 1) * GC)
            gs = slice(DA + g * GC, DA + (g + 1) * GC)
            pooled, _ = _pooled(ext_ref, g, rows, tm)
            po = _dot(pooled.astype(BF16), wp_ref[g]) * sc_ref[:, cs]
            gt = gp_ref[:, cs].astype(F32)
            ys_ref[:, gs] = po * (gt * jax.nn.sigmoid(gt))
        yv = ys_ref[...]
        y_ref[...] = yv.astype(BF16)
        yt_ref[...] = yv.T.astype(BF16)

    row = lambda i: (i, 0)
    blk = pl.BlockSpec((tm, DA), row)
    lanes = pl.BlockSpec((tm, LANES), row)
    o_res, l_res = _residue_specs(tm, DA), _residue_specs(tm, LANES)
    return pl.pallas_call(
        body, name="mix", grid=(S // tm,),
        in_specs=[blk, lanes, o_res[0], l_res[0], o_res[1], l_res[1],
                  pl.BlockSpec((tm, DP), row),
                  pl.BlockSpec((HALO, DP), lambda i: (jnp.maximum(i * (tm // HALO) - 1, 0), 0)),
                  pl.BlockSpec((tm, DA), lambda i: (i, 1)), pl.BlockSpec((tm, DP), lambda i: (i, 2)),
                  pl.BlockSpec((N_GROUPS, GC, GC), lambda i: (0, 0, 0)),
                  pl.BlockSpec((1, DP), lambda i: (0, 0))],
        out_specs=[pl.BlockSpec((tm, D), row), pl.BlockSpec((D, tm), lambda i: (0, i)), blk, lanes] + l_res,
        out_shape=[SDS((S, D), BF16), SDS((D, S), BF16), SDS((S, DA), BF16), SDS((S, LANES), F32)]
        + [SDS((d, S // d, LANES), F32) for d in DILATIONS[1:]],
        scratch_shapes=[pltpu.VMEM((HALO + tm, DP), F32), pltpu.VMEM((tm, D), F32),
                        pltpu.VMEM((2, n_slab, tm, DH), F32), pltpu.VMEM((tm, LANES), F32)],
        compiler_params=_params(("arbitrary",)),
    )(o1, l1, o4, l4, o16, l16, hug, hug, hug, hug, wp, scale)


def _outproj_ln(y, wout, x, target, gain, bias):
    tm = 512
    te = 128

    def body(y_ref, w_ref, x_ref, t_ref, g_ref, b_ref, dz_ref, dzb_ref, st_ref, out_ref):
        i = pl.program_id(0)

        @pl.when(i == 0)
        def _():
            st_ref[...] = jnp.zeros((8, D), F32)

        out_ref[...] = _dot(y_ref[...], w_ref[...])
        gn = g_ref[...]
        for e in range(tm // te):
            rs = slice(e * te, (e + 1) * te)
            z = ALPHA * x_ref[rs, :] + out_ref[rs, :]
            mu = jnp.mean(z, axis=1, keepdims=True)
            zc = z - mu
            var = jnp.mean(zc * zc, axis=1, keepdims=True)
            rstd = lax.rsqrt(var + LN_EPS)
            xhat = zc * rstd
            diff = xhat * gn + b_ref[...] - t_ref[rs, :]
            dyln = diff / D
            st_ref[0:1, :] += jnp.sum(dyln * xhat, axis=0, keepdims=True)
            st_ref[1:2, :] += jnp.sum(dyln, axis=0, keepdims=True)
            row_loss = jnp.sum(diff * diff, axis=1, keepdims=True) / D
            st_ref[2:3, :] += jnp.broadcast_to(0.5 * jnp.sum(row_loss, axis=0, keepdims=True), (1, D))
            dxh = dyln * gn
            m1 = jnp.mean(dxh, axis=1, keepdims=True)
            m2 = jnp.mean(dxh * xhat, axis=1, keepdims=True)
            dz = rstd * (dxh - m1 - xhat * m2)
            dz_ref[rs, :] = dz
            dzb_ref[rs, :] = dz.astype(BF16)

    row = lambda i: (i, 0)
    const = lambda i: (0, 0)
    return pl.pallas_call(
        body, name="outproj_ln", grid=(S // tm,),
        in_specs=[pl.BlockSpec((tm, D), row),
                  pl.BlockSpec((D, D), const, pipeline_mode=pl.Buffered(1)),
                  pl.BlockSpec((tm, D), row), pl.BlockSpec((tm, D), row),
                  pl.BlockSpec((1, D), const), pl.BlockSpec((1, D), const)],
        out_specs=[pl.BlockSpec((tm, D), row), pl.BlockSpec((tm, D), row), pl.BlockSpec((8, D), const)],
        out_shape=[SDS((S, D), F32), SDS((S, D), BF16), SDS((8, D), F32)],
        scratch_shapes=[pltpu.VMEM((tm, D), F32)],
        compiler_params=_params(("arbitrary",)),
    )(y, wout, x, target, gain, bias)


def _bwd_mid(dzb, wout, hug, attn, wp, scale, rider=None):
    tm = 256
    n = S // tm

    def body(dz_ref, w_ref, ga_ref, gp_ref, at_ref, u_ref, uh_ref, wp_ref, sc_ref,
             dh_ref, do_ref, do4_ref, do16_ref, dd_ref, dd4_ref, dd16_ref, dwp_ref, dwp16_ref, gps_ref,
             ext_ref, eext_ref, acc_ref, nat_ref, ds_ref):
        i = pl.program_id(0)
        ib = n - 1 - i

        @pl.when(i == 0)
        def _():
            eext_ref[tm:tm + HALO, :] = jnp.zeros((HALO, DP), F32)
            acc_ref[...] = jnp.zeros((N_GROUPS, GC, GC), F32)
            gps_ref[...] = jnp.zeros((8, DP), F32)

        dy = _dot(dz_ref[...], w_ref[...], NT)

        def through_gate(dy_part, gt):
            sg = jax.nn.sigmoid(gt)
            return dy_part * (gt * sg), dy_part * (sg * (1.0 + gt * (1.0 - sg)))

        dat, dyg_a = through_gate(dy[:, 0:DA], ga_ref[...].astype(F32))
        dmix_p, dyg_p = through_gate(dy[:, DA:D], gp_ref[...].astype(F32))

        at = at_ref[...].astype(F32)
        do_ref[...] = dat.astype(BF16)
        dh_ref[:, DP:DP + DA] = (dyg_a * at).astype(BF16)
        prod = dat * at
        dd = jnp.zeros((tm, LANES), F32)
        for h in range(N_HEADS):
            hs = slice(h * DH, (h + 1) * DH)
            dd = _to_lane(dd, jnp.sum(prod[:, hs], axis=1, keepdims=True), h)
            nat_ref[h] = dat[:, hs]
        dd_ref[...] = dd
        ds_ref[...] = dd
        for d, do_out, dd_out in ((DILATIONS[1], do4_ref, dd4_ref), (DILATIONS[2], do16_ref, dd16_ref)):
            for r in range(d):
                dd_out[r] = _residues(ds_ref, (), d, r, tm // d)
                for h in range(N_HEADS):
                    do_out[r, :, h * DH:(h + 1) * DH] = _residues(nat_ref, (h,), d, r, tm // d).astype(BF16)

        _fill_ext(ext_ref, u_ref, uh_ref, ib, tm)
        rows = ib * tm + lax.broadcasted_iota(jnp.int32, (tm, 1), 0)
        for g in range(N_GROUPS):
            w = POOL_WINDOWS[g]
            cs = slice(g * GC, (g + 1) * GC)
            pooled, cnt = _pooled(ext_ref, g, rows, tm)
            pre = _dot(pooled.astype(BF16), wp_ref[g])
            sc = sc_ref[:, cs]
            dpo = dmix_p[:, cs]
            gps_ref[0:1, cs] += jnp.sum(dpo * pre, axis=0, keepdims=True)
            dh_ref[:, DP + DA + g * GC:DP + DA + (g + 1) * GC] = (dyg_p[:, cs] * (pre * sc)).astype(BF16)
            dpre = (dpo * sc).astype(BF16)
            acc_ref[g] += _dot(pooled.T.astype(BF16), dpre)
            dpooled = _dot(dpre, wp_ref[g], NT)
            eext_ref[0:tm, cs] = dpooled / cnt
            du = eext_ref[0:tm, cs]
            for j in range(1, w):
                du = du + eext_ref[j:j + tm, cs]
            dh_ref[:, cs] = (du - dpooled).astype(BF16)
        eext_ref[tm:tm + HALO, :] = eext_ref[0:HALO, :]

        @pl.when(i == n - 1)
        def _():
            for j in range(N_DEV):
                for g in range(N_GROUPS):
                    blk = acc_ref[g, j * PB:(j + 1) * PB, :]
                    dwp_ref[j, g * PB:(g + 1) * PB, :] = blk
                    dwp16_ref[j, g * PB:(g + 1) * PB, :] = blk.astype(BF16)

    rev = lambda i: (n - 1 - i, 0)
    const = lambda i: (0, 0)
    res = lambda width: [pl.BlockSpec((d, tm // d, width), lambda i: (0, n - 1 - i, 0)) for d in DILATIONS[1:]]
    pool_blocks = pl.BlockSpec((N_DEV, N_GROUPS * PB, GC), lambda i: (0, 0, 0))
    outs, carried = _call_carrying(
        body, rider, (n,),
        [pl.BlockSpec((tm, D), rev),
         pl.BlockSpec((D, D), const, pipeline_mode=pl.Buffered(1)),
         pl.BlockSpec((tm, DA), lambda i: (n - 1 - i, 1)), pl.BlockSpec((tm, DP), lambda i: (n - 1 - i, 2)),
         pl.BlockSpec((tm, DA), rev), pl.BlockSpec((tm, DP), rev),
         pl.BlockSpec((HALO, DP), lambda i: (jnp.maximum((n - 1 - i) * (tm // HALO) - 1, 0), 0)),
         pl.BlockSpec((N_GROUPS, GC, GC), lambda i: (0, 0, 0)),
         pl.BlockSpec((1, DP), const)],
        [pl.BlockSpec((tm, DP + D), lambda i: (n - 1 - i, 1)), pl.BlockSpec((tm, DA), rev)] + res(DA)
        + [pl.BlockSpec((tm, LANES), rev)] + res(LANES) + [pool_blocks, pool_blocks, pl.BlockSpec((8, DP), const)],
        [SDS((S, D_IN), BF16), SDS((S, DA), BF16)] + [SDS((d, S // d, DA), BF16) for d in DILATIONS[1:]]
        + [SDS((S, LANES), F32)] + [SDS((d, S // d, LANES), F32) for d in DILATIONS[1:]]
        + [SDS((N_DEV, N_GROUPS * PB, GC), F32), SDS((N_DEV, N_GROUPS * PB, GC), BF16), SDS((8, DP), F32)],
        [pltpu.VMEM((HALO + tm, DP), F32), pltpu.VMEM((tm + HALO, DP), F32),
         pltpu.VMEM((N_GROUPS, GC, GC), F32), pltpu.VMEM((N_HEADS, tm, DH), F32), pltpu.VMEM((tm, LANES), F32)],
        [dzb, wout, hug, hug, attn, hug, hug, wp, scale], "bwd_mid")
    return outs, carried


def _attn_bwd(h16, do, lse, dd, rope, name, others=None, dh=None, rider=None):
    d, n_sub, _ = h16.shape
    n_i = n_sub // BQ
    n_kb = n_sub // KB
    qw = BQ + KB
    final = others is not None
    out_dtype = BF16
    n_cb = 3 * DA // DH

    def nxt(i):
        return jnp.minimum(2 * i + 2, n_kb - 1)

    def body(qc_ref, qn_ref, kc_ref, vc_ref, doc_ref, don_ref, lc_ref, ln_ref, dc_ref, dn_ref,
             c_ref, sn_ref, *rest):
        if final:
            acc4_ref, acc16_ref, _, out_ref, carry_ref, qw_ref, dow_ref, lw_ref, dw_ref, nat_ref = rest
        else:
            out_ref, carry_ref, qw_ref, dow_ref, lw_ref, dw_ref = rest
        i = pl.program_id(1)

        @pl.when(i == 0)
        def _():
            carry_ref[...] = jnp.zeros((KB, DA), F32)

        for win, own, after in ((qw_ref, qc_ref, qn_ref), (dow_ref, doc_ref, don_ref), (lw_ref, lc_ref, ln_ref),
                                (dw_ref, dc_ref, dn_ref)):
            win[0:BQ, :] = own[...]
            win[BQ:qw, :] = after[...]
        if final:
            for n, (dil, acc) in enumerate(((DILATIONS[1], acc4_ref), (DILATIONS[2], acc16_ref))):
                for r in range(dil):
                    for cb in range(n_cb):
                        nat_ref[n, cb, pl.ds(r, BQ // dil, stride=dil), :] = acc[r, :, cb * DH:(cb + 1) * DH].astype(F32)

        a = lax.broadcasted_iota(jnp.int32, (qw, BQ), 0)
        b = lax.broadcasted_iota(jnp.int32, (qw, BQ), 1)
        n_q = jnp.where(i == n_i - 1, BQ, qw)
        mask = (b <= a) & (a <= b + KB) & (a < n_q)
        tok = pl.ds(pl.program_id(0) + d * BQ * i, BQ, stride=d)
        tabs = (c_ref[tok, :], sn_ref[tok, :])
        for h in range(N_HEADS):
            hs = slice(h * DH, (h + 1) * DH)
            hc = slice(h, h + 1)
            q, k, vv, dob = qw_ref[:, hs], kc_ref[:, hs], vc_ref[:, hs], dow_ref[:, hs]
            s = _dot(q, k, NT) * SCALE
            p = jnp.exp(jnp.where(mask, s - lw_ref[:, hc], NEG))
            dp = _dot(dob, vv, NT)
            ds = (p * (dp - dw_ref[:, hc]) * SCALE).astype(BF16)
            dq = _dot(ds, k)
            dk = _dot(ds, q, T_N)
            dv = _dot(p.astype(BF16), dob, T_N)
            dq_lo = dq[0:KB] + carry_ref[:, hs]
            carry_ref[:, hs] = dq[BQ:qw]
            dq_own = _rope_t(jnp.concatenate([dq_lo, dq[KB:BQ]], axis=0), *tabs)
            for base, gv in ((0, dq_own), (DA, _rope_t(dk, *tabs)), (2 * DA, dv)):
                if final:
                    cb = base // DH + h
                    gv = gv + nat_ref[0, cb] + nat_ref[1, cb]
                out_ref[:, base + h * DH:base + (h + 1) * DH] = gv.astype(out_dtype)

    def cur(rows, width, col=0):
        return pl.BlockSpec((None, rows, width), lambda r, i: (r, i, col))

    def nx(width, col=0):
        return pl.BlockSpec((None, KB, width), lambda r, i: (r, nxt(i), col))

    in_specs = [cur(BQ, DA), nx(DA), cur(BQ, DA, 1), cur(BQ, DA, 2), cur(BQ, DA), nx(DA),
                cur(BQ, LANES), nx(LANES), cur(BQ, LANES), nx(LANES)]
    in_specs += [pl.BlockSpec((S, DH), lambda r, i: (0, 0), pipeline_mode=pl.Buffered(1))] * 2
    args = [h16, h16, h16, h16, do, do, lse, lse, dd, dd] + list(rope)
    scratch = [pltpu.VMEM((KB, DA), F32), pltpu.VMEM((qw, DA), BF16), pltpu.VMEM((qw, DA), BF16),
               pltpu.VMEM((qw, LANES), F32), pltpu.VMEM((qw, LANES), F32)]
    if final:
        assert d == 1
        in_specs += [pl.BlockSpec((dil, BQ // dil, 3 * DA), lambda r, i: (0, i, 0)) for dil in DILATIONS[1:]]
        in_specs.append(pl.BlockSpec(memory_space=pl.ANY))
        args += list(others) + [dh[None]]
        scratch.append(pltpu.VMEM((2, n_cb, BQ, DH), F32))
    outs, carried = _call_carrying(
        body, rider, (d, n_i), in_specs, [cur(BQ, 3 * DA)], [SDS((d, n_sub, D_IN if final else 3 * DA), out_dtype)],
        scratch, args, name, {len(args) - 1: 0} if final else {})
    return outs[0] if rider is None else (outs[0], carried)


def _grad_w_out(yt, dzb):
    tn = 256

    def body(yt_ref, dz_ref, o_ref, o16_ref):
        acc = _dot(yt_ref[...], dz_ref[...])
        o_ref[...] = acc
        o16_ref[...] = acc.astype(BF16)

    cols = pl.BlockSpec((D, tn), lambda c: (0, c))
    o, o16 = pl.pallas_call(
        body, name="grad_w_out", grid=(D // tn,),
        in_specs=[pl.BlockSpec((D, S), lambda c: (0, 0), pipeline_mode=pl.Buffered(1)),
                  pl.BlockSpec((S, tn), lambda c: (0, c))],
        out_specs=[cols, cols],
        out_shape=[SDS((D, D), F32), SDS((D, D), BF16)],
        compiler_params=_params(("arbitrary",)),
    )(yt, dzb)
    return o.reshape(N_DEV, OB, D), o16.reshape(N_DEV, OB, D)


def _grad_w_in(xt, dh, order):
    n_send = N_DEV // 2
    n_steps = D_IN // TN
    own0 = n_send * W_PHASES
    sums0 = own0 + W_PHASES
    step_of_send = [W_PHASES * k + W_PHASES + 1 for k in range(n_send)]

    def body(ord_ref, xt_ref, dh_ref, o_ref, o16_ref, r1_ref, s2_ref, st_ref, got_ref,
             piece_sems, send_sems, recv_sems, got_sem):
        c = pl.program_id(0)
        x, y, core, _ = _place()

        def piece(cc):
            blk, sub = ord_ref[cc // W_PHASES], cc % W_PHASES
            return pltpu.make_async_copy(st_ref.at[cc % 2], o16_ref.at[blk, :, pl.ds(sub * TN, TN)], piece_sems.at[cc % 2])

        def send(k):
            return pltpu.make_async_remote_copy(
                src_ref=o16_ref.at[ord_ref[k]], dst_ref=r1_ref.at[k], send_sem=send_sems.at[k], recv_sem=recv_sems.at[k],
                device_id=(x, y, 1 - core), device_id_type=MESH)

        def sibling_part(cc):
            k, sub = cc // W_PHASES - n_send, cc % W_PHASES
            return pltpu.make_async_copy(r1_ref.at[k, :, pl.ds(sub * TN, TN)], got_ref, got_sem)

        @pl.when((c >= 2) & (c < own0 + 2))
        def _():
            piece(c - 2).wait()

        for k in range(n_send):
            pl.when(c == step_of_send[k])(lambda k=k: send(k).start())
        for k in range(1, n_send):
            pl.when(c == own0 + W_PHASES * k)(lambda k=k: send(k).wait_recv())
        pl.when(c >= sums0)(lambda: sibling_part(c).start())

        acc = _dot(xt_ref[...], dh_ref[...])

        @pl.when((c >= own0) & (c < sums0))
        def _():
            o_ref[...] = acc

        @pl.when(c < own0)
        def _():
            st_ref[c % 2] = acc.astype(BF16)
            piece(c).start()

        @pl.when(c >= sums0)
        def _():
            sibling_part(c).wait()
            s2_ref[...] = (acc + got_ref[...].astype(F32)).astype(BF16)

        @pl.when(c == n_steps - 1)
        def _():
            send(0).wait_recv()
            for k in range(n_send):
                send(k).wait_send()

    col = lambda c, ordr: W_PHASES * ordr[c // W_PHASES] + c % W_PHASES
    hbm = pl.BlockSpec(memory_space=pl.ANY)
    sums_blk = lambda c, ordr: (jnp.maximum(c // W_PHASES - (n_send + 1), 0), 0, jnp.where(c >= sums0, c % W_PHASES, 0))
    partial, _, got, sums = pl.pallas_call(
        body, name="grad_w_in",
        grid_spec=pltpu.PrefetchScalarGridSpec(
            num_scalar_prefetch=1, grid=(n_steps,),
            in_specs=[pl.BlockSpec((D, S), lambda c, ordr: (0, 0), pipeline_mode=pl.Buffered(1)),
                      pl.BlockSpec((S, TN), lambda c, ordr: (0, col(c, ordr)))],
            out_specs=[pl.BlockSpec((None, D, TN), lambda c, ordr: (0, 0, jnp.clip(c - own0, 0, W_PHASES - 1))), hbm, hbm,
                       pl.BlockSpec((None, D, TN), sums_blk)],
            scratch_shapes=[pltpu.VMEM((2, D, TN), BF16), pltpu.VMEM((D, TN), BF16), pltpu.SemaphoreType.DMA((2,)),
                            pltpu.SemaphoreType.DMA((n_send,)), pltpu.SemaphoreType.DMA((n_send,)),
                            pltpu.SemaphoreType.DMA(())]),
        out_shape=[SDS((1, D, WB), F32), SDS((N_DEV, D, WB), BF16), SDS((n_send, D, WB), BF16),
                   SDS((n_send - 1, D, WB), BF16)],
        compiler_params=_params(("arbitrary",)),
    )(order, xt, dh)
    return partial, got, sums


def _grad_x(dz, dh, wgs, rider=None):
    tm = 1024

    def body(dz_ref, dh_ref, *rest):
        o_ref = rest[-1]

        @pl.when(pl.program_id(1) == 0)
        def _():
            o_ref[...] = ALPHA * dz_ref[...]

        acc = _dot(dh_ref[:, 0:TN], rest[0][...], NT)
        for s in range(1, W_PHASES):
            acc = acc + _dot(dh_ref[:, s * TN:(s + 1) * TN], rest[s][...], NT)
        o_ref[...] += acc

    outs, carried = _call_carrying(
        body, rider, (S // tm, N_DEV),
        [pl.BlockSpec((tm, D), lambda i, j: (i, 0)), pl.BlockSpec((tm, WB), lambda i, j: (i, j))]
        + [pl.BlockSpec((None, D, TN), lambda i, j: (j, 0, 0))] * W_PHASES,
        [pl.BlockSpec((tm, D), lambda i, j: (i, 0))], [SDS((S, D), F32)], [], [dz, dh] + list(wgs), "grad_x")
    return outs[0], carried


def _place():
    x, y, c = lax.axis_index("x"), lax.axis_index("y"), lax.axis_index("c")
    chips = [(x, y), (1 - x, y), (x, 1 - y), (1 - x, 1 - y)]
    return x, y, c, chips


def _blk(x, y, c):
    return 4 * x + 2 * y + c


def _adamw(w, g, m, v):
    m = ADAM_B1 * m + (1.0 - ADAM_B1) * g
    v = ADAM_B2 * v + (1.0 - ADAM_B2) * (g * g)
    m_hat = m / (1.0 - ADAM_B1 ** ADAM_STEP)
    v_hat = v / (1.0 - ADAM_B2 ** ADAM_STEP)
    delta = -ADAM_LR * (m_hat / (jnp.sqrt(v_hat) + ADAM_EPS) + ADAM_WD * w)
    return delta, m, v


class _Rider:
    def __init__(self, args, in_specs, out_shape, out_specs, scratch, start, finish):
        self.args, self.in_specs, self.out_shape, self.out_specs = args, in_specs, out_shape, out_specs
        self.scratch, self.start, self.finish = scratch, start, finish


def _carry(body, rider, n_in, n_out, first, last):
    if rider is None:
        return body
    r_in, r_out, r_scr = len(rider.args), len(rider.out_shape), len(rider.scratch)

    def carrying(*refs):
        o0 = n_in + r_in
        s0 = o0 + n_out + r_out
        s1 = len(refs) - r_scr
        theirs = (refs[n_in:o0], refs[o0 + n_out:s0], refs[s1:])
        pl.when(first())(lambda: rider.start(*theirs))
        body(*refs[:n_in], *refs[o0:o0 + n_out], *refs[s0:s1])
        pl.when(last())(lambda: rider.finish(*theirs))

    return carrying


def _call_carrying(body, rider, grid, in_specs, out_specs, out_shape, scratch, args, name, aliases=None):
    n_in, n_out = len(in_specs), len(out_specs)
    ids = lambda: [pl.program_id(a) for a in range(len(grid))]
    first = lambda: functools.reduce(jnp.logical_and, [i == 0 for i in ids()])
    last = lambda: functools.reduce(jnp.logical_and, [i == n - 1 for i, n in zip(ids(), grid)])
    if rider is not None:
        in_specs, args = in_specs + rider.in_specs, list(args) + rider.args
        out_specs, out_shape = out_specs + rider.out_specs, out_shape + rider.out_shape
        scratch = scratch + rider.scratch
    outs = pl.pallas_call(
        _carry(body, rider, n_in, n_out, first, last), name=name, grid=grid,
        in_specs=in_specs, out_specs=out_specs, out_shape=out_shape, scratch_shapes=scratch,
        input_output_aliases=aliases or {}, compiler_params=_params(("arbitrary",) * len(grid)),
    )(*args)
    return list(outs[:n_out]), list(outs[n_out:])


def _gather_copy(tensors, send_sems, recv_sems, t, k, block, to, src=None):
    dst = tensors[t][1](_blk(*block))
    return pltpu.make_async_remote_copy(
        src_ref=dst if src is None else src, dst_ref=dst,
        send_sem=send_sems.at[t, k], recv_sem=recv_sems.at[t, k], device_id=to, device_id_type=MESH)


def _gather_start(tensors, send_sems, recv_sems, local_sems):
    x, y, c, chips = _place()
    me, sib = (x, y, c), (x, y, 1 - c)
    for t, (src, dst) in enumerate(tensors):
        pltpu.make_async_copy(src, dst(_blk(*me)), local_sems.at[t]).start()
        _gather_copy(tensors, send_sems, recv_sems, t, 0, me, sib, src).start()
        for j in (1, 2, 3):
            _gather_copy(tensors, send_sems, recv_sems, t, j, me, (*chips[j], c), src).start()


def _gather_finish(tensors, send_sems, recv_sems, local_sems):
    x, y, c, chips = _place()
    me, sib = (x, y, c), (x, y, 1 - c)
    copy = functools.partial(_gather_copy, tensors, send_sems, recv_sems)
    for t in range(len(tensors)):
        for j in (1, 2, 3):
            copy(t, j, (*chips[j], c), me).wait_recv()
            copy(t, 3 + j, (*chips[j], c), sib).start()
    for t, (src, dst) in enumerate(tensors):
        copy(t, 0, sib, me).wait_recv()
        for j in (1, 2, 3):
            copy(t, 3 + j, (*chips[j], 1 - c), me).wait_recv()
        copy(t, 0, me, sib, src).wait_send()
        for j in (1, 2, 3):
            copy(t, j, me, (*chips[j], c), src).wait_send()
            copy(t, 3 + j, (*chips[j], c), sib).wait_send()
        pltpu.make_async_copy(src, dst(_blk(*me)), local_sems.at[t]).wait()


W_PHASES = WB // TN


def _gather_first_rider(w_in_s, w_out_s, w_pool_s):
    shapes = [w_in_s.shape, w_out_s.shape, w_pool_s.shape]

    def keeps(outs, scr):
        return [pltpu.make_async_copy(scr[n], outs[n], scr[6].at[n]) for n in range(3)]

    def tensors(outs, scr):
        return [(scr[0].at[:, pl.ds(0, TN)], lambda b: outs[3].at[b])]

    def start(ins, outs, scr):
        for n in range(3):
            scr[n][...] = ins[n][...].astype(BF16)
        for keep in keeps(outs, scr):
            keep.start()
        _gather_start(tensors(outs, scr), *scr[3:6])

    def finish(ins, outs, scr):
        _gather_finish(tensors(outs, scr), *scr[3:6])
        for keep in keeps(outs, scr):
            keep.wait()

    vmem = pl.BlockSpec(memory_space=pltpu.VMEM)
    hbm = pl.BlockSpec(memory_space=pl.ANY)
    return _Rider(
        args=[w_in_s, w_out_s, w_pool_s], in_specs=[vmem] * 3,
        out_shape=[SDS(sh, BF16) for sh in shapes] + [SDS((N_DEV, D, TN), BF16)], out_specs=[hbm] * 4,
        scratch=[pltpu.VMEM(sh, BF16) for sh in shapes]
        + [pltpu.SemaphoreType.DMA((1, 7)), pltpu.SemaphoreType.DMA((1, 7)), pltpu.SemaphoreType.DMA((1,)),
           pltpu.SemaphoreType.DMA((3,))],
        start=start, finish=finish)


def _gather_w_in_rider(shard16, s):
    def tensors(ins, outs):
        return [(ins[0].at[:, pl.ds(s * TN, TN)], lambda b: outs[0].at[b])]

    hbm = pl.BlockSpec(memory_space=pl.ANY)
    return _Rider(
        args=[shard16], in_specs=[hbm], out_shape=[SDS((N_DEV, D, TN), BF16)], out_specs=[hbm],
        scratch=[pltpu.SemaphoreType.DMA((1, 7)), pltpu.SemaphoreType.DMA((1, 7)), pltpu.SemaphoreType.DMA((1,))],
        start=lambda ins, outs, scr: _gather_start(tensors(ins, outs), *scr),
        finish=lambda ins, outs, scr: _gather_finish(tensors(ins, outs), *scr))


def _gather_small_rider(w_out16, w_pool16):
    def tensors(ins, outs):
        gout_ref, gpool_ref = outs

        def pool_rows(b):
            return gpool_ref.at[:, pl.ds(pl.multiple_of(b * PB, PB), PB), :]

        return [(ins[0], lambda b: gout_ref.at[b]), (ins[1], pool_rows)]

    hbm = pl.BlockSpec(memory_space=pl.ANY)
    return _Rider(
        args=[w_out16, w_pool16], in_specs=[hbm, hbm],
        out_shape=[SDS((N_DEV, OB, D), BF16), SDS((N_GROUPS, GC, GC), BF16)], out_specs=[hbm, hbm],
        scratch=[pltpu.SemaphoreType.DMA((2, 7)), pltpu.SemaphoreType.DMA((2, 7)), pltpu.SemaphoreType.DMA((2,))],
        start=lambda ins, outs, scr: _gather_start(tensors(ins, outs), *scr),
        finish=lambda ins, outs, scr: _gather_finish(tensors(ins, outs), *scr))


def _block_table():
    x, y, c, chips = _place()
    return jnp.stack([_blk(*chip, c) for chip in chips]).astype(jnp.int32)


def _sibling_first_order():
    x, y, c, chips = _place()
    return jnp.stack([_blk(*chip, side) for side in (1 - c, c) for chip in chips]).astype(jnp.int32)


def _exchange_rider(p16):
    _, r_tot, cols = p16.shape

    def copies(ins, outs, scr):
        x, y, c, chips = _place()
        return [pltpu.make_async_remote_copy(
            src_ref=ins[0].at[_blk(*chips[k], 1 - c)], dst_ref=outs[0].at[k],
            send_sem=scr[0].at[k], recv_sem=scr[1].at[k], device_id=(x, y, 1 - c), device_id_type=MESH)
            for k in range(4)]

    def start(ins, outs, scr):
        for cp in copies(ins, outs, scr):
            cp.start()

    def finish(ins, outs, scr):
        for cp in copies(ins, outs, scr):
            cp.wait()

    hbm = pl.BlockSpec(memory_space=pl.ANY)
    return _Rider([p16], [hbm], [SDS((4, r_tot, cols), BF16)], [hbm],
                  [pltpu.SemaphoreType.DMA((4,)), pltpu.SemaphoreType.DMA((4,))], start, finish)


def _chip_sums(name, table, p32, r1, rc):
    _, r_tot, cols = p32.shape

    def body(tbl_ref, p_ref, r_ref, o_ref):
        o_ref[...] = (p_ref[...] + r_ref[...].astype(F32)).astype(BF16)

    return pl.pallas_call(
        body, name=name,
        grid_spec=pltpu.PrefetchScalarGridSpec(
            num_scalar_prefetch=1, grid=(3, r_tot // rc),
            in_specs=[pl.BlockSpec((None, rc, cols), lambda k, ch, tbl: (tbl[k + 1], ch, 0)),
                      pl.BlockSpec((None, rc, cols), lambda k, ch, tbl: (k + 1, ch, 0))],
            out_specs=pl.BlockSpec((None, rc, cols), lambda k, ch, tbl: (k, ch, 0))),
        out_shape=SDS((3, r_tot, cols), BF16),
        compiler_params=_params(("arbitrary", "arbitrary")),
    )(table, p32, r1)


def _stage2_rider(sums, stats=None):
    n_t = len(sums)

    def copies(ins, outs, scr):
        x, y, c, chips = _place()
        out = []
        for t in range(n_t):
            for k in (1, 2, 3):
                out.append(pltpu.make_async_remote_copy(
                    src_ref=ins[t].at[k - 1], dst_ref=outs[t].at[k - 1],
                    send_sem=scr[0].at[t, k - 1], recv_sem=scr[1].at[t, k - 1],
                    device_id=(*chips[k], c), device_id_type=MESH))
        if stats is not None:
            for k in range(1, N_DEV):
                peer = (x ^ ((k >> 2) & 1), y ^ ((k >> 1) & 1), c ^ (k & 1))
                out.append(pltpu.make_async_remote_copy(
                    src_ref=scr[4], dst_ref=outs[n_t].at[_blk(x, y, c)],
                    send_sem=scr[2].at[k - 1], recv_sem=scr[3].at[k - 1], device_id=peer, device_id_type=MESH))
        return out

    def own_rows(outs, scr):
        x, y, c, _ = _place()
        return pltpu.make_async_copy(scr[4], outs[n_t].at[_blk(x, y, c)], scr[5])

    def start(ins, outs, scr):
        if stats is not None:
            scr[4][...] = ins[n_t][...]
            scr[4][3:4, 0:DP] = ins[n_t + 1][0:1, :]
            own_rows(outs, scr).start()
        for cp in copies(ins, outs, scr):
            cp.start()

    def finish(ins, outs, scr):
        for cp in copies(ins, outs, scr):
            cp.wait()
        if stats is not None:
            own_rows(outs, scr).wait()

    vmem = pl.BlockSpec(memory_space=pltpu.VMEM)
    hbm = pl.BlockSpec(memory_space=pl.ANY)
    scratch = [pltpu.SemaphoreType.DMA((n_t, 3)), pltpu.SemaphoreType.DMA((n_t, 3))]
    args, in_specs = list(sums), [hbm] * n_t
    out_shape, out_specs = [SDS(s.shape, BF16) for s in sums], [hbm] * n_t
    if stats is not None:
        scratch += [pltpu.SemaphoreType.DMA((N_DEV - 1,)), pltpu.SemaphoreType.DMA((N_DEV - 1,)),
                    pltpu.VMEM((8, D), F32), pltpu.SemaphoreType.DMA(())]
        args, in_specs = args + list(stats), in_specs + [vmem, vmem]
        out_shape, out_specs = out_shape + [SDS((N_DEV, 8, D), F32)], out_specs + [hbm]
    return _Rider(args, in_specs, out_shape, out_specs, scratch, start, finish)


def _adamw_shard(name, table, p32, r1, r2, w, m, v, rc):
    _, r_tot, cols = p32.shape

    def body(tbl_ref, p_ref, r1_ref, r2_ref, w_ref, m_ref, v_ref, g_ref, d_ref, nm_ref, nv_ref):
        g = p_ref[...] + r1_ref[...].astype(F32)
        for k in range(3):
            g = g + r2_ref[k].astype(F32)
        delta, nm, nv = _adamw(w_ref[...], g, m_ref[...], v_ref[...])
        g_ref[...] = g
        d_ref[...] = delta
        nm_ref[...] = nm
        nv_ref[...] = nv

    rows = pl.BlockSpec((rc, cols), lambda ch, tbl: (ch, 0))
    shard = SDS((r_tot, cols), F32)
    return pl.pallas_call(
        body, name=name,
        grid_spec=pltpu.PrefetchScalarGridSpec(
            num_scalar_prefetch=1, grid=(r_tot // rc,),
            in_specs=[pl.BlockSpec((None, rc, cols), lambda ch, tbl: (tbl[0], ch, 0)),
                      pl.BlockSpec((None, rc, cols), lambda ch, tbl: (0, ch, 0)),
                      pl.BlockSpec((3, rc, cols), lambda ch, tbl: (0, ch, 0)), rows, rows, rows],
            out_specs=[rows, rows, rows, rows]),
        out_shape=[shard, shard, shard, shard],
        compiler_params=_params(("arbitrary",)),
    )(table, p32, r1, r2, w, m, v)


def _replicated_adamw(gathered, gain, bias, scale, m_gain, m_bias, m_scale, v_gain, v_bias, v_scale):
    def body(all_ref, s_ref, g_ref, b_ref, ms_ref, mg_ref, mb_ref, vs_ref, vg_ref, vb_ref, loss_ref, *outs):
        tot = all_ref[0]
        for b in range(1, N_DEV):
            tot = tot + all_ref[b]
        loss_ref[...] = tot[2:3, 0:LANES]
        for n, (row, width, w_r, m_r, v_r) in enumerate(((3, DP, s_ref, ms_ref, vs_ref), (0, D, g_ref, mg_ref, vg_ref),
                                                         (1, D, b_ref, mb_ref, vb_ref))):
            g = tot[row:row + 1, 0:width]
            delta, nm, nv = _adamw(w_r[...], g, m_r[...], v_r[...])
            for out, val in zip(outs[4 * n:4 * n + 4], (g, delta, nm, nv)):
                out[...] = val

    vmem = pl.BlockSpec(memory_space=pltpu.VMEM)
    shapes = [SDS((1, LANES), F32)] + [SDS((1, width), F32) for width in (DP, D, D) for _ in range(4)]
    return pl.pallas_call(
        body, name="replicated_adamw",
        in_specs=[vmem] * 10, out_specs=[vmem] * len(shapes), out_shape=shapes,
        compiler_params=_params(),
    )(gathered, scale, gain, bias, m_scale, m_gain, m_bias, v_scale, v_gain, v_bias)


def kernel(x, w_in, w_pool, pool_scale, w_out, ln_gain, ln_bias, loss_target, m_w_in, m_w_pool, m_pool_scale, m_w_out, m_ln_gain, m_ln_bias, v_w_in, v_w_pool, v_pool_scale, v_w_out, v_ln_gain, v_ln_bias):
    pool_rows = (N_GROUPS * PB, GC)
    x2, target = x[0], loss_target[0]
    table = _block_table()

    (xb, xt, *rope), (shard16, w_out16, w_pool16, wg0) = _prep_x(
        x2, _rope_rates(), _gather_first_rider(w_in[0], w_out[0], w_pool[0]))
    h, (wg1, wg2, wg_out, wg_pool) = _proj_all(xb, rope, wg0, shard16, w_out16, w_pool16)
    wg_out = wg_out.reshape(D, D)
    h16, hug = [h[0][None], h[1], h[2]], h[3]
    fwd = [_attn_fwd(h16[n], f"attn_fwd_d{d}") for n, d in enumerate(DILATIONS)]
    y, yt, attn, lse1, lse4, lse16 = _mix(fwd[0][0][0], fwd[0][1][0], *fwd[1], *fwd[2], hug, wg_pool, pool_scale)
    dz, dzb, stats = _outproj_ln(y, wg_out, x2, target, ln_gain, ln_bias)

    dwout, dwout16 = _grad_w_out(yt, dzb)
    mid, (r1_out,) = _bwd_mid(dzb, wg_out, hug, attn, wg_pool, pool_scale, rider=_exchange_rider(dwout16))
    dh, do1, do4, do16, dd1, dd4, dd16, dwp, dwp16, gps = mid
    s2_out = _chip_sums("rs_out_sums", table, dwout, r1_out, OB)

    do, lse, dd = [do1[None], do4, do16], [lse1[None], lse4, lse16], [dd1[None], dd4, dd16]
    acc4, (r2_out,) = _attn_bwd(h16[1], do[1], lse[1], dd[1], rope, "attn_bwd_d4", rider=_stage2_rider([s2_out]))
    g_out, d_out, nm_out, nv_out = _adamw_shard(
        "adamw_w_out", table, dwout, r1_out, r2_out, w_out[0], m_w_out[0], v_w_out[0], OB // 2)
    acc16, (r1_pool,) = _attn_bwd(h16[2], do[2], lse[2], dd[2], rope, "attn_bwd_d16", rider=_exchange_rider(dwp16))
    s2_pool = _chip_sums("rs_pool_sums", table, dwp, r1_pool, N_GROUPS * PB)
    dh = _attn_bwd(h16[0], do[0], lse[0], dd[0], rope, "attn_bwd_d1", others=(acc4, acc16), dh=dh)[0]

    dwin, r1_in, s2_in = _grad_w_in(xt, dh, _sibling_first_order())
    grad_x, (r2_in, r2_pool, gathered) = _grad_x(
        dz, dh, (wg0, wg1, wg2), rider=_stage2_rider([s2_in, s2_pool], stats=(stats, gps)))
    g_in, d_in, nm_in, nv_in = _adamw_shard(
        "adamw_w_in", jnp.zeros_like(table), dwin, r1_in, r2_in, w_in[0], m_w_in[0], v_w_in[0], 512)
    g_pool, d_pool, nm_pool, nv_pool = _adamw_shard(
        "adamw_w_pool", table, dwp, r1_pool, r2_pool, w_pool[0].reshape(pool_rows), m_w_pool[0].reshape(pool_rows),
        v_w_pool[0].reshape(pool_rows), N_GROUPS * PB)
    loss, *small = _replicated_adamw(gathered, ln_gain, ln_bias, pool_scale, m_ln_gain, m_ln_bias, m_pool_scale,
                                     v_ln_gain, v_ln_bias, v_pool_scale)
    (g_ps, d_ps, nm_ps, nv_ps, g_gain, d_gain, nm_gain, nv_gain, g_bias, d_bias, nm_bias, nv_bias) = small

    shard4 = lambda t: t.reshape(1, N_GROUPS, PB, GC)
    lead = lambda t: t[None]
    return (loss[0, 0], lead(grad_x),
            lead(g_in), shard4(g_pool), g_ps, lead(g_out), g_gain, g_bias,
            lead(d_in), shard4(d_pool), d_ps, lead(d_out), d_gain, d_bias,
            lead(nm_in), shard4(nm_pool), nm_ps, lead(nm_out), nm_gain, nm_bias,
            lead(nv_in), shard4(nv_pool), nv_ps, lead(nv_out), nv_gain, nv_bias)
```

```python
import functools

import jax
import jax.numpy as jnp
from jax import lax
from jax.experimental import pallas as pl
from jax.experimental.pallas import tpu as pltpu

F32 = jnp.float32
BF16 = jnp.bfloat16
SDS = jax.ShapeDtypeStruct
MESH = pl.DeviceIdType.MESH

N_DEV = 8
S = 4096
D = 2048
N_HEADS = 8
DH = 128
DA = N_HEADS * DH
DP = 1024
N_GROUPS = 4
GC = DP // N_GROUPS
POOL_WINDOWS = (2, 4, 8, 16)
HALO = 16
D_IN = 3 * DA + DP + D
WB = D_IN // N_DEV
TN = 256
HW = 3 * DA
OB = D // N_DEV
PB = GC // N_DEV
ROPE_DIM = DH // 4
ROPE_HALF = ROPE_DIM // 2
ROPE_THETA = 500000.0
DILATIONS = (1, 4, 16)
KB = 128
LN_EPS = 1e-5
ALPHA = 2.0 ** 0.25
SCALE = DH ** -0.5
NEG = -1e30
ADAM_LR, ADAM_B1, ADAM_B2, ADAM_EPS, ADAM_WD, ADAM_STEP = 0.001, 0.9, 0.999, 1e-08, 0.01, 10

VMEM_LIMIT_V7X = 61 * 1024 * 1024

NT = (((1,), (1,)), ((), ()))
T_N = (((0,), (0,)), ((), ()))


def _params(sem=None):
    return pltpu.CompilerParams(dimension_semantics=sem, vmem_limit_bytes=VMEM_LIMIT_V7X)


def _dot(a, b, dims=None):
    if dims is None:
        return jnp.dot(a, b, preferred_element_type=F32)
    return lax.dot_general(a, b, dims, preferred_element_type=F32)


def _rope_rates():
    inv_freq = ROPE_THETA ** (-(2.0 * jnp.arange(ROPE_HALF, dtype=F32)) / ROPE_DIM)
    return jnp.concatenate([inv_freq, inv_freq, jnp.zeros((DH - ROPE_DIM,), F32)])[None, :]


def _rope_table_rows(rates, first_row, n):
    pos = (first_row + lax.broadcasted_iota(jnp.int32, (n, DH), 0)).astype(F32)
    lane = lax.broadcasted_iota(jnp.int32, (n, DH), 1)
    ang = pos * rates
    sin = jnp.sin(ang)
    c = jnp.where(lane < ROPE_DIM, jnp.cos(ang), 1.0)
    sn = jnp.where(lane < ROPE_HALF, -sin, jnp.where(lane < ROPE_DIM, sin, 0.0))
    return c, sn


def _rope_partner(t):
    lane = lax.broadcasted_iota(jnp.int32, t.shape, 1)
    return jnp.where(lane < ROPE_HALF, pltpu.roll(t, DH - ROPE_HALF, 1), pltpu.roll(t, ROPE_HALF, 1))


def _rope(t, c, sn):
    return t * c + _rope_partner(t) * sn


def _rope_t(g, c, sn):
    return g * c - _rope_partner(g) * sn


def _prep_x(x, rates, rider):
    tm = 512

    def body(x_ref, f_ref, xb_ref, xt_ref, c_ref, sn_ref):
        xv = x_ref[...]
        xb_ref[...] = xv.astype(BF16)
        xt_ref[...] = xv.T.astype(BF16)
        c_ref[...], sn_ref[...] = _rope_table_rows(f_ref[...], pl.program_id(0) * tm, tm)

    row = lambda i: (i, 0)
    return _call_carrying(
        body, rider, (S // tm,), [pl.BlockSpec((tm, D), row), pl.BlockSpec((1, DH), lambda i: (0, 0))],
        [pl.BlockSpec((tm, D), row), pl.BlockSpec((D, tm), lambda i: (0, i)),
         pl.BlockSpec((tm, DH), row), pl.BlockSpec((tm, DH), row)],
        [SDS((S, D), BF16), SDS((D, S), BF16), SDS((S, DH), F32), SDS((S, DH), F32)], [], [x, rates], "prep_x")


def _residues(slab_ref, idx, d, r, n):
    return slab_ref[(*idx, pl.ds(r, n, stride=d), slice(None))]


def _proj_all(xb, rope, wg0, shard16, w_out16, w_pool16):
    n_heads = TN // DH
    n16 = HW // TN
    j16 = n16 // W_PHASES
    n_rope = 2 * DA // TN
    once = pl.Buffered(1)
    riders = [_gather_w_in_rider(shard16, s) for s in range(1, W_PHASES)] + [_gather_small_rider(w_out16, w_pool16)]
    last_j = N_DEV - 1

    def body(*refs):
        x_ref, c_ref, sn_ref, g0_ref = refs[:4]
        pos = 4
        r_ins, r_outs, r_scrs = [], [], []
        for r in riders:
            r_ins.append(refs[pos:pos + len(r.args)])
            pos += len(r.args)
        h16_ref, h4_ref, h16r_ref, hug_ref = refs[pos:pos + 4]
        pos += 4
        for r in riders:
            r_outs.append(refs[pos:pos + len(r.out_shape)])
            pos += len(r.out_shape)
        slab_ref, res_ref, w_ref, w_sems = refs[pos:pos + 4]
        pos += 4
        for r in riders:
            r_scrs.append(refs[pos:pos + len(r.scratch)])
            pos += len(r.scratch)
        gathered = [g0_ref] + [r_outs[s][0] for s in range(W_PHASES - 1)]
        s, j = pl.program_id(0), pl.program_id(1)
        d4, d16 = DILATIONS[1], DILATIONS[2]

        def fetch(ph, blk, slot):
            return pltpu.make_async_copy(gathered[ph].at[blk], w_ref.at[slot], w_sems.at[slot])

        pl.when((s == 0) & (j == 0))(lambda: fetch(0, 0, 0).start())
        for ph in range(W_PHASES):
            pl.when((s == ph) & (j == 0))(lambda ph=ph: riders[ph].start(r_ins[ph], r_outs[ph], r_scrs[ph]))
            pl.when(s == ph)(lambda ph=ph: fetch(ph, j, j % 2).wait())
            pl.when((s == ph) & (j == last_j))(lambda ph=ph: riders[ph].finish(r_ins[ph], r_outs[ph], r_scrs[ph]))
            pl.when((s == ph) & (j < last_j))(lambda ph=ph: fetch(ph, j + 1, (j + 1) % 2).start())
            if ph + 1 < W_PHASES:
                pl.when((s == ph) & (j == last_j))(lambda ph=ph: fetch(ph + 1, 0, 0).start())

        acc = _dot(x_ref[...], w_ref[j % 2])

        def heads(with_rope):
            for hh in range(n_heads):
                hs = slice(hh * DH, (hh + 1) * DH)
                t = _rope(acc[:, hs], c_ref[...], sn_ref[...]) if with_rope else acc[:, hs]
                h16_ref[:, hs] = t.astype(BF16)
                slab_ref[...] = t
                for q in range(d4):
                    t4 = _residues(slab_ref, (), d4, q, S // d4)
                    h4_ref[q, :, hs] = t4.astype(BF16)
                    res_ref[...] = t4
                    for m in range(d16 // d4):
                        h16r_ref[d4 * m + q, :, hs] = _residues(res_ref, (), d4, m, S // d16).astype(BF16)

        pl.when(W_PHASES * j + s < n_rope)(lambda: heads(True))
        pl.when((W_PHASES * j + s >= n_rope) & (j < j16))(lambda: heads(False))

        @pl.when(j >= j16)
        def _():
            hug_ref[...] = acc.astype(BF16)

    col16 = lambda s, j: W_PHASES * jnp.minimum(j, j16 - 1) + s
    colug = lambda s, j: W_PHASES * jnp.maximum(j, j16) + s - n16
    hbm = pl.BlockSpec(memory_space=pl.ANY)
    resident = lambda shape: pl.BlockSpec(shape, lambda s, j: (0, 0), pipeline_mode=once)
    in_specs = [resident((S, D)), resident((S, DH)), resident((S, DH)), hbm]
    args = [xb, *rope, wg0]
    out_specs = [pl.BlockSpec((S, TN), lambda s, j: (0, col16(s, j)))]
    out_specs += [pl.BlockSpec((d, S // d, TN), lambda s, j: (0, 0, col16(s, j))) for d in DILATIONS[1:]]
    out_specs += [pl.BlockSpec((S, TN), lambda s, j: (0, colug(s, j)))]
    out_shape = [SDS((S, HW), BF16)] + [SDS((d, S // d, HW), BF16) for d in DILATIONS[1:]] + [SDS((S, HW), BF16)]
    scratch = [pltpu.VMEM((S, DH), F32), pltpu.VMEM((S // DILATIONS[1], DH), F32), pltpu.VMEM((2, D, TN), BF16),
               pltpu.SemaphoreType.DMA((2,))]
    for r in riders:
        in_specs, args = in_specs + r.in_specs, args + r.args
        out_specs, out_shape = out_specs + r.out_specs, out_shape + r.out_shape
        scratch = scratch + r.scratch
    outs = pl.pallas_call(
        body, name="proj_all", grid=(W_PHASES, N_DEV),
        in_specs=in_specs, out_specs=out_specs, out_shape=out_shape, scratch_shapes=scratch,
        compiler_params=_params(("arbitrary", "arbitrary")),
    )(*args)
    return list(outs[:4]), list(outs[4:])


BQ = 2 * KB
LANES = 128


def _to_lane(acc, col, h):
    lane = lax.broadcasted_iota(jnp.int32, acc.shape, 1)
    return jnp.where(lane == h, col, acc)


def _attn_fwd(h16, name):
    d, n_sub, _ = h16.shape
    n_i = n_sub // BQ
    kw = KB + BQ

    def body(q_ref, kc_ref, vc_ref, o_ref, l_ref, kw_ref, vw_ref):
        i = pl.program_id(1)

        @pl.when(i == 0)
        def _():
            kw_ref[0:KB, :] = jnp.zeros((KB, DA), BF16)
            vw_ref[0:KB, :] = jnp.zeros((KB, DA), BF16)

        @pl.when(i > 0)
        def _():
            kw_ref[0:KB, :] = kw_ref[BQ:kw, :]
            vw_ref[0:KB, :] = vw_ref[BQ:kw, :]

        kw_ref[KB:kw, :] = kc_ref[...]
        vw_ref[KB:kw, :] = vc_ref[...]
        a = lax.broadcasted_iota(jnp.int32, (KB, 2 * KB), 0)
        b = lax.broadcasted_iota(jnp.int32, (KB, 2 * KB), 1)
        band = (b >= a) & (b <= a + KB)
        first_key = jnp.where(i == 0, KB, 0)
        masks = (band & (b >= first_key), band)
        for half in range(2):
            rs = slice(half * KB, (half + 1) * KB)
            ks = slice(half * KB, (half + 2) * KB)
            lse = jnp.zeros((KB, LANES), F32)
            for h in range(N_HEADS):
                hs = slice(h * DH, (h + 1) * DH)
                s = jnp.where(masks[half], _dot(q_ref[rs, hs], kw_ref[ks, hs], NT) * SCALE, NEG)
                m = jnp.max(s, axis=1, keepdims=True)
                p = jnp.exp(s - m)
                den = jnp.sum(p, axis=1, keepdims=True)
                o_ref[rs, hs] = (_dot(p.astype(BF16), vw_ref[ks, hs]) / den).astype(BF16)
                lse = _to_lane(lse, m + jnp.log(den), h)
            l_ref[rs, :] = lse

    return pl.pallas_call(
        body, name=name, grid=(d, n_i),
        in_specs=[pl.BlockSpec((None, BQ, DA), lambda r, i: (r, i, 0)),
                  pl.BlockSpec((None, BQ, DA), lambda r, i: (r, i, 1)),
                  pl.BlockSpec((None, BQ, DA), lambda r, i: (r, i, 2))],
        out_specs=[pl.BlockSpec((None, BQ, DA), lambda r, i: (r, i, 0)),
                   pl.BlockSpec((None, BQ, LANES), lambda r, i: (r, i, 0))],
        out_shape=[SDS((d, n_sub, DA), BF16), SDS((d, n_sub, LANES), F32)],
        scratch_shapes=[pltpu.VMEM((kw, DA), BF16), pltpu.VMEM((kw, DA), BF16)],
        compiler_params=_params(("arbitrary", "arbitrary")),
    )(h16, h16, h16)


def _pooled(ext_ref, g, rows, tm):
    w = POOL_WINDOWS[g]
    cs = slice(g * GC, (g + 1) * GC)
    cur = ext_ref[HALO:HALO + tm, cs]
    win = cur
    for j in range(1, w):
        win = win + ext_ref[HALO - j:HALO - j + tm, cs]
    cnt = jnp.minimum(rows + 1, w).astype(F32)
    return win / cnt - cur, cnt


def _fill_ext(ext_ref, u_ref, uh_ref, blk, tm):
    @pl.when(blk == 0)
    def _():
        ext_ref[0:HALO, :] = jnp.zeros((HALO, DP), F32)

    @pl.when(blk > 0)
    def _():
        ext_ref[0:HALO, :] = uh_ref[...].astype(F32)

    ext_ref[HALO:HALO + tm, :] = u_ref[...].astype(F32)


def _residue_specs(tm, width):
    return [pl.BlockSpec((d, tm // d, width), lambda i: (0, i, 0)) for d in DILATIONS[1:]]


def _mix(o1, l1, o4, l4, o16, l16, hug, wp, scale):
    tm = 256
    n_slab = N_HEADS + 1

    def body(o1r, l1r, o4r, l4r, o16r, l16r, u_ref, uh_ref, ga_ref, gp_ref, wp_ref, sc_ref,
             y_ref, yt_ref, attn_ref, lse_ref, lse4_ref, lse16_ref, ext_ref, ys_ref, nat_ref, ls_ref):
        i = pl.program_id(0)
        for n, (d, o_r, l_r) in enumerate(((DILATIONS[1], o4r, l4r), (DILATIONS[2], o16r, l16r))):
            for r in range(d):
                rows = pl.ds(r, tm // d, stride=d)
                for h in range(N_HEADS):
                    nat_ref[n, h, rows, :] = o_r[r, :, h * DH:(h + 1) * DH].astype(F32)
                nat_ref[n, N_HEADS, rows, :] = l_r[r]
        la, lb, lc = l1r[...], nat_ref[0, N_HEADS], nat_ref[1, N_HEADS]
        mx = jnp.maximum(jnp.maximum(la, lb), lc)
        ea, eb, ec = jnp.exp(la - mx), jnp.exp(lb - mx), jnp.exp(lc - mx)
        z = ea + eb + ec
        wa, wb, wc = ea / z, eb / z, ec / z
        lse = mx + jnp.log(z)
        lse_ref[...] = lse
        ls_ref[...] = lse
        for d, out in ((DILATIONS[1], lse4_ref), (DILATIONS[2], lse16_ref)):
            for r in range(d):
                out[r] = ls_ref[pl.ds(r, tm // d, stride=d), :]
        for h in range(N_HEADS):
            hs = slice(h * DH, (h + 1) * DH)
            hc = slice(h, h + 1)
            attn = wa[:, hc] * o1r[:, hs].astype(F32) + wb[:, hc] * nat_ref[0, h] + wc[:, hc] * nat_ref[1, h]
            attn_ref[:, hs] = attn.astype(BF16)
            gt = ga_ref[:, hs].astype(F32)
            ys_ref[:, hs] = attn * (gt * jax.nn.sigmoid(gt))

        _fill_ext(ext_ref, u_ref, uh_ref, i, tm)
        rows = i * tm + lax.broadcasted_iota(jnp.int32, (tm, 1), 0)
        for g in range(N_GROUPS):
            cs = slice(g * GC, (g + 1) * GC)
            gs = slice(DA + g * GC, DA + (g + 1) * GC)
            pooled, _ = _pooled(ext_ref, g, rows, tm)
            po = _dot(pooled.astype(BF16), wp_ref[g]) * sc_ref[:, cs]
            gt = gp_ref[:, cs].astype(F32)
            ys_ref[:, gs] = po * (gt * jax.nn.sigmoid(gt))
        yv = ys_ref[...]
        y_ref[...] = yv.astype(BF16)
        yt_ref[...] = yv.T.astype(BF16)

    row = lambda i: (i, 0)
    blk = pl.BlockSpec((tm, DA), row)
    lanes = pl.BlockSpec((tm, LANES), row)
    o_res, l_res = _residue_specs(tm, DA), _residue_specs(tm, LANES)
    return pl.pallas_call(
        body, name="mix", grid=(S // tm,),
        in_specs=[blk, lanes, o_res[0], l_res[0], o_res[1], l_res[1],
                  pl.BlockSpec((tm, DP), row),
                  pl.BlockSpec((HALO, DP), lambda i: (jnp.maximum(i * (tm // HALO) - 1, 0), 0)),
                  pl.BlockSpec((tm, DA), lambda i: (i, 1)), pl.BlockSpec((tm, DP), lambda i: (i, 2)),
                  pl.BlockSpec((N_GROUPS, GC, GC), lambda i: (0, 0, 0)),
                  pl.BlockSpec((1, DP), lambda i: (0, 0))],
        out_specs=[pl.BlockSpec((tm, D), row), pl.BlockSpec((D, tm), lambda i: (0, i)), blk, lanes] + l_res,
        out_shape=[SDS((S, D), BF16), SDS((D, S), BF16), SDS((S, DA), BF16), SDS((S, LANES), F32)]
        + [SDS((d, S // d, LANES), F32) for d in DILATIONS[1:]],
        scratch_shapes=[pltpu.VMEM((HALO + tm, DP), F32), pltpu.VMEM((tm, D), F32),
                        pltpu.VMEM((2, n_slab, tm, DH), F32), pltpu.VMEM((tm, LANES), F32)],
        compiler_params=_params(("arbitrary",)),
    )(o1, l1, o4, l4, o16, l16, hug, hug, hug, hug, wp, scale)


def _outproj_ln(y, wout, x, target, gain, bias):
    tm = 512
    te = 128

    def body(y_ref, w_ref, x_ref, t_ref, g_ref, b_ref, dz_ref, dzb_ref, st_ref, out_ref):
        i = pl.program_id(0)

        @pl.when(i == 0)
        def _():
            st_ref[...] = jnp.zeros((8, D), F32)

        out_ref[...] = _dot(y_ref[...], w_ref[...])
        gn = g_ref[...]
        for e in range(tm // te):
            rs = slice(e * te, (e + 1) * te)
            z = ALPHA * x_ref[rs, :] + out_ref[rs, :]
            mu = jnp.mean(z, axis=1, keepdims=True)
            zc = z - mu
            var = jnp.mean(zc * zc, axis=1, keepdims=True)
            rstd = lax.rsqrt(var + LN_EPS)
            xhat = zc * rstd
            diff = xhat * gn + b_ref[...] - t_ref[rs, :]
            dyln = diff / D
            st_ref[0:1, :] += jnp.sum(dyln * xhat, axis=0, keepdims=True)
            st_ref[1:2, :] += jnp.sum(dyln, axis=0, keepdims=True)
            row_loss = jnp.sum(diff * diff, axis=1, keepdims=True) / D
            st_ref[2:3, :] += jnp.broadcast_to(0.5 * jnp.sum(row_loss, axis=0, keepdims=True), (1, D))
            dxh = dyln * gn
            m1 = jnp.mean(dxh, axis=1, keepdims=True)
            m2 = jnp.mean(dxh * xhat, axis=1, keepdims=True)
            dz = rstd * (dxh - m1 - xhat * m2)
            dz_ref[rs, :] = dz
            dzb_ref[rs, :] = dz.astype(BF16)

    row = lambda i: (i, 0)
    const = lambda i: (0, 0)
    return pl.pallas_call(
        body, name="outproj_ln", grid=(S // tm,),
        in_specs=[pl.BlockSpec((tm, D), row),
                  pl.BlockSpec((D, D), const, pipeline_mode=pl.Buffered(1)),
                  pl.BlockSpec((tm, D), row), pl.BlockSpec((tm, D), row),
                  pl.BlockSpec((1, D), const), pl.BlockSpec((1, D), const)],
        out_specs=[pl.BlockSpec((tm, D), row), pl.BlockSpec((tm, D), row), pl.BlockSpec((8, D), const)],
        out_shape=[SDS((S, D), F32), SDS((S, D), BF16), SDS((8, D), F32)],
        scratch_shapes=[pltpu.VMEM((tm, D), F32)],
        compiler_params=_params(("arbitrary",)),
    )(y, wout, x, target, gain, bias)


def _bwd_mid(dzb, wout, hug, attn, wp, scale, rider=None):
    tm = 256
    n = S // tm

    def body(dz_ref, w_ref, ga_ref, gp_ref, at_ref, u_ref, uh_ref, wp_ref, sc_ref,
             dh_ref, do_ref, do4_ref, do16_ref, dd_ref, dd4_ref, dd16_ref, dwp_ref, dwp16_ref, gps_ref,
             ext_ref, eext_ref, acc_ref, nat_ref, ds_ref):
        i = pl.program_id(0)
        ib = n - 1 - i

        @pl.when(i == 0)
        def _():
            eext_ref[tm:tm + HALO, :] = jnp.zeros((HALO, DP), F32)
            acc_ref[...] = jnp.zeros((N_GROUPS, GC, GC), F32)
            gps_ref[...] = jnp.zeros((8, DP), F32)

        dy = _dot(dz_ref[...], w_ref[...], NT)

        def through_gate(dy_part, gt):
            sg = jax.nn.sigmoid(gt)
            return dy_part * (gt * sg), dy_part * (sg * (1.0 + gt * (1.0 - sg)))

        dat, dyg_a = through_gate(dy[:, 0:DA], ga_ref[...].astype(F32))
        dmix_p, dyg_p = through_gate(dy[:, DA:D], gp_ref[...].astype(F32))

        at = at_ref[...].astype(F32)
        do_ref[...] = dat.astype(BF16)
        dh_ref[:, DP:DP + DA] = (dyg_a * at).astype(BF16)
        prod = dat * at
        dd = jnp.zeros((tm, LANES), F32)
        for h in range(N_HEADS):
            hs = slice(h * DH, (h + 1) * DH)
            dd = _to_lane(dd, jnp.sum(prod[:, hs], axis=1, keepdims=True), h)
            nat_ref[h] = dat[:, hs]
        dd_ref[...] = dd
        ds_ref[...] = dd
        for d, do_out, dd_out in ((DILATIONS[1], do4_ref, dd4_ref), (DILATIONS[2], do16_ref, dd16_ref)):
            for r in range(d):
                dd_out[r] = _residues(ds_ref, (), d, r, tm // d)
                for h in range(N_HEADS):
                    do_out[r, :, h * DH:(h + 1) * DH] = _residues(nat_ref, (h,), d, r, tm // d).astype(BF16)

        _fill_ext(ext_ref, u_ref, uh_ref, ib, tm)
        rows = ib * tm + lax.broadcasted_iota(jnp.int32, (tm, 1), 0)
        for g in range(N_GROUPS):
            w = POOL_WINDOWS[g]
            cs = slice(g * GC, (g + 1) * GC)
            pooled, cnt = _pooled(ext_ref, g, rows, tm)
            pre = _dot(pooled.astype(BF16), wp_ref[g])
            sc = sc_ref[:, cs]
            dpo = dmix_p[:, cs]
            gps_ref[0:1, cs] += jnp.sum(dpo * pre, axis=0, keepdims=True)
            dh_ref[:, DP + DA + g * GC:DP + DA + (g + 1) * GC] = (dyg_p[:, cs] * (pre * sc)).astype(BF16)
            dpre = (dpo * sc).astype(BF16)
            acc_ref[g] += _dot(pooled.T.astype(BF16), dpre)
            dpooled = _dot(dpre, wp_ref[g], NT)
            eext_ref[0:tm, cs] = dpooled / cnt
            du = eext_ref[0:tm, cs]
            for j in range(1, w):
                du = du + eext_ref[j:j + tm, cs]
            dh_ref[:, cs] = (du - dpooled).astype(BF16)
        eext_ref[tm:tm + HALO, :] = eext_ref[0:HALO, :]

        @pl.when(i == n - 1)
        def _():
            for j in range(N_DEV):
                for g in range(N_GROUPS):
                    blk = acc_ref[g, j * PB:(j + 1) * PB, :]
                    dwp_ref[j, g * PB:(g + 1) * PB, :] = blk
                    dwp16_ref[j, g * PB:(g + 1) * PB, :] = blk.astype(BF16)

    rev = lambda i: (n - 1 - i, 0)
    const = lambda i: (0, 0)
    res = lambda width: [pl.BlockSpec((d, tm // d, width), lambda i: (0, n - 1 - i, 0)) for d in DILATIONS[1:]]
    pool_blocks = pl.BlockSpec((N_DEV, N_GROUPS * PB, GC), lambda i: (0, 0, 0))
    outs, carried = _call_carrying(
        body, rider, (n,),
        [pl.BlockSpec((tm, D), rev),
         pl.BlockSpec((D, D), const, pipeline_mode=pl.Buffered(1)),
         pl.BlockSpec((tm, DA), lambda i: (n - 1 - i, 1)), pl.BlockSpec((tm, DP), lambda i: (n - 1 - i, 2)),
         pl.BlockSpec((tm, DA), rev), pl.BlockSpec((tm, DP), rev),
         pl.BlockSpec((HALO, DP), lambda i: (jnp.maximum((n - 1 - i) * (tm // HALO) - 1, 0), 0)),
         pl.BlockSpec((N_GROUPS, GC, GC), lambda i: (0, 0, 0)),
         pl.BlockSpec((1, DP), const)],
        [pl.BlockSpec((tm, DP + D), lambda i: (n - 1 - i, 1)), pl.BlockSpec((tm, DA), rev)] + res(DA)
        + [pl.BlockSpec((tm, LANES), rev)] + res(LANES) + [pool_blocks, pool_blocks, pl.BlockSpec((8, DP), const)],
        [SDS((S, D_IN), BF16), SDS((S, DA), BF16)] + [SDS((d, S // d, DA), BF16) for d in DILATIONS[1:]]
        + [SDS((S, LANES), F32)] + [SDS((d, S // d, LANES), F32) for d in DILATIONS[1:]]
        + [SDS((N_DEV, N_GROUPS * PB, GC), F32), SDS((N_DEV, N_GROUPS * PB, GC), BF16), SDS((8, DP), F32)],
        [pltpu.VMEM((HALO + tm, DP), F32), pltpu.VMEM((tm + HALO, DP), F32),
         pltpu.VMEM((N_GROUPS, GC, GC), F32), pltpu.VMEM((N_HEADS, tm, DH), F32), pltpu.VMEM((tm, LANES), F32)],
        [dzb, wout, hug, hug, attn, hug, hug, wp, scale], "bwd_mid")
    return outs, carried


def _attn_bwd(h16, do, lse, dd, rope, name, others=None, dh=None, rider=None):
    d, n_sub, _ = h16.shape
    n_i = n_sub // BQ
    n_kb = n_sub // KB
    qw = BQ + KB
    final = others is not None
    out_dtype = BF16
    n_cb = 3 * DA // DH

    def nxt(i):
        return jnp.minimum(2 * i + 2, n_kb - 1)

    def body(qc_ref, qn_ref, kc_ref, vc_ref, doc_ref, don_ref, lc_ref, ln_ref, dc_ref, dn_ref,
             c_ref, sn_ref, *rest):
        if final:
            acc4_ref, acc16_ref, _, out_ref, carry_ref, qw_ref, dow_ref, lw_ref, dw_ref, nat_ref = rest
        else:
            out_ref, carry_ref, qw_ref, dow_ref, lw_ref, dw_ref = rest
        i = pl.program_id(1)

        @pl.when(i == 0)
        def _():
            carry_ref[...] = jnp.zeros((KB, DA), F32)

        for win, own, after in ((qw_ref, qc_ref, qn_ref), (dow_ref, doc_ref, don_ref), (lw_ref, lc_ref, ln_ref),
                                (dw_ref, dc_ref, dn_ref)):
            win[0:BQ, :] = own[...]
            win[BQ:qw, :] = after[...]
        if final:
            for n, (dil, acc) in enumerate(((DILATIONS[1], acc4_ref), (DILATIONS[2], acc16_ref))):
                for r in range(dil):
                    for cb in range(n_cb):
                        nat_ref[n, cb, pl.ds(r, BQ // dil, stride=dil), :] = acc[r, :, cb * DH:(cb + 1) * DH].astype(F32)

        a = lax.broadcasted_iota(jnp.int32, (qw, BQ), 0)
        b = lax.broadcasted_iota(jnp.int32, (qw, BQ), 1)
        n_q = jnp.where(i == n_i - 1, BQ, qw)
        mask = (b <= a) & (a <= b + KB) & (a < n_q)
        tok = pl.ds(pl.program_id(0) + d * BQ * i, BQ, stride=d)
        tabs = (c_ref[tok, :], sn_ref[tok, :])
        for h in range(N_HEADS):
            hs = slice(h * DH, (h + 1) * DH)
            hc = slice(h, h + 1)
            q, k, vv, dob = qw_ref[:, hs], kc_ref[:, hs], vc_ref[:, hs], dow_ref[:, hs]
            s = _dot(q, k, NT) * SCALE
            p = jnp.exp(jnp.where(mask, s - lw_ref[:, hc], NEG))
            dp = _dot(dob, vv, NT)
            ds = (p * (dp - dw_ref[:, hc]) * SCALE).astype(BF16)
            dq = _dot(ds, k)
            dk = _dot(ds, q, T_N)
            dv = _dot(p.astype(BF16), dob, T_N)
            dq_lo = dq[0:KB] + carry_ref[:, hs]
            carry_ref[:, hs] = dq[BQ:qw]
            dq_own = _rope_t(jnp.concatenate([dq_lo, dq[KB:BQ]], axis=0), *tabs)
            for base, gv in ((0, dq_own), (DA, _rope_t(dk, *tabs)), (2 * DA, dv)):
                if final:
                    cb = base // DH + h
                    gv = gv + nat_ref[0, cb] + nat_ref[1, cb]
                out_ref[:, base + h * DH:base + (h + 1) * DH] = gv.astype(out_dtype)

    def cur(rows, width, col=0):
        return pl.BlockSpec((None, rows, width), lambda r, i: (r, i, col))

    def nx(width, col=0):
        return pl.BlockSpec((None, KB, width), lambda r, i: (r, nxt(i), col))

    in_specs = [cur(BQ, DA), nx(DA), cur(BQ, DA, 1), cur(BQ, DA, 2), cur(BQ, DA), nx(DA),
                cur(BQ, LANES), nx(LANES), cur(BQ, LANES), nx(LANES)]
    in_specs += [pl.BlockSpec((S, DH), lambda r, i: (0, 0), pipeline_mode=pl.Buffered(1))] * 2
    args = [h16, h16, h16, h16, do, do, lse, lse, dd, dd] + list(rope)
    scratch = [pltpu.VMEM((KB, DA), F32), pltpu.VMEM((qw, DA), BF16), pltpu.VMEM((qw, DA), BF16),
               pltpu.VMEM((qw, LANES), F32), pltpu.VMEM((qw, LANES), F32)]
    if final:
        assert d == 1
        in_specs += [pl.BlockSpec((dil, BQ // dil, 3 * DA), lambda r, i: (0, i, 0)) for dil in DILATIONS[1:]]
        in_specs.append(pl.BlockSpec(memory_space=pl.ANY))
        args += list(others) + [dh[None]]
        scratch.append(pltpu.VMEM((2, n_cb, BQ, DH), F32))
    outs, carried = _call_carrying(
        body, rider, (d, n_i), in_specs, [cur(BQ, 3 * DA)], [SDS((d, n_sub, D_IN if final else 3 * DA), out_dtype)],
        scratch, args, name, {len(args) - 1: 0} if final else {})
    return outs[0] if rider is None else (outs[0], carried)


def _grad_w_out(yt, dzb):
    tn = 512

    def body(yt_ref, dz_ref, o_ref, o16_ref):
        acc = _dot(yt_ref[...], dz_ref[...])
        o_ref[...] = acc
        o16_ref[...] = acc.astype(BF16)

    cols = pl.BlockSpec((D, tn), lambda c: (0, c))
    o, o16 = pl.pallas_call(
        body, name="grad_w_out", grid=(D // tn,),
        in_specs=[pl.BlockSpec((D, S), lambda c: (0, 0), pipeline_mode=pl.Buffered(1)),
                  pl.BlockSpec((S, tn), lambda c: (0, c))],
        out_specs=[cols, cols],
        out_shape=[SDS((D, D), F32), SDS((D, D), BF16)],
        compiler_params=_params(("arbitrary",)),
    )(yt, dzb)
    return o.reshape(N_DEV, OB, D), o16.reshape(N_DEV, OB, D)


def _grad_w_in(xt, dh, order):
    n_send = N_DEV // 2
    n_steps = D_IN // TN
    own0 = n_send * W_PHASES
    sums0 = own0 + W_PHASES
    step_of_send = [W_PHASES * k + W_PHASES + 1 for k in range(n_send)]

    def body(ord_ref, xt_ref, dh_ref, o_ref, o16_ref, r1_ref, s2_ref, st_ref, got_ref,
             piece_sems, send_sems, recv_sems, got_sem):
        c = pl.program_id(0)
        x, y, core, _ = _place()

        def piece(cc):
            blk, sub = ord_ref[cc // W_PHASES], cc % W_PHASES
            return pltpu.make_async_copy(st_ref.at[cc % 2], o16_ref.at[blk, :, pl.ds(sub * TN, TN)], piece_sems.at[cc % 2])

        def send(k):
            return pltpu.make_async_remote_copy(
                src_ref=o16_ref.at[ord_ref[k]], dst_ref=r1_ref.at[k], send_sem=send_sems.at[k], recv_sem=recv_sems.at[k],
                device_id=(x, y, 1 - core), device_id_type=MESH)

        def sibling_part(cc):
            k, sub = cc // W_PHASES - n_send, cc % W_PHASES
            return pltpu.make_async_copy(r1_ref.at[k, :, pl.ds(sub * TN, TN)], got_ref, got_sem)

        @pl.when((c >= 2) & (c < own0 + 2))
        def _():
            piece(c - 2).wait()

        for k in range(n_send):
            pl.when(c == step_of_send[k])(lambda k=k: send(k).start())
        for k in range(1, n_send):
            pl.when(c == own0 + W_PHASES * k)(lambda k=k: send(k).wait_recv())
        pl.when(c >= sums0)(lambda: sibling_part(c).start())

        acc = _dot(xt_ref[...], dh_ref[...])

        @pl.when((c >= own0) & (c < sums0))
        def _():
            o_ref[...] = acc

        @pl.when(c < own0)
        def _():
            st_ref[c % 2] = acc.astype(BF16)
            piece(c).start()

        @pl.when(c >= sums0)
        def _():
            sibling_part(c).wait()
            s2_ref[...] = (acc + got_ref[...].astype(F32)).astype(BF16)

        @pl.when(c == n_steps - 1)
        def _():
            send(0).wait_recv()
            for k in range(n_send):
                send(k).wait_send()

    col = lambda c, ordr: W_PHASES * ordr[c // W_PHASES] + c % W_PHASES
    hbm = pl.BlockSpec(memory_space=pl.ANY)
    sums_blk = lambda c, ordr: (jnp.maximum(c // W_PHASES - (n_send + 1), 0), 0, jnp.where(c >= sums0, c % W_PHASES, 0))
    partial, _, got, sums = pl.pallas_call(
        body, name="grad_w_in",
        grid_spec=pltpu.PrefetchScalarGridSpec(
            num_scalar_prefetch=1, grid=(n_steps,),
            in_specs=[pl.BlockSpec((D, S), lambda c, ordr: (0, 0), pipeline_mode=pl.Buffered(1)),
                      pl.BlockSpec((S, TN), lambda c, ordr: (0, col(c, ordr)))],
            out_specs=[pl.BlockSpec((None, D, TN), lambda c, ordr: (0, 0, jnp.clip(c - own0, 0, W_PHASES - 1))), hbm, hbm,
                       pl.BlockSpec((None, D, TN), sums_blk)],
            scratch_shapes=[pltpu.VMEM((2, D, TN), BF16), pltpu.VMEM((D, TN), BF16), pltpu.SemaphoreType.DMA((2,)),
                            pltpu.SemaphoreType.DMA((n_send,)), pltpu.SemaphoreType.DMA((n_send,)),
                            pltpu.SemaphoreType.DMA(())]),
        out_shape=[SDS((1, D, WB), F32), SDS((N_DEV, D, WB), BF16), SDS((n_send, D, WB), BF16),
                   SDS((n_send - 1, D, WB), BF16)],
        compiler_params=_params(("arbitrary",)),
    )(order, xt, dh)
    return partial, got, sums


def _grad_x(dz, dh, wgs, rider=None):
    tm = 1024

    def body(dz_ref, dh_ref, *rest):
        o_ref = rest[-1]

        @pl.when(pl.program_id(1) == 0)
        def _():
            o_ref[...] = ALPHA * dz_ref[...]

        acc = _dot(dh_ref[:, 0:TN], rest[0][...], NT)
        for s in range(1, W_PHASES):
            acc = acc + _dot(dh_ref[:, s * TN:(s + 1) * TN], rest[s][...], NT)
        o_ref[...] += acc

    outs, carried = _call_carrying(
        body, rider, (S // tm, N_DEV),
        [pl.BlockSpec((tm, D), lambda i, j: (i, 0)), pl.BlockSpec((tm, WB), lambda i, j: (i, j))]
        + [pl.BlockSpec((None, D, TN), lambda i, j: (j, 0, 0))] * W_PHASES,
        [pl.BlockSpec((tm, D), lambda i, j: (i, 0))], [SDS((S, D), F32)], [], [dz, dh] + list(wgs), "grad_x")
    return outs[0], carried


def _place():
    x, y, c = lax.axis_index("x"), lax.axis_index("y"), lax.axis_index("c")
    chips = [(x, y), (1 - x, y), (x, 1 - y), (1 - x, 1 - y)]
    return x, y, c, chips


def _blk(x, y, c):
    return 4 * x + 2 * y + c


def _adamw(w, g, m, v):
    m = ADAM_B1 * m + (1.0 - ADAM_B1) * g
    v = ADAM_B2 * v + (1.0 - ADAM_B2) * (g * g)
    m_hat = m / (1.0 - ADAM_B1 ** ADAM_STEP)
    v_hat = v / (1.0 - ADAM_B2 ** ADAM_STEP)
    delta = -ADAM_LR * (m_hat / (jnp.sqrt(v_hat) + ADAM_EPS) + ADAM_WD * w)
    return delta, m, v


class _Rider:
    def __init__(self, args, in_specs, out_shape, out_specs, scratch, start, finish):
        self.args, self.in_specs, self.out_shape, self.out_specs = args, in_specs, out_shape, out_specs
        self.scratch, self.start, self.finish = scratch, start, finish


def _carry(body, rider, n_in, n_out, first, last):
    if rider is None:
        return body
    r_in, r_out, r_scr = len(rider.args), len(rider.out_shape), len(rider.scratch)

    def carrying(*refs):
        o0 = n_in + r_in
        s0 = o0 + n_out + r_out
        s1 = len(refs) - r_scr
        theirs = (refs[n_in:o0], refs[o0 + n_out:s0], refs[s1:])
        pl.when(first())(lambda: rider.start(*theirs))
        body(*refs[:n_in], *refs[o0:o0 + n_out], *refs[s0:s1])
        pl.when(last())(lambda: rider.finish(*theirs))

    return carrying


def _call_carrying(body, rider, grid, in_specs, out_specs, out_shape, scratch, args, name, aliases=None):
    n_in, n_out = len(in_specs), len(out_specs)
    ids = lambda: [pl.program_id(a) for a in range(len(grid))]
    first = lambda: functools.reduce(jnp.logical_and, [i == 0 for i in ids()])
    last = lambda: functools.reduce(jnp.logical_and, [i == n - 1 for i, n in zip(ids(), grid)])
    if rider is not None:
        in_specs, args = in_specs + rider.in_specs, list(args) + rider.args
        out_specs, out_shape = out_specs + rider.out_specs, out_shape + rider.out_shape
        scratch = scratch + rider.scratch
    outs = pl.pallas_call(
        _carry(body, rider, n_in, n_out, first, last), name=name, grid=grid,
        in_specs=in_specs, out_specs=out_specs, out_shape=out_shape, scratch_shapes=scratch,
        input_output_aliases=aliases or {}, compiler_params=_params(("arbitrary",) * len(grid)),
    )(*args)
    return list(outs[:n_out]), list(outs[n_out:])


def _gather_copy(tensors, send_sems, recv_sems, t, k, block, to, src=None):
    dst = tensors[t][1](_blk(*block))
    return pltpu.make_async_remote_copy(
        src_ref=dst if src is None else src, dst_ref=dst,
        send_sem=send_sems.at[t, k], recv_sem=recv_sems.at[t, k], device_id=to, device_id_type=MESH)


def _gather_start(tensors, send_sems, recv_sems, local_sems):
    x, y, c, chips = _place()
    me, sib = (x, y, c), (x, y, 1 - c)
    for t, (src, dst) in enumerate(tensors):
        pltpu.make_async_copy(src, dst(_blk(*me)), local_sems.at[t]).start()
        _gather_copy(tensors, send_sems, recv_sems, t, 0, me, sib, src).start()
        for j in (1, 2, 3):
            _gather_copy(tensors, send_sems, recv_sems, t, j, me, (*chips[j], c), src).start()


def _gather_finish(tensors, send_sems, recv_sems, local_sems):
    x, y, c, chips = _place()
    me, sib = (x, y, c), (x, y, 1 - c)
    copy = functools.partial(_gather_copy, tensors, send_sems, recv_sems)
    for t in range(len(tensors)):
        for j in (1, 2, 3):
            copy(t, j, (*chips[j], c), me).wait_recv()
            copy(t, 3 + j, (*chips[j], c), sib).start()
    for t, (src, dst) in enumerate(tensors):
        copy(t, 0, sib, me).wait_recv()
        for j in (1, 2, 3):
            copy(t, 3 + j, (*chips[j], 1 - c), me).wait_recv()
        copy(t, 0, me, sib, src).wait_send()
        for j in (1, 2, 3):
            copy(t, j, me, (*chips[j], c), src).wait_send()
            copy(t, 3 + j, (*chips[j], c), sib).wait_send()
        pltpu.make_async_copy(src, dst(_blk(*me)), local_sems.at[t]).wait()


W_PHASES = WB // TN


def _gather_first_rider(w_in_s, w_out_s, w_pool_s):
    shapes = [w_in_s.shape, w_out_s.shape, w_pool_s.shape]

    def keeps(outs, scr):
        return [pltpu.make_async_copy(scr[n], outs[n], scr[6].at[n]) for n in range(3)]

    def tensors(outs, scr):
        return [(scr[0].at[:, pl.ds(0, TN)], lambda b: outs[3].at[b])]

    def start(ins, outs, scr):
        for n in range(3):
            scr[n][...] = ins[n][...].astype(BF16)
        for keep in keeps(outs, scr):
            keep.start()
        _gather_start(tensors(outs, scr), *scr[3:6])

    def finish(ins, outs, scr):
        _gather_finish(tensors(outs, scr), *scr[3:6])
        for keep in keeps(outs, scr):
            keep.wait()

    vmem = pl.BlockSpec(memory_space=pltpu.VMEM)
    hbm = pl.BlockSpec(memory_space=pl.ANY)
    return _Rider(
        args=[w_in_s, w_out_s, w_pool_s], in_specs=[vmem] * 3,
        out_shape=[SDS(sh, BF16) for sh in shapes] + [SDS((N_DEV, D, TN), BF16)], out_specs=[hbm] * 4,
        scratch=[pltpu.VMEM(sh, BF16) for sh in shapes]
        + [pltpu.SemaphoreType.DMA((1, 7)), pltpu.SemaphoreType.DMA((1, 7)), pltpu.SemaphoreType.DMA((1,)),
           pltpu.SemaphoreType.DMA((3,))],
        start=start, finish=finish)


def _gather_w_in_rider(shard16, s):
    def tensors(ins, outs):
        return [(ins[0].at[:, pl.ds(s * TN, TN)], lambda b: outs[0].at[b])]

    hbm = pl.BlockSpec(memory_space=pl.ANY)
    return _Rider(
        args=[shard16], in_specs=[hbm], out_shape=[SDS((N_DEV, D, TN), BF16)], out_specs=[hbm],
        scratch=[pltpu.SemaphoreType.DMA((1, 7)), pltpu.SemaphoreType.DMA((1, 7)), pltpu.SemaphoreType.DMA((1,))],
        start=lambda ins, outs, scr: _gather_start(tensors(ins, outs), *scr),
        finish=lambda ins, outs, scr: _gather_finish(tensors(ins, outs), *scr))


def _gather_small_rider(w_out16, w_pool16):
    def tensors(ins, outs):
        gout_ref, gpool_ref = outs

        def pool_rows(b):
            return gpool_ref.at[:, pl.ds(pl.multiple_of(b * PB, PB), PB), :]

        return [(ins[0], lambda b: gout_ref.at[b]), (ins[1], pool_rows)]

    hbm = pl.BlockSpec(memory_space=pl.ANY)
    return _Rider(
        args=[w_out16, w_pool16], in_specs=[hbm, hbm],
        out_shape=[SDS((N_DEV, OB, D), BF16), SDS((N_GROUPS, GC, GC), BF16)], out_specs=[hbm, hbm],
        scratch=[pltpu.SemaphoreType.DMA((2, 7)), pltpu.SemaphoreType.DMA((2, 7)), pltpu.SemaphoreType.DMA((2,))],
        start=lambda ins, outs, scr: _gather_start(tensors(ins, outs), *scr),
        finish=lambda ins, outs, scr: _gather_finish(tensors(ins, outs), *scr))


def _block_table():
    x, y, c, chips = _place()
    return jnp.stack([_blk(*chip, c) for chip in chips]).astype(jnp.int32)


def _sibling_first_order():
    x, y, c, chips = _place()
    return jnp.stack([_blk(*chip, side) for side in (1 - c, c) for chip in chips]).astype(jnp.int32)


def _exchange_rider(p16):
    _, r_tot, cols = p16.shape

    def copies(ins, outs, scr):
        x, y, c, chips = _place()
        return [pltpu.make_async_remote_copy(
            src_ref=ins[0].at[_blk(*chips[k], 1 - c)], dst_ref=outs[0].at[k],
            send_sem=scr[0].at[k], recv_sem=scr[1].at[k], device_id=(x, y, 1 - c), device_id_type=MESH)
            for k in range(4)]

    def start(ins, outs, scr):
        for cp in copies(ins, outs, scr):
            cp.start()

    def finish(ins, outs, scr):
        for cp in copies(ins, outs, scr):
            cp.wait()

    hbm = pl.BlockSpec(memory_space=pl.ANY)
    return _Rider([p16], [hbm], [SDS((4, r_tot, cols), BF16)], [hbm],
                  [pltpu.SemaphoreType.DMA((4,)), pltpu.SemaphoreType.DMA((4,))], start, finish)


def _chip_sums(name, table, p32, r1, rc):
    _, r_tot, cols = p32.shape

    def body(tbl_ref, p_ref, r_ref, o_ref):
        o_ref[...] = (p_ref[...] + r_ref[...].astype(F32)).astype(BF16)

    return pl.pallas_call(
        body, name=name,
        grid_spec=pltpu.PrefetchScalarGridSpec(
            num_scalar_prefetch=1, grid=(3, r_tot // rc),
            in_specs=[pl.BlockSpec((None, rc, cols), lambda k, ch, tbl: (tbl[k + 1], ch, 0)),
                      pl.BlockSpec((None, rc, cols), lambda k, ch, tbl: (k + 1, ch, 0))],
            out_specs=pl.BlockSpec((None, rc, cols), lambda k, ch, tbl: (k, ch, 0))),
        out_shape=SDS((3, r_tot, cols), BF16),
        compiler_params=_params(("arbitrary", "arbitrary")),
    )(table, p32, r1)


def _stage2_rider(sums, stats=None):
    n_t = len(sums)

    def copies(ins, outs, scr):
        x, y, c, chips = _place()
        out = []
        for t in range(n_t):
            for k in (1, 2, 3):
                out.append(pltpu.make_async_remote_copy(
                    src_ref=ins[t].at[k - 1], dst_ref=outs[t].at[k - 1],
                    send_sem=scr[0].at[t, k - 1], recv_sem=scr[1].at[t, k - 1],
                    device_id=(*chips[k], c), device_id_type=MESH))
        if stats is not None:
            for k in range(1, N_DEV):
                peer = (x ^ ((k >> 2) & 1), y ^ ((k >> 1) & 1), c ^ (k & 1))
                out.append(pltpu.make_async_remote_copy(
                    src_ref=scr[4], dst_ref=outs[n_t].at[_blk(x, y, c)],
                    send_sem=scr[2].at[k - 1], recv_sem=scr[3].at[k - 1], device_id=peer, device_id_type=MESH))
        return out

    def own_rows(outs, scr):
        x, y, c, _ = _place()
        return pltpu.make_async_copy(scr[4], outs[n_t].at[_blk(x, y, c)], scr[5])

    def start(ins, outs, scr):
        if stats is not None:
            scr[4][...] = ins[n_t][...]
            scr[4][3:4, 0:DP] = ins[n_t + 1][0:1, :]
            own_rows(outs, scr).start()
        for cp in copies(ins, outs, scr):
            cp.start()

    def finish(ins, outs, scr):
        for cp in copies(ins, outs, scr):
            cp.wait()
        if stats is not None:
            own_rows(outs, scr).wait()

    vmem = pl.BlockSpec(memory_space=pltpu.VMEM)
    hbm = pl.BlockSpec(memory_space=pl.ANY)
    scratch = [pltpu.SemaphoreType.DMA((n_t, 3)), pltpu.SemaphoreType.DMA((n_t, 3))]
    args, in_specs = list(sums), [hbm] * n_t
    out_shape, out_specs = [SDS(s.shape, BF16) for s in sums], [hbm] * n_t
    if stats is not None:
        scratch += [pltpu.SemaphoreType.DMA((N_DEV - 1,)), pltpu.SemaphoreType.DMA((N_DEV - 1,)),
                    pltpu.VMEM((8, D), F32), pltpu.SemaphoreType.DMA(())]
        args, in_specs = args + list(stats), in_specs + [vmem, vmem]
        out_shape, out_specs = out_shape + [SDS((N_DEV, 8, D), F32)], out_specs + [hbm]
    return _Rider(args, in_specs, out_shape, out_specs, scratch, start, finish)


def _adamw_shard(name, table, p32, r1, r2, w, m, v, rc):
    _, r_tot, cols = p32.shape

    def body(tbl_ref, p_ref, r1_ref, r2_ref, w_ref, m_ref, v_ref, g_ref, d_ref, nm_ref, nv_ref):
        g = p_ref[...] + r1_ref[...].astype(F32)
        for k in range(3):
            g = g + r2_ref[k].astype(F32)
        delta, nm, nv = _adamw(w_ref[...], g, m_ref[...], v_ref[...])
        g_ref[...] = g
        d_ref[...] = delta
        nm_ref[...] = nm
        nv_ref[...] = nv

    rows = pl.BlockSpec((rc, cols), lambda ch, tbl: (ch, 0))
    shard = SDS((r_tot, cols), F32)
    return pl.pallas_call(
        body, name=name,
        grid_spec=pltpu.PrefetchScalarGridSpec(
            num_scalar_prefetch=1, grid=(r_tot // rc,),
            in_specs=[pl.BlockSpec((None, rc, cols), lambda ch, tbl: (tbl[0], ch, 0)),
                      pl.BlockSpec((None, rc, cols), lambda ch, tbl: (0, ch, 0)),
                      pl.BlockSpec((3, rc, cols), lambda ch, tbl: (0, ch, 0)), rows, rows, rows],
            out_specs=[rows, rows, rows, rows]),
        out_shape=[shard, shard, shard, shard],
        compiler_params=_params(("arbitrary",)),
    )(table, p32, r1, r2, w, m, v)


def _replicated_adamw(gathered, gain, bias, scale, m_gain, m_bias, m_scale, v_gain, v_bias, v_scale):
    def body(all_ref, s_ref, g_ref, b_ref, ms_ref, mg_ref, mb_ref, vs_ref, vg_ref, vb_ref, loss_ref, *outs):
        tot = all_ref[0]
        for b in range(1, N_DEV):
            tot = tot + all_ref[b]
        loss_ref[...] = tot[2:3, 0:LANES]
        for n, (row, width, w_r, m_r, v_r) in enumerate(((3, DP, s_ref, ms_ref, vs_ref), (0, D, g_ref, mg_ref, vg_ref),
                                                         (1, D, b_ref, mb_ref, vb_ref))):
            g = tot[row:row + 1, 0:width]
            delta, nm, nv = _adamw(w_r[...], g, m_r[...], v_r[...])
            for out, val in zip(outs[4 * n:4 * n + 4], (g, delta, nm, nv)):
                out[...] = val

    vmem = pl.BlockSpec(memory_space=pltpu.VMEM)
    shapes = [SDS((1, LANES), F32)] + [SDS((1, width), F32) for width in (DP, D, D) for _ in range(4)]
    return pl.pallas_call(
        body, name="replicated_adamw",
        in_specs=[vmem] * 10, out_specs=[vmem] * len(shapes), out_shape=shapes,
        compiler_params=_params(),
    )(gathered, scale, gain, bias, m_scale, m_gain, m_bias, v_scale, v_gain, v_bias)


def kernel(x, w_in, w_pool, pool_scale, w_out, ln_gain, ln_bias, loss_target, m_w_in, m_w_pool, m_pool_scale, m_w_out, m_ln_gain, m_ln_bias, v_w_in, v_w_pool, v_pool_scale, v_w_out, v_ln_gain, v_ln_bias):
    pool_rows = (N_GROUPS * PB, GC)
    x2, target = x[0], loss_target[0]
    table = _block_table()

    (xb, xt, *rope), (shard16, w_out16, w_pool16, wg0) = _prep_x(
        x2, _rope_rates(), _gather_first_rider(w_in[0], w_out[0], w_pool[0]))
    h, (wg1, wg2, wg_out, wg_pool) = _proj_all(xb, rope, wg0, shard16, w_out16, w_pool16)
    wg_out = wg_out.reshape(D, D)
    h16, hug = [h[0][None], h[1], h[2]], h[3]
    fwd = [_attn_fwd(h16[n], f"attn_fwd_d{d}") for n, d in enumerate(DILATIONS)]
    y, yt, attn, lse1, lse4, lse16 = _mix(fwd[0][0][0], fwd[0][1][0], *fwd[1], *fwd[2], hug, wg_pool, pool_scale)
    dz, dzb, stats = _outproj_ln(y, wg_out, x2, target, ln_gain, ln_bias)

    dwout, dwout16 = _grad_w_out(yt, dzb)
    mid, (r1_out,) = _bwd_mid(dzb, wg_out, hug, attn, wg_pool, pool_scale, rider=_exchange_rider(dwout16))
    dh, do1, do4, do16, dd1, dd4, dd16, dwp, dwp16, gps = mid
    s2_out = _chip_sums("rs_out_sums", table, dwout, r1_out, OB)

    do, lse, dd = [do1[None], do4, do16], [lse1[None], lse4, lse16], [dd1[None], dd4, dd16]
    acc4, (r2_out,) = _attn_bwd(h16[1], do[1], lse[1], dd[1], rope, "attn_bwd_d4", rider=_stage2_rider([s2_out]))
    g_out, d_out, nm_out, nv_out = _adamw_shard(
        "adamw_w_out", table, dwout, r1_out, r2_out, w_out[0], m_w_out[0], v_w_out[0], OB // 2)
    acc16, (r1_pool,) = _attn_bwd(h16[2], do[2], lse[2], dd[2], rope, "attn_bwd_d16", rider=_exchange_rider(dwp16))
    s2_pool = _chip_sums("rs_pool_sums", table, dwp, r1_pool, N_GROUPS * PB)
    dh = _attn_bwd(h16[0], do[0], lse[0], dd[0], rope, "attn_bwd_d1", others=(acc4, acc16), dh=dh)[0]

    dwin, r1_in, s2_in = _grad_w_in(xt, dh, _sibling_first_order())
    grad_x, (r2_in, r2_pool, gathered) = _grad_x(
        dz, dh, (wg0, wg1, wg2), rider=_stage2_rider([s2_in, s2_pool], stats=(stats, gps)))
    g_in, d_in, nm_in, nv_in = _adamw_shard(
        "adamw_w_in", jnp.zeros_like(table), dwin, r1_in, r2_in, w_in[0], m_w_in[0], v_w_in[0], 512)
    g_pool, d_pool, nm_pool, nv_pool = _adamw_shard(
        "adamw_w_pool", table, dwp, r1_pool, r2_pool, w_pool[0].reshape(pool_rows), m_w_pool[0].reshape(pool_rows),
        v_w_pool[0].reshape(pool_rows), N_GROUPS * PB)
    loss, *small = _replicated_adamw(gathered, ln_gain, ln_bias, pool_scale, m_ln_gain, m_ln_bias, m_pool_scale,
                                     v_ln_gain, v_ln_bias, v_pool_scale)
    (g_ps, d_ps, nm_ps, nv_ps, g_gain, d_gain, nm_gain, nv_gain, g_bias, d_bias, nm_bias, nv_bias) = small

    shard4 = lambda t: t.reshape(1, N_GROUPS, PB, GC)
    lead = lambda t: t[None]
    return (loss[0, 0], lead(grad_x),
            lead(g_in), shard4(g_pool), g_ps, lead(g_out), g_gain, g_bias,
            lead(d_in), shard4(d_pool), d_ps, lead(d_out), d_gain, d_bias,
            lead(nm_in), shard4(nm_pool), nm_ps, lead(nm_out), nm_gain, nm_bias,
            lead(nv_in), shard4(nv_pool), nv_ps, lead(nv_out), nv_gain, nv_bias)
```
